```python
import math
import jax, jax.numpy as jnp
from jax import lax
import numpy as np


D_MODEL = 1024
BATCH = 8
SEQ = 8192
DEPTH = 1

SSM_WIDTH = D_MODEL // 2
SSM_GROUP = 16
SSM_GROUPS = SSM_WIDTH // SSM_GROUP
SSM_STATE = 64
SB_HEAD_DIM = 64
SB_WIDTH = D_MODEL // 2
SB_HEADS = SB_WIDTH // SB_HEAD_DIM
MIX_WIDTH = SSM_WIDTH + SB_WIDTH
IN_WIDTH = SSM_WIDTH + 3 * SB_WIDTH
Q_BLOCK = 128
MEM_LEN = 256
XA_HEADS = 4
XA_HEAD_DIM = 128
XA_WIDTH = XA_HEADS * XA_HEAD_DIM
D_FF = 4 * D_MODEL
NORM_EPS = 1e-6
DT_MIN = 1e-3
DT_MAX = 1e-1

kernel_name = "hymba_s5_stickbreaking_block"


def rmsnorm(x, g):
    xf = x.astype(jnp.float32)
    xf = xf * lax.rsqrt(jnp.mean(xf * xf, axis=-1, keepdims=True) + NORM_EPS)
    return (xf * g.astype(jnp.float32)).astype(x.dtype)


def _linear_recurrence_combine(left, right):
    a_l, b_l = left
    a_r, b_r = right
    return a_r * a_l, a_r * b_l + b_r


def s5_mixer(u, a_re, a_im, log_dt, b_re, b_im, c_re, c_im, d_skip, w_glu):
    bsz, seq, _ = u.shape
    f32 = jnp.float32
    uf = u.astype(f32).reshape(bsz, seq, SSM_GROUPS, SSM_GROUP)
    lam = lax.complex(a_re.astype(f32), a_im.astype(f32))
    dt = jnp.exp(log_dt.astype(f32))[:, None]
    a_bar = jnp.exp(lam * dt)
    b = lax.complex(b_re.astype(f32), b_im.astype(f32))
    b_bar = ((a_bar - 1.0) / lam)[..., None] * b
    c = lax.complex(c_re.astype(f32), c_im.astype(f32))
    bu = jnp.einsum('bsgc,gpc->bsgp', uf.astype(jnp.complex64), b_bar)
    a_seq = jnp.broadcast_to(a_bar[None, None], (1, seq, SSM_GROUPS, SSM_STATE))
    _, states = lax.associative_scan(_linear_recurrence_combine, (a_seq, bu), axis=1)
    y = jnp.einsum('bsgp,gcp->bsgc', states, c).real + d_skip.astype(f32).reshape(SSM_GROUPS, SSM_GROUP) * uf
    y = jax.nn.gelu(y.reshape(bsz, seq, SSM_WIDTH))
    y = y * jax.nn.sigmoid(y @ w_glu.astype(f32))
    return y.astype(u.dtype)


def stick_breaking_attention(q, k, v):
    bsz, seq, h, dh = q.shape
    nblk = seq // Q_BLOCK
    scale = dh ** -0.5
    qb = q.reshape(bsz, nblk, Q_BLOCK, h, dh).transpose(1, 0, 2, 3, 4)
    key_pos = jnp.arange(seq)

    def block(args):
        qi, i = args
        logits = jnp.einsum('bqhd,bkhd->bhqk', qi, k, preferred_element_type=jnp.float32) * scale
        q_pos = i * Q_BLOCK + jnp.arange(Q_BLOCK)
        mask = key_pos[None, :] < q_pos[:, None]
        log_beta = jax.nn.log_sigmoid(logits)
        log_1m_beta = jnp.where(mask, jax.nn.log_sigmoid(-logits), 0.0)
        after = lax.cumsum(log_1m_beta, axis=3, reverse=True) - log_1m_beta
        w = jnp.where(mask, jnp.exp(log_beta + after), 0.0)
        return jnp.einsum('bhqk,bkhd->bqhd', w.astype(v.dtype), v)

    out = lax.map(block, (qb, jnp.arange(nblk)))
    return out.transpose(1, 0, 2, 3, 4).reshape(bsz, seq, h, dh)


def memory_cross_attention(h, mem_n, w_q, w_kv, g_q, g_k, w_o):
    bsz, seq, _ = h.shape
    mlen = mem_n.shape[1]
    q = (h @ w_q).reshape(bsz, seq, XA_HEADS, XA_HEAD_DIM)
    kv = mem_n @ w_kv
    k = kv[..., :XA_WIDTH].reshape(bsz, mlen, XA_HEADS, XA_HEAD_DIM)
    v = kv[..., XA_WIDTH:].reshape(bsz, mlen, XA_HEADS, XA_HEAD_DIM)
    q = rmsnorm(q, g_q)
    k = rmsnorm(k, g_k)
    scores = jnp.einsum('bqhd,bkhd->bhqk', q, k, preferred_element_type=jnp.float32) * (XA_HEAD_DIM ** -0.5)
    p = jax.nn.softmax(scores, axis=-1).astype(v.dtype)
    o = jnp.einsum('bhqk,bkhd->bqhd', p, v).reshape(bsz, seq, XA_WIDTH)
    return o @ w_o


def _fwd_setup_inputs(seed: int = 0) -> dict:
    key = jax.random.key(seed)
    ks = jax.random.split(key, 32)

    def nrm(k, shape, scale):
        return jax.random.normal(k, shape, jnp.float32) * scale

    def gain(k, shape):
        return 1.0 + 0.01 * jax.random.normal(k, shape, jnp.float32)

    L = DEPTH
    n = jnp.arange(SSM_STATE, dtype=jnp.float32)
    return {
        "x": nrm(ks[0], (BATCH, SEQ, D_MODEL), 1.0),
        "mem": nrm(ks[1], (BATCH, MEM_LEN, D_MODEL), 1.0),
        "g_mix": gain(ks[2], (L, D_MODEL)),
        "w_in": nrm(ks[3], (L, D_MODEL, IN_WIDTH), D_MODEL ** -0.5),
        "ssm_a_re": -0.5 + 0.01 * jax.random.normal(ks[4], (L, SSM_GROUPS, SSM_STATE), jnp.float32),
        "ssm_a_im": jnp.pi * n + 0.01 * jax.random.normal(ks[5], (L, SSM_GROUPS, SSM_STATE), jnp.float32),
        "ssm_log_dt": jax.random.uniform(ks[6], (L, SSM_GROUPS), jnp.float32, math.log(DT_MIN), math.log(DT_MAX)),
        "ssm_b_re": nrm(ks[7], (L, SSM_GROUPS, SSM_STATE, SSM_GROUP), (2 * SSM_GROUP) ** -0.5),
        "ssm_b_im": nrm(ks[8], (L, SSM_GROUPS, SSM_STATE, SSM_GROUP), (2 * SSM_GROUP) ** -0.5),
        "ssm_c_re": nrm(ks[9], (L, SSM_GROUPS, SSM_GROUP, SSM_STATE), 0.5),
        "ssm_c_im": nrm(ks[10], (L, SSM_GROUPS, SSM_GROUP, SSM_STATE), 0.5),
        "ssm_d": nrm(ks[11], (L, SSM_WIDTH), 1.0),
        "ssm_w_glu": nrm(ks[12], (L, SSM_WIDTH, SSM_WIDTH), SSM_WIDTH ** -0.5),
        "sb_g_q": gain(ks[13], (L, SB_HEAD_DIM)),
        "sb_g_k": gain(ks[14], (L, SB_HEAD_DIM)),
        "g_out_ssm": gain(ks[15], (L, SSM_WIDTH)),
        "g_out_sb": gain(ks[16], (L, SB_WIDTH)),
        "w_out": nrm(ks[17], (L, MIX_WIDTH, D_MODEL), MIX_WIDTH ** -0.5),
        "g_xa": gain(ks[18], (L, D_MODEL)),
        "g_mem": gain(ks[19], (L, D_MODEL)),
        "xa_w_q": nrm(ks[20], (L, D_MODEL, XA_WIDTH), D_MODEL ** -0.5),
        "xa_w_kv": nrm(ks[21], (L, D_MODEL, 2 * XA_WIDTH), D_MODEL ** -0.5),
        "xa_g_q": gain(ks[22], (L, XA_HEAD_DIM)),
        "xa_g_k": gain(ks[23], (L, XA_HEAD_DIM)),
        "xa_w_o": nrm(ks[24], (L, XA_WIDTH, D_MODEL), XA_WIDTH ** -0.5),
        "g_mlp": gain(ks[25], (L, D_MODEL)),
        "w_up": nrm(ks[26], (L, D_MODEL, D_FF), D_MODEL ** -0.5),
        "w_down": nrm(ks[27], (L, D_FF, D_MODEL), D_FF ** -0.5),
    }


def _fwd_reference(x, mem, g_mix, w_in, ssm_a_re, ssm_a_im, ssm_log_dt, ssm_b_re, ssm_b_im,
              ssm_c_re, ssm_c_im, ssm_d, ssm_w_glu, sb_g_q, sb_g_k, g_out_ssm, g_out_sb,
              w_out, g_xa, g_mem, xa_w_q, xa_w_kv, xa_g_q, xa_g_k, xa_w_o, g_mlp, w_up, w_down):
    bsz, seq, _ = x.shape
    for l in range(DEPTH):
        h = rmsnorm(x, g_mix[l])
        proj = h @ w_in[l]
        u = proj[..., :SSM_WIDTH]
        q = proj[..., SSM_WIDTH:SSM_WIDTH + SB_WIDTH].reshape(bsz, seq, SB_HEADS, SB_HEAD_DIM)
        k = proj[..., SSM_WIDTH + SB_WIDTH:SSM_WIDTH + 2 * SB_WIDTH].reshape(bsz, seq, SB_HEADS, SB_HEAD_DIM)
        v = proj[..., SSM_WIDTH + 2 * SB_WIDTH:].reshape(bsz, seq, SB_HEADS, SB_HEAD_DIM)
        y_ssm = s5_mixer(u, ssm_a_re[l], ssm_a_im[l], ssm_log_dt[l], ssm_b_re[l], ssm_b_im[l],
                         ssm_c_re[l], ssm_c_im[l], ssm_d[l], ssm_w_glu[l])
        q = rmsnorm(q, sb_g_q[l])
        k = rmsnorm(k, sb_g_k[l])
        y_sb = stick_breaking_attention(q, k, v).reshape(bsz, seq, SB_WIDTH)
        y = jnp.concatenate([rmsnorm(y_ssm, g_out_ssm[l]), rmsnorm(y_sb, g_out_sb[l])], axis=-1)
        x = x + y @ w_out[l]
        h = rmsnorm(x, g_xa[l])
        mem_n = rmsnorm(mem, g_mem[l])
        x = x + memory_cross_attention(h, mem_n, xa_w_q[l], xa_w_kv[l], xa_g_q[l], xa_g_k[l], xa_w_o[l])
        h = rmsnorm(x, g_mlp[l])
        x = x + jnp.square(jax.nn.relu(h @ w_up[l])) @ w_down[l]
    return x


import jax as _jax
import jax.numpy as _jnp

TWIN_FORMAT = 'train_step'
FWD_PARAMS = ['x', 'mem', 'g_mix', 'w_in', 'ssm_a_re', 'ssm_a_im', 'ssm_log_dt', 'ssm_b_re', 'ssm_b_im', 'ssm_c_re', 'ssm_c_im', 'ssm_d', 'ssm_w_glu', 'sb_g_q', 'sb_g_k', 'g_out_ssm', 'g_out_sb', 'w_out', 'g_xa', 'g_mem', 'xa_w_q', 'xa_w_kv', 'xa_g_q', 'xa_g_k', 'xa_w_o', 'g_mlp', 'w_up', 'w_down']
TWIN_WEIGHTS = ['g_mix', 'w_in', 'ssm_a_re', 'ssm_a_im', 'ssm_log_dt', 'ssm_b_re', 'ssm_b_im', 'ssm_c_re', 'ssm_c_im', 'ssm_d', 'ssm_w_glu', 'sb_g_q', 'sb_g_k', 'g_out_ssm', 'g_out_sb', 'w_out', 'g_xa', 'g_mem', 'xa_w_q', 'xa_w_kv', 'xa_g_q', 'xa_g_k', 'xa_w_o', 'g_mlp', 'w_up', 'w_down']
TWIN_DIFF_INPUT = 'x'
TWIN_INPUTS = ['x', 'mem', 'g_mix', 'w_in', 'ssm_a_re', 'ssm_a_im', 'ssm_log_dt', 'ssm_b_re', 'ssm_b_im', 'ssm_c_re', 'ssm_c_im', 'ssm_d', 'ssm_w_glu', 'sb_g_q', 'sb_g_k', 'g_out_ssm', 'g_out_sb', 'w_out', 'g_xa', 'g_mem', 'xa_w_q', 'xa_w_kv', 'xa_g_q', 'xa_g_k', 'xa_w_o', 'g_mlp', 'w_up', 'w_down', 'loss_target', 'm_g_mix', 'm_w_in', 'm_ssm_a_re', 'm_ssm_a_im', 'm_ssm_log_dt', 'm_ssm_b_re', 'm_ssm_b_im', 'm_ssm_c_re', 'm_ssm_c_im', 'm_ssm_d', 'm_ssm_w_glu', 'm_sb_g_q', 'm_sb_g_k', 'm_g_out_ssm', 'm_g_out_sb', 'm_w_out', 'm_g_xa', 'm_g_mem', 'm_xa_w_q', 'm_xa_w_kv', 'm_xa_g_q', 'm_xa_g_k', 'm_xa_w_o', 'm_g_mlp', 'm_w_up', 'm_w_down', 'v_g_mix', 'v_w_in', 'v_ssm_a_re', 'v_ssm_a_im', 'v_ssm_log_dt', 'v_ssm_b_re', 'v_ssm_b_im', 'v_ssm_c_re', 'v_ssm_c_im', 'v_ssm_d', 'v_ssm_w_glu', 'v_sb_g_q', 'v_sb_g_k', 'v_g_out_ssm', 'v_g_out_sb', 'v_w_out', 'v_g_xa', 'v_g_mem', 'v_xa_w_q', 'v_xa_w_kv', 'v_xa_g_q', 'v_xa_g_k', 'v_xa_w_o', 'v_g_mlp', 'v_w_up', 'v_w_down']
TWIN_OUTPUTS = ['loss', 'grad_x', 'grad_g_mix', 'grad_w_in', 'grad_ssm_a_re', 'grad_ssm_a_im', 'grad_ssm_log_dt', 'grad_ssm_b_re', 'grad_ssm_b_im', 'grad_ssm_c_re', 'grad_ssm_c_im', 'grad_ssm_d', 'grad_ssm_w_glu', 'grad_sb_g_q', 'grad_sb_g_k', 'grad_g_out_ssm', 'grad_g_out_sb', 'grad_w_out', 'grad_g_xa', 'grad_g_mem', 'grad_xa_w_q', 'grad_xa_w_kv', 'grad_xa_g_q', 'grad_xa_g_k', 'grad_xa_w_o', 'grad_g_mlp', 'grad_w_up', 'grad_w_down', 'delta_g_mix', 'delta_w_in', 'delta_ssm_a_re', 'delta_ssm_a_im', 'delta_ssm_log_dt', 'delta_ssm_b_re', 'delta_ssm_b_im', 'delta_ssm_c_re', 'delta_ssm_c_im', 'delta_ssm_d', 'delta_ssm_w_glu', 'delta_sb_g_q', 'delta_sb_g_k', 'delta_g_out_ssm', 'delta_g_out_sb', 'delta_w_out', 'delta_g_xa', 'delta_g_mem', 'delta_xa_w_q', 'delta_xa_w_kv', 'delta_xa_g_q', 'delta_xa_g_k', 'delta_xa_w_o', 'delta_g_mlp', 'delta_w_up', 'delta_w_down', 'new_m_g_mix', 'new_m_w_in', 'new_m_ssm_a_re', 'new_m_ssm_a_im', 'new_m_ssm_log_dt', 'new_m_ssm_b_re', 'new_m_ssm_b_im', 'new_m_ssm_c_re', 'new_m_ssm_c_im', 'new_m_ssm_d', 'new_m_ssm_w_glu', 'new_m_sb_g_q', 'new_m_sb_g_k', 'new_m_g_out_ssm', 'new_m_g_out_sb', 'new_m_w_out', 'new_m_g_xa', 'new_m_g_mem', 'new_m_xa_w_q', 'new_m_xa_w_kv', 'new_m_xa_g_q', 'new_m_xa_g_k', 'new_m_xa_w_o', 'new_m_g_mlp', 'new_m_w_up', 'new_m_w_down', 'new_v_g_mix', 'new_v_w_in', 'new_v_ssm_a_re', 'new_v_ssm_a_im', 'new_v_ssm_log_dt', 'new_v_ssm_b_re', 'new_v_ssm_b_im', 'new_v_ssm_c_re', 'new_v_ssm_c_im', 'new_v_ssm_d', 'new_v_ssm_w_glu', 'new_v_sb_g_q', 'new_v_sb_g_k', 'new_v_g_out_ssm', 'new_v_g_out_sb', 'new_v_w_out', 'new_v_g_xa', 'new_v_g_mem', 'new_v_xa_w_q', 'new_v_xa_w_kv', 'new_v_xa_g_q', 'new_v_xa_g_k', 'new_v_xa_w_o', 'new_v_g_mlp', 'new_v_w_up', 'new_v_w_down']
TWIN_LEAF_KINDS = {'loss': 'loss', 'grad_x': 'grad_x', 'grad_g_mix': 'grad_w', 'grad_w_in': 'grad_w', 'grad_ssm_a_re': 'grad_w', 'grad_ssm_a_im': 'grad_w', 'grad_ssm_log_dt': 'grad_w', 'grad_ssm_b_re': 'grad_w', 'grad_ssm_b_im': 'grad_w', 'grad_ssm_c_re': 'grad_w', 'grad_ssm_c_im': 'grad_w', 'grad_ssm_d': 'grad_w', 'grad_ssm_w_glu': 'grad_w', 'grad_sb_g_q': 'grad_w', 'grad_sb_g_k': 'grad_w', 'grad_g_out_ssm': 'grad_w', 'grad_g_out_sb': 'grad_w', 'grad_w_out': 'grad_w', 'grad_g_xa': 'grad_w', 'grad_g_mem': 'grad_w', 'grad_xa_w_q': 'grad_w', 'grad_xa_w_kv': 'grad_w', 'grad_xa_g_q': 'grad_w', 'grad_xa_g_k': 'grad_w', 'grad_xa_w_o': 'grad_w', 'grad_g_mlp': 'grad_w', 'grad_w_up': 'grad_w', 'grad_w_down': 'grad_w', 'delta_g_mix': 'delta_w', 'delta_w_in': 'delta_w', 'delta_ssm_a_re': 'delta_w', 'delta_ssm_a_im': 'delta_w', 'delta_ssm_log_dt': 'delta_w', 'delta_ssm_b_re': 'delta_w', 'delta_ssm_b_im': 'delta_w', 'delta_ssm_c_re': 'delta_w', 'delta_ssm_c_im': 'delta_w', 'delta_ssm_d': 'delta_w', 'delta_ssm_w_glu': 'delta_w', 'delta_sb_g_q': 'delta_w', 'delta_sb_g_k': 'delta_w', 'delta_g_out_ssm': 'delta_w', 'delta_g_out_sb': 'delta_w', 'delta_w_out': 'delta_w', 'delta_g_xa': 'delta_w', 'delta_g_mem': 'delta_w', 'delta_xa_w_q': 'delta_w', 'delta_xa_w_kv': 'delta_w', 'delta_xa_g_q': 'delta_w', 'delta_xa_g_k': 'delta_w', 'delta_xa_w_o': 'delta_w', 'delta_g_mlp': 'delta_w', 'delta_w_up': 'delta_w', 'delta_w_down': 'delta_w', 'new_m_g_mix': 'new_m', 'new_m_w_in': 'new_m', 'new_m_ssm_a_re': 'new_m', 'new_m_ssm_a_im': 'new_m', 'new_m_ssm_log_dt': 'new_m', 'new_m_ssm_b_re': 'new_m', 'new_m_ssm_b_im': 'new_m', 'new_m_ssm_c_re': 'new_m', 'new_m_ssm_c_im': 'new_m', 'new_m_ssm_d': 'new_m', 'new_m_ssm_w_glu': 'new_m', 'new_m_sb_g_q': 'new_m', 'new_m_sb_g_k': 'new_m', 'new_m_g_out_ssm': 'new_m', 'new_m_g_out_sb': 'new_m', 'new_m_w_out': 'new_m', 'new_m_g_xa': 'new_m', 'new_m_g_mem': 'new_m', 'new_m_xa_w_q': 'new_m', 'new_m_xa_w_kv': 'new_m', 'new_m_xa_g_q': 'new_m', 'new_m_xa_g_k': 'new_m', 'new_m_xa_w_o': 'new_m', 'new_m_g_mlp': 'new_m', 'new_m_w_up': 'new_m', 'new_m_w_down': 'new_m', 'new_v_g_mix': 'new_v', 'new_v_w_in': 'new_v', 'new_v_ssm_a_re': 'new_v', 'new_v_ssm_a_im': 'new_v', 'new_v_ssm_log_dt': 'new_v', 'new_v_ssm_b_re': 'new_v', 'new_v_ssm_b_im': 'new_v', 'new_v_ssm_c_re': 'new_v', 'new_v_ssm_c_im': 'new_v', 'new_v_ssm_d': 'new_v', 'new_v_ssm_w_glu': 'new_v', 'new_v_sb_g_q': 'new_v', 'new_v_sb_g_k': 'new_v', 'new_v_g_out_ssm': 'new_v', 'new_v_g_out_sb': 'new_v', 'new_v_w_out': 'new_v', 'new_v_g_xa': 'new_v', 'new_v_g_mem': 'new_v', 'new_v_xa_w_q': 'new_v', 'new_v_xa_w_kv': 'new_v', 'new_v_xa_g_q': 'new_v', 'new_v_xa_g_k': 'new_v', 'new_v_xa_w_o': 'new_v', 'new_v_g_mlp': 'new_v', 'new_v_w_up': 'new_v', 'new_v_w_down': 'new_v'}


def _forward(args):
    return _fwd_reference(*[args[k] for k in FWD_PARAMS])


def _output_shape():
    def fwd():
        inp = _fwd_setup_inputs(0)
        return _fwd_reference(*[inp[k] for k in FWD_PARAMS])
    out = _jax.eval_shape(fwd)
    return out.shape, out.dtype

N_MICROBATCH = 1
ADAM_LR = 0.001
ADAM_B1 = 0.9
ADAM_B2 = 0.999
ADAM_EPS = 1e-08
ADAM_WD = 0.01
ADAM_STEP = 10
PER_EXAMPLE_BATCH_AXIS = {'x': 0, 'mem': 0, 'loss_target': 0}
SHARED_INPUTS = []
_WEIGHT_DTYPES = {'g_mix': _jnp.float32, 'w_in': _jnp.float32, 'ssm_a_re': _jnp.float32, 'ssm_a_im': _jnp.float32, 'ssm_log_dt': _jnp.float32, 'ssm_b_re': _jnp.float32, 'ssm_b_im': _jnp.float32, 'ssm_c_re': _jnp.float32, 'ssm_c_im': _jnp.float32, 'ssm_d': _jnp.float32, 'ssm_w_glu': _jnp.float32, 'sb_g_q': _jnp.float32, 'sb_g_k': _jnp.float32, 'g_out_ssm': _jnp.float32, 'g_out_sb': _jnp.float32, 'w_out': _jnp.float32, 'g_xa': _jnp.float32, 'g_mem': _jnp.float32, 'xa_w_q': _jnp.float32, 'xa_w_kv': _jnp.float32, 'xa_g_q': _jnp.float32, 'xa_g_k': _jnp.float32, 'xa_w_o': _jnp.float32, 'g_mlp': _jnp.float32, 'w_up': _jnp.float32, 'w_down': _jnp.float32}
MOMENT_SCALE = {'g_mix': 2.158118e+00, 'w_in': 1.203054e+00, 'ssm_a_re': 8.584140e-01, 'ssm_a_im': 3.650025e-01, 'ssm_log_dt': 2.404486e+02, 'ssm_b_re': 2.909406e-01, 'ssm_b_im': 2.900642e-01, 'ssm_c_re': 1.441310e-01, 'ssm_c_im': 1.536295e-01, 'ssm_d': 2.617133e+01, 'ssm_w_glu': 4.439309e+00, 'sb_g_q': 1.210405e+00, 'sb_g_k': 1.222093e+00, 'g_out_ssm': 1.105886e+02, 'g_out_sb': 6.430722e+01, 'w_out': 1.982877e+01, 'g_xa': 4.827765e-01, 'g_mem': 1.635996e+00, 'xa_w_q': 6.756889e-01, 'xa_w_kv': 1.473371e+00, 'xa_g_q': 5.389637e+00, 'xa_g_k': 5.392469e+00, 'xa_w_o': 1.299424e+00, 'g_mlp': 1.924984e+02, 'w_up': 7.698983e+00, 'w_down': 2.134542e+01}


def _to_microbatches(a, axis):
    t = _jnp.moveaxis(a, axis, 0)
    t = t.reshape((N_MICROBATCH, t.shape[0] // N_MICROBATCH) + t.shape[1:])
    return _jnp.moveaxis(t, 1, axis + 1)


def setup_inputs(seed: int = 0) -> dict:
    inp = _fwd_setup_inputs(seed)
    key = _jax.random.fold_in(_jax.random.key(seed), 7919)
    shape, _ = _output_shape()
    out = dict(inp)
    out["loss_target"] = _jax.random.normal(_jax.random.fold_in(key, 0), shape, _jnp.float32)
    for i, name in enumerate(TWIN_WEIGHTS):
        w = inp[name].astype(_jnp.float32)
        if MOMENT_SCALE is None:
            s = _jnp.sqrt(_jnp.mean(_jnp.square(w)) + 1e-30)
        else:
            s = MOMENT_SCALE[name]
        km, kv = _jax.random.split(_jax.random.fold_in(key, i + 1))
        out[name] = w
        out["m_" + name] = s * _jax.random.normal(km, w.shape, _jnp.float32)
        out["v_" + name] = (s * s) * _jax.random.uniform(kv, w.shape, _jnp.float32, 0.5, 1.5)
    if N_MICROBATCH > 1:
        for name, axis in PER_EXAMPLE_BATCH_AXIS.items():
            out[name] = _to_microbatches(out[name], axis)
    return {'x': out['x'], 'mem': out['mem'], 'g_mix': out['g_mix'], 'w_in': out['w_in'], 'ssm_a_re': out['ssm_a_re'], 'ssm_a_im': out['ssm_a_im'], 'ssm_log_dt': out['ssm_log_dt'], 'ssm_b_re': out['ssm_b_re'], 'ssm_b_im': out['ssm_b_im'], 'ssm_c_re': out['ssm_c_re'], 'ssm_c_im': out['ssm_c_im'], 'ssm_d': out['ssm_d'], 'ssm_w_glu': out['ssm_w_glu'], 'sb_g_q': out['sb_g_q'], 'sb_g_k': out['sb_g_k'], 'g_out_ssm': out['g_out_ssm'], 'g_out_sb': out['g_out_sb'], 'w_out': out['w_out'], 'g_xa': out['g_xa'], 'g_mem': out['g_mem'], 'xa_w_q': out['xa_w_q'], 'xa_w_kv': out['xa_w_kv'], 'xa_g_q': out['xa_g_q'], 'xa_g_k': out['xa_g_k'], 'xa_w_o': out['xa_w_o'], 'g_mlp': out['g_mlp'], 'w_up': out['w_up'], 'w_down': out['w_down'], 'loss_target': out['loss_target'], 'm_g_mix': out['m_g_mix'], 'm_w_in': out['m_w_in'], 'm_ssm_a_re': out['m_ssm_a_re'], 'm_ssm_a_im': out['m_ssm_a_im'], 'm_ssm_log_dt': out['m_ssm_log_dt'], 'm_ssm_b_re': out['m_ssm_b_re'], 'm_ssm_b_im': out['m_ssm_b_im'], 'm_ssm_c_re': out['m_ssm_c_re'], 'm_ssm_c_im': out['m_ssm_c_im'], 'm_ssm_d': out['m_ssm_d'], 'm_ssm_w_glu': out['m_ssm_w_glu'], 'm_sb_g_q': out['m_sb_g_q'], 'm_sb_g_k': out['m_sb_g_k'], 'm_g_out_ssm': out['m_g_out_ssm'], 'm_g_out_sb': out['m_g_out_sb'], 'm_w_out': out['m_w_out'], 'm_g_xa': out['m_g_xa'], 'm_g_mem': out['m_g_mem'], 'm_xa_w_q': out['m_xa_w_q'], 'm_xa_w_kv': out['m_xa_w_kv'], 'm_xa_g_q': out['m_xa_g_q'], 'm_xa_g_k': out['m_xa_g_k'], 'm_xa_w_o': out['m_xa_w_o'], 'm_g_mlp': out['m_g_mlp'], 'm_w_up': out['m_w_up'], 'm_w_down': out['m_w_down'], 'v_g_mix': out['v_g_mix'], 'v_w_in': out['v_w_in'], 'v_ssm_a_re': out['v_ssm_a_re'], 'v_ssm_a_im': out['v_ssm_a_im'], 'v_ssm_log_dt': out['v_ssm_log_dt'], 'v_ssm_b_re': out['v_ssm_b_re'], 'v_ssm_b_im': out['v_ssm_b_im'], 'v_ssm_c_re': out['v_ssm_c_re'], 'v_ssm_c_im': out['v_ssm_c_im'], 'v_ssm_d': out['v_ssm_d'], 'v_ssm_w_glu': out['v_ssm_w_glu'], 'v_sb_g_q': out['v_sb_g_q'], 'v_sb_g_k': out['v_sb_g_k'], 'v_g_out_ssm': out['v_g_out_ssm'], 'v_g_out_sb': out['v_g_out_sb'], 'v_w_out': out['v_w_out'], 'v_g_xa': out['v_g_xa'], 'v_g_mem': out['v_g_mem'], 'v_xa_w_q': out['v_xa_w_q'], 'v_xa_w_kv': out['v_xa_w_kv'], 'v_xa_g_q': out['v_xa_g_q'], 'v_xa_g_k': out['v_xa_g_k'], 'v_xa_w_o': out['v_xa_w_o'], 'v_g_mlp': out['v_g_mlp'], 'v_w_up': out['v_w_up'], 'v_w_down': out['v_w_down']}


def _loss(weights, diff, rest, loss_target):
    with _jax.named_scope("forward"):
        args = {**rest, TWIN_DIFF_INPUT: diff, **{k: w.astype(_WEIGHT_DTYPES[k]) for k, w in weights.items()}}
        y = _forward(args)
    with _jax.named_scope("loss_head"):
        err = _jnp.square(y.astype(_jnp.float32) - loss_target)
        return 0.5 * _jnp.sum(_jnp.mean(err, axis=-1)) if err.ndim else 0.5 * err


def _adamw(w, g, m, v):
    m = ADAM_B1 * m + (1.0 - ADAM_B1) * g
    v = ADAM_B2 * v + (1.0 - ADAM_B2) * _jnp.square(g)
    m_hat = m / (1.0 - ADAM_B1 ** ADAM_STEP)
    v_hat = v / (1.0 - ADAM_B2 ** ADAM_STEP)
    delta = -ADAM_LR * (m_hat / (_jnp.sqrt(v_hat) + ADAM_EPS) + ADAM_WD * w)
    return delta, m, v


def reference(x, mem, g_mix, w_in, ssm_a_re, ssm_a_im, ssm_log_dt, ssm_b_re, ssm_b_im, ssm_c_re, ssm_c_im, ssm_d, ssm_w_glu, sb_g_q, sb_g_k, g_out_ssm, g_out_sb, w_out, g_xa, g_mem, xa_w_q, xa_w_kv, xa_g_q, xa_g_k, xa_w_o, g_mlp, w_up, w_down, loss_target, m_g_mix, m_w_in, m_ssm_a_re, m_ssm_a_im, m_ssm_log_dt, m_ssm_b_re, m_ssm_b_im, m_ssm_c_re, m_ssm_c_im, m_ssm_d, m_ssm_w_glu, m_sb_g_q, m_sb_g_k, m_g_out_ssm, m_g_out_sb, m_w_out, m_g_xa, m_g_mem, m_xa_w_q, m_xa_w_kv, m_xa_g_q, m_xa_g_k, m_xa_w_o, m_g_mlp, m_w_up, m_w_down, v_g_mix, v_w_in, v_ssm_a_re, v_ssm_a_im, v_ssm_log_dt, v_ssm_b_re, v_ssm_b_im, v_ssm_c_re, v_ssm_c_im, v_ssm_d, v_ssm_w_glu, v_sb_g_q, v_sb_g_k, v_g_out_ssm, v_g_out_sb, v_w_out, v_g_xa, v_g_mem, v_xa_w_q, v_xa_w_kv, v_xa_g_q, v_xa_g_k, v_xa_w_o, v_g_mlp, v_w_up, v_w_down):
    given = dict(x=x, mem=mem, g_mix=g_mix, w_in=w_in, ssm_a_re=ssm_a_re, ssm_a_im=ssm_a_im, ssm_log_dt=ssm_log_dt, ssm_b_re=ssm_b_re, ssm_b_im=ssm_b_im, ssm_c_re=ssm_c_re, ssm_c_im=ssm_c_im, ssm_d=ssm_d, ssm_w_glu=ssm_w_glu, sb_g_q=sb_g_q, sb_g_k=sb_g_k, g_out_ssm=g_out_ssm, g_out_sb=g_out_sb, w_out=w_out, g_xa=g_xa, g_mem=g_mem, xa_w_q=xa_w_q, xa_w_kv=xa_w_kv, xa_g_q=xa_g_q, xa_g_k=xa_g_k, xa_w_o=xa_w_o, g_mlp=g_mlp, w_up=w_up, w_down=w_down, loss_target=loss_target, m_g_mix=m_g_mix, m_w_in=m_w_in, m_ssm_a_re=m_ssm_a_re, m_ssm_a_im=m_ssm_a_im, m_ssm_log_dt=m_ssm_log_dt, m_ssm_b_re=m_ssm_b_re, m_ssm_b_im=m_ssm_b_im, m_ssm_c_re=m_ssm_c_re, m_ssm_c_im=m_ssm_c_im, m_ssm_d=m_ssm_d, m_ssm_w_glu=m_ssm_w_glu, m_sb_g_q=m_sb_g_q, m_sb_g_k=m_sb_g_k, m_g_out_ssm=m_g_out_ssm, m_g_out_sb=m_g_out_sb, m_w_out=m_w_out, m_g_xa=m_g_xa, m_g_mem=m_g_mem, m_xa_w_q=m_xa_w_q, m_xa_w_kv=m_xa_w_kv, m_xa_g_q=m_xa_g_q, m_xa_g_k=m_xa_g_k, m_xa_w_o=m_xa_w_o, m_g_mlp=m_g_mlp, m_w_up=m_w_up, m_w_down=m_w_down, v_g_mix=v_g_mix, v_w_in=v_w_in, v_ssm_a_re=v_ssm_a_re, v_ssm_a_im=v_ssm_a_im, v_ssm_log_dt=v_ssm_log_dt, v_ssm_b_re=v_ssm_b_re, v_ssm_b_im=v_ssm_b_im, v_ssm_c_re=v_ssm_c_re, v_ssm_c_im=v_ssm_c_im, v_ssm_d=v_ssm_d, v_ssm_w_glu=v_ssm_w_glu, v_sb_g_q=v_sb_g_q, v_sb_g_k=v_sb_g_k, v_g_out_ssm=v_g_out_ssm, v_g_out_sb=v_g_out_sb, v_w_out=v_w_out, v_g_xa=v_g_xa, v_g_mem=v_g_mem, v_xa_w_q=v_xa_w_q, v_xa_w_kv=v_xa_w_kv, v_xa_g_q=v_xa_g_q, v_xa_g_k=v_xa_g_k, v_xa_w_o=v_xa_w_o, v_g_mlp=v_g_mlp, v_w_up=v_w_up, v_w_down=v_w_down)
    weights = {n: given[n] for n in TWIN_WEIGHTS}
    shared = {n: given[n] for n in SHARED_INPUTS}
    per_example = {n: given[n] for n in ['x', 'mem']}
    grad_fn = _jax.value_and_grad(_loss, argnums=(0, 1))

    def one_microbatch(ex, loss_target):
        ex = dict(ex)
        diff = ex.pop(TWIN_DIFF_INPUT)
        return grad_fn(weights, diff, {**shared, **ex}, loss_target)

    if N_MICROBATCH == 1:
        loss, (grad_w, grad_x) = one_microbatch(per_example, given["loss_target"])
    else:
        def body(carry, xs):
            loss_sum, grad_sum = carry
            l_k, (gw_k, gx_k) = one_microbatch(xs[0], xs[1])
            with _jax.named_scope("update"):
                return (loss_sum + l_k, _jax.tree.map(_jnp.add, grad_sum, gw_k)), gx_k

        init = (_jnp.zeros((), _jnp.float32), _jax.tree.map(_jnp.zeros_like, weights))
        (loss, grad_w), grad_x = _jax.lax.scan(body, init, (per_example, given["loss_target"]))
    with _jax.named_scope("update"):
        delta_w, new_m, new_v = {}, {}, {}
        for n in TWIN_WEIGHTS:
            delta_w[n], new_m[n], new_v[n] = _adamw(weights[n], grad_w[n], given["m_" + n], given["v_" + n])
    return (loss, grad_x, *[grad_w[n] for n in TWIN_WEIGHTS], *[delta_w[n] for n in TWIN_WEIGHTS],
            *[new_m[n] for n in TWIN_WEIGHTS], *[new_v[n] for n in TWIN_WEIGHTS])
```

```python
import jax
import jax.numpy as jnp
from jax import lax
from jax.experimental import pallas as pl
from jax.experimental.pallas import tpu as pltpu

f32 = jnp.float32
bf16 = jnp.bfloat16

NORM_EPS = 1e-6
SSM_GROUPS = 32
SSM_GROUP = 16
SSM_STATE = 64
SB_HEADS = 8
SB_HEAD_DIM = 64
XA_HEADS = 4
XA_HEAD_DIM = 128
LANES = 128
SUBLANES = 8
N_CHIPS = 4
ADAM_LR = 0.001
ADAM_B1 = 0.9
ADAM_B2 = 0.999
ADAM_EPS = 1e-08
ADAM_WD = 0.01
ADAM_STEP = 10
VMEM_LIMIT = 56 * 1024 * 1024
MESH = pl.DeviceIdType.MESH
ANY = pl.BlockSpec(memory_space=pl.ANY)


def _cp(sem=None):
    return pltpu.CompilerParams(dimension_semantics=sem, vmem_limit_bytes=VMEM_LIMIT)


def _tile(n, pref):
    if n <= pref:
        return n
    t = (pref // LANES) * LANES
    while t > LANES and n % t:
        t -= LANES
    assert n % t == 0, (n, pref)
    return t


def _row_tile(n, pref):
    if n <= pref:
        return n
    t = (pref // SUBLANES) * SUBLANES
    while n % t:
        t -= SUBLANES
    return t


def _dot(a, b, dims):
    return lax.dot_general(a.astype(bf16), b.astype(bf16), (dims, ((), ())), preferred_element_type=f32)


_NN = ((1,), (0,))
_NT = ((1,), (1,))
_TN = ((0,), (0,))


@jax.custom_vjp
def bdot_nn(a, b):
    return _dot(a, b, _NN)


def _bdot_nn_fwd(a, b):
    return _dot(a, b, _NN), (a, b)


def _bdot_nn_bwd(res, g):
    a, b = res
    return _dot(g, b, _NT), _dot(a, g, _TN)


bdot_nn.defvjp(_bdot_nn_fwd, _bdot_nn_bwd)


@jax.custom_vjp
def bdot_nt(a, b):
    return _dot(a, b, _NT)


def _bdot_nt_fwd(a, b):
    return _dot(a, b, _NT), (a, b)


def _bdot_nt_bwd(res, g):
    a, b = res
    return _dot(g, b, _NN), _dot(g, a, _TN)


bdot_nt.defvjp(_bdot_nt_fwd, _bdot_nt_bwd)


def _rms(x, g, denom):
    r = lax.rsqrt(jnp.sum(x * x, axis=-1, keepdims=True) * (1.0 / denom) + NORM_EPS)
    return x * r * g


def _opspec(block, row_of, col_of, shards, ncol_tiles):
    if shards == 1:
        return pl.BlockSpec(block, lambda i, j, k: (row_of(i, j, k), col_of(i, j, k)))
    per = ncol_tiles // shards
    return pl.BlockSpec((None,) + block,
                        lambda i, j, k: (col_of(i, j, k) // per, row_of(i, j, k), col_of(i, j, k) % per))


def mm(a, b, *, mode, name, tm=512, tn=512, tk=512, pro="none", r=None, g=None, epi="none", aux=None,
       out_dtype=f32, a_shards=1, b_shards=1, out_shards=1):
    ar, ac = a.shape[-2], a.shape[-1] * a_shards
    br, bc = b.shape[-2], b.shape[-1] * b_shards
    if mode == "nn":
        M, K, N = ar, ac, bc
        assert br == K
    elif mode == "nt":
        M, K, N = ar, ac, br
        assert bc == K
    else:
        M, K, N = ac, ar, bc
        assert br == K
    tm, tn, tk = _tile(M, tm), _tile(N, tn), _tile(K, tk)
    if a_shards > 1:
        if mode == "tn":
            tm = _tile(ac // a_shards, tm)
        else:
            tk = _tile(ac // a_shards, tk)
    if b_shards > 1:
        if mode == "nt":
            tk = _tile(bc // b_shards, tk)
        else:
            tn = _tile(bc // b_shards, tn)
    if out_shards > 1:
        tn = _tile(N // out_shards, tn)
    nm, nn_, nk = M // tm, N // tn, K // tk
    I = lambda i, j, k: i
    J = lambda i, j, k: j
    Kk = lambda i, j, k: k
    if mode == "nn":
        a_spec = _opspec((tm, tk), I, Kk, a_shards, nk)
        b_spec = _opspec((tk, tn), Kk, J, b_shards, nn_)
        dims = _NN
    elif mode == "nt":
        a_spec = _opspec((tm, tk), I, Kk, a_shards, nk)
        b_spec = _opspec((tn, tk), J, Kk, b_shards, nk)
        dims = _NT
    else:
        a_spec = _opspec((tk, tm), Kk, I, a_shards, nm)
        b_spec = _opspec((tk, tn), Kk, J, b_shards, nn_)
        dims = _TN
    in_specs = [a_spec, b_spec]
    args = [a, b]
    if pro == "rms":
        if mode == "tn":
            in_specs += [pl.BlockSpec((tk, 1), lambda i, j, k: (k, 0)), pl.BlockSpec((1, tm), lambda i, j, k: (0, i))]
        else:
            in_specs += [pl.BlockSpec((tm, 1), lambda i, j, k: (i, 0)), pl.BlockSpec((1, tk), lambda i, j, k: (0, k))]
        args += [r, g]
    if epi != "none":
        in_specs.append(pl.BlockSpec((tm, tn), lambda i, j, k: (i, j)))
        args.append(aux)
    if out_shards == 1:
        out_spec = pl.BlockSpec((tm, tn), lambda i, j, k: (i, j))
        out_shape = jax.ShapeDtypeStruct((M, N), out_dtype)
    else:
        per = nn_ // out_shards
        out_spec = pl.BlockSpec((None, tm, tn), lambda i, j, k: (j // per, i, j % per))
        out_shape = jax.ShapeDtypeStruct((out_shards, M, N // out_shards), out_dtype)

    def body(*refs):
        a_ref, b_ref = refs[0], refs[1]
        pos = 2
        if pro == "rms":
            r_ref, g_ref = refs[2], refs[3]
            pos = 4
        if epi != "none":
            aux_ref = refs[pos]
            pos += 1
        o_ref, acc_ref = refs[pos], refs[pos + 1]
        k = pl.program_id(2)

        @pl.when(k == 0)
        def _():
            acc_ref[...] = jnp.zeros_like(acc_ref)

        av = a_ref[...]
        if pro == "rms":
            av = av.astype(f32) * r_ref[...] * g_ref[...]
        elif pro == "relu2":
            av = jnp.square(jnp.maximum(av.astype(f32), 0.0))
        acc_ref[...] += _dot(av, b_ref[...], dims)

        @pl.when(k == nk - 1)
        def _():
            res = acc_ref[...]
            if epi == "add":
                res = res + aux_ref[...].astype(f32)
            elif epi == "mul2relu":
                res = res * (2.0 * jnp.maximum(aux_ref[...].astype(f32), 0.0))
            o_ref[...] = res.astype(out_dtype)

    return pl.pallas_call(
        body, name=name, grid=(nm, nn_, nk), in_specs=in_specs, out_specs=out_spec, out_shape=out_shape,
        scratch_shapes=[pltpu.VMEM((tm, tn), f32)],
        compiler_params=_cp(("parallel", "parallel", "arbitrary")),
    )(*args)


def rms_stats(x, denom, *, name, ts=512):
    S, D = x.shape
    ts = _tile(S, ts)

    def body(x_ref, r_ref):
        xv = x_ref[...].astype(f32)
        r_ref[...] = lax.rsqrt(jnp.sum(xv * xv, axis=-1, keepdims=True) * (1.0 / denom) + NORM_EPS)

    return pl.pallas_call(
        body, name=name, grid=(S // ts,), in_specs=[pl.BlockSpec((ts, D), lambda i: (i, 0))],
        out_specs=pl.BlockSpec((ts, 1), lambda i: (i, 0)), out_shape=jax.ShapeDtypeStruct((S, 1), f32),
        compiler_params=_cp(("parallel",)),
    )(x)


def rms_bwd(x, g, dy, res, denom, *, name, ts=256):
    S, D = x.shape
    ts = _tile(S, ts)
    has_res = res is not None

    def body(*refs):
        if has_res:
            x_ref, g_ref, dy_ref, res_ref, dx_ref, dg_ref = refs
        else:
            x_ref, g_ref, dy_ref, dx_ref, dg_ref = refs
        _, vjp = jax.vjp(lambda xv, gv: _rms(xv, gv, denom), x_ref[...], g_ref[...])
        dx, dg = vjp(dy_ref[...])
        if has_res:
            dx = dx + res_ref[...]
        dx_ref[...] = dx

        @pl.when(pl.program_id(0) == 0)
        def _():
            dg_ref[...] = jnp.zeros_like(dg_ref)

        dg_ref[...] += dg

    row = pl.BlockSpec((ts, D), lambda i: (i, 0))
    vec = pl.BlockSpec((1, D), lambda i: (0, 0))
    in_specs = [row, vec, row] + ([row] if has_res else [])
    args = [x, g, dy] + ([res] if has_res else [])
    return pl.pallas_call(
        body, name=name, grid=(S // ts,), in_specs=in_specs, out_specs=[row, vec],
        out_shape=[jax.ShapeDtypeStruct((S, D), f32), jax.ShapeDtypeStruct((1, D), f32)],
        compiler_params=_cp(("arbitrary",)),
    )(*args)


def _qk_fn(q, k, gq, gk):
    qs, ks = [], []
    for h in range(SB_HEADS):
        sl = slice(h * LANES, (h + 1) * LANES)
        qs.append(_rms(q[:, sl], gq, SB_HEAD_DIM) * (SB_HEAD_DIM ** -0.5))
        ks.append(_rms(k[:, sl], gk, SB_HEAD_DIM))
    return jnp.concatenate(qs, axis=1), jnp.concatenate(ks, axis=1)


def qkv_prep(qkv, gq, gk, *, ts=256):
    S = qkv.shape[0]
    W = SB_HEADS * LANES
    ts = _tile(S, ts)

    def body(q_ref, k_ref, v_ref, gq_ref, gk_ref, qn_ref, kn_ref, vb_ref):
        qn, kn = _qk_fn(q_ref[...], k_ref[...], gq_ref[...], gk_ref[...])
        qn_ref[...] = qn.astype(bf16)
        kn_ref[...] = kn.astype(bf16)
        vb_ref[...] = v_ref[...].astype(bf16)

    out = jax.ShapeDtypeStruct((S, W), bf16)
    gspec = pl.BlockSpec((1, LANES), lambda i: (0, 0))
    ospec = pl.BlockSpec((ts, W), lambda i: (i, 0))
    col = lambda c: pl.BlockSpec((ts, W), lambda i: (i, c))
    return pl.pallas_call(
        body, name="qkv_prep", grid=(S // ts,), in_specs=[col(0), col(1), col(2), gspec, gspec],
        out_specs=[ospec, ospec, ospec], out_shape=[out, out, out], compiler_params=_cp(("parallel",)),
    )(qkv, qkv, qkv, gq, gk)


def qkv_bwd(qkv, gq, gk, dqn, dkn, dv, *, ts=256):
    S = qkv.shape[0]
    W = SB_HEADS * LANES
    ts = _tile(S, ts)

    def body(q_ref, k_ref, gq_ref, gk_ref, dqn_ref, dkn_ref, dv_ref, o_ref, dgq_ref, dgk_ref):
        _, vjp = jax.vjp(_qk_fn, q_ref[...], k_ref[...], gq_ref[...], gk_ref[...])
        dq, dk, dgq, dgk = vjp((dqn_ref[...], dkn_ref[...]))
        o_ref[:, 0:W] = dq
        o_ref[:, W:2 * W] = dk
        o_ref[:, 2 * W:3 * W] = dv_ref[...]

        @pl.when(pl.program_id(0) == 0)
        def _():
            dgq_ref[...] = jnp.zeros_like(dgq_ref)
            dgk_ref[...] = jnp.zeros_like(dgk_ref)

        dgq_ref[...] += dgq
        dgk_ref[...] += dgk

    gspec = pl.BlockSpec((1, LANES), lambda i: (0, 0))
    row = pl.BlockSpec((ts, W), lambda i: (i, 0))
    col = lambda c: pl.BlockSpec((ts, W), lambda i: (i, c))
    return pl.pallas_call(
        body, name="qkv_bwd", grid=(S // ts,), in_specs=[col(0), col(1), gspec, gspec, row, row, row],
        out_specs=[pl.BlockSpec((ts, 3 * W), lambda i: (i, 0)), gspec, gspec],
        out_shape=[jax.ShapeDtypeStruct((S, 3 * W), f32), jax.ShapeDtypeStruct((1, LANES), f32),
                   jax.ShapeDtypeStruct((1, LANES), f32)],
        compiler_params=_cp(("arbitrary",)),
    )(qkv, qkv, gq, gk, dqn, dkn, dv)


def _sb_block(q, k, R, masked, row, col, U):
    l = _dot(q, k, _NT)
    lp = jnp.log1p(jnp.exp(-jnp.abs(l)))
    lb = jnp.minimum(l, 0.0) - lp
    lm = lb - l
    if masked:
        lm = jnp.where(col < row, lm, 0.0)
    hi = lm.astype(bf16)
    lo = (lm - hi.astype(f32)).astype(bf16)
    A = _dot(hi, U, _NN) + _dot(lo, U, _NN) + R
    w = jnp.exp(lb + A)
    if masked:
        w = jnp.where(col < row, w, 0.0)
    R_new = A[:, 0:1] + lm[:, 0:1]
    return lb, w, R_new


def sb_fwd(qn, kn, vb, *, tq=256):
    S, W = qn.shape
    H = W // LANES
    tq = _tile(S, tq)
    tk = tq
    nq = S // tq

    def body(q_ref, k_ref, v_ref, o_ref):
        i = pl.program_id(1)
        q = q_ref[...]
        row = lax.broadcasted_iota(jnp.int32, (tq, tk), 0)
        col = lax.broadcasted_iota(jnp.int32, (tq, tk), 1)
        U = (row > col).astype(bf16)

        def blk(j, R, acc, masked):
            start = pl.multiple_of(j * tk, tk)
            k = k_ref[pl.ds(start, tk), :]
            v = v_ref[pl.ds(start, tk), :]
            _, w, R = _sb_block(q, k, R, masked, row, col, U)
            return R, acc + _dot(w, v, _NN)

        R, acc = blk(i, jnp.zeros((tq, 1), f32), jnp.zeros((tq, LANES), f32), True)
        R, acc = lax.fori_loop(0, i, lambda jj, c: blk(i - 1 - jj, c[0], c[1], False), (R, acc))
        o_ref[...] = acc

    qspec = pl.BlockSpec((tq, LANES), lambda h, i: (i, h))
    kspec = pl.BlockSpec((S, LANES), lambda h, i: (0, h))
    return pl.pallas_call(
        body, name="sb_fwd", grid=(H, nq), in_specs=[qspec, kspec, kspec], out_specs=qspec,
        out_shape=jax.ShapeDtypeStruct((S, W), f32), compiler_params=_cp(("parallel", "arbitrary")),
    )(qn, kn, vb)


def sb_bwd(qn, kn, vb, do, *, tq=256):
    S, W = qn.shape
    H = W // LANES
    tq = _tile(S, tq)
    tk = tq
    nq = S // tq

    def body(q_ref, k_ref, v_ref, do_ref, dq_ref, dk_ref, dv_ref, dz_s, beta_s):
        i = pl.program_id(1)

        @pl.when(i == 0)
        def _():
            dk_ref[...] = jnp.zeros_like(dk_ref)
            dv_ref[...] = jnp.zeros_like(dv_ref)

        q = q_ref[...]
        dob = do_ref[...].astype(bf16)
        row = lax.broadcasted_iota(jnp.int32, (tq, tk), 0)
        col = lax.broadcasted_iota(jnp.int32, (tq, tk), 1)
        U = (row > col).astype(bf16)
        Ue = (row < col).astype(bf16)

        def sweep1(j, R, masked):
            start = pl.multiple_of(j * tk, tk)
            k = k_ref[pl.ds(start, tk), :]
            v = v_ref[pl.ds(start, tk), :]
            lb, w, R = _sb_block(q, k, R, masked, row, col, U)
            dz_s[j] = _dot(dob, v, _NT) * w
            beta_s[j] = jnp.exp(lb)
            dv_ref[pl.ds(start, tk), :] += _dot(w, dob, _TN)
            return R

        R = sweep1(i, jnp.zeros((tq, 1), f32), True)
        lax.fori_loop(0, i, lambda jj, R: sweep1(i - 1 - jj, R, False), R)

        def sweep2(j, Lz, dq, masked):
            start = pl.multiple_of(j * tk, tk)
            k = k_ref[pl.ds(start, tk), :]
            dz = dz_s[j]
            hi = dz.astype(bf16)
            lo = (dz - hi.astype(f32)).astype(bf16)
            Cz = _dot(hi, Ue, _NN) + _dot(lo, Ue, _NN) + Lz
            dl = dz - beta_s[j] * (dz + Cz)
            if masked:
                dl = jnp.where(col < row, dl, 0.0)
            dlb = dl.astype(bf16)
            dk_ref[pl.ds(start, tk), :] += _dot(dlb, q, _TN)
            return Cz[:, tk - 1:tk] + dz[:, tk - 1:tk], dq + _dot(dlb, k, _NN)

        c = (jnp.zeros((tq, 1), f32), jnp.zeros((tq, LANES), f32))
        c = lax.fori_loop(0, i, lambda j, c: sweep2(j, c[0], c[1], False), c)
        dq_ref[...] = sweep2(i, c[0], c[1], True)[1]

    qspec = pl.BlockSpec((tq, LANES), lambda h, i: (i, h))
    kspec = pl.BlockSpec((S, LANES), lambda h, i: (0, h))
    full = jax.ShapeDtypeStruct((S, W), f32)
    return pl.pallas_call(
        body, name="sb_bwd", grid=(H, nq), in_specs=[qspec, kspec, kspec, qspec],
        out_specs=[qspec, kspec, kspec], out_shape=[full, full, full],
        scratch_shapes=[pltpu.VMEM((nq, tq, tk), f32), pltpu.VMEM((nq, tq, tk), f32)],
        compiler_params=_cp(("parallel", "arbitrary")),
    )(qn, kn, vb, do)


def _xa_fn(qx, kv, gq, gk):
    XW = XA_HEADS * XA_HEAD_DIM
    outs = []
    for h in range(XA_HEADS):
        sl = slice(h * XA_HEAD_DIM, (h + 1) * XA_HEAD_DIM)
        qn = _rms(qx[:, sl], gq, XA_HEAD_DIM)
        kn = _rms(kv[:, sl], gk, XA_HEAD_DIM)
        v = kv[:, XW + h * XA_HEAD_DIM:XW + (h + 1) * XA_HEAD_DIM]
        s = bdot_nt(qn, kn) * (XA_HEAD_DIM ** -0.5)
        e = jnp.exp(s - lax.stop_gradient(jnp.max(s, axis=-1, keepdims=True)))
        p = e / jnp.sum(e, axis=-1, keepdims=True)
        outs.append(bdot_nn(p, v))
    return jnp.concatenate(outs, axis=1)


def xa_fwd(qx, kv, gq, gk, *, ts=256):
    S, XW = qx.shape
    M = kv.shape[0]
    ts = _tile(S, ts)

    def body(q_ref, kv_ref, gq_ref, gk_ref, o_ref):
        o_ref[...] = _xa_fn(q_ref[...], kv_ref[...], gq_ref[...], gk_ref[...])

    row = pl.BlockSpec((ts, XW), lambda i: (i, 0))
    gspec = pl.BlockSpec((1, XA_HEAD_DIM), lambda i: (0, 0))
    return pl.pallas_call(
        body, name="xa_fwd", grid=(S // ts,),
        in_specs=[row, pl.BlockSpec((M, 2 * XW), lambda i: (0, 0)), gspec, gspec], out_specs=row,
        out_shape=jax.ShapeDtypeStruct((S, XW), f32), compiler_params=_cp(("parallel",)),
    )(qx, kv, gq, gk)


def xa_bwd(qx, kv, gq, gk, do, *, ts=256):
    S, XW = qx.shape
    M = kv.shape[0]
    ts = _tile(S, ts)

    def body(q_ref, kv_ref, gq_ref, gk_ref, do_ref, dq_ref, dkv_ref, dgq_ref, dgk_ref):
        _, vjp = jax.vjp(_xa_fn, q_ref[...], kv_ref[...], gq_ref[...], gk_ref[...])
        dq, dkv, dgq, dgk = vjp(do_ref[...])
        dq_ref[...] = dq

        @pl.when(pl.program_id(0) == 0)
        def _():
            dkv_ref[...] = jnp.zeros_like(dkv_ref)
            dgq_ref[...] = jnp.zeros_like(dgq_ref)
            dgk_ref[...] = jnp.zeros_like(dgk_ref)

        dkv_ref[...] += dkv
        dgq_ref[...] += dgq
        dgk_ref[...] += dgk

    row = pl.BlockSpec((ts, XW), lambda i: (i, 0))
    gspec = pl.BlockSpec((1, XA_HEAD_DIM), lambda i: (0, 0))
    kvspec = pl.BlockSpec((M, 2 * XW), lambda i: (0, 0))
    gshape = jax.ShapeDtypeStruct((1, XA_HEAD_DIM), f32)
    return pl.pallas_call(
        body, name="xa_bwd", grid=(S // ts,), in_specs=[row, kvspec, gspec, gspec, row],
        out_specs=[row, kvspec, gspec, gspec],
        out_shape=[jax.ShapeDtypeStruct((S, XW), f32), jax.ShapeDtypeStruct((M, 2 * XW), f32), gshape, gshape],
        compiler_params=_cp(("arbitrary",)),
    )(qx, kv, gq, gk, do)


def _s5_prep_fn(a_re, a_im, ldt, bT_re, bT_im, cT_re, cT_im):
    G, P, C = SSM_GROUPS, SSM_STATE, SSM_GROUP
    GP, GC = G * P, G * C
    lg_p, lg_c = P.bit_length() - 1, C.bit_length() - 1
    gi = lax.broadcasted_iota(jnp.int32, (G, GP), 0)
    ci = lax.broadcasted_iota(jnp.int32, (G, GP), 1) >> lg_p
    expand_dt = (gi == ci).astype(f32)
    dte = jnp.dot(jnp.exp(ldt), expand_dt, precision=lax.Precision.HIGHEST, preferred_element_type=f32)
    zr, zi = a_re * dte, a_im * dte
    mag = jnp.exp(zr)
    abr, abi = mag * jnp.cos(zi), mag * jnp.sin(zi)
    nr, ni = abr - 1.0, abi
    den = a_re * a_re + a_im * a_im
    cr = (nr * a_re + ni * a_im) / den
    cim = (ni * a_re - nr * a_im) / den
    bbr = cr * bT_re - cim * bT_im
    bbi = cr * bT_im + cim * bT_re
    rowg = lax.broadcasted_iota(jnp.int32, (GC, GP), 0) >> lg_c
    colg = lax.broadcasted_iota(jnp.int32, (GC, GP), 1) >> lg_p
    diag = rowg == colg

    def expand(t):
        return jnp.where(diag, jnp.broadcast_to(t[None], (G, C, GP)).reshape(GC, GP), 0.0)

    return abr, abi, expand(bbr), expand(bbi), expand(cT_re), expand(-cT_im)


def s5_prep(a_re, a_im, ldt, bT_re, bT_im, cT_re, cT_im):
    GP, GC = SSM_GROUPS * SSM_STATE, SSM_GROUPS * SSM_GROUP

    def body(a_re_ref, a_im_ref, ldt_ref, bTr_ref, bTi_ref, cTr_ref, cTi_ref, abr_ref, abi_ref, B_ref, C_ref):
        abr, abi, Br, Bi, Cr, Ci = _s5_prep_fn(a_re_ref[...], a_im_ref[...], ldt_ref[...], bTr_ref[...],
                                               bTi_ref[...], cTr_ref[...], cTi_ref[...])
        abr_ref[...] = abr
        abi_ref[...] = abi
        B_ref[0] = Br
        B_ref[1] = Bi
        C_ref[0] = Cr
        C_ref[1] = Ci

    vec = jax.ShapeDtypeStruct((1, GP), f32)
    mat = jax.ShapeDtypeStruct((2, GC, GP), f32)
    return pl.pallas_call(body, name="s5_prep", out_shape=[vec, vec, mat, mat], compiler_params=_cp())(
        a_re, a_im, ldt, bT_re, bT_im, cT_re, cT_im)


def s5_prep_bwd(a_re, a_im, ldt, bT_re, bT_im, cT_re, cT_im, dabr, dabi, dB, dC):
    def body(a_re_ref, a_im_ref, ldt_ref, bTr_ref, bTi_ref, cTr_ref, cTi_ref, dabr_ref, dabi_ref, dB_ref, dC_ref,
             *outs):
        _, vjp = jax.vjp(_s5_prep_fn, a_re_ref[...], a_im_ref[...], ldt_ref[...], bTr_ref[...], bTi_ref[...],
                         cTr_ref[...], cTi_ref[...])
        grads = vjp((dabr_ref[...], dabi_ref[...], dB_ref[0], dB_ref[1], dC_ref[0], dC_ref[1]))
        for o_ref, gv in zip(outs, grads):
            o_ref[...] = gv

    ins = (a_re, a_im, ldt, bT_re, bT_im, cT_re, cT_im)
    return pl.pallas_call(body, name="s5_prep_bwd", out_shape=[jax.ShapeDtypeStruct(v.shape, f32) for v in ins],
                          compiler_params=_cp())(*ins, dabr, dabi, dB, dC)


def _cmul(ar, ai, br, bi):
    return ar * br - ai * bi, ar * bi + ai * br


def scan_fwd(bu, abr, abi):
    _, S, N = bu.shape
    n = S // SUBLANES
    shp = (SUBLANES, LANES)

    def body(bu_ref, ar_ref, ai_ref, st_ref, pwr_ref, pwi_ref):
        ar = jnp.broadcast_to(ar_ref[...], shp)
        ai = jnp.broadcast_to(ai_ref[...], shp)
        xr_ref, xi_ref = bu_ref.at[0], bu_ref.at[1]
        sr_ref, si_ref = st_ref.at[0], st_ref.at[1]

        def step(j, c):
            sr, si, pr, pi = c
            rows = pl.ds(j, SUBLANES, stride=n)
            mr, mi = _cmul(ar, ai, sr, si)
            sr, si = mr + xr_ref[rows, :], mi + xi_ref[rows, :]
            sr_ref[rows, :] = sr
            si_ref[rows, :] = si
            pwr_ref[pl.ds(j, 1), :] = pr[0:1]
            pwi_ref[pl.ds(j, 1), :] = pi[0:1]
            npr, npi = _cmul(ar, ai, pr, pi)
            return sr, si, npr, npi

        z = jnp.zeros(shp, f32)
        sr, si, _, _ = lax.fori_loop(0, n, step, (z, z, ar, ai))
        sub = lax.broadcasted_iota(jnp.int32, shp, 0)

        def down(x):
            return jnp.where(sub == 0, 0.0, pltpu.roll(x, 1, 0))

        Pr = jnp.broadcast_to(pwr_ref[pl.ds(n - 1, 1), :], shp)
        Pi = jnp.broadcast_to(pwi_ref[pl.ds(n - 1, 1), :], shp)
        Tr, Ti = down(sr), down(si)
        cr, ci = Tr, Ti
        for _ in range(SUBLANES - 1):
            mr, mi = _cmul(Pr, Pi, down(cr), down(ci))
            cr, ci = Tr + mr, Ti + mi

        def step2(j, _):
            rows = pl.ds(j, SUBLANES, stride=n)
            pr = jnp.broadcast_to(pwr_ref[pl.ds(j, 1), :], shp)
            pi = jnp.broadcast_to(pwi_ref[pl.ds(j, 1), :], shp)
            mr, mi = _cmul(pr, pi, cr, ci)
            sr_ref[rows, :] += mr
            si_ref[rows, :] += mi
            return 0

        lax.fori_loop(0, n, step2, 0)

    blk = pl.BlockSpec((2, S, LANES), lambda t: (0, 0, t))
    vec = pl.BlockSpec((1, LANES), lambda t: (0, t))
    return pl.pallas_call(
        body, name="scan_fwd", grid=(N // LANES,), in_specs=[blk, vec, vec], out_specs=blk,
        out_shape=jax.ShapeDtypeStruct((2, S, N), f32),
        scratch_shapes=[pltpu.VMEM((n, LANES), f32), pltpu.VMEM((n, LANES), f32)],
        compiler_params=_cp(("parallel",)),
    )(bu, abr, abi)


def scan_bwd(G, st, abr, abi):
    _, S, N = G.shape
    n = S // SUBLANES
    shp = (SUBLANES, LANES)

    def body(G_ref, st_ref, ar_ref, ai_ref, g_ref, dar_ref, dai_ref, qwr_ref, qwi_ref):
        ar = jnp.broadcast_to(ar_ref[...], shp)
        ai = jnp.broadcast_to(ai_ref[...], shp)
        nai = -ai
        Gr_ref, Gi_ref = G_ref.at[0], G_ref.at[1]
        sr_ref, si_ref = st_ref.at[0], st_ref.at[1]
        gr_ref, gi_ref = g_ref.at[0], g_ref.at[1]

        def step(jj, c):
            gr, gi, qr, qi = c
            j = n - 1 - jj
            rows = pl.ds(j, SUBLANES, stride=n)
            mr, mi = _cmul(ar, nai, gr, gi)
            gr, gi = mr + Gr_ref[rows, :], mi + Gi_ref[rows, :]
            gr_ref[rows, :] = gr
            gi_ref[rows, :] = gi
            qwr_ref[pl.ds(j, 1), :] = qr[0:1]
            qwi_ref[pl.ds(j, 1), :] = qi[0:1]
            nqr, nqi = _cmul(ar, nai, qr, qi)
            return gr, gi, nqr, nqi

        z = jnp.zeros(shp, f32)
        gr, gi, _, _ = lax.fori_loop(0, n, step, (z, z, ar, nai))
        sub = lax.broadcasted_iota(jnp.int32, shp, 0)

        def up(x):
            return jnp.where(sub == SUBLANES - 1, 0.0, pltpu.roll(x, SUBLANES - 1, 0))

        def down(x):
            return jnp.where(sub == 0, 0.0, pltpu.roll(x, 1, 0))

        Pr = jnp.broadcast_to(qwr_ref[pl.ds(0, 1), :], shp)
        Pi = jnp.broadcast_to(qwi_ref[pl.ds(0, 1), :], shp)
        Tr, Ti = up(gr), up(gi)
        cr, ci = Tr, Ti
        for _ in range(SUBLANES - 1):
            mr, mi = _cmul(Pr, Pi, up(cr), up(ci))
            cr, ci = Tr + mr, Ti + mi

        def fix(j, spr, spi, acc):
            rows = pl.ds(j, SUBLANES, stride=n)
            qr = jnp.broadcast_to(qwr_ref[pl.ds(j, 1), :], shp)
            qi = jnp.broadcast_to(qwi_ref[pl.ds(j, 1), :], shp)
            mr, mi = _cmul(qr, qi, cr, ci)
            gr = gr_ref[rows, :] + mr
            gi = gi_ref[rows, :] + mi
            gr_ref[rows, :] = gr
            gi_ref[rows, :] = gi
            return acc[0] + gr * spr + gi * spi, acc[1] + gi * spr - gr * spi

        last = pl.ds(n - 1, SUBLANES, stride=n)
        acc = fix(0, down(sr_ref[last, :]), down(si_ref[last, :]), (z, z))

        def step2(j, acc):
            prev = pl.ds(j - 1, SUBLANES, stride=n)
            return fix(j, sr_ref[prev, :], si_ref[prev, :], acc)

        acc = lax.fori_loop(1, n, step2, acc)
        dar_ref[...] = jnp.sum(acc[0], axis=0, keepdims=True)
        dai_ref[...] = jnp.sum(acc[1], axis=0, keepdims=True)

    blk = pl.BlockSpec((2, S, LANES), lambda t: (0, 0, t))
    vec = pl.BlockSpec((1, LANES), lambda t: (0, t))
    vshape = jax.ShapeDtypeStruct((1, N), f32)
    return pl.pallas_call(
        body, name="scan_bwd", grid=(N // LANES,), in_specs=[blk, blk, vec, vec], out_specs=[blk, vec, vec],
        out_shape=[jax.ShapeDtypeStruct((2, S, N), f32), vshape, vshape],
        scratch_shapes=[pltpu.VMEM((n, LANES), f32), pltpu.VMEM((n, LANES), f32)],
        compiler_params=_cp(("parallel",)),
    )(G, st, abr, abi)


def _glu_fn(ypre, wglu):
    y = jax.nn.gelu(ypre)
    return y * jax.nn.sigmoid(bdot_nn(y, wglu))


def glu_fwd(ypre0, u, d, wglu, *, ts=512):
    S, W = u.shape
    ts = _tile(S, ts)

    def body(y0_ref, u_ref, d_ref, w_ref, ypre_ref, z_ref, rz_ref):
        ypre = y0_ref[...] + d_ref[...] * u_ref[...]
        z = _glu_fn(ypre, w_ref[...])
        ypre_ref[...] = ypre
        z_ref[...] = z
        rz_ref[...] = lax.rsqrt(jnp.sum(z * z, axis=-1, keepdims=True) * (1.0 / W) + NORM_EPS)

    row = pl.BlockSpec((ts, W), lambda i: (i, 0))
    full = jax.ShapeDtypeStruct((S, W), f32)
    return pl.pallas_call(
        body, name="glu_fwd", grid=(S // ts,),
        in_specs=[row, row, pl.BlockSpec((1, W), lambda i: (0, 0)), pl.BlockSpec((W, W), lambda i: (0, 0))],
        out_specs=[row, row, pl.BlockSpec((ts, 1), lambda i: (i, 0))],
        out_shape=[full, full, jax.ShapeDtypeStruct((S, 1), f32)], compiler_params=_cp(("parallel",)),
    )(ypre0, u, d, wglu)


def glu_bwd(ypre, u, d, wglu, dz, *, ts=512):
    S, W = u.shape
    ts = _tile(S, ts)

    def body(y_ref, u_ref, d_ref, w_ref, dz_ref, dy_ref, du_ref, dw_ref, dd_ref):
        _, vjp = jax.vjp(_glu_fn, y_ref[...], w_ref[...])
        dy, dw = vjp(dz_ref[...])
        dy_ref[...] = dy
        du_ref[...] = d_ref[...] * dy

        @pl.when(pl.program_id(0) == 0)
        def _():
            dw_ref[...] = jnp.zeros_like(dw_ref)
            dd_ref[...] = jnp.zeros_like(dd_ref)

        dw_ref[...] += dw
        dd_ref[...] += jnp.sum(dy * u_ref[...], axis=0, keepdims=True)

    row = pl.BlockSpec((ts, W), lambda i: (i, 0))
    vec = pl.BlockSpec((1, W), lambda i: (0, 0))
    sq = pl.BlockSpec((W, W), lambda i: (0, 0))
    full = jax.ShapeDtypeStruct((S, W), f32)
    return pl.pallas_call(
        body, name="glu_bwd", grid=(S // ts,), in_specs=[row, row, vec, sq, row], out_specs=[row, row, sq, vec],
        out_shape=[full, full, jax.ShapeDtypeStruct((W, W), f32), jax.ShapeDtypeStruct((1, W), f32)],
        compiler_params=_cp(("arbitrary",)),
    )(ypre, u, d, wglu, dz)


def loss_head(y, target, *, ts=512):
    S, D = y.shape
    ts = _tile(S, ts)

    def body(y_ref, t_ref, dy_ref, l_ref):
        err = y_ref[...] - t_ref[...]
        dy_ref[...] = err * (1.0 / D)

        @pl.when(pl.program_id(0) == 0)
        def _():
            l_ref[...] = jnp.zeros_like(l_ref)

        rows = jnp.sum(err * err, axis=1, keepdims=True) * (1.0 / D)
        l_ref[...] += 0.5 * jnp.sum(rows, axis=0, keepdims=True)

    row = pl.BlockSpec((ts, D), lambda i: (i, 0))
    return pl.pallas_call(
        body, name="loss_head", grid=(S // ts,), in_specs=[row, row],
        out_specs=[row, pl.BlockSpec((1, 1), lambda i: (0, 0))],
        out_shape=[jax.ShapeDtypeStruct((S, D), f32), jax.ShapeDtypeStruct((1, 1), f32)],
        compiler_params=_cp(("arbitrary",)),
    )(y, target)


def adamw(w, g, m, v, *, name, tr=256):
    R, C = w.shape
    tr = _row_tile(R, tr)

    def body(w_ref, g_ref, m_ref, v_ref, d_ref, nm_ref, nv_ref):
        gv = g_ref[...]
        nm = ADAM_B1 * m_ref[...] + (1.0 - ADAM_B1) * gv
        nv = ADAM_B2 * v_ref[...] + (1.0 - ADAM_B2) * jnp.square(gv)
        m_hat = nm / (1.0 - ADAM_B1 ** ADAM_STEP)
        v_hat = nv / (1.0 - ADAM_B2 ** ADAM_STEP)
        d_ref[...] = -ADAM_LR * (m_hat / (jnp.sqrt(v_hat) + ADAM_EPS) + ADAM_WD * w_ref[...])
        nm_ref[...] = nm
        nv_ref[...] = nv

    row = pl.BlockSpec((tr, C), lambda i: (i, 0))
    full = jax.ShapeDtypeStruct((R, C), f32)
    return pl.pallas_call(
        body, name=name, grid=(R // tr,), in_specs=[row] * 4, out_specs=[row] * 3, out_shape=[full] * 3,
        compiler_params=_cp(("parallel",)),
    )(w, g, m, v)


def add_half(g4, recv, c, *, name, tr=256):
    _, _, Rh, C = g4.shape
    tr = _row_tile(Rh, tr)

    def body(c_ref, a_ref, b_ref, o_ref):
        o_ref[...] = a_ref[...] + b_ref[...]

    grid_spec = pltpu.PrefetchScalarGridSpec(
        num_scalar_prefetch=1, grid=(N_CHIPS, Rh // tr),
        in_specs=[pl.BlockSpec((None, None, tr, C), lambda k, i, c_ref: (k, c_ref[0], i, 0)),
                  pl.BlockSpec((None, tr, C), lambda k, i, c_ref: (k, i, 0))],
        out_specs=pl.BlockSpec((None, tr, C), lambda k, i, c_ref: (k, i, 0)))
    return pl.pallas_call(body, name=name, grid_spec=grid_spec, out_shape=jax.ShapeDtypeStruct(recv.shape, f32),
                          compiler_params=_cp(("parallel", "parallel")))(c, g4, recv)


def sum_chips(p4, *, name, tr=256):
    _, Rh, C = p4.shape
    tr = _row_tile(Rh, tr)

    def body(a_ref, b_ref, c_ref, d_ref, o_ref):
        o_ref[...] = ((a_ref[...] + b_ref[...]) + c_ref[...]) + d_ref[...]

    spec = lambda k: pl.BlockSpec((None, tr, C), lambda i: (k, i, 0))
    return pl.pallas_call(
        body, name=name, grid=(Rh // tr,), in_specs=[spec(0), spec(1), spec(2), spec(3)],
        out_specs=pl.BlockSpec((tr, C), lambda i: (i, 0)), out_shape=jax.ShapeDtypeStruct((Rh, C), f32),
        compiler_params=_cp(("parallel",)),
    )(p4, p4, p4, p4)


def _place():
    return lax.axis_index("x"), lax.axis_index("y"), lax.axis_index("c")


def _other_chips(x, y):
    return [(1 - x, y), (x, 1 - y), (1 - x, 1 - y)]


def allgather_chips(arrs, *, name):
    n = len(arrs)

    def body(*refs):
        ins, outs = refs[:n], refs[n:2 * n]
        send, recv, loc = refs[2 * n:]
        x, y, c = _place()
        me = 2 * x + y
        peers = _other_chips(x, y)
        started = []
        for a in range(n):
            own = pltpu.make_async_copy(ins[a], outs[a].at[me], loc.at[a])
            own.start()
            started.append(own)
        for a in range(n):
            for p, (px, py) in enumerate(peers):
                cp = pltpu.make_async_remote_copy(
                    src_ref=ins[a], dst_ref=outs[a].at[me], send_sem=send.at[3 * a + p], recv_sem=recv.at[3 * a + p],
                    device_id=(px, py, c), device_id_type=MESH)
                cp.start()
        for a in range(n):
            for p, (px, py) in enumerate(peers):
                pltpu.make_async_remote_copy(
                    src_ref=ins[a], dst_ref=outs[a].at[2 * px + py], send_sem=send.at[3 * a + p],
                    recv_sem=recv.at[3 * a + p], device_id=(px, py, c), device_id_type=MESH).wait()
        for own in started:
            own.wait()

    return pl.pallas_call(
        body, name=name, in_specs=[ANY] * n, out_specs=[ANY] * n,
        out_shape=[jax.ShapeDtypeStruct((N_CHIPS,) + a.shape, a.dtype) for a in arrs],
        scratch_shapes=[pltpu.SemaphoreType.DMA((3 * n,)), pltpu.SemaphoreType.DMA((3 * n,)),
                        pltpu.SemaphoreType.DMA((n,))],
        compiler_params=pltpu.CompilerParams(has_side_effects=True),
    )(*arrs)


def sibling_swap(arrs, *, half, name):
    n = len(arrs)

    def body(*refs):
        ins, outs = refs[:n], refs[n:2 * n]
        send, recv = refs[2 * n:]
        x, y, c = _place()
        cps = []
        for a in range(n):
            src = ins[a].at[:, 1 - c] if half else ins[a]
            cp = pltpu.make_async_remote_copy(src_ref=src, dst_ref=outs[a], send_sem=send.at[a], recv_sem=recv.at[a],
                                              device_id=(x, y, 1 - c), device_id_type=MESH)
            cp.start()
            cps.append(cp)
        for cp in cps:
            cp.wait()

    def oshape(a):
        return jax.ShapeDtypeStruct((a.shape[0],) + a.shape[2:] if half else a.shape, a.dtype)

    return pl.pallas_call(
        body, name=name, in_specs=[ANY] * n, out_specs=[ANY] * n, out_shape=[oshape(a) for a in arrs],
        scratch_shapes=[pltpu.SemaphoreType.DMA((n,)), pltpu.SemaphoreType.DMA((n,))],
        compiler_params=pltpu.CompilerParams(has_side_effects=True),
    )(*arrs)


def chip_scatter(arrs, *, name):
    n = len(arrs)

    def body(*refs):
        ins, outs = refs[:n], refs[n:2 * n]
        send, recv, loc = refs[2 * n:]
        x, y, c = _place()
        me = 2 * x + y
        peers = _other_chips(x, y)
        started = []
        for a in range(n):
            own = pltpu.make_async_copy(ins[a].at[me], outs[a].at[me], loc.at[a])
            own.start()
            started.append(own)
        for a in range(n):
            for p, (px, py) in enumerate(peers):
                pltpu.make_async_remote_copy(
                    src_ref=ins[a].at[2 * px + py], dst_ref=outs[a].at[me], send_sem=send.at[3 * a + p],
                    recv_sem=recv.at[3 * a + p], device_id=(px, py, c), device_id_type=MESH).start()
        for a in range(n):
            for p, (px, py) in enumerate(peers):
                pltpu.make_async_remote_copy(
                    src_ref=ins[a].at[2 * px + py], dst_ref=outs[a].at[2 * px + py], send_sem=send.at[3 * a + p],
                    recv_sem=recv.at[3 * a + p], device_id=(px, py, c), device_id_type=MESH).wait()
        for own in started:
            own.wait()

    return pl.pallas_call(
        body, name=name, in_specs=[ANY] * n, out_specs=[ANY] * n,
        out_shape=[jax.ShapeDtypeStruct(a.shape, a.dtype) for a in arrs],
        scratch_shapes=[pltpu.SemaphoreType.DMA((3 * n,)), pltpu.SemaphoreType.DMA((3 * n,)),
                        pltpu.SemaphoreType.DMA((n,))],
        compiler_params=pltpu.CompilerParams(has_side_effects=True),
    )(*arrs)


def _pad_cols(w):
    K = w.shape[0]
    w = w.reshape(K, -1, SB_HEAD_DIM)
    return jnp.pad(w, ((0, 0), (0, 0), (0, LANES - SB_HEAD_DIM))).reshape(K, -1)


def _unpad_cols(w):
    K = w.shape[0]
    return w.reshape(K, -1, LANES)[:, :, :SB_HEAD_DIM].reshape(K, -1)


def _pad_rows(w):
    N = w.shape[1]
    w = w.reshape(-1, SB_HEAD_DIM, N)
    return jnp.pad(w, ((0, 0), (0, LANES - SB_HEAD_DIM), (0, 0))).reshape(-1, N)


def _unpad_rows(w):
    N = w.shape[1]
    return w.reshape(-1, LANES, N)[:, :SB_HEAD_DIM, :].reshape(-1, N)


_PACK_ROWS = N_CHIPS * 2 * SUBLANES


def _pack(arrs):
    flat = jnp.concatenate([a.reshape(-1) for a in arrs])
    rows = -(-flat.shape[0] // LANES)
    rows = -(-rows // _PACK_ROWS) * _PACK_ROWS
    return jnp.pad(flat, (0, rows * LANES - flat.shape[0])).reshape(rows, LANES)


def _unpack(buf, shapes):
    flat = buf.reshape(-1)
    out, pos = [], 0
    for shp in shapes:
        size = 1
        for d in shp:
            size *= d
        out.append(flat[pos:pos + size].reshape(shp))
        pos += size
    return out


BIG = ("w_in", "ssm_w_glu", "w_out", "xa_w_q", "xa_w_kv", "xa_w_o", "w_up", "w_down")
SMALL = ("g_mix", "ssm_a_re", "ssm_a_im", "ssm_log_dt", "ssm_b_re", "ssm_b_im", "ssm_c_re", "ssm_c_im", "ssm_d",
         "sb_g_q", "sb_g_k", "g_out_ssm", "g_out_sb", "g_xa", "g_mem", "xa_g_q", "xa_g_k", "g_mlp")
WEIGHTS = ("g_mix", "w_in", "ssm_a_re", "ssm_a_im", "ssm_log_dt", "ssm_b_re", "ssm_b_im", "ssm_c_re", "ssm_c_im",
           "ssm_d", "ssm_w_glu", "sb_g_q", "sb_g_k", "g_out_ssm", "g_out_sb", "w_out", "g_xa", "g_mem", "xa_w_q",
           "xa_w_kv", "xa_g_q", "xa_g_k", "xa_w_o", "g_mlp", "w_up", "w_down")


def kernel(x, mem, g_mix, w_in, ssm_a_re, ssm_a_im, ssm_log_dt, ssm_b_re, ssm_b_im, ssm_c_re, ssm_c_im, ssm_d, ssm_w_glu, sb_g_q, sb_g_k, g_out_ssm, g_out_sb, w_out, g_xa, g_mem, xa_w_q, xa_w_kv, xa_g_q, xa_g_k, xa_w_o, g_mlp, w_up, w_down, loss_target, m_g_mix, m_w_in, m_ssm_a_re, m_ssm_a_im, m_ssm_log_dt, m_ssm_b_re, m_ssm_b_im, m_ssm_c_re, m_ssm_c_im, m_ssm_d, m_ssm_w_glu, m_sb_g_q, m_sb_g_k, m_g_out_ssm, m_g_out_sb, m_w_out, m_g_xa, m_g_mem, m_xa_w_q, m_xa_w_kv, m_xa_g_q, m_xa_g_k, m_xa_w_o, m_g_mlp, m_w_up, m_w_down, v_g_mix, v_w_in, v_ssm_a_re, v_ssm_a_im, v_ssm_log_dt, v_ssm_b_re, v_ssm_b_im, v_ssm_c_re, v_ssm_c_im, v_ssm_d, v_ssm_w_glu, v_sb_g_q, v_sb_g_k, v_g_out_ssm, v_g_out_sb, v_w_out, v_g_xa, v_g_mem, v_xa_w_q, v_xa_w_kv, v_xa_g_q, v_xa_g_k, v_xa_w_o, v_g_mlp, v_w_up, v_w_down):
    env = dict(locals())
    W = {n: env[n] for n in WEIGHTS}
    M1 = {n: env["m_" + n] for n in WEIGHTS}
    V2 = {n: env["v_" + n] for n in WEIGHTS}
    xs, mems, tgt = x[0], mem[0], loss_target[0]
    S, D = xs.shape
    G, P, C = SSM_GROUPS, SSM_STATE, SSM_GROUP
    GP = G * P
    SBW = SB_HEADS * SB_HEAD_DIM
    c_idx = lax.axis_index("c")

    gath = allgather_chips([W[n][0].astype(bf16) for n in BIG], name="gather_weights")
    g_in, g_glu, g_out, g_xq, g_xkv, g_xo, g_up, g_down = gath
    Wu = g_in[0]
    Wqkv = jnp.concatenate([_pad_cols(g_in[1]), _pad_cols(g_in[2]), _pad_cols(g_in[3])], axis=1)
    Wglu = g_glu.reshape(-1, g_glu.shape[-1])
    Wout = g_out.reshape(-1, g_out.shape[-1])
    Wo_ssm, Wo_sb = Wout[:SBW], _pad_rows(Wout[SBW:])
    Wxq = g_xq.reshape(-1, g_xq.shape[-1])
    Wxkv = g_xkv.reshape(-1, g_xkv.shape[-1])
    Wxo = g_xo.transpose(1, 0, 2).reshape(g_xo.shape[1], -1)
    Wup = g_up.transpose(1, 0, 2).reshape(g_up.shape[1], -1)
    Wdown = g_down.reshape(-1, g_down.shape[-1])
    gq_pad, gk_pad = _pad_cols(sb_g_q), _pad_cols(sb_g_k)
    gosb_pad = _pad_cols(g_out_sb)
    a_re, a_im = ssm_a_re.reshape(1, GP), ssm_a_im.reshape(1, GP)
    bT_re = ssm_b_re[0].transpose(2, 0, 1).reshape(C, GP)
    bT_im = ssm_b_im[0].transpose(2, 0, 1).reshape(C, GP)
    cT_re = ssm_c_re[0].transpose(1, 0, 2).reshape(C, GP)
    cT_im = ssm_c_im[0].transpose(1, 0, 2).reshape(C, GP)
    s5_in = (a_re, a_im, ssm_log_dt, bT_re, bT_im, cT_re, cT_im)

    r0 = rms_stats(xs, D, name="rms_x")
    u = mm(xs, Wu, mode="nn", name="proj_u", pro="rms", r=r0, g=g_mix)
    qkv = mm(xs, Wqkv, mode="nn", name="proj_qkv", pro="rms", r=r0, g=g_mix, tn=1024, tk=1024)
    qn, kn, vb = qkv_prep(qkv, gq_pad, gk_pad)
    o = sb_fwd(qn, kn, vb)
    abr, abi, Bm, Cm = s5_prep(*s5_in)
    bu = mm(u, Bm, mode="nn", name="s5_bu", b_shards=2, out_shards=2, tn=1024)
    st = scan_fwd(bu, abr, abi)
    ypre0 = mm(st, Cm, mode="nt", name="s5_y", a_shards=2, b_shards=2, tk=1024)
    ypre, z, rz = glu_fwd(ypre0, u, ssm_d, Wglu)
    ro = rms_stats(o, SBW, name="rms_o")
    x1a = mm(z, Wo_ssm, mode="nn", name="out_ssm", pro="rms", r=rz, g=g_out_ssm, epi="add", aux=xs, tn=1024)
    x1 = mm(o, Wo_sb, mode="nn", name="out_sb", pro="rms", r=ro, g=gosb_pad, epi="add", aux=x1a, tn=1024, tk=1024)
    r1 = rms_stats(x1, D, name="rms_x1")
    qx = mm(x1, Wxq, mode="nn", name="xa_q", pro="rms", r=r1, g=g_xa, tk=1024)
    rm = rms_stats(mems, D, name="rms_mem")
    kv = mm(mems, Wxkv, mode="nn", name="xa_kv", pro="rms", r=rm, g=g_mem, tn=1024, tk=1024)
    ox = xa_fwd(qx, kv, xa_g_q, xa_g_k)
    x2 = mm(ox, Wxo, mode="nn", name="xa_o", epi="add", aux=x1, tn=1024)
    r2 = rms_stats(x2, D, name="rms_x2")
    act = mm(x2, Wup, mode="nn", name="mlp_up", pro="rms", r=r2, g=g_mlp, tn=1024, tk=1024)
    x3 = mm(act, Wdown, mode="nn", name="mlp_down", pro="relu2", epi="add", aux=x2, tn=1024, tk=1024)
    dx3, loss_part = loss_head(x3, tgt)
    loss = lax.psum(loss_part[0, 0], ("x", "y", "c"))

    dact = mm(dx3, Wdown, mode="nt", name="d_act", epi="mul2relu", aux=act, tn=1024, tk=1024)
    dWdown = mm(act, dx3, mode="tn", name="dw_down", pro="relu2", tn=1024, tk=1024)
    dWup = mm(x2, dact, mode="tn", name="dw_up", pro="rms", r=r2, g=g_mlp, out_shards=N_CHIPS, tn=1024, tk=1024)
    dh2 = mm(dact, Wup, mode="nt", name="d_h2", tn=1024, tk=1024)
    dx2, dg_mlp = rms_bwd(x2, g_mlp, dh2, dx3, D, name="rms_bwd_mlp")
    dox = mm(dx2, Wxo, mode="nt", name="d_ox", tk=1024)
    dWxo = mm(ox, dx2, mode="tn", name="dw_xo", out_shards=N_CHIPS, tk=1024)
    dqx, dkv, dg_xq, dg_xk = xa_bwd(qx, kv, xa_g_q, xa_g_k, dox)
    dWxq = mm(x1, dqx, mode="tn", name="dw_xq", pro="rms", r=r1, g=g_xa, tk=1024)
    dh1 = mm(dqx, Wxq, mode="nt", name="d_h1", tn=1024)
    dx1, dg_xa = rms_bwd(x1, g_xa, dh1, dx2, D, name="rms_bwd_xa")
    dWxkv = mm(mems, dkv, mode="tn", name="dw_xkv", pro="rms", r=rm, g=g_mem, tn=1024)
    dmemn = mm(dkv, Wxkv, mode="nt", name="d_memn", tn=1024, tk=1024)
    _, dg_mem = rms_bwd(mems, g_mem, dmemn, None, D, name="rms_bwd_mem")
    dyn_ssm = mm(dx1, Wo_ssm, mode="nt", name="d_yn_ssm", tk=1024)
    dyn_sb = mm(dx1, Wo_sb, mode="nt", name="d_yn_sb", tn=1024, tk=1024)
    dWo_ssm = mm(z, dx1, mode="tn", name="dw_out_ssm", pro="rms", r=rz, g=g_out_ssm, tn=1024, tk=1024)
    dWo_sb = mm(o, dx1, mode="tn", name="dw_out_sb", pro="rms", r=ro, g=gosb_pad, tn=1024, tk=1024)
    dz, dg_os = rms_bwd(z, g_out_ssm, dyn_ssm, None, SBW, name="rms_bwd_ssm")
    do, dg_osb = rms_bwd(o, gosb_pad, dyn_sb, None, SBW, name="rms_bwd_sb")
    dqn, dkn, dv = sb_bwd(qn, kn, vb, do)
    dqkv, dg_q, dg_k = qkv_bwd(qkv, gq_pad, gk_pad, dqn, dkn, dv)
    dypre, du_skip, dWglu, dd = glu_bwd(ypre, u, ssm_d, Wglu, dz)
    dst = mm(dypre, Cm, mode="nn", name="d_states", b_shards=2, out_shards=2, tn=1024)
    dCm = mm(dypre, st, mode="tn", name="d_cmat", b_shards=2, out_shards=2, tn=1024, tk=1024)
    gst, dabr, dabi = scan_bwd(dst, st, abr, abi)
    dBm = mm(u, gst, mode="tn", name="d_bmat", b_shards=2, out_shards=2, tn=1024, tk=1024)
    du = mm(gst, Bm, mode="nt", name="d_u", a_shards=2, b_shards=2, epi="add", aux=du_skip, tk=1024)
    s5_g = s5_prep_bwd(*s5_in, dabr, dabi, dBm, dCm)
    dWu = mm(xs, du, mode="tn", name="dw_u", pro="rms", r=r0, g=g_mix, tk=1024)
    dWqkv = mm(xs, dqkv, mode="tn", name="dw_qkv", pro="rms", r=r0, g=g_mix, tn=1024, tk=1024)
    dh0a = mm(du, Wu, mode="nt", name="d_h0_u", tn=1024)
    dh0 = mm(dqkv, Wqkv, mode="nt", name="d_h0_qkv", epi="add", aux=dh0a, tn=1024, tk=1024)
    dx, dg_mix = rms_bwd(xs, g_mix, dh0, dx1, D, name="rms_bwd_mix")

    HW = SB_HEADS * LANES
    big_g = {
        "w_in": jnp.stack([dWu, _unpad_cols(dWqkv[:, :HW]), _unpad_cols(dWqkv[:, HW:2 * HW]),
                           _unpad_cols(dWqkv[:, 2 * HW:])]),
        "ssm_w_glu": dWglu.reshape(N_CHIPS, -1, dWglu.shape[1]),
        "w_out": jnp.concatenate([dWo_ssm, _unpad_rows(dWo_sb)]).reshape(N_CHIPS, -1, D),
        "xa_w_q": dWxq.reshape(N_CHIPS, -1, dWxq.shape[1]),
        "xa_w_kv": dWxkv.reshape(N_CHIPS, -1, dWxkv.shape[1]),
        "xa_w_o": dWxo,
        "w_up": dWup,
        "w_down": dWdown.reshape(N_CHIPS, -1, D),
    }
    da_re, da_im, dldt, dbT_re, dbT_im, dcT_re, dcT_im = s5_g
    small_g = {
        "g_mix": dg_mix, "ssm_a_re": da_re, "ssm_a_im": da_im, "ssm_log_dt": dldt,
        "ssm_b_re": dbT_re.reshape(C, G, P).transpose(1, 2, 0), "ssm_b_im": dbT_im.reshape(C, G, P).transpose(1, 2, 0),
        "ssm_c_re": dcT_re.reshape(C, G, P).transpose(1, 0, 2), "ssm_c_im": dcT_im.reshape(C, G, P).transpose(1, 0, 2),
        "ssm_d": dd, "sb_g_q": dg_q[:, :SB_HEAD_DIM], "sb_g_k": dg_k[:, :SB_HEAD_DIM], "g_out_ssm": dg_os,
        "g_out_sb": _unpad_cols(dg_osb), "g_xa": dg_xa, "g_mem": dg_mem, "xa_g_q": dg_xq, "xa_g_k": dg_xk,
        "g_mlp": dg_mlp,
    }
    packed = _pack([small_g[n] for n in SMALL])
    full_g = [big_g[n] for n in BIG] + [packed.reshape(N_CHIPS, -1, LANES)]

    g4 = [g.reshape(N_CHIPS, 2, g.shape[1] // 2, g.shape[2]) for g in full_g]
    from_sib = sibling_swap(g4, half=True, name="grad_to_sibling")
    c_arr = c_idx.astype(jnp.int32).reshape(1)
    names = list(BIG) + ["small"]
    pair = [add_half(a, b, c_arr, name="add_sibling_" + n) for a, b, n in zip(g4, from_sib, names)]
    parts = chip_scatter(pair, name="grad_to_chips")
    mine = [sum_chips(p, name="sum_chips_" + n) for p, n in zip(parts, names)]
    other = sibling_swap(mine, half=False, name="grad_half_to_sibling")
    shard = [jnp.where(c_idx == 0, jnp.concatenate([a, b]), jnp.concatenate([b, a])) for a, b in zip(mine, other)]
    small_all = allgather_chips([shard[-1]], name="gather_small")[0]
    small_red = small_all.reshape(-1, LANES)

    out = {}
    for n, gs in zip(BIG, shard[:-1]):
        shp = W[n].shape
        w2, m2, v2 = (t.reshape(gs.shape) for t in (W[n], M1[n], V2[n]))
        d, nm, nv = adamw(w2, gs, m2, v2, name="adamw_" + n)
        out[n] = tuple(t.reshape(shp) for t in (gs, d, nm, nv))
    shapes = [W[n].shape for n in SMALL]
    d, nm, nv = adamw(_pack([W[n] for n in SMALL]), small_red, _pack([M1[n] for n in SMALL]),
                      _pack([V2[n] for n in SMALL]), name="adamw_small")
    for n, gs, dd_, mm_, vv_ in zip(SMALL, _unpack(small_red, shapes), _unpack(d, shapes), _unpack(nm, shapes),
                                    _unpack(nv, shapes)):
        out[n] = (gs, dd_, mm_, vv_)
    res = [loss, dx[None]]
    for kind in range(4):
        res += [out[n][kind] for n in WEIGHTS]
    return tuple(res)
```

```python
import jax
import jax.numpy as jnp
from jax import lax
from jax.experimental import pallas as pl
from jax.experimental.pallas import tpu as pltpu

f32 = jnp.float32
bf16 = jnp.bfloat16

NORM_EPS = 1e-6
SSM_GROUPS = 32
SSM_GROUP = 16
SSM_STATE = 64
SB_HEADS = 8
SB_HEAD_DIM = 64
XA_HEADS = 4
XA_HEAD_DIM = 128
LANES = 128
SUBLANES = 8
N_CHIPS = 4
ADAM_LR = 0.001
ADAM_B1 = 0.9
ADAM_B2 = 0.999
ADAM_EPS = 1e-08
ADAM_WD = 0.01
ADAM_STEP = 10
VMEM_LIMIT = 56 * 1024 * 1024
MESH = pl.DeviceIdType.MESH
ANY = pl.BlockSpec(memory_space=pl.ANY)


def _cp(sem=None):
    return pltpu.CompilerParams(dimension_semantics=sem, vmem_limit_bytes=VMEM_LIMIT)


def _tile(n, pref):
    if n <= pref:
        return n
    t = (pref // LANES) * LANES
    while t > LANES and n % t:
        t -= LANES
    assert n % t == 0, (n, pref)
    return t


def _row_tile(n, pref):
    if n <= pref:
        return n
    t = (pref // SUBLANES) * SUBLANES
    while n % t:
        t -= SUBLANES
    return t


def _dot(a, b, dims):
    return lax.dot_general(a.astype(bf16), b.astype(bf16), (dims, ((), ())), preferred_element_type=f32)


_NN = ((1,), (0,))
_NT = ((1,), (1,))
_TN = ((0,), (0,))


@jax.custom_vjp
def bdot_nn(a, b):
    return _dot(a, b, _NN)


def _bdot_nn_fwd(a, b):
    return _dot(a, b, _NN), (a, b)


def _bdot_nn_bwd(res, g):
    a, b = res
    return _dot(g, b, _NT), _dot(a, g, _TN)


bdot_nn.defvjp(_bdot_nn_fwd, _bdot_nn_bwd)


@jax.custom_vjp
def bdot_nt(a, b):
    return _dot(a, b, _NT)


def _bdot_nt_fwd(a, b):
    return _dot(a, b, _NT), (a, b)


def _bdot_nt_bwd(res, g):
    a, b = res
    return _dot(g, b, _NN), _dot(g, a, _TN)


bdot_nt.defvjp(_bdot_nt_fwd, _bdot_nt_bwd)


def _rms(x, g, denom):
    r = lax.rsqrt(jnp.sum(x * x, axis=-1, keepdims=True) * (1.0 / denom) + NORM_EPS)
    return x * r * g


def _opspec(block, row_of, col_of, shards, ncol_tiles):
    if shards == 1:
        return pl.BlockSpec(block, lambda i, j, k: (row_of(i, j, k), col_of(i, j, k)))
    per = ncol_tiles // shards
    return pl.BlockSpec((None,) + block,
                        lambda i, j, k: (col_of(i, j, k) // per, row_of(i, j, k), col_of(i, j, k) % per))


def mm(a, b, *, mode, name, tm=512, tn=512, tk=512, pro="none", r=None, g=None, epi="none", aux=None,
       out_dtype=f32, a_shards=1, b_shards=1, out_shards=1):
    ar, ac = a.shape[-2], a.shape[-1] * a_shards
    br, bc = b.shape[-2], b.shape[-1] * b_shards
    if mode == "nn":
        M, K, N = ar, ac, bc
        assert br == K
    elif mode == "nt":
        M, K, N = ar, ac, br
        assert bc == K
    else:
        M, K, N = ac, ar, bc
        assert br == K
    tm, tn, tk = _tile(M, tm), _tile(N, tn), _tile(K, tk)
    if a_shards > 1:
        if mode == "tn":
            tm = _tile(ac // a_shards, tm)
        else:
            tk = _tile(ac // a_shards, tk)
    if b_shards > 1:
        if mode == "nt":
            tk = _tile(bc // b_shards, tk)
        else:
            tn = _tile(bc // b_shards, tn)
    if out_shards > 1:
        tn = _tile(N // out_shards, tn)
    nm, nn_, nk = M // tm, N // tn, K // tk
    I = lambda i, j, k: i
    J = lambda i, j, k: j
    Kk = lambda i, j, k: k
    if mode == "nn":
        a_spec = _opspec((tm, tk), I, Kk, a_shards, nk)
        b_spec = _opspec((tk, tn), Kk, J, b_shards, nn_)
        dims = _NN
    elif mode == "nt":
        a_spec = _opspec((tm, tk), I, Kk, a_shards, nk)
        b_spec = _opspec((tn, tk), J, Kk, b_shards, nk)
        dims = _NT
    else:
        a_spec = _opspec((tk, tm), Kk, I, a_shards, nm)
        b_spec = _opspec((tk, tn), Kk, J, b_shards, nn_)
        dims = _TN
    in_specs = [a_spec, b_spec]
    args = [a, b]
    if pro == "rms":
        if mode == "tn":
            in_specs += [pl.BlockSpec((tk, 1), lambda i, j, k: (k, 0)), pl.BlockSpec((1, tm), lambda i, j, k: (0, i))]
        else:
            in_specs += [pl.BlockSpec((tm, 1), lambda i, j, k: (i, 0)), pl.BlockSpec((1, tk), lambda i, j, k: (0, k))]
        args += [r, g]
    if epi != "none":
        in_specs.append(pl.BlockSpec((tm, tn), lambda i, j, k: (i, j)))
        args.append(aux)
    if out_shards == 1:
        out_spec = pl.BlockSpec((tm, tn), lambda i, j, k: (i, j))
        out_shape = jax.ShapeDtypeStruct((M, N), out_dtype)
    else:
        per = nn_ // out_shards
        out_spec = pl.BlockSpec((None, tm, tn), lambda i, j, k: (j // per, i, j % per))
        out_shape = jax.ShapeDtypeStruct((out_shards, M, N // out_shards), out_dtype)

    def body(*refs):
        a_ref, b_ref = refs[0], refs[1]
        pos = 2
        if pro == "rms":
            r_ref, g_ref = refs[2], refs[3]
            pos = 4
        if epi != "none":
            aux_ref = refs[pos]
            pos += 1
        o_ref, acc_ref = refs[pos], refs[pos + 1]
        k = pl.program_id(2)

        @pl.when(k == 0)
        def _():
            acc_ref[...] = jnp.zeros_like(acc_ref)

        av = a_ref[...]
        if pro == "rms":
            av = av.astype(f32) * r_ref[...] * g_ref[...]
        elif pro == "relu2":
            av = jnp.square(jnp.maximum(av.astype(f32), 0.0))
        acc_ref[...] += _dot(av, b_ref[...], dims)

        @pl.when(k == nk - 1)
        def _():
            res = acc_ref[...]
            if epi == "add":
                res = res + aux_ref[...].astype(f32)
            elif epi == "mul2relu":
                res = res * (2.0 * jnp.maximum(aux_ref[...].astype(f32), 0.0))
            o_ref[...] = res.astype(out_dtype)

    return pl.pallas_call(
        body, name=name, grid=(nm, nn_, nk), in_specs=in_specs, out_specs=out_spec, out_shape=out_shape,
        scratch_shapes=[pltpu.VMEM((tm, tn), f32)],
        compiler_params=_cp(("parallel", "parallel", "arbitrary")),
    )(*args)


def rms_stats(x, denom, *, name, ts=512):
    S, D = x.shape
    ts = _tile(S, ts)

    def body(x_ref, r_ref):
        xv = x_ref[...].astype(f32)
        r_ref[...] = lax.rsqrt(jnp.sum(xv * xv, axis=-1, keepdims=True) * (1.0 / denom) + NORM_EPS)

    return pl.pallas_call(
        body, name=name, grid=(S // ts,), in_specs=[pl.BlockSpec((ts, D), lambda i: (i, 0))],
        out_specs=pl.BlockSpec((ts, 1), lambda i: (i, 0)), out_shape=jax.ShapeDtypeStruct((S, 1), f32),
        compiler_params=_cp(("parallel",)),
    )(x)


def rms_bwd(x, g, dy, res, denom, *, name, ts=256):
    S, D = x.shape
    ts = _tile(S, ts)
    has_res = res is not None

    def body(*refs):
        if has_res:
            x_ref, g_ref, dy_ref, res_ref, dx_ref, dg_ref = refs
        else:
            x_ref, g_ref, dy_ref, dx_ref, dg_ref = refs
        _, vjp = jax.vjp(lambda xv, gv: _rms(xv, gv, denom), x_ref[...], g_ref[...])
        dx, dg = vjp(dy_ref[...])
        if has_res:
            dx = dx + res_ref[...]
        dx_ref[...] = dx

        @pl.when(pl.program_id(0) == 0)
        def _():
            dg_ref[...] = jnp.zeros_like(dg_ref)

        dg_ref[...] += dg

    row = pl.BlockSpec((ts, D), lambda i: (i, 0))
    vec = pl.BlockSpec((1, D), lambda i: (0, 0))
    in_specs = [row, vec, row] + ([row] if has_res else [])
    args = [x, g, dy] + ([res] if has_res else [])
    return pl.pallas_call(
        body, name=name, grid=(S // ts,), in_specs=in_specs, out_specs=[row, vec],
        out_shape=[jax.ShapeDtypeStruct((S, D), f32), jax.ShapeDtypeStruct((1, D), f32)],
        compiler_params=_cp(("arbitrary",)),
    )(*args)


LOG2E = 1.4426950408889634
LN2 = 0.6931471805599453


def _qk_fn(q, k, gq, gk):
    qs, ks = [], []
    for h in range(SB_HEADS):
        sl = slice(h * LANES, (h + 1) * LANES)
        qs.append(_rms(q[:, sl], gq, SB_HEAD_DIM) * (SB_HEAD_DIM ** -0.5 * LOG2E))
        ks.append(_rms(k[:, sl], gk, SB_HEAD_DIM))
    return jnp.concatenate(qs, axis=1), jnp.concatenate(ks, axis=1)


def qkv_prep(qkv, gq, gk, *, ts=256):
    S = qkv.shape[0]
    W = SB_HEADS * LANES
    ts = _tile(S, ts)

    def body(q_ref, k_ref, v_ref, gq_ref, gk_ref, qn_ref, kn_ref, vb_ref):
        qn, kn = _qk_fn(q_ref[...], k_ref[...], gq_ref[...], gk_ref[...])
        qn_ref[...] = qn.astype(bf16)
        kn_ref[...] = kn.astype(bf16)
        vb_ref[...] = v_ref[...].astype(bf16)

    out = jax.ShapeDtypeStruct((S, W), bf16)
    gspec = pl.BlockSpec((1, LANES), lambda i: (0, 0))
    ospec = pl.BlockSpec((ts, W), lambda i: (i, 0))
    col = lambda c: pl.BlockSpec((ts, W), lambda i: (i, c))
    return pl.pallas_call(
        body, name="qkv_prep", grid=(S // ts,), in_specs=[col(0), col(1), col(2), gspec, gspec],
        out_specs=[ospec, ospec, ospec], out_shape=[out, out, out], compiler_params=_cp(("parallel",)),
    )(qkv, qkv, qkv, gq, gk)


def qkv_bwd(qkv, gq, gk, dqn, dkn, dv, *, ts=256):
    S = qkv.shape[0]
    W = SB_HEADS * LANES
    ts = _tile(S, ts)

    def body(q_ref, k_ref, gq_ref, gk_ref, dqn_ref, dkn_ref, dv_ref, o_ref, dgq_ref, dgk_ref):
        _, vjp = jax.vjp(_qk_fn, q_ref[...], k_ref[...], gq_ref[...], gk_ref[...])
        dq, dk, dgq, dgk = vjp((dqn_ref[...] * LN2, dkn_ref[...] * LN2))
        o_ref[:, 0:W] = dq
        o_ref[:, W:2 * W] = dk
        o_ref[:, 2 * W:3 * W] = dv_ref[...]

        @pl.when(pl.program_id(0) == 0)
        def _():
            dgq_ref[...] = jnp.zeros_like(dgq_ref)
            dgk_ref[...] = jnp.zeros_like(dgk_ref)

        dgq_ref[...] += dgq
        dgk_ref[...] += dgk

    gspec = pl.BlockSpec((1, LANES), lambda i: (0, 0))
    row = pl.BlockSpec((ts, W), lambda i: (i, 0))
    col = lambda c: pl.BlockSpec((ts, W), lambda i: (i, c))
    return pl.pallas_call(
        body, name="qkv_bwd", grid=(S // ts,), in_specs=[col(0), col(1), gspec, gspec, row, row, row],
        out_specs=[pl.BlockSpec((ts, 3 * W), lambda i: (i, 0)), gspec, gspec],
        out_shape=[jax.ShapeDtypeStruct((S, 3 * W), f32), jax.ShapeDtypeStruct((1, LANES), f32),
                   jax.ShapeDtypeStruct((1, LANES), f32)],
        compiler_params=_cp(("arbitrary",)),
    )(qkv, qkv, gq, gk, dqn, dkn, dv)


def _sb_weights(q, ks, R, masked, row, col, UU):
    ls = [_dot(q, k, _NT) for k in ks]
    lbs, lm0s, cats = [], [], []
    for l in ls:
        lp = jnp.log2(1.0 + jnp.exp2(-jnp.abs(l)))
        lb = jnp.minimum(l, 0.0) - lp
        lm = lb - l
        if masked:
            lm = jnp.where(col < row, lm, 0.0)
        hi = lm.astype(bf16)
        lo = (lm - hi.astype(f32)).astype(bf16)
        lbs.append(lb)
        lm0s.append(lm[:, 0:1])
        cats.append(jnp.concatenate([hi, lo], axis=1))
    sums = [_dot(c, UU, _NN) for c in cats]
    ws = []
    for lb, lm0, A in zip(lbs, lm0s, sums):
        A = A + R
        w = jnp.exp2(lb + A)
        if masked:
            w = jnp.where(col < row, w, 0.0)
        R = A[:, 0:1] + lm0
        ws.append(w)
    return lbs, ws, R


def _tri2(tk):
    r = lax.broadcasted_iota(jnp.int32, (2 * tk, tk), 0)
    r = jnp.where(r >= tk, r - tk, r)
    c = lax.broadcasted_iota(jnp.int32, (2 * tk, tk), 1)
    return (r > c).astype(bf16)


def _grouped(n, group, blocks_of, carry):
    rem = n % group
    carry = lax.fori_loop(0, rem, lambda t, c: blocks_of([t], c), carry)
    return lax.fori_loop(0, n // group, lambda p, c: blocks_of([rem + p * group + u for u in range(group)], c), carry)


SB_GROUP = 4


def sb_fwd(qn, kn, vb, *, tq=256):
    S, W = qn.shape
    H = W // LANES
    tq = _tile(S, tq)
    tk = tq
    nq = S // tq

    def body(q_ref, k_ref, v_ref, o_ref):
        i = pl.program_id(1)
        q = q_ref[...]
        row = lax.broadcasted_iota(jnp.int32, (tq, tk), 0)
        col = lax.broadcasted_iota(jnp.int32, (tq, tk), 1)
        UU = _tri2(tk)

        def blocks(js, c, masked):
            rows = [pl.ds(pl.multiple_of(j * tk, tk), tk) for j in js]
            _, ws, R = _sb_weights(q, [k_ref[r, :] for r in rows], c[0], masked, row, col, UU)
            acc = c[1]
            for w, r in zip(ws, rows):
                acc = acc + _dot(w, v_ref[r, :], _NN)
            return R, acc

        c = blocks([i], (jnp.zeros((tq, 1), f32), jnp.zeros((tq, LANES), f32)), True)
        c = _grouped(i, SB_GROUP, lambda ts, c: blocks([i - 1 - t for t in ts], c, False), c)
        o_ref[...] = c[1]

    qspec = pl.BlockSpec((tq, LANES), lambda h, i: (i, h))
    kspec = pl.BlockSpec((S, LANES), lambda h, i: (0, h))
    return pl.pallas_call(
        body, name="sb_fwd", grid=(H, nq), in_specs=[qspec, kspec, kspec], out_specs=qspec,
        out_shape=jax.ShapeDtypeStruct((S, W), f32), compiler_params=_cp(("parallel", "arbitrary")),
    )(qn, kn, vb)


def sb_bwd(qn, kn, vb, do, *, tq=256):
    S, W = qn.shape
    H = W // LANES
    tq = _tile(S, tq)
    tk = tq
    nq = S // tq

    def body(q_ref, k_ref, v_ref, do_ref, dq_ref, dk_ref, dv_ref, dz_s, beta_s):
        i = pl.program_id(1)

        @pl.when(i == 0)
        def _():
            dk_ref[...] = jnp.zeros_like(dk_ref)
            dv_ref[...] = jnp.zeros_like(dv_ref)

        q = q_ref[...]
        dob = do_ref[...].astype(bf16)
        row = lax.broadcasted_iota(jnp.int32, (tq, tk), 0)
        col = lax.broadcasted_iota(jnp.int32, (tq, tk), 1)
        UU = _tri2(tk)
        Ue = (row < col).astype(bf16)

        def sweep1(js, R, masked):
            rows = [pl.ds(pl.multiple_of(j * tk, tk), tk) for j in js]
            dws = [_dot(dob, v_ref[r, :], _NT) for r in rows]
            lbs, ws, R = _sb_weights(q, [k_ref[r, :] for r in rows], R, masked, row, col, UU)
            for j, lb, w, dw in zip(js, lbs, ws, dws):
                dz_s[j] = (dw * w).astype(bf16)
                beta_s[j] = jnp.exp2(lb).astype(bf16)
            for r, w in zip(rows, ws):
                dv_ref[r, :] += _dot(w, dob, _TN)
            return R

        R = sweep1([i], jnp.zeros((tq, 1), f32), True)
        _grouped(i, SB_GROUP, lambda ts, R: sweep1([i - 1 - t for t in ts], R, False), R)

        def sweep2(js, c, masked):
            rows = [pl.ds(pl.multiple_of(j * tk, tk), tk) for j in js]
            dzbs = [dz_s[j] for j in js]
            sums = [_dot(dzb, Ue, _NN) for dzb in dzbs]
            Lz, dq = c
            dlbs = []
            for j, dzb, Cz in zip(js, dzbs, sums):
                dz = dzb.astype(f32)
                Cz = Cz + Lz
                dl = dz - beta_s[j].astype(f32) * (dz + Cz)
                if masked:
                    dl = jnp.where(col < row, dl, 0.0)
                Lz = Cz[:, tk - 1:tk] + dz[:, tk - 1:tk]
                dlbs.append(dl.astype(bf16))
            for r, dlb in zip(rows, dlbs):
                dq = dq + _dot(dlb, k_ref[r, :], _NN)
            for r, dlb in zip(rows, dlbs):
                dk_ref[r, :] += _dot(dlb, q, _TN)
            return Lz, dq

        c = (jnp.zeros((tq, 1), f32), jnp.zeros((tq, LANES), f32))
        c = _grouped(i, SB_GROUP, lambda ts, c: sweep2(ts, c, False), c)
        dq_ref[...] = sweep2([i], c, True)[1]

    qspec = pl.BlockSpec((tq, LANES), lambda h, i: (i, h))
    kspec = pl.BlockSpec((S, LANES), lambda h, i: (0, h))
    full = jax.ShapeDtypeStruct((S, W), f32)
    return pl.pallas_call(
        body, name="sb_bwd", grid=(H, nq), in_specs=[qspec, kspec, kspec, qspec],
        out_specs=[qspec, kspec, kspec], out_shape=[full, full, full],
        scratch_shapes=[pltpu.VMEM((nq, tq, tk), bf16), pltpu.VMEM((nq, tq, tk), bf16)],
        compiler_params=_cp(("parallel", "arbitrary")),
    )(qn, kn, vb, do)


def _xa_fn(qx, kv, gq, gk):
    XW = XA_HEADS * XA_HEAD_DIM
    outs = []
    for h in range(XA_HEADS):
        sl = slice(h * XA_HEAD_DIM, (h + 1) * XA_HEAD_DIM)
        qn = _rms(qx[:, sl], gq, XA_HEAD_DIM)
        kn = _rms(kv[:, sl], gk, XA_HEAD_DIM)
        v = kv[:, XW + h * XA_HEAD_DIM:XW + (h + 1) * XA_HEAD_DIM]
        s = bdot_nt(qn, kn) * (XA_HEAD_DIM ** -0.5)
        e = jnp.exp(s - lax.stop_gradient(jnp.max(s, axis=-1, keepdims=True)))
        p = e / jnp.sum(e, axis=-1, keepdims=True)
        outs.append(bdot_nn(p, v))
    return jnp.concatenate(outs, axis=1)


def xa_fwd(qx, kv, gq, gk, *, ts=256):
    S, XW = qx.shape
    M = kv.shape[0]
    ts = _tile(S, ts)

    def body(q_ref, kv_ref, gq_ref, gk_ref, o_ref):
        o_ref[...] = _xa_fn(q_ref[...], kv_ref[...], gq_ref[...], gk_ref[...])

    row = pl.BlockSpec((ts, XW), lambda i: (i, 0))
    gspec = pl.BlockSpec((1, XA_HEAD_DIM), lambda i: (0, 0))
    return pl.pallas_call(
        body, name="xa_fwd", grid=(S // ts,),
        in_specs=[row, pl.BlockSpec((M, 2 * XW), lambda i: (0, 0)), gspec, gspec], out_specs=row,
        out_shape=jax.ShapeDtypeStruct((S, XW), f32), compiler_params=_cp(("parallel",)),
    )(qx, kv, gq, gk)


def xa_bwd(qx, kv, gq, gk, do, *, ts=256):
    S, XW = qx.shape
    M = kv.shape[0]
    ts = _tile(S, ts)

    def body(q_ref, kv_ref, gq_ref, gk_ref, do_ref, dq_ref, dkv_ref, dgq_ref, dgk_ref):
        _, vjp = jax.vjp(_xa_fn, q_ref[...], kv_ref[...], gq_ref[...], gk_ref[...])
        dq, dkv, dgq, dgk = vjp(do_ref[...])
        dq_ref[...] = dq

        @pl.when(pl.program_id(0) == 0)
        def _():
            dkv_ref[...] = jnp.zeros_like(dkv_ref)
            dgq_ref[...] = jnp.zeros_like(dgq_ref)
            dgk_ref[...] = jnp.zeros_like(dgk_ref)

        dkv_ref[...] += dkv
        dgq_ref[...] += dgq
        dgk_ref[...] += dgk

    row = pl.BlockSpec((ts, XW), lambda i: (i, 0))
    gspec = pl.BlockSpec((1, XA_HEAD_DIM), lambda i: (0, 0))
    kvspec = pl.BlockSpec((M, 2 * XW), lambda i: (0, 0))
    gshape = jax.ShapeDtypeStruct((1, XA_HEAD_DIM), f32)
    return pl.pallas_call(
        body, name="xa_bwd", grid=(S // ts,), in_specs=[row, kvspec, gspec, gspec, row],
        out_specs=[row, kvspec, gspec, gspec],
        out_shape=[jax.ShapeDtypeStruct((S, XW), f32), jax.ShapeDtypeStruct((M, 2 * XW), f32), gshape, gshape],
        compiler_params=_cp(("arbitrary",)),
    )(qx, kv, gq, gk, do)


def _s5_prep_fn(a_re, a_im, ldt, bT_re, bT_im, cT_re, cT_im):
    G, P, C = SSM_GROUPS, SSM_STATE, SSM_GROUP
    GP, GC = G * P, G * C
    lg_p, lg_c = P.bit_length() - 1, C.bit_length() - 1
    gi = lax.broadcasted_iota(jnp.int32, (G, GP), 0)
    ci = lax.broadcasted_iota(jnp.int32, (G, GP), 1) >> lg_p
    expand_dt = (gi == ci).astype(f32)
    dte = jnp.dot(jnp.exp(ldt), expand_dt, precision=lax.Precision.HIGHEST, preferred_element_type=f32)
    zr, zi = a_re * dte, a_im * dte
    mag = jnp.exp(zr)
    abr, abi = mag * jnp.cos(zi), mag * jnp.sin(zi)
    nr, ni = abr - 1.0, abi
    den = a_re * a_re + a_im * a_im
    cr = (nr * a_re + ni * a_im) / den
    cim = (ni * a_re - nr * a_im) / den
    bbr = cr * bT_re - cim * bT_im
    bbi = cr * bT_im + cim * bT_re
    rowg = lax.broadcasted_iota(jnp.int32, (GC, GP), 0) >> lg_c
    colg = lax.broadcasted_iota(jnp.int32, (GC, GP), 1) >> lg_p
    diag = rowg == colg

    def expand(t):
        return jnp.where(diag, jnp.broadcast_to(t[None], (G, C, GP)).reshape(GC, GP), 0.0)

    return abr, abi, expand(bbr), expand(bbi), expand(cT_re), expand(-cT_im)


def s5_prep(a_re, a_im, ldt, bT_re, bT_im, cT_re, cT_im):
    GP, GC = SSM_GROUPS * SSM_STATE, SSM_GROUPS * SSM_GROUP

    def body(a_re_ref, a_im_ref, ldt_ref, bTr_ref, bTi_ref, cTr_ref, cTi_ref, abr_ref, abi_ref, B_ref, C_ref):
        abr, abi, Br, Bi, Cr, Ci = _s5_prep_fn(a_re_ref[...], a_im_ref[...], ldt_ref[...], bTr_ref[...],
                                               bTi_ref[...], cTr_ref[...], cTi_ref[...])
        abr_ref[...] = abr
        abi_ref[...] = abi
        B_ref[0] = Br
        B_ref[1] = Bi
        C_ref[0] = Cr
        C_ref[1] = Ci

    vec = jax.ShapeDtypeStruct((1, GP), f32)
    mat = jax.ShapeDtypeStruct((2, GC, GP), f32)
    return pl.pallas_call(body, name="s5_prep", out_shape=[vec, vec, mat, mat], compiler_params=_cp())(
        a_re, a_im, ldt, bT_re, bT_im, cT_re, cT_im)


def s5_prep_bwd(a_re, a_im, ldt, bT_re, bT_im, cT_re, cT_im, dabr, dabi, dB, dC):
    def body(a_re_ref, a_im_ref, ldt_ref, bTr_ref, bTi_ref, cTr_ref, cTi_ref, dabr_ref, dabi_ref, dB_ref, dC_ref,
             *outs):
        _, vjp = jax.vjp(_s5_prep_fn, a_re_ref[...], a_im_ref[...], ldt_ref[...], bTr_ref[...], bTi_ref[...],
                         cTr_ref[...], cTi_ref[...])
        grads = vjp((dabr_ref[...], dabi_ref[...], dB_ref[0], dB_ref[1], dC_ref[0], dC_ref[1]))
        for o_ref, gv in zip(outs, grads):
            o_ref[...] = gv

    ins = (a_re, a_im, ldt, bT_re, bT_im, cT_re, cT_im)
    return pl.pallas_call(body, name="s5_prep_bwd", out_shape=[jax.ShapeDtypeStruct(v.shape, f32) for v in ins],
                          compiler_params=_cp())(*ins, dabr, dabi, dB, dC)


def _cmul(ar, ai, br, bi):
    return ar * br - ai * bi, ar * bi + ai * br


SCAN_CHUNKS = 32


def _chunk_carry(Lr, Li, Pr, Pi, scratch, reverse):
    lr_ref, li_ref, cr_ref, ci_ref = scratch
    lr_ref[...] = Lr
    li_ref[...] = Li
    cur_r = jnp.zeros((1, LANES), f32)
    cur_i = jnp.zeros((1, LANES), f32)
    order = range(SCAN_CHUNKS - 1, -1, -1) if reverse else range(SCAN_CHUNKS)
    for c in order:
        cr_ref[pl.ds(c, 1), :] = cur_r
        ci_ref[pl.ds(c, 1), :] = cur_i
        mr, mi = _cmul(Pr, Pi, cur_r, cur_i)
        cur_r, cur_i = lr_ref[pl.ds(c, 1), :] + mr, li_ref[pl.ds(c, 1), :] + mi
    return cr_ref[...], ci_ref[...]


def _scan_scratch(n):
    small = pltpu.VMEM((SCAN_CHUNKS, LANES), f32)
    return [pltpu.VMEM((n, LANES), f32), pltpu.VMEM((n, LANES), f32), small, small, small, small]


def scan_fwd(bu, abr, abi):
    _, S, N = bu.shape
    n = S // SCAN_CHUNKS
    shp = (SCAN_CHUNKS, LANES)

    def body(bu_ref, ar_ref, ai_ref, st_ref, pwr_ref, pwi_ref, *scratch):
        a1r, a1i = ar_ref[...], ai_ref[...]
        ar = jnp.broadcast_to(a1r, shp)
        ai = jnp.broadcast_to(a1i, shp)
        xr_ref, xi_ref = bu_ref.at[0], bu_ref.at[1]
        sr_ref, si_ref = st_ref.at[0], st_ref.at[1]

        def step(j, c):
            sr, si, pr, pi = c
            rows = pl.ds(j, SCAN_CHUNKS, stride=n)
            mr, mi = _cmul(ar, ai, sr, si)
            sr, si = mr + xr_ref[rows, :], mi + xi_ref[rows, :]
            sr_ref[rows, :] = sr
            si_ref[rows, :] = si
            pwr_ref[pl.ds(j, 1), :] = pr
            pwi_ref[pl.ds(j, 1), :] = pi
            npr, npi = _cmul(a1r, a1i, pr, pi)
            return sr, si, npr, npi

        z = jnp.zeros(shp, f32)
        sr, si, _, _ = lax.fori_loop(0, n, step, (z, z, a1r, a1i), unroll=2)
        cr, ci = _chunk_carry(sr, si, pwr_ref[pl.ds(n - 1, 1), :], pwi_ref[pl.ds(n - 1, 1), :], scratch, False)

        def step2(j, _):
            rows = pl.ds(j, SCAN_CHUNKS, stride=n)
            pr = jnp.broadcast_to(pwr_ref[pl.ds(j, 1), :], shp)
            pi = jnp.broadcast_to(pwi_ref[pl.ds(j, 1), :], shp)
            mr, mi = _cmul(pr, pi, cr, ci)
            sr_ref[rows, :] += mr
            si_ref[rows, :] += mi
            return 0

        lax.fori_loop(0, n, step2, 0, unroll=4)

    blk = pl.BlockSpec((2, S, LANES), lambda t: (0, 0, t))
    vec = pl.BlockSpec((1, LANES), lambda t: (0, t))
    return pl.pallas_call(
        body, name="scan_fwd", grid=(N // LANES,), in_specs=[blk, vec, vec], out_specs=blk,
        out_shape=jax.ShapeDtypeStruct((2, S, N), f32), scratch_shapes=_scan_scratch(n),
        compiler_params=_cp(("parallel",)),
    )(bu, abr, abi)


def scan_bwd(G, st, abr, abi):
    _, S, N = G.shape
    n = S // SCAN_CHUNKS
    shp = (SCAN_CHUNKS, LANES)

    def body(G_ref, st_ref, ar_ref, ai_ref, g_ref, dar_ref, dai_ref, qwr_ref, qwi_ref, *scratch):
        a1r, a1i = ar_ref[...], -ai_ref[...]
        ar = jnp.broadcast_to(a1r, shp)
        nai = jnp.broadcast_to(a1i, shp)
        Gr_ref, Gi_ref = G_ref.at[0], G_ref.at[1]
        sr_ref, si_ref = st_ref.at[0], st_ref.at[1]
        gr_ref, gi_ref = g_ref.at[0], g_ref.at[1]

        def step(jj, c):
            gr, gi, qr, qi = c
            j = n - 1 - jj
            rows = pl.ds(j, SCAN_CHUNKS, stride=n)
            mr, mi = _cmul(ar, nai, gr, gi)
            gr, gi = mr + Gr_ref[rows, :], mi + Gi_ref[rows, :]
            gr_ref[rows, :] = gr
            gi_ref[rows, :] = gi
            qwr_ref[pl.ds(j, 1), :] = qr
            qwi_ref[pl.ds(j, 1), :] = qi
            nqr, nqi = _cmul(a1r, a1i, qr, qi)
            return gr, gi, nqr, nqi

        z = jnp.zeros(shp, f32)
        gr, gi, _, _ = lax.fori_loop(0, n, step, (z, z, a1r, a1i), unroll=2)
        cr, ci = _chunk_carry(gr, gi, qwr_ref[pl.ds(0, 1), :], qwi_ref[pl.ds(0, 1), :], scratch, True)
        sub = lax.broadcasted_iota(jnp.int32, shp, 0)

        def fix(j, spr, spi, acc):
            rows = pl.ds(j, SCAN_CHUNKS, stride=n)
            qr = jnp.broadcast_to(qwr_ref[pl.ds(j, 1), :], shp)
            qi = jnp.broadcast_to(qwi_ref[pl.ds(j, 1), :], shp)
            mr, mi = _cmul(qr, qi, cr, ci)
            gr = gr_ref[rows, :] + mr
            gi = gi_ref[rows, :] + mi
            gr_ref[rows, :] = gr
            gi_ref[rows, :] = gi
            return acc[0] + gr * spr + gi * spi, acc[1] + gi * spr - gr * spi

        last = pl.ds(n - 1, SCAN_CHUNKS, stride=n)
        spr = jnp.where(sub == 0, 0.0, pltpu.roll(sr_ref[last, :], 1, 0))
        spi = jnp.where(sub == 0, 0.0, pltpu.roll(si_ref[last, :], 1, 0))
        acc = fix(0, spr, spi, (z, z))

        def step2(j, acc):
            prev = pl.ds(j - 1, SCAN_CHUNKS, stride=n)
            return fix(j, sr_ref[prev, :], si_ref[prev, :], acc)

        acc = lax.fori_loop(1, n, step2, acc)
        dar_ref[...] = jnp.sum(acc[0], axis=0, keepdims=True)
        dai_ref[...] = jnp.sum(acc[1], axis=0, keepdims=True)

    blk = pl.BlockSpec((2, S, LANES), lambda t: (0, 0, t))
    vec = pl.BlockSpec((1, LANES), lambda t: (0, t))
    vshape = jax.ShapeDtypeStruct((1, N), f32)
    return pl.pallas_call(
        body, name="scan_bwd", grid=(N // LANES,), in_specs=[blk, blk, vec, vec], out_specs=[blk, vec, vec],
        out_shape=[jax.ShapeDtypeStruct((2, S, N), f32), vshape, vshape], scratch_shapes=_scan_scratch(n),
        compiler_params=_cp(("parallel",)),
    )(G, st, abr, abi)


def _glu_fn(ypre, wglu):
    y = jax.nn.gelu(ypre)
    return y * jax.nn.sigmoid(bdot_nn(y, wglu))


def glu_fwd(ypre0, u, d, wglu, *, ts=512):
    S, W = u.shape
    ts = _tile(S, ts)

    def body(y0_ref, u_ref, d_ref, w_ref, ypre_ref, z_ref, rz_ref):
        ypre = y0_ref[...] + d_ref[...] * u_ref[...]
        z = _glu_fn(ypre, w_ref[...])
        ypre_ref[...] = ypre
        z_ref[...] = z
        rz_ref[...] = lax.rsqrt(jnp.sum(z * z, axis=-1, keepdims=True) * (1.0 / W) + NORM_EPS)

    row = pl.BlockSpec((ts, W), lambda i: (i, 0))
    full = jax.ShapeDtypeStruct((S, W), f32)
    return pl.pallas_call(
        body, name="glu_fwd", grid=(S // ts,),
        in_specs=[row, row, pl.BlockSpec((1, W), lambda i: (0, 0)), pl.BlockSpec((W, W), lambda i: (0, 0))],
        out_specs=[row, row, pl.BlockSpec((ts, 1), lambda i: (i, 0))],
        out_shape=[full, full, jax.ShapeDtypeStruct((S, 1), f32)], compiler_params=_cp(("parallel",)),
    )(ypre0, u, d, wglu)


def glu_bwd(ypre, u, d, wglu, dz, *, ts=512):
    S, W = u.shape
    ts = _tile(S, ts)

    def body(y_ref, u_ref, d_ref, w_ref, dz_ref, dy_ref, du_ref, dw_ref, dd_ref):
        _, vjp = jax.vjp(_glu_fn, y_ref[...], w_ref[...])
        dy, dw = vjp(dz_ref[...])
        dy_ref[...] = dy
        du_ref[...] = d_ref[...] * dy

        @pl.when(pl.program_id(0) == 0)
        def _():
            dw_ref[...] = jnp.zeros_like(dw_ref)
            dd_ref[...] = jnp.zeros_like(dd_ref)

        dw_ref[...] += dw
        dd_ref[...] += jnp.sum(dy * u_ref[...], axis=0, keepdims=True)

    row = pl.BlockSpec((ts, W), lambda i: (i, 0))
    vec = pl.BlockSpec((1, W), lambda i: (0, 0))
    sq = pl.BlockSpec((W, W), lambda i: (0, 0))
    full = jax.ShapeDtypeStruct((S, W), f32)
    return pl.pallas_call(
        body, name="glu_bwd", grid=(S // ts,), in_specs=[row, row, vec, sq, row], out_specs=[row, row, sq, vec],
        out_shape=[full, full, jax.ShapeDtypeStruct((W, W), f32), jax.ShapeDtypeStruct((1, W), f32)],
        compiler_params=_cp(("arbitrary",)),
    )(ypre, u, d, wglu, dz)


def loss_head(y, target, *, ts=512):
    S, D = y.shape
    ts = _tile(S, ts)

    def body(y_ref, t_ref, dy_ref, l_ref):
        err = y_ref[...] - t_ref[...]
        dy_ref[...] = err * (1.0 / D)

        @pl.when(pl.program_id(0) == 0)
        def _():
            l_ref[...] = jnp.zeros_like(l_ref)

        rows = jnp.sum(err * err, axis=1, keepdims=True) * (1.0 / D)
        l_ref[...] += 0.5 * jnp.sum(rows, axis=0, keepdims=True)

    row = pl.BlockSpec((ts, D), lambda i: (i, 0))
    return pl.pallas_call(
        body, name="loss_head", grid=(S // ts,), in_specs=[row, row],
        out_specs=[row, pl.BlockSpec((1, 1), lambda i: (0, 0))],
        out_shape=[jax.ShapeDtypeStruct((S, D), f32), jax.ShapeDtypeStruct((1, 1), f32)],
        compiler_params=_cp(("arbitrary",)),
    )(y, target)


def adamw(w, g, m, v, *, name, tr=256):
    R, C = w.shape
    tr = _row_tile(R, tr)

    def body(w_ref, g_ref, m_ref, v_ref, d_ref, nm_ref, nv_ref):
        gv = g_ref[...]
        nm = ADAM_B1 * m_ref[...] + (1.0 - ADAM_B1) * gv
        nv = ADAM_B2 * v_ref[...] + (1.0 - ADAM_B2) * jnp.square(gv)
        m_hat = nm / (1.0 - ADAM_B1 ** ADAM_STEP)
        v_hat = nv / (1.0 - ADAM_B2 ** ADAM_STEP)
        d_ref[...] = -ADAM_LR * (m_hat / (jnp.sqrt(v_hat) + ADAM_EPS) + ADAM_WD * w_ref[...])
        nm_ref[...] = nm
        nv_ref[...] = nv

    row = pl.BlockSpec((tr, C), lambda i: (i, 0))
    full = jax.ShapeDtypeStruct((R, C), f32)
    return pl.pallas_call(
        body, name=name, grid=(R // tr,), in_specs=[row] * 4, out_specs=[row] * 3, out_shape=[full] * 3,
        compiler_params=_cp(("parallel",)),
    )(w, g, m, v)


def add_half(g4, recv, c, *, name, tr=256):
    _, _, Rh, C = g4.shape
    tr = _row_tile(Rh, tr)

    def body(c_ref, a_ref, b_ref, o_ref):
        o_ref[...] = a_ref[...] + b_ref[...]

    grid_spec = pltpu.PrefetchScalarGridSpec(
        num_scalar_prefetch=1, grid=(N_CHIPS, Rh // tr),
        in_specs=[pl.BlockSpec((None, None, tr, C), lambda k, i, c_ref: (k, c_ref[0], i, 0)),
                  pl.BlockSpec((None, tr, C), lambda k, i, c_ref: (k, i, 0))],
        out_specs=pl.BlockSpec((None, tr, C), lambda k, i, c_ref: (k, i, 0)))
    return pl.pallas_call(body, name=name, grid_spec=grid_spec, out_shape=jax.ShapeDtypeStruct(recv.shape, f32),
                          compiler_params=_cp(("parallel", "parallel")))(c, g4, recv)


def sum_chips(p4, *, name, tr=256):
    _, Rh, C = p4.shape
    tr = _row_tile(Rh, tr)

    def body(a_ref, b_ref, c_ref, d_ref, o_ref):
        o_ref[...] = ((a_ref[...] + b_ref[...]) + c_ref[...]) + d_ref[...]

    spec = lambda k: pl.BlockSpec((None, tr, C), lambda i: (k, i, 0))
    return pl.pallas_call(
        body, name=name, grid=(Rh // tr,), in_specs=[spec(0), spec(1), spec(2), spec(3)],
        out_specs=pl.BlockSpec((tr, C), lambda i: (i, 0)), out_shape=jax.ShapeDtypeStruct((Rh, C), f32),
        compiler_params=_cp(("parallel",)),
    )(p4, p4, p4, p4)


def _place():
    return lax.axis_index("x"), lax.axis_index("y"), lax.axis_index("c")


def _other_chips(x, y):
    return [(1 - x, y), (x, 1 - y), (1 - x, 1 - y)]


def allgather_chips(arrs, *, name):
    n = len(arrs)

    def body(*refs):
        ins, outs = refs[:n], refs[n:2 * n]
        send, recv, loc = refs[2 * n:]
        x, y, c = _place()
        me = 2 * x + y
        peers = _other_chips(x, y)
        started = []
        for a in range(n):
            own = pltpu.make_async_copy(ins[a], outs[a].at[me], loc.at[a])
            own.start()
            started.append(own)
        for a in range(n):
            for p, (px, py) in enumerate(peers):
                cp = pltpu.make_async_remote_copy(
                    src_ref=ins[a], dst_ref=outs[a].at[me], send_sem=send.at[3 * a + p], recv_sem=recv.at[3 * a + p],
                    device_id=(px, py, c), device_id_type=MESH)
                cp.start()
        for a in range(n):
            for p, (px, py) in enumerate(peers):
                pltpu.make_async_remote_copy(
                    src_ref=ins[a], dst_ref=outs[a].at[2 * px + py], send_sem=send.at[3 * a + p],
                    recv_sem=recv.at[3 * a + p], device_id=(px, py, c), device_id_type=MESH).wait()
        for own in started:
            own.wait()

    return pl.pallas_call(
        body, name=name, in_specs=[ANY] * n, out_specs=[ANY] * n,
        out_shape=[jax.ShapeDtypeStruct((N_CHIPS,) + a.shape, a.dtype) for a in arrs],
        scratch_shapes=[pltpu.SemaphoreType.DMA((3 * n,)), pltpu.SemaphoreType.DMA((3 * n,)),
                        pltpu.SemaphoreType.DMA((n,))],
        compiler_params=pltpu.CompilerParams(has_side_effects=True),
    )(*arrs)


def sibling_swap(arrs, *, half, name):
    n = len(arrs)

    def body(*refs):
        ins, outs = refs[:n], refs[n:2 * n]
        send, recv = refs[2 * n:]
        x, y, c = _place()
        cps = []
        for a in range(n):
            src = ins[a].at[:, 1 - c] if half else ins[a]
            cp = pltpu.make_async_remote_copy(src_ref=src, dst_ref=outs[a], send_sem=send.at[a], recv_sem=recv.at[a],
                                              device_id=(x, y, 1 - c), device_id_type=MESH)
            cp.start()
            cps.append(cp)
        for cp in cps:
            cp.wait()

    def oshape(a):
        return jax.ShapeDtypeStruct((a.shape[0],) + a.shape[2:] if half else a.shape, a.dtype)

    return pl.pallas_call(
        body, name=name, in_specs=[ANY] * n, out_specs=[ANY] * n, out_shape=[oshape(a) for a in arrs],
        scratch_shapes=[pltpu.SemaphoreType.DMA((n,)), pltpu.SemaphoreType.DMA((n,))],
        compiler_params=pltpu.CompilerParams(has_side_effects=True),
    )(*arrs)


def chip_scatter(arrs, *, name):
    n = len(arrs)

    def body(*refs):
        ins, outs = refs[:n], refs[n:2 * n]
        send, recv, loc = refs[2 * n:]
        x, y, c = _place()
        me = 2 * x + y
        peers = _other_chips(x, y)
        started = []
        for a in range(n):
            own = pltpu.make_async_copy(ins[a].at[me], outs[a].at[me], loc.at[a])
            own.start()
            started.append(own)
        for a in range(n):
            for p, (px, py) in enumerate(peers):
                pltpu.make_async_remote_copy(
                    src_ref=ins[a].at[2 * px + py], dst_ref=outs[a].at[me], send_sem=send.at[3 * a + p],
                    recv_sem=recv.at[3 * a + p], device_id=(px, py, c), device_id_type=MESH).start()
        for a in range(n):
            for p, (px, py) in enumerate(peers):
                pltpu.make_async_remote_copy(
                    src_ref=ins[a].at[2 * px + py], dst_ref=outs[a].at[2 * px + py], send_sem=send.at[3 * a + p],
                    recv_sem=recv.at[3 * a + p], device_id=(px, py, c), device_id_type=MESH).wait()
        for own in started:
            own.wait()

    return pl.pallas_call(
        body, name=name, in_specs=[ANY] * n, out_specs=[ANY] * n,
        out_shape=[jax.ShapeDtypeStruct(a.shape, a.dtype) for a in arrs],
        scratch_shapes=[pltpu.SemaphoreType.DMA((3 * n,)), pltpu.SemaphoreType.DMA((3 * n,)),
                        pltpu.SemaphoreType.DMA((n,))],
        compiler_params=pltpu.CompilerParams(has_side_effects=True),
    )(*arrs)


def _pad_cols(w):
    K = w.shape[0]
    w = w.reshape(K, -1, SB_HEAD_DIM)
    return jnp.pad(w, ((0, 0), (0, 0), (0, LANES - SB_HEAD_DIM))).reshape(K, -1)


def _unpad_cols(w):
    K = w.shape[0]
    return w.reshape(K, -1, LANES)[:, :, :SB_HEAD_DIM].reshape(K, -1)


def _pad_rows(w):
    N = w.shape[1]
    w = w.reshape(-1, SB_HEAD_DIM, N)
    return jnp.pad(w, ((0, 0), (0, LANES - SB_HEAD_DIM), (0, 0))).reshape(-1, N)


def _unpad_rows(w):
    N = w.shape[1]
    return w.reshape(-1, LANES, N)[:, :SB_HEAD_DIM, :].reshape(-1, N)


_PACK_ROWS = N_CHIPS * 2 * SUBLANES


def _pack(arrs):
    flat = jnp.concatenate([a.reshape(-1) for a in arrs])
    rows = -(-flat.shape[0] // LANES)
    rows = -(-rows // _PACK_ROWS) * _PACK_ROWS
    return jnp.pad(flat, (0, rows * LANES - flat.shape[0])).reshape(rows, LANES)


def _unpack(buf, shapes):
    flat = buf.reshape(-1)
    out, pos = [], 0
    for shp in shapes:
        size = 1
        for d in shp:
            size *= d
        out.append(flat[pos:pos + size].reshape(shp))
        pos += size
    return out


BIG = ("w_in", "ssm_w_glu", "w_out", "xa_w_q", "xa_w_kv", "xa_w_o", "w_up", "w_down")
SMALL = ("g_mix", "ssm_a_re", "ssm_a_im", "ssm_log_dt", "ssm_b_re", "ssm_b_im", "ssm_c_re", "ssm_c_im", "ssm_d",
         "sb_g_q", "sb_g_k", "g_out_ssm", "g_out_sb", "g_xa", "g_mem", "xa_g_q", "xa_g_k", "g_mlp")
WEIGHTS = ("g_mix", "w_in", "ssm_a_re", "ssm_a_im", "ssm_log_dt", "ssm_b_re", "ssm_b_im", "ssm_c_re", "ssm_c_im",
           "ssm_d", "ssm_w_glu", "sb_g_q", "sb_g_k", "g_out_ssm", "g_out_sb", "w_out", "g_xa", "g_mem", "xa_w_q",
           "xa_w_kv", "xa_g_q", "xa_g_k", "xa_w_o", "g_mlp", "w_up", "w_down")


def kernel(x, mem, g_mix, w_in, ssm_a_re, ssm_a_im, ssm_log_dt, ssm_b_re, ssm_b_im, ssm_c_re, ssm_c_im, ssm_d, ssm_w_glu, sb_g_q, sb_g_k, g_out_ssm, g_out_sb, w_out, g_xa, g_mem, xa_w_q, xa_w_kv, xa_g_q, xa_g_k, xa_w_o, g_mlp, w_up, w_down, loss_target, m_g_mix, m_w_in, m_ssm_a_re, m_ssm_a_im, m_ssm_log_dt, m_ssm_b_re, m_ssm_b_im, m_ssm_c_re, m_ssm_c_im, m_ssm_d, m_ssm_w_glu, m_sb_g_q, m_sb_g_k, m_g_out_ssm, m_g_out_sb, m_w_out, m_g_xa, m_g_mem, m_xa_w_q, m_xa_w_kv, m_xa_g_q, m_xa_g_k, m_xa_w_o, m_g_mlp, m_w_up, m_w_down, v_g_mix, v_w_in, v_ssm_a_re, v_ssm_a_im, v_ssm_log_dt, v_ssm_b_re, v_ssm_b_im, v_ssm_c_re, v_ssm_c_im, v_ssm_d, v_ssm_w_glu, v_sb_g_q, v_sb_g_k, v_g_out_ssm, v_g_out_sb, v_w_out, v_g_xa, v_g_mem, v_xa_w_q, v_xa_w_kv, v_xa_g_q, v_xa_g_k, v_xa_w_o, v_g_mlp, v_w_up, v_w_down):
    env = dict(locals())
    W = {n: env[n] for n in WEIGHTS}
    M1 = {n: env["m_" + n] for n in WEIGHTS}
    V2 = {n: env["v_" + n] for n in WEIGHTS}
    xs, mems, tgt = x[0], mem[0], loss_target[0]
    S, D = xs.shape
    G, P, C = SSM_GROUPS, SSM_STATE, SSM_GROUP
    GP = G * P
    SBW = SB_HEADS * SB_HEAD_DIM
    c_idx = lax.axis_index("c")

    gath = allgather_chips([W[n][0].astype(bf16) for n in BIG], name="gather_weights")
    g_in, g_glu, g_out, g_xq, g_xkv, g_xo, g_up, g_down = gath
    Wu = g_in[0]
    Wqkv = jnp.concatenate([_pad_cols(g_in[1]), _pad_cols(g_in[2]), _pad_cols(g_in[3])], axis=1)
    Wglu = g_glu.reshape(-1, g_glu.shape[-1])
    Wout = g_out.reshape(-1, g_out.shape[-1])
    Wo_ssm, Wo_sb = Wout[:SBW], _pad_rows(Wout[SBW:])
    Wxq = g_xq.reshape(-1, g_xq.shape[-1])
    Wxkv = g_xkv.reshape(-1, g_xkv.shape[-1])
    Wxo = g_xo.transpose(1, 0, 2).reshape(g_xo.shape[1], -1)
    Wup = g_up.transpose(1, 0, 2).reshape(g_up.shape[1], -1)
    Wdown = g_down.reshape(-1, g_down.shape[-1])
    gq_pad, gk_pad = _pad_cols(sb_g_q), _pad_cols(sb_g_k)
    gosb_pad = _pad_cols(g_out_sb)
    a_re, a_im = ssm_a_re.reshape(1, GP), ssm_a_im.reshape(1, GP)
    bT_re = ssm_b_re[0].transpose(2, 0, 1).reshape(C, GP)
    bT_im = ssm_b_im[0].transpose(2, 0, 1).reshape(C, GP)
    cT_re = ssm_c_re[0].transpose(1, 0, 2).reshape(C, GP)
    cT_im = ssm_c_im[0].transpose(1, 0, 2).reshape(C, GP)
    s5_in = (a_re, a_im, ssm_log_dt, bT_re, bT_im, cT_re, cT_im)

    r0 = rms_stats(xs, D, name="rms_x")
    u = mm(xs, Wu, mode="nn", name="proj_u", pro="rms", r=r0, g=g_mix)
    qkv = mm(xs, Wqkv, mode="nn", name="proj_qkv", pro="rms", r=r0, g=g_mix, tn=1024, tk=1024)
    qn, kn, vb = qkv_prep(qkv, gq_pad, gk_pad)
    o = sb_fwd(qn, kn, vb)
    abr, abi, Bm, Cm = s5_prep(*s5_in)
    bu = mm(u, Bm, mode="nn", name="s5_bu", b_shards=2, out_shards=2, tn=1024)
    st = scan_fwd(bu, abr, abi)
    ypre0 = mm(st, Cm, mode="nt", name="s5_y", a_shards=2, b_shards=2, tk=1024)
    ypre, z, rz = glu_fwd(ypre0, u, ssm_d, Wglu)
    ro = rms_stats(o, SBW, name="rms_o")
    x1a = mm(z, Wo_ssm, mode="nn", name="out_ssm", pro="rms", r=rz, g=g_out_ssm, epi="add", aux=xs, tn=1024)
    x1 = mm(o, Wo_sb, mode="nn", name="out_sb", pro="rms", r=ro, g=gosb_pad, epi="add", aux=x1a, tn=1024, tk=1024)
    r1 = rms_stats(x1, D, name="rms_x1")
    qx = mm(x1, Wxq, mode="nn", name="xa_q", pro="rms", r=r1, g=g_xa, tk=1024)
    rm = rms_stats(mems, D, name="rms_mem")
    kv = mm(mems, Wxkv, mode="nn", name="xa_kv", pro="rms", r=rm, g=g_mem, tn=1024, tk=1024)
    ox = xa_fwd(qx, kv, xa_g_q, xa_g_k)
    x2 = mm(ox, Wxo, mode="nn", name="xa_o", epi="add", aux=x1, tn=1024)
    r2 = rms_stats(x2, D, name="rms_x2")
    act = mm(x2, Wup, mode="nn", name="mlp_up", pro="rms", r=r2, g=g_mlp, tn=1024, tk=1024)
    x3 = mm(act, Wdown, mode="nn", name="mlp_down", pro="relu2", epi="add", aux=x2, tn=1024, tk=1024)
    dx3, loss_part = loss_head(x3, tgt)
    loss = lax.psum(loss_part[0, 0], ("x", "y", "c"))

    dact = mm(dx3, Wdown, mode="nt", name="d_act", epi="mul2relu", aux=act, tn=1024, tk=1024)
    dWdown = mm(act, dx3, mode="tn", name="dw_down", pro="relu2", tn=1024, tk=1024)
    dWup = mm(x2, dact, mode="tn", name="dw_up", pro="rms", r=r2, g=g_mlp, out_shards=N_CHIPS, tn=1024, tk=1024)
    dh2 = mm(dact, Wup, mode="nt", name="d_h2", tn=1024, tk=1024)
    dx2, dg_mlp = rms_bwd(x2, g_mlp, dh2, dx3, D, name="rms_bwd_mlp")
    dox = mm(dx2, Wxo, mode="nt", name="d_ox", tk=1024)
    dWxo = mm(ox, dx2, mode="tn", name="dw_xo", out_shards=N_CHIPS, tk=1024)
    dqx, dkv, dg_xq, dg_xk = xa_bwd(qx, kv, xa_g_q, xa_g_k, dox)
    dWxq = mm(x1, dqx, mode="tn", name="dw_xq", pro="rms", r=r1, g=g_xa, tk=1024)
    dh1 = mm(dqx, Wxq, mode="nt", name="d_h1", tn=1024)
    dx1, dg_xa = rms_bwd(x1, g_xa, dh1, dx2, D, name="rms_bwd_xa")
    dWxkv = mm(mems, dkv, mode="tn", name="dw_xkv", pro="rms", r=rm, g=g_mem, tn=1024)
    dmemn = mm(dkv, Wxkv, mode="nt", name="d_memn", tn=1024, tk=1024)
    _, dg_mem = rms_bwd(mems, g_mem, dmemn, None, D, name="rms_bwd_mem")
    dyn_ssm = mm(dx1, Wo_ssm, mode="nt", name="d_yn_ssm", tk=1024)
    dyn_sb = mm(dx1, Wo_sb, mode="nt", name="d_yn_sb", tn=1024, tk=1024)
    dWo_ssm = mm(z, dx1, mode="tn", name="dw_out_ssm", pro="rms", r=rz, g=g_out_ssm, tn=1024, tk=1024)
    dWo_sb = mm(o, dx1, mode="tn", name="dw_out_sb", pro="rms", r=ro, g=gosb_pad, tn=1024, tk=1024)
    dz, dg_os = rms_bwd(z, g_out_ssm, dyn_ssm, None, SBW, name="rms_bwd_ssm")
    do, dg_osb = rms_bwd(o, gosb_pad, dyn_sb, None, SBW, name="rms_bwd_sb")
    dqn, dkn, dv = sb_bwd(qn, kn, vb, do)
    dqkv, dg_q, dg_k = qkv_bwd(qkv, gq_pad, gk_pad, dqn, dkn, dv)
    dypre, du_skip, dWglu, dd = glu_bwd(ypre, u, ssm_d, Wglu, dz)
    dst = mm(dypre, Cm, mode="nn", name="d_states", b_shards=2, out_shards=2, tn=1024)
    dCm = mm(dypre, st, mode="tn", name="d_cmat", b_shards=2, out_shards=2, tn=1024, tk=1024)
    gst, dabr, dabi = scan_bwd(dst, st, abr, abi)
    dBm = mm(u, gst, mode="tn", name="d_bmat", b_shards=2, out_shards=2, tn=1024, tk=1024)
    du = mm(gst, Bm, mode="nt", name="d_u", a_shards=2, b_shards=2, epi="add", aux=du_skip, tk=1024)
    s5_g = s5_prep_bwd(*s5_in, dabr, dabi, dBm, dCm)
    dWu = mm(xs, du, mode="tn", name="dw_u", pro="rms", r=r0, g=g_mix, tk=1024)
    dWqkv = mm(xs, dqkv, mode="tn", name="dw_qkv", pro="rms", r=r0, g=g_mix, tn=1024, tk=1024)
    dh0a = mm(du, Wu, mode="nt", name="d_h0_u", tn=1024)
    dh0 = mm(dqkv, Wqkv, mode="nt", name="d_h0_qkv", epi="add", aux=dh0a, tn=1024, tk=1024)
    dx, dg_mix = rms_bwd(xs, g_mix, dh0, dx1, D, name="rms_bwd_mix")

    HW = SB_HEADS * LANES
    big_g = {
        "w_in": jnp.stack([dWu, _unpad_cols(dWqkv[:, :HW]), _unpad_cols(dWqkv[:, HW:2 * HW]),
                           _unpad_cols(dWqkv[:, 2 * HW:])]),
        "ssm_w_glu": dWglu.reshape(N_CHIPS, -1, dWglu.shape[1]),
        "w_out": jnp.concatenate([dWo_ssm, _unpad_rows(dWo_sb)]).reshape(N_CHIPS, -1, D),
        "xa_w_q": dWxq.reshape(N_CHIPS, -1, dWxq.shape[1]),
        "xa_w_kv": dWxkv.reshape(N_CHIPS, -1, dWxkv.shape[1]),
        "xa_w_o": dWxo,
        "w_up": dWup,
        "w_down": dWdown.reshape(N_CHIPS, -1, D),
    }
    da_re, da_im, dldt, dbT_re, dbT_im, dcT_re, dcT_im = s5_g
    small_g = {
        "g_mix": dg_mix, "ssm_a_re": da_re, "ssm_a_im": da_im, "ssm_log_dt": dldt,
        "ssm_b_re": dbT_re.reshape(C, G, P).transpose(1, 2, 0), "ssm_b_im": dbT_im.reshape(C, G, P).transpose(1, 2, 0),
        "ssm_c_re": dcT_re.reshape(C, G, P).transpose(1, 0, 2), "ssm_c_im": dcT_im.reshape(C, G, P).transpose(1, 0, 2),
        "ssm_d": dd, "sb_g_q": dg_q[:, :SB_HEAD_DIM], "sb_g_k": dg_k[:, :SB_HEAD_DIM], "g_out_ssm": dg_os,
        "g_out_sb": _unpad_cols(dg_osb), "g_xa": dg_xa, "g_mem": dg_mem, "xa_g_q": dg_xq, "xa_g_k": dg_xk,
        "g_mlp": dg_mlp,
    }
    packed = _pack([small_g[n] for n in SMALL])
    full_g = [big_g[n] for n in BIG] + [packed.reshape(N_CHIPS, -1, LANES)]

    g4 = [g.reshape(N_CHIPS, 2, g.shape[1] // 2, g.shape[2]) for g in full_g]
    from_sib = sibling_swap(g4, half=True, name="grad_to_sibling")
    c_arr = c_idx.astype(jnp.int32).reshape(1)
    names = list(BIG) + ["small"]
    pair = [add_half(a, b, c_arr, name="add_sibling_" + n) for a, b, n in zip(g4, from_sib, names)]
    parts = chip_scatter(pair, name="grad_to_chips")
    mine = [sum_chips(p, name="sum_chips_" + n) for p, n in zip(parts, names)]
    other = sibling_swap(mine, half=False, name="grad_half_to_sibling")
    shard = [jnp.where(c_idx == 0, jnp.concatenate([a, b]), jnp.concatenate([b, a])) for a, b in zip(mine, other)]
    small_all = allgather_chips([shard[-1]], name="gather_small")[0]
    small_red = small_all.reshape(-1, LANES)

    out = {}
    for n, gs in zip(BIG, shard[:-1]):
        shp = W[n].shape
        w2, m2, v2 = (t.reshape(gs.shape) for t in (W[n], M1[n], V2[n]))
        d, nm, nv = adamw(w2, gs, m2, v2, name="adamw_" + n)
        out[n] = tuple(t.reshape(shp) for t in (gs, d, nm, nv))
    shapes = [W[n].shape for n in SMALL]
    d, nm, nv = adamw(_pack([W[n] for n in SMALL]), small_red, _pack([M1[n] for n in SMALL]),
                      _pack([V2[n] for n in SMALL]), name="adamw_small")
    for n, gs, dd_, mm_, vv_ in zip(SMALL, _unpack(small_red, shapes), _unpack(d, shapes), _unpack(nm, shapes),
                                    _unpack(nv, shapes)):
        out[n] = (gs, dd_, mm_, vv_)
    res = [loss, dx[None]]
    for kind in range(4):
        res += [out[n][kind] for n in WEIGHTS]
    return tuple(res)
```

```python
import jax
import jax.numpy as jnp
from jax import lax
from jax.experimental import pallas as pl
from jax.experimental.pallas import tpu as pltpu

f32 = jnp.float32
bf16 = jnp.bfloat16

NORM_EPS = 1e-6
SSM_GROUPS = 32
SSM_GROUP = 16
SSM_STATE = 64
SB_HEADS = 8
SB_HEAD_DIM = 64
XA_HEADS = 4
XA_HEAD_DIM = 128
LANES = 128
SUBLANES = 8
N_CHIPS = 4
ADAM_LR = 0.001
ADAM_B1 = 0.9
ADAM_B2 = 0.999
ADAM_EPS = 1e-08
ADAM_WD = 0.01
ADAM_STEP = 10
VMEM_LIMIT = 56 * 1024 * 1024
MESH = pl.DeviceIdType.MESH
ANY = pl.BlockSpec(memory_space=pl.ANY)


def _cp(sem=None):
    return pltpu.CompilerParams(dimension_semantics=sem, vmem_limit_bytes=VMEM_LIMIT)


def _tile(n, pref):
    if n <= pref:
        return n
    t = (pref // LANES) * LANES
    while t > LANES and n % t:
        t -= LANES
    assert n % t == 0, (n, pref)
    return t


def _row_tile(n, pref):
    if n <= pref:
        return n
    t = (pref // SUBLANES) * SUBLANES
    while n % t:
        t -= SUBLANES
    return t


def _dot(a, b, dims):
    return lax.dot_general(a.astype(bf16), b.astype(bf16), (dims, ((), ())), preferred_element_type=f32)


_NN = ((1,), (0,))
_NT = ((1,), (1,))
_TN = ((0,), (0,))


@jax.custom_vjp
def bdot_nn(a, b):
    return _dot(a, b, _NN)


def _bdot_nn_fwd(a, b):
    return _dot(a, b, _NN), (a, b)


def _bdot_nn_bwd(res, g):
    a, b = res
    return _dot(g, b, _NT), _dot(a, g, _TN)


bdot_nn.defvjp(_bdot_nn_fwd, _bdot_nn_bwd)


@jax.custom_vjp
def bdot_nt(a, b):
    return _dot(a, b, _NT)


def _bdot_nt_fwd(a, b):
    return _dot(a, b, _NT), (a, b)


def _bdot_nt_bwd(res, g):
    a, b = res
    return _dot(g, b, _NN), _dot(g, a, _TN)


bdot_nt.defvjp(_bdot_nt_fwd, _bdot_nt_bwd)


def _rms(x, g, denom):
    r = lax.rsqrt(jnp.sum(x * x, axis=-1, keepdims=True) * (1.0 / denom) + NORM_EPS)
    return x * r * g


def _opspec(block, row_of, col_of, shards, ncol_tiles):
    if shards == 1:
        return pl.BlockSpec(block, lambda i, j, k: (row_of(i, j, k), col_of(i, j, k)))
    per = ncol_tiles // shards
    return pl.BlockSpec((None,) + block,
                        lambda i, j, k: (col_of(i, j, k) // per, row_of(i, j, k), col_of(i, j, k) % per))


def mm(a, b, *, mode, name, tm=512, tn=512, tk=512, pro="none", epi="none", aux=None,
       out_dtype=f32, a_shards=1, b_shards=1, out_shards=1):
    ar, ac = a.shape[-2], a.shape[-1] * a_shards
    br, bc = b.shape[-2], b.shape[-1] * b_shards
    if mode == "nn":
        M, K, N = ar, ac, bc
        assert br == K
    elif mode == "nt":
        M, K, N = ar, ac, br
        assert bc == K
    else:
        M, K, N = ac, ar, bc
        assert br == K
    tm, tn, tk = _tile(M, tm), _tile(N, tn), _tile(K, tk)
    if a_shards > 1:
        if mode == "tn":
            tm = _tile(ac // a_shards, tm)
        else:
            tk = _tile(ac // a_shards, tk)
    if b_shards > 1:
        if mode == "nt":
            tk = _tile(bc // b_shards, tk)
        else:
            tn = _tile(bc // b_shards, tn)
    if out_shards > 1:
        tn = _tile(N // out_shards, tn)
    nm, nn_, nk = M // tm, N // tn, K // tk
    I = lambda i, j, k: i
    J = lambda i, j, k: j
    Kk = lambda i, j, k: k
    if mode == "nn":
        a_spec = _opspec((tm, tk), I, Kk, a_shards, nk)
        b_spec = _opspec((tk, tn), Kk, J, b_shards, nn_)
        dims = _NN
    elif mode == "nt":
        a_spec = _opspec((tm, tk), I, Kk, a_shards, nk)
        b_spec = _opspec((tn, tk), J, Kk, b_shards, nk)
        dims = _NT
    else:
        a_spec = _opspec((tk, tm), Kk, I, a_shards, nm)
        b_spec = _opspec((tk, tn), Kk, J, b_shards, nn_)
        dims = _TN
    in_specs = [a_spec, b_spec]
    args = [a, b]
    if epi != "none":
        in_specs.append(pl.BlockSpec((tm, tn), lambda i, j, k: (i, j)))
        args.append(aux)
    if out_shards == 1:
        out_spec = pl.BlockSpec((tm, tn), lambda i, j, k: (i, j))
        out_shape = jax.ShapeDtypeStruct((M, N), out_dtype)
    else:
        per = nn_ // out_shards
        out_spec = pl.BlockSpec((None, tm, tn), lambda i, j, k: (j // per, i, j % per))
        out_shape = jax.ShapeDtypeStruct((out_shards, M, N // out_shards), out_dtype)

    def body(*refs):
        a_ref, b_ref = refs[0], refs[1]
        pos = 2
        if epi != "none":
            aux_ref = refs[pos]
            pos += 1
        o_ref, acc_ref = refs[pos], refs[pos + 1]
        k = pl.program_id(2)

        @pl.when(k == 0)
        def _():
            acc_ref[...] = jnp.zeros_like(acc_ref)

        av = a_ref[...]
        if pro == "relu2":
            av = jnp.square(jnp.maximum(av.astype(f32), 0.0))
        acc_ref[...] += _dot(av, b_ref[...], dims)

        @pl.when(k == nk - 1)
        def _():
            res = acc_ref[...]
            if epi == "add":
                res = res + aux_ref[...].astype(f32)
            elif epi == "mul2relu":
                res = res * (2.0 * jnp.maximum(aux_ref[...].astype(f32), 0.0))
            o_ref[...] = res.astype(out_dtype)

    return pl.pallas_call(
        body, name=name, grid=(nm, nn_, nk), in_specs=in_specs, out_specs=out_spec, out_shape=out_shape,
        scratch_shapes=[pltpu.VMEM((tm, tn), f32)],
        compiler_params=_cp(("parallel", "parallel", "arbitrary")),
    )(*args)


def rms_norm(x, g, denom, *, name, ts=512):
    S, D = x.shape
    ts = _tile(S, ts)

    def body(x_ref, g_ref, h_ref):
        h_ref[...] = _rms(x_ref[...], g_ref[...], denom).astype(bf16)

    row = pl.BlockSpec((ts, D), lambda i: (i, 0))
    return pl.pallas_call(
        body, name=name, grid=(S // ts,), in_specs=[row, pl.BlockSpec((1, D), lambda i: (0, 0))], out_specs=row,
        out_shape=jax.ShapeDtypeStruct((S, D), bf16), compiler_params=_cp(("parallel",)),
    )(x, g)


def rms_bwd(x, g, dy, res, denom, *, name, ts=256, twin=False):
    S, D = x.shape
    ts = _tile(S, ts)
    has_res = res is not None

    def body(*refs):
        x_ref, g_ref, dy_ref = refs[:3]
        outs = refs[4:] if has_res else refs[3:]
        _, vjp = jax.vjp(lambda xv, gv: _rms(xv, gv, denom), x_ref[...], g_ref[...])
        dx, dg = vjp(dy_ref[...])
        if has_res:
            dx = dx + refs[3][...]
        outs[0][...] = dx
        if twin:
            outs[2][...] = dx.astype(bf16)
        dg_ref = outs[1]

        @pl.when(pl.program_id(0) == 0)
        def _():
            dg_ref[...] = jnp.zeros_like(dg_ref)

        dg_ref[...] += dg

    row = pl.BlockSpec((ts, D), lambda i: (i, 0))
    vec = pl.BlockSpec((1, D), lambda i: (0, 0))
    in_specs = [row, vec, row] + ([row] if has_res else [])
    args = [x, g, dy] + ([res] if has_res else [])
    return pl.pallas_call(
        body, name=name, grid=(S // ts,), in_specs=in_specs, out_specs=[row, vec] + ([row] if twin else []),
        out_shape=[jax.ShapeDtypeStruct((S, D), f32), jax.ShapeDtypeStruct((1, D), f32)]
        + ([jax.ShapeDtypeStruct((S, D), bf16)] if twin else []),
        compiler_params=_cp(("arbitrary",)),
    )(*args)


LOG2E = 1.4426950408889634
LN2 = 0.6931471805599453


def _qk_fn(q, k, gq, gk):
    qs, ks = [], []
    for h in range(SB_HEADS):
        sl = slice(h * LANES, (h + 1) * LANES)
        qs.append(_rms(q[:, sl], gq, SB_HEAD_DIM) * (SB_HEAD_DIM ** -0.5 * LOG2E))
        ks.append(_rms(k[:, sl], gk, SB_HEAD_DIM))
    return jnp.concatenate(qs, axis=1), jnp.concatenate(ks, axis=1)


def qkv_prep(qkv, gq, gk, *, ts=256):
    S = qkv.shape[0]
    W = SB_HEADS * LANES
    ts = _tile(S, ts)

    def body(q_ref, k_ref, v_ref, gq_ref, gk_ref, qn_ref, kn_ref, vb_ref):
        qn, kn = _qk_fn(q_ref[...], k_ref[...], gq_ref[...], gk_ref[...])
        qn_ref[...] = qn.astype(bf16)
        kn_ref[...] = kn.astype(bf16)
        vb_ref[...] = v_ref[...].astype(bf16)

    out = jax.ShapeDtypeStruct((S, W), bf16)
    gspec = pl.BlockSpec((1, LANES), lambda i: (0, 0))
    ospec = pl.BlockSpec((ts, W), lambda i: (i, 0))
    col = lambda c: pl.BlockSpec((ts, W), lambda i: (i, c))
    return pl.pallas_call(
        body, name="qkv_prep", grid=(S // ts,), in_specs=[col(0), col(1), col(2), gspec, gspec],
        out_specs=[ospec, ospec, ospec], out_shape=[out, out, out], compiler_params=_cp(("parallel",)),
    )(qkv, qkv, qkv, gq, gk)


def qkv_bwd(qkv, gq, gk, dqn, dkn, dv, *, ts=256):
    S = qkv.shape[0]
    W = SB_HEADS * LANES
    ts = _tile(S, ts)

    def body(q_ref, k_ref, gq_ref, gk_ref, dqn_ref, dkn_ref, dv_ref, o_ref, dgq_ref, dgk_ref):
        _, vjp = jax.vjp(_qk_fn, q_ref[...], k_ref[...], gq_ref[...], gk_ref[...])
        dq, dk, dgq, dgk = vjp((dqn_ref[...] * LN2, dkn_ref[...] * LN2))
        o_ref[:, 0:W] = dq.astype(bf16)
        o_ref[:, W:2 * W] = dk.astype(bf16)
        o_ref[:, 2 * W:3 * W] = dv_ref[...].astype(bf16)

        @pl.when(pl.program_id(0) == 0)
        def _():
            dgq_ref[...] = jnp.zeros_like(dgq_ref)
            dgk_ref[...] = jnp.zeros_like(dgk_ref)

        dgq_ref[...] += dgq
        dgk_ref[...] += dgk

    gspec = pl.BlockSpec((1, LANES), lambda i: (0, 0))
    row = pl.BlockSpec((ts, W), lambda i: (i, 0))
    col = lambda c: pl.BlockSpec((ts, W), lambda i: (i, c))
    return pl.pallas_call(
        body, name="qkv_bwd", grid=(S // ts,), in_specs=[col(0), col(1), gspec, gspec, row, row, row],
        out_specs=[pl.BlockSpec((ts, 3 * W), lambda i: (i, 0)), gspec, gspec],
        out_shape=[jax.ShapeDtypeStruct((S, 3 * W), bf16), jax.ShapeDtypeStruct((1, LANES), f32),
                   jax.ShapeDtypeStruct((1, LANES), f32)],
        compiler_params=_cp(("arbitrary",)),
    )(qkv, qkv, gq, gk, dqn, dkn, dv)


def _sb_weights(q, ks, R, masked, row, col, UU):
    ls = [_dot(q, k, _NT) for k in ks]
    lbs, lm0s, cats = [], [], []
    for l, diag in zip(ls, masked):
        lp = jnp.log2(1.0 + jnp.exp2(-jnp.abs(l)))
        lb = jnp.minimum(l, 0.0) - lp
        lm = lb - l
        if diag:
            lm = jnp.where(col < row, lm, 0.0)
        hi = lm.astype(bf16)
        lo = (lm - hi.astype(f32)).astype(bf16)
        lbs.append(lb)
        lm0s.append(lm[:, 0:1])
        cats.append(jnp.concatenate([hi, lo], axis=1))
    sums = [_dot(c, UU, _NN) for c in cats]
    ws = []
    for lb, lm0, A, diag in zip(lbs, lm0s, sums, masked):
        A = A + R
        w = jnp.exp2(lb + A)
        if diag:
            w = jnp.where(col < row, w, 0.0)
        R = A[:, 0:1] + lm0
        ws.append(w)
    return lbs, ws, R


def _tri2(tk):
    r = lax.broadcasted_iota(jnp.int32, (2 * tk, tk), 0)
    r = jnp.where(r >= tk, r - tk, r)
    c = lax.broadcasted_iota(jnp.int32, (2 * tk, tk), 1)
    return (r > c).astype(bf16)


SB_GROUP = 4


def _sweep(i, blocks_of, carry, descending):
    G = SB_GROUP
    rem, full = i % G, i // G

    def edge(r):
        if descending:
            return lambda c: blocks_of([i - u for u in range(r + 1)], c, [True] + [False] * r)
        return lambda c: blocks_of([i - r + u for u in range(r + 1)], c, [False] * r + [True])

    def body(p, c):
        first = i - rem - 1 - p * G if descending else p * G
        return blocks_of([first - u if descending else first + u for u in range(G)], c, [False] * G)

    if descending:
        carry = lax.switch(rem, [edge(r) for r in range(G)], carry)
        return lax.fori_loop(0, full, body, carry)
    carry = lax.fori_loop(0, full, body, carry)
    return lax.switch(rem, [edge(r) for r in range(G)], carry)


def sb_fwd(qn, kn, vb, *, tq=256):
    S, W = qn.shape
    H = W // LANES
    tq = _tile(S, tq)
    tk = tq
    nq = S // tq

    def body(q_ref, k_ref, v_ref, o_ref):
        i = pl.program_id(1)
        q = q_ref[...]
        row = lax.broadcasted_iota(jnp.int32, (tq, tk), 0)
        col = lax.broadcasted_iota(jnp.int32, (tq, tk), 1)
        UU = _tri2(tk)

        def blocks(js, c, masked):
            rows = [pl.ds(pl.multiple_of(j * tk, tk), tk) for j in js]
            _, ws, R = _sb_weights(q, [k_ref[r, :] for r in rows], c[0], masked, row, col, UU)
            acc = c[1]
            for w, r in zip(ws, rows):
                acc = acc + _dot(w, v_ref[r, :], _NN)
            return R, acc

        c = _sweep(i, blocks, (jnp.zeros((tq, 1), f32), jnp.zeros((tq, LANES), f32)), True)
        o_ref[...] = c[1]

    qspec = pl.BlockSpec((tq, LANES), lambda h, i: (i, h))
    kspec = pl.BlockSpec((S, LANES), lambda h, i: (0, h))
    return pl.pallas_call(
        body, name="sb_fwd", grid=(H, nq), in_specs=[qspec, kspec, kspec], out_specs=qspec,
        out_shape=jax.ShapeDtypeStruct((S, W), f32), compiler_params=_cp(("parallel", "arbitrary")),
    )(qn, kn, vb)


def sb_bwd(qn, kn, vb, do, *, tq=256):
    S, W = qn.shape
    H = W // LANES
    tq = _tile(S, tq)
    tk = tq
    nq = S // tq

    def body(q_ref, k_ref, v_ref, do_ref, dq_ref, dk_ref, dv_ref, dz_s, beta_s):
        i = pl.program_id(1)

        @pl.when(i == 0)
        def _():
            dk_ref[...] = jnp.zeros_like(dk_ref)
            dv_ref[...] = jnp.zeros_like(dv_ref)

        q = q_ref[...]
        dob = do_ref[...].astype(bf16)
        row = lax.broadcasted_iota(jnp.int32, (tq, tk), 0)
        col = lax.broadcasted_iota(jnp.int32, (tq, tk), 1)
        UU = _tri2(tk)
        Ue = (row < col).astype(bf16)

        def sweep1(js, R, masked):
            rows = [pl.ds(pl.multiple_of(j * tk, tk), tk) for j in js]
            dws = [_dot(dob, v_ref[r, :], _NT) for r in rows]
            lbs, ws, R = _sb_weights(q, [k_ref[r, :] for r in rows], R, masked, row, col, UU)
            for j, lb, w, dw in zip(js, lbs, ws, dws):
                dz_s[j] = (dw * w).astype(bf16)
                beta_s[j] = jnp.exp2(lb).astype(bf16)
            for r, w in zip(rows, ws):
                dv_ref[r, :] += _dot(w, dob, _TN)
            return R

        _sweep(i, sweep1, jnp.zeros((tq, 1), f32), True)

        def sweep2(js, c, masked):
            rows = [pl.ds(pl.multiple_of(j * tk, tk), tk) for j in js]
            dzbs = [dz_s[j] for j in js]
            sums = [_dot(dzb, Ue, _NN) for dzb in dzbs]
            Lz, dq = c
            dlbs = []
            for j, dzb, Cz, diag in zip(js, dzbs, sums, masked):
                dz = dzb.astype(f32)
                Cz = Cz + Lz
                dl = dz - beta_s[j].astype(f32) * (dz + Cz)
                if diag:
                    dl = jnp.where(col < row, dl, 0.0)
                Lz = Cz[:, tk - 1:tk] + dz[:, tk - 1:tk]
                dlbs.append(dl.astype(bf16))
            for r, dlb in zip(rows, dlbs):
                dq = dq + _dot(dlb, k_ref[r, :], _NN)
            for r, dlb in zip(rows, dlbs):
                dk_ref[r, :] += _dot(dlb, q, _TN)
            return Lz, dq

        c = _sweep(i, sweep2, (jnp.zeros((tq, 1), f32), jnp.zeros((tq, LANES), f32)), False)
        dq_ref[...] = c[1]

    qspec = pl.BlockSpec((tq, LANES), lambda h, i: (i, h))
    kspec = pl.BlockSpec((S, LANES), lambda h, i: (0, h))
    full = jax.ShapeDtypeStruct((S, W), f32)
    return pl.pallas_call(
        body, name="sb_bwd", grid=(H, nq), in_specs=[qspec, kspec, kspec, qspec],
        out_specs=[qspec, kspec, kspec], out_shape=[full, full, full],
        scratch_shapes=[pltpu.VMEM((nq, tq, tk), bf16), pltpu.VMEM((nq, tq, tk), bf16)],
        compiler_params=_cp(("parallel", "arbitrary")),
    )(qn, kn, vb, do)


def _xa_fn(qx, kv, gq, gk):
    XW = XA_HEADS * XA_HEAD_DIM
    outs = []
    for h in range(XA_HEADS):
        sl = slice(h * XA_HEAD_DIM, (h + 1) * XA_HEAD_DIM)
        qn = _rms(qx[:, sl], gq, XA_HEAD_DIM)
        kn = _rms(kv[:, sl], gk, XA_HEAD_DIM)
        v = kv[:, XW + h * XA_HEAD_DIM:XW + (h + 1) * XA_HEAD_DIM]
        s = bdot_nt(qn, kn) * (XA_HEAD_DIM ** -0.5)
        e = jnp.exp(s - lax.stop_gradient(jnp.max(s, axis=-1, keepdims=True)))
        p = e / jnp.sum(e, axis=-1, keepdims=True)
        outs.append(bdot_nn(p, v))
    return jnp.concatenate(outs, axis=1)


def xa_fwd(qx, kv, gq, gk, *, ts=256):
    S, XW = qx.shape
    M = kv.shape[0]
    ts = _tile(S, ts)

    def body(q_ref, kv_ref, gq_ref, gk_ref, o_ref):
        o_ref[...] = _xa_fn(q_ref[...], kv_ref[...], gq_ref[...], gk_ref[...]).astype(bf16)

    row = pl.BlockSpec((ts, XW), lambda i: (i, 0))
    gspec = pl.BlockSpec((1, XA_HEAD_DIM), lambda i: (0, 0))
    return pl.pallas_call(
        body, name="xa_fwd", grid=(S // ts,),
        in_specs=[row, pl.BlockSpec((M, 2 * XW), lambda i: (0, 0)), gspec, gspec], out_specs=row,
        out_shape=jax.ShapeDtypeStruct((S, XW), bf16), compiler_params=_cp(("parallel",)),
    )(qx, kv, gq, gk)


def xa_bwd(qx, kv, gq, gk, do, *, ts=256):
    S, XW = qx.shape
    M = kv.shape[0]
    ts = _tile(S, ts)

    def body(q_ref, kv_ref, gq_ref, gk_ref, do_ref, dq_ref, dkv_ref, dgq_ref, dgk_ref):
        _, vjp = jax.vjp(_xa_fn, q_ref[...], kv_ref[...], gq_ref[...], gk_ref[...])
        dq, dkv, dgq, dgk = vjp(do_ref[...].astype(f32))
        dq_ref[...] = dq.astype(bf16)

        @pl.when(pl.program_id(0) == 0)
        def _():
            dkv_ref[...] = jnp.zeros_like(dkv_ref)
            dgq_ref[...] = jnp.zeros_like(dgq_ref)
            dgk_ref[...] = jnp.zeros_like(dgk_ref)

        dkv_ref[...] += dkv
        dgq_ref[...] += dgq
        dgk_ref[...] += dgk

    row = pl.BlockSpec((ts, XW), lambda i: (i, 0))
    gspec = pl.BlockSpec((1, XA_HEAD_DIM), lambda i: (0, 0))
    kvspec = pl.BlockSpec((M, 2 * XW), lambda i: (0, 0))
    gshape = jax.ShapeDtypeStruct((1, XA_HEAD_DIM), f32)
    return pl.pallas_call(
        body, name="xa_bwd", grid=(S // ts,), in_specs=[row, kvspec, gspec, gspec, row],
        out_specs=[row, kvspec, gspec, gspec],
        out_shape=[jax.ShapeDtypeStruct((S, XW), bf16), jax.ShapeDtypeStruct((M, 2 * XW), f32), gshape, gshape],
        compiler_params=_cp(("arbitrary",)),
    )(qx, kv, gq, gk, do)


def _s5_prep_fn(a_re, a_im, ldt, bT_re, bT_im, cT_re, cT_im):
    G, P, C = SSM_GROUPS, SSM_STATE, SSM_GROUP
    GP, GC = G * P, G * C
    lg_p, lg_c = P.bit_length() - 1, C.bit_length() - 1
    gi = lax.broadcasted_iota(jnp.int32, (G, GP), 0)
    ci = lax.broadcasted_iota(jnp.int32, (G, GP), 1) >> lg_p
    expand_dt = (gi == ci).astype(f32)
    dte = jnp.dot(jnp.exp(ldt), expand_dt, precision=lax.Precision.HIGHEST, preferred_element_type=f32)
    zr, zi = a_re * dte, a_im * dte
    mag = jnp.exp(zr)
    abr, abi = mag * jnp.cos(zi), mag * jnp.sin(zi)
    nr, ni = abr - 1.0, abi
    den = a_re * a_re + a_im * a_im
    cr = (nr * a_re + ni * a_im) / den
    cim = (ni * a_re - nr * a_im) / den
    bbr = cr * bT_re - cim * bT_im
    bbi = cr * bT_im + cim * bT_re
    rowg = lax.broadcasted_iota(jnp.int32, (GC, GP), 0) >> lg_c
    colg = lax.broadcasted_iota(jnp.int32, (GC, GP), 1) >> lg_p
    diag = rowg == colg

    def expand(t):
        return jnp.where(diag, jnp.broadcast_to(t[None], (G, C, GP)).reshape(GC, GP), 0.0)

    return abr, abi, expand(bbr), expand(bbi), expand(cT_re), expand(-cT_im)


def s5_prep(a_re, a_im, ldt, bT_re, bT_im, cT_re, cT_im):
    GP, GC = SSM_GROUPS * SSM_STATE, SSM_GROUPS * SSM_GROUP

    def body(a_re_ref, a_im_ref, ldt_ref, bTr_ref, bTi_ref, cTr_ref, cTi_ref, abr_ref, abi_ref, B_ref, C_ref):
        abr, abi, Br, Bi, Cr, Ci = _s5_prep_fn(a_re_ref[...], a_im_ref[...], ldt_ref[...], bTr_ref[...],
                                               bTi_ref[...], cTr_ref[...], cTi_ref[...])
        abr_ref[...] = abr
        abi_ref[...] = abi
        B_ref[0] = Br
        B_ref[1] = Bi
        C_ref[0] = Cr
        C_ref[1] = Ci

    vec = jax.ShapeDtypeStruct((1, GP), f32)
    mat = jax.ShapeDtypeStruct((2, GC, GP), f32)
    return pl.pallas_call(body, name="s5_prep", out_shape=[vec, vec, mat, mat], compiler_params=_cp())(
        a_re, a_im, ldt, bT_re, bT_im, cT_re, cT_im)


def s5_prep_bwd(a_re, a_im, ldt, bT_re, bT_im, cT_re, cT_im, dabr, dabi, dB, dC):
    def body(a_re_ref, a_im_ref, ldt_ref, bTr_ref, bTi_ref, cTr_ref, cTi_ref, dabr_ref, dabi_ref, dB_ref, dC_ref,
             *outs):
        _, vjp = jax.vjp(_s5_prep_fn, a_re_ref[...], a_im_ref[...], ldt_ref[...], bTr_ref[...], bTi_ref[...],
                         cTr_ref[...], cTi_ref[...])
        grads = vjp((dabr_ref[...], dabi_ref[...], dB_ref[0], dB_ref[1], dC_ref[0], dC_ref[1]))
        for o_ref, gv in zip(outs, grads):
            o_ref[...] = gv

    ins = (a_re, a_im, ldt, bT_re, bT_im, cT_re, cT_im)
    return pl.pallas_call(body, name="s5_prep_bwd", out_shape=[jax.ShapeDtypeStruct(v.shape, f32) for v in ins],
                          compiler_params=_cp())(*ins, dabr, dabi, dB, dC)


def _cmul(ar, ai, br, bi):
    return ar * br - ai * bi, ar * bi + ai * br


SCAN_CHUNKS = 32


def _chunk_carry(Lr, Li, Pr, Pi, scratch, reverse):
    lr_ref, li_ref, cr_ref, ci_ref = scratch
    lr_ref[...] = Lr
    li_ref[...] = Li
    cur_r = jnp.zeros((1, LANES), f32)
    cur_i = jnp.zeros((1, LANES), f32)
    order = range(SCAN_CHUNKS - 1, -1, -1) if reverse else range(SCAN_CHUNKS)
    for c in order:
        cr_ref[pl.ds(c, 1), :] = cur_r
        ci_ref[pl.ds(c, 1), :] = cur_i
        mr, mi = _cmul(Pr, Pi, cur_r, cur_i)
        cur_r, cur_i = lr_ref[pl.ds(c, 1), :] + mr, li_ref[pl.ds(c, 1), :] + mi
    return cr_ref[...], ci_ref[...]


def _chunk_rows(j):
    return pl.ds(pl.multiple_of(j * SCAN_CHUNKS, SCAN_CHUNKS), SCAN_CHUNKS)


def row_shuffle(x, a, b, *, name, add=None, out_dtype=f32):
    S, W = x.shape
    assert a * b == S and x.dtype == f32

    def body(*refs):
        x_ref, o_ref = refs[0], refs[-1]

        def step(i, _):
            dst = pl.ds(pl.multiple_of(i * b, b), b)
            v = x_ref[pl.ds(i, b, stride=a), :]
            if add is not None:
                v = v + refs[1][dst, :]
            o_ref[dst, :] = v.astype(out_dtype)
            return 0

        lax.fori_loop(0, a, step, 0)

    col = pl.BlockSpec((S, LANES), lambda t: (0, t))
    args = [x] + ([add] if add is not None else [])
    return pl.pallas_call(
        body, name=name, grid=(W // LANES,), in_specs=[col] * len(args), out_specs=col,
        out_shape=jax.ShapeDtypeStruct((S, W), out_dtype), compiler_params=_cp(("parallel",)),
    )(*args)


def _scan_scratch(n):
    small = pltpu.VMEM((SCAN_CHUNKS, LANES), f32)
    return [pltpu.VMEM((n, LANES), f32), pltpu.VMEM((n, LANES), f32), small, small, small, small]


def scan_fwd(bu, abr, abi):
    _, S, N = bu.shape
    n = S // SCAN_CHUNKS
    shp = (SCAN_CHUNKS, LANES)

    def body(bu_ref, ar_ref, ai_ref, st_ref, pwr_ref, pwi_ref, *scratch):
        a1r, a1i = ar_ref[...], ai_ref[...]
        ar = jnp.broadcast_to(a1r, shp)
        ai = jnp.broadcast_to(a1i, shp)
        xr_ref, xi_ref = bu_ref.at[0], bu_ref.at[1]
        sr_ref, si_ref = st_ref.at[0], st_ref.at[1]

        def step(j, c):
            sr, si, pr, pi = c
            rows = _chunk_rows(j)
            mr, mi = _cmul(ar, ai, sr, si)
            sr, si = mr + xr_ref[rows, :], mi + xi_ref[rows, :]
            sr_ref[rows, :] = sr
            si_ref[rows, :] = si
            pwr_ref[pl.ds(j, 1), :] = pr
            pwi_ref[pl.ds(j, 1), :] = pi
            npr, npi = _cmul(a1r, a1i, pr, pi)
            return sr, si, npr, npi

        z = jnp.zeros(shp, f32)
        sr, si, _, _ = lax.fori_loop(0, n, step, (z, z, a1r, a1i), unroll=2)
        cr, ci = _chunk_carry(sr, si, pwr_ref[pl.ds(n - 1, 1), :], pwi_ref[pl.ds(n - 1, 1), :], scratch, False)

        def step2(j, _):
            rows = _chunk_rows(j)
            pr = jnp.broadcast_to(pwr_ref[pl.ds(j, 1), :], shp)
            pi = jnp.broadcast_to(pwi_ref[pl.ds(j, 1), :], shp)
            mr, mi = _cmul(pr, pi, cr, ci)
            sr_ref[rows, :] += mr
            si_ref[rows, :] += mi
            return 0

        lax.fori_loop(0, n, step2, 0, unroll=4)

    blk = pl.BlockSpec((2, S, LANES), lambda t: (0, 0, t))
    vec = pl.BlockSpec((1, LANES), lambda t: (0, t))
    return pl.pallas_call(
        body, name="scan_fwd", grid=(N // LANES,), in_specs=[blk, vec, vec], out_specs=blk,
        out_shape=jax.ShapeDtypeStruct((2, S, N), f32), scratch_shapes=_scan_scratch(n),
        compiler_params=_cp(("parallel",)),
    )(bu, abr, abi)


def scan_bwd(G, st, abr, abi):
    _, S, N = G.shape
    n = S // SCAN_CHUNKS
    shp = (SCAN_CHUNKS, LANES)

    def body(G_ref, st_ref, ar_ref, ai_ref, g_ref, dar_ref, dai_ref, qwr_ref, qwi_ref, *scratch):
        a1r, a1i = ar_ref[...], -ai_ref[...]
        ar = jnp.broadcast_to(a1r, shp)
        nai = jnp.broadcast_to(a1i, shp)
        Gr_ref, Gi_ref = G_ref.at[0], G_ref.at[1]
        sr_ref, si_ref = st_ref.at[0], st_ref.at[1]
        gr_ref, gi_ref = g_ref.at[0], g_ref.at[1]

        def step(jj, c):
            gr, gi, qr, qi = c
            j = n - 1 - jj
            rows = _chunk_rows(j)
            mr, mi = _cmul(ar, nai, gr, gi)
            gr, gi = mr + Gr_ref[rows, :], mi + Gi_ref[rows, :]
            gr_ref[rows, :] = gr
            gi_ref[rows, :] = gi
            qwr_ref[pl.ds(j, 1), :] = qr
            qwi_ref[pl.ds(j, 1), :] = qi
            nqr, nqi = _cmul(a1r, a1i, qr, qi)
            return gr, gi, nqr, nqi

        z = jnp.zeros(shp, f32)
        gr, gi, _, _ = lax.fori_loop(0, n, step, (z, z, a1r, a1i), unroll=2)
        cr, ci = _chunk_carry(gr, gi, qwr_ref[pl.ds(0, 1), :], qwi_ref[pl.ds(0, 1), :], scratch, True)
        sub = lax.broadcasted_iota(jnp.int32, shp, 0)

        def fix(j, spr, spi, acc):
            rows = _chunk_rows(j)
            qr = jnp.broadcast_to(qwr_ref[pl.ds(j, 1), :], shp)
            qi = jnp.broadcast_to(qwi_ref[pl.ds(j, 1), :], shp)
            mr, mi = _cmul(qr, qi, cr, ci)
            gr = gr_ref[rows, :] + mr
            gi = gi_ref[rows, :] + mi
            gr_ref[rows, :] = gr
            gi_ref[rows, :] = gi
            return acc[0] + gr * spr + gi * spi, acc[1] + gi * spr - gr * spi

        last = _chunk_rows(n - 1)
        spr = jnp.where(sub == 0, 0.0, pltpu.roll(sr_ref[last, :], 1, 0))
        spi = jnp.where(sub == 0, 0.0, pltpu.roll(si_ref[last, :], 1, 0))
        acc = fix(0, spr, spi, (z, z))

        def step2(j, acc):
            prev = _chunk_rows(j - 1)
            return fix(j, sr_ref[prev, :], si_ref[prev, :], acc)

        acc = lax.fori_loop(1, n, step2, acc)
        dar_ref[...] = jnp.sum(acc[0], axis=0, keepdims=True)
        dai_ref[...] = jnp.sum(acc[1], axis=0, keepdims=True)

    blk = pl.BlockSpec((2, S, LANES), lambda t: (0, 0, t))
    vec = pl.BlockSpec((1, LANES), lambda t: (0, t))
    vshape = jax.ShapeDtypeStruct((1, N), f32)
    return pl.pallas_call(
        body, name="scan_bwd", grid=(N // LANES,), in_specs=[blk, blk, vec, vec], out_specs=[blk, vec, vec],
        out_shape=[jax.ShapeDtypeStruct((2, S, N), f32), vshape, vshape], scratch_shapes=_scan_scratch(n),
        compiler_params=_cp(("parallel",)),
    )(G, st, abr, abi)


def _glu_fn(ypre, wglu):
    y = jax.nn.gelu(ypre)
    return y * jax.nn.sigmoid(bdot_nn(y, wglu))


def glu_fwd(ypre0, u, d, wglu, g_out, *, ts=512):
    S, W = u.shape
    ts = _tile(S, ts)

    def body(y0_ref, u_ref, d_ref, w_ref, g_ref, ypre_ref, z_ref, zn_ref):
        ypre = y0_ref[...] + d_ref[...] * u_ref[...]
        z = _glu_fn(ypre, w_ref[...])
        ypre_ref[...] = ypre
        z_ref[...] = z
        zn_ref[...] = _rms(z, g_ref[...], W).astype(bf16)

    row = pl.BlockSpec((ts, W), lambda i: (i, 0))
    vec = pl.BlockSpec((1, W), lambda i: (0, 0))
    full = jax.ShapeDtypeStruct((S, W), f32)
    return pl.pallas_call(
        body, name="glu_fwd", grid=(S // ts,),
        in_specs=[row, row, vec, pl.BlockSpec((W, W), lambda i: (0, 0)), vec], out_specs=[row, row, row],
        out_shape=[full, full, jax.ShapeDtypeStruct((S, W), bf16)], compiler_params=_cp(("parallel",)),
    )(ypre0, u, d, wglu, g_out)


def glu_bwd(ypre, u, d, wglu, dz, *, ts=512):
    S, W = u.shape
    ts = _tile(S, ts)

    def body(y_ref, u_ref, d_ref, w_ref, dz_ref, dy_ref, du_ref, dw_ref, dd_ref):
        _, vjp = jax.vjp(_glu_fn, y_ref[...], w_ref[...])
        dy, dw = vjp(dz_ref[...])
        dy_ref[...] = dy
        du_ref[...] = d_ref[...] * dy

        @pl.when(pl.program_id(0) == 0)
        def _():
            dw_ref[...] = jnp.zeros_like(dw_ref)
            dd_ref[...] = jnp.zeros_like(dd_ref)

        dw_ref[...] += dw
        dd_ref[...] += jnp.sum(dy * u_ref[...], axis=0, keepdims=True)

    row = pl.BlockSpec((ts, W), lambda i: (i, 0))
    vec = pl.BlockSpec((1, W), lambda i: (0, 0))
    sq = pl.BlockSpec((W, W), lambda i: (0, 0))
    full = jax.ShapeDtypeStruct((S, W), f32)
    return pl.pallas_call(
        body, name="glu_bwd", grid=(S // ts,), in_specs=[row, row, vec, sq, row], out_specs=[row, row, sq, vec],
        out_shape=[full, full, jax.ShapeDtypeStruct((W, W), f32), jax.ShapeDtypeStruct((1, W), f32)],
        compiler_params=_cp(("arbitrary",)),
    )(ypre, u, d, wglu, dz)


def loss_head(y, target, *, ts=512):
    S, D = y.shape
    ts = _tile(S, ts)

    def body(y_ref, t_ref, dy_ref, l_ref, dyb_ref):
        err = y_ref[...] - t_ref[...]
        dy_ref[...] = err * (1.0 / D)
        dyb_ref[...] = (err * (1.0 / D)).astype(bf16)

        @pl.when(pl.program_id(0) == 0)
        def _():
            l_ref[...] = jnp.zeros_like(l_ref)

        rows = jnp.sum(err * err, axis=1, keepdims=True) * (1.0 / D)
        l_ref[...] += 0.5 * jnp.sum(rows, axis=0, keepdims=True)

    row = pl.BlockSpec((ts, D), lambda i: (i, 0))
    return pl.pallas_call(
        body, name="loss_head", grid=(S // ts,), in_specs=[row, row],
        out_specs=[row, pl.BlockSpec((1, 1), lambda i: (0, 0)), row],
        out_shape=[jax.ShapeDtypeStruct((S, D), f32), jax.ShapeDtypeStruct((1, 1), f32),
                   jax.ShapeDtypeStruct((S, D), bf16)],
        compiler_params=_cp(("arbitrary",)),
    )(y, target)


def adamw(w, g, m, v, *, name, tr=256):
    R, C = w.shape
    tr = _row_tile(R, tr)

    def body(w_ref, g_ref, m_ref, v_ref, d_ref, nm_ref, nv_ref):
        gv = g_ref[...]
        nm = ADAM_B1 * m_ref[...] + (1.0 - ADAM_B1) * gv
        nv = ADAM_B2 * v_ref[...] + (1.0 - ADAM_B2) * jnp.square(gv)
        m_hat = nm / (1.0 - ADAM_B1 ** ADAM_STEP)
        v_hat = nv / (1.0 - ADAM_B2 ** ADAM_STEP)
        d_ref[...] = -ADAM_LR * (m_hat / (jnp.sqrt(v_hat) + ADAM_EPS) + ADAM_WD * w_ref[...])
        nm_ref[...] = nm
        nv_ref[...] = nv

    row = pl.BlockSpec((tr, C), lambda i: (i, 0))
    full = jax.ShapeDtypeStruct((R, C), f32)
    return pl.pallas_call(
        body, name=name, grid=(R // tr,), in_specs=[row] * 4, out_specs=[row] * 3, out_shape=[full] * 3,
        compiler_params=_cp(("parallel",)),
    )(w, g, m, v)


def add_half(g4, recv, c, *, name, tr=256):
    _, _, Rh, C = g4.shape
    tr = _row_tile(Rh, tr)

    def body(c_ref, a_ref, b_ref, o_ref):
        o_ref[...] = a_ref[...] + b_ref[...]

    grid_spec = pltpu.PrefetchScalarGridSpec(
        num_scalar_prefetch=1, grid=(N_CHIPS, Rh // tr),
        in_specs=[pl.BlockSpec((None, None, tr, C), lambda k, i, c_ref: (k, c_ref[0], i, 0)),
                  pl.BlockSpec((None, tr, C), lambda k, i, c_ref: (k, i, 0))],
        out_specs=pl.BlockSpec((None, tr, C), lambda k, i, c_ref: (k, i, 0)))
    return pl.pallas_call(body, name=name, grid_spec=grid_spec, out_shape=jax.ShapeDtypeStruct(recv.shape, f32),
                          compiler_params=_cp(("parallel", "parallel")))(c, g4, recv)


def sum_chips(p4, *, name, tr=256):
    _, Rh, C = p4.shape
    tr = _row_tile(Rh, tr)

    def body(a_ref, b_ref, c_ref, d_ref, o_ref):
        o_ref[...] = ((a_ref[...] + b_ref[...]) + c_ref[...]) + d_ref[...]

    spec = lambda k: pl.BlockSpec((None, tr, C), lambda i: (k, i, 0))
    return pl.pallas_call(
        body, name=name, grid=(Rh // tr,), in_specs=[spec(0), spec(1), spec(2), spec(3)],
        out_specs=pl.BlockSpec((tr, C), lambda i: (i, 0)), out_shape=jax.ShapeDtypeStruct((Rh, C), f32),
        compiler_params=_cp(("parallel",)),
    )(p4, p4, p4, p4)


def _place():
    return lax.axis_index("x"), lax.axis_index("y"), lax.axis_index("c")


def _other_chips(x, y):
    return [(1 - x, y), (x, 1 - y), (1 - x, 1 - y)]


def allgather_chips(arrs, *, name):
    n = len(arrs)

    def body(*refs):
        ins, outs = refs[:n], refs[n:2 * n]
        send, recv, loc = refs[2 * n:]
        x, y, c = _place()
        me = 2 * x + y
        peers = _other_chips(x, y)
        started = []
        for a in range(n):
            own = pltpu.make_async_copy(ins[a], outs[a].at[me], loc.at[a])
            own.start()
            started.append(own)
        for a in range(n):
            for p, (px, py) in enumerate(peers):
                cp = pltpu.make_async_remote_copy(
                    src_ref=ins[a], dst_ref=outs[a].at[me], send_sem=send.at[3 * a + p], recv_sem=recv.at[3 * a + p],
                    device_id=(px, py, c), device_id_type=MESH)
                cp.start()
        for a in range(n):
            for p, (px, py) in enumerate(peers):
                pltpu.make_async_remote_copy(
                    src_ref=ins[a], dst_ref=outs[a].at[2 * px + py], send_sem=send.at[3 * a + p],
                    recv_sem=recv.at[3 * a + p], device_id=(px, py, c), device_id_type=MESH).wait()
        for own in started:
            own.wait()

    return pl.pallas_call(
        body, name=name, in_specs=[ANY] * n, out_specs=[ANY] * n,
        out_shape=[jax.ShapeDtypeStruct((N_CHIPS,) + a.shape, a.dtype) for a in arrs],
        scratch_shapes=[pltpu.SemaphoreType.DMA((3 * n,)), pltpu.SemaphoreType.DMA((3 * n,)),
                        pltpu.SemaphoreType.DMA((n,))],
        compiler_params=pltpu.CompilerParams(has_side_effects=True),
    )(*arrs)


def sibling_swap(arrs, *, half, name):
    n = len(arrs)

    def body(*refs):
        ins, outs = refs[:n], refs[n:2 * n]
        send, recv = refs[2 * n:]
        x, y, c = _place()
        cps = []
        for a in range(n):
            src = ins[a].at[:, 1 - c] if half else ins[a]
            cp = pltpu.make_async_remote_copy(src_ref=src, dst_ref=outs[a], send_sem=send.at[a], recv_sem=recv.at[a],
                                              device_id=(x, y, 1 - c), device_id_type=MESH)
            cp.start()
            cps.append(cp)
        for cp in cps:
            cp.wait()

    def oshape(a):
        return jax.ShapeDtypeStruct((a.shape[0],) + a.shape[2:] if half else a.shape, a.dtype)

    return pl.pallas_call(
        body, name=name, in_specs=[ANY] * n, out_specs=[ANY] * n, out_shape=[oshape(a) for a in arrs],
        scratch_shapes=[pltpu.SemaphoreType.DMA((n,)), pltpu.SemaphoreType.DMA((n,))],
        compiler_params=pltpu.CompilerParams(has_side_effects=True),
    )(*arrs)


def chip_scatter(arrs, *, name):
    n = len(arrs)

    def body(*refs):
        ins, outs = refs[:n], refs[n:2 * n]
        send, recv, loc = refs[2 * n:]
        x, y, c = _place()
        me = 2 * x + y
        peers = _other_chips(x, y)
        started = []
        for a in range(n):
            own = pltpu.make_async_copy(ins[a].at[me], outs[a].at[me], loc.at[a])
            own.start()
            started.append(own)
        for a in range(n):
            for p, (px, py) in enumerate(peers):
                pltpu.make_async_remote_copy(
                    src_ref=ins[a].at[2 * px + py], dst_ref=outs[a].at[me], send_sem=send.at[3 * a + p],
                    recv_sem=recv.at[3 * a + p], device_id=(px, py, c), device_id_type=MESH).start()
        for a in range(n):
            for p, (px, py) in enumerate(peers):
                pltpu.make_async_remote_copy(
                    src_ref=ins[a].at[2 * px + py], dst_ref=outs[a].at[2 * px + py], send_sem=send.at[3 * a + p],
                    recv_sem=recv.at[3 * a + p], device_id=(px, py, c), device_id_type=MESH).wait()
        for own in started:
            own.wait()

    return pl.pallas_call(
        body, name=name, in_specs=[ANY] * n, out_specs=[ANY] * n,
        out_shape=[jax.ShapeDtypeStruct(a.shape, a.dtype) for a in arrs],
        scratch_shapes=[pltpu.SemaphoreType.DMA((3 * n,)), pltpu.SemaphoreType.DMA((3 * n,)),
                        pltpu.SemaphoreType.DMA((n,))],
        compiler_params=pltpu.CompilerParams(has_side_effects=True),
    )(*arrs)


def _pad_cols(w):
    K = w.shape[0]
    w = w.reshape(K, -1, SB_HEAD_DIM)
    return jnp.pad(w, ((0, 0), (0, 0), (0, LANES - SB_HEAD_DIM))).reshape(K, -1)


def _unpad_cols(w):
    K = w.shape[0]
    return w.reshape(K, -1, LANES)[:, :, :SB_HEAD_DIM].reshape(K, -1)


def _pad_rows(w):
    N = w.shape[1]
    w = w.reshape(-1, SB_HEAD_DIM, N)
    return jnp.pad(w, ((0, 0), (0, LANES - SB_HEAD_DIM), (0, 0))).reshape(-1, N)


def _unpad_rows(w):
    N = w.shape[1]
    return w.reshape(-1, LANES, N)[:, :SB_HEAD_DIM, :].reshape(-1, N)


_PACK_ROWS = N_CHIPS * 2 * SUBLANES


def _pack(arrs):
    flat = jnp.concatenate([a.reshape(-1) for a in arrs])
    rows = -(-flat.shape[0] // LANES)
    rows = -(-rows // _PACK_ROWS) * _PACK_ROWS
    return jnp.pad(flat, (0, rows * LANES - flat.shape[0])).reshape(rows, LANES)


def _unpack(buf, shapes):
    flat = buf.reshape(-1)
    out, pos = [], 0
    for shp in shapes:
        size = 1
        for d in shp:
            size *= d
        out.append(flat[pos:pos + size].reshape(shp))
        pos += size
    return out


BIG = ("w_in", "ssm_w_glu", "w_out", "xa_w_q", "xa_w_kv", "xa_w_o", "w_up", "w_down")
SMALL = ("g_mix", "ssm_a_re", "ssm_a_im", "ssm_log_dt", "ssm_b_re", "ssm_b_im", "ssm_c_re", "ssm_c_im", "ssm_d",
         "sb_g_q", "sb_g_k", "g_out_ssm", "g_out_sb", "g_xa", "g_mem", "xa_g_q", "xa_g_k", "g_mlp")
WEIGHTS = ("g_mix", "w_in", "ssm_a_re", "ssm_a_im", "ssm_log_dt", "ssm_b_re", "ssm_b_im", "ssm_c_re", "ssm_c_im",
           "ssm_d", "ssm_w_glu", "sb_g_q", "sb_g_k", "g_out_ssm", "g_out_sb", "w_out", "g_xa", "g_mem", "xa_w_q",
           "xa_w_kv", "xa_g_q", "xa_g_k", "xa_w_o", "g_mlp", "w_up", "w_down")


def kernel(x, mem, g_mix, w_in, ssm_a_re, ssm_a_im, ssm_log_dt, ssm_b_re, ssm_b_im, ssm_c_re, ssm_c_im, ssm_d, ssm_w_glu, sb_g_q, sb_g_k, g_out_ssm, g_out_sb, w_out, g_xa, g_mem, xa_w_q, xa_w_kv, xa_g_q, xa_g_k, xa_w_o, g_mlp, w_up, w_down, loss_target, m_g_mix, m_w_in, m_ssm_a_re, m_ssm_a_im, m_ssm_log_dt, m_ssm_b_re, m_ssm_b_im, m_ssm_c_re, m_ssm_c_im, m_ssm_d, m_ssm_w_glu, m_sb_g_q, m_sb_g_k, m_g_out_ssm, m_g_out_sb, m_w_out, m_g_xa, m_g_mem, m_xa_w_q, m_xa_w_kv, m_xa_g_q, m_xa_g_k, m_xa_w_o, m_g_mlp, m_w_up, m_w_down, v_g_mix, v_w_in, v_ssm_a_re, v_ssm_a_im, v_ssm_log_dt, v_ssm_b_re, v_ssm_b_im, v_ssm_c_re, v_ssm_c_im, v_ssm_d, v_ssm_w_glu, v_sb_g_q, v_sb_g_k, v_g_out_ssm, v_g_out_sb, v_w_out, v_g_xa, v_g_mem, v_xa_w_q, v_xa_w_kv, v_xa_g_q, v_xa_g_k, v_xa_w_o, v_g_mlp, v_w_up, v_w_down):
    env = dict(locals())
    W = {n: env[n] for n in WEIGHTS}
    M1 = {n: env["m_" + n] for n in WEIGHTS}
    V2 = {n: env["v_" + n] for n in WEIGHTS}
    xs, mems, tgt = x[0], mem[0], loss_target[0]
    S, D = xs.shape
    G, P, C = SSM_GROUPS, SSM_STATE, SSM_GROUP
    GP = G * P
    SBW = SB_HEADS * SB_HEAD_DIM
    c_idx = lax.axis_index("c")

    gath = allgather_chips([W[n][0].astype(bf16) for n in BIG], name="gather_weights")
    g_in, g_glu, g_out, g_xq, g_xkv, g_xo, g_up, g_down = gath
    Wu = g_in[0]
    Wqkv = jnp.concatenate([_pad_cols(g_in[1]), _pad_cols(g_in[2]), _pad_cols(g_in[3])], axis=1)
    Wglu = g_glu.reshape(-1, g_glu.shape[-1])
    Wout = g_out.reshape(-1, g_out.shape[-1])
    Wo_ssm, Wo_sb = Wout[:SBW], _pad_rows(Wout[SBW:])
    Wxq = g_xq.reshape(-1, g_xq.shape[-1])
    Wxkv = g_xkv.reshape(-1, g_xkv.shape[-1])
    Wxo = g_xo.transpose(1, 0, 2).reshape(g_xo.shape[1], -1)
    Wup = g_up.transpose(1, 0, 2).reshape(g_up.shape[1], -1)
    Wdown = g_down.reshape(-1, g_down.shape[-1])
    gq_pad, gk_pad = _pad_cols(sb_g_q), _pad_cols(sb_g_k)
    gosb_pad = _pad_cols(g_out_sb)
    a_re, a_im = ssm_a_re.reshape(1, GP), ssm_a_im.reshape(1, GP)
    bT_re = ssm_b_re[0].transpose(2, 0, 1).reshape(C, GP)
    bT_im = ssm_b_im[0].transpose(2, 0, 1).reshape(C, GP)
    cT_re = ssm_c_re[0].transpose(1, 0, 2).reshape(C, GP)
    cT_im = ssm_c_im[0].transpose(1, 0, 2).reshape(C, GP)
    s5_in = (a_re, a_im, ssm_log_dt, bT_re, bT_im, cT_re, cT_im)

    big = dict(tn=1024, tk=1024)
    h0 = rms_norm(xs, g_mix, D, name="norm_x")
    u = mm(h0, Wu, mode="nn", name="proj_u", tk=1024)
    qkv = mm(h0, Wqkv, mode="nn", name="proj_qkv", **big)
    qn, kn, vb = qkv_prep(qkv, gq_pad, gk_pad)
    o = sb_fwd(qn, kn, vb)
    abr, abi, Bm, Cm = s5_prep(*s5_in)
    n_pos = S // SCAN_CHUNKS
    u_il = row_shuffle(u, n_pos, SCAN_CHUNKS, name="u_interleave", out_dtype=bf16)
    bu = mm(u_il, Bm, mode="nn", name="s5_bu", b_shards=2, out_shards=2, tn=1024)
    st = scan_fwd(bu, abr, abi)
    ypre0_il = mm(st, Cm, mode="nt", name="s5_y", a_shards=2, b_shards=2, tk=1024)
    ypre0 = row_shuffle(ypre0_il, SCAN_CHUNKS, n_pos, name="y_token_order")
    ypre, z, zn = glu_fwd(ypre0, u, ssm_d, Wglu, g_out_ssm)
    on = rms_norm(o, gosb_pad, SBW, name="norm_o")
    x1a = mm(zn, Wo_ssm, mode="nn", name="out_ssm", epi="add", aux=xs, tn=1024)
    x1 = mm(on, Wo_sb, mode="nn", name="out_sb", epi="add", aux=x1a, **big)
    h1 = rms_norm(x1, g_xa, D, name="norm_x1")
    qx = mm(h1, Wxq, mode="nn", name="xa_q", tk=1024)
    memn = rms_norm(mems, g_mem, D, name="norm_mem")
    kv = mm(memn, Wxkv, mode="nn", name="xa_kv", **big)
    ox = xa_fwd(qx, kv, xa_g_q, xa_g_k)
    x2 = mm(ox, Wxo, mode="nn", name="xa_o", epi="add", aux=x1, tn=1024)
    h2 = rms_norm(x2, g_mlp, D, name="norm_x2")
    act = mm(h2, Wup, mode="nn", name="mlp_up", out_dtype=bf16, tn=2048, tk=1024)
    x3 = mm(act, Wdown, mode="nn", name="mlp_down", pro="relu2", epi="add", aux=x2, **big)
    dx3, loss_part, dx3b = loss_head(x3, tgt)
    loss = lax.psum(loss_part[0, 0], ("x", "y", "c"))

    dact = mm(dx3b, Wdown, mode="nt", name="d_act", epi="mul2relu", aux=act, out_dtype=bf16, tn=2048, tk=1024)
    dWdown = mm(act, dx3b, mode="tn", name="dw_down", pro="relu2", tm=1024, **big)
    dWup = mm(h2, dact, mode="tn", name="dw_up", out_shards=N_CHIPS, tm=1024, **big)
    dh2 = mm(dact, Wup, mode="nt", name="d_h2", **big)
    dx2, dg_mlp, dx2b = rms_bwd(x2, g_mlp, dh2, dx3, D, name="rms_bwd_mlp", twin=True)
    dox = mm(dx2b, Wxo, mode="nt", name="d_ox", out_dtype=bf16, tk=1024)
    dWxo = mm(ox, dx2b, mode="tn", name="dw_xo", out_shards=N_CHIPS, tk=1024)
    dqx, dkv, dg_xq, dg_xk = xa_bwd(qx, kv, xa_g_q, xa_g_k, dox)
    dWxq = mm(h1, dqx, mode="tn", name="dw_xq", tm=1024, tk=1024)
    dh1 = mm(dqx, Wxq, mode="nt", name="d_h1", tn=1024)
    dx1, dg_xa, dx1b = rms_bwd(x1, g_xa, dh1, dx2, D, name="rms_bwd_xa", twin=True)
    dWxkv = mm(memn, dkv, mode="tn", name="dw_xkv", tm=1024, tn=1024)
    dmemn = mm(dkv, Wxkv, mode="nt", name="d_memn", **big)
    _, dg_mem = rms_bwd(mems, g_mem, dmemn, None, D, name="rms_bwd_mem")
    dyn_ssm = mm(dx1b, Wo_ssm, mode="nt", name="d_yn_ssm", tk=1024)
    dyn_sb = mm(dx1b, Wo_sb, mode="nt", name="d_yn_sb", **big)
    dWo_ssm = mm(zn, dx1b, mode="tn", name="dw_out_ssm", **big)
    dWo_sb = mm(on, dx1b, mode="tn", name="dw_out_sb", tm=1024, **big)
    dz, dg_os = rms_bwd(z, g_out_ssm, dyn_ssm, None, SBW, name="rms_bwd_ssm")
    do, dg_osb = rms_bwd(o, gosb_pad, dyn_sb, None, SBW, name="rms_bwd_sb")
    dqn, dkn, dv = sb_bwd(qn, kn, vb, do)
    dqkv, dg_q, dg_k = qkv_bwd(qkv, gq_pad, gk_pad, dqn, dkn, dv)
    dypre, du_skip, dWglu, dd = glu_bwd(ypre, u, ssm_d, Wglu, dz)
    dypre_il = row_shuffle(dypre, n_pos, SCAN_CHUNKS, name="dy_interleave", out_dtype=bf16)
    dst = mm(dypre_il, Cm, mode="nn", name="d_states", b_shards=2, out_shards=2, tn=1024)
    dCm = mm(dypre_il, st, mode="tn", name="d_cmat", b_shards=2, out_shards=2, **big)
    gst, dabr, dabi = scan_bwd(dst, st, abr, abi)
    dBm = mm(u_il, gst, mode="tn", name="d_bmat", b_shards=2, out_shards=2, **big)
    du_il = mm(gst, Bm, mode="nt", name="d_u", a_shards=2, b_shards=2, tk=1024)
    du = row_shuffle(du_il, SCAN_CHUNKS, n_pos, name="du_token_order", add=du_skip, out_dtype=bf16)
    s5_g = s5_prep_bwd(*s5_in, dabr, dabi, dBm, dCm)
    dWu = mm(h0, du, mode="tn", name="dw_u", tm=1024, tk=1024)
    dWqkv = mm(h0, dqkv, mode="tn", name="dw_qkv", tm=1024, **big)
    dh0a = mm(du, Wu, mode="nt", name="d_h0_u", tn=1024)
    dh0 = mm(dqkv, Wqkv, mode="nt", name="d_h0_qkv", epi="add", aux=dh0a, **big)
    dx, dg_mix = rms_bwd(xs, g_mix, dh0, dx1, D, name="rms_bwd_mix")

    HW = SB_HEADS * LANES
    big_g = {
        "w_in": jnp.stack([dWu, _unpad_cols(dWqkv[:, :HW]), _unpad_cols(dWqkv[:, HW:2 * HW]),
                           _unpad_cols(dWqkv[:, 2 * HW:])]),
        "ssm_w_glu": dWglu.reshape(N_CHIPS, -1, dWglu.shape[1]),
        "w_out": jnp.concatenate([dWo_ssm, _unpad_rows(dWo_sb)]).reshape(N_CHIPS, -1, D),
        "xa_w_q": dWxq.reshape(N_CHIPS, -1, dWxq.shape[1]),
        "xa_w_kv": dWxkv.reshape(N_CHIPS, -1, dWxkv.shape[1]),
        "xa_w_o": dWxo,
        "w_up": dWup,
        "w_down": dWdown.reshape(N_CHIPS, -1, D),
    }
    da_re, da_im, dldt, dbT_re, dbT_im, dcT_re, dcT_im = s5_g
    small_g = {
        "g_mix": dg_mix, "ssm_a_re": da_re, "ssm_a_im": da_im, "ssm_log_dt": dldt,
        "ssm_b_re": dbT_re.reshape(C, G, P).transpose(1, 2, 0), "ssm_b_im": dbT_im.reshape(C, G, P).transpose(1, 2, 0),
        "ssm_c_re": dcT_re.reshape(C, G, P).transpose(1, 0, 2), "ssm_c_im": dcT_im.reshape(C, G, P).transpose(1, 0, 2),
        "ssm_d": dd, "sb_g_q": dg_q[:, :SB_HEAD_DIM], "sb_g_k": dg_k[:, :SB_HEAD_DIM], "g_out_ssm": dg_os,
        "g_out_sb": _unpad_cols(dg_osb), "g_xa": dg_xa, "g_mem": dg_mem, "xa_g_q": dg_xq, "xa_g_k": dg_xk,
        "g_mlp": dg_mlp,
    }
    packed = _pack([small_g[n] for n in SMALL])
    full_g = [big_g[n] for n in BIG] + [packed.reshape(N_CHIPS, -1, LANES)]

    g4 = [g.reshape(N_CHIPS, 2, g.shape[1] // 2, g.shape[2]) for g in full_g]
    from_sib = sibling_swap(g4, half=True, name="grad_to_sibling")
    c_arr = c_idx.astype(jnp.int32).reshape(1)
    names = list(BIG) + ["small"]
    pair = [add_half(a, b, c_arr, name="add_sibling_" + n) for a, b, n in zip(g4, from_sib, names)]
    parts = chip_scatter(pair, name="grad_to_chips")
    mine = [sum_chips(p, name="sum_chips_" + n) for p, n in zip(parts, names)]
    other = sibling_swap(mine, half=False, name="grad_half_to_sibling")
    shard = [jnp.where(c_idx == 0, jnp.concatenate([a, b]), jnp.concatenate([b, a])) for a, b in zip(mine, other)]
    small_all = allgather_chips([shard[-1]], name="gather_small")[0]
    small_red = small_all.reshape(-1, LANES)

    out = {}
    for n, gs in zip(BIG, shard[:-1]):
        shp = W[n].shape
        w2, m2, v2 = (t.reshape(gs.shape) for t in (W[n], M1[n], V2[n]))
        d, nm, nv = adamw(w2, gs, m2, v2, name="adamw_" + n)
        out[n] = tuple(t.reshape(shp) for t in (gs, d, nm, nv))
    shapes = [W[n].shape for n in SMALL]
    d, nm, nv = adamw(_pack([W[n] for n in SMALL]), small_red, _pack([M1[n] for n in SMALL]),
                      _pack([V2[n] for n in SMALL]), name="adamw_small")
    for n, gs, dd_, mm_, vv_ in zip(SMALL, _unpack(small_red, shapes), _unpack(d, shapes), _unpack(nm, shapes),
                                    _unpack(nv, shapes)):
        out[n] = (gs, dd_, mm_, vv_)
    res = [loss, dx[None]]
    for kind in range(4):
        res += [out[n][kind] for n in WEIGHTS]
    return tuple(res)
```

```python
import jax
import jax.numpy as jnp
from jax import lax
from jax.experimental import pallas as pl
from jax.experimental.pallas import tpu as pltpu

f32 = jnp.float32
bf16 = jnp.bfloat16

NORM_EPS = 1e-6
SSM_GROUPS = 32
SSM_GROUP = 16
SSM_STATE = 64
SB_HEADS = 8
SB_HEAD_DIM = 64
XA_HEADS = 4
XA_HEAD_DIM = 128
LANES = 128
SUBLANES = 8
N_CHIPS = 4
ADAM_LR = 0.001
ADAM_B1 = 0.9
ADAM_B2 = 0.999
ADAM_EPS = 1e-08
ADAM_WD = 0.01
ADAM_STEP = 10
VMEM_LIMIT = 56 * 1024 * 1024
MESH = pl.DeviceIdType.MESH
ANY = pl.BlockSpec(memory_space=pl.ANY)


def _cp(sem=None):
    return pltpu.CompilerParams(dimension_semantics=sem, vmem_limit_bytes=VMEM_LIMIT)


def _tile(n, pref):
    if n <= pref:
        return n
    t = (pref // LANES) * LANES
    while t > LANES and n % t:
        t -= LANES
    assert n % t == 0, (n, pref)
    return t


def _row_tile(n, pref):
    if n <= pref:
        return n
    t = (pref // SUBLANES) * SUBLANES
    while n % t:
        t -= SUBLANES
    return t


def _dot(a, b, dims):
    return lax.dot_general(a.astype(bf16), b.astype(bf16), (dims, ((), ())), preferred_element_type=f32)


_NN = ((1,), (0,))
_NT = ((1,), (1,))
_TN = ((0,), (0,))


@jax.custom_vjp
def bdot_nn(a, b):
    return _dot(a, b, _NN)


def _bdot_nn_fwd(a, b):
    return _dot(a, b, _NN), (a, b)


def _bdot_nn_bwd(res, g):
    a, b = res
    return _dot(g, b, _NT), _dot(a, g, _TN)


bdot_nn.defvjp(_bdot_nn_fwd, _bdot_nn_bwd)


@jax.custom_vjp
def bdot_nt(a, b):
    return _dot(a, b, _NT)


def _bdot_nt_fwd(a, b):
    return _dot(a, b, _NT), (a, b)


def _bdot_nt_bwd(res, g):
    a, b = res
    return _dot(g, b, _NN), _dot(g, a, _TN)


bdot_nt.defvjp(_bdot_nt_fwd, _bdot_nt_bwd)


def _rms(x, g, denom):
    r = lax.rsqrt(jnp.sum(x * x, axis=-1, keepdims=True) * (1.0 / denom) + NORM_EPS)
    return x * r * g


def _opspec(block, row_of, col_of, shards, ncol_tiles):
    if shards == 1:
        return pl.BlockSpec(block, lambda i, j, k: (row_of(i, j, k), col_of(i, j, k)))
    per = ncol_tiles // shards
    return pl.BlockSpec((None,) + block,
                        lambda i, j, k: (col_of(i, j, k) // per, row_of(i, j, k), col_of(i, j, k) % per))


def mm(a, b, *, mode, name, tm=512, tn=512, tk=512, pro="none", epi="none", aux=None,
       out_dtype=f32, a_shards=1, b_shards=1, out_shards=1):
    ar, ac = a.shape[-2], a.shape[-1] * a_shards
    br, bc = b.shape[-2], b.shape[-1] * b_shards
    if mode == "nn":
        M, K, N = ar, ac, bc
        assert br == K
    elif mode == "nt":
        M, K, N = ar, ac, br
        assert bc == K
    else:
        M, K, N = ac, ar, bc
        assert br == K
    tm, tn, tk = _tile(M, tm), _tile(N, tn), _tile(K, tk)
    if a_shards > 1:
        if mode == "tn":
            tm = _tile(ac // a_shards, tm)
        else:
            tk = _tile(ac // a_shards, tk)
    if b_shards > 1:
        if mode == "nt":
            tk = _tile(bc // b_shards, tk)
        else:
            tn = _tile(bc // b_shards, tn)
    if out_shards > 1:
        tn = _tile(N // out_shards, tn)
    nm, nn_, nk = M // tm, N // tn, K // tk
    I = lambda i, j, k: i
    J = lambda i, j, k: j
    Kk = lambda i, j, k: k
    if mode == "nn":
        a_spec = _opspec((tm, tk), I, Kk, a_shards, nk)
        b_spec = _opspec((tk, tn), Kk, J, b_shards, nn_)
        dims = _NN
    elif mode == "nt":
        a_spec = _opspec((tm, tk), I, Kk, a_shards, nk)
        b_spec = _opspec((tn, tk), J, Kk, b_shards, nk)
        dims = _NT
    else:
        a_spec = _opspec((tk, tm), Kk, I, a_shards, nm)
        b_spec = _opspec((tk, tn), Kk, J, b_shards, nn_)
        dims = _TN
    in_specs = [a_spec, b_spec]
    args = [a, b]
    if epi != "none":
        in_specs.append(pl.BlockSpec((tm, tn), lambda i, j, k: (i, j)))
        args.append(aux)
    if out_shards == 1:
        out_spec = pl.BlockSpec((tm, tn), lambda i, j, k: (i, j))
        out_shape = jax.ShapeDtypeStruct((M, N), out_dtype)
    else:
        per = nn_ // out_shards
        out_spec = pl.BlockSpec((None, tm, tn), lambda i, j, k: (j // per, i, j % per))
        out_shape = jax.ShapeDtypeStruct((out_shards, M, N // out_shards), out_dtype)

    def body(*refs):
        a_ref, b_ref = refs[0], refs[1]
        pos = 2
        if epi != "none":
            aux_ref = refs[pos]
            pos += 1
        o_ref, acc_ref = refs[pos], refs[pos + 1]
        k = pl.program_id(2)

        @pl.when(k == 0)
        def _():
            acc_ref[...] = jnp.zeros_like(acc_ref)

        av = a_ref[...]
        if pro == "relu2":
            av = jnp.square(jnp.maximum(av.astype(f32), 0.0))
        acc_ref[...] += _dot(av, b_ref[...], dims)

        @pl.when(k == nk - 1)
        def _():
            res = acc_ref[...]
            if epi == "add":
                res = res + aux_ref[...].astype(f32)
            elif epi == "mul2relu":
                res = res * (2.0 * jnp.maximum(aux_ref[...].astype(f32), 0.0))
            o_ref[...] = res.astype(out_dtype)

    return pl.pallas_call(
        body, name=name, grid=(nm, nn_, nk), in_specs=in_specs, out_specs=out_spec, out_shape=out_shape,
        scratch_shapes=[pltpu.VMEM((tm, tn), f32)],
        compiler_params=_cp(("parallel", "parallel", "arbitrary")),
    )(*args)


def rms_norm(x, g, denom, *, name, ts=512):
    S, D = x.shape
    ts = _tile(S, ts)

    def body(x_ref, g_ref, h_ref):
        h_ref[...] = _rms(x_ref[...], g_ref[...], denom).astype(bf16)

    row = pl.BlockSpec((ts, D), lambda i: (i, 0))
    return pl.pallas_call(
        body, name=name, grid=(S // ts,), in_specs=[row, pl.BlockSpec((1, D), lambda i: (0, 0))], out_specs=row,
        out_shape=jax.ShapeDtypeStruct((S, D), bf16), compiler_params=_cp(("parallel",)),
    )(x, g)


def rms_bwd(x, g, dy, res, denom, *, name, ts=256, twin=False):
    S, D = x.shape
    ts = _tile(S, ts)
    has_res = res is not None

    def body(*refs):
        x_ref, g_ref, dy_ref = refs[:3]
        outs = refs[4:] if has_res else refs[3:]
        _, vjp = jax.vjp(lambda xv, gv: _rms(xv, gv, denom), x_ref[...], g_ref[...])
        dx, dg = vjp(dy_ref[...])
        if has_res:
            dx = dx + refs[3][...]
        outs[0][...] = dx
        if twin:
            outs[2][...] = dx.astype(bf16)
        dg_ref = outs[1]

        @pl.when(pl.program_id(0) == 0)
        def _():
            dg_ref[...] = jnp.zeros_like(dg_ref)

        dg_ref[...] += dg

    row = pl.BlockSpec((ts, D), lambda i: (i, 0))
    vec = pl.BlockSpec((1, D), lambda i: (0, 0))
    in_specs = [row, vec, row] + ([row] if has_res else [])
    args = [x, g, dy] + ([res] if has_res else [])
    return pl.pallas_call(
        body, name=name, grid=(S // ts,), in_specs=in_specs, out_specs=[row, vec] + ([row] if twin else []),
        out_shape=[jax.ShapeDtypeStruct((S, D), f32), jax.ShapeDtypeStruct((1, D), f32)]
        + ([jax.ShapeDtypeStruct((S, D), bf16)] if twin else []),
        compiler_params=_cp(("arbitrary",)),
    )(*args)


LOG2E = 1.4426950408889634
LN2 = 0.6931471805599453


def _qk_fn(q, k, gq, gk):
    qs, ks = [], []
    for h in range(SB_HEADS):
        sl = slice(h * LANES, (h + 1) * LANES)
        qs.append(_rms(q[:, sl], gq, SB_HEAD_DIM) * (SB_HEAD_DIM ** -0.5 * LOG2E))
        ks.append(_rms(k[:, sl], gk, SB_HEAD_DIM))
    return jnp.concatenate(qs, axis=1), jnp.concatenate(ks, axis=1)


def qkv_prep(qkv, gq, gk, *, ts=256):
    S = qkv.shape[0]
    W = SB_HEADS * LANES
    ts = _tile(S, ts)

    def body(q_ref, k_ref, v_ref, gq_ref, gk_ref, qn_ref, kn_ref, vb_ref):
        qn, kn = _qk_fn(q_ref[...], k_ref[...], gq_ref[...], gk_ref[...])
        qn_ref[...] = qn.astype(bf16)
        kn_ref[...] = kn.astype(bf16)
        vb_ref[...] = v_ref[...].astype(bf16)

    out = jax.ShapeDtypeStruct((S, W), bf16)
    gspec = pl.BlockSpec((1, LANES), lambda i: (0, 0))
    ospec = pl.BlockSpec((ts, W), lambda i: (i, 0))
    col = lambda c: pl.BlockSpec((ts, W), lambda i: (i, c))
    return pl.pallas_call(
        body, name="qkv_prep", grid=(S // ts,), in_specs=[col(0), col(1), col(2), gspec, gspec],
        out_specs=[ospec, ospec, ospec], out_shape=[out, out, out], compiler_params=_cp(("parallel",)),
    )(qkv, qkv, qkv, gq, gk)


def qkv_bwd(qkv, gq, gk, dqn, dkn, dv, *, ts=256):
    S = qkv.shape[0]
    W = SB_HEADS * LANES
    ts = _tile(S, ts)

    def body(q_ref, k_ref, gq_ref, gk_ref, dqn_ref, dkn_ref, dv_ref, o_ref, dgq_ref, dgk_ref):
        _, vjp = jax.vjp(_qk_fn, q_ref[...], k_ref[...], gq_ref[...], gk_ref[...])
        dq, dk, dgq, dgk = vjp((dqn_ref[...] * LN2, dkn_ref[...] * LN2))
        o_ref[:, 0:W] = dq.astype(bf16)
        o_ref[:, W:2 * W] = dk.astype(bf16)
        o_ref[:, 2 * W:3 * W] = dv_ref[...].astype(bf16)

        @pl.when(pl.program_id(0) == 0)
        def _():
            dgq_ref[...] = jnp.zeros_like(dgq_ref)
            dgk_ref[...] = jnp.zeros_like(dgk_ref)

        dgq_ref[...] += dgq
        dgk_ref[...] += dgk

    gspec = pl.BlockSpec((1, LANES), lambda i: (0, 0))
    row = pl.BlockSpec((ts, W), lambda i: (i, 0))
    col = lambda c: pl.BlockSpec((ts, W), lambda i: (i, c))
    return pl.pallas_call(
        body, name="qkv_bwd", grid=(S // ts,), in_specs=[col(0), col(1), gspec, gspec, row, row, row],
        out_specs=[pl.BlockSpec((ts, 3 * W), lambda i: (i, 0)), gspec, gspec],
        out_shape=[jax.ShapeDtypeStruct((S, 3 * W), bf16), jax.ShapeDtypeStruct((1, LANES), f32),
                   jax.ShapeDtypeStruct((1, LANES), f32)],
        compiler_params=_cp(("arbitrary",)),
    )(qkv, qkv, gq, gk, dqn, dkn, dv)


def _sb_weights(q, ks, R, masked, row, col, UU):
    ls = [_dot(q, k, _NT) for k in ks]
    lbs, lm0s, cats = [], [], []
    for l, diag in zip(ls, masked):
        neg_abs = pltpu.bitcast(pltpu.bitcast(l, jnp.uint32) | jnp.uint32(0x80000000), f32)
        lp = jnp.log2(1.0 + jnp.exp2(neg_abs))
        lb = jnp.minimum(l, 0.0) - lp
        lm = lb - l
        if diag:
            lm = jnp.where(col < row, lm, 0.0)
        hi = lm.astype(bf16)
        lo = (lm - hi.astype(f32)).astype(bf16)
        lbs.append(lb)
        lm0s.append(lm[:, 0:1])
        cats.append(jnp.concatenate([hi, lo], axis=1))
    sums = [_dot(c, UU, _NN) for c in cats]
    ws = []
    for lb, lm0, A, diag in zip(lbs, lm0s, sums, masked):
        w = jnp.exp2(lb + (A + R))
        if diag:
            w = jnp.where(col < row, w, 0.0)
        R = R + (A[:, 0:1] + lm0)
        ws.append(w)
    return lbs, ws, R


def _tri2(tk):
    r = lax.broadcasted_iota(jnp.int32, (2 * tk, tk), 0)
    r = jnp.where(r >= tk, r - tk, r)
    c = lax.broadcasted_iota(jnp.int32, (2 * tk, tk), 1)
    return (r > c).astype(bf16)


SB_GROUP = 8


def _sweep(i, blocks_of, carry, descending):
    G = SB_GROUP
    rem, full = i % G, i // G

    def edge(r):
        if descending:
            return lambda c: blocks_of([i - u for u in range(r + 1)], c, [True] + [False] * r)
        return lambda c: blocks_of([i - r + u for u in range(r + 1)], c, [False] * r + [True])

    def body(p, c):
        first = i - rem - 1 - p * G if descending else p * G
        return blocks_of([first - u if descending else first + u for u in range(G)], c, [False] * G)

    if descending:
        carry = lax.switch(rem, [edge(r) for r in range(G)], carry)
        return lax.fori_loop(0, full, body, carry)
    carry = lax.fori_loop(0, full, body, carry)
    return lax.switch(rem, [edge(r) for r in range(G)], carry)


def sb_fwd(qn, kn, vb, gather=(), *, tq=256):
    S, W = qn.shape
    H = W // LANES
    tq = _tile(S, tq)
    tk = tq
    nq = S // tq

    n = len(gather)

    def body(*refs):
        q_ref, k_ref, v_ref = refs[:3]
        g_ins, o_ref, g_outs, sems = refs[3:3 + n], refs[3 + n], refs[4 + n:4 + 2 * n], refs[4 + 2 * n:]
        h, i = pl.program_id(0), pl.program_id(1)

        @pl.when((h == 0) & (i == 0))
        def _():
            _chip_exchange(g_ins, g_outs, sems, False, True)

        q = q_ref[...]
        row = lax.broadcasted_iota(jnp.int32, (tq, tk), 0)
        col = lax.broadcasted_iota(jnp.int32, (tq, tk), 1)
        UU = _tri2(tk)

        def blocks(js, c, masked):
            rows = [pl.ds(pl.multiple_of(j * tk, tk), tk) for j in js]
            _, ws, R = _sb_weights(q, [k_ref[r, :] for r in rows], c[0], masked, row, col, UU)
            acc = c[1]
            for w, r in zip(ws, rows):
                acc = acc + _dot(w, v_ref[r, :], _NN)
            return R, acc

        c = _sweep(i, blocks, (jnp.zeros((tq, 1), f32), jnp.zeros((tq, LANES), f32)), True)
        o_ref[...] = c[1]

        @pl.when((h == H - 1) & (i == nq - 1))
        def _():
            _chip_exchange(g_ins, g_outs, sems, False, False)

    qspec = pl.BlockSpec((tq, LANES), lambda h, i: (i, h))
    kspec = pl.BlockSpec((S, LANES), lambda h, i: (0, h))
    g_shapes, g_sems = _chip_exchange_args(gather, False)
    res = pl.pallas_call(
        body, name="sb_fwd", grid=(H, nq), in_specs=[qspec, kspec, kspec] + [ANY] * n, out_specs=[qspec] + [ANY] * n,
        out_shape=[jax.ShapeDtypeStruct((S, W), f32)] + g_shapes, scratch_shapes=g_sems,
        compiler_params=_cp(("arbitrary", "arbitrary")),
    )(qn, kn, vb, *gather)
    return res[0], res[1:]


def sb_bwd(qn, kn, vb, do, scatter=(), *, tq=256):
    S, W = qn.shape
    H = W // LANES
    tq = _tile(S, tq)
    tk = tq
    nq = S // tq
    n = len(scatter)

    def body(*refs):
        q_ref, k_ref, v_ref, do_ref = refs[:4]
        s_ins = refs[4:4 + n]
        dq_ref, dk_ref, dv_ref = refs[4 + n:7 + n]
        s_outs = refs[7 + n:7 + 2 * n]
        dz_s, beta_s = refs[7 + 2 * n:9 + 2 * n]
        sems = refs[9 + 2 * n:]
        h, i = pl.program_id(0), pl.program_id(1)

        @pl.when((h == 0) & (i == 0))
        def _():
            _chip_exchange(s_ins, s_outs, sems, True, True)

        @pl.when(i == 0)
        def _():
            dk_ref[...] = jnp.zeros_like(dk_ref)
            dv_ref[...] = jnp.zeros_like(dv_ref)

        q = q_ref[...]
        dob = do_ref[...].astype(bf16)
        row = lax.broadcasted_iota(jnp.int32, (tq, tk), 0)
        col = lax.broadcasted_iota(jnp.int32, (tq, tk), 1)
        UU = _tri2(tk)
        Ue = (row < col).astype(bf16)

        def sweep1(js, R, masked):
            rows = [pl.ds(pl.multiple_of(j * tk, tk), tk) for j in js]
            dws = [_dot(dob, v_ref[r, :], _NT) for r in rows]
            lbs, ws, R = _sb_weights(q, [k_ref[r, :] for r in rows], R, masked, row, col, UU)
            for j, lb, w, dw in zip(js, lbs, ws, dws):
                dz_s[j] = (dw * w).astype(bf16)
                beta_s[j] = jnp.exp2(lb).astype(bf16)
            for r, w in zip(rows, ws):
                dv_ref[r, :] += _dot(w, dob, _TN)
            return R

        _sweep(i, sweep1, jnp.zeros((tq, 1), f32), True)

        def sweep2(js, c, masked):
            rows = [pl.ds(pl.multiple_of(j * tk, tk), tk) for j in js]
            dzbs = [dz_s[j] for j in js]
            sums = [_dot(dzb, Ue, _NN) for dzb in dzbs]
            Lz, dq = c
            dlbs = []
            for j, dzb, Cz, diag in zip(js, dzbs, sums, masked):
                dz = dzb.astype(f32)
                dl = dz - beta_s[j].astype(f32) * (dz + (Cz + Lz))
                if diag:
                    dl = jnp.where(col < row, dl, 0.0)
                Lz = Lz + (Cz[:, tk - 1:tk] + dz[:, tk - 1:tk])
                dlbs.append(dl.astype(bf16))
            for r, dlb in zip(rows, dlbs):
                dq = dq + _dot(dlb, k_ref[r, :], _NN)
            for r, dlb in zip(rows, dlbs):
                dk_ref[r, :] += _dot(dlb, q, _TN)
            return Lz, dq

        c = _sweep(i, sweep2, (jnp.zeros((tq, 1), f32), jnp.zeros((tq, LANES), f32)), False)
        dq_ref[...] = c[1]

        @pl.when((h == H - 1) & (i == nq - 1))
        def _():
            _chip_exchange(s_ins, s_outs, sems, True, False)

    qspec = pl.BlockSpec((tq, LANES), lambda h, i: (i, h))
    kspec = pl.BlockSpec((S, LANES), lambda h, i: (0, h))
    full = jax.ShapeDtypeStruct((S, W), f32)
    s_shapes, s_sems = _chip_exchange_args(scatter, True)
    res = pl.pallas_call(
        body, name="sb_bwd", grid=(H, nq), in_specs=[qspec, kspec, kspec, qspec] + [ANY] * n,
        out_specs=[qspec, kspec, kspec] + [ANY] * n, out_shape=[full, full, full] + s_shapes,
        scratch_shapes=[pltpu.VMEM((nq, tq, tk), bf16), pltpu.VMEM((nq, tq, tk), bf16)] + s_sems,
        compiler_params=_cp(("arbitrary", "arbitrary")),
    )(qn, kn, vb, do, *scatter)
    return res[0], res[1], res[2], res[3:]


def _xa_fn(qx, kv, gq, gk):
    XW = XA_HEADS * XA_HEAD_DIM
    outs = []
    for h in range(XA_HEADS):
        sl = slice(h * XA_HEAD_DIM, (h + 1) * XA_HEAD_DIM)
        qn = _rms(qx[:, sl], gq, XA_HEAD_DIM)
        kn = _rms(kv[:, sl], gk, XA_HEAD_DIM)
        v = kv[:, XW + h * XA_HEAD_DIM:XW + (h + 1) * XA_HEAD_DIM]
        s = bdot_nt(qn, kn) * (XA_HEAD_DIM ** -0.5)
        e = jnp.exp(s - lax.stop_gradient(jnp.max(s, axis=-1, keepdims=True)))
        p = e / jnp.sum(e, axis=-1, keepdims=True)
        outs.append(bdot_nn(p, v))
    return jnp.concatenate(outs, axis=1)


def xa_fwd(qx, kv, gq, gk, *, ts=256):
    S, XW = qx.shape
    M = kv.shape[0]
    ts = _tile(S, ts)

    def body(q_ref, kv_ref, gq_ref, gk_ref, o_ref):
        o_ref[...] = _xa_fn(q_ref[...], kv_ref[...], gq_ref[...], gk_ref[...]).astype(bf16)

    row = pl.BlockSpec((ts, XW), lambda i: (i, 0))
    gspec = pl.BlockSpec((1, XA_HEAD_DIM), lambda i: (0, 0))
    return pl.pallas_call(
        body, name="xa_fwd", grid=(S // ts,),
        in_specs=[row, pl.BlockSpec((M, 2 * XW), lambda i: (0, 0)), gspec, gspec], out_specs=row,
        out_shape=jax.ShapeDtypeStruct((S, XW), bf16), compiler_params=_cp(("parallel",)),
    )(qx, kv, gq, gk)


def xa_bwd(qx, kv, gq, gk, do, *, ts=256):
    S, XW = qx.shape
    M = kv.shape[0]
    ts = _tile(S, ts)

    def body(q_ref, kv_ref, gq_ref, gk_ref, do_ref, dq_ref, dkv_ref, dgq_ref, dgk_ref):
        _, vjp = jax.vjp(_xa_fn, q_ref[...], kv_ref[...], gq_ref[...], gk_ref[...])
        dq, dkv, dgq, dgk = vjp(do_ref[...].astype(f32))
        dq_ref[...] = dq.astype(bf16)

        @pl.when(pl.program_id(0) == 0)
        def _():
            dkv_ref[...] = jnp.zeros_like(dkv_ref)
            dgq_ref[...] = jnp.zeros_like(dgq_ref)
            dgk_ref[...] = jnp.zeros_like(dgk_ref)

        dkv_ref[...] += dkv
        dgq_ref[...] += dgq
        dgk_ref[...] += dgk

    row = pl.BlockSpec((ts, XW), lambda i: (i, 0))
    gspec = pl.BlockSpec((1, XA_HEAD_DIM), lambda i: (0, 0))
    kvspec = pl.BlockSpec((M, 2 * XW), lambda i: (0, 0))
    gshape = jax.ShapeDtypeStruct((1, XA_HEAD_DIM), f32)
    return pl.pallas_call(
        body, name="xa_bwd", grid=(S // ts,), in_specs=[row, kvspec, gspec, gspec, row],
        out_specs=[row, kvspec, gspec, gspec],
        out_shape=[jax.ShapeDtypeStruct((S, XW), bf16), jax.ShapeDtypeStruct((M, 2 * XW), f32), gshape, gshape],
        compiler_params=_cp(("arbitrary",)),
    )(qx, kv, gq, gk, do)


def _s5_prep_fn(a_re, a_im, ldt, bT_re, bT_im, cT_re, cT_im):
    G, P, C = SSM_GROUPS, SSM_STATE, SSM_GROUP
    GP, GC = G * P, G * C
    lg_p, lg_c = P.bit_length() - 1, C.bit_length() - 1
    gi = lax.broadcasted_iota(jnp.int32, (G, GP), 0)
    ci = lax.broadcasted_iota(jnp.int32, (G, GP), 1) >> lg_p
    expand_dt = (gi == ci).astype(f32)
    dte = jnp.dot(jnp.exp(ldt), expand_dt, precision=lax.Precision.HIGHEST, preferred_element_type=f32)
    zr, zi = a_re * dte, a_im * dte
    mag = jnp.exp(zr)
    abr, abi = mag * jnp.cos(zi), mag * jnp.sin(zi)
    nr, ni = abr - 1.0, abi
    den = a_re * a_re + a_im * a_im
    cr = (nr * a_re + ni * a_im) / den
    cim = (ni * a_re - nr * a_im) / den
    bbr = cr * bT_re - cim * bT_im
    bbi = cr * bT_im + cim * bT_re
    rowg = lax.broadcasted_iota(jnp.int32, (GC, GP), 0) >> lg_c
    colg = lax.broadcasted_iota(jnp.int32, (GC, GP), 1) >> lg_p
    diag = rowg == colg

    def expand(t):
        return jnp.where(diag, jnp.broadcast_to(t[None], (G, C, GP)).reshape(GC, GP), 0.0)

    return abr, abi, expand(bbr), expand(bbi), expand(cT_re), expand(-cT_im)


def s5_prep(a_re, a_im, ldt, bT_re, bT_im, cT_re, cT_im):
    GP, GC = SSM_GROUPS * SSM_STATE, SSM_GROUPS * SSM_GROUP

    def body(a_re_ref, a_im_ref, ldt_ref, bTr_ref, bTi_ref, cTr_ref, cTi_ref, abr_ref, abi_ref, B_ref, C_ref):
        abr, abi, Br, Bi, Cr, Ci = _s5_prep_fn(a_re_ref[...], a_im_ref[...], ldt_ref[...], bTr_ref[...],
                                               bTi_ref[...], cTr_ref[...], cTi_ref[...])
        abr_ref[...] = abr
        abi_ref[...] = abi
        B_ref[0] = Br
        B_ref[1] = Bi
        C_ref[0] = Cr
        C_ref[1] = Ci

    vec = jax.ShapeDtypeStruct((1, GP), f32)
    mat = jax.ShapeDtypeStruct((2, GC, GP), f32)
    return pl.pallas_call(body, name="s5_prep", out_shape=[vec, vec, mat, mat], compiler_params=_cp())(
        a_re, a_im, ldt, bT_re, bT_im, cT_re, cT_im)


def s5_prep_bwd(a_re, a_im, ldt, bT_re, bT_im, cT_re, cT_im, dabr, dabi, dB, dC):
    def body(a_re_ref, a_im_ref, ldt_ref, bTr_ref, bTi_ref, cTr_ref, cTi_ref, dabr_ref, dabi_ref, dB_ref, dC_ref,
             *outs):
        _, vjp = jax.vjp(_s5_prep_fn, a_re_ref[...], a_im_ref[...], ldt_ref[...], bTr_ref[...], bTi_ref[...],
                         cTr_ref[...], cTi_ref[...])
        grads = vjp((dabr_ref[...], dabi_ref[...], dB_ref[0], dB_ref[1], dC_ref[0], dC_ref[1]))
        for o_ref, gv in zip(outs, grads):
            o_ref[...] = gv

    ins = (a_re, a_im, ldt, bT_re, bT_im, cT_re, cT_im)
    return pl.pallas_call(body, name="s5_prep_bwd", out_shape=[jax.ShapeDtypeStruct(v.shape, f32) for v in ins],
                          compiler_params=_cp())(*ins, dabr, dabi, dB, dC)


def _cmul(ar, ai, br, bi):
    return ar * br - ai * bi, ar * bi + ai * br


SCAN_CHUNKS = 32


def _chunk_carry(Lr, Li, Pr, Pi, scratch, reverse):
    lr_ref, li_ref, cr_ref, ci_ref = scratch
    lr_ref[...] = Lr
    li_ref[...] = Li
    cur_r = jnp.zeros((1, LANES), f32)
    cur_i = jnp.zeros((1, LANES), f32)
    order = range(SCAN_CHUNKS - 1, -1, -1) if reverse else range(SCAN_CHUNKS)
    for c in order:
        cr_ref[pl.ds(c, 1), :] = cur_r
        ci_ref[pl.ds(c, 1), :] = cur_i
        mr, mi = _cmul(Pr, Pi, cur_r, cur_i)
        cur_r, cur_i = lr_ref[pl.ds(c, 1), :] + mr, li_ref[pl.ds(c, 1), :] + mi
    return cr_ref[...], ci_ref[...]


def _chunk_rows(j):
    return pl.ds(pl.multiple_of(j * SCAN_CHUNKS, SCAN_CHUNKS), SCAN_CHUNKS)


def row_shuffle(x, a, b, *, name, add=None, out_dtype=f32):
    S, W = x.shape
    assert a * b == S and x.dtype == f32

    def body(*refs):
        x_ref, o_ref = refs[0], refs[-1]

        def step(i, _):
            dst = pl.ds(pl.multiple_of(i * b, b), b)
            v = x_ref[pl.ds(i, b, stride=a), :]
            if add is not None:
                v = v + refs[1][dst, :]
            o_ref[dst, :] = v.astype(out_dtype)
            return 0

        lax.fori_loop(0, a, step, 0)

    col = pl.BlockSpec((S, LANES), lambda t: (0, t))
    args = [x] + ([add] if add is not None else [])
    return pl.pallas_call(
        body, name=name, grid=(W // LANES,), in_specs=[col] * len(args), out_specs=col,
        out_shape=jax.ShapeDtypeStruct((S, W), out_dtype), compiler_params=_cp(("parallel",)),
    )(*args)


def _scan_scratch(n):
    small = pltpu.VMEM((SCAN_CHUNKS, LANES), f32)
    return [pltpu.VMEM((n, LANES), f32), pltpu.VMEM((n, LANES), f32), small, small, small, small]


def scan_fwd(bu, abr, abi):
    _, S, N = bu.shape
    n = S // SCAN_CHUNKS
    shp = (SCAN_CHUNKS, LANES)

    def body(bu_ref, ar_ref, ai_ref, st_ref, pwr_ref, pwi_ref, *scratch):
        a1r, a1i = ar_ref[...], ai_ref[...]
        ar = jnp.broadcast_to(a1r, shp)
        ai = jnp.broadcast_to(a1i, shp)
        xr_ref, xi_ref = bu_ref.at[0], bu_ref.at[1]
        sr_ref, si_ref = st_ref.at[0], st_ref.at[1]

        def step(j, c):
            sr, si, pr, pi = c
            rows = _chunk_rows(j)
            mr, mi = _cmul(ar, ai, sr, si)
            sr, si = mr + xr_ref[rows, :], mi + xi_ref[rows, :]
            sr_ref[rows, :] = sr
            si_ref[rows, :] = si
            pwr_ref[pl.ds(j, 1), :] = pr
            pwi_ref[pl.ds(j, 1), :] = pi
            npr, npi = _cmul(a1r, a1i, pr, pi)
            return sr, si, npr, npi

        z = jnp.zeros(shp, f32)
        sr, si, _, _ = lax.fori_loop(0, n, step, (z, z, a1r, a1i), unroll=2)
        cr, ci = _chunk_carry(sr, si, pwr_ref[pl.ds(n - 1, 1), :], pwi_ref[pl.ds(n - 1, 1), :], scratch, False)

        def step2(j, _):
            rows = _chunk_rows(j)
            pr = jnp.broadcast_to(pwr_ref[pl.ds(j, 1), :], shp)
            pi = jnp.broadcast_to(pwi_ref[pl.ds(j, 1), :], shp)
            mr, mi = _cmul(pr, pi, cr, ci)
            sr_ref[rows, :] += mr
            si_ref[rows, :] += mi
            return 0

        lax.fori_loop(0, n, step2, 0, unroll=4)

    blk = pl.BlockSpec((2, S, LANES), lambda t: (0, 0, t))
    vec = pl.BlockSpec((1, LANES), lambda t: (0, t))
    return pl.pallas_call(
        body, name="scan_fwd", grid=(N // LANES,), in_specs=[blk, vec, vec], out_specs=blk,
        out_shape=jax.ShapeDtypeStruct((2, S, N), f32), scratch_shapes=_scan_scratch(n),
        compiler_params=_cp(("parallel",)),
    )(bu, abr, abi)


def scan_bwd(G, st, abr, abi):
    _, S, N = G.shape
    n = S // SCAN_CHUNKS
    shp = (SCAN_CHUNKS, LANES)

    def body(G_ref, st_ref, ar_ref, ai_ref, g_ref, dar_ref, dai_ref, qwr_ref, qwi_ref, *scratch):
        a1r, a1i = ar_ref[...], -ai_ref[...]
        ar = jnp.broadcast_to(a1r, shp)
        nai = jnp.broadcast_to(a1i, shp)
        Gr_ref, Gi_ref = G_ref.at[0], G_ref.at[1]
        sr_ref, si_ref = st_ref.at[0], st_ref.at[1]
        gr_ref, gi_ref = g_ref.at[0], g_ref.at[1]

        def step(jj, c):
            gr, gi, qr, qi = c
            j = n - 1 - jj
            rows = _chunk_rows(j)
            mr, mi = _cmul(ar, nai, gr, gi)
            gr, gi = mr + Gr_ref[rows, :], mi + Gi_ref[rows, :]
            gr_ref[rows, :] = gr
            gi_ref[rows, :] = gi
            qwr_ref[pl.ds(j, 1), :] = qr
            qwi_ref[pl.ds(j, 1), :] = qi
            nqr, nqi = _cmul(a1r, a1i, qr, qi)
            return gr, gi, nqr, nqi

        z = jnp.zeros(shp, f32)
        gr, gi, _, _ = lax.fori_loop(0, n, step, (z, z, a1r, a1i), unroll=2)
        cr, ci = _chunk_carry(gr, gi, qwr_ref[pl.ds(0, 1), :], qwi_ref[pl.ds(0, 1), :], scratch, True)
        sub = lax.broadcasted_iota(jnp.int32, shp, 0)

        def fix(j, spr, spi, acc):
            rows = _chunk_rows(j)
            qr = jnp.broadcast_to(qwr_ref[pl.ds(j, 1), :], shp)
            qi = jnp.broadcast_to(qwi_ref[pl.ds(j, 1), :], shp)
            mr, mi = _cmul(qr, qi, cr, ci)
            gr = gr_ref[rows, :] + mr
            gi = gi_ref[rows, :] + mi
            gr_ref[rows, :] = gr
            gi_ref[rows, :] = gi
            return acc[0] + gr * spr + gi * spi, acc[1] + gi * spr - gr * spi

        last = _chunk_rows(n - 1)
        spr = jnp.where(sub == 0, 0.0, pltpu.roll(sr_ref[last, :], 1, 0))
        spi = jnp.where(sub == 0, 0.0, pltpu.roll(si_ref[last, :], 1, 0))
        acc = fix(0, spr, spi, (z, z))

        def step2(j, acc):
            prev = _chunk_rows(j - 1)
            return fix(j, sr_ref[prev, :], si_ref[prev, :], acc)

        acc = lax.fori_loop(1, n, step2, acc)
        dar_ref[...] = jnp.sum(acc[0], axis=0, keepdims=True)
        dai_ref[...] = jnp.sum(acc[1], axis=0, keepdims=True)

    blk = pl.BlockSpec((2, S, LANES), lambda t: (0, 0, t))
    vec = pl.BlockSpec((1, LANES), lambda t: (0, t))
    vshape = jax.ShapeDtypeStruct((1, N), f32)
    return pl.pallas_call(
        body, name="scan_bwd", grid=(N // LANES,), in_specs=[blk, blk, vec, vec], out_specs=[blk, vec, vec],
        out_shape=[jax.ShapeDtypeStruct((2, S, N), f32), vshape, vshape], scratch_shapes=_scan_scratch(n),
        compiler_params=_cp(("parallel",)),
    )(G, st, abr, abi)


def _glu_fn(ypre, wglu):
    y = jax.nn.gelu(ypre)
    return y * jax.nn.sigmoid(bdot_nn(y, wglu))


def glu_fwd(ypre0, u, d, wglu, g_out, *, ts=512):
    S, W = u.shape
    ts = _tile(S, ts)

    def body(y0_ref, u_ref, d_ref, w_ref, g_ref, ypre_ref, z_ref, zn_ref):
        ypre = y0_ref[...] + d_ref[...] * u_ref[...]
        z = _glu_fn(ypre, w_ref[...])
        ypre_ref[...] = ypre
        z_ref[...] = z
        zn_ref[...] = _rms(z, g_ref[...], W).astype(bf16)

    row = pl.BlockSpec((ts, W), lambda i: (i, 0))
    vec = pl.BlockSpec((1, W), lambda i: (0, 0))
    full = jax.ShapeDtypeStruct((S, W), f32)
    return pl.pallas_call(
        body, name="glu_fwd", grid=(S // ts,),
        in_specs=[row, row, vec, pl.BlockSpec((W, W), lambda i: (0, 0)), vec], out_specs=[row, row, row],
        out_shape=[full, full, jax.ShapeDtypeStruct((S, W), bf16)], compiler_params=_cp(("parallel",)),
    )(ypre0, u, d, wglu, g_out)


def glu_bwd(ypre, u, d, wglu, dz, *, ts=512):
    S, W = u.shape
    ts = _tile(S, ts)

    def body(y_ref, u_ref, d_ref, w_ref, dz_ref, dy_ref, du_ref, dw_ref, dd_ref):
        _, vjp = jax.vjp(_glu_fn, y_ref[...], w_ref[...])
        dy, dw = vjp(dz_ref[...])
        dy_ref[...] = dy
        du_ref[...] = d_ref[...] * dy

        @pl.when(pl.program_id(0) == 0)
        def _():
            dw_ref[...] = jnp.zeros_like(dw_ref)
            dd_ref[...] = jnp.zeros_like(dd_ref)

        dw_ref[...] += dw
        dd_ref[...] += jnp.sum(dy * u_ref[...], axis=0, keepdims=True)

    row = pl.BlockSpec((ts, W), lambda i: (i, 0))
    vec = pl.BlockSpec((1, W), lambda i: (0, 0))
    sq = pl.BlockSpec((W, W), lambda i: (0, 0))
    full = jax.ShapeDtypeStruct((S, W), f32)
    return pl.pallas_call(
        body, name="glu_bwd", grid=(S // ts,), in_specs=[row, row, vec, sq, row], out_specs=[row, row, sq, vec],
        out_shape=[full, full, jax.ShapeDtypeStruct((W, W), f32), jax.ShapeDtypeStruct((1, W), f32)],
        compiler_params=_cp(("arbitrary",)),
    )(ypre, u, d, wglu, dz)


def loss_head(y, target, *, ts=512):
    S, D = y.shape
    ts = _tile(S, ts)

    def body(y_ref, t_ref, dy_ref, l_ref, dyb_ref):
        err = y_ref[...] - t_ref[...]
        dy_ref[...] = err * (1.0 / D)
        dyb_ref[...] = (err * (1.0 / D)).astype(bf16)

        @pl.when(pl.program_id(0) == 0)
        def _():
            l_ref[...] = jnp.zeros_like(l_ref)

        rows = jnp.sum(err * err, axis=1, keepdims=True) * (1.0 / D)
        l_ref[...] += 0.5 * jnp.sum(rows, axis=0, keepdims=True)

    row = pl.BlockSpec((ts, D), lambda i: (i, 0))
    return pl.pallas_call(
        body, name="loss_head", grid=(S // ts,), in_specs=[row, row],
        out_specs=[row, pl.BlockSpec((1, 1), lambda i: (0, 0)), row],
        out_shape=[jax.ShapeDtypeStruct((S, D), f32), jax.ShapeDtypeStruct((1, 1), f32),
                   jax.ShapeDtypeStruct((S, D), bf16)],
        compiler_params=_cp(("arbitrary",)),
    )(y, target)


def adamw(w, g, m, v, *, name, tr=256):
    R, C = w.shape
    tr = _row_tile(R, tr)

    def body(w_ref, g_ref, m_ref, v_ref, d_ref, nm_ref, nv_ref):
        gv = g_ref[...]
        nm = ADAM_B1 * m_ref[...] + (1.0 - ADAM_B1) * gv
        nv = ADAM_B2 * v_ref[...] + (1.0 - ADAM_B2) * jnp.square(gv)
        m_hat = nm / (1.0 - ADAM_B1 ** ADAM_STEP)
        v_hat = nv / (1.0 - ADAM_B2 ** ADAM_STEP)
        d_ref[...] = -ADAM_LR * (m_hat / (jnp.sqrt(v_hat) + ADAM_EPS) + ADAM_WD * w_ref[...])
        nm_ref[...] = nm
        nv_ref[...] = nv

    row = pl.BlockSpec((tr, C), lambda i: (i, 0))
    full = jax.ShapeDtypeStruct((R, C), f32)
    return pl.pallas_call(
        body, name=name, grid=(R // tr,), in_specs=[row] * 4, out_specs=[row] * 3, out_shape=[full] * 3,
        compiler_params=_cp(("parallel",)),
    )(w, g, m, v)


def add_half(g4, recv, c, *, name, tr=256):
    _, _, Rh, C = g4.shape
    tr = _row_tile(Rh, tr)

    def body(c_ref, a_ref, b_ref, o_ref):
        o_ref[...] = a_ref[...] + b_ref[...]

    grid_spec = pltpu.PrefetchScalarGridSpec(
        num_scalar_prefetch=1, grid=(N_CHIPS, Rh // tr),
        in_specs=[pl.BlockSpec((None, None, tr, C), lambda k, i, c_ref: (k, c_ref[0], i, 0)),
                  pl.BlockSpec((None, tr, C), lambda k, i, c_ref: (k, i, 0))],
        out_specs=pl.BlockSpec((None, tr, C), lambda k, i, c_ref: (k, i, 0)))
    return pl.pallas_call(body, name=name, grid_spec=grid_spec, out_shape=jax.ShapeDtypeStruct(recv.shape, f32),
                          compiler_params=_cp(("parallel", "parallel")))(c, g4, recv)


def sum_chips(p4, *, name, tr=256):
    _, Rh, C = p4.shape
    tr = _row_tile(Rh, tr)

    def body(a_ref, b_ref, c_ref, d_ref, o_ref):
        o_ref[...] = ((a_ref[...] + b_ref[...]) + c_ref[...]) + d_ref[...]

    spec = lambda k: pl.BlockSpec((None, tr, C), lambda i: (k, i, 0))
    return pl.pallas_call(
        body, name=name, grid=(Rh // tr,), in_specs=[spec(0), spec(1), spec(2), spec(3)],
        out_specs=pl.BlockSpec((tr, C), lambda i: (i, 0)), out_shape=jax.ShapeDtypeStruct((Rh, C), f32),
        compiler_params=_cp(("parallel",)),
    )(p4, p4, p4, p4)


def _place():
    return lax.axis_index("x"), lax.axis_index("y"), lax.axis_index("c")


def _other_chips(x, y):
    return [(1 - x, y), (x, 1 - y), (1 - x, 1 - y)]


def _chip_exchange(ins, outs, sems, scatter, start):
    if not ins:
        return
    send, recv, loc = sems
    x, y, c = _place()
    me = 2 * x + y
    for a in range(len(ins)):
        own = pltpu.make_async_copy(ins[a].at[me] if scatter else ins[a], outs[a].at[me], loc.at[a])
        own.start() if start else own.wait()
        for p, (px, py) in enumerate(_other_chips(x, y)):
            k = 2 * px + py
            cp = pltpu.make_async_remote_copy(
                src_ref=ins[a].at[k] if scatter else ins[a], dst_ref=outs[a].at[me if start else k],
                send_sem=send.at[3 * a + p], recv_sem=recv.at[3 * a + p], device_id=(px, py, c), device_id_type=MESH)
            cp.start() if start else cp.wait()


def _chip_exchange_args(arrs, scatter):
    n = len(arrs)
    shapes = [jax.ShapeDtypeStruct(a.shape if scatter else (N_CHIPS,) + a.shape, a.dtype) for a in arrs]
    sems = [pltpu.SemaphoreType.DMA((3 * n,)), pltpu.SemaphoreType.DMA((3 * n,)), pltpu.SemaphoreType.DMA((n,))]
    return shapes, sems if n else []


def _chip_exchange_call(arrs, scatter, name):
    n = len(arrs)

    def body(*refs):
        ins, outs, sems = refs[:n], refs[n:2 * n], refs[2 * n:]
        _chip_exchange(ins, outs, sems, scatter, True)
        _chip_exchange(ins, outs, sems, scatter, False)

    shapes, sems = _chip_exchange_args(arrs, scatter)
    return pl.pallas_call(
        body, name=name, in_specs=[ANY] * n, out_specs=[ANY] * n, out_shape=shapes, scratch_shapes=sems,
        compiler_params=pltpu.CompilerParams(has_side_effects=True),
    )(*arrs)


def allgather_chips(arrs, *, name):
    return _chip_exchange_call(arrs, False, name)


def sibling_swap(arrs, *, half, name):
    n = len(arrs)

    def body(*refs):
        ins, outs = refs[:n], refs[n:2 * n]
        send, recv = refs[2 * n:]
        x, y, c = _place()
        cps = []
        for a in range(n):
            src = ins[a].at[:, 1 - c] if half else ins[a]
            cp = pltpu.make_async_remote_copy(src_ref=src, dst_ref=outs[a], send_sem=send.at[a], recv_sem=recv.at[a],
                                              device_id=(x, y, 1 - c), device_id_type=MESH)
            cp.start()
            cps.append(cp)
        for cp in cps:
            cp.wait()

    def oshape(a):
        return jax.ShapeDtypeStruct((a.shape[0],) + a.shape[2:] if half else a.shape, a.dtype)

    return pl.pallas_call(
        body, name=name, in_specs=[ANY] * n, out_specs=[ANY] * n, out_shape=[oshape(a) for a in arrs],
        scratch_shapes=[pltpu.SemaphoreType.DMA((n,)), pltpu.SemaphoreType.DMA((n,))],
        compiler_params=pltpu.CompilerParams(has_side_effects=True),
    )(*arrs)


def chip_scatter(arrs, *, name):
    return _chip_exchange_call(arrs, True, name)


def _pad_cols(w):
    K = w.shape[0]
    w = w.reshape(K, -1, SB_HEAD_DIM)
    return jnp.pad(w, ((0, 0), (0, 0), (0, LANES - SB_HEAD_DIM))).reshape(K, -1)


def _unpad_cols(w):
    K = w.shape[0]
    return w.reshape(K, -1, LANES)[:, :, :SB_HEAD_DIM].reshape(K, -1)


def _pad_rows(w):
    N = w.shape[1]
    w = w.reshape(-1, SB_HEAD_DIM, N)
    return jnp.pad(w, ((0, 0), (0, LANES - SB_HEAD_DIM), (0, 0))).reshape(-1, N)


def _unpad_rows(w):
    N = w.shape[1]
    return w.reshape(-1, LANES, N)[:, :SB_HEAD_DIM, :].reshape(-1, N)


_PACK_ROWS = N_CHIPS * 2 * SUBLANES


def _pack(arrs):
    flat = jnp.concatenate([a.reshape(-1) for a in arrs])
    rows = -(-flat.shape[0] // LANES)
    rows = -(-rows // _PACK_ROWS) * _PACK_ROWS
    return jnp.pad(flat, (0, rows * LANES - flat.shape[0])).reshape(rows, LANES)


def _unpack(buf, shapes):
    flat = buf.reshape(-1)
    out, pos = [], 0
    for shp in shapes:
        size = 1
        for d in shp:
            size *= d
        out.append(flat[pos:pos + size].reshape(shp))
        pos += size
    return out


BIG = ("w_in", "ssm_w_glu", "w_out", "xa_w_q", "xa_w_kv", "xa_w_o", "w_up", "w_down")
SMALL = ("g_mix", "ssm_a_re", "ssm_a_im", "ssm_log_dt", "ssm_b_re", "ssm_b_im", "ssm_c_re", "ssm_c_im", "ssm_d",
         "sb_g_q", "sb_g_k", "g_out_ssm", "g_out_sb", "g_xa", "g_mem", "xa_g_q", "xa_g_k", "g_mlp")
WEIGHTS = ("g_mix", "w_in", "ssm_a_re", "ssm_a_im", "ssm_log_dt", "ssm_b_re", "ssm_b_im", "ssm_c_re", "ssm_c_im",
           "ssm_d", "ssm_w_glu", "sb_g_q", "sb_g_k", "g_out_ssm", "g_out_sb", "w_out", "g_xa", "g_mem", "xa_w_q",
           "xa_w_kv", "xa_g_q", "xa_g_k", "xa_w_o", "g_mlp", "w_up", "w_down")


def kernel(x, mem, g_mix, w_in, ssm_a_re, ssm_a_im, ssm_log_dt, ssm_b_re, ssm_b_im, ssm_c_re, ssm_c_im, ssm_d, ssm_w_glu, sb_g_q, sb_g_k, g_out_ssm, g_out_sb, w_out, g_xa, g_mem, xa_w_q, xa_w_kv, xa_g_q, xa_g_k, xa_w_o, g_mlp, w_up, w_down, loss_target, m_g_mix, m_w_in, m_ssm_a_re, m_ssm_a_im, m_ssm_log_dt, m_ssm_b_re, m_ssm_b_im, m_ssm_c_re, m_ssm_c_im, m_ssm_d, m_ssm_w_glu, m_sb_g_q, m_sb_g_k, m_g_out_ssm, m_g_out_sb, m_w_out, m_g_xa, m_g_mem, m_xa_w_q, m_xa_w_kv, m_xa_g_q, m_xa_g_k, m_xa_w_o, m_g_mlp, m_w_up, m_w_down, v_g_mix, v_w_in, v_ssm_a_re, v_ssm_a_im, v_ssm_log_dt, v_ssm_b_re, v_ssm_b_im, v_ssm_c_re, v_ssm_c_im, v_ssm_d, v_ssm_w_glu, v_sb_g_q, v_sb_g_k, v_g_out_ssm, v_g_out_sb, v_w_out, v_g_xa, v_g_mem, v_xa_w_q, v_xa_w_kv, v_xa_g_q, v_xa_g_k, v_xa_w_o, v_g_mlp, v_w_up, v_w_down):
    env = dict(locals())
    W = {n: env[n] for n in WEIGHTS}
    M1 = {n: env["m_" + n] for n in WEIGHTS}
    V2 = {n: env["v_" + n] for n in WEIGHTS}
    xs, mems, tgt = x[0], mem[0], loss_target[0]
    S, D = xs.shape
    G, P, C = SSM_GROUPS, SSM_STATE, SSM_GROUP
    GP = G * P
    SBW = SB_HEADS * SB_HEAD_DIM
    c_idx = lax.axis_index("c")

    (g_in,) = allgather_chips([w_in[0].astype(bf16)], name="gather_w_in")
    Wu = g_in[0]
    Wqkv = jnp.concatenate([_pad_cols(g_in[1]), _pad_cols(g_in[2]), _pad_cols(g_in[3])], axis=1)
    gq_pad, gk_pad = _pad_cols(sb_g_q), _pad_cols(sb_g_k)
    gosb_pad = _pad_cols(g_out_sb)
    a_re, a_im = ssm_a_re.reshape(1, GP), ssm_a_im.reshape(1, GP)
    bT_re = ssm_b_re[0].transpose(2, 0, 1).reshape(C, GP)
    bT_im = ssm_b_im[0].transpose(2, 0, 1).reshape(C, GP)
    cT_re = ssm_c_re[0].transpose(1, 0, 2).reshape(C, GP)
    cT_im = ssm_c_im[0].transpose(1, 0, 2).reshape(C, GP)
    s5_in = (a_re, a_im, ssm_log_dt, bT_re, bT_im, cT_re, cT_im)

    big = dict(tn=1024, tk=1024)
    h0 = rms_norm(xs, g_mix, D, name="norm_x")
    u = mm(h0, Wu, mode="nn", name="proj_u", tk=1024)
    qkv = mm(h0, Wqkv, mode="nn", name="proj_qkv", **big)
    qn, kn, vb = qkv_prep(qkv, gq_pad, gk_pad)
    o, gath = sb_fwd(qn, kn, vb, [W[n][0].astype(bf16) for n in BIG[1:]])
    g_glu, g_out, g_xq, g_xkv, g_xo, g_up, g_down = gath
    Wglu = g_glu.reshape(-1, g_glu.shape[-1])
    Wout = g_out.reshape(-1, g_out.shape[-1])
    Wo_ssm, Wo_sb = Wout[:SBW], _pad_rows(Wout[SBW:])
    Wxq = g_xq.reshape(-1, g_xq.shape[-1])
    Wxkv = g_xkv.reshape(-1, g_xkv.shape[-1])
    Wxo = g_xo.transpose(1, 0, 2).reshape(g_xo.shape[1], -1)
    Wup = g_up.transpose(1, 0, 2).reshape(g_up.shape[1], -1)
    Wdown = g_down.reshape(-1, g_down.shape[-1])
    abr, abi, Bm, Cm = s5_prep(*s5_in)
    n_pos = S // SCAN_CHUNKS
    u_il = row_shuffle(u, n_pos, SCAN_CHUNKS, name="u_interleave", out_dtype=bf16)
    bu = mm(u_il, Bm, mode="nn", name="s5_bu", b_shards=2, out_shards=2, tn=1024)
    st = scan_fwd(bu, abr, abi)
    ypre0_il = mm(st, Cm, mode="nt", name="s5_y", a_shards=2, b_shards=2, tk=1024)
    ypre0 = row_shuffle(ypre0_il, SCAN_CHUNKS, n_pos, name="y_token_order")
    ypre, z, zn = glu_fwd(ypre0, u, ssm_d, Wglu, g_out_ssm)
    on = rms_norm(o, gosb_pad, SBW, name="norm_o")
    x1a = mm(zn, Wo_ssm, mode="nn", name="out_ssm", epi="add", aux=xs, tn=1024)
    x1 = mm(on, Wo_sb, mode="nn", name="out_sb", epi="add", aux=x1a, **big)
    h1 = rms_norm(x1, g_xa, D, name="norm_x1")
    qx = mm(h1, Wxq, mode="nn", name="xa_q", tk=1024)
    memn = rms_norm(mems, g_mem, D, name="norm_mem")
    kv = mm(memn, Wxkv, mode="nn", name="xa_kv", **big)
    ox = xa_fwd(qx, kv, xa_g_q, xa_g_k)
    x2 = mm(ox, Wxo, mode="nn", name="xa_o", epi="add", aux=x1, tn=1024)
    h2 = rms_norm(x2, g_mlp, D, name="norm_x2")
    act = mm(h2, Wup, mode="nn", name="mlp_up", out_dtype=bf16, tn=2048, tk=1024)
    x3 = mm(act, Wdown, mode="nn", name="mlp_down", pro="relu2", epi="add", aux=x2, **big)
    dx3, loss_part, dx3b = loss_head(x3, tgt)
    loss = lax.psum(loss_part[0, 0], ("x", "y", "c"))

    dact = mm(dx3b, Wdown, mode="nt", name="d_act", epi="mul2relu", aux=act, out_dtype=bf16, tn=2048, tk=1024)
    dWdown = mm(act, dx3b, mode="tn", name="dw_down", pro="relu2", tm=1024, **big)
    dWup = mm(h2, dact, mode="tn", name="dw_up", out_shards=N_CHIPS, tm=1024, **big)
    dh2 = mm(dact, Wup, mode="nt", name="d_h2", **big)
    dx2, dg_mlp, dx2b = rms_bwd(x2, g_mlp, dh2, dx3, D, name="rms_bwd_mlp", twin=True)
    dox = mm(dx2b, Wxo, mode="nt", name="d_ox", out_dtype=bf16, tk=1024)
    dWxo = mm(ox, dx2b, mode="tn", name="dw_xo", out_shards=N_CHIPS, tk=1024)
    dqx, dkv, dg_xq, dg_xk = xa_bwd(qx, kv, xa_g_q, xa_g_k, dox)
    dWxq = mm(h1, dqx, mode="tn", name="dw_xq", tm=1024, tk=1024)
    dh1 = mm(dqx, Wxq, mode="nt", name="d_h1", tn=1024)
    dx1, dg_xa, dx1b = rms_bwd(x1, g_xa, dh1, dx2, D, name="rms_bwd_xa", twin=True)
    dWxkv = mm(memn, dkv, mode="tn", name="dw_xkv", tm=1024, tn=1024)
    dmemn = mm(dkv, Wxkv, mode="nt", name="d_memn", **big)
    _, dg_mem = rms_bwd(mems, g_mem, dmemn, None, D, name="rms_bwd_mem")
    dyn_ssm = mm(dx1b, Wo_ssm, mode="nt", name="d_yn_ssm", tk=1024)
    dyn_sb = mm(dx1b, Wo_sb, mode="nt", name="d_yn_sb", **big)
    dWo_ssm = mm(zn, dx1b, mode="tn", name="dw_out_ssm", **big)
    dWo_sb = mm(on, dx1b, mode="tn", name="dw_out_sb", tm=1024, **big)
    dz, dg_os = rms_bwd(z, g_out_ssm, dyn_ssm, None, SBW, name="rms_bwd_ssm")
    do, dg_osb = rms_bwd(o, gosb_pad, dyn_sb, None, SBW, name="rms_bwd_sb")
    c_arr = c_idx.astype(jnp.int32).reshape(1)

    def sibling_sums(grads, names, tag):
        g4 = [g.reshape(N_CHIPS, 2, g.shape[1] // 2, g.shape[2]) for g in grads]
        from_sib = sibling_swap(g4, half=True, name="grad_to_sibling_" + tag)
        return [add_half(a, b, c_arr, name="add_sibling_" + n) for a, b, n in zip(g4, from_sib, names)]

    early = ("xa_w_q", "xa_w_kv", "xa_w_o", "w_up", "w_down")
    early_g = [dWxq.reshape(N_CHIPS, -1, dWxq.shape[1]), dWxkv.reshape(N_CHIPS, -1, dWxkv.shape[1]), dWxo, dWup,
               dWdown.reshape(N_CHIPS, -1, D)]
    dqn, dkn, dv, parts_early = sb_bwd(qn, kn, vb, do, sibling_sums(early_g, early, "early"))
    mine = {n: sum_chips(p, name="sum_chips_" + n) for n, p in zip(early, parts_early)}
    dqkv, dg_q, dg_k = qkv_bwd(qkv, gq_pad, gk_pad, dqn, dkn, dv)
    dypre, du_skip, dWglu, dd = glu_bwd(ypre, u, ssm_d, Wglu, dz)
    dypre_il = row_shuffle(dypre, n_pos, SCAN_CHUNKS, name="dy_interleave", out_dtype=bf16)
    dst = mm(dypre_il, Cm, mode="nn", name="d_states", b_shards=2, out_shards=2, tn=1024)
    dCm = mm(dypre_il, st, mode="tn", name="d_cmat", b_shards=2, out_shards=2, **big)
    gst, dabr, dabi = scan_bwd(dst, st, abr, abi)
    dBm = mm(u_il, gst, mode="tn", name="d_bmat", b_shards=2, out_shards=2, **big)
    du_il = mm(gst, Bm, mode="nt", name="d_u", a_shards=2, b_shards=2, tk=1024)
    du = row_shuffle(du_il, SCAN_CHUNKS, n_pos, name="du_token_order", add=du_skip, out_dtype=bf16)
    s5_g = s5_prep_bwd(*s5_in, dabr, dabi, dBm, dCm)
    dWu = mm(h0, du, mode="tn", name="dw_u", tm=1024, tk=1024)
    dWqkv = mm(h0, dqkv, mode="tn", name="dw_qkv", tm=1024, **big)
    dh0a = mm(du, Wu, mode="nt", name="d_h0_u", tn=1024)
    dh0 = mm(dqkv, Wqkv, mode="nt", name="d_h0_qkv", epi="add", aux=dh0a, **big)
    dx, dg_mix = rms_bwd(xs, g_mix, dh0, dx1, D, name="rms_bwd_mix")

    HW = SB_HEADS * LANES
    late = ("w_in", "ssm_w_glu", "w_out", "small")
    late_g = [jnp.stack([dWu, _unpad_cols(dWqkv[:, :HW]), _unpad_cols(dWqkv[:, HW:2 * HW]),
                         _unpad_cols(dWqkv[:, 2 * HW:])]),
              dWglu.reshape(N_CHIPS, -1, dWglu.shape[1]),
              jnp.concatenate([dWo_ssm, _unpad_rows(dWo_sb)]).reshape(N_CHIPS, -1, D)]
    da_re, da_im, dldt, dbT_re, dbT_im, dcT_re, dcT_im = s5_g
    small_g = {
        "g_mix": dg_mix, "ssm_a_re": da_re, "ssm_a_im": da_im, "ssm_log_dt": dldt,
        "ssm_b_re": dbT_re.reshape(C, G, P).transpose(1, 2, 0), "ssm_b_im": dbT_im.reshape(C, G, P).transpose(1, 2, 0),
        "ssm_c_re": dcT_re.reshape(C, G, P).transpose(1, 0, 2), "ssm_c_im": dcT_im.reshape(C, G, P).transpose(1, 0, 2),
        "ssm_d": dd, "sb_g_q": dg_q[:, :SB_HEAD_DIM], "sb_g_k": dg_k[:, :SB_HEAD_DIM], "g_out_ssm": dg_os,
        "g_out_sb": _unpad_cols(dg_osb), "g_xa": dg_xa, "g_mem": dg_mem, "xa_g_q": dg_xq, "xa_g_k": dg_xk,
        "g_mlp": dg_mlp,
    }
    late_g.append(_pack([small_g[n] for n in SMALL]).reshape(N_CHIPS, -1, LANES))

    parts_late = chip_scatter(sibling_sums(late_g, late, "late"), name="grad_to_chips_late")
    mine.update({n: sum_chips(p, name="sum_chips_" + n) for n, p in zip(late, parts_late)})
    mine = [mine[n] for n in list(BIG) + ["small"]]
    other = sibling_swap(mine, half=False, name="grad_half_to_sibling")
    shard = [jnp.where(c_idx == 0, jnp.concatenate([a, b]), jnp.concatenate([b, a])) for a, b in zip(mine, other)]
    small_all = allgather_chips([shard[-1]], name="gather_small")[0]
    small_red = small_all.reshape(-1, LANES)

    out = {}
    for n, gs in zip(BIG, shard[:-1]):
        shp = W[n].shape
        w2, m2, v2 = (t.reshape(gs.shape) for t in (W[n], M1[n], V2[n]))
        d, nm, nv = adamw(w2, gs, m2, v2, name="adamw_" + n)
        out[n] = tuple(t.reshape(shp) for t in (gs, d, nm, nv))
    shapes = [W[n].shape for n in SMALL]
    d, nm, nv = adamw(_pack([W[n] for n in SMALL]), small_red, _pack([M1[n] for n in SMALL]),
                      _pack([V2[n] for n in SMALL]), name="adamw_small")
    for n, gs, dd_, mm_, vv_ in zip(SMALL, _unpack(small_red, shapes), _unpack(d, shapes), _unpack(nm, shapes),
                                    _unpack(nv, shapes)):
        out[n] = (gs, dd_, mm_, vv_)
    res = [loss, dx[None]]
    for kind in range(4):
        res += [out[n][kind] for n in WEIGHTS]
    return tuple(res)
```

```python
import jax
import jax.numpy as jnp
from jax import lax
from jax.experimental import pallas as pl
from jax.experimental.pallas import tpu as pltpu

f32 = jnp.float32
bf16 = jnp.bfloat16

NORM_EPS = 1e-6
SSM_GROUPS = 32
SSM_GROUP = 16
SSM_STATE = 64
SB_HEADS = 8
SB_HEAD_DIM = 64
XA_HEADS = 4
XA_HEAD_DIM = 128
LANES = 128
SUBLANES = 8
N_CHIPS = 4
ADAM_LR = 0.001
ADAM_B1 = 0.9
ADAM_B2 = 0.999
ADAM_EPS = 1e-08
ADAM_WD = 0.01
ADAM_STEP = 10
VMEM_LIMIT = 56 * 1024 * 1024
MESH = pl.DeviceIdType.MESH
ANY = pl.BlockSpec(memory_space=pl.ANY)


def _cp(sem=None):
    return pltpu.CompilerParams(dimension_semantics=sem, vmem_limit_bytes=VMEM_LIMIT)


def _tile(n, pref):
    if n <= pref:
        return n
    t = (pref // LANES) * LANES
    while t > LANES and n % t:
        t -= LANES
    assert n % t == 0, (n, pref)
    return t


def _row_tile(n, pref):
    if n <= pref:
        return n
    t = (pref // SUBLANES) * SUBLANES
    while n % t:
        t -= SUBLANES
    return t


def _dot(a, b, dims):
    return lax.dot_general(a.astype(bf16), b.astype(bf16), (dims, ((), ())), preferred_element_type=f32)


_NN = ((1,), (0,))
_NT = ((1,), (1,))
_TN = ((0,), (0,))


@jax.custom_vjp
def bdot_nn(a, b):
    return _dot(a, b, _NN)


def _bdot_nn_fwd(a, b):
    return _dot(a, b, _NN), (a, b)


def _bdot_nn_bwd(res, g):
    a, b = res
    return _dot(g, b, _NT), _dot(a, g, _TN)


bdot_nn.defvjp(_bdot_nn_fwd, _bdot_nn_bwd)


@jax.custom_vjp
def bdot_nt(a, b):
    return _dot(a, b, _NT)


def _bdot_nt_fwd(a, b):
    return _dot(a, b, _NT), (a, b)


def _bdot_nt_bwd(res, g):
    a, b = res
    return _dot(g, b, _NN), _dot(g, a, _TN)


bdot_nt.defvjp(_bdot_nt_fwd, _bdot_nt_bwd)


def _rms(x, g, denom):
    r = lax.rsqrt(jnp.sum(x * x, axis=-1, keepdims=True) * (1.0 / denom) + NORM_EPS)
    return x * r * g


def _opspec(block, row_of, col_of, shards, ncol_tiles):
    if shards == 1:
        return pl.BlockSpec(block, lambda i, j, k: (row_of(i, j, k), col_of(i, j, k)))
    per = ncol_tiles // shards
    return pl.BlockSpec((None,) + block,
                        lambda i, j, k: (col_of(i, j, k) // per, row_of(i, j, k), col_of(i, j, k) % per))


def mm(a, b, *, mode, name, tm=512, tn=512, tk=512, pro="none", epi="none", aux=None,
       out_dtype=f32, a_shards=1, b_shards=1, out_shards=1):
    ar, ac = a.shape[-2], a.shape[-1] * a_shards
    br, bc = b.shape[-2], b.shape[-1] * b_shards
    if mode == "nn":
        M, K, N = ar, ac, bc
        assert br == K
    elif mode == "nt":
        M, K, N = ar, ac, br
        assert bc == K
    else:
        M, K, N = ac, ar, bc
        assert br == K
    tm, tn, tk = _tile(M, tm), _tile(N, tn), _tile(K, tk)
    if a_shards > 1:
        if mode == "tn":
            tm = _tile(ac // a_shards, tm)
        else:
            tk = _tile(ac // a_shards, tk)
    if b_shards > 1:
        if mode == "nt":
            tk = _tile(bc // b_shards, tk)
        else:
            tn = _tile(bc // b_shards, tn)
    if out_shards > 1:
        tn = _tile(N // out_shards, tn)
    nm, nn_, nk = M // tm, N // tn, K // tk
    I = lambda i, j, k: i
    J = lambda i, j, k: j
    Kk = lambda i, j, k: k
    if mode == "nn":
        a_spec = _opspec((tm, tk), I, Kk, a_shards, nk)
        b_spec = _opspec((tk, tn), Kk, J, b_shards, nn_)
        dims = _NN
    elif mode == "nt":
        a_spec = _opspec((tm, tk), I, Kk, a_shards, nk)
        b_spec = _opspec((tn, tk), J, Kk, b_shards, nk)
        dims = _NT
    else:
        a_spec = _opspec((tk, tm), Kk, I, a_shards, nm)
        b_spec = _opspec((tk, tn), Kk, J, b_shards, nn_)
        dims = _TN
    in_specs = [a_spec, b_spec]
    args = [a, b]
    if epi != "none":
        in_specs.append(pl.BlockSpec((tm, tn), lambda i, j, k: (i, j)))
        args.append(aux)
    if out_shards == 1:
        out_spec = pl.BlockSpec((tm, tn), lambda i, j, k: (i, j))
        out_shape = jax.ShapeDtypeStruct((M, N), out_dtype)
    else:
        per = nn_ // out_shards
        out_spec = pl.BlockSpec((None, tm, tn), lambda i, j, k: (j // per, i, j % per))
        out_shape = jax.ShapeDtypeStruct((out_shards, M, N // out_shards), out_dtype)

    def body(*refs):
        a_ref, b_ref = refs[0], refs[1]
        pos = 2
        if epi != "none":
            aux_ref = refs[pos]
            pos += 1
        o_ref, acc_ref = refs[pos], refs[pos + 1]
        k = pl.program_id(2)

        @pl.when(k == 0)
        def _():
            acc_ref[...] = jnp.zeros_like(acc_ref)

        av = a_ref[...]
        if pro == "relu2":
            av = jnp.square(jnp.maximum(av.astype(f32), 0.0))
        acc_ref[...] += _dot(av, b_ref[...], dims)

        @pl.when(k == nk - 1)
        def _():
            res = acc_ref[...]
            if epi == "add":
                res = res + aux_ref[...].astype(f32)
            elif epi == "mul2relu":
                res = res * (2.0 * jnp.maximum(aux_ref[...].astype(f32), 0.0))
            o_ref[...] = res.astype(out_dtype)

    return pl.pallas_call(
        body, name=name, grid=(nm, nn_, nk), in_specs=in_specs, out_specs=out_spec, out_shape=out_shape,
        scratch_shapes=[pltpu.VMEM((tm, tn), f32)],
        compiler_params=_cp(("parallel", "parallel", "arbitrary")),
    )(*args)


def rms_norm(x, g, denom, *, name, ts=512):
    S, D = x.shape
    ts = _tile(S, ts)

    def body(x_ref, g_ref, h_ref):
        h_ref[...] = _rms(x_ref[...], g_ref[...], denom).astype(bf16)

    row = pl.BlockSpec((ts, D), lambda i: (i, 0))
    return pl.pallas_call(
        body, name=name, grid=(S // ts,), in_specs=[row, pl.BlockSpec((1, D), lambda i: (0, 0))], out_specs=row,
        out_shape=jax.ShapeDtypeStruct((S, D), bf16), compiler_params=_cp(("parallel",)),
    )(x, g)


def rms_bwd(x, g, dy, res, denom, *, name, ts=256, twin=False):
    S, D = x.shape
    ts = _tile(S, ts)
    has_res = res is not None

    def body(*refs):
        x_ref, g_ref, dy_ref = refs[:3]
        outs = refs[4:] if has_res else refs[3:]
        _, vjp = jax.vjp(lambda xv, gv: _rms(xv, gv, denom), x_ref[...], g_ref[...])
        dx, dg = vjp(dy_ref[...])
        if has_res:
            dx = dx + refs[3][...]
        outs[0][...] = dx
        if twin:
            outs[2][...] = dx.astype(bf16)
        dg_ref = outs[1]

        @pl.when(pl.program_id(0) == 0)
        def _():
            dg_ref[...] = jnp.zeros_like(dg_ref)

        dg_ref[...] += dg

    row = pl.BlockSpec((ts, D), lambda i: (i, 0))
    vec = pl.BlockSpec((1, D), lambda i: (0, 0))
    in_specs = [row, vec, row] + ([row] if has_res else [])
    args = [x, g, dy] + ([res] if has_res else [])
    return pl.pallas_call(
        body, name=name, grid=(S // ts,), in_specs=in_specs, out_specs=[row, vec] + ([row] if twin else []),
        out_shape=[jax.ShapeDtypeStruct((S, D), f32), jax.ShapeDtypeStruct((1, D), f32)]
        + ([jax.ShapeDtypeStruct((S, D), bf16)] if twin else []),
        compiler_params=_cp(("arbitrary",)),
    )(*args)


LOG2E = 1.4426950408889634
LN2 = 0.6931471805599453


def _qk_fn(q, k, gq, gk):
    qs, ks = [], []
    for h in range(SB_HEADS):
        sl = slice(h * LANES, (h + 1) * LANES)
        qs.append(_rms(q[:, sl], gq, SB_HEAD_DIM) * (SB_HEAD_DIM ** -0.5 * LOG2E))
        ks.append(_rms(k[:, sl], gk, SB_HEAD_DIM))
    return jnp.concatenate(qs, axis=1), jnp.concatenate(ks, axis=1)


def qkv_prep(qkv, gq, gk, *, ts=256):
    S = qkv.shape[0]
    W = SB_HEADS * LANES
    ts = _tile(S, ts)

    def body(q_ref, k_ref, v_ref, gq_ref, gk_ref, qn_ref, kn_ref, vb_ref):
        qn, kn = _qk_fn(q_ref[...], k_ref[...], gq_ref[...], gk_ref[...])
        qn_ref[...] = qn.astype(bf16)
        kn_ref[...] = kn.astype(bf16)
        vb_ref[...] = v_ref[...].astype(bf16)

    out = jax.ShapeDtypeStruct((S, W), bf16)
    gspec = pl.BlockSpec((1, LANES), lambda i: (0, 0))
    ospec = pl.BlockSpec((ts, W), lambda i: (i, 0))
    col = lambda c: pl.BlockSpec((ts, W), lambda i: (i, c))
    return pl.pallas_call(
        body, name="qkv_prep", grid=(S // ts,), in_specs=[col(0), col(1), col(2), gspec, gspec],
        out_specs=[ospec, ospec, ospec], out_shape=[out, out, out], compiler_params=_cp(("parallel",)),
    )(qkv, qkv, qkv, gq, gk)


def qkv_bwd(qkv, gq, gk, dqn, dkn, dv, *, ts=256):
    S = qkv.shape[0]
    W = SB_HEADS * LANES
    ts = _tile(S, ts)

    def body(q_ref, k_ref, gq_ref, gk_ref, dqn_ref, dkn_ref, dv_ref, o_ref, dgq_ref, dgk_ref):
        _, vjp = jax.vjp(_qk_fn, q_ref[...], k_ref[...], gq_ref[...], gk_ref[...])
        dq, dk, dgq, dgk = vjp((dqn_ref[...] * LN2, dkn_ref[...] * LN2))
        o_ref[:, 0:W] = dq.astype(bf16)
        o_ref[:, W:2 * W] = dk.astype(bf16)
        o_ref[:, 2 * W:3 * W] = dv_ref[...].astype(bf16)

        @pl.when(pl.program_id(0) == 0)
        def _():
            dgq_ref[...] = jnp.zeros_like(dgq_ref)
            dgk_ref[...] = jnp.zeros_like(dgk_ref)

        dgq_ref[...] += dgq
        dgk_ref[...] += dgk

    gspec = pl.BlockSpec((1, LANES), lambda i: (0, 0))
    row = pl.BlockSpec((ts, W), lambda i: (i, 0))
    col = lambda c: pl.BlockSpec((ts, W), lambda i: (i, c))
    return pl.pallas_call(
        body, name="qkv_bwd", grid=(S // ts,), in_specs=[col(0), col(1), gspec, gspec, row, row, row],
        out_specs=[pl.BlockSpec((ts, 3 * W), lambda i: (i, 0)), gspec, gspec],
        out_shape=[jax.ShapeDtypeStruct((S, 3 * W), bf16), jax.ShapeDtypeStruct((1, LANES), f32),
                   jax.ShapeDtypeStruct((1, LANES), f32)],
        compiler_params=_cp(("arbitrary",)),
    )(qkv, qkv, gq, gk, dqn, dkn, dv)


def _sb_weights(q, ks, R, masked, row, col, UU):
    ls = [_dot(q, k, _NT) for k in ks]
    lbs, lm0s, cats = [], [], []
    for l, diag in zip(ls, masked):
        neg_abs = pltpu.bitcast(pltpu.bitcast(l, jnp.uint32) | jnp.uint32(0x80000000), f32)
        lp = jnp.log2(1.0 + jnp.exp2(neg_abs))
        lb = jnp.minimum(l, 0.0) - lp
        lm = lb - l
        if diag:
            lm = jnp.where(col < row, lm, 0.0)
        hi = lm.astype(bf16)
        lo = (lm - hi.astype(f32)).astype(bf16)
        lbs.append(lb)
        lm0s.append(lm[:, 0:1])
        cats.append(jnp.concatenate([hi, lo], axis=1))
    sums = [_dot(c, UU, _NN) for c in cats]
    ws = []
    for lb, lm0, A, diag in zip(lbs, lm0s, sums, masked):
        w = jnp.exp2(lb + (A + R))
        if diag:
            w = jnp.where(col < row, w, 0.0)
        R = R + (A[:, 0:1] + lm0)
        ws.append(w)
    return lbs, ws, R


def _tri2(tk):
    r = lax.broadcasted_iota(jnp.int32, (2 * tk, tk), 0)
    r = jnp.where(r >= tk, r - tk, r)
    c = lax.broadcasted_iota(jnp.int32, (2 * tk, tk), 1)
    return (r > c).astype(bf16)


SB_GROUP = 8


SB_ALL_ZERO_BELOW = -160.0


def _sweep(i, blocks_of, carry, descending, right_sum=None, ran=None):
    G = SB_GROUP
    rem, full = i % G, i // G

    def edge(r):
        if descending:
            return lambda c: blocks_of([i - u for u in range(r + 1)], c, [True] + [False] * r)
        return lambda c: blocks_of([i - r + u for u in range(r + 1)], c, [False] * r + [True])

    def body(p, c):
        first = i - rem - 1 - p * G if descending else p * G
        return blocks_of([first - u if descending else first + u for u in range(G)], c, [False] * G)

    if descending:
        carry = lax.switch(rem, [edge(r) for r in range(G)], carry)
        more = lambda s: (s[0] < full) & (jnp.max(right_sum(s[1])) > SB_ALL_ZERO_BELOW)
        ran, carry = lax.while_loop(more, lambda s: (s[0] + 1, body(s[0], s[1])), (jnp.int32(0), carry))
        return carry, ran
    carry = lax.fori_loop(full - ran, full, body, carry)
    return lax.switch(rem, [edge(r) for r in range(G)], carry)


def sb_fwd(qn, kn, vb, gather=(), *, tq=256):
    S, W = qn.shape
    H = W // LANES
    tq = _tile(S, tq)
    tk = tq
    nq = S // tq

    n = len(gather)

    def body(*refs):
        q_ref, k_ref, v_ref = refs[:3]
        g_ins, o_ref, g_outs, sems = refs[3:3 + n], refs[3 + n], refs[4 + n:4 + 2 * n], refs[4 + 2 * n:]
        h, i = pl.program_id(0), pl.program_id(1)

        @pl.when((h == 0) & (i == 0))
        def _():
            _chip_exchange(g_ins, g_outs, sems, False, True)

        q = q_ref[...]
        row = lax.broadcasted_iota(jnp.int32, (tq, tk), 0)
        col = lax.broadcasted_iota(jnp.int32, (tq, tk), 1)
        UU = _tri2(tk)

        def blocks(js, c, masked):
            rows = [pl.ds(pl.multiple_of(j * tk, tk), tk) for j in js]
            _, ws, R = _sb_weights(q, [k_ref[r, :] for r in rows], c[0], masked, row, col, UU)
            acc = c[1]
            for w, r in zip(ws, rows):
                acc = acc + _dot(w, v_ref[r, :], _NN)
            return R, acc

        c, _ = _sweep(i, blocks, (jnp.zeros((tq, 1), f32), jnp.zeros((tq, LANES), f32)), True, lambda c: c[0])
        o_ref[...] = c[1]

        @pl.when((h == H - 1) & (i == nq - 1))
        def _():
            _chip_exchange(g_ins, g_outs, sems, False, False)

    qspec = pl.BlockSpec((tq, LANES), lambda h, i: (i, h))
    kspec = pl.BlockSpec((S, LANES), lambda h, i: (0, h))
    g_shapes, g_sems = _chip_exchange_args(gather, False)
    res = pl.pallas_call(
        body, name="sb_fwd", grid=(H, nq), in_specs=[qspec, kspec, kspec] + [ANY] * n, out_specs=[qspec] + [ANY] * n,
        out_shape=[jax.ShapeDtypeStruct((S, W), f32)] + g_shapes, scratch_shapes=g_sems,
        compiler_params=_cp(("arbitrary", "arbitrary")),
    )(qn, kn, vb, *gather)
    return res[0], res[1:]


def sb_bwd(qn, kn, vb, do, scatter=(), *, tq=256):
    S, W = qn.shape
    H = W // LANES
    tq = _tile(S, tq)
    tk = tq
    nq = S // tq
    n = len(scatter)

    def body(*refs):
        q_ref, k_ref, v_ref, do_ref = refs[:4]
        s_ins = refs[4:4 + n]
        dq_ref, dk_ref, dv_ref = refs[4 + n:7 + n]
        s_outs = refs[7 + n:7 + 2 * n]
        dz_s, beta_s = refs[7 + 2 * n:9 + 2 * n]
        sems = refs[9 + 2 * n:]
        h, i = pl.program_id(0), pl.program_id(1)

        @pl.when((h == 0) & (i == 0))
        def _():
            _chip_exchange(s_ins, s_outs, sems, True, True)

        @pl.when(i == 0)
        def _():
            dk_ref[...] = jnp.zeros_like(dk_ref)
            dv_ref[...] = jnp.zeros_like(dv_ref)

        q = q_ref[...]
        dob = do_ref[...].astype(bf16)
        row = lax.broadcasted_iota(jnp.int32, (tq, tk), 0)
        col = lax.broadcasted_iota(jnp.int32, (tq, tk), 1)
        UU = _tri2(tk)
        Ue = (row < col).astype(bf16)

        def sweep1(js, R, masked):
            rows = [pl.ds(pl.multiple_of(j * tk, tk), tk) for j in js]
            dws = [_dot(dob, v_ref[r, :], _NT) for r in rows]
            lbs, ws, R = _sb_weights(q, [k_ref[r, :] for r in rows], R, masked, row, col, UU)
            for j, lb, w, dw in zip(js, lbs, ws, dws):
                dz_s[j] = (dw * w).astype(bf16)
                beta_s[j] = jnp.exp2(lb).astype(bf16)
            for r, w in zip(rows, ws):
                dv_ref[r, :] += _dot(w, dob, _TN)
            return R

        _, ran = _sweep(i, sweep1, jnp.zeros((tq, 1), f32), True, lambda R: R)

        def sweep2(js, c, masked):
            rows = [pl.ds(pl.multiple_of(j * tk, tk), tk) for j in js]
            dzbs = [dz_s[j] for j in js]
            sums = [_dot(dzb, Ue, _NN) for dzb in dzbs]
            Lz, dq = c
            dlbs = []
            for j, dzb, Cz, diag in zip(js, dzbs, sums, masked):
                dz = dzb.astype(f32)
                dl = dz - beta_s[j].astype(f32) * (dz + (Cz + Lz))
                if diag:
                    dl = jnp.where(col < row, dl, 0.0)
                Lz = Lz + (Cz[:, tk - 1:tk] + dz[:, tk - 1:tk])
                dlbs.append(dl.astype(bf16))
            for r, dlb in zip(rows, dlbs):
                dq = dq + _dot(dlb, k_ref[r, :], _NN)
            for r, dlb in zip(rows, dlbs):
                dk_ref[r, :] += _dot(dlb, q, _TN)
            return Lz, dq

        c = _sweep(i, sweep2, (jnp.zeros((tq, 1), f32), jnp.zeros((tq, LANES), f32)), False, ran=ran)
        dq_ref[...] = c[1]

        @pl.when((h == H - 1) & (i == nq - 1))
        def _():
            _chip_exchange(s_ins, s_outs, sems, True, False)

    qspec = pl.BlockSpec((tq, LANES), lambda h, i: (i, h))
    kspec = pl.BlockSpec((S, LANES), lambda h, i: (0, h))
    full = jax.ShapeDtypeStruct((S, W), f32)
    s_shapes, s_sems = _chip_exchange_args(scatter, True)
    res = pl.pallas_call(
        body, name="sb_bwd", grid=(H, nq), in_specs=[qspec, kspec, kspec, qspec] + [ANY] * n,
        out_specs=[qspec, kspec, kspec] + [ANY] * n, out_shape=[full, full, full] + s_shapes,
        scratch_shapes=[pltpu.VMEM((nq, tq, tk), bf16), pltpu.VMEM((nq, tq, tk), bf16)] + s_sems,
        compiler_params=_cp(("arbitrary", "arbitrary")),
    )(qn, kn, vb, do, *scatter)
    return res[0], res[1], res[2], res[3:]


def _xa_fn(qx, kv, gq, gk):
    XW = XA_HEADS * XA_HEAD_DIM
    outs = []
    for h in range(XA_HEADS):
        sl = slice(h * XA_HEAD_DIM, (h + 1) * XA_HEAD_DIM)
        qn = _rms(qx[:, sl], gq, XA_HEAD_DIM)
        kn = _rms(kv[:, sl], gk, XA_HEAD_DIM)
        v = kv[:, XW + h * XA_HEAD_DIM:XW + (h + 1) * XA_HEAD_DIM]
        s = bdot_nt(qn, kn) * (XA_HEAD_DIM ** -0.5)
        e = jnp.exp(s - lax.stop_gradient(jnp.max(s, axis=-1, keepdims=True)))
        p = e / jnp.sum(e, axis=-1, keepdims=True)
        outs.append(bdot_nn(p, v))
    return jnp.concatenate(outs, axis=1)


def xa_fwd(qx, kv, gq, gk, *, ts=256):
    S, XW = qx.shape
    M = kv.shape[0]
    ts = _tile(S, ts)

    def body(q_ref, kv_ref, gq_ref, gk_ref, o_ref):
        o_ref[...] = _xa_fn(q_ref[...], kv_ref[...], gq_ref[...], gk_ref[...]).astype(bf16)

    row = pl.BlockSpec((ts, XW), lambda i: (i, 0))
    gspec = pl.BlockSpec((1, XA_HEAD_DIM), lambda i: (0, 0))
    return pl.pallas_call(
        body, name="xa_fwd", grid=(S // ts,),
        in_specs=[row, pl.BlockSpec((M, 2 * XW), lambda i: (0, 0)), gspec, gspec], out_specs=row,
        out_shape=jax.ShapeDtypeStruct((S, XW), bf16), compiler_params=_cp(("parallel",)),
    )(qx, kv, gq, gk)


def xa_bwd(qx, kv, gq, gk, do, *, ts=256):
    S, XW = qx.shape
    M = kv.shape[0]
    ts = _tile(S, ts)

    def body(q_ref, kv_ref, gq_ref, gk_ref, do_ref, dq_ref, dkv_ref, dgq_ref, dgk_ref):
        _, vjp = jax.vjp(_xa_fn, q_ref[...], kv_ref[...], gq_ref[...], gk_ref[...])
        dq, dkv, dgq, dgk = vjp(do_ref[...].astype(f32))
        dq_ref[...] = dq.astype(bf16)

        @pl.when(pl.program_id(0) == 0)
        def _():
            dkv_ref[...] = jnp.zeros_like(dkv_ref)
            dgq_ref[...] = jnp.zeros_like(dgq_ref)
            dgk_ref[...] = jnp.zeros_like(dgk_ref)

        dkv_ref[...] += dkv
        dgq_ref[...] += dgq
        dgk_ref[...] += dgk

    row = pl.BlockSpec((ts, XW), lambda i: (i, 0))
    gspec = pl.BlockSpec((1, XA_HEAD_DIM), lambda i: (0, 0))
    kvspec = pl.BlockSpec((M, 2 * XW), lambda i: (0, 0))
    gshape = jax.ShapeDtypeStruct((1, XA_HEAD_DIM), f32)
    return pl.pallas_call(
        body, name="xa_bwd", grid=(S // ts,), in_specs=[row, kvspec, gspec, gspec, row],
        out_specs=[row, kvspec, gspec, gspec],
        out_shape=[jax.ShapeDtypeStruct((S, XW), bf16), jax.ShapeDtypeStruct((M, 2 * XW), f32), gshape, gshape],
        compiler_params=_cp(("arbitrary",)),
    )(qx, kv, gq, gk, do)


def _s5_prep_fn(a_re, a_im, ldt, bT_re, bT_im, cT_re, cT_im):
    G, P, C = SSM_GROUPS, SSM_STATE, SSM_GROUP
    GP, GC = G * P, G * C
    lg_p, lg_c = P.bit_length() - 1, C.bit_length() - 1
    gi = lax.broadcasted_iota(jnp.int32, (G, GP), 0)
    ci = lax.broadcasted_iota(jnp.int32, (G, GP), 1) >> lg_p
    expand_dt = (gi == ci).astype(f32)
    dte = jnp.dot(jnp.exp(ldt), expand_dt, precision=lax.Precision.HIGHEST, preferred_element_type=f32)
    zr, zi = a_re * dte, a_im * dte
    mag = jnp.exp(zr)
    abr, abi = mag * jnp.cos(zi), mag * jnp.sin(zi)
    nr, ni = abr - 1.0, abi
    den = a_re * a_re + a_im * a_im
    cr = (nr * a_re + ni * a_im) / den
    cim = (ni * a_re - nr * a_im) / den
    bbr = cr * bT_re - cim * bT_im
    bbi = cr * bT_im + cim * bT_re
    rowg = lax.broadcasted_iota(jnp.int32, (GC, GP), 0) >> lg_c
    colg = lax.broadcasted_iota(jnp.int32, (GC, GP), 1) >> lg_p
    diag = rowg == colg

    def expand(t):
        return jnp.where(diag, jnp.broadcast_to(t[None], (G, C, GP)).reshape(GC, GP), 0.0)

    return abr, abi, expand(bbr), expand(bbi), expand(cT_re), expand(-cT_im)


def s5_prep(a_re, a_im, ldt, bT_re, bT_im, cT_re, cT_im):
    GP, GC = SSM_GROUPS * SSM_STATE, SSM_GROUPS * SSM_GROUP

    def body(a_re_ref, a_im_ref, ldt_ref, bTr_ref, bTi_ref, cTr_ref, cTi_ref, abr_ref, abi_ref, B_ref, C_ref):
        abr, abi, Br, Bi, Cr, Ci = _s5_prep_fn(a_re_ref[...], a_im_ref[...], ldt_ref[...], bTr_ref[...],
                                               bTi_ref[...], cTr_ref[...], cTi_ref[...])
        abr_ref[...] = abr
        abi_ref[...] = abi
        B_ref[0] = Br
        B_ref[1] = Bi
        C_ref[0] = Cr
        C_ref[1] = Ci

    vec = jax.ShapeDtypeStruct((1, GP), f32)
    mat = jax.ShapeDtypeStruct((2, GC, GP), f32)
    return pl.pallas_call(body, name="s5_prep", out_shape=[vec, vec, mat, mat], compiler_params=_cp())(
        a_re, a_im, ldt, bT_re, bT_im, cT_re, cT_im)


def s5_prep_bwd(a_re, a_im, ldt, bT_re, bT_im, cT_re, cT_im, dabr, dabi, dB, dC):
    def body(a_re_ref, a_im_ref, ldt_ref, bTr_ref, bTi_ref, cTr_ref, cTi_ref, dabr_ref, dabi_ref, dB_ref, dC_ref,
             *outs):
        _, vjp = jax.vjp(_s5_prep_fn, a_re_ref[...], a_im_ref[...], ldt_ref[...], bTr_ref[...], bTi_ref[...],
                         cTr_ref[...], cTi_ref[...])
        grads = vjp((dabr_ref[...], dabi_ref[...], dB_ref[0], dB_ref[1], dC_ref[0], dC_ref[1]))
        for o_ref, gv in zip(outs, grads):
            o_ref[...] = gv

    ins = (a_re, a_im, ldt, bT_re, bT_im, cT_re, cT_im)
    return pl.pallas_call(body, name="s5_prep_bwd", out_shape=[jax.ShapeDtypeStruct(v.shape, f32) for v in ins],
                          compiler_params=_cp())(*ins, dabr, dabi, dB, dC)


def _cmul(ar, ai, br, bi):
    return ar * br - ai * bi, ar * bi + ai * br


SCAN_CHUNKS = 32


def _chunk_carry(Lr, Li, Pr, Pi, scratch, reverse):
    lr_ref, li_ref, cr_ref, ci_ref = scratch
    lr_ref[...] = Lr
    li_ref[...] = Li
    cur_r = jnp.zeros((1, LANES), f32)
    cur_i = jnp.zeros((1, LANES), f32)
    order = range(SCAN_CHUNKS - 1, -1, -1) if reverse else range(SCAN_CHUNKS)
    for c in order:
        cr_ref[pl.ds(c, 1), :] = cur_r
        ci_ref[pl.ds(c, 1), :] = cur_i
        mr, mi = _cmul(Pr, Pi, cur_r, cur_i)
        cur_r, cur_i = lr_ref[pl.ds(c, 1), :] + mr, li_ref[pl.ds(c, 1), :] + mi
    return cr_ref[...], ci_ref[...]


def _chunk_rows(j):
    return pl.ds(pl.multiple_of(j * SCAN_CHUNKS, SCAN_CHUNKS), SCAN_CHUNKS)


def row_shuffle(x, a, b, *, name, add=None, out_dtype=f32):
    S, W = x.shape
    assert a * b == S and x.dtype == f32

    def body(*refs):
        x_ref, o_ref = refs[0], refs[-1]

        def step(i, _):
            dst = pl.ds(pl.multiple_of(i * b, b), b)
            v = x_ref[pl.ds(i, b, stride=a), :]
            if add is not None:
                v = v + refs[1][dst, :]
            o_ref[dst, :] = v.astype(out_dtype)
            return 0

        lax.fori_loop(0, a, step, 0)

    col = pl.BlockSpec((S, LANES), lambda t: (0, t))
    args = [x] + ([add] if add is not None else [])
    return pl.pallas_call(
        body, name=name, grid=(W // LANES,), in_specs=[col] * len(args), out_specs=col,
        out_shape=jax.ShapeDtypeStruct((S, W), out_dtype), compiler_params=_cp(("parallel",)),
    )(*args)


def _scan_scratch(n):
    small = pltpu.VMEM((SCAN_CHUNKS, LANES), f32)
    return [pltpu.VMEM((n, LANES), f32), pltpu.VMEM((n, LANES), f32), small, small, small, small]


def scan_fwd(bu, abr, abi):
    _, S, N = bu.shape
    n = S // SCAN_CHUNKS
    shp = (SCAN_CHUNKS, LANES)

    def body(bu_ref, ar_ref, ai_ref, st_ref, pwr_ref, pwi_ref, *scratch):
        a1r, a1i = ar_ref[...], ai_ref[...]
        ar = jnp.broadcast_to(a1r, shp)
        ai = jnp.broadcast_to(a1i, shp)
        xr_ref, xi_ref = bu_ref.at[0], bu_ref.at[1]
        sr_ref, si_ref = st_ref.at[0], st_ref.at[1]

        def step(j, c):
            sr, si, pr, pi = c
            rows = _chunk_rows(j)
            mr, mi = _cmul(ar, ai, sr, si)
            sr, si = mr + xr_ref[rows, :], mi + xi_ref[rows, :]
            sr_ref[rows, :] = sr
            si_ref[rows, :] = si
            pwr_ref[pl.ds(j, 1), :] = pr
            pwi_ref[pl.ds(j, 1), :] = pi
            npr, npi = _cmul(a1r, a1i, pr, pi)
            return sr, si, npr, npi

        z = jnp.zeros(shp, f32)
        sr, si, _, _ = lax.fori_loop(0, n, step, (z, z, a1r, a1i), unroll=2)
        cr, ci = _chunk_carry(sr, si, pwr_ref[pl.ds(n - 1, 1), :], pwi_ref[pl.ds(n - 1, 1), :], scratch, False)

        def step2(j, _):
            rows = _chunk_rows(j)
            pr = jnp.broadcast_to(pwr_ref[pl.ds(j, 1), :], shp)
            pi = jnp.broadcast_to(pwi_ref[pl.ds(j, 1), :], shp)
            mr, mi = _cmul(pr, pi, cr, ci)
            sr_ref[rows, :] += mr
            si_ref[rows, :] += mi
            return 0

        lax.fori_loop(0, n, step2, 0, unroll=4)

    blk = pl.BlockSpec((2, S, LANES), lambda t: (0, 0, t))
    vec = pl.BlockSpec((1, LANES), lambda t: (0, t))
    return pl.pallas_call(
        body, name="scan_fwd", grid=(N // LANES,), in_specs=[blk, vec, vec], out_specs=blk,
        out_shape=jax.ShapeDtypeStruct((2, S, N), f32), scratch_shapes=_scan_scratch(n),
        compiler_params=_cp(("parallel",)),
    )(bu, abr, abi)


def scan_bwd(G, st, abr, abi):
    _, S, N = G.shape
    n = S // SCAN_CHUNKS
    shp = (SCAN_CHUNKS, LANES)

    def body(G_ref, st_ref, ar_ref, ai_ref, g_ref, dar_ref, dai_ref, qwr_ref, qwi_ref, *scratch):
        a1r, a1i = ar_ref[...], -ai_ref[...]
        ar = jnp.broadcast_to(a1r, shp)
        nai = jnp.broadcast_to(a1i, shp)
        Gr_ref, Gi_ref = G_ref.at[0], G_ref.at[1]
        sr_ref, si_ref = st_ref.at[0], st_ref.at[1]
        gr_ref, gi_ref = g_ref.at[0], g_ref.at[1]

        def step(jj, c):
            gr, gi, qr, qi = c
            j = n - 1 - jj
            rows = _chunk_rows(j)
            mr, mi = _cmul(ar, nai, gr, gi)
            gr, gi = mr + Gr_ref[rows, :], mi + Gi_ref[rows, :]
            gr_ref[rows, :] = gr
            gi_ref[rows, :] = gi
            qwr_ref[pl.ds(j, 1), :] = qr
            qwi_ref[pl.ds(j, 1), :] = qi
            nqr, nqi = _cmul(a1r, a1i, qr, qi)
            return gr, gi, nqr, nqi

        z = jnp.zeros(shp, f32)
        gr, gi, _, _ = lax.fori_loop(0, n, step, (z, z, a1r, a1i), unroll=2)
        cr, ci = _chunk_carry(gr, gi, qwr_ref[pl.ds(0, 1), :], qwi_ref[pl.ds(0, 1), :], scratch, True)
        sub = lax.broadcasted_iota(jnp.int32, shp, 0)

        def fix(j, spr, spi, acc):
            rows = _chunk_rows(j)
            qr = jnp.broadcast_to(qwr_ref[pl.ds(j, 1), :], shp)
            qi = jnp.broadcast_to(qwi_ref[pl.ds(j, 1), :], shp)
            mr, mi = _cmul(qr, qi, cr, ci)
            gr = gr_ref[rows, :] + mr
            gi = gi_ref[rows, :] + mi
            gr_ref[rows, :] = gr
            gi_ref[rows, :] = gi
            return acc[0] + gr * spr + gi * spi, acc[1] + gi * spr - gr * spi

        last = _chunk_rows(n - 1)
        spr = jnp.where(sub == 0, 0.0, pltpu.roll(sr_ref[last, :], 1, 0))
        spi = jnp.where(sub == 0, 0.0, pltpu.roll(si_ref[last, :], 1, 0))
        acc = fix(0, spr, spi, (z, z))

        def step2(j, acc):
            prev = _chunk_rows(j - 1)
            return fix(j, sr_ref[prev, :], si_ref[prev, :], acc)

        acc = lax.fori_loop(1, n, step2, acc)
        dar_ref[...] = jnp.sum(acc[0], axis=0, keepdims=True)
        dai_ref[...] = jnp.sum(acc[1], axis=0, keepdims=True)

    blk = pl.BlockSpec((2, S, LANES), lambda t: (0, 0, t))
    vec = pl.BlockSpec((1, LANES), lambda t: (0, t))
    vshape = jax.ShapeDtypeStruct((1, N), f32)
    return pl.pallas_call(
        body, name="scan_bwd", grid=(N // LANES,), in_specs=[blk, blk, vec, vec], out_specs=[blk, vec, vec],
        out_shape=[jax.ShapeDtypeStruct((2, S, N), f32), vshape, vshape], scratch_shapes=_scan_scratch(n),
        compiler_params=_cp(("parallel",)),
    )(G, st, abr, abi)


def _glu_fn(ypre, wglu):
    y = jax.nn.gelu(ypre)
    return y * jax.nn.sigmoid(bdot_nn(y, wglu))


def glu_fwd(ypre0, u, d, wglu, g_out, *, ts=512):
    S, W = u.shape
    ts = _tile(S, ts)

    def body(y0_ref, u_ref, d_ref, w_ref, g_ref, ypre_ref, z_ref, zn_ref):
        ypre = y0_ref[...] + d_ref[...] * u_ref[...]
        z = _glu_fn(ypre, w_ref[...])
        ypre_ref[...] = ypre
        z_ref[...] = z
        zn_ref[...] = _rms(z, g_ref[...], W).astype(bf16)

    row = pl.BlockSpec((ts, W), lambda i: (i, 0))
    vec = pl.BlockSpec((1, W), lambda i: (0, 0))
    full = jax.ShapeDtypeStruct((S, W), f32)
    return pl.pallas_call(
        body, name="glu_fwd", grid=(S // ts,),
        in_specs=[row, row, vec, pl.BlockSpec((W, W), lambda i: (0, 0)), vec], out_specs=[row, row, row],
        out_shape=[full, full, jax.ShapeDtypeStruct((S, W), bf16)], compiler_params=_cp(("parallel",)),
    )(ypre0, u, d, wglu, g_out)


def glu_bwd(ypre, u, d, wglu, dz, *, ts=512):
    S, W = u.shape
    ts = _tile(S, ts)

    def body(y_ref, u_ref, d_ref, w_ref, dz_ref, dy_ref, du_ref, dw_ref, dd_ref):
        _, vjp = jax.vjp(_glu_fn, y_ref[...], w_ref[...])
        dy, dw = vjp(dz_ref[...])
        dy_ref[...] = dy
        du_ref[...] = d_ref[...] * dy

        @pl.when(pl.program_id(0) == 0)
        def _():
            dw_ref[...] = jnp.zeros_like(dw_ref)
            dd_ref[...] = jnp.zeros_like(dd_ref)

        dw_ref[...] += dw
        dd_ref[...] += jnp.sum(dy * u_ref[...], axis=0, keepdims=True)

    row = pl.BlockSpec((ts, W), lambda i: (i, 0))
    vec = pl.BlockSpec((1, W), lambda i: (0, 0))
    sq = pl.BlockSpec((W, W), lambda i: (0, 0))
    full = jax.ShapeDtypeStruct((S, W), f32)
    return pl.pallas_call(
        body, name="glu_bwd", grid=(S // ts,), in_specs=[row, row, vec, sq, row], out_specs=[row, row, sq, vec],
        out_shape=[full, full, jax.ShapeDtypeStruct((W, W), f32), jax.ShapeDtypeStruct((1, W), f32)],
        compiler_params=_cp(("arbitrary",)),
    )(ypre, u, d, wglu, dz)


def loss_head(y, target, *, ts=512):
    S, D = y.shape
    ts = _tile(S, ts)

    def body(y_ref, t_ref, dy_ref, l_ref, dyb_ref):
        err = y_ref[...] - t_ref[...]
        dy_ref[...] = err * (1.0 / D)
        dyb_ref[...] = (err * (1.0 / D)).astype(bf16)

        @pl.when(pl.program_id(0) == 0)
        def _():
            l_ref[...] = jnp.zeros_like(l_ref)

        rows = jnp.sum(err * err, axis=1, keepdims=True) * (1.0 / D)
        l_ref[...] += 0.5 * jnp.sum(rows, axis=0, keepdims=True)

    row = pl.BlockSpec((ts, D), lambda i: (i, 0))
    return pl.pallas_call(
        body, name="loss_head", grid=(S // ts,), in_specs=[row, row],
        out_specs=[row, pl.BlockSpec((1, 1), lambda i: (0, 0)), row],
        out_shape=[jax.ShapeDtypeStruct((S, D), f32), jax.ShapeDtypeStruct((1, 1), f32),
                   jax.ShapeDtypeStruct((S, D), bf16)],
        compiler_params=_cp(("arbitrary",)),
    )(y, target)


def adamw(w, g, m, v, *, name, tr=256):
    R, C = w.shape
    tr = _row_tile(R, tr)

    def body(w_ref, g_ref, m_ref, v_ref, d_ref, nm_ref, nv_ref):
        gv = g_ref[...]
        nm = ADAM_B1 * m_ref[...] + (1.0 - ADAM_B1) * gv
        nv = ADAM_B2 * v_ref[...] + (1.0 - ADAM_B2) * jnp.square(gv)
        m_hat = nm / (1.0 - ADAM_B1 ** ADAM_STEP)
        v_hat = nv / (1.0 - ADAM_B2 ** ADAM_STEP)
        d_ref[...] = -ADAM_LR * (m_hat / (jnp.sqrt(v_hat) + ADAM_EPS) + ADAM_WD * w_ref[...])
        nm_ref[...] = nm
        nv_ref[...] = nv

    row = pl.BlockSpec((tr, C), lambda i: (i, 0))
    full = jax.ShapeDtypeStruct((R, C), f32)
    return pl.pallas_call(
        body, name=name, grid=(R // tr,), in_specs=[row] * 4, out_specs=[row] * 3, out_shape=[full] * 3,
        compiler_params=_cp(("parallel",)),
    )(w, g, m, v)


def add_half(g4, recv, c, *, name, tr=256):
    _, _, Rh, C = g4.shape
    tr = _row_tile(Rh, tr)

    def body(c_ref, a_ref, b_ref, o_ref):
        o_ref[...] = a_ref[...] + b_ref[...]

    grid_spec = pltpu.PrefetchScalarGridSpec(
        num_scalar_prefetch=1, grid=(N_CHIPS, Rh // tr),
        in_specs=[pl.BlockSpec((None, None, tr, C), lambda k, i, c_ref: (k, c_ref[0], i, 0)),
                  pl.BlockSpec((None, tr, C), lambda k, i, c_ref: (k, i, 0))],
        out_specs=pl.BlockSpec((None, tr, C), lambda k, i, c_ref: (k, i, 0)))
    return pl.pallas_call(body, name=name, grid_spec=grid_spec, out_shape=jax.ShapeDtypeStruct(recv.shape, f32),
                          compiler_params=_cp(("parallel", "parallel")))(c, g4, recv)


def sum_chips(p4, *, name, tr=256):
    _, Rh, C = p4.shape
    tr = _row_tile(Rh, tr)

    def body(a_ref, b_ref, c_ref, d_ref, o_ref):
        o_ref[...] = ((a_ref[...] + b_ref[...]) + c_ref[...]) + d_ref[...]

    spec = lambda k: pl.BlockSpec((None, tr, C), lambda i: (k, i, 0))
    return pl.pallas_call(
        body, name=name, grid=(Rh // tr,), in_specs=[spec(0), spec(1), spec(2), spec(3)],
        out_specs=pl.BlockSpec((tr, C), lambda i: (i, 0)), out_shape=jax.ShapeDtypeStruct((Rh, C), f32),
        compiler_params=_cp(("parallel",)),
    )(p4, p4, p4, p4)


def _place():
    return lax.axis_index("x"), lax.axis_index("y"), lax.axis_index("c")


def _other_chips(x, y):
    return [(1 - x, y), (x, 1 - y), (1 - x, 1 - y)]


def _chip_exchange(ins, outs, sems, scatter, start):
    if not ins:
        return
    send, recv, loc = sems
    x, y, c = _place()
    me = 2 * x + y
    for a in range(len(ins)):
        own = pltpu.make_async_copy(ins[a].at[me] if scatter else ins[a], outs[a].at[me], loc.at[a])
        own.start() if start else own.wait()
        for p, (px, py) in enumerate(_other_chips(x, y)):
            k = 2 * px + py
            cp = pltpu.make_async_remote_copy(
                src_ref=ins[a].at[k] if scatter else ins[a], dst_ref=outs[a].at[me if start else k],
                send_sem=send.at[3 * a + p], recv_sem=recv.at[3 * a + p], device_id=(px, py, c), device_id_type=MESH)
            cp.start() if start else cp.wait()


def _chip_exchange_args(arrs, scatter):
    n = len(arrs)
    shapes = [jax.ShapeDtypeStruct(a.shape if scatter else (N_CHIPS,) + a.shape, a.dtype) for a in arrs]
    sems = [pltpu.SemaphoreType.DMA((3 * n,)), pltpu.SemaphoreType.DMA((3 * n,)), pltpu.SemaphoreType.DMA((n,))]
    return shapes, sems if n else []


def _chip_exchange_call(arrs, scatter, name):
    n = len(arrs)

    def body(*refs):
        ins, outs, sems = refs[:n], refs[n:2 * n], refs[2 * n:]
        _chip_exchange(ins, outs, sems, scatter, True)
        _chip_exchange(ins, outs, sems, scatter, False)

    shapes, sems = _chip_exchange_args(arrs, scatter)
    return pl.pallas_call(
        body, name=name, in_specs=[ANY] * n, out_specs=[ANY] * n, out_shape=shapes, scratch_shapes=sems,
        compiler_params=pltpu.CompilerParams(has_side_effects=True),
    )(*arrs)


def allgather_chips(arrs, *, name):
    return _chip_exchange_call(arrs, False, name)


def sibling_swap(arrs, *, half, name):
    n = len(arrs)

    def body(*refs):
        ins, outs = refs[:n], refs[n:2 * n]
        send, recv = refs[2 * n:]
        x, y, c = _place()
        cps = []
        for a in range(n):
            src = ins[a].at[:, 1 - c] if half else ins[a]
            cp = pltpu.make_async_remote_copy(src_ref=src, dst_ref=outs[a], send_sem=send.at[a], recv_sem=recv.at[a],
                                              device_id=(x, y, 1 - c), device_id_type=MESH)
            cp.start()
            cps.append(cp)
        for cp in cps:
            cp.wait()

    def oshape(a):
        return jax.ShapeDtypeStruct((a.shape[0],) + a.shape[2:] if half else a.shape, a.dtype)

    return pl.pallas_call(
        body, name=name, in_specs=[ANY] * n, out_specs=[ANY] * n, out_shape=[oshape(a) for a in arrs],
        scratch_shapes=[pltpu.SemaphoreType.DMA((n,)), pltpu.SemaphoreType.DMA((n,))],
        compiler_params=pltpu.CompilerParams(has_side_effects=True),
    )(*arrs)


def chip_scatter(arrs, *, name):
    return _chip_exchange_call(arrs, True, name)


def _pad_cols(w):
    K = w.shape[0]
    w = w.reshape(K, -1, SB_HEAD_DIM)
    return jnp.pad(w, ((0, 0), (0, 0), (0, LANES - SB_HEAD_DIM))).reshape(K, -1)


def _unpad_cols(w):
    K = w.shape[0]
    return w.reshape(K, -1, LANES)[:, :, :SB_HEAD_DIM].reshape(K, -1)


def _pad_rows(w):
    N = w.shape[1]
    w = w.reshape(-1, SB_HEAD_DIM, N)
    return jnp.pad(w, ((0, 0), (0, LANES - SB_HEAD_DIM), (0, 0))).reshape(-1, N)


def _unpad_rows(w):
    N = w.shape[1]
    return w.reshape(-1, LANES, N)[:, :SB_HEAD_DIM, :].reshape(-1, N)


_PACK_ROWS = N_CHIPS * 2 * SUBLANES


def _pack(arrs):
    flat = jnp.concatenate([a.reshape(-1) for a in arrs])
    rows = -(-flat.shape[0] // LANES)
    rows = -(-rows // _PACK_ROWS) * _PACK_ROWS
    return jnp.pad(flat, (0, rows * LANES - flat.shape[0])).reshape(rows, LANES)


def _unpack(buf, shapes):
    flat = buf.reshape(-1)
    out, pos = [], 0
    for shp in shapes:
        size = 1
        for d in shp:
            size *= d
        out.append(flat[pos:pos + size].reshape(shp))
        pos += size
    return out


BIG = ("w_in", "ssm_w_glu", "w_out", "xa_w_q", "xa_w_kv", "xa_w_o", "w_up", "w_down")
SMALL = ("g_mix", "ssm_a_re", "ssm_a_im", "ssm_log_dt", "ssm_b_re", "ssm_b_im", "ssm_c_re", "ssm_c_im", "ssm_d",
         "sb_g_q", "sb_g_k", "g_out_ssm", "g_out_sb", "g_xa", "g_mem", "xa_g_q", "xa_g_k", "g_mlp")
WEIGHTS = ("g_mix", "w_in", "ssm_a_re", "ssm_a_im", "ssm_log_dt", "ssm_b_re", "ssm_b_im", "ssm_c_re", "ssm_c_im",
           "ssm_d", "ssm_w_glu", "sb_g_q", "sb_g_k", "g_out_ssm", "g_out_sb", "w_out", "g_xa", "g_mem", "xa_w_q",
           "xa_w_kv", "xa_g_q", "xa_g_k", "xa_w_o", "g_mlp", "w_up", "w_down")


def kernel(x, mem, g_mix, w_in, ssm_a_re, ssm_a_im, ssm_log_dt, ssm_b_re, ssm_b_im, ssm_c_re, ssm_c_im, ssm_d, ssm_w_glu, sb_g_q, sb_g_k, g_out_ssm, g_out_sb, w_out, g_xa, g_mem, xa_w_q, xa_w_kv, xa_g_q, xa_g_k, xa_w_o, g_mlp, w_up, w_down, loss_target, m_g_mix, m_w_in, m_ssm_a_re, m_ssm_a_im, m_ssm_log_dt, m_ssm_b_re, m_ssm_b_im, m_ssm_c_re, m_ssm_c_im, m_ssm_d, m_ssm_w_glu, m_sb_g_q, m_sb_g_k, m_g_out_ssm, m_g_out_sb, m_w_out, m_g_xa, m_g_mem, m_xa_w_q, m_xa_w_kv, m_xa_g_q, m_xa_g_k, m_xa_w_o, m_g_mlp, m_w_up, m_w_down, v_g_mix, v_w_in, v_ssm_a_re, v_ssm_a_im, v_ssm_log_dt, v_ssm_b_re, v_ssm_b_im, v_ssm_c_re, v_ssm_c_im, v_ssm_d, v_ssm_w_glu, v_sb_g_q, v_sb_g_k, v_g_out_ssm, v_g_out_sb, v_w_out, v_g_xa, v_g_mem, v_xa_w_q, v_xa_w_kv, v_xa_g_q, v_xa_g_k, v_xa_w_o, v_g_mlp, v_w_up, v_w_down):
    env = dict(locals())
    W = {n: env[n] for n in WEIGHTS}
    M1 = {n: env["m_" + n] for n in WEIGHTS}
    V2 = {n: env["v_" + n] for n in WEIGHTS}
    xs, mems, tgt = x[0], mem[0], loss_target[0]
    S, D = xs.shape
    G, P, C = SSM_GROUPS, SSM_STATE, SSM_GROUP
    GP = G * P
    SBW = SB_HEADS * SB_HEAD_DIM
    c_idx = lax.axis_index("c")

    (g_in,) = allgather_chips([w_in[0].astype(bf16)], name="gather_w_in")
    Wu = g_in[0]
    Wqkv = jnp.concatenate([_pad_cols(g_in[1]), _pad_cols(g_in[2]), _pad_cols(g_in[3])], axis=1)
    gq_pad, gk_pad = _pad_cols(sb_g_q), _pad_cols(sb_g_k)
    gosb_pad = _pad_cols(g_out_sb)
    a_re, a_im = ssm_a_re.reshape(1, GP), ssm_a_im.reshape(1, GP)
    bT_re = ssm_b_re[0].transpose(2, 0, 1).reshape(C, GP)
    bT_im = ssm_b_im[0].transpose(2, 0, 1).reshape(C, GP)
    cT_re = ssm_c_re[0].transpose(1, 0, 2).reshape(C, GP)
    cT_im = ssm_c_im[0].transpose(1, 0, 2).reshape(C, GP)
    s5_in = (a_re, a_im, ssm_log_dt, bT_re, bT_im, cT_re, cT_im)

    big = dict(tn=1024, tk=1024)
    h0 = rms_norm(xs, g_mix, D, name="norm_x")
    u = mm(h0, Wu, mode="nn", name="proj_u", tk=1024)
    qkv = mm(h0, Wqkv, mode="nn", name="proj_qkv", **big)
    qn, kn, vb = qkv_prep(qkv, gq_pad, gk_pad)
    o, gath = sb_fwd(qn, kn, vb, [W[n][0].astype(bf16) for n in BIG[1:]])
    g_glu, g_out, g_xq, g_xkv, g_xo, g_up, g_down = gath
    Wglu = g_glu.reshape(-1, g_glu.shape[-1])
    Wout = g_out.reshape(-1, g_out.shape[-1])
    Wo_ssm, Wo_sb = Wout[:SBW], _pad_rows(Wout[SBW:])
    Wxq = g_xq.reshape(-1, g_xq.shape[-1])
    Wxkv = g_xkv.reshape(-1, g_xkv.shape[-1])
    Wxo = g_xo.transpose(1, 0, 2).reshape(g_xo.shape[1], -1)
    Wup = g_up.transpose(1, 0, 2).reshape(g_up.shape[1], -1)
    Wdown = g_down.reshape(-1, g_down.shape[-1])
    abr, abi, Bm, Cm = s5_prep(*s5_in)
    n_pos = S // SCAN_CHUNKS
    u_il = row_shuffle(u, n_pos, SCAN_CHUNKS, name="u_interleave", out_dtype=bf16)
    bu = mm(u_il, Bm, mode="nn", name="s5_bu", b_shards=2, out_shards=2, tn=1024)
    st = scan_fwd(bu, abr, abi)
    ypre0_il = mm(st, Cm, mode="nt", name="s5_y", a_shards=2, b_shards=2, tk=1024)
    ypre0 = row_shuffle(ypre0_il, SCAN_CHUNKS, n_pos, name="y_token_order")
    ypre, z, zn = glu_fwd(ypre0, u, ssm_d, Wglu, g_out_ssm)
    on = rms_norm(o, gosb_pad, SBW, name="norm_o")
    x1a = mm(zn, Wo_ssm, mode="nn", name="out_ssm", epi="add", aux=xs, tn=1024)
    x1 = mm(on, Wo_sb, mode="nn", name="out_sb", epi="add", aux=x1a, **big)
    h1 = rms_norm(x1, g_xa, D, name="norm_x1")
    qx = mm(h1, Wxq, mode="nn", name="xa_q", tk=1024)
    memn = rms_norm(mems, g_mem, D, name="norm_mem")
    kv = mm(memn, Wxkv, mode="nn", name="xa_kv", **big)
    ox = xa_fwd(qx, kv, xa_g_q, xa_g_k)
    x2 = mm(ox, Wxo, mode="nn", name="xa_o", epi="add", aux=x1, tn=1024)
    h2 = rms_norm(x2, g_mlp, D, name="norm_x2")
    act = mm(h2, Wup, mode="nn", name="mlp_up", out_dtype=bf16, tn=2048, tk=1024)
    x3 = mm(act, Wdown, mode="nn", name="mlp_down", pro="relu2", epi="add", aux=x2, **big)
    dx3, loss_part, dx3b = loss_head(x3, tgt)
    loss = lax.psum(loss_part[0, 0], ("x", "y", "c"))

    dact = mm(dx3b, Wdown, mode="nt", name="d_act", epi="mul2relu", aux=act, out_dtype=bf16, tn=2048, tk=1024)
    dWdown = mm(act, dx3b, mode="tn", name="dw_down", pro="relu2", tm=1024, **big)
    dWup = mm(h2, dact, mode="tn", name="dw_up", out_shards=N_CHIPS, tm=1024, **big)
    dh2 = mm(dact, Wup, mode="nt", name="d_h2", **big)
    dx2, dg_mlp, dx2b = rms_bwd(x2, g_mlp, dh2, dx3, D, name="rms_bwd_mlp", twin=True)
    dox = mm(dx2b, Wxo, mode="nt", name="d_ox", out_dtype=bf16, tk=1024)
    dWxo = mm(ox, dx2b, mode="tn", name="dw_xo", out_shards=N_CHIPS, tk=1024)
    dqx, dkv, dg_xq, dg_xk = xa_bwd(qx, kv, xa_g_q, xa_g_k, dox)
    dWxq = mm(h1, dqx, mode="tn", name="dw_xq", tm=1024, tk=1024)
    dh1 = mm(dqx, Wxq, mode="nt", name="d_h1", tn=1024)
    dx1, dg_xa, dx1b = rms_bwd(x1, g_xa, dh1, dx2, D, name="rms_bwd_xa", twin=True)
    dWxkv = mm(memn, dkv, mode="tn", name="dw_xkv", tm=1024, tn=1024)
    dmemn = mm(dkv, Wxkv, mode="nt", name="d_memn", **big)
    _, dg_mem = rms_bwd(mems, g_mem, dmemn, None, D, name="rms_bwd_mem")
    dyn_ssm = mm(dx1b, Wo_ssm, mode="nt", name="d_yn_ssm", tk=1024)
    dyn_sb = mm(dx1b, Wo_sb, mode="nt", name="d_yn_sb", **big)
    dWo_ssm = mm(zn, dx1b, mode="tn", name="dw_out_ssm", **big)
    dWo_sb = mm(on, dx1b, mode="tn", name="dw_out_sb", tm=1024, **big)
    dz, dg_os = rms_bwd(z, g_out_ssm, dyn_ssm, None, SBW, name="rms_bwd_ssm")
    do, dg_osb = rms_bwd(o, gosb_pad, dyn_sb, None, SBW, name="rms_bwd_sb")
    c_arr = c_idx.astype(jnp.int32).reshape(1)

    def sibling_sums(grads, names, tag):
        g4 = [g.reshape(N_CHIPS, 2, g.shape[1] // 2, g.shape[2]) for g in grads]
        from_sib = sibling_swap(g4, half=True, name="grad_to_sibling_" + tag)
        return [add_half(a, b, c_arr, name="add_sibling_" + n) for a, b, n in zip(g4, from_sib, names)]

    early = ("xa_w_q", "xa_w_kv", "xa_w_o", "w_up", "w_down")
    early_g = [dWxq.reshape(N_CHIPS, -1, dWxq.shape[1]), dWxkv.reshape(N_CHIPS, -1, dWxkv.shape[1]), dWxo, dWup,
               dWdown.reshape(N_CHIPS, -1, D)]
    dqn, dkn, dv, parts_early = sb_bwd(qn, kn, vb, do, sibling_sums(early_g, early, "early"))
    mine = {n: sum_chips(p, name="sum_chips_" + n) for n, p in zip(early, parts_early)}
    dqkv, dg_q, dg_k = qkv_bwd(qkv, gq_pad, gk_pad, dqn, dkn, dv)
    dypre, du_skip, dWglu, dd = glu_bwd(ypre, u, ssm_d, Wglu, dz)
    dypre_il = row_shuffle(dypre, n_pos, SCAN_CHUNKS, name="dy_interleave", out_dtype=bf16)
    dst = mm(dypre_il, Cm, mode="nn", name="d_states", b_shards=2, out_shards=2, tn=1024)
    dCm = mm(dypre_il, st, mode="tn", name="d_cmat", b_shards=2, out_shards=2, **big)
    gst, dabr, dabi = scan_bwd(dst, st, abr, abi)
    dBm = mm(u_il, gst, mode="tn", name="d_bmat", b_shards=2, out_shards=2, **big)
    du_il = mm(gst, Bm, mode="nt", name="d_u", a_shards=2, b_shards=2, tk=1024)
    du = row_shuffle(du_il, SCAN_CHUNKS, n_pos, name="du_token_order", add=du_skip, out_dtype=bf16)
    s5_g = s5_prep_bwd(*s5_in, dabr, dabi, dBm, dCm)
    dWu = mm(h0, du, mode="tn", name="dw_u", tm=1024, tk=1024)
    dWqkv = mm(h0, dqkv, mode="tn", name="dw_qkv", tm=1024, **big)
    dh0a = mm(du, Wu, mode="nt", name="d_h0_u", tn=1024)
    dh0 = mm(dqkv, Wqkv, mode="nt", name="d_h0_qkv", epi="add", aux=dh0a, **big)
    dx, dg_mix = rms_bwd(xs, g_mix, dh0, dx1, D, name="rms_bwd_mix")

    HW = SB_HEADS * LANES
    late = ("w_in", "ssm_w_glu", "w_out", "small")
    late_g = [jnp.stack([dWu, _unpad_cols(dWqkv[:, :HW]), _unpad_cols(dWqkv[:, HW:2 * HW]),
                         _unpad_cols(dWqkv[:, 2 * HW:])]),
              dWglu.reshape(N_CHIPS, -1, dWglu.shape[1]),
              jnp.concatenate([dWo_ssm, _unpad_rows(dWo_sb)]).reshape(N_CHIPS, -1, D)]
    da_re, da_im, dldt, dbT_re, dbT_im, dcT_re, dcT_im = s5_g
    small_g = {
        "g_mix": dg_mix, "ssm_a_re": da_re, "ssm_a_im": da_im, "ssm_log_dt": dldt,
        "ssm_b_re": dbT_re.reshape(C, G, P).transpose(1, 2, 0), "ssm_b_im": dbT_im.reshape(C, G, P).transpose(1, 2, 0),
        "ssm_c_re": dcT_re.reshape(C, G, P).transpose(1, 0, 2), "ssm_c_im": dcT_im.reshape(C, G, P).transpose(1, 0, 2),
        "ssm_d": dd, "sb_g_q": dg_q[:, :SB_HEAD_DIM], "sb_g_k": dg_k[:, :SB_HEAD_DIM], "g_out_ssm": dg_os,
        "g_out_sb": _unpad_cols(dg_osb), "g_xa": dg_xa, "g_mem": dg_mem, "xa_g_q": dg_xq, "xa_g_k": dg_xk,
        "g_mlp": dg_mlp,
    }
    late_g.append(_pack([small_g[n] for n in SMALL]).reshape(N_CHIPS, -1, LANES))

    parts_late = chip_scatter(sibling_sums(late_g, late, "late"), name="grad_to_chips_late")
    mine.update({n: sum_chips(p, name="sum_chips_" + n) for n, p in zip(late, parts_late)})
    mine = [mine[n] for n in list(BIG) + ["small"]]
    other = sibling_swap(mine, half=False, name="grad_half_to_sibling")
    shard = [jnp.where(c_idx == 0, jnp.concatenate([a, b]), jnp.concatenate([b, a])) for a, b in zip(mine, other)]
    small_all = allgather_chips([shard[-1]], name="gather_small")[0]
    small_red = small_all.reshape(-1, LANES)

    out = {}
    for n, gs in zip(BIG, shard[:-1]):
        shp = W[n].shape
        w2, m2, v2 = (t.reshape(gs.shape) for t in (W[n], M1[n], V2[n]))
        d, nm, nv = adamw(w2, gs, m2, v2, name="adamw_" + n)
        out[n] = tuple(t.reshape(shp) for t in (gs, d, nm, nv))
    shapes = [W[n].shape for n in SMALL]
    d, nm, nv = adamw(_pack([W[n] for n in SMALL]), small_red, _pack([M1[n] for n in SMALL]),
                      _pack([V2[n] for n in SMALL]), name="adamw_small")
    for n, gs, dd_, mm_, vv_ in zip(SMALL, _unpack(small_red, shapes), _unpack(d, shapes), _unpack(nm, shapes),
                                    _unpack(nv, shapes)):
        out[n] = (gs, dd_, mm_, vv_)
    res = [loss, dx[None]]
    for kind in range(4):
        res += [out[n][kind] for n in WEIGHTS]
    return tuple(res)
```

```python
import jax
import jax.numpy as jnp
from jax import lax
from jax.experimental import pallas as pl
from jax.experimental.pallas import tpu as pltpu

f32 = jnp.float32
bf16 = jnp.bfloat16

NORM_EPS = 1e-6
SSM_GROUPS = 32
SSM_GROUP = 16
SSM_STATE = 64
SB_HEADS = 8
SB_HEAD_DIM = 64
XA_HEADS = 4
XA_HEAD_DIM = 128
LANES = 128
SUBLANES = 8
N_CHIPS = 4
ADAM_LR = 0.001
ADAM_B1 = 0.9
ADAM_B2 = 0.999
ADAM_EPS = 1e-08
ADAM_WD = 0.01
ADAM_STEP = 10
VMEM_LIMIT = 56 * 1024 * 1024
MESH = pl.DeviceIdType.MESH
ANY = pl.BlockSpec(memory_space=pl.ANY)


def _cp(sem=None):
    return pltpu.CompilerParams(dimension_semantics=sem, vmem_limit_bytes=VMEM_LIMIT)


def _tile(n, pref):
    if n <= pref:
        return n
    t = (pref // LANES) * LANES
    while t > LANES and n % t:
        t -= LANES
    assert n % t == 0, (n, pref)
    return t


def _row_tile(n, pref):
    if n <= pref:
        return n
    t = (pref // SUBLANES) * SUBLANES
    while n % t:
        t -= SUBLANES
    return t


def _dot(a, b, dims):
    return lax.dot_general(a.astype(bf16), b.astype(bf16), (dims, ((), ())), preferred_element_type=f32)


_NN = ((1,), (0,))
_NT = ((1,), (1,))
_TN = ((0,), (0,))


@jax.custom_vjp
def bdot_nn(a, b):
    return _dot(a, b, _NN)


def _bdot_nn_fwd(a, b):
    return _dot(a, b, _NN), (a, b)


def _bdot_nn_bwd(res, g):
    a, b = res
    return _dot(g, b, _NT), _dot(a, g, _TN)


bdot_nn.defvjp(_bdot_nn_fwd, _bdot_nn_bwd)


@jax.custom_vjp
def bdot_nt(a, b):
    return _dot(a, b, _NT)


def _bdot_nt_fwd(a, b):
    return _dot(a, b, _NT), (a, b)


def _bdot_nt_bwd(res, g):
    a, b = res
    return _dot(g, b, _NN), _dot(g, a, _TN)


bdot_nt.defvjp(_bdot_nt_fwd, _bdot_nt_bwd)


def _rms(x, g, denom):
    r = lax.rsqrt(jnp.sum(x * x, axis=-1, keepdims=True) * (1.0 / denom) + NORM_EPS)
    return x * r * g


def _opspec(block, row_of, col_of, shards, ncol_tiles):
    if shards == 1:
        return pl.BlockSpec(block, lambda i, j, k: (row_of(i, j, k), col_of(i, j, k)))
    per = ncol_tiles // shards
    return pl.BlockSpec((None,) + block,
                        lambda i, j, k: (col_of(i, j, k) // per, row_of(i, j, k), col_of(i, j, k) % per))


def _call_with_exchange(body, args, exchange, *, name, grid, in_specs, out_specs, out_shape, scratch_shapes=()):
    xs, scatter = exchange
    n, n_in, n_out, n_scr = len(xs), len(in_specs), len(out_specs), len(scratch_shapes)
    x_shapes, sems = _chip_exchange_args(xs, scatter)

    def wrapped(*refs):
        ins, x_ins = refs[:n_in], refs[n_in:n_in + n]
        outs, x_outs = refs[n_in + n:n_in + n + n_out], refs[n_in + n + n_out:n_in + 2 * n + n_out]
        scratch, x_sems = refs[n_in + 2 * n + n_out:n_in + 2 * n + n_out + n_scr], refs[n_in + 2 * n + n_out + n_scr:]
        first, last = True, True
        for d, steps in enumerate(grid):
            first = first & (pl.program_id(d) == 0)
            last = last & (pl.program_id(d) == steps - 1)

        @pl.when(first)
        def _():
            _chip_exchange(x_ins, x_outs, x_sems, scatter, True)

        body(*ins, *outs, *scratch)

        @pl.when(last)
        def _():
            _chip_exchange(x_ins, x_outs, x_sems, scatter, False)

    res = pl.pallas_call(
        wrapped, name=name, grid=grid, in_specs=list(in_specs) + [ANY] * n, out_specs=list(out_specs) + [ANY] * n,
        out_shape=list(out_shape) + x_shapes, scratch_shapes=list(scratch_shapes) + sems,
        compiler_params=_cp(("arbitrary",) * len(grid)),
    )(*args, *xs)
    return res[:n_out], res[n_out:]


def mm(a, b, *, mode, name, tm=512, tn=512, tk=512, pro="none", epi="none", aux=None,
       out_dtype=f32, a_shards=1, b_shards=1, out_shards=1, exchange=None):
    ar, ac = a.shape[-2], a.shape[-1] * a_shards
    br, bc = b.shape[-2], b.shape[-1] * b_shards
    if mode == "nn":
        M, K, N = ar, ac, bc
        assert br == K
    elif mode == "nt":
        M, K, N = ar, ac, br
        assert bc == K
    else:
        M, K, N = ac, ar, bc
        assert br == K
    tm, tn, tk = _tile(M, tm), _tile(N, tn), _tile(K, tk)
    if a_shards > 1:
        if mode == "tn":
            tm = _tile(ac // a_shards, tm)
        else:
            tk = _tile(ac // a_shards, tk)
    if b_shards > 1:
        if mode == "nt":
            tk = _tile(bc // b_shards, tk)
        else:
            tn = _tile(bc // b_shards, tn)
    if out_shards > 1:
        tn = _tile(N // out_shards, tn)
    nm, nn_, nk = M // tm, N // tn, K // tk
    I = lambda i, j, k: i
    J = lambda i, j, k: j
    Kk = lambda i, j, k: k
    if mode == "nn":
        a_spec = _opspec((tm, tk), I, Kk, a_shards, nk)
        b_spec = _opspec((tk, tn), Kk, J, b_shards, nn_)
        dims = _NN
    elif mode == "nt":
        a_spec = _opspec((tm, tk), I, Kk, a_shards, nk)
        b_spec = _opspec((tn, tk), J, Kk, b_shards, nk)
        dims = _NT
    else:
        a_spec = _opspec((tk, tm), Kk, I, a_shards, nm)
        b_spec = _opspec((tk, tn), Kk, J, b_shards, nn_)
        dims = _TN
    in_specs = [a_spec, b_spec]
    args = [a, b]
    if epi != "none":
        in_specs.append(pl.BlockSpec((tm, tn), lambda i, j, k: (i, j)))
        args.append(aux)
    if out_shards == 1:
        out_spec = pl.BlockSpec((tm, tn), lambda i, j, k: (i, j))
        out_shape = jax.ShapeDtypeStruct((M, N), out_dtype)
    else:
        per = nn_ // out_shards
        out_spec = pl.BlockSpec((None, tm, tn), lambda i, j, k: (j // per, i, j % per))
        out_shape = jax.ShapeDtypeStruct((out_shards, M, N // out_shards), out_dtype)

    def body(*refs):
        a_ref, b_ref = refs[0], refs[1]
        pos = 2
        if epi != "none":
            aux_ref = refs[pos]
            pos += 1
        o_ref, acc_ref = refs[pos], refs[pos + 1]
        k = pl.program_id(2)

        @pl.when(k == 0)
        def _():
            acc_ref[...] = jnp.zeros_like(acc_ref)

        av = a_ref[...]
        if pro == "relu2":
            av = jnp.square(jnp.maximum(av.astype(f32), 0.0))
        acc_ref[...] += _dot(av, b_ref[...], dims)

        @pl.when(k == nk - 1)
        def _():
            res = acc_ref[...]
            if epi == "add":
                res = res + aux_ref[...].astype(f32)
            elif epi == "mul2relu":
                res = res * (2.0 * jnp.maximum(aux_ref[...].astype(f32), 0.0))
            o_ref[...] = res.astype(out_dtype)

    acc = [pltpu.VMEM((tm, tn), f32)]
    if exchange is not None:
        (out,), moved = _call_with_exchange(body, args, exchange, name=name, grid=(nm, nn_, nk), in_specs=in_specs,
                                            out_specs=[out_spec], out_shape=[out_shape], scratch_shapes=acc)
        return out, moved
    return pl.pallas_call(
        body, name=name, grid=(nm, nn_, nk), in_specs=in_specs, out_specs=out_spec, out_shape=out_shape,
        scratch_shapes=acc, compiler_params=_cp(("parallel", "parallel", "arbitrary")),
    )(*args)


def rms_norm(x, g, denom, *, name, ts=512):
    S, D = x.shape
    ts = _tile(S, ts)

    def body(x_ref, g_ref, h_ref):
        h_ref[...] = _rms(x_ref[...], g_ref[...], denom).astype(bf16)

    row = pl.BlockSpec((ts, D), lambda i: (i, 0))
    return pl.pallas_call(
        body, name=name, grid=(S // ts,), in_specs=[row, pl.BlockSpec((1, D), lambda i: (0, 0))], out_specs=row,
        out_shape=jax.ShapeDtypeStruct((S, D), bf16), compiler_params=_cp(("parallel",)),
    )(x, g)


def rms_bwd(x, g, dy, res, denom, *, name, ts=256, twin=False):
    S, D = x.shape
    ts = _tile(S, ts)
    has_res = res is not None

    def body(*refs):
        x_ref, g_ref, dy_ref = refs[:3]
        outs = refs[4:] if has_res else refs[3:]
        _, vjp = jax.vjp(lambda xv, gv: _rms(xv, gv, denom), x_ref[...], g_ref[...])
        dx, dg = vjp(dy_ref[...])
        if has_res:
            dx = dx + refs[3][...]
        outs[0][...] = dx
        if twin:
            outs[2][...] = dx.astype(bf16)
        dg_ref = outs[1]

        @pl.when(pl.program_id(0) == 0)
        def _():
            dg_ref[...] = jnp.zeros_like(dg_ref)

        dg_ref[...] += dg

    row = pl.BlockSpec((ts, D), lambda i: (i, 0))
    vec = pl.BlockSpec((1, D), lambda i: (0, 0))
    in_specs = [row, vec, row] + ([row] if has_res else [])
    args = [x, g, dy] + ([res] if has_res else [])
    return pl.pallas_call(
        body, name=name, grid=(S // ts,), in_specs=in_specs, out_specs=[row, vec] + ([row] if twin else []),
        out_shape=[jax.ShapeDtypeStruct((S, D), f32), jax.ShapeDtypeStruct((1, D), f32)]
        + ([jax.ShapeDtypeStruct((S, D), bf16)] if twin else []),
        compiler_params=_cp(("arbitrary",)),
    )(*args)


LOG2E = 1.4426950408889634
LN2 = 0.6931471805599453


def _qk_fn(q, k, gq, gk):
    qs, ks = [], []
    for h in range(SB_HEADS):
        sl = slice(h * LANES, (h + 1) * LANES)
        qs.append(_rms(q[:, sl], gq, SB_HEAD_DIM) * (SB_HEAD_DIM ** -0.5 * LOG2E))
        ks.append(_rms(k[:, sl], gk, SB_HEAD_DIM))
    return jnp.concatenate(qs, axis=1), jnp.concatenate(ks, axis=1)


def qkv_prep(qkv, gq, gk, *, ts=256):
    S = qkv.shape[0]
    W = SB_HEADS * LANES
    ts = _tile(S, ts)

    def body(q_ref, k_ref, v_ref, gq_ref, gk_ref, qn_ref, kn_ref, vb_ref):
        qn, kn = _qk_fn(q_ref[...], k_ref[...], gq_ref[...], gk_ref[...])
        qn_ref[...] = qn.astype(bf16)
        kn_ref[...] = kn.astype(bf16)
        vb_ref[...] = v_ref[...].astype(bf16)

    out = jax.ShapeDtypeStruct((S, W), bf16)
    gspec = pl.BlockSpec((1, LANES), lambda i: (0, 0))
    ospec = pl.BlockSpec((ts, W), lambda i: (i, 0))
    col = lambda c: pl.BlockSpec((ts, W), lambda i: (i, c))
    return pl.pallas_call(
        body, name="qkv_prep", grid=(S // ts,), in_specs=[col(0), col(1), col(2), gspec, gspec],
        out_specs=[ospec, ospec, ospec], out_shape=[out, out, out], compiler_params=_cp(("parallel",)),
    )(qkv, qkv, qkv, gq, gk)


def qkv_bwd(qkv, gq, gk, dqn, dkn, dv, exchange, *, ts=256):
    S = qkv.shape[0]
    W = SB_HEADS * LANES
    ts = _tile(S, ts)

    def body(q_ref, k_ref, gq_ref, gk_ref, dqn_ref, dkn_ref, dv_ref, o_ref, dgq_ref, dgk_ref):
        _, vjp = jax.vjp(_qk_fn, q_ref[...], k_ref[...], gq_ref[...], gk_ref[...])
        dq, dk, dgq, dgk = vjp((dqn_ref[...] * LN2, dkn_ref[...] * LN2))
        o_ref[:, 0:W] = dq.astype(bf16)
        o_ref[:, W:2 * W] = dk.astype(bf16)
        o_ref[:, 2 * W:3 * W] = dv_ref[...].astype(bf16)

        @pl.when(pl.program_id(0) == 0)
        def _():
            dgq_ref[...] = jnp.zeros_like(dgq_ref)
            dgk_ref[...] = jnp.zeros_like(dgk_ref)

        dgq_ref[...] += dgq
        dgk_ref[...] += dgk

    gspec = pl.BlockSpec((1, LANES), lambda i: (0, 0))
    row = pl.BlockSpec((ts, W), lambda i: (i, 0))
    col = lambda c: pl.BlockSpec((ts, W), lambda i: (i, c))
    (dqkv, dgq, dgk), moved = _call_with_exchange(
        body, (qkv, qkv, gq, gk, dqn, dkn, dv), exchange, name="qkv_bwd", grid=(S // ts,),
        in_specs=[col(0), col(1), gspec, gspec, row, row, row],
        out_specs=[pl.BlockSpec((ts, 3 * W), lambda i: (i, 0)), gspec, gspec],
        out_shape=[jax.ShapeDtypeStruct((S, 3 * W), bf16), jax.ShapeDtypeStruct((1, LANES), f32),
                   jax.ShapeDtypeStruct((1, LANES), f32)])
    return dqkv, dgq, dgk, moved


def _sb_weights(q, ks, R, masked, row, col, UU):
    ls = [_dot(q, k, _NT) for k in ks]
    lbs, lm0s, cats = [], [], []
    for l, diag in zip(ls, masked):
        neg_abs = pltpu.bitcast(pltpu.bitcast(l, jnp.uint32) | jnp.uint32(0x80000000), f32)
        lp = jnp.log2(1.0 + jnp.exp2(neg_abs))
        lb = jnp.minimum(l, 0.0) - lp
        lm = lb - l
        if diag:
            lm = jnp.where(col < row, lm, 0.0)
        hi = lm.astype(bf16)
        lo = (lm - hi.astype(f32)).astype(bf16)
        lbs.append(lb)
        lm0s.append(lm[:, 0:1])
        cats.append(jnp.concatenate([hi, lo], axis=1))
    sums = [_dot(c, UU, _NN) for c in cats]
    ws = []
    for lb, lm0, A, diag in zip(lbs, lm0s, sums, masked):
        w = jnp.exp2(lb + (A + R))
        if diag:
            w = jnp.where(col < row, w, 0.0)
        R = R + (A[:, 0:1] + lm0)
        ws.append(w)
    return lbs, ws, R


def _tri2(tk):
    r = lax.broadcasted_iota(jnp.int32, (2 * tk, tk), 0)
    r = jnp.where(r >= tk, r - tk, r)
    c = lax.broadcasted_iota(jnp.int32, (2 * tk, tk), 1)
    return (r > c).astype(bf16)


SB_GROUP = 8


SB_ALL_ZERO_BELOW = -160.0


def _sweep(i, blocks_of, carry, descending, right_sum=None, ran=None):
    G = SB_GROUP
    n = jnp.maximum(i - 1, 0)
    rem, full = n % G, n // G
    asc = lambda js: js if descending else js[::-1]

    def first_group(c):
        one = lambda c: blocks_of([i], c, [True])
        two = lambda c: blocks_of(asc([i, i - 1]), c, asc([True, False]))
        return lax.cond(i >= 1, two, one, c)

    def body(p, c):
        return blocks_of(asc([i - 2 - p * G - u for u in range(G)]), c, [False] * G)

    def left_over(r):
        return lambda c: blocks_of(asc([r - 1 - u for u in range(r)]), c, [False] * r) if r else c

    if descending:
        carry = first_group(carry)
        alive = lambda c: jnp.max(right_sum(c)) > SB_ALL_ZERO_BELOW
        bodies, carry = lax.while_loop(lambda s: (s[0] < full) & alive(s[1]),
                                       lambda s: (s[0] + 1, body(s[0], s[1])), (jnp.int32(0), carry))
        tail = (bodies == full) & alive(carry)
        carry = lax.switch(jnp.where(tail, rem, 0), [left_over(r) for r in range(G)], carry)
        return carry, (bodies, tail)
    bodies, tail = ran
    carry = lax.switch(jnp.where(tail, rem, 0), [left_over(r) for r in range(G)], carry)
    carry = lax.fori_loop(0, bodies, lambda t, c: body(bodies - 1 - t, c), carry)
    return first_group(carry)


def sb_fwd(qn, kn, vb, *, tq=256):
    S, W = qn.shape
    H = W // LANES
    tq = _tile(S, tq)
    tk = tq
    nq = S // tq

    def body(q_ref, k_ref, v_ref, o_ref):
        i = pl.program_id(1)
        q = q_ref[...]
        row = lax.broadcasted_iota(jnp.int32, (tq, tk), 0)
        col = lax.broadcasted_iota(jnp.int32, (tq, tk), 1)
        UU = _tri2(tk)

        def blocks(js, c, masked):
            rows = [pl.ds(pl.multiple_of(j * tk, tk), tk) for j in js]
            _, ws, R = _sb_weights(q, [k_ref[r, :] for r in rows], c[0], masked, row, col, UU)
            acc = c[1]
            for w, r in zip(ws, rows):
                acc = acc + _dot(w, v_ref[r, :], _NN)
            return R, acc

        c, _ = _sweep(i, blocks, (jnp.zeros((tq, 1), f32), jnp.zeros((tq, LANES), f32)), True, lambda c: c[0])
        o_ref[...] = c[1]

    qspec = pl.BlockSpec((tq, LANES), lambda h, i: (i, h))
    kspec = pl.BlockSpec((S, LANES), lambda h, i: (0, h))
    return pl.pallas_call(
        body, name="sb_fwd", grid=(H, nq), in_specs=[qspec, kspec, kspec], out_specs=qspec,
        out_shape=jax.ShapeDtypeStruct((S, W), f32), compiler_params=_cp(("parallel", "arbitrary")),
    )(qn, kn, vb)


def sb_bwd(qn, kn, vb, do, *, tq=256):
    S, W = qn.shape
    H = W // LANES
    tq = _tile(S, tq)
    tk = tq
    nq = S // tq

    def body(q_ref, k_ref, v_ref, do_ref, dq_ref, dk_ref, dv_ref, dz_s, beta_s):
        i = pl.program_id(1)

        @pl.when(i == 0)
        def _():
            dk_ref[...] = jnp.zeros_like(dk_ref)
            dv_ref[...] = jnp.zeros_like(dv_ref)

        q = q_ref[...]
        dob = do_ref[...].astype(bf16)
        row = lax.broadcasted_iota(jnp.int32, (tq, tk), 0)
        col = lax.broadcasted_iota(jnp.int32, (tq, tk), 1)
        UU = _tri2(tk)
        Ue = (row < col).astype(bf16)

        def sweep1(js, R, masked):
            rows = [pl.ds(pl.multiple_of(j * tk, tk), tk) for j in js]
            dws = [_dot(dob, v_ref[r, :], _NT) for r in rows]
            lbs, ws, R = _sb_weights(q, [k_ref[r, :] for r in rows], R, masked, row, col, UU)
            for j, lb, w, dw in zip(js, lbs, ws, dws):
                dz_s[j] = (dw * w).astype(bf16)
                beta_s[j] = jnp.exp2(lb).astype(bf16)
            for r, w in zip(rows, ws):
                dv_ref[r, :] += _dot(w, dob, _TN)
            return R

        _, ran = _sweep(i, sweep1, jnp.zeros((tq, 1), f32), True, lambda R: R)

        def sweep2(js, c, masked):
            rows = [pl.ds(pl.multiple_of(j * tk, tk), tk) for j in js]
            dzbs = [dz_s[j] for j in js]
            sums = [_dot(dzb, Ue, _NN) for dzb in dzbs]
            Lz, dq = c
            dlbs = []
            for j, dzb, Cz, diag in zip(js, dzbs, sums, masked):
                dz = dzb.astype(f32)
                dl = dz - beta_s[j].astype(f32) * (dz + (Cz + Lz))
                if diag:
                    dl = jnp.where(col < row, dl, 0.0)
                Lz = Lz + (Cz[:, tk - 1:tk] + dz[:, tk - 1:tk])
                dlbs.append(dl.astype(bf16))
            for r, dlb in zip(rows, dlbs):
                dq = dq + _dot(dlb, k_ref[r, :], _NN)
            for r, dlb in zip(rows, dlbs):
                dk_ref[r, :] += _dot(dlb, q, _TN)
            return Lz, dq

        c = _sweep(i, sweep2, (jnp.zeros((tq, 1), f32), jnp.zeros((tq, LANES), f32)), False, ran=ran)
        dq_ref[...] = c[1]

    qspec = pl.BlockSpec((tq, LANES), lambda h, i: (i, h))
    kspec = pl.BlockSpec((S, LANES), lambda h, i: (0, h))
    full = jax.ShapeDtypeStruct((S, W), f32)
    return pl.pallas_call(
        body, name="sb_bwd", grid=(H, nq), in_specs=[qspec, kspec, kspec, qspec],
        out_specs=[qspec, kspec, kspec], out_shape=[full, full, full],
        scratch_shapes=[pltpu.VMEM((nq, tq, tk), bf16), pltpu.VMEM((nq, tq, tk), bf16)],
        compiler_params=_cp(("parallel", "arbitrary")),
    )(qn, kn, vb, do)


def _xa_fn(qx, kv, gq, gk):
    XW = XA_HEADS * XA_HEAD_DIM
    outs = []
    for h in range(XA_HEADS):
        sl = slice(h * XA_HEAD_DIM, (h + 1) * XA_HEAD_DIM)
        qn = _rms(qx[:, sl], gq, XA_HEAD_DIM)
        kn = _rms(kv[:, sl], gk, XA_HEAD_DIM)
        v = kv[:, XW + h * XA_HEAD_DIM:XW + (h + 1) * XA_HEAD_DIM]
        s = bdot_nt(qn, kn) * (XA_HEAD_DIM ** -0.5)
        e = jnp.exp(s - lax.stop_gradient(jnp.max(s, axis=-1, keepdims=True)))
        p = e / jnp.sum(e, axis=-1, keepdims=True)
        outs.append(bdot_nn(p, v))
    return jnp.concatenate(outs, axis=1)


def xa_fwd(qx, kv, gq, gk, *, ts=256):
    S, XW = qx.shape
    M = kv.shape[0]
    ts = _tile(S, ts)

    def body(q_ref, kv_ref, gq_ref, gk_ref, o_ref):
        o_ref[...] = _xa_fn(q_ref[...], kv_ref[...], gq_ref[...], gk_ref[...]).astype(bf16)

    row = pl.BlockSpec((ts, XW), lambda i: (i, 0))
    gspec = pl.BlockSpec((1, XA_HEAD_DIM), lambda i: (0, 0))
    return pl.pallas_call(
        body, name="xa_fwd", grid=(S // ts,),
        in_specs=[row, pl.BlockSpec((M, 2 * XW), lambda i: (0, 0)), gspec, gspec], out_specs=row,
        out_shape=jax.ShapeDtypeStruct((S, XW), bf16), compiler_params=_cp(("parallel",)),
    )(qx, kv, gq, gk)


def xa_bwd(qx, kv, gq, gk, do, *, ts=256):
    S, XW = qx.shape
    M = kv.shape[0]
    ts = _tile(S, ts)

    def body(q_ref, kv_ref, gq_ref, gk_ref, do_ref, dq_ref, dkv_ref, dgq_ref, dgk_ref):
        _, vjp = jax.vjp(_xa_fn, q_ref[...], kv_ref[...], gq_ref[...], gk_ref[...])
        dq, dkv, dgq, dgk = vjp(do_ref[...].astype(f32))
        dq_ref[...] = dq.astype(bf16)

        @pl.when(pl.program_id(0) == 0)
        def _():
            dkv_ref[...] = jnp.zeros_like(dkv_ref)
            dgq_ref[...] = jnp.zeros_like(dgq_ref)
            dgk_ref[...] = jnp.zeros_like(dgk_ref)

        dkv_ref[...] += dkv
        dgq_ref[...] += dgq
        dgk_ref[...] += dgk

    row = pl.BlockSpec((ts, XW), lambda i: (i, 0))
    gspec = pl.BlockSpec((1, XA_HEAD_DIM), lambda i: (0, 0))
    kvspec = pl.BlockSpec((M, 2 * XW), lambda i: (0, 0))
    gshape = jax.ShapeDtypeStruct((1, XA_HEAD_DIM), f32)
    return pl.pallas_call(
        body, name="xa_bwd", grid=(S // ts,), in_specs=[row, kvspec, gspec, gspec, row],
        out_specs=[row, kvspec, gspec, gspec],
        out_shape=[jax.ShapeDtypeStruct((S, XW), bf16), jax.ShapeDtypeStruct((M, 2 * XW), f32), gshape, gshape],
        compiler_params=_cp(("arbitrary",)),
    )(qx, kv, gq, gk, do)


def _s5_prep_fn(a_re, a_im, ldt, bT_re, bT_im, cT_re, cT_im):
    G, P, C = SSM_GROUPS, SSM_STATE, SSM_GROUP
    GP, GC = G * P, G * C
    lg_p, lg_c = P.bit_length() - 1, C.bit_length() - 1
    gi = lax.broadcasted_iota(jnp.int32, (G, GP), 0)
    ci = lax.broadcasted_iota(jnp.int32, (G, GP), 1) >> lg_p
    expand_dt = (gi == ci).astype(f32)
    dte = jnp.dot(jnp.exp(ldt), expand_dt, precision=lax.Precision.HIGHEST, preferred_element_type=f32)
    zr, zi = a_re * dte, a_im * dte
    mag = jnp.exp(zr)
    abr, abi = mag * jnp.cos(zi), mag * jnp.sin(zi)
    nr, ni = abr - 1.0, abi
    den = a_re * a_re + a_im * a_im
    cr = (nr * a_re + ni * a_im) / den
    cim = (ni * a_re - nr * a_im) / den
    bbr = cr * bT_re - cim * bT_im
    bbi = cr * bT_im + cim * bT_re
    rowg = lax.broadcasted_iota(jnp.int32, (GC, GP), 0) >> lg_c
    colg = lax.broadcasted_iota(jnp.int32, (GC, GP), 1) >> lg_p
    diag = rowg == colg

    def expand(t):
        return jnp.where(diag, jnp.broadcast_to(t[None], (G, C, GP)).reshape(GC, GP), 0.0)

    return abr, abi, expand(bbr), expand(bbi), expand(cT_re), expand(-cT_im)


def s5_prep(a_re, a_im, ldt, bT_re, bT_im, cT_re, cT_im):
    GP, GC = SSM_GROUPS * SSM_STATE, SSM_GROUPS * SSM_GROUP

    def body(a_re_ref, a_im_ref, ldt_ref, bTr_ref, bTi_ref, cTr_ref, cTi_ref, abr_ref, abi_ref, B_ref, C_ref):
        abr, abi, Br, Bi, Cr, Ci = _s5_prep_fn(a_re_ref[...], a_im_ref[...], ldt_ref[...], bTr_ref[...],
                                               bTi_ref[...], cTr_ref[...], cTi_ref[...])
        abr_ref[...] = abr
        abi_ref[...] = abi
        B_ref[0] = Br
        B_ref[1] = Bi
        C_ref[0] = Cr
        C_ref[1] = Ci

    vec = jax.ShapeDtypeStruct((1, GP), f32)
    mat = jax.ShapeDtypeStruct((2, GC, GP), f32)
    return pl.pallas_call(body, name="s5_prep", out_shape=[vec, vec, mat, mat], compiler_params=_cp())(
        a_re, a_im, ldt, bT_re, bT_im, cT_re, cT_im)


def s5_prep_bwd(a_re, a_im, ldt, bT_re, bT_im, cT_re, cT_im, dabr, dabi, dB, dC):
    def body(a_re_ref, a_im_ref, ldt_ref, bTr_ref, bTi_ref, cTr_ref, cTi_ref, dabr_ref, dabi_ref, dB_ref, dC_ref,
             *outs):
        _, vjp = jax.vjp(_s5_prep_fn, a_re_ref[...], a_im_ref[...], ldt_ref[...], bTr_ref[...], bTi_ref[...],
                         cTr_ref[...], cTi_ref[...])
        grads = vjp((dabr_ref[...], dabi_ref[...], dB_ref[0], dB_ref[1], dC_ref[0], dC_ref[1]))
        for o_ref, gv in zip(outs, grads):
            o_ref[...] = gv

    ins = (a_re, a_im, ldt, bT_re, bT_im, cT_re, cT_im)
    return pl.pallas_call(body, name="s5_prep_bwd", out_shape=[jax.ShapeDtypeStruct(v.shape, f32) for v in ins],
                          compiler_params=_cp())(*ins, dabr, dabi, dB, dC)


def _cmul(ar, ai, br, bi):
    return ar * br - ai * bi, ar * bi + ai * br


SCAN_CHUNKS = 32


def _chunk_carry(Lr, Li, Pr, Pi, scratch, reverse):
    lr_ref, li_ref, cr_ref, ci_ref = scratch
    lr_ref[...] = Lr
    li_ref[...] = Li
    cur_r = jnp.zeros((1, LANES), f32)
    cur_i = jnp.zeros((1, LANES), f32)
    order = range(SCAN_CHUNKS - 1, -1, -1) if reverse else range(SCAN_CHUNKS)
    for c in order:
        cr_ref[pl.ds(c, 1), :] = cur_r
        ci_ref[pl.ds(c, 1), :] = cur_i
        mr, mi = _cmul(Pr, Pi, cur_r, cur_i)
        cur_r, cur_i = lr_ref[pl.ds(c, 1), :] + mr, li_ref[pl.ds(c, 1), :] + mi
    return cr_ref[...], ci_ref[...]


def _chunk_rows(j):
    return pl.ds(pl.multiple_of(j * SCAN_CHUNKS, SCAN_CHUNKS), SCAN_CHUNKS)


def row_shuffle(x, a, b, *, name, add=None, out_dtype=f32):
    S, W = x.shape
    assert a * b == S and x.dtype == f32

    def body(*refs):
        x_ref, o_ref = refs[0], refs[-1]

        def step(i, _):
            dst = pl.ds(pl.multiple_of(i * b, b), b)
            v = x_ref[pl.ds(i, b, stride=a), :]
            if add is not None:
                v = v + refs[1][dst, :]
            o_ref[dst, :] = v.astype(out_dtype)
            return 0

        lax.fori_loop(0, a, step, 0)

    col = pl.BlockSpec((S, LANES), lambda t: (0, t))
    args = [x] + ([add] if add is not None else [])
    return pl.pallas_call(
        body, name=name, grid=(W // LANES,), in_specs=[col] * len(args), out_specs=col,
        out_shape=jax.ShapeDtypeStruct((S, W), out_dtype), compiler_params=_cp(("parallel",)),
    )(*args)


def _scan_scratch(n):
    small = pltpu.VMEM((SCAN_CHUNKS, LANES), f32)
    return [pltpu.VMEM((n, LANES), f32), pltpu.VMEM((n, LANES), f32), small, small, small, small]


def scan_fwd(bu, abr, abi):
    _, S, N = bu.shape
    n = S // SCAN_CHUNKS
    shp = (SCAN_CHUNKS, LANES)

    def body(bu_ref, ar_ref, ai_ref, st_ref, pwr_ref, pwi_ref, *scratch):
        a1r, a1i = ar_ref[...], ai_ref[...]
        ar = jnp.broadcast_to(a1r, shp)
        ai = jnp.broadcast_to(a1i, shp)
        xr_ref, xi_ref = bu_ref.at[0], bu_ref.at[1]
        sr_ref, si_ref = st_ref.at[0], st_ref.at[1]

        def step(j, c):
            sr, si, pr, pi = c
            rows = _chunk_rows(j)
            mr, mi = _cmul(ar, ai, sr, si)
            sr, si = mr + xr_ref[rows, :], mi + xi_ref[rows, :]
            sr_ref[rows, :] = sr
            si_ref[rows, :] = si
            pwr_ref[pl.ds(j, 1), :] = pr
            pwi_ref[pl.ds(j, 1), :] = pi
            npr, npi = _cmul(a1r, a1i, pr, pi)
            return sr, si, npr, npi

        z = jnp.zeros(shp, f32)
        sr, si, _, _ = lax.fori_loop(0, n, step, (z, z, a1r, a1i), unroll=2)
        cr, ci = _chunk_carry(sr, si, pwr_ref[pl.ds(n - 1, 1), :], pwi_ref[pl.ds(n - 1, 1), :], scratch, False)

        def step2(j, _):
            rows = _chunk_rows(j)
            pr = jnp.broadcast_to(pwr_ref[pl.ds(j, 1), :], shp)
            pi = jnp.broadcast_to(pwi_ref[pl.ds(j, 1), :], shp)
            mr, mi = _cmul(pr, pi, cr, ci)
            sr_ref[rows, :] += mr
            si_ref[rows, :] += mi
            return 0

        lax.fori_loop(0, n, step2, 0, unroll=4)

    blk = pl.BlockSpec((2, S, LANES), lambda t: (0, 0, t))
    vec = pl.BlockSpec((1, LANES), lambda t: (0, t))
    return pl.pallas_call(
        body, name="scan_fwd", grid=(N // LANES,), in_specs=[blk, vec, vec], out_specs=blk,
        out_shape=jax.ShapeDtypeStruct((2, S, N), f32), scratch_shapes=_scan_scratch(n),
        compiler_params=_cp(("parallel",)),
    )(bu, abr, abi)


def scan_bwd(G, st, abr, abi, exchange):
    _, S, N = G.shape
    n = S // SCAN_CHUNKS
    shp = (SCAN_CHUNKS, LANES)

    def body(G_ref, st_ref, ar_ref, ai_ref, g_ref, dar_ref, dai_ref, qwr_ref, qwi_ref, *scratch):
        a1r, a1i = ar_ref[...], -ai_ref[...]
        ar = jnp.broadcast_to(a1r, shp)
        nai = jnp.broadcast_to(a1i, shp)
        Gr_ref, Gi_ref = G_ref.at[0], G_ref.at[1]
        sr_ref, si_ref = st_ref.at[0], st_ref.at[1]
        gr_ref, gi_ref = g_ref.at[0], g_ref.at[1]

        def step(jj, c):
            gr, gi, qr, qi = c
            j = n - 1 - jj
            rows = _chunk_rows(j)
            mr, mi = _cmul(ar, nai, gr, gi)
            gr, gi = mr + Gr_ref[rows, :], mi + Gi_ref[rows, :]
            gr_ref[rows, :] = gr
            gi_ref[rows, :] = gi
            qwr_ref[pl.ds(j, 1), :] = qr
            qwi_ref[pl.ds(j, 1), :] = qi
            nqr, nqi = _cmul(a1r, a1i, qr, qi)
            return gr, gi, nqr, nqi

        z = jnp.zeros(shp, f32)
        gr, gi, _, _ = lax.fori_loop(0, n, step, (z, z, a1r, a1i), unroll=2)
        cr, ci = _chunk_carry(gr, gi, qwr_ref[pl.ds(0, 1), :], qwi_ref[pl.ds(0, 1), :], scratch, True)
        sub = lax.broadcasted_iota(jnp.int32, shp, 0)

        def fix(j, spr, spi, acc):
            rows = _chunk_rows(j)
            qr = jnp.broadcast_to(qwr_ref[pl.ds(j, 1), :], shp)
            qi = jnp.broadcast_to(qwi_ref[pl.ds(j, 1), :], shp)
            mr, mi = _cmul(qr, qi, cr, ci)
            gr = gr_ref[rows, :] + mr
            gi = gi_ref[rows, :] + mi
            gr_ref[rows, :] = gr
            gi_ref[rows, :] = gi
            return acc[0] + gr * spr + gi * spi, acc[1] + gi * spr - gr * spi

        last = _chunk_rows(n - 1)
        spr = jnp.where(sub == 0, 0.0, pltpu.roll(sr_ref[last, :], 1, 0))
        spi = jnp.where(sub == 0, 0.0, pltpu.roll(si_ref[last, :], 1, 0))
        acc = fix(0, spr, spi, (z, z))

        def step2(j, acc):
            prev = _chunk_rows(j - 1)
            return fix(j, sr_ref[prev, :], si_ref[prev, :], acc)

        acc = lax.fori_loop(1, n, step2, acc)
        dar_ref[...] = jnp.sum(acc[0], axis=0, keepdims=True)
        dai_ref[...] = jnp.sum(acc[1], axis=0, keepdims=True)

    blk = pl.BlockSpec((2, S, LANES), lambda t: (0, 0, t))
    vec = pl.BlockSpec((1, LANES), lambda t: (0, t))
    vshape = jax.ShapeDtypeStruct((1, N), f32)
    (g, dar, dai), moved = _call_with_exchange(
        body, (G, st, abr, abi), exchange, name="scan_bwd", grid=(N // LANES,), in_specs=[blk, blk, vec, vec],
        out_specs=[blk, vec, vec], out_shape=[jax.ShapeDtypeStruct((2, S, N), f32), vshape, vshape],
        scratch_shapes=_scan_scratch(n))
    return g, dar, dai, moved


def _glu_fn(ypre, wglu):
    y = jax.nn.gelu(ypre)
    return y * jax.nn.sigmoid(bdot_nn(y, wglu))


def glu_fwd(ypre0, u, d, wglu, g_out, *, ts=512):
    S, W = u.shape
    ts = _tile(S, ts)

    def body(y0_ref, u_ref, d_ref, w_ref, g_ref, ypre_ref, z_ref, zn_ref):
        ypre = y0_ref[...] + d_ref[...] * u_ref[...]
        z = _glu_fn(ypre, w_ref[...])
        ypre_ref[...] = ypre
        z_ref[...] = z
        zn_ref[...] = _rms(z, g_ref[...], W).astype(bf16)

    row = pl.BlockSpec((ts, W), lambda i: (i, 0))
    vec = pl.BlockSpec((1, W), lambda i: (0, 0))
    full = jax.ShapeDtypeStruct((S, W), f32)
    return pl.pallas_call(
        body, name="glu_fwd", grid=(S // ts,),
        in_specs=[row, row, vec, pl.BlockSpec((W, W), lambda i: (0, 0)), vec], out_specs=[row, row, row],
        out_shape=[full, full, jax.ShapeDtypeStruct((S, W), bf16)], compiler_params=_cp(("parallel",)),
    )(ypre0, u, d, wglu, g_out)


def glu_bwd(ypre, u, d, wglu, dz, *, ts=512):
    S, W = u.shape
    ts = _tile(S, ts)

    def body(y_ref, u_ref, d_ref, w_ref, dz_ref, dy_ref, du_ref, dw_ref, dd_ref):
        _, vjp = jax.vjp(_glu_fn, y_ref[...], w_ref[...])
        dy, dw = vjp(dz_ref[...])
        dy_ref[...] = dy
        du_ref[...] = d_ref[...] * dy

        @pl.when(pl.program_id(0) == 0)
        def _():
            dw_ref[...] = jnp.zeros_like(dw_ref)
            dd_ref[...] = jnp.zeros_like(dd_ref)

        dw_ref[...] += dw
        dd_ref[...] += jnp.sum(dy * u_ref[...], axis=0, keepdims=True)

    row = pl.BlockSpec((ts, W), lambda i: (i, 0))
    vec = pl.BlockSpec((1, W), lambda i: (0, 0))
    sq = pl.BlockSpec((W, W), lambda i: (0, 0))
    full = jax.ShapeDtypeStruct((S, W), f32)
    return pl.pallas_call(
        body, name="glu_bwd", grid=(S // ts,), in_specs=[row, row, vec, sq, row], out_specs=[row, row, sq, vec],
        out_shape=[full, full, jax.ShapeDtypeStruct((W, W), f32), jax.ShapeDtypeStruct((1, W), f32)],
        compiler_params=_cp(("arbitrary",)),
    )(ypre, u, d, wglu, dz)


def loss_head(y, target, *, ts=512):
    S, D = y.shape
    ts = _tile(S, ts)

    def body(y_ref, t_ref, dy_ref, l_ref, dyb_ref):
        err = y_ref[...] - t_ref[...]
        dy_ref[...] = err * (1.0 / D)
        dyb_ref[...] = (err * (1.0 / D)).astype(bf16)

        @pl.when(pl.program_id(0) == 0)
        def _():
            l_ref[...] = jnp.zeros_like(l_ref)

        rows = jnp.sum(err * err, axis=1, keepdims=True) * (1.0 / D)
        l_ref[...] += 0.5 * jnp.sum(rows, axis=0, keepdims=True)

    row = pl.BlockSpec((ts, D), lambda i: (i, 0))
    return pl.pallas_call(
        body, name="loss_head", grid=(S // ts,), in_specs=[row, row],
        out_specs=[row, pl.BlockSpec((1, 1), lambda i: (0, 0)), row],
        out_shape=[jax.ShapeDtypeStruct((S, D), f32), jax.ShapeDtypeStruct((1, 1), f32),
                   jax.ShapeDtypeStruct((S, D), bf16)],
        compiler_params=_cp(("arbitrary",)),
    )(y, target)


def adamw(w, g, m, v, *, name, tr=256):
    R, C = w.shape
    tr = _row_tile(R, tr)

    def body(w_ref, g_ref, m_ref, v_ref, d_ref, nm_ref, nv_ref):
        gv = g_ref[...]
        nm = ADAM_B1 * m_ref[...] + (1.0 - ADAM_B1) * gv
        nv = ADAM_B2 * v_ref[...] + (1.0 - ADAM_B2) * jnp.square(gv)
        m_hat = nm / (1.0 - ADAM_B1 ** ADAM_STEP)
        v_hat = nv / (1.0 - ADAM_B2 ** ADAM_STEP)
        d_ref[...] = -ADAM_LR * (m_hat / (jnp.sqrt(v_hat) + ADAM_EPS) + ADAM_WD * w_ref[...])
        nm_ref[...] = nm
        nv_ref[...] = nv

    row = pl.BlockSpec((tr, C), lambda i: (i, 0))
    full = jax.ShapeDtypeStruct((R, C), f32)
    return pl.pallas_call(
        body, name=name, grid=(R // tr,), in_specs=[row] * 4, out_specs=[row] * 3, out_shape=[full] * 3,
        compiler_params=_cp(("parallel",)),
    )(w, g, m, v)


def add_half(g4, recv, c, *, name, tr=256):
    _, _, Rh, C = g4.shape
    tr = _row_tile(Rh, tr)

    def body(c_ref, a_ref, b_ref, o_ref):
        o_ref[...] = a_ref[...] + b_ref[...]

    grid_spec = pltpu.PrefetchScalarGridSpec(
        num_scalar_prefetch=1, grid=(N_CHIPS, Rh // tr),
        in_specs=[pl.BlockSpec((None, None, tr, C), lambda k, i, c_ref: (k, c_ref[0], i, 0)),
                  pl.BlockSpec((None, tr, C), lambda k, i, c_ref: (k, i, 0))],
        out_specs=pl.BlockSpec((None, tr, C), lambda k, i, c_ref: (k, i, 0)))
    return pl.pallas_call(body, name=name, grid_spec=grid_spec, out_shape=jax.ShapeDtypeStruct(recv.shape, f32),
                          compiler_params=_cp(("parallel", "parallel")))(c, g4, recv)


def sum_chips(p4, *, name, tr=256):
    _, Rh, C = p4.shape
    tr = _row_tile(Rh, tr)

    def body(a_ref, b_ref, c_ref, d_ref, o_ref):
        o_ref[...] = ((a_ref[...] + b_ref[...]) + c_ref[...]) + d_ref[...]

    spec = lambda k: pl.BlockSpec((None, tr, C), lambda i: (k, i, 0))
    return pl.pallas_call(
        body, name=name, grid=(Rh // tr,), in_specs=[spec(0), spec(1), spec(2), spec(3)],
        out_specs=pl.BlockSpec((tr, C), lambda i: (i, 0)), out_shape=jax.ShapeDtypeStruct((Rh, C), f32),
        compiler_params=_cp(("parallel",)),
    )(p4, p4, p4, p4)


def _place():
    return lax.axis_index("x"), lax.axis_index("y"), lax.axis_index("c")


def _other_chips(x, y):
    return [(1 - x, y), (x, 1 - y), (1 - x, 1 - y)]


def _chip_exchange(ins, outs, sems, scatter, start):
    if not ins:
        return
    send, recv, loc = sems
    x, y, c = _place()
    me = 2 * x + y
    for a in range(len(ins)):
        own = pltpu.make_async_copy(ins[a].at[me] if scatter else ins[a], outs[a].at[me], loc.at[a])
        own.start() if start else own.wait()
        for p, (px, py) in enumerate(_other_chips(x, y)):
            k = 2 * px + py
            cp = pltpu.make_async_remote_copy(
                src_ref=ins[a].at[k] if scatter else ins[a], dst_ref=outs[a].at[me if start else k],
                send_sem=send.at[3 * a + p], recv_sem=recv.at[3 * a + p], device_id=(px, py, c), device_id_type=MESH)
            cp.start() if start else cp.wait()


def _chip_exchange_args(arrs, scatter):
    n = len(arrs)
    shapes = [jax.ShapeDtypeStruct(a.shape if scatter else (N_CHIPS,) + a.shape, a.dtype) for a in arrs]
    sems = [pltpu.SemaphoreType.DMA((3 * n,)), pltpu.SemaphoreType.DMA((3 * n,)), pltpu.SemaphoreType.DMA((n,))]
    return shapes, sems if n else []


def _chip_exchange_call(arrs, scatter, name):
    n = len(arrs)

    def body(*refs):
        ins, outs, sems = refs[:n], refs[n:2 * n], refs[2 * n:]
        _chip_exchange(ins, outs, sems, scatter, True)
        _chip_exchange(ins, outs, sems, scatter, False)

    shapes, sems = _chip_exchange_args(arrs, scatter)
    return pl.pallas_call(
        body, name=name, in_specs=[ANY] * n, out_specs=[ANY] * n, out_shape=shapes, scratch_shapes=sems,
        compiler_params=pltpu.CompilerParams(has_side_effects=True),
    )(*arrs)


def allgather_chips(arrs, *, name):
    return _chip_exchange_call(arrs, False, name)


def sibling_swap(arrs, *, half, name):
    n = len(arrs)

    def body(*refs):
        ins, outs = refs[:n], refs[n:2 * n]
        send, recv = refs[2 * n:]
        x, y, c = _place()
        cps = []
        for a in range(n):
            src = ins[a].at[:, 1 - c] if half else ins[a]
            cp = pltpu.make_async_remote_copy(src_ref=src, dst_ref=outs[a], send_sem=send.at[a], recv_sem=recv.at[a],
                                              device_id=(x, y, 1 - c), device_id_type=MESH)
            cp.start()
            cps.append(cp)
        for cp in cps:
            cp.wait()

    def oshape(a):
        return jax.ShapeDtypeStruct((a.shape[0],) + a.shape[2:] if half else a.shape, a.dtype)

    return pl.pallas_call(
        body, name=name, in_specs=[ANY] * n, out_specs=[ANY] * n, out_shape=[oshape(a) for a in arrs],
        scratch_shapes=[pltpu.SemaphoreType.DMA((n,)), pltpu.SemaphoreType.DMA((n,))],
        compiler_params=pltpu.CompilerParams(has_side_effects=True),
    )(*arrs)


def chip_scatter(arrs, *, name):
    return _chip_exchange_call(arrs, True, name)


def _pad_cols(w):
    K = w.shape[0]
    w = w.reshape(K, -1, SB_HEAD_DIM)
    return jnp.pad(w, ((0, 0), (0, 0), (0, LANES - SB_HEAD_DIM))).reshape(K, -1)


def _unpad_cols(w):
    K = w.shape[0]
    return w.reshape(K, -1, LANES)[:, :, :SB_HEAD_DIM].reshape(K, -1)


def _pad_rows(w):
    N = w.shape[1]
    w = w.reshape(-1, SB_HEAD_DIM, N)
    return jnp.pad(w, ((0, 0), (0, LANES - SB_HEAD_DIM), (0, 0))).reshape(-1, N)


def _unpad_rows(w):
    N = w.shape[1]
    return w.reshape(-1, LANES, N)[:, :SB_HEAD_DIM, :].reshape(-1, N)


_PACK_ROWS = N_CHIPS * 2 * SUBLANES


def _pack(arrs):
    flat = jnp.concatenate([a.reshape(-1) for a in arrs])
    rows = -(-flat.shape[0] // LANES)
    rows = -(-rows // _PACK_ROWS) * _PACK_ROWS
    return jnp.pad(flat, (0, rows * LANES - flat.shape[0])).reshape(rows, LANES)


def _unpack(buf, shapes):
    flat = buf.reshape(-1)
    out, pos = [], 0
    for shp in shapes:
        size = 1
        for d in shp:
            size *= d
        out.append(flat[pos:pos + size].reshape(shp))
        pos += size
    return out


BIG = ("w_in", "ssm_w_glu", "w_out", "xa_w_q", "xa_w_kv", "xa_w_o", "w_up", "w_down")
SMALL = ("g_mix", "ssm_a_re", "ssm_a_im", "ssm_log_dt", "ssm_b_re", "ssm_b_im", "ssm_c_re", "ssm_c_im", "ssm_d",
         "sb_g_q", "sb_g_k", "g_out_ssm", "g_out_sb", "g_xa", "g_mem", "xa_g_q", "xa_g_k", "g_mlp")
WEIGHTS = ("g_mix", "w_in", "ssm_a_re", "ssm_a_im", "ssm_log_dt", "ssm_b_re", "ssm_b_im", "ssm_c_re", "ssm_c_im",
           "ssm_d", "ssm_w_glu", "sb_g_q", "sb_g_k", "g_out_ssm", "g_out_sb", "w_out", "g_xa", "g_mem", "xa_w_q",
           "xa_w_kv", "xa_g_q", "xa_g_k", "xa_w_o", "g_mlp", "w_up", "w_down")


def kernel(x, mem, g_mix, w_in, ssm_a_re, ssm_a_im, ssm_log_dt, ssm_b_re, ssm_b_im, ssm_c_re, ssm_c_im, ssm_d, ssm_w_glu, sb_g_q, sb_g_k, g_out_ssm, g_out_sb, w_out, g_xa, g_mem, xa_w_q, xa_w_kv, xa_g_q, xa_g_k, xa_w_o, g_mlp, w_up, w_down, loss_target, m_g_mix, m_w_in, m_ssm_a_re, m_ssm_a_im, m_ssm_log_dt, m_ssm_b_re, m_ssm_b_im, m_ssm_c_re, m_ssm_c_im, m_ssm_d, m_ssm_w_glu, m_sb_g_q, m_sb_g_k, m_g_out_ssm, m_g_out_sb, m_w_out, m_g_xa, m_g_mem, m_xa_w_q, m_xa_w_kv, m_xa_g_q, m_xa_g_k, m_xa_w_o, m_g_mlp, m_w_up, m_w_down, v_g_mix, v_w_in, v_ssm_a_re, v_ssm_a_im, v_ssm_log_dt, v_ssm_b_re, v_ssm_b_im, v_ssm_c_re, v_ssm_c_im, v_ssm_d, v_ssm_w_glu, v_sb_g_q, v_sb_g_k, v_g_out_ssm, v_g_out_sb, v_w_out, v_g_xa, v_g_mem, v_xa_w_q, v_xa_w_kv, v_xa_g_q, v_xa_g_k, v_xa_w_o, v_g_mlp, v_w_up, v_w_down):
    env = dict(locals())
    W = {n: env[n] for n in WEIGHTS}
    M1 = {n: env["m_" + n] for n in WEIGHTS}
    V2 = {n: env["v_" + n] for n in WEIGHTS}
    xs, mems, tgt = x[0], mem[0], loss_target[0]
    S, D = xs.shape
    G, P, C = SSM_GROUPS, SSM_STATE, SSM_GROUP
    GP = G * P
    SBW = SB_HEADS * SB_HEAD_DIM
    c_idx = lax.axis_index("c")

    (g_in,) = allgather_chips([w_in[0].astype(bf16)], name="gather_w_in")
    Wu = g_in[0]
    Wqkv = jnp.concatenate([_pad_cols(g_in[1]), _pad_cols(g_in[2]), _pad_cols(g_in[3])], axis=1)
    gq_pad, gk_pad = _pad_cols(sb_g_q), _pad_cols(sb_g_k)
    gosb_pad = _pad_cols(g_out_sb)
    a_re, a_im = ssm_a_re.reshape(1, GP), ssm_a_im.reshape(1, GP)
    bT_re = ssm_b_re[0].transpose(2, 0, 1).reshape(C, GP)
    bT_im = ssm_b_im[0].transpose(2, 0, 1).reshape(C, GP)
    cT_re = ssm_c_re[0].transpose(1, 0, 2).reshape(C, GP)
    cT_im = ssm_c_im[0].transpose(1, 0, 2).reshape(C, GP)
    s5_in = (a_re, a_im, ssm_log_dt, bT_re, bT_im, cT_re, cT_im)

    big = dict(tn=1024, tk=1024)
    h0 = rms_norm(xs, g_mix, D, name="norm_x")
    u = mm(h0, Wu, mode="nn", name="proj_u", tk=1024)
    shard = {n: W[n][0].astype(bf16) for n in BIG[1:]}
    qkv, (g_glu, g_out, g_xq, g_xkv, g_xo) = mm(
        h0, Wqkv, mode="nn", name="proj_qkv",
        exchange=([shard[n] for n in ("ssm_w_glu", "w_out", "xa_w_q", "xa_w_kv", "xa_w_o")], False), **big)
    qn, kn, vb = qkv_prep(qkv, gq_pad, gk_pad)
    o = sb_fwd(qn, kn, vb)
    Wglu = g_glu.reshape(-1, g_glu.shape[-1])
    Wout = g_out.reshape(-1, g_out.shape[-1])
    Wo_ssm, Wo_sb = Wout[:SBW], _pad_rows(Wout[SBW:])
    Wxq = g_xq.reshape(-1, g_xq.shape[-1])
    Wxkv = g_xkv.reshape(-1, g_xkv.shape[-1])
    Wxo = g_xo.transpose(1, 0, 2).reshape(g_xo.shape[1], -1)
    abr, abi, Bm, Cm = s5_prep(*s5_in)
    n_pos = S // SCAN_CHUNKS
    u_il = row_shuffle(u, n_pos, SCAN_CHUNKS, name="u_interleave", out_dtype=bf16)
    bu, (g_up,) = mm(u_il, Bm, mode="nn", name="s5_bu", b_shards=2, out_shards=2, tn=1024,
                     exchange=([shard["w_up"]], False))
    st = scan_fwd(bu, abr, abi)
    ypre0_il, (g_down,) = mm(st, Cm, mode="nt", name="s5_y", a_shards=2, b_shards=2, tk=1024,
                             exchange=([shard["w_down"]], False))
    Wup = g_up.transpose(1, 0, 2).reshape(g_up.shape[1], -1)
    Wdown = g_down.reshape(-1, g_down.shape[-1])
    ypre0 = row_shuffle(ypre0_il, SCAN_CHUNKS, n_pos, name="y_token_order")
    ypre, z, zn = glu_fwd(ypre0, u, ssm_d, Wglu, g_out_ssm)
    on = rms_norm(o, gosb_pad, SBW, name="norm_o")
    x1a = mm(zn, Wo_ssm, mode="nn", name="out_ssm", epi="add", aux=xs, tn=1024)
    x1 = mm(on, Wo_sb, mode="nn", name="out_sb", epi="add", aux=x1a, **big)
    h1 = rms_norm(x1, g_xa, D, name="norm_x1")
    qx = mm(h1, Wxq, mode="nn", name="xa_q", tk=1024)
    memn = rms_norm(mems, g_mem, D, name="norm_mem")
    kv = mm(memn, Wxkv, mode="nn", name="xa_kv", **big)
    ox = xa_fwd(qx, kv, xa_g_q, xa_g_k)
    x2 = mm(ox, Wxo, mode="nn", name="xa_o", epi="add", aux=x1, tn=1024)
    h2 = rms_norm(x2, g_mlp, D, name="norm_x2")
    act = mm(h2, Wup, mode="nn", name="mlp_up", out_dtype=bf16, tn=2048, tk=1024)
    x3 = mm(act, Wdown, mode="nn", name="mlp_down", pro="relu2", epi="add", aux=x2, **big)
    dx3, loss_part, dx3b = loss_head(x3, tgt)
    loss = lax.psum(loss_part[0, 0], ("x", "y", "c"))

    dact = mm(dx3b, Wdown, mode="nt", name="d_act", epi="mul2relu", aux=act, out_dtype=bf16, tn=2048, tk=1024)
    dWdown = mm(act, dx3b, mode="tn", name="dw_down", pro="relu2", tm=1024, **big)
    dWup = mm(h2, dact, mode="tn", name="dw_up", out_shards=N_CHIPS, tm=1024, **big)
    dh2 = mm(dact, Wup, mode="nt", name="d_h2", **big)
    dx2, dg_mlp, dx2b = rms_bwd(x2, g_mlp, dh2, dx3, D, name="rms_bwd_mlp", twin=True)
    dox = mm(dx2b, Wxo, mode="nt", name="d_ox", out_dtype=bf16, tk=1024)
    dWxo = mm(ox, dx2b, mode="tn", name="dw_xo", out_shards=N_CHIPS, tk=1024)
    dqx, dkv, dg_xq, dg_xk = xa_bwd(qx, kv, xa_g_q, xa_g_k, dox)
    dWxq = mm(h1, dqx, mode="tn", name="dw_xq", tm=1024, tk=1024)
    dh1 = mm(dqx, Wxq, mode="nt", name="d_h1", tn=1024)
    dx1, dg_xa, dx1b = rms_bwd(x1, g_xa, dh1, dx2, D, name="rms_bwd_xa", twin=True)
    dWxkv = mm(memn, dkv, mode="tn", name="dw_xkv", tm=1024, tn=1024)
    dmemn = mm(dkv, Wxkv, mode="nt", name="d_memn", **big)
    _, dg_mem = rms_bwd(mems, g_mem, dmemn, None, D, name="rms_bwd_mem")
    dyn_ssm = mm(dx1b, Wo_ssm, mode="nt", name="d_yn_ssm", tk=1024)
    dyn_sb = mm(dx1b, Wo_sb, mode="nt", name="d_yn_sb", **big)
    dWo_ssm = mm(zn, dx1b, mode="tn", name="dw_out_ssm", **big)
    dWo_sb = mm(on, dx1b, mode="tn", name="dw_out_sb", tm=1024, **big)
    dz, dg_os = rms_bwd(z, g_out_ssm, dyn_ssm, None, SBW, name="rms_bwd_ssm")
    do, dg_osb = rms_bwd(o, gosb_pad, dyn_sb, None, SBW, name="rms_bwd_sb")
    c_arr = c_idx.astype(jnp.int32).reshape(1)

    def sibling_sums(grads, names, tag):
        g4 = [g.reshape(N_CHIPS, 2, g.shape[1] // 2, g.shape[2]) for g in grads]
        from_sib = sibling_swap(g4, half=True, name="grad_to_sibling_" + tag)
        return [add_half(a, b, c_arr, name="add_sibling_" + n) for a, b, n in zip(g4, from_sib, names)]

    early = ("xa_w_q", "xa_w_kv", "xa_w_o", "w_up", "w_down")
    early_g = [dWxq.reshape(N_CHIPS, -1, dWxq.shape[1]), dWxkv.reshape(N_CHIPS, -1, dWxkv.shape[1]), dWxo, dWup,
               dWdown.reshape(N_CHIPS, -1, D)]
    pair = sibling_sums(early_g, early, "early")
    dqn, dkn, dv = sb_bwd(qn, kn, vb, do)
    dqkv, dg_q, dg_k, parts_xa = qkv_bwd(qkv, gq_pad, gk_pad, dqn, dkn, dv, (pair[:3], True))
    dypre, du_skip, dWglu, dd = glu_bwd(ypre, u, ssm_d, Wglu, dz)
    dypre_il = row_shuffle(dypre, n_pos, SCAN_CHUNKS, name="dy_interleave", out_dtype=bf16)
    dst, parts_up = mm(dypre_il, Cm, mode="nn", name="d_states", b_shards=2, out_shards=2, tn=1024,
                       exchange=(pair[3:4], True))
    dCm = mm(dypre_il, st, mode="tn", name="d_cmat", b_shards=2, out_shards=2, **big)
    gst, dabr, dabi, parts_down = scan_bwd(dst, st, abr, abi, (pair[4:], True))
    mine = {n: sum_chips(p, name="sum_chips_" + n) for n, p in zip(early, [*parts_xa, *parts_up, *parts_down])}
    dBm = mm(u_il, gst, mode="tn", name="d_bmat", b_shards=2, out_shards=2, **big)
    du_il = mm(gst, Bm, mode="nt", name="d_u", a_shards=2, b_shards=2, tk=1024)
    du = row_shuffle(du_il, SCAN_CHUNKS, n_pos, name="du_token_order", add=du_skip, out_dtype=bf16)
    s5_g = s5_prep_bwd(*s5_in, dabr, dabi, dBm, dCm)
    dWu = mm(h0, du, mode="tn", name="dw_u", tm=1024, tk=1024)
    dWqkv = mm(h0, dqkv, mode="tn", name="dw_qkv", tm=1024, **big)
    dh0a = mm(du, Wu, mode="nt", name="d_h0_u", tn=1024)
    dh0 = mm(dqkv, Wqkv, mode="nt", name="d_h0_qkv", epi="add", aux=dh0a, **big)
    dx, dg_mix = rms_bwd(xs, g_mix, dh0, dx1, D, name="rms_bwd_mix")

    HW = SB_HEADS * LANES
    late = ("w_in", "ssm_w_glu", "w_out", "small")
    late_g = [jnp.stack([dWu, _unpad_cols(dWqkv[:, :HW]), _unpad_cols(dWqkv[:, HW:2 * HW]),
                         _unpad_cols(dWqkv[:, 2 * HW:])]),
              dWglu.reshape(N_CHIPS, -1, dWglu.shape[1]),
              jnp.concatenate([dWo_ssm, _unpad_rows(dWo_sb)]).reshape(N_CHIPS, -1, D)]
    da_re, da_im, dldt, dbT_re, dbT_im, dcT_re, dcT_im = s5_g
    small_g = {
        "g_mix": dg_mix, "ssm_a_re": da_re, "ssm_a_im": da_im, "ssm_log_dt": dldt,
        "ssm_b_re": dbT_re.reshape(C, G, P).transpose(1, 2, 0), "ssm_b_im": dbT_im.reshape(C, G, P).transpose(1, 2, 0),
        "ssm_c_re": dcT_re.reshape(C, G, P).transpose(1, 0, 2), "ssm_c_im": dcT_im.reshape(C, G, P).transpose(1, 0, 2),
        "ssm_d": dd, "sb_g_q": dg_q[:, :SB_HEAD_DIM], "sb_g_k": dg_k[:, :SB_HEAD_DIM], "g_out_ssm": dg_os,
        "g_out_sb": _unpad_cols(dg_osb), "g_xa": dg_xa, "g_mem": dg_mem, "xa_g_q": dg_xq, "xa_g_k": dg_xk,
        "g_mlp": dg_mlp,
    }
    late_g.append(_pack([small_g[n] for n in SMALL]).reshape(N_CHIPS, -1, LANES))

    parts_late = chip_scatter(sibling_sums(late_g, late, "late"), name="grad_to_chips_late")
    mine.update({n: sum_chips(p, name="sum_chips_" + n) for n, p in zip(late, parts_late)})
    mine = [mine[n] for n in list(BIG) + ["small"]]
    other = sibling_swap(mine, half=False, name="grad_half_to_sibling")
    shard = [jnp.where(c_idx == 0, jnp.concatenate([a, b]), jnp.concatenate([b, a])) for a, b in zip(mine, other)]
    small_all = allgather_chips([shard[-1]], name="gather_small")[0]
    small_red = small_all.reshape(-1, LANES)

    out = {}
    for n, gs in zip(BIG, shard[:-1]):
        shp = W[n].shape
        w2, m2, v2 = (t.reshape(gs.shape) for t in (W[n], M1[n], V2[n]))
        d, nm, nv = adamw(w2, gs, m2, v2, name="adamw_" + n)
        out[n] = tuple(t.reshape(shp) for t in (gs, d, nm, nv))
    shapes = [W[n].shape for n in SMALL]
    d, nm, nv = adamw(_pack([W[n] for n in SMALL]), small_red, _pack([M1[n] for n in SMALL]),
                      _pack([V2[n] for n in SMALL]), name="adamw_small")
    for n, gs, dd_, mm_, vv_ in zip(SMALL, _unpack(small_red, shapes), _unpack(d, shapes), _unpack(nm, shapes),
                                    _unpack(nv, shapes)):
        out[n] = (gs, dd_, mm_, vv_)
    res = [loss, dx[None]]
    for kind in range(4):
        res += [out[n][kind] for n in WEIGHTS]
    return tuple(res)
```

```python
import jax
import jax.numpy as jnp
from jax import lax
from jax.experimental import pallas as pl
from jax.experimental.pallas import tpu as pltpu

f32 = jnp.float32
bf16 = jnp.bfloat16

NORM_EPS = 1e-6
SSM_GROUPS = 32
SSM_GROUP = 16
SSM_STATE = 64
SB_HEADS = 8
SB_HEAD_DIM = 64
XA_HEADS = 4
XA_HEAD_DIM = 128
LANES = 128
SUBLANES = 8
N_CHIPS = 4
ADAM_LR = 0.001
ADAM_B1 = 0.9
ADAM_B2 = 0.999
ADAM_EPS = 1e-08
ADAM_WD = 0.01
ADAM_STEP = 10
VMEM_LIMIT = 56 * 1024 * 1024
MESH = pl.DeviceIdType.MESH
ANY = pl.BlockSpec(memory_space=pl.ANY)


def _cp(sem=None):
    return pltpu.CompilerParams(dimension_semantics=sem, vmem_limit_bytes=VMEM_LIMIT)


def _tile(n, pref):
    if n <= pref:
        return n
    t = (pref // LANES) * LANES
    while t > LANES and n % t:
        t -= LANES
    assert n % t == 0, (n, pref)
    return t


def _row_tile(n, pref):
    if n <= pref:
        return n
    t = (pref // SUBLANES) * SUBLANES
    while n % t:
        t -= SUBLANES
    return t


def _dot(a, b, dims):
    return lax.dot_general(a.astype(bf16), b.astype(bf16), (dims, ((), ())), preferred_element_type=f32)


_NN = ((1,), (0,))
_NT = ((1,), (1,))
_TN = ((0,), (0,))


@jax.custom_vjp
def bdot_nn(a, b):
    return _dot(a, b, _NN)


def _bdot_nn_fwd(a, b):
    return _dot(a, b, _NN), (a, b)


def _bdot_nn_bwd(res, g):
    a, b = res
    return _dot(g, b, _NT), _dot(a, g, _TN)


bdot_nn.defvjp(_bdot_nn_fwd, _bdot_nn_bwd)


@jax.custom_vjp
def bdot_nt(a, b):
    return _dot(a, b, _NT)


def _bdot_nt_fwd(a, b):
    return _dot(a, b, _NT), (a, b)


def _bdot_nt_bwd(res, g):
    a, b = res
    return _dot(g, b, _NN), _dot(g, a, _TN)


bdot_nt.defvjp(_bdot_nt_fwd, _bdot_nt_bwd)


def _rms(x, g, denom):
    r = lax.rsqrt(jnp.sum(x * x, axis=-1, keepdims=True) * (1.0 / denom) + NORM_EPS)
    return x * r * g


def _opspec(block, row_of, col_of, shards, ncol_tiles):
    if shards == 1:
        return pl.BlockSpec(block, lambda i, j, k: (row_of(i, j, k), col_of(i, j, k)))
    per = ncol_tiles // shards
    return pl.BlockSpec((None,) + block,
                        lambda i, j, k: (col_of(i, j, k) // per, row_of(i, j, k), col_of(i, j, k) % per))


def _call_with_exchange(body, args, exchange, *, name, grid, in_specs, out_specs, out_shape, scratch_shapes=()):
    xs, scatter = exchange
    n, n_in, n_out, n_scr = len(xs), len(in_specs), len(out_specs), len(scratch_shapes)
    x_shapes, sems = _chip_exchange_args(xs, scatter)

    def wrapped(*refs):
        ins, x_ins = refs[:n_in], refs[n_in:n_in + n]
        outs, x_outs = refs[n_in + n:n_in + n + n_out], refs[n_in + n + n_out:n_in + 2 * n + n_out]
        scratch, x_sems = refs[n_in + 2 * n + n_out:n_in + 2 * n + n_out + n_scr], refs[n_in + 2 * n + n_out + n_scr:]
        first, last = True, True
        for d, steps in enumerate(grid):
            first = first & (pl.program_id(d) == 0)
            last = last & (pl.program_id(d) == steps - 1)

        @pl.when(first)
        def _():
            _chip_exchange(x_ins, x_outs, x_sems, scatter, True)

        body(*ins, *outs, *scratch)

        @pl.when(last)
        def _():
            _chip_exchange(x_ins, x_outs, x_sems, scatter, False)

    res = pl.pallas_call(
        wrapped, name=name, grid=grid, in_specs=list(in_specs) + [ANY] * n, out_specs=list(out_specs) + [ANY] * n,
        out_shape=list(out_shape) + x_shapes, scratch_shapes=list(scratch_shapes) + sems,
        compiler_params=_cp(("arbitrary",) * len(grid)),
    )(*args, *xs)
    return res[:n_out], res[n_out:]


def mm(a, b, *, mode, name, tm=512, tn=512, tk=512, pro="none", epi="none", aux=None,
       out_dtype=f32, a_shards=1, b_shards=1, out_shards=1, exchange=None):
    ar, ac = a.shape[-2], a.shape[-1] * a_shards
    br, bc = b.shape[-2], b.shape[-1] * b_shards
    if mode == "nn":
        M, K, N = ar, ac, bc
        assert br == K
    elif mode == "nt":
        M, K, N = ar, ac, br
        assert bc == K
    else:
        M, K, N = ac, ar, bc
        assert br == K
    tm, tn, tk = _tile(M, tm), _tile(N, tn), _tile(K, tk)
    if a_shards > 1:
        if mode == "tn":
            tm = _tile(ac // a_shards, tm)
        else:
            tk = _tile(ac // a_shards, tk)
    if b_shards > 1:
        if mode == "nt":
            tk = _tile(bc // b_shards, tk)
        else:
            tn = _tile(bc // b_shards, tn)
    if out_shards > 1:
        tn = _tile(N // out_shards, tn)
    nm, nn_, nk = M // tm, N // tn, K // tk
    I = lambda i, j, k: i
    J = lambda i, j, k: j
    Kk = lambda i, j, k: k
    if mode == "nn":
        a_spec = _opspec((tm, tk), I, Kk, a_shards, nk)
        b_spec = _opspec((tk, tn), Kk, J, b_shards, nn_)
        dims = _NN
    elif mode == "nt":
        a_spec = _opspec((tm, tk), I, Kk, a_shards, nk)
        b_spec = _opspec((tn, tk), J, Kk, b_shards, nk)
        dims = _NT
    else:
        a_spec = _opspec((tk, tm), Kk, I, a_shards, nm)
        b_spec = _opspec((tk, tn), Kk, J, b_shards, nn_)
        dims = _TN
    in_specs = [a_spec, b_spec]
    args = [a, b]
    if epi != "none":
        in_specs.append(pl.BlockSpec((tm, tn), lambda i, j, k: (i, j)))
        args.append(aux)
    if out_shards == 1:
        out_spec = pl.BlockSpec((tm, tn), lambda i, j, k: (i, j))
        out_shape = jax.ShapeDtypeStruct((M, N), out_dtype)
    else:
        per = nn_ // out_shards
        out_spec = pl.BlockSpec((None, tm, tn), lambda i, j, k: (j // per, i, j % per))
        out_shape = jax.ShapeDtypeStruct((out_shards, M, N // out_shards), out_dtype)

    def body(*refs):
        a_ref, b_ref = refs[0], refs[1]
        pos = 2
        if epi != "none":
            aux_ref = refs[pos]
            pos += 1
        o_ref, acc_ref = refs[pos], refs[pos + 1]
        k = pl.program_id(2)

        @pl.when(k == 0)
        def _():
            acc_ref[...] = jnp.zeros_like(acc_ref)

        av = a_ref[...]
        if pro == "relu2":
            av = jnp.square(jnp.maximum(av.astype(f32), 0.0))
        acc_ref[...] += _dot(av, b_ref[...], dims)

        @pl.when(k == nk - 1)
        def _():
            res = acc_ref[...]
            if epi == "add":
                res = res + aux_ref[...].astype(f32)
            elif epi == "mul2relu":
                res = res * (2.0 * jnp.maximum(aux_ref[...].astype(f32), 0.0))
            o_ref[...] = res.astype(out_dtype)

    acc = [pltpu.VMEM((tm, tn), f32)]
    if exchange is not None:
        (out,), moved = _call_with_exchange(body, args, exchange, name=name, grid=(nm, nn_, nk), in_specs=in_specs,
                                            out_specs=[out_spec], out_shape=[out_shape], scratch_shapes=acc)
        return out, moved
    return pl.pallas_call(
        body, name=name, grid=(nm, nn_, nk), in_specs=in_specs, out_specs=out_spec, out_shape=out_shape,
        scratch_shapes=acc, compiler_params=_cp(("parallel", "parallel", "arbitrary")),
    )(*args)


def rms_norm(x, g, denom, *, name, ts=512):
    S, D = x.shape
    ts = _tile(S, ts)

    def body(x_ref, g_ref, h_ref):
        h_ref[...] = _rms(x_ref[...], g_ref[...], denom).astype(bf16)

    row = pl.BlockSpec((ts, D), lambda i: (i, 0))
    return pl.pallas_call(
        body, name=name, grid=(S // ts,), in_specs=[row, pl.BlockSpec((1, D), lambda i: (0, 0))], out_specs=row,
        out_shape=jax.ShapeDtypeStruct((S, D), bf16), compiler_params=_cp(("parallel",)),
    )(x, g)


def rms_bwd(x, g, dy, res, denom, *, name, ts=256, twin=False):
    S, D = x.shape
    ts = _tile(S, ts)
    has_res = res is not None

    def body(*refs):
        x_ref, g_ref, dy_ref = refs[:3]
        outs = refs[4:] if has_res else refs[3:]
        _, vjp = jax.vjp(lambda xv, gv: _rms(xv, gv, denom), x_ref[...], g_ref[...])
        dx, dg = vjp(dy_ref[...])
        if has_res:
            dx = dx + refs[3][...]
        outs[0][...] = dx
        if twin:
            outs[2][...] = dx.astype(bf16)
        dg_ref = outs[1]

        @pl.when(pl.program_id(0) == 0)
        def _():
            dg_ref[...] = jnp.zeros_like(dg_ref)

        dg_ref[...] += dg

    row = pl.BlockSpec((ts, D), lambda i: (i, 0))
    vec = pl.BlockSpec((1, D), lambda i: (0, 0))
    in_specs = [row, vec, row] + ([row] if has_res else [])
    args = [x, g, dy] + ([res] if has_res else [])
    return pl.pallas_call(
        body, name=name, grid=(S // ts,), in_specs=in_specs, out_specs=[row, vec] + ([row] if twin else []),
        out_shape=[jax.ShapeDtypeStruct((S, D), f32), jax.ShapeDtypeStruct((1, D), f32)]
        + ([jax.ShapeDtypeStruct((S, D), bf16)] if twin else []),
        compiler_params=_cp(("arbitrary",)),
    )(*args)


LOG2E = 1.4426950408889634
LN2 = 0.6931471805599453


def _qk_fn(q, k, gq, gk):
    qs, ks = [], []
    for h in range(SB_HEADS):
        sl = slice(h * LANES, (h + 1) * LANES)
        qs.append(_rms(q[:, sl], gq, SB_HEAD_DIM) * (SB_HEAD_DIM ** -0.5 * LOG2E))
        ks.append(_rms(k[:, sl], gk, SB_HEAD_DIM))
    return jnp.concatenate(qs, axis=1), jnp.concatenate(ks, axis=1)


def qkv_prep(qkv, gq, gk, *, ts=256):
    S = qkv.shape[0]
    W = SB_HEADS * LANES
    ts = _tile(S, ts)

    def body(q_ref, k_ref, v_ref, gq_ref, gk_ref, qn_ref, kn_ref, vb_ref):
        qn, kn = _qk_fn(q_ref[...], k_ref[...], gq_ref[...], gk_ref[...])
        qn_ref[...] = qn.astype(bf16)
        kn_ref[...] = kn.astype(bf16)
        vb_ref[...] = v_ref[...].astype(bf16)

    out = jax.ShapeDtypeStruct((S, W), bf16)
    gspec = pl.BlockSpec((1, LANES), lambda i: (0, 0))
    ospec = pl.BlockSpec((ts, W), lambda i: (i, 0))
    col = lambda c: pl.BlockSpec((ts, W), lambda i: (i, c))
    return pl.pallas_call(
        body, name="qkv_prep", grid=(S // ts,), in_specs=[col(0), col(1), col(2), gspec, gspec],
        out_specs=[ospec, ospec, ospec], out_shape=[out, out, out], compiler_params=_cp(("parallel",)),
    )(qkv, qkv, qkv, gq, gk)


def qkv_bwd(qkv, gq, gk, dqn, dkn, dv, exchange, *, ts=256):
    S = qkv.shape[0]
    W = SB_HEADS * LANES
    ts = _tile(S, ts)

    def body(q_ref, k_ref, gq_ref, gk_ref, dqn_ref, dkn_ref, dv_ref, o_ref, dgq_ref, dgk_ref):
        _, vjp = jax.vjp(_qk_fn, q_ref[...], k_ref[...], gq_ref[...], gk_ref[...])
        dq, dk, dgq, dgk = vjp((dqn_ref[...] * LN2, dkn_ref[...] * LN2))
        o_ref[:, 0:W] = dq.astype(bf16)
        o_ref[:, W:2 * W] = dk.astype(bf16)
        o_ref[:, 2 * W:3 * W] = dv_ref[...].astype(bf16)

        @pl.when(pl.program_id(0) == 0)
        def _():
            dgq_ref[...] = jnp.zeros_like(dgq_ref)
            dgk_ref[...] = jnp.zeros_like(dgk_ref)

        dgq_ref[...] += dgq
        dgk_ref[...] += dgk

    gspec = pl.BlockSpec((1, LANES), lambda i: (0, 0))
    row = pl.BlockSpec((ts, W), lambda i: (i, 0))
    col = lambda c: pl.BlockSpec((ts, W), lambda i: (i, c))
    (dqkv, dgq, dgk), moved = _call_with_exchange(
        body, (qkv, qkv, gq, gk, dqn, dkn, dv), exchange, name="qkv_bwd", grid=(S // ts,),
        in_specs=[col(0), col(1), gspec, gspec, row, row, row],
        out_specs=[pl.BlockSpec((ts, 3 * W), lambda i: (i, 0)), gspec, gspec],
        out_shape=[jax.ShapeDtypeStruct((S, 3 * W), bf16), jax.ShapeDtypeStruct((1, LANES), f32),
                   jax.ShapeDtypeStruct((1, LANES), f32)])
    return dqkv, dgq, dgk, moved


def _sb_weights(q, ks, R, masked, row, col, UU):
    ls = [_dot(q, k, _NT) for k in ks]
    lbs, lm0s, cats = [], [], []
    for l, diag in zip(ls, masked):
        neg_abs = pltpu.bitcast(pltpu.bitcast(l, jnp.uint32) | jnp.uint32(0x80000000), f32)
        lp = jnp.log2(1.0 + jnp.exp2(neg_abs))
        lb = jnp.minimum(l, 0.0) - lp
        lm = lb - l
        if diag:
            lm = jnp.where(col < row, lm, 0.0)
        hi = lm.astype(bf16)
        lo = (lm - hi.astype(f32)).astype(bf16)
        lbs.append(lb)
        lm0s.append(lm[:, 0:1])
        cats.append(jnp.concatenate([hi, lo], axis=1))
    sums = [_dot(c, UU, _NN) for c in cats]
    ws = []
    for lb, lm0, A, diag in zip(lbs, lm0s, sums, masked):
        w = jnp.exp2(lb + (A + R))
        if diag:
            w = jnp.where(col < row, w, 0.0)
        R = R + (A[:, 0:1] + lm0)
        ws.append(w)
    return lbs, ws, R


def _tri2(tk):
    r = lax.broadcasted_iota(jnp.int32, (2 * tk, tk), 0)
    r = jnp.where(r >= tk, r - tk, r)
    c = lax.broadcasted_iota(jnp.int32, (2 * tk, tk), 1)
    return (r > c).astype(bf16)


SB_GROUP = 8


SB_ALL_ZERO_BELOW = -160.0


def _sweep(i, blocks_of, carry, descending, right_sum=None, ran=None):
    G = SB_GROUP
    n = jnp.maximum(i - 1, 0)
    rem, full = n % G, n // G
    asc = lambda js: js if descending else js[::-1]

    def first_group(c):
        one = lambda c: blocks_of([i], c, [True])
        two = lambda c: blocks_of(asc([i, i - 1]), c, asc([True, False]))
        return lax.cond(i >= 1, two, one, c)

    def body(p, c):
        return blocks_of(asc([i - 2 - p * G - u for u in range(G)]), c, [False] * G)

    def left_over(r):
        return lambda c: blocks_of(asc([r - 1 - u for u in range(r)]), c, [False] * r) if r else c

    if descending:
        carry = first_group(carry)
        alive = lambda c: jnp.max(right_sum(c)) > SB_ALL_ZERO_BELOW
        bodies, carry = lax.while_loop(lambda s: (s[0] < full) & alive(s[1]),
                                       lambda s: (s[0] + 1, body(s[0], s[1])), (jnp.int32(0), carry))
        tail = (bodies == full) & alive(carry)
        carry = lax.switch(jnp.where(tail, rem, 0), [left_over(r) for r in range(G)], carry)
        return carry, (bodies, tail)
    bodies, tail = ran
    carry = lax.switch(jnp.where(tail, rem, 0), [left_over(r) for r in range(G)], carry)
    carry = lax.fori_loop(0, bodies, lambda t, c: body(bodies - 1 - t, c), carry)
    return first_group(carry)


def sb_fwd(qn, kn, vb, *, tq=256):
    S, W = qn.shape
    H = W // LANES
    tq = _tile(S, tq)
    tk = tq
    nq = S // tq

    def body(q_ref, k_ref, v_ref, o_ref):
        i = pl.program_id(1)
        q = q_ref[...]
        row = lax.broadcasted_iota(jnp.int32, (tq, tk), 0)
        col = lax.broadcasted_iota(jnp.int32, (tq, tk), 1)
        UU = _tri2(tk)

        def blocks(js, c, masked):
            rows = [pl.ds(pl.multiple_of(j * tk, tk), tk) for j in js]
            _, ws, R = _sb_weights(q, [k_ref[r, :] for r in rows], c[0], masked, row, col, UU)
            acc = c[1]
            for w, r in zip(ws, rows):
                acc = acc + _dot(w, v_ref[r, :], _NN)
            return R, acc

        c, _ = _sweep(i, blocks, (jnp.zeros((tq, 1), f32), jnp.zeros((tq, LANES), f32)), True, lambda c: c[0])
        o_ref[...] = c[1]

    qspec = pl.BlockSpec((tq, LANES), lambda h, i: (i, h))
    kspec = pl.BlockSpec((S, LANES), lambda h, i: (0, h))
    return pl.pallas_call(
        body, name="sb_fwd", grid=(H, nq), in_specs=[qspec, kspec, kspec], out_specs=qspec,
        out_shape=jax.ShapeDtypeStruct((S, W), f32), compiler_params=_cp(("parallel", "arbitrary")),
    )(qn, kn, vb)


def sb_bwd(qn, kn, vb, do, *, tq=256):
    S, W = qn.shape
    H = W // LANES
    tq = _tile(S, tq)
    tk = tq
    nq = S // tq

    def body(q_ref, k_ref, v_ref, do_ref, dq_ref, dk_ref, dv_ref, dz_s, beta_s):
        i = pl.program_id(1)

        @pl.when(i == 0)
        def _():
            dk_ref[...] = jnp.zeros_like(dk_ref)
            dv_ref[...] = jnp.zeros_like(dv_ref)

        q = q_ref[...]
        dob = do_ref[...].astype(bf16)
        row = lax.broadcasted_iota(jnp.int32, (tq, tk), 0)
        col = lax.broadcasted_iota(jnp.int32, (tq, tk), 1)
        UU = _tri2(tk)
        Ue = (row < col).astype(bf16)

        def sweep1(js, R, masked):
            rows = [pl.ds(pl.multiple_of(j * tk, tk), tk) for j in js]
            dws = [_dot(dob, v_ref[r, :], _NT) for r in rows]
            lbs, ws, R = _sb_weights(q, [k_ref[r, :] for r in rows], R, masked, row, col, UU)
            for j, lb, w, dw in zip(js, lbs, ws, dws):
                dz_s[j] = (dw * w).astype(bf16)
                beta_s[j] = jnp.exp2(lb).astype(bf16)
            for r, w in zip(rows, ws):
                dv_ref[r, :] += _dot(w, dob, _TN)
            return R

        _, ran = _sweep(i, sweep1, jnp.zeros((tq, 1), f32), True, lambda R: R)

        def sweep2(js, c, masked):
            rows = [pl.ds(pl.multiple_of(j * tk, tk), tk) for j in js]
            dzbs = [dz_s[j] for j in js]
            sums = [_dot(dzb, Ue, _NN) for dzb in dzbs]
            Lz, dq = c
            dlbs = []
            for j, dzb, Cz, diag in zip(js, dzbs, sums, masked):
                dz = dzb.astype(f32)
                dl = dz - beta_s[j].astype(f32) * (dz + (Cz + Lz))
                if diag:
                    dl = jnp.where(col < row, dl, 0.0)
                Lz = Lz + (Cz[:, tk - 1:tk] + dz[:, tk - 1:tk])
                dlbs.append(dl.astype(bf16))
            for r, dlb in zip(rows, dlbs):
                dq = dq + _dot(dlb, k_ref[r, :], _NN)
            for r, dlb in zip(rows, dlbs):
                dk_ref[r, :] += _dot(dlb, q, _TN)
            return Lz, dq

        c = _sweep(i, sweep2, (jnp.zeros((tq, 1), f32), jnp.zeros((tq, LANES), f32)), False, ran=ran)
        dq_ref[...] = c[1]

    qspec = pl.BlockSpec((tq, LANES), lambda h, i: (i, h))
    kspec = pl.BlockSpec((S, LANES), lambda h, i: (0, h))
    full = jax.ShapeDtypeStruct((S, W), f32)
    return pl.pallas_call(
        body, name="sb_bwd", grid=(H, nq), in_specs=[qspec, kspec, kspec, qspec],
        out_specs=[qspec, kspec, kspec], out_shape=[full, full, full],
        scratch_shapes=[pltpu.VMEM((nq, tq, tk), bf16), pltpu.VMEM((nq, tq, tk), bf16)],
        compiler_params=_cp(("parallel", "arbitrary")),
    )(qn, kn, vb, do)


def _xa_fn(qx, kv, gq, gk):
    XW = XA_HEADS * XA_HEAD_DIM
    outs = []
    for h in range(XA_HEADS):
        sl = slice(h * XA_HEAD_DIM, (h + 1) * XA_HEAD_DIM)
        qn = _rms(qx[:, sl], gq, XA_HEAD_DIM)
        kn = _rms(kv[:, sl], gk, XA_HEAD_DIM)
        v = kv[:, XW + h * XA_HEAD_DIM:XW + (h + 1) * XA_HEAD_DIM]
        s = bdot_nt(qn, kn) * (XA_HEAD_DIM ** -0.5)
        e = jnp.exp(s - lax.stop_gradient(jnp.max(s, axis=-1, keepdims=True)))
        p = e / jnp.sum(e, axis=-1, keepdims=True)
        outs.append(bdot_nn(p, v))
    return jnp.concatenate(outs, axis=1)


def xa_fwd(qx, kv, gq, gk, *, ts=256):
    S, XW = qx.shape
    M = kv.shape[0]
    ts = _tile(S, ts)

    def body(q_ref, kv_ref, gq_ref, gk_ref, o_ref):
        o_ref[...] = _xa_fn(q_ref[...], kv_ref[...], gq_ref[...], gk_ref[...]).astype(bf16)

    row = pl.BlockSpec((ts, XW), lambda i: (i, 0))
    gspec = pl.BlockSpec((1, XA_HEAD_DIM), lambda i: (0, 0))
    return pl.pallas_call(
        body, name="xa_fwd", grid=(S // ts,),
        in_specs=[row, pl.BlockSpec((M, 2 * XW), lambda i: (0, 0)), gspec, gspec], out_specs=row,
        out_shape=jax.ShapeDtypeStruct((S, XW), bf16), compiler_params=_cp(("parallel",)),
    )(qx, kv, gq, gk)


def xa_bwd(qx, kv, gq, gk, do, *, ts=256):
    S, XW = qx.shape
    M = kv.shape[0]
    ts = _tile(S, ts)

    def body(q_ref, kv_ref, gq_ref, gk_ref, do_ref, dq_ref, dkv_ref, dgq_ref, dgk_ref):
        _, vjp = jax.vjp(_xa_fn, q_ref[...], kv_ref[...], gq_ref[...], gk_ref[...])
        dq, dkv, dgq, dgk = vjp(do_ref[...].astype(f32))
        dq_ref[...] = dq.astype(bf16)

        @pl.when(pl.program_id(0) == 0)
        def _():
            dkv_ref[...] = jnp.zeros_like(dkv_ref)
            dgq_ref[...] = jnp.zeros_like(dgq_ref)
            dgk_ref[...] = jnp.zeros_like(dgk_ref)

        dkv_ref[...] += dkv
        dgq_ref[...] += dgq
        dgk_ref[...] += dgk

    row = pl.BlockSpec((ts, XW), lambda i: (i, 0))
    gspec = pl.BlockSpec((1, XA_HEAD_DIM), lambda i: (0, 0))
    kvspec = pl.BlockSpec((M, 2 * XW), lambda i: (0, 0))
    gshape = jax.ShapeDtypeStruct((1, XA_HEAD_DIM), f32)
    return pl.pallas_call(
        body, name="xa_bwd", grid=(S // ts,), in_specs=[row, kvspec, gspec, gspec, row],
        out_specs=[row, kvspec, gspec, gspec],
        out_shape=[jax.ShapeDtypeStruct((S, XW), bf16), jax.ShapeDtypeStruct((M, 2 * XW), f32), gshape, gshape],
        compiler_params=_cp(("arbitrary",)),
    )(qx, kv, gq, gk, do)


def _s5_prep_fn(a_re, a_im, ldt, bT_re, bT_im, cT_re, cT_im):
    G, P, C = SSM_GROUPS, SSM_STATE, SSM_GROUP
    GP, GC = G * P, G * C
    lg_p, lg_c = P.bit_length() - 1, C.bit_length() - 1
    gi = lax.broadcasted_iota(jnp.int32, (G, GP), 0)
    ci = lax.broadcasted_iota(jnp.int32, (G, GP), 1) >> lg_p
    expand_dt = (gi == ci).astype(f32)
    dte = jnp.dot(jnp.exp(ldt), expand_dt, precision=lax.Precision.HIGHEST, preferred_element_type=f32)
    zr, zi = a_re * dte, a_im * dte
    mag = jnp.exp(zr)
    abr, abi = mag * jnp.cos(zi), mag * jnp.sin(zi)
    nr, ni = abr - 1.0, abi
    den = a_re * a_re + a_im * a_im
    cr = (nr * a_re + ni * a_im) / den
    cim = (ni * a_re - nr * a_im) / den
    bbr = cr * bT_re - cim * bT_im
    bbi = cr * bT_im + cim * bT_re
    rowg = lax.broadcasted_iota(jnp.int32, (GC, GP), 0) >> lg_c
    colg = lax.broadcasted_iota(jnp.int32, (GC, GP), 1) >> lg_p
    diag = rowg == colg

    def expand(t):
        return jnp.where(diag, jnp.broadcast_to(t[None], (G, C, GP)).reshape(GC, GP), 0.0)

    return abr, abi, expand(bbr), expand(bbi), expand(cT_re), expand(-cT_im)


def s5_prep(a_re, a_im, ldt, bT_re, bT_im, cT_re, cT_im):
    GP, GC = SSM_GROUPS * SSM_STATE, SSM_GROUPS * SSM_GROUP

    def body(a_re_ref, a_im_ref, ldt_ref, bTr_ref, bTi_ref, cTr_ref, cTi_ref, abr_ref, abi_ref, B_ref, C_ref):
        abr, abi, Br, Bi, Cr, Ci = _s5_prep_fn(a_re_ref[...], a_im_ref[...], ldt_ref[...], bTr_ref[...],
                                               bTi_ref[...], cTr_ref[...], cTi_ref[...])
        abr_ref[...] = abr
        abi_ref[...] = abi
        B_ref[0] = Br
        B_ref[1] = Bi
        C_ref[0] = Cr
        C_ref[1] = Ci

    vec = jax.ShapeDtypeStruct((1, GP), f32)
    mat = jax.ShapeDtypeStruct((2, GC, GP), f32)
    return pl.pallas_call(body, name="s5_prep", out_shape=[vec, vec, mat, mat], compiler_params=_cp())(
        a_re, a_im, ldt, bT_re, bT_im, cT_re, cT_im)


def s5_prep_bwd(a_re, a_im, ldt, bT_re, bT_im, cT_re, cT_im, dabr, dabi, dB, dC):
    def body(a_re_ref, a_im_ref, ldt_ref, bTr_ref, bTi_ref, cTr_ref, cTi_ref, dabr_ref, dabi_ref, dB_ref, dC_ref,
             *outs):
        _, vjp = jax.vjp(_s5_prep_fn, a_re_ref[...], a_im_ref[...], ldt_ref[...], bTr_ref[...], bTi_ref[...],
                         cTr_ref[...], cTi_ref[...])
        grads = vjp((dabr_ref[...], dabi_ref[...], dB_ref[0], dB_ref[1], dC_ref[0], dC_ref[1]))
        for o_ref, gv in zip(outs, grads):
            o_ref[...] = gv

    ins = (a_re, a_im, ldt, bT_re, bT_im, cT_re, cT_im)
    return pl.pallas_call(body, name="s5_prep_bwd", out_shape=[jax.ShapeDtypeStruct(v.shape, f32) for v in ins],
                          compiler_params=_cp())(*ins, dabr, dabi, dB, dC)


def _cmul(ar, ai, br, bi):
    return ar * br - ai * bi, ar * bi + ai * br


SCAN_CHUNKS = 32


def _chunk_carry(Lr, Li, Pr, Pi, scratch, reverse):
    lr_ref, li_ref, cr_ref, ci_ref = scratch
    lr_ref[...] = Lr
    li_ref[...] = Li
    cur_r = jnp.zeros((1, LANES), f32)
    cur_i = jnp.zeros((1, LANES), f32)
    order = range(SCAN_CHUNKS - 1, -1, -1) if reverse else range(SCAN_CHUNKS)
    for c in order:
        cr_ref[pl.ds(c, 1), :] = cur_r
        ci_ref[pl.ds(c, 1), :] = cur_i
        mr, mi = _cmul(Pr, Pi, cur_r, cur_i)
        cur_r, cur_i = lr_ref[pl.ds(c, 1), :] + mr, li_ref[pl.ds(c, 1), :] + mi
    return cr_ref[...], ci_ref[...]


def _chunk_rows(j):
    return pl.ds(pl.multiple_of(j * SCAN_CHUNKS, SCAN_CHUNKS), SCAN_CHUNKS)


def row_shuffle(x, a, b, *, name, add=None, out_dtype=f32):
    S, W = x.shape
    assert a * b == S and x.dtype == f32

    def body(*refs):
        x_ref, o_ref = refs[0], refs[-1]

        def step(i, _):
            dst = pl.ds(pl.multiple_of(i * b, b), b)
            v = x_ref[pl.ds(i, b, stride=a), :]
            if add is not None:
                v = v + refs[1][dst, :]
            o_ref[dst, :] = v.astype(out_dtype)
            return 0

        lax.fori_loop(0, a, step, 0)

    col = pl.BlockSpec((S, LANES), lambda t: (0, t))
    args = [x] + ([add] if add is not None else [])
    return pl.pallas_call(
        body, name=name, grid=(W // LANES,), in_specs=[col] * len(args), out_specs=col,
        out_shape=jax.ShapeDtypeStruct((S, W), out_dtype), compiler_params=_cp(("parallel",)),
    )(*args)


def _scan_scratch(n):
    small = pltpu.VMEM((SCAN_CHUNKS, LANES), f32)
    return [pltpu.VMEM((n, LANES), f32), pltpu.VMEM((n, LANES), f32), small, small, small, small]


def _drive(x_ref, m_ref, work_ref):
    S = x_ref.shape[0]
    rows = min(S, 1024)
    m = jnp.concatenate([m_ref[0], m_ref[1]], axis=1)

    def chunk(c, _):
        r = pl.ds(pl.multiple_of(c * rows, rows), rows)
        y = _dot(x_ref[r, :], m, _NN)
        work_ref[0, r, :] = y[:, :LANES]
        work_ref[1, r, :] = y[:, LANES:]
        return 0

    lax.fori_loop(0, S // rows, chunk, 0)


def scan_fwd(u_il, Bm, abr, abi, exchange):
    S, C = u_il.shape
    N = Bm.shape[2]
    n = S // SCAN_CHUNKS
    shp = (SCAN_CHUNKS, LANES)

    def body(u_ref, B_ref, ar_ref, ai_ref, st_ref, work_ref, pwr_ref, pwi_ref, *scratch):
        a1r, a1i = ar_ref[...], ai_ref[...]
        ar = jnp.broadcast_to(a1r, shp)
        ai = jnp.broadcast_to(a1i, shp)
        _drive(u_ref, B_ref, work_ref)
        sr_ref, si_ref = work_ref.at[0], work_ref.at[1]

        def step(j, c):
            sr, si, pr, pi = c
            rows = _chunk_rows(j)
            mr, mi = _cmul(ar, ai, sr, si)
            sr, si = mr + sr_ref[rows, :], mi + si_ref[rows, :]
            sr_ref[rows, :] = sr
            si_ref[rows, :] = si
            pwr_ref[pl.ds(j, 1), :] = pr
            pwi_ref[pl.ds(j, 1), :] = pi
            npr, npi = _cmul(a1r, a1i, pr, pi)
            return sr, si, npr, npi

        z = jnp.zeros(shp, f32)
        sr, si, _, _ = lax.fori_loop(0, n, step, (z, z, a1r, a1i), unroll=2)
        cr, ci = _chunk_carry(sr, si, pwr_ref[pl.ds(n - 1, 1), :], pwi_ref[pl.ds(n - 1, 1), :], scratch, False)

        def step2(j, _):
            rows = _chunk_rows(j)
            pr = jnp.broadcast_to(pwr_ref[pl.ds(j, 1), :], shp)
            pi = jnp.broadcast_to(pwi_ref[pl.ds(j, 1), :], shp)
            mr, mi = _cmul(pr, pi, cr, ci)
            st_ref[0, rows, :] = (sr_ref[rows, :] + mr).astype(bf16)
            st_ref[1, rows, :] = (si_ref[rows, :] + mi).astype(bf16)
            return 0

        lax.fori_loop(0, n, step2, 0, unroll=4)

    blk = pl.BlockSpec((2, S, LANES), lambda t: (0, 0, t))
    vec = pl.BlockSpec((1, LANES), lambda t: (0, t))
    (st,), moved = _call_with_exchange(
        body, (u_il, Bm, abr, abi), exchange, name="scan_fwd", grid=(N // LANES,),
        in_specs=[pl.BlockSpec((S, C), lambda t: (0, 0)), pl.BlockSpec((2, C, LANES), lambda t: (0, 0, t)), vec, vec],
        out_specs=[blk], out_shape=[jax.ShapeDtypeStruct((2, S, N), bf16)],
        scratch_shapes=[pltpu.VMEM((2, S, LANES), f32)] + _scan_scratch(n))
    return st, moved


def scan_bwd(dy_il, Cm, st, abr, abi, exchange):
    _, S, N = st.shape
    C = dy_il.shape[1]
    n = S // SCAN_CHUNKS
    shp = (SCAN_CHUNKS, LANES)

    def body(dy_ref, C_ref, st_ref, ar_ref, ai_ref, g_ref, dar_ref, dai_ref, work_ref, qwr_ref, qwi_ref, *scratch):
        a1r, a1i = ar_ref[...], -ai_ref[...]
        ar = jnp.broadcast_to(a1r, shp)
        nai = jnp.broadcast_to(a1i, shp)
        _drive(dy_ref, C_ref, work_ref)
        gr_ref, gi_ref = work_ref.at[0], work_ref.at[1]
        sr_ref, si_ref = st_ref.at[0], st_ref.at[1]

        def step(jj, c):
            gr, gi, qr, qi = c
            j = n - 1 - jj
            rows = _chunk_rows(j)
            mr, mi = _cmul(ar, nai, gr, gi)
            gr, gi = mr + gr_ref[rows, :], mi + gi_ref[rows, :]
            gr_ref[rows, :] = gr
            gi_ref[rows, :] = gi
            qwr_ref[pl.ds(j, 1), :] = qr
            qwi_ref[pl.ds(j, 1), :] = qi
            nqr, nqi = _cmul(a1r, a1i, qr, qi)
            return gr, gi, nqr, nqi

        z = jnp.zeros(shp, f32)
        gr, gi, _, _ = lax.fori_loop(0, n, step, (z, z, a1r, a1i), unroll=2)
        cr, ci = _chunk_carry(gr, gi, qwr_ref[pl.ds(0, 1), :], qwi_ref[pl.ds(0, 1), :], scratch, True)
        sub = lax.broadcasted_iota(jnp.int32, shp, 0)

        def fix(j, spr, spi, acc):
            rows = _chunk_rows(j)
            qr = jnp.broadcast_to(qwr_ref[pl.ds(j, 1), :], shp)
            qi = jnp.broadcast_to(qwi_ref[pl.ds(j, 1), :], shp)
            mr, mi = _cmul(qr, qi, cr, ci)
            gr = gr_ref[rows, :] + mr
            gi = gi_ref[rows, :] + mi
            g_ref[0, rows, :] = gr.astype(bf16)
            g_ref[1, rows, :] = gi.astype(bf16)
            return acc[0] + gr * spr + gi * spi, acc[1] + gi * spr - gr * spi

        last = _chunk_rows(n - 1)
        spr = jnp.where(sub == 0, 0.0, pltpu.roll(sr_ref[last, :].astype(f32), 1, 0))
        spi = jnp.where(sub == 0, 0.0, pltpu.roll(si_ref[last, :].astype(f32), 1, 0))
        acc = fix(0, spr, spi, (z, z))

        def step2(j, acc):
            prev = _chunk_rows(j - 1)
            return fix(j, sr_ref[prev, :].astype(f32), si_ref[prev, :].astype(f32), acc)

        acc = lax.fori_loop(1, n, step2, acc)
        dar_ref[...] = jnp.sum(acc[0], axis=0, keepdims=True)
        dai_ref[...] = jnp.sum(acc[1], axis=0, keepdims=True)

    blk = pl.BlockSpec((2, S, LANES), lambda t: (0, 0, t))
    vec = pl.BlockSpec((1, LANES), lambda t: (0, t))
    vshape = jax.ShapeDtypeStruct((1, N), f32)
    (g, dar, dai), moved = _call_with_exchange(
        body, (dy_il, Cm, st, abr, abi), exchange, name="scan_bwd", grid=(N // LANES,),
        in_specs=[pl.BlockSpec((S, C), lambda t: (0, 0)), pl.BlockSpec((2, C, LANES), lambda t: (0, 0, t)), blk,
                  vec, vec],
        out_specs=[blk, vec, vec], out_shape=[jax.ShapeDtypeStruct((2, S, N), bf16), vshape, vshape],
        scratch_shapes=[pltpu.VMEM((2, S, LANES), f32)] + _scan_scratch(n))
    return g, dar, dai, moved


def _glu_fn(ypre, wglu):
    y = jax.nn.gelu(ypre)
    return y * jax.nn.sigmoid(bdot_nn(y, wglu))


def glu_fwd(ypre0, u, d, wglu, g_out, *, ts=512):
    S, W = u.shape
    ts = _tile(S, ts)

    def body(y0_ref, u_ref, d_ref, w_ref, g_ref, ypre_ref, z_ref, zn_ref):
        ypre = y0_ref[...] + d_ref[...] * u_ref[...]
        z = _glu_fn(ypre, w_ref[...])
        ypre_ref[...] = ypre
        z_ref[...] = z
        zn_ref[...] = _rms(z, g_ref[...], W).astype(bf16)

    row = pl.BlockSpec((ts, W), lambda i: (i, 0))
    vec = pl.BlockSpec((1, W), lambda i: (0, 0))
    full = jax.ShapeDtypeStruct((S, W), f32)
    return pl.pallas_call(
        body, name="glu_fwd", grid=(S // ts,),
        in_specs=[row, row, vec, pl.BlockSpec((W, W), lambda i: (0, 0)), vec], out_specs=[row, row, row],
        out_shape=[full, full, jax.ShapeDtypeStruct((S, W), bf16)], compiler_params=_cp(("parallel",)),
    )(ypre0, u, d, wglu, g_out)


def glu_bwd(ypre, u, d, wglu, dz, *, ts=512):
    S, W = u.shape
    ts = _tile(S, ts)

    def body(y_ref, u_ref, d_ref, w_ref, dz_ref, dy_ref, du_ref, dw_ref, dd_ref):
        _, vjp = jax.vjp(_glu_fn, y_ref[...], w_ref[...])
        dy, dw = vjp(dz_ref[...])
        dy_ref[...] = dy
        du_ref[...] = d_ref[...] * dy

        @pl.when(pl.program_id(0) == 0)
        def _():
            dw_ref[...] = jnp.zeros_like(dw_ref)
            dd_ref[...] = jnp.zeros_like(dd_ref)

        dw_ref[...] += dw
        dd_ref[...] += jnp.sum(dy * u_ref[...], axis=0, keepdims=True)

    row = pl.BlockSpec((ts, W), lambda i: (i, 0))
    vec = pl.BlockSpec((1, W), lambda i: (0, 0))
    sq = pl.BlockSpec((W, W), lambda i: (0, 0))
    full = jax.ShapeDtypeStruct((S, W), f32)
    return pl.pallas_call(
        body, name="glu_bwd", grid=(S // ts,), in_specs=[row, row, vec, sq, row], out_specs=[row, row, sq, vec],
        out_shape=[full, full, jax.ShapeDtypeStruct((W, W), f32), jax.ShapeDtypeStruct((1, W), f32)],
        compiler_params=_cp(("arbitrary",)),
    )(ypre, u, d, wglu, dz)


def loss_head(y, target, *, ts=512):
    S, D = y.shape
    ts = _tile(S, ts)

    def body(y_ref, t_ref, dy_ref, l_ref, dyb_ref):
        err = y_ref[...] - t_ref[...]
        dy_ref[...] = err * (1.0 / D)
        dyb_ref[...] = (err * (1.0 / D)).astype(bf16)

        @pl.when(pl.program_id(0) == 0)
        def _():
            l_ref[...] = jnp.zeros_like(l_ref)

        rows = jnp.sum(err * err, axis=1, keepdims=True) * (1.0 / D)
        l_ref[...] += 0.5 * jnp.sum(rows, axis=0, keepdims=True)

    row = pl.BlockSpec((ts, D), lambda i: (i, 0))
    return pl.pallas_call(
        body, name="loss_head", grid=(S // ts,), in_specs=[row, row],
        out_specs=[row, pl.BlockSpec((1, 1), lambda i: (0, 0)), row],
        out_shape=[jax.ShapeDtypeStruct((S, D), f32), jax.ShapeDtypeStruct((1, 1), f32),
                   jax.ShapeDtypeStruct((S, D), bf16)],
        compiler_params=_cp(("arbitrary",)),
    )(y, target)


def adamw(w, g, m, v, *, name, tr=256):
    R, C = w.shape
    tr = _row_tile(R, tr)

    def body(w_ref, g_ref, m_ref, v_ref, d_ref, nm_ref, nv_ref):
        gv = g_ref[...]
        nm = ADAM_B1 * m_ref[...] + (1.0 - ADAM_B1) * gv
        nv = ADAM_B2 * v_ref[...] + (1.0 - ADAM_B2) * jnp.square(gv)
        m_hat = nm / (1.0 - ADAM_B1 ** ADAM_STEP)
        v_hat = nv / (1.0 - ADAM_B2 ** ADAM_STEP)
        d_ref[...] = -ADAM_LR * (m_hat / (jnp.sqrt(v_hat) + ADAM_EPS) + ADAM_WD * w_ref[...])
        nm_ref[...] = nm
        nv_ref[...] = nv

    row = pl.BlockSpec((tr, C), lambda i: (i, 0))
    full = jax.ShapeDtypeStruct((R, C), f32)
    return pl.pallas_call(
        body, name=name, grid=(R // tr,), in_specs=[row] * 4, out_specs=[row] * 3, out_shape=[full] * 3,
        compiler_params=_cp(("parallel",)),
    )(w, g, m, v)


def add_half(g4, recv, c, *, name, tr=256):
    _, _, Rh, C = g4.shape
    tr = _row_tile(Rh, tr)

    def body(c_ref, a_ref, b_ref, o_ref):
        o_ref[...] = a_ref[...] + b_ref[...]

    grid_spec = pltpu.PrefetchScalarGridSpec(
        num_scalar_prefetch=1, grid=(N_CHIPS, Rh // tr),
        in_specs=[pl.BlockSpec((None, None, tr, C), lambda k, i, c_ref: (k, c_ref[0], i, 0)),
                  pl.BlockSpec((None, tr, C), lambda k, i, c_ref: (k, i, 0))],
        out_specs=pl.BlockSpec((None, tr, C), lambda k, i, c_ref: (k, i, 0)))
    return pl.pallas_call(body, name=name, grid_spec=grid_spec, out_shape=jax.ShapeDtypeStruct(recv.shape, f32),
                          compiler_params=_cp(("parallel", "parallel")))(c, g4, recv)


def sum_chips(p4, *, name, tr=256):
    _, Rh, C = p4.shape
    tr = _row_tile(Rh, tr)

    def body(a_ref, b_ref, c_ref, d_ref, o_ref):
        o_ref[...] = ((a_ref[...] + b_ref[...]) + c_ref[...]) + d_ref[...]

    spec = lambda k: pl.BlockSpec((None, tr, C), lambda i: (k, i, 0))
    return pl.pallas_call(
        body, name=name, grid=(Rh // tr,), in_specs=[spec(0), spec(1), spec(2), spec(3)],
        out_specs=pl.BlockSpec((tr, C), lambda i: (i, 0)), out_shape=jax.ShapeDtypeStruct((Rh, C), f32),
        compiler_params=_cp(("parallel",)),
    )(p4, p4, p4, p4)


def _place():
    return lax.axis_index("x"), lax.axis_index("y"), lax.axis_index("c")


def _other_chips(x, y):
    return [(1 - x, y), (x, 1 - y), (1 - x, 1 - y)]


def _chip_exchange(ins, outs, sems, scatter, start):
    if not ins:
        return
    send, recv, loc = sems
    x, y, c = _place()
    me = 2 * x + y
    for a in range(len(ins)):
        own = pltpu.make_async_copy(ins[a].at[me] if scatter else ins[a], outs[a].at[me], loc.at[a])
        own.start() if start else own.wait()
        for p, (px, py) in enumerate(_other_chips(x, y)):
            k = 2 * px + py
            cp = pltpu.make_async_remote_copy(
                src_ref=ins[a].at[k] if scatter else ins[a], dst_ref=outs[a].at[me if start else k],
                send_sem=send.at[3 * a + p], recv_sem=recv.at[3 * a + p], device_id=(px, py, c), device_id_type=MESH)
            cp.start() if start else cp.wait()


def _chip_exchange_args(arrs, scatter):
    n = len(arrs)
    shapes = [jax.ShapeDtypeStruct(a.shape if scatter else (N_CHIPS,) + a.shape, a.dtype) for a in arrs]
    sems = [pltpu.SemaphoreType.DMA((3 * n,)), pltpu.SemaphoreType.DMA((3 * n,)), pltpu.SemaphoreType.DMA((n,))]
    return shapes, sems if n else []


def _chip_exchange_call(arrs, scatter, name):
    n = len(arrs)

    def body(*refs):
        ins, outs, sems = refs[:n], refs[n:2 * n], refs[2 * n:]
        _chip_exchange(ins, outs, sems, scatter, True)
        _chip_exchange(ins, outs, sems, scatter, False)

    shapes, sems = _chip_exchange_args(arrs, scatter)
    return pl.pallas_call(
        body, name=name, in_specs=[ANY] * n, out_specs=[ANY] * n, out_shape=shapes, scratch_shapes=sems,
        compiler_params=pltpu.CompilerParams(has_side_effects=True),
    )(*arrs)


def allgather_chips(arrs, *, name):
    return _chip_exchange_call(arrs, False, name)


def sibling_swap(arrs, *, half, name):
    n = len(arrs)

    def body(*refs):
        ins, outs = refs[:n], refs[n:2 * n]
        send, recv = refs[2 * n:]
        x, y, c = _place()
        cps = []
        for a in range(n):
            src = ins[a].at[:, 1 - c] if half else ins[a]
            cp = pltpu.make_async_remote_copy(src_ref=src, dst_ref=outs[a], send_sem=send.at[a], recv_sem=recv.at[a],
                                              device_id=(x, y, 1 - c), device_id_type=MESH)
            cp.start()
            cps.append(cp)
        for cp in cps:
            cp.wait()

    def oshape(a):
        return jax.ShapeDtypeStruct((a.shape[0],) + a.shape[2:] if half else a.shape, a.dtype)

    return pl.pallas_call(
        body, name=name, in_specs=[ANY] * n, out_specs=[ANY] * n, out_shape=[oshape(a) for a in arrs],
        scratch_shapes=[pltpu.SemaphoreType.DMA((n,)), pltpu.SemaphoreType.DMA((n,))],
        compiler_params=pltpu.CompilerParams(has_side_effects=True),
    )(*arrs)


def chip_scatter(arrs, *, name):
    return _chip_exchange_call(arrs, True, name)


def _pad_cols(w):
    K = w.shape[0]
    w = w.reshape(K, -1, SB_HEAD_DIM)
    return jnp.pad(w, ((0, 0), (0, 0), (0, LANES - SB_HEAD_DIM))).reshape(K, -1)


def _unpad_cols(w):
    K = w.shape[0]
    return w.reshape(K, -1, LANES)[:, :, :SB_HEAD_DIM].reshape(K, -1)


def _pad_rows(w):
    N = w.shape[1]
    w = w.reshape(-1, SB_HEAD_DIM, N)
    return jnp.pad(w, ((0, 0), (0, LANES - SB_HEAD_DIM), (0, 0))).reshape(-1, N)


def _unpad_rows(w):
    N = w.shape[1]
    return w.reshape(-1, LANES, N)[:, :SB_HEAD_DIM, :].reshape(-1, N)


_PACK_ROWS = N_CHIPS * 2 * SUBLANES


def _pack(arrs):
    flat = jnp.concatenate([a.reshape(-1) for a in arrs])
    rows = -(-flat.shape[0] // LANES)
    rows = -(-rows // _PACK_ROWS) * _PACK_ROWS
    return jnp.pad(flat, (0, rows * LANES - flat.shape[0])).reshape(rows, LANES)


def _unpack(buf, shapes):
    flat = buf.reshape(-1)
    out, pos = [], 0
    for shp in shapes:
        size = 1
        for d in shp:
            size *= d
        out.append(flat[pos:pos + size].reshape(shp))
        pos += size
    return out


BIG = ("w_in", "ssm_w_glu", "w_out", "xa_w_q", "xa_w_kv", "xa_w_o", "w_up", "w_down")
SMALL = ("g_mix", "ssm_a_re", "ssm_a_im", "ssm_log_dt", "ssm_b_re", "ssm_b_im", "ssm_c_re", "ssm_c_im", "ssm_d",
         "sb_g_q", "sb_g_k", "g_out_ssm", "g_out_sb", "g_xa", "g_mem", "xa_g_q", "xa_g_k", "g_mlp")
WEIGHTS = ("g_mix", "w_in", "ssm_a_re", "ssm_a_im", "ssm_log_dt", "ssm_b_re", "ssm_b_im", "ssm_c_re", "ssm_c_im",
           "ssm_d", "ssm_w_glu", "sb_g_q", "sb_g_k", "g_out_ssm", "g_out_sb", "w_out", "g_xa", "g_mem", "xa_w_q",
           "xa_w_kv", "xa_g_q", "xa_g_k", "xa_w_o", "g_mlp", "w_up", "w_down")


def kernel(x, mem, g_mix, w_in, ssm_a_re, ssm_a_im, ssm_log_dt, ssm_b_re, ssm_b_im, ssm_c_re, ssm_c_im, ssm_d, ssm_w_glu, sb_g_q, sb_g_k, g_out_ssm, g_out_sb, w_out, g_xa, g_mem, xa_w_q, xa_w_kv, xa_g_q, xa_g_k, xa_w_o, g_mlp, w_up, w_down, loss_target, m_g_mix, m_w_in, m_ssm_a_re, m_ssm_a_im, m_ssm_log_dt, m_ssm_b_re, m_ssm_b_im, m_ssm_c_re, m_ssm_c_im, m_ssm_d, m_ssm_w_glu, m_sb_g_q, m_sb_g_k, m_g_out_ssm, m_g_out_sb, m_w_out, m_g_xa, m_g_mem, m_xa_w_q, m_xa_w_kv, m_xa_g_q, m_xa_g_k, m_xa_w_o, m_g_mlp, m_w_up, m_w_down, v_g_mix, v_w_in, v_ssm_a_re, v_ssm_a_im, v_ssm_log_dt, v_ssm_b_re, v_ssm_b_im, v_ssm_c_re, v_ssm_c_im, v_ssm_d, v_ssm_w_glu, v_sb_g_q, v_sb_g_k, v_g_out_ssm, v_g_out_sb, v_w_out, v_g_xa, v_g_mem, v_xa_w_q, v_xa_w_kv, v_xa_g_q, v_xa_g_k, v_xa_w_o, v_g_mlp, v_w_up, v_w_down):
    env = dict(locals())
    W = {n: env[n] for n in WEIGHTS}
    M1 = {n: env["m_" + n] for n in WEIGHTS}
    V2 = {n: env["v_" + n] for n in WEIGHTS}
    xs, mems, tgt = x[0], mem[0], loss_target[0]
    S, D = xs.shape
    G, P, C = SSM_GROUPS, SSM_STATE, SSM_GROUP
    GP = G * P
    SBW = SB_HEADS * SB_HEAD_DIM
    c_idx = lax.axis_index("c")

    (g_in,) = allgather_chips([w_in[0].astype(bf16)], name="gather_w_in")
    Wu = g_in[0]
    Wqkv = jnp.concatenate([_pad_cols(g_in[1]), _pad_cols(g_in[2]), _pad_cols(g_in[3])], axis=1)
    gq_pad, gk_pad = _pad_cols(sb_g_q), _pad_cols(sb_g_k)
    gosb_pad = _pad_cols(g_out_sb)
    a_re, a_im = ssm_a_re.reshape(1, GP), ssm_a_im.reshape(1, GP)
    bT_re = ssm_b_re[0].transpose(2, 0, 1).reshape(C, GP)
    bT_im = ssm_b_im[0].transpose(2, 0, 1).reshape(C, GP)
    cT_re = ssm_c_re[0].transpose(1, 0, 2).reshape(C, GP)
    cT_im = ssm_c_im[0].transpose(1, 0, 2).reshape(C, GP)
    s5_in = (a_re, a_im, ssm_log_dt, bT_re, bT_im, cT_re, cT_im)

    big = dict(tn=1024, tk=1024)
    h0 = rms_norm(xs, g_mix, D, name="norm_x")
    u = mm(h0, Wu, mode="nn", name="proj_u", tk=1024)
    shard = {n: W[n][0].astype(bf16) for n in BIG[1:]}
    qkv, (g_glu, g_out, g_xq, g_xkv, g_xo) = mm(
        h0, Wqkv, mode="nn", name="proj_qkv",
        exchange=([shard[n] for n in ("ssm_w_glu", "w_out", "xa_w_q", "xa_w_kv", "xa_w_o")], False), **big)
    qn, kn, vb = qkv_prep(qkv, gq_pad, gk_pad)
    o = sb_fwd(qn, kn, vb)
    Wglu = g_glu.reshape(-1, g_glu.shape[-1])
    Wout = g_out.reshape(-1, g_out.shape[-1])
    Wo_ssm, Wo_sb = Wout[:SBW], _pad_rows(Wout[SBW:])
    Wxq = g_xq.reshape(-1, g_xq.shape[-1])
    Wxkv = g_xkv.reshape(-1, g_xkv.shape[-1])
    Wxo = g_xo.transpose(1, 0, 2).reshape(g_xo.shape[1], -1)
    abr, abi, Bm, Cm = s5_prep(*s5_in)
    n_pos = S // SCAN_CHUNKS
    u_il = row_shuffle(u, n_pos, SCAN_CHUNKS, name="u_interleave", out_dtype=bf16)
    st, (g_up,) = scan_fwd(u_il, Bm, abr, abi, ([shard["w_up"]], False))
    ypre0_il, (g_down,) = mm(st, Cm, mode="nt", name="s5_y", a_shards=2, b_shards=2, tk=1024,
                             exchange=([shard["w_down"]], False))
    Wup = g_up.transpose(1, 0, 2).reshape(g_up.shape[1], -1)
    Wdown = g_down.reshape(-1, g_down.shape[-1])
    ypre0 = row_shuffle(ypre0_il, SCAN_CHUNKS, n_pos, name="y_token_order")
    ypre, z, zn = glu_fwd(ypre0, u, ssm_d, Wglu, g_out_ssm)
    on = rms_norm(o, gosb_pad, SBW, name="norm_o")
    x1a = mm(zn, Wo_ssm, mode="nn", name="out_ssm", epi="add", aux=xs, tn=1024)
    x1 = mm(on, Wo_sb, mode="nn", name="out_sb", epi="add", aux=x1a, **big)
    h1 = rms_norm(x1, g_xa, D, name="norm_x1")
    qx = mm(h1, Wxq, mode="nn", name="xa_q", tk=1024)
    memn = rms_norm(mems, g_mem, D, name="norm_mem")
    kv = mm(memn, Wxkv, mode="nn", name="xa_kv", **big)
    ox = xa_fwd(qx, kv, xa_g_q, xa_g_k)
    x2 = mm(ox, Wxo, mode="nn", name="xa_o", epi="add", aux=x1, tn=1024)
    h2 = rms_norm(x2, g_mlp, D, name="norm_x2")
    act = mm(h2, Wup, mode="nn", name="mlp_up", out_dtype=bf16, tn=2048, tk=1024)
    x3 = mm(act, Wdown, mode="nn", name="mlp_down", pro="relu2", epi="add", aux=x2, **big)
    dx3, loss_part, dx3b = loss_head(x3, tgt)
    loss = lax.psum(loss_part[0, 0], ("x", "y", "c"))

    dact = mm(dx3b, Wdown, mode="nt", name="d_act", epi="mul2relu", aux=act, out_dtype=bf16, tn=2048, tk=1024)
    dWdown = mm(act, dx3b, mode="tn", name="dw_down", pro="relu2", tm=1024, **big)
    dWup = mm(h2, dact, mode="tn", name="dw_up", out_shards=N_CHIPS, tm=1024, **big)
    dh2 = mm(dact, Wup, mode="nt", name="d_h2", **big)
    dx2, dg_mlp, dx2b = rms_bwd(x2, g_mlp, dh2, dx3, D, name="rms_bwd_mlp", twin=True)
    dox = mm(dx2b, Wxo, mode="nt", name="d_ox", out_dtype=bf16, tk=1024)
    dWxo = mm(ox, dx2b, mode="tn", name="dw_xo", out_shards=N_CHIPS, tk=1024)
    dqx, dkv, dg_xq, dg_xk = xa_bwd(qx, kv, xa_g_q, xa_g_k, dox)
    dWxq = mm(h1, dqx, mode="tn", name="dw_xq", tm=1024, tk=1024)
    dh1 = mm(dqx, Wxq, mode="nt", name="d_h1", tn=1024)
    dx1, dg_xa, dx1b = rms_bwd(x1, g_xa, dh1, dx2, D, name="rms_bwd_xa", twin=True)
    dWxkv = mm(memn, dkv, mode="tn", name="dw_xkv", tm=1024, tn=1024)
    dmemn = mm(dkv, Wxkv, mode="nt", name="d_memn", **big)
    _, dg_mem = rms_bwd(mems, g_mem, dmemn, None, D, name="rms_bwd_mem")
    dyn_ssm = mm(dx1b, Wo_ssm, mode="nt", name="d_yn_ssm", tk=1024)
    dyn_sb = mm(dx1b, Wo_sb, mode="nt", name="d_yn_sb", **big)
    dWo_ssm = mm(zn, dx1b, mode="tn", name="dw_out_ssm", **big)
    dWo_sb = mm(on, dx1b, mode="tn", name="dw_out_sb", tm=1024, **big)
    dz, dg_os = rms_bwd(z, g_out_ssm, dyn_ssm, None, SBW, name="rms_bwd_ssm")
    do, dg_osb = rms_bwd(o, gosb_pad, dyn_sb, None, SBW, name="rms_bwd_sb")
    c_arr = c_idx.astype(jnp.int32).reshape(1)

    def sibling_sums(grads, names, tag):
        g4 = [g.reshape(N_CHIPS, 2, g.shape[1] // 2, g.shape[2]) for g in grads]
        from_sib = sibling_swap(g4, half=True, name="grad_to_sibling_" + tag)
        return [add_half(a, b, c_arr, name="add_sibling_" + n) for a, b, n in zip(g4, from_sib, names)]

    early = ("xa_w_q", "xa_w_kv", "xa_w_o", "w_up", "w_down")
    early_g = [dWxq.reshape(N_CHIPS, -1, dWxq.shape[1]), dWxkv.reshape(N_CHIPS, -1, dWxkv.shape[1]), dWxo, dWup,
               dWdown.reshape(N_CHIPS, -1, D)]
    pair = sibling_sums(early_g, early, "early")
    dqn, dkn, dv = sb_bwd(qn, kn, vb, do)
    dqkv, dg_q, dg_k, parts_xa = qkv_bwd(qkv, gq_pad, gk_pad, dqn, dkn, dv, (pair[:3], True))
    dypre, du_skip, dWglu, dd = glu_bwd(ypre, u, ssm_d, Wglu, dz)
    dypre_il = row_shuffle(dypre, n_pos, SCAN_CHUNKS, name="dy_interleave", out_dtype=bf16)
    dCm = mm(dypre_il, st, mode="tn", name="d_cmat", b_shards=2, out_shards=2, **big)
    gst, dabr, dabi, parts_down = scan_bwd(dypre_il, Cm, st, abr, abi, (pair[4:], True))
    dBm = mm(u_il, gst, mode="tn", name="d_bmat", b_shards=2, out_shards=2, **big)
    du_il, parts_up = mm(gst, Bm, mode="nt", name="d_u", a_shards=2, b_shards=2, tk=1024,
                         exchange=(pair[3:4], True))
    mine = {n: sum_chips(p, name="sum_chips_" + n) for n, p in zip(early, [*parts_xa, *parts_up, *parts_down])}
    du = row_shuffle(du_il, SCAN_CHUNKS, n_pos, name="du_token_order", add=du_skip, out_dtype=bf16)
    s5_g = s5_prep_bwd(*s5_in, dabr, dabi, dBm, dCm)
    dWu = mm(h0, du, mode="tn", name="dw_u", tm=1024, tk=1024)
    dWqkv = mm(h0, dqkv, mode="tn", name="dw_qkv", tm=1024, **big)
    dh0a = mm(du, Wu, mode="nt", name="d_h0_u", tn=1024)
    dh0 = mm(dqkv, Wqkv, mode="nt", name="d_h0_qkv", epi="add", aux=dh0a, **big)
    dx, dg_mix = rms_bwd(xs, g_mix, dh0, dx1, D, name="rms_bwd_mix")

    HW = SB_HEADS * LANES
    late = ("w_in", "ssm_w_glu", "w_out", "small")
    late_g = [jnp.stack([dWu, _unpad_cols(dWqkv[:, :HW]), _unpad_cols(dWqkv[:, HW:2 * HW]),
                         _unpad_cols(dWqkv[:, 2 * HW:])]),
              dWglu.reshape(N_CHIPS, -1, dWglu.shape[1]),
              jnp.concatenate([dWo_ssm, _unpad_rows(dWo_sb)]).reshape(N_CHIPS, -1, D)]
    da_re, da_im, dldt, dbT_re, dbT_im, dcT_re, dcT_im = s5_g
    small_g = {
        "g_mix": dg_mix, "ssm_a_re": da_re, "ssm_a_im": da_im, "ssm_log_dt": dldt,
        "ssm_b_re": dbT_re.reshape(C, G, P).transpose(1, 2, 0), "ssm_b_im": dbT_im.reshape(C, G, P).transpose(1, 2, 0),
        "ssm_c_re": dcT_re.reshape(C, G, P).transpose(1, 0, 2), "ssm_c_im": dcT_im.reshape(C, G, P).transpose(1, 0, 2),
        "ssm_d": dd, "sb_g_q": dg_q[:, :SB_HEAD_DIM], "sb_g_k": dg_k[:, :SB_HEAD_DIM], "g_out_ssm": dg_os,
        "g_out_sb": _unpad_cols(dg_osb), "g_xa": dg_xa, "g_mem": dg_mem, "xa_g_q": dg_xq, "xa_g_k": dg_xk,
        "g_mlp": dg_mlp,
    }
    late_g.append(_pack([small_g[n] for n in SMALL]).reshape(N_CHIPS, -1, LANES))

    parts_late = chip_scatter(sibling_sums(late_g, late, "late"), name="grad_to_chips_late")
    mine.update({n: sum_chips(p, name="sum_chips_" + n) for n, p in zip(late, parts_late)})
    mine = [mine[n] for n in list(BIG) + ["small"]]
    other = sibling_swap(mine, half=False, name="grad_half_to_sibling")
    shard = [jnp.where(c_idx == 0, jnp.concatenate([a, b]), jnp.concatenate([b, a])) for a, b in zip(mine, other)]
    small_all = allgather_chips([shard[-1]], name="gather_small")[0]
    small_red = small_all.reshape(-1, LANES)

    out = {}
    for n, gs in zip(BIG, shard[:-1]):
        shp = W[n].shape
        w2, m2, v2 = (t.reshape(gs.shape) for t in (W[n], M1[n], V2[n]))
        d, nm, nv = adamw(w2, gs, m2, v2, name="adamw_" + n)
        out[n] = tuple(t.reshape(shp) for t in (gs, d, nm, nv))
    shapes = [W[n].shape for n in SMALL]
    d, nm, nv = adamw(_pack([W[n] for n in SMALL]), small_red, _pack([M1[n] for n in SMALL]),
                      _pack([V2[n] for n in SMALL]), name="adamw_small")
    for n, gs, dd_, mm_, vv_ in zip(SMALL, _unpack(small_red, shapes), _unpack(d, shapes), _unpack(nm, shapes),
                                    _unpack(nv, shapes)):
        out[n] = (gs, dd_, mm_, vv_)
    res = [loss, dx[None]]
    for kind in range(4):
        res += [out[n][kind] for n in WEIGHTS]
    return tuple(res)
```

```python
import jax
import jax.numpy as jnp
from jax import lax
from jax.experimental import pallas as pl
from jax.experimental.pallas import tpu as pltpu

f32 = jnp.float32
bf16 = jnp.bfloat16

NORM_EPS = 1e-6
SSM_GROUPS = 32
SSM_GROUP = 16
SSM_STATE = 64
SB_HEADS = 8
SB_HEAD_DIM = 64
XA_HEADS = 4
XA_HEAD_DIM = 128
LANES = 128
SUBLANES = 8
N_CHIPS = 4
ADAM_LR = 0.001
ADAM_B1 = 0.9
ADAM_B2 = 0.999
ADAM_EPS = 1e-08
ADAM_WD = 0.01
ADAM_STEP = 10
VMEM_LIMIT = 56 * 1024 * 1024
MESH = pl.DeviceIdType.MESH
ANY = pl.BlockSpec(memory_space=pl.ANY)


def _cp(sem=None):
    return pltpu.CompilerParams(dimension_semantics=sem, vmem_limit_bytes=VMEM_LIMIT)


def _tile(n, pref):
    if n <= pref:
        return n
    t = (pref // LANES) * LANES
    while t > LANES and n % t:
        t -= LANES
    assert n % t == 0, (n, pref)
    return t


def _row_tile(n, pref):
    if n <= pref:
        return n
    t = (pref // SUBLANES) * SUBLANES
    while n % t:
        t -= SUBLANES
    return t


def _dot(a, b, dims):
    return lax.dot_general(a.astype(bf16), b.astype(bf16), (dims, ((), ())), preferred_element_type=f32)


_NN = ((1,), (0,))
_NT = ((1,), (1,))
_TN = ((0,), (0,))


@jax.custom_vjp
def bdot_nn(a, b):
    return _dot(a, b, _NN)


def _bdot_nn_fwd(a, b):
    return _dot(a, b, _NN), (a, b)


def _bdot_nn_bwd(res, g):
    a, b = res
    return _dot(g, b, _NT), _dot(a, g, _TN)


bdot_nn.defvjp(_bdot_nn_fwd, _bdot_nn_bwd)


@jax.custom_vjp
def bdot_nt(a, b):
    return _dot(a, b, _NT)


def _bdot_nt_fwd(a, b):
    return _dot(a, b, _NT), (a, b)


def _bdot_nt_bwd(res, g):
    a, b = res
    return _dot(g, b, _NN), _dot(g, a, _TN)


bdot_nt.defvjp(_bdot_nt_fwd, _bdot_nt_bwd)


def _rms(x, g, denom):
    r = lax.rsqrt(jnp.sum(x * x, axis=-1, keepdims=True) * (1.0 / denom) + NORM_EPS)
    return x * r * g


def _opspec(block, row_of, col_of, shards, ncol_tiles):
    if shards == 1:
        return pl.BlockSpec(block, lambda i, j, k: (row_of(i, j, k), col_of(i, j, k)))
    per = ncol_tiles // shards
    return pl.BlockSpec((None,) + block,
                        lambda i, j, k: (col_of(i, j, k) // per, row_of(i, j, k), col_of(i, j, k) % per))


def _call_with_exchange(body, args, exchange, *, name, grid, in_specs, out_specs, out_shape, scratch_shapes=()):
    xs, scatter = exchange
    n, n_in, n_out, n_scr = len(xs), len(in_specs), len(out_specs), len(scratch_shapes)
    x_shapes, sems = _chip_exchange_args(xs, scatter)

    def wrapped(*refs):
        ins, x_ins = refs[:n_in], refs[n_in:n_in + n]
        outs, x_outs = refs[n_in + n:n_in + n + n_out], refs[n_in + n + n_out:n_in + 2 * n + n_out]
        scratch, x_sems = refs[n_in + 2 * n + n_out:n_in + 2 * n + n_out + n_scr], refs[n_in + 2 * n + n_out + n_scr:]
        first, last = True, True
        for d, steps in enumerate(grid):
            first = first & (pl.program_id(d) == 0)
            last = last & (pl.program_id(d) == steps - 1)

        @pl.when(first)
        def _():
            _chip_exchange(x_ins, x_outs, x_sems, scatter, True)

        body(*ins, *outs, *scratch)

        @pl.when(last)
        def _():
            _chip_exchange(x_ins, x_outs, x_sems, scatter, False)

    res = pl.pallas_call(
        wrapped, name=name, grid=grid, in_specs=list(in_specs) + [ANY] * n, out_specs=list(out_specs) + [ANY] * n,
        out_shape=list(out_shape) + x_shapes, scratch_shapes=list(scratch_shapes) + sems,
        compiler_params=_cp(("arbitrary",) * len(grid)),
    )(*args, *xs)
    return res[:n_out], res[n_out:]


def mm(a, b, *, mode, name, tm=512, tn=512, tk=512, pro="none", epi="none", aux=None,
       out_dtype=f32, a_shards=1, b_shards=1, out_shards=1, exchange=None):
    ar, ac = a.shape[-2], a.shape[-1] * a_shards
    br, bc = b.shape[-2], b.shape[-1] * b_shards
    if mode == "nn":
        M, K, N = ar, ac, bc
        assert br == K
    elif mode == "nt":
        M, K, N = ar, ac, br
        assert bc == K
    else:
        M, K, N = ac, ar, bc
        assert br == K
    tm, tn, tk = _tile(M, tm), _tile(N, tn), _tile(K, tk)
    if a_shards > 1:
        if mode == "tn":
            tm = _tile(ac // a_shards, tm)
        else:
            tk = _tile(ac // a_shards, tk)
    if b_shards > 1:
        if mode == "nt":
            tk = _tile(bc // b_shards, tk)
        else:
            tn = _tile(bc // b_shards, tn)
    if out_shards > 1:
        tn = _tile(N // out_shards, tn)
    nm, nn_, nk = M // tm, N // tn, K // tk
    I = lambda i, j, k: i
    J = lambda i, j, k: j
    Kk = lambda i, j, k: k
    if mode == "nn":
        a_spec = _opspec((tm, tk), I, Kk, a_shards, nk)
        b_spec = _opspec((tk, tn), Kk, J, b_shards, nn_)
        dims = _NN
    elif mode == "nt":
        a_spec = _opspec((tm, tk), I, Kk, a_shards, nk)
        b_spec = _opspec((tn, tk), J, Kk, b_shards, nk)
        dims = _NT
    else:
        a_spec = _opspec((tk, tm), Kk, I, a_shards, nm)
        b_spec = _opspec((tk, tn), Kk, J, b_shards, nn_)
        dims = _TN
    in_specs = [a_spec, b_spec]
    args = [a, b]
    if epi != "none":
        in_specs.append(pl.BlockSpec((tm, tn), lambda i, j, k: (i, j)))
        args.append(aux)
    if out_shards == 1:
        out_spec = pl.BlockSpec((tm, tn), lambda i, j, k: (i, j))
        out_shape = jax.ShapeDtypeStruct((M, N), out_dtype)
    else:
        per = nn_ // out_shards
        out_spec = pl.BlockSpec((None, tm, tn), lambda i, j, k: (j // per, i, j % per))
        out_shape = jax.ShapeDtypeStruct((out_shards, M, N // out_shards), out_dtype)

    def body(*refs):
        a_ref, b_ref = refs[0], refs[1]
        pos = 2
        if epi != "none":
            aux_ref = refs[pos]
            pos += 1
        o_ref, acc_ref = refs[pos], refs[pos + 1]
        k = pl.program_id(2)

        @pl.when(k == 0)
        def _():
            acc_ref[...] = jnp.zeros_like(acc_ref)

        av = a_ref[...]
        if pro == "relu2":
            av = jnp.square(jnp.maximum(av.astype(f32), 0.0))
        acc_ref[...] += _dot(av, b_ref[...], dims)

        @pl.when(k == nk - 1)
        def _():
            res = acc_ref[...]
            if epi == "add":
                res = res + aux_ref[...].astype(f32)
            elif epi == "mul2relu":
                res = res * (2.0 * jnp.maximum(aux_ref[...].astype(f32), 0.0))
            o_ref[...] = res.astype(out_dtype)

    acc = [pltpu.VMEM((tm, tn), f32)]
    if exchange is not None:
        (out,), moved = _call_with_exchange(body, args, exchange, name=name, grid=(nm, nn_, nk), in_specs=in_specs,
                                            out_specs=[out_spec], out_shape=[out_shape], scratch_shapes=acc)
        return out, moved
    return pl.pallas_call(
        body, name=name, grid=(nm, nn_, nk), in_specs=in_specs, out_specs=out_spec, out_shape=out_shape,
        scratch_shapes=acc, compiler_params=_cp(("parallel", "parallel", "arbitrary")),
    )(*args)


def rms_norm(x, g, denom, *, name, ts=512):
    S, D = x.shape
    ts = _tile(S, ts)

    def body(x_ref, g_ref, h_ref):
        h_ref[...] = _rms(x_ref[...], g_ref[...], denom).astype(bf16)

    row = pl.BlockSpec((ts, D), lambda i: (i, 0))
    return pl.pallas_call(
        body, name=name, grid=(S // ts,), in_specs=[row, pl.BlockSpec((1, D), lambda i: (0, 0))], out_specs=row,
        out_shape=jax.ShapeDtypeStruct((S, D), bf16), compiler_params=_cp(("parallel",)),
    )(x, g)


def rms_bwd(x, g, dy, res, denom, *, name, ts=256, twin=False):
    S, D = x.shape
    ts = _tile(S, ts)
    has_res = res is not None

    def body(*refs):
        x_ref, g_ref, dy_ref = refs[:3]
        outs = refs[4:] if has_res else refs[3:]
        _, vjp = jax.vjp(lambda xv, gv: _rms(xv, gv, denom), x_ref[...], g_ref[...])
        dx, dg = vjp(dy_ref[...])
        if has_res:
            dx = dx + refs[3][...]
        outs[0][...] = dx
        if twin:
            outs[2][...] = dx.astype(bf16)
        dg_ref = outs[1]

        @pl.when(pl.program_id(0) == 0)
        def _():
            dg_ref[...] = jnp.zeros_like(dg_ref)

        dg_ref[...] += dg

    row = pl.BlockSpec((ts, D), lambda i: (i, 0))
    vec = pl.BlockSpec((1, D), lambda i: (0, 0))
    in_specs = [row, vec, row] + ([row] if has_res else [])
    args = [x, g, dy] + ([res] if has_res else [])
    return pl.pallas_call(
        body, name=name, grid=(S // ts,), in_specs=in_specs, out_specs=[row, vec] + ([row] if twin else []),
        out_shape=[jax.ShapeDtypeStruct((S, D), f32), jax.ShapeDtypeStruct((1, D), f32)]
        + ([jax.ShapeDtypeStruct((S, D), bf16)] if twin else []),
        compiler_params=_cp(("arbitrary",)),
    )(*args)


LOG2E = 1.4426950408889634
LN2 = 0.6931471805599453


def _qk_fn(q, k, gq, gk):
    qs, ks = [], []
    for h in range(SB_HEADS):
        sl = slice(h * LANES, (h + 1) * LANES)
        qs.append(_rms(q[:, sl], gq, SB_HEAD_DIM) * (SB_HEAD_DIM ** -0.5 * LOG2E))
        ks.append(_rms(k[:, sl], gk, SB_HEAD_DIM))
    return jnp.concatenate(qs, axis=1), jnp.concatenate(ks, axis=1)


def qkv_prep(qkv, gq, gk, *, ts=256):
    S = qkv.shape[0]
    W = SB_HEADS * LANES
    ts = _tile(S, ts)

    def body(q_ref, k_ref, v_ref, gq_ref, gk_ref, qn_ref, kn_ref, vb_ref):
        qn, kn = _qk_fn(q_ref[...], k_ref[...], gq_ref[...], gk_ref[...])
        qn_ref[...] = qn.astype(bf16)
        kn_ref[...] = kn.astype(bf16)
        vb_ref[...] = v_ref[...].astype(bf16)

    out = jax.ShapeDtypeStruct((S, W), bf16)
    gspec = pl.BlockSpec((1, LANES), lambda i: (0, 0))
    ospec = pl.BlockSpec((ts, W), lambda i: (i, 0))
    col = lambda c: pl.BlockSpec((ts, W), lambda i: (i, c))
    return pl.pallas_call(
        body, name="qkv_prep", grid=(S // ts,), in_specs=[col(0), col(1), col(2), gspec, gspec],
        out_specs=[ospec, ospec, ospec], out_shape=[out, out, out], compiler_params=_cp(("parallel",)),
    )(qkv, qkv, qkv, gq, gk)


def qkv_bwd(qkv, gq, gk, dqn, dkn, dv, exchange, *, ts=256):
    S = qkv.shape[0]
    W = SB_HEADS * LANES
    ts = _tile(S, ts)

    def body(q_ref, k_ref, gq_ref, gk_ref, dqn_ref, dkn_ref, dv_ref, o_ref, dgq_ref, dgk_ref):
        _, vjp = jax.vjp(_qk_fn, q_ref[...], k_ref[...], gq_ref[...], gk_ref[...])
        dq, dk, dgq, dgk = vjp((dqn_ref[...] * LN2, dkn_ref[...] * LN2))
        o_ref[:, 0:W] = dq.astype(bf16)
        o_ref[:, W:2 * W] = dk.astype(bf16)
        o_ref[:, 2 * W:3 * W] = dv_ref[...].astype(bf16)

        @pl.when(pl.program_id(0) == 0)
        def _():
            dgq_ref[...] = jnp.zeros_like(dgq_ref)
            dgk_ref[...] = jnp.zeros_like(dgk_ref)

        dgq_ref[...] += dgq
        dgk_ref[...] += dgk

    gspec = pl.BlockSpec((1, LANES), lambda i: (0, 0))
    row = pl.BlockSpec((ts, W), lambda i: (i, 0))
    col = lambda c: pl.BlockSpec((ts, W), lambda i: (i, c))
    (dqkv, dgq, dgk), moved = _call_with_exchange(
        body, (qkv, qkv, gq, gk, dqn, dkn, dv), exchange, name="qkv_bwd", grid=(S // ts,),
        in_specs=[col(0), col(1), gspec, gspec, row, row, row],
        out_specs=[pl.BlockSpec((ts, 3 * W), lambda i: (i, 0)), gspec, gspec],
        out_shape=[jax.ShapeDtypeStruct((S, 3 * W), bf16), jax.ShapeDtypeStruct((1, LANES), f32),
                   jax.ShapeDtypeStruct((1, LANES), f32)])
    return dqkv, dgq, dgk, moved


def _sb_weights(q, ks, R, masked, row, col, UU):
    ls = [_dot(q, k, _NT) for k in ks]
    lbs, lm0s, cats = [], [], []
    for l, diag in zip(ls, masked):
        neg_abs = pltpu.bitcast(pltpu.bitcast(l, jnp.uint32) | jnp.uint32(0x80000000), f32)
        lp = jnp.log2(1.0 + jnp.exp2(neg_abs))
        lb = jnp.minimum(l, 0.0) - lp
        lm = lb - l
        if diag:
            lm = jnp.where(col < row, lm, 0.0)
        hi = lm.astype(bf16)
        lo = (lm - hi.astype(f32)).astype(bf16)
        lbs.append(lb)
        lm0s.append(lm[:, 0:1])
        cats.append(jnp.concatenate([hi, lo], axis=1))
    sums = [_dot(c, UU, _NN) for c in cats]
    ws = []
    for lb, lm0, A, diag in zip(lbs, lm0s, sums, masked):
        w = jnp.exp2(lb + (A + R))
        if diag:
            w = jnp.where(col < row, w, 0.0)
        R = R + (A[:, 0:1] + lm0)
        ws.append(w)
    return lbs, ws, R


def _tri2(tk):
    r = lax.broadcasted_iota(jnp.int32, (2 * tk, tk), 0)
    r = jnp.where(r >= tk, r - tk, r)
    c = lax.broadcasted_iota(jnp.int32, (2 * tk, tk), 1)
    return (r > c).astype(bf16)


SB_GROUP = 8


SB_ALL_ZERO_BELOW = -160.0


def _sweep(i, blocks_of, carry, descending, right_sum=None, ran=None):
    G = SB_GROUP
    n = jnp.maximum(i - 1, 0)
    rem, full = n % G, n // G
    asc = lambda js: js if descending else js[::-1]

    def first_group(c):
        one = lambda c: blocks_of([i], c, [True])
        two = lambda c: blocks_of(asc([i, i - 1]), c, asc([True, False]))
        return lax.cond(i >= 1, two, one, c)

    def body(p, c):
        return blocks_of(asc([i - 2 - p * G - u for u in range(G)]), c, [False] * G)

    def left_over(r):
        return lambda c: blocks_of(asc([r - 1 - u for u in range(r)]), c, [False] * r) if r else c

    if descending:
        carry = first_group(carry)
        alive = lambda c: jnp.max(right_sum(c)) > SB_ALL_ZERO_BELOW
        bodies, carry = lax.while_loop(lambda s: (s[0] < full) & alive(s[1]),
                                       lambda s: (s[0] + 1, body(s[0], s[1])), (jnp.int32(0), carry))
        tail = (bodies == full) & alive(carry)
        carry = lax.switch(jnp.where(tail, rem, 0), [left_over(r) for r in range(G)], carry)
        return carry, (bodies, tail)
    bodies, tail = ran
    carry = lax.switch(jnp.where(tail, rem, 0), [left_over(r) for r in range(G)], carry)
    carry = lax.fori_loop(0, bodies, lambda t, c: body(bodies - 1 - t, c), carry)
    return first_group(carry)


def sb_fwd(qn, kn, vb, *, tq=256):
    S, W = qn.shape
    H = W // LANES
    tq = _tile(S, tq)
    tk = tq
    nq = S // tq

    def body(q_ref, k_ref, v_ref, o_ref, uu_s):
        i = pl.program_id(1)

        @pl.when((pl.program_id(0) == 0) & (i == 0))
        def _():
            uu_s[...] = _tri2(tk)

        q = q_ref[...]
        row = lax.broadcasted_iota(jnp.int32, (tq, tk), 0)
        col = lax.broadcasted_iota(jnp.int32, (tq, tk), 1)
        UU = uu_s[...]

        def blocks(js, c, masked):
            rows = [pl.ds(pl.multiple_of(j * tk, tk), tk) for j in js]
            _, ws, R = _sb_weights(q, [k_ref[r, :] for r in rows], c[0], masked, row, col, UU)
            acc = c[1]
            for w, r in zip(ws, rows):
                acc = acc + _dot(w, v_ref[r, :], _NN)
            return R, acc

        c, _ = _sweep(i, blocks, (jnp.zeros((tq, 1), f32), jnp.zeros((tq, LANES), f32)), True, lambda c: c[0])
        o_ref[...] = c[1]

    qspec = pl.BlockSpec((tq, LANES), lambda h, i: (i, h))
    kspec = pl.BlockSpec((S, LANES), lambda h, i: (0, h))
    return pl.pallas_call(
        body, name="sb_fwd", grid=(H, nq), in_specs=[qspec, kspec, kspec], out_specs=qspec,
        out_shape=jax.ShapeDtypeStruct((S, W), f32), scratch_shapes=[pltpu.VMEM((2 * tk, tk), bf16)],
        compiler_params=_cp(("arbitrary", "arbitrary")),
    )(qn, kn, vb)


def sb_bwd(qn, kn, vb, do, *, tq=256):
    S, W = qn.shape
    H = W // LANES
    tq = _tile(S, tq)
    tk = tq
    nq = S // tq

    def body(q_ref, k_ref, v_ref, do_ref, dq_ref, dk_ref, dv_ref, dz_s, beta_s, uu_s, ue_s):
        i = pl.program_id(1)
        row = lax.broadcasted_iota(jnp.int32, (tq, tk), 0)
        col = lax.broadcasted_iota(jnp.int32, (tq, tk), 1)

        @pl.when((pl.program_id(0) == 0) & (i == 0))
        def _():
            uu_s[...] = _tri2(tk)
            ue_s[...] = (row < col).astype(bf16)

        @pl.when(i == 0)
        def _():
            dk_ref[...] = jnp.zeros_like(dk_ref)
            dv_ref[...] = jnp.zeros_like(dv_ref)

        q = q_ref[...]
        dob = do_ref[...].astype(bf16)
        UU = uu_s[...]
        Ue = ue_s[...]

        def sweep1(js, R, masked):
            rows = [pl.ds(pl.multiple_of(j * tk, tk), tk) for j in js]
            dws = [_dot(dob, v_ref[r, :], _NT) for r in rows]
            lbs, ws, R = _sb_weights(q, [k_ref[r, :] for r in rows], R, masked, row, col, UU)
            for j, lb, w, dw in zip(js, lbs, ws, dws):
                dz_s[j] = (dw * w).astype(bf16)
                beta_s[j] = jnp.exp2(lb).astype(bf16)
            for r, w in zip(rows, ws):
                dv_ref[r, :] += _dot(w, dob, _TN)
            return R

        _, ran = _sweep(i, sweep1, jnp.zeros((tq, 1), f32), True, lambda R: R)

        def sweep2(js, c, masked):
            rows = [pl.ds(pl.multiple_of(j * tk, tk), tk) for j in js]
            dzbs = [dz_s[j] for j in js]
            sums = [_dot(dzb, Ue, _NN) for dzb in dzbs]
            Lz, dq = c
            dlbs = []
            for j, dzb, Cz, diag in zip(js, dzbs, sums, masked):
                dz = dzb.astype(f32)
                dl = dz - beta_s[j].astype(f32) * (dz + (Cz + Lz))
                if diag:
                    dl = jnp.where(col < row, dl, 0.0)
                Lz = Lz + (Cz[:, tk - 1:tk] + dz[:, tk - 1:tk])
                dlbs.append(dl.astype(bf16))
            for r, dlb in zip(rows, dlbs):
                dq = dq + _dot(dlb, k_ref[r, :], _NN)
            for r, dlb in zip(rows, dlbs):
                dk_ref[r, :] += _dot(dlb, q, _TN)
            return Lz, dq

        c = _sweep(i, sweep2, (jnp.zeros((tq, 1), f32), jnp.zeros((tq, LANES), f32)), False, ran=ran)
        dq_ref[...] = c[1]

    qspec = pl.BlockSpec((tq, LANES), lambda h, i: (i, h))
    kspec = pl.BlockSpec((S, LANES), lambda h, i: (0, h))
    full = jax.ShapeDtypeStruct((S, W), f32)
    return pl.pallas_call(
        body, name="sb_bwd", grid=(H, nq), in_specs=[qspec, kspec, kspec, qspec],
        out_specs=[qspec, kspec, kspec], out_shape=[full, full, full],
        scratch_shapes=[pltpu.VMEM((nq, tq, tk), bf16), pltpu.VMEM((nq, tq, tk), bf16),
                        pltpu.VMEM((2 * tk, tk), bf16), pltpu.VMEM((tk, tk), bf16)],
        compiler_params=_cp(("arbitrary", "arbitrary")),
    )(qn, kn, vb, do)


def _xa_fn(qx, kv, gq, gk):
    XW = XA_HEADS * XA_HEAD_DIM
    outs = []
    for h in range(XA_HEADS):
        sl = slice(h * XA_HEAD_DIM, (h + 1) * XA_HEAD_DIM)
        qn = _rms(qx[:, sl], gq, XA_HEAD_DIM)
        kn = _rms(kv[:, sl], gk, XA_HEAD_DIM)
        v = kv[:, XW + h * XA_HEAD_DIM:XW + (h + 1) * XA_HEAD_DIM]
        s = bdot_nt(qn, kn) * (XA_HEAD_DIM ** -0.5)
        e = jnp.exp(s - lax.stop_gradient(jnp.max(s, axis=-1, keepdims=True)))
        p = e / jnp.sum(e, axis=-1, keepdims=True)
        outs.append(bdot_nn(p, v))
    return jnp.concatenate(outs, axis=1)


def xa_fwd(qx, kv, gq, gk, *, ts=256):
    S, XW = qx.shape
    M = kv.shape[0]
    ts = _tile(S, ts)

    def body(q_ref, kv_ref, gq_ref, gk_ref, o_ref):
        o_ref[...] = _xa_fn(q_ref[...], kv_ref[...], gq_ref[...], gk_ref[...]).astype(bf16)

    row = pl.BlockSpec((ts, XW), lambda i: (i, 0))
    gspec = pl.BlockSpec((1, XA_HEAD_DIM), lambda i: (0, 0))
    return pl.pallas_call(
        body, name="xa_fwd", grid=(S // ts,),
        in_specs=[row, pl.BlockSpec((M, 2 * XW), lambda i: (0, 0)), gspec, gspec], out_specs=row,
        out_shape=jax.ShapeDtypeStruct((S, XW), bf16), compiler_params=_cp(("parallel",)),
    )(qx, kv, gq, gk)


def xa_bwd(qx, kv, gq, gk, do, *, ts=256):
    S, XW = qx.shape
    M = kv.shape[0]
    ts = _tile(S, ts)

    def body(q_ref, kv_ref, gq_ref, gk_ref, do_ref, dq_ref, dkv_ref, dgq_ref, dgk_ref):
        _, vjp = jax.vjp(_xa_fn, q_ref[...], kv_ref[...], gq_ref[...], gk_ref[...])
        dq, dkv, dgq, dgk = vjp(do_ref[...].astype(f32))
        dq_ref[...] = dq.astype(bf16)

        @pl.when(pl.program_id(0) == 0)
        def _():
            dkv_ref[...] = jnp.zeros_like(dkv_ref)
            dgq_ref[...] = jnp.zeros_like(dgq_ref)
            dgk_ref[...] = jnp.zeros_like(dgk_ref)

        dkv_ref[...] += dkv
        dgq_ref[...] += dgq
        dgk_ref[...] += dgk

    row = pl.BlockSpec((ts, XW), lambda i: (i, 0))
    gspec = pl.BlockSpec((1, XA_HEAD_DIM), lambda i: (0, 0))
    kvspec = pl.BlockSpec((M, 2 * XW), lambda i: (0, 0))
    gshape = jax.ShapeDtypeStruct((1, XA_HEAD_DIM), f32)
    return pl.pallas_call(
        body, name="xa_bwd", grid=(S // ts,), in_specs=[row, kvspec, gspec, gspec, row],
        out_specs=[row, kvspec, gspec, gspec],
        out_shape=[jax.ShapeDtypeStruct((S, XW), bf16), jax.ShapeDtypeStruct((M, 2 * XW), f32), gshape, gshape],
        compiler_params=_cp(("arbitrary",)),
    )(qx, kv, gq, gk, do)


def _s5_prep_fn(a_re, a_im, ldt, bT_re, bT_im, cT_re, cT_im):
    G, P, C = SSM_GROUPS, SSM_STATE, SSM_GROUP
    GP, GC = G * P, G * C
    lg_p, lg_c = P.bit_length() - 1, C.bit_length() - 1
    gi = lax.broadcasted_iota(jnp.int32, (G, GP), 0)
    ci = lax.broadcasted_iota(jnp.int32, (G, GP), 1) >> lg_p
    expand_dt = (gi == ci).astype(f32)
    dte = jnp.dot(jnp.exp(ldt), expand_dt, precision=lax.Precision.HIGHEST, preferred_element_type=f32)
    zr, zi = a_re * dte, a_im * dte
    mag = jnp.exp(zr)
    abr, abi = mag * jnp.cos(zi), mag * jnp.sin(zi)
    nr, ni = abr - 1.0, abi
    den = a_re * a_re + a_im * a_im
    cr = (nr * a_re + ni * a_im) / den
    cim = (ni * a_re - nr * a_im) / den
    bbr = cr * bT_re - cim * bT_im
    bbi = cr * bT_im + cim * bT_re
    rowg = lax.broadcasted_iota(jnp.int32, (GC, GP), 0) >> lg_c
    colg = lax.broadcasted_iota(jnp.int32, (GC, GP), 1) >> lg_p
    diag = rowg == colg

    def expand(t):
        return jnp.where(diag, jnp.broadcast_to(t[None], (G, C, GP)).reshape(GC, GP), 0.0)

    return abr, abi, expand(bbr), expand(bbi), expand(cT_re), expand(-cT_im)


def s5_prep(a_re, a_im, ldt, bT_re, bT_im, cT_re, cT_im):
    GP, GC = SSM_GROUPS * SSM_STATE, SSM_GROUPS * SSM_GROUP

    def body(a_re_ref, a_im_ref, ldt_ref, bTr_ref, bTi_ref, cTr_ref, cTi_ref, abr_ref, abi_ref, B_ref, C_ref):
        abr, abi, Br, Bi, Cr, Ci = _s5_prep_fn(a_re_ref[...], a_im_ref[...], ldt_ref[...], bTr_ref[...],
                                               bTi_ref[...], cTr_ref[...], cTi_ref[...])
        abr_ref[...] = abr
        abi_ref[...] = abi
        B_ref[0] = Br.astype(bf16)
        B_ref[1] = Bi.astype(bf16)
        C_ref[0] = Cr.astype(bf16)
        C_ref[1] = Ci.astype(bf16)

    vec = jax.ShapeDtypeStruct((1, GP), f32)
    mat = jax.ShapeDtypeStruct((2, GC, GP), bf16)
    return pl.pallas_call(body, name="s5_prep", out_shape=[vec, vec, mat, mat], compiler_params=_cp())(
        a_re, a_im, ldt, bT_re, bT_im, cT_re, cT_im)


def s5_prep_bwd(a_re, a_im, ldt, bT_re, bT_im, cT_re, cT_im, dabr, dabi, dB, dC):
    def body(a_re_ref, a_im_ref, ldt_ref, bTr_ref, bTi_ref, cTr_ref, cTi_ref, dabr_ref, dabi_ref, dB_ref, dC_ref,
             *outs):
        _, vjp = jax.vjp(_s5_prep_fn, a_re_ref[...], a_im_ref[...], ldt_ref[...], bTr_ref[...], bTi_ref[...],
                         cTr_ref[...], cTi_ref[...])
        grads = vjp((dabr_ref[...], dabi_ref[...], dB_ref[0], dB_ref[1], dC_ref[0], dC_ref[1]))
        for o_ref, gv in zip(outs, grads):
            o_ref[...] = gv

    ins = (a_re, a_im, ldt, bT_re, bT_im, cT_re, cT_im)
    return pl.pallas_call(body, name="s5_prep_bwd", out_shape=[jax.ShapeDtypeStruct(v.shape, f32) for v in ins],
                          compiler_params=_cp())(*ins, dabr, dabi, dB, dC)


def _cmul(ar, ai, br, bi):
    return ar * br - ai * bi, ar * bi + ai * br


SCAN_CHUNKS = 32


def _chunk_carry(Lr, Li, Pr, Pi, scratch, reverse):
    lr_ref, li_ref, cr_ref, ci_ref = scratch
    lr_ref[...] = Lr
    li_ref[...] = Li
    cur_r = jnp.zeros((1, LANES), f32)
    cur_i = jnp.zeros((1, LANES), f32)
    order = range(SCAN_CHUNKS - 1, -1, -1) if reverse else range(SCAN_CHUNKS)
    for c in order:
        cr_ref[pl.ds(c, 1), :] = cur_r
        ci_ref[pl.ds(c, 1), :] = cur_i
        mr, mi = _cmul(Pr, Pi, cur_r, cur_i)
        cur_r, cur_i = lr_ref[pl.ds(c, 1), :] + mr, li_ref[pl.ds(c, 1), :] + mi
    return cr_ref[...], ci_ref[...]


def _chunk_rows(j):
    return pl.ds(pl.multiple_of(j * SCAN_CHUNKS, SCAN_CHUNKS), SCAN_CHUNKS)


def row_shuffle(x, a, b, *, name, add=None, out_dtype=f32):
    S, W = x.shape
    assert a * b == S and x.dtype == f32

    def body(*refs):
        x_ref, o_ref = refs[0], refs[-1]

        def step(i, _):
            dst = pl.ds(pl.multiple_of(i * b, b), b)
            v = x_ref[pl.ds(i, b, stride=a), :]
            if add is not None:
                v = v + refs[1][dst, :]
            o_ref[dst, :] = v.astype(out_dtype)
            return 0

        lax.fori_loop(0, a, step, 0)

    col = pl.BlockSpec((S, LANES), lambda t: (0, t))
    args = [x] + ([add] if add is not None else [])
    return pl.pallas_call(
        body, name=name, grid=(W // LANES,), in_specs=[col] * len(args), out_specs=col,
        out_shape=jax.ShapeDtypeStruct((S, W), out_dtype), compiler_params=_cp(("parallel",)),
    )(*args)


def _scan_scratch(n):
    small = pltpu.VMEM((SCAN_CHUNKS, LANES), f32)
    return [pltpu.VMEM((n, LANES), f32), pltpu.VMEM((n, LANES), f32), small, small, small, small]


def _drive(x_ref, m_ref, work_ref):
    S = x_ref.shape[0]
    rows = min(S, 1024)
    m = jnp.concatenate([m_ref[0], m_ref[1]], axis=1)

    def chunk(c, _):
        r = pl.ds(pl.multiple_of(c * rows, rows), rows)
        y = _dot(x_ref[r, :], m, _NN)
        work_ref[0, r, :] = y[:, :LANES]
        work_ref[1, r, :] = y[:, LANES:]
        return 0

    lax.fori_loop(0, S // rows, chunk, 0)


def scan_fwd(u_il, Bm, abr, abi, exchange):
    S, C = u_il.shape
    N = Bm.shape[2]
    n = S // SCAN_CHUNKS
    shp = (SCAN_CHUNKS, LANES)

    def body(u_ref, B_ref, ar_ref, ai_ref, st_ref, work_ref, pwr_ref, pwi_ref, *scratch):
        a1r, a1i = ar_ref[...], ai_ref[...]
        ar = jnp.broadcast_to(a1r, shp)
        ai = jnp.broadcast_to(a1i, shp)
        _drive(u_ref, B_ref, work_ref)
        sr_ref, si_ref = work_ref.at[0], work_ref.at[1]

        def step(j, c):
            sr, si, pr, pi = c
            rows = _chunk_rows(j)
            mr, mi = _cmul(ar, ai, sr, si)
            sr, si = mr + sr_ref[rows, :], mi + si_ref[rows, :]
            sr_ref[rows, :] = sr
            si_ref[rows, :] = si
            pwr_ref[pl.ds(j, 1), :] = pr
            pwi_ref[pl.ds(j, 1), :] = pi
            npr, npi = _cmul(a1r, a1i, pr, pi)
            return sr, si, npr, npi

        z = jnp.zeros(shp, f32)
        sr, si, _, _ = lax.fori_loop(0, n, step, (z, z, a1r, a1i), unroll=2)
        cr, ci = _chunk_carry(sr, si, pwr_ref[pl.ds(n - 1, 1), :], pwi_ref[pl.ds(n - 1, 1), :], scratch, False)

        def step2(j, _):
            rows = _chunk_rows(j)
            pr = jnp.broadcast_to(pwr_ref[pl.ds(j, 1), :], shp)
            pi = jnp.broadcast_to(pwi_ref[pl.ds(j, 1), :], shp)
            mr, mi = _cmul(pr, pi, cr, ci)
            st_ref[0, rows, :] = (sr_ref[rows, :] + mr).astype(bf16)
            st_ref[1, rows, :] = (si_ref[rows, :] + mi).astype(bf16)
            return 0

        lax.fori_loop(0, n, step2, 0, unroll=4)

    blk = pl.BlockSpec((2, S, LANES), lambda t: (0, 0, t))
    vec = pl.BlockSpec((1, LANES), lambda t: (0, t))
    (st,), moved = _call_with_exchange(
        body, (u_il, Bm, abr, abi), exchange, name="scan_fwd", grid=(N // LANES,),
        in_specs=[pl.BlockSpec((S, C), lambda t: (0, 0)), pl.BlockSpec((2, C, LANES), lambda t: (0, 0, t)), vec, vec],
        out_specs=[blk], out_shape=[jax.ShapeDtypeStruct((2, S, N), bf16)],
        scratch_shapes=[pltpu.VMEM((2, S, LANES), f32)] + _scan_scratch(n))
    return st, moved


def scan_bwd(dy_il, Cm, st, abr, abi, exchange):
    _, S, N = st.shape
    C = dy_il.shape[1]
    n = S // SCAN_CHUNKS
    shp = (SCAN_CHUNKS, LANES)

    def body(dy_ref, C_ref, st_ref, ar_ref, ai_ref, g_ref, dar_ref, dai_ref, work_ref, qwr_ref, qwi_ref, *scratch):
        a1r, a1i = ar_ref[...], -ai_ref[...]
        ar = jnp.broadcast_to(a1r, shp)
        nai = jnp.broadcast_to(a1i, shp)
        _drive(dy_ref, C_ref, work_ref)
        gr_ref, gi_ref = work_ref.at[0], work_ref.at[1]
        sr_ref, si_ref = st_ref.at[0], st_ref.at[1]

        def step(jj, c):
            gr, gi, qr, qi = c
            j = n - 1 - jj
            rows = _chunk_rows(j)
            mr, mi = _cmul(ar, nai, gr, gi)
            gr, gi = mr + gr_ref[rows, :], mi + gi_ref[rows, :]
            gr_ref[rows, :] = gr
            gi_ref[rows, :] = gi
            qwr_ref[pl.ds(j, 1), :] = qr
            qwi_ref[pl.ds(j, 1), :] = qi
            nqr, nqi = _cmul(a1r, a1i, qr, qi)
            return gr, gi, nqr, nqi

        z = jnp.zeros(shp, f32)
        gr, gi, _, _ = lax.fori_loop(0, n, step, (z, z, a1r, a1i), unroll=2)
        cr, ci = _chunk_carry(gr, gi, qwr_ref[pl.ds(0, 1), :], qwi_ref[pl.ds(0, 1), :], scratch, True)
        sub = lax.broadcasted_iota(jnp.int32, shp, 0)

        def fix(j, spr, spi, acc):
            rows = _chunk_rows(j)
            qr = jnp.broadcast_to(qwr_ref[pl.ds(j, 1), :], shp)
            qi = jnp.broadcast_to(qwi_ref[pl.ds(j, 1), :], shp)
            mr, mi = _cmul(qr, qi, cr, ci)
            gr = gr_ref[rows, :] + mr
            gi = gi_ref[rows, :] + mi
            g_ref[0, rows, :] = gr.astype(bf16)
            g_ref[1, rows, :] = gi.astype(bf16)
            return acc[0] + gr * spr + gi * spi, acc[1] + gi * spr - gr * spi

        last = _chunk_rows(n - 1)
        spr = jnp.where(sub == 0, 0.0, pltpu.roll(sr_ref[last, :].astype(f32), 1, 0))
        spi = jnp.where(sub == 0, 0.0, pltpu.roll(si_ref[last, :].astype(f32), 1, 0))
        acc = fix(0, spr, spi, (z, z))

        def step2(j, acc):
            prev = _chunk_rows(j - 1)
            return fix(j, sr_ref[prev, :].astype(f32), si_ref[prev, :].astype(f32), acc)

        acc = lax.fori_loop(1, n, step2, acc)
        dar_ref[...] = jnp.sum(acc[0], axis=0, keepdims=True)
        dai_ref[...] = jnp.sum(acc[1], axis=0, keepdims=True)

    blk = pl.BlockSpec((2, S, LANES), lambda t: (0, 0, t))
    vec = pl.BlockSpec((1, LANES), lambda t: (0, t))
    vshape = jax.ShapeDtypeStruct((1, N), f32)
    (g, dar, dai), moved = _call_with_exchange(
        body, (dy_il, Cm, st, abr, abi), exchange, name="scan_bwd", grid=(N // LANES,),
        in_specs=[pl.BlockSpec((S, C), lambda t: (0, 0)), pl.BlockSpec((2, C, LANES), lambda t: (0, 0, t)), blk,
                  vec, vec],
        out_specs=[blk, vec, vec], out_shape=[jax.ShapeDtypeStruct((2, S, N), bf16), vshape, vshape],
        scratch_shapes=[pltpu.VMEM((2, S, LANES), f32)] + _scan_scratch(n))
    return g, dar, dai, moved


def _glu_fn(ypre, wglu):
    y = jax.nn.gelu(ypre)
    return y * jax.nn.sigmoid(bdot_nn(y, wglu))


def glu_fwd(ypre0, u, d, wglu, g_out, *, ts=512):
    S, W = u.shape
    ts = _tile(S, ts)

    def body(y0_ref, u_ref, d_ref, w_ref, g_ref, ypre_ref, z_ref, zn_ref):
        ypre = y0_ref[...] + d_ref[...] * u_ref[...]
        z = _glu_fn(ypre, w_ref[...])
        ypre_ref[...] = ypre
        z_ref[...] = z
        zn_ref[...] = _rms(z, g_ref[...], W).astype(bf16)

    row = pl.BlockSpec((ts, W), lambda i: (i, 0))
    vec = pl.BlockSpec((1, W), lambda i: (0, 0))
    full = jax.ShapeDtypeStruct((S, W), f32)
    return pl.pallas_call(
        body, name="glu_fwd", grid=(S // ts,),
        in_specs=[row, row, vec, pl.BlockSpec((W, W), lambda i: (0, 0)), vec], out_specs=[row, row, row],
        out_shape=[full, full, jax.ShapeDtypeStruct((S, W), bf16)], compiler_params=_cp(("parallel",)),
    )(ypre0, u, d, wglu, g_out)


def glu_bwd(ypre, u, d, wglu, dz, *, ts=512):
    S, W = u.shape
    ts = _tile(S, ts)

    def body(y_ref, u_ref, d_ref, w_ref, dz_ref, dy_ref, du_ref, dw_ref, dd_ref):
        _, vjp = jax.vjp(_glu_fn, y_ref[...], w_ref[...])
        dy, dw = vjp(dz_ref[...])
        dy_ref[...] = dy
        du_ref[...] = d_ref[...] * dy

        @pl.when(pl.program_id(0) == 0)
        def _():
            dw_ref[...] = jnp.zeros_like(dw_ref)
            dd_ref[...] = jnp.zeros_like(dd_ref)

        dw_ref[...] += dw
        dd_ref[...] += jnp.sum(dy * u_ref[...], axis=0, keepdims=True)

    row = pl.BlockSpec((ts, W), lambda i: (i, 0))
    vec = pl.BlockSpec((1, W), lambda i: (0, 0))
    sq = pl.BlockSpec((W, W), lambda i: (0, 0))
    full = jax.ShapeDtypeStruct((S, W), f32)
    return pl.pallas_call(
        body, name="glu_bwd", grid=(S // ts,), in_specs=[row, row, vec, sq, row], out_specs=[row, row, sq, vec],
        out_shape=[full, full, jax.ShapeDtypeStruct((W, W), f32), jax.ShapeDtypeStruct((1, W), f32)],
        compiler_params=_cp(("arbitrary",)),
    )(ypre, u, d, wglu, dz)


def loss_head(y, target, *, ts=512):
    S, D = y.shape
    ts = _tile(S, ts)

    def body(y_ref, t_ref, dy_ref, l_ref, dyb_ref):
        err = y_ref[...] - t_ref[...]
        dy_ref[...] = err * (1.0 / D)
        dyb_ref[...] = (err * (1.0 / D)).astype(bf16)

        @pl.when(pl.program_id(0) == 0)
        def _():
            l_ref[...] = jnp.zeros_like(l_ref)

        rows = jnp.sum(err * err, axis=1, keepdims=True) * (1.0 / D)
        l_ref[...] += 0.5 * jnp.sum(rows, axis=0, keepdims=True)

    row = pl.BlockSpec((ts, D), lambda i: (i, 0))
    return pl.pallas_call(
        body, name="loss_head", grid=(S // ts,), in_specs=[row, row],
        out_specs=[row, pl.BlockSpec((1, 1), lambda i: (0, 0)), row],
        out_shape=[jax.ShapeDtypeStruct((S, D), f32), jax.ShapeDtypeStruct((1, 1), f32),
                   jax.ShapeDtypeStruct((S, D), bf16)],
        compiler_params=_cp(("arbitrary",)),
    )(y, target)


def adamw(w, g, m, v, *, name, tr=256):
    R, C = w.shape
    tr = _row_tile(R, tr)

    def body(w_ref, g_ref, m_ref, v_ref, d_ref, nm_ref, nv_ref):
        gv = g_ref[...]
        nm = ADAM_B1 * m_ref[...] + (1.0 - ADAM_B1) * gv
        nv = ADAM_B2 * v_ref[...] + (1.0 - ADAM_B2) * jnp.square(gv)
        m_hat = nm / (1.0 - ADAM_B1 ** ADAM_STEP)
        v_hat = nv / (1.0 - ADAM_B2 ** ADAM_STEP)
        d_ref[...] = -ADAM_LR * (m_hat / (jnp.sqrt(v_hat) + ADAM_EPS) + ADAM_WD * w_ref[...])
        nm_ref[...] = nm
        nv_ref[...] = nv

    row = pl.BlockSpec((tr, C), lambda i: (i, 0))
    full = jax.ShapeDtypeStruct((R, C), f32)
    return pl.pallas_call(
        body, name=name, grid=(R // tr,), in_specs=[row] * 4, out_specs=[row] * 3, out_shape=[full] * 3,
        compiler_params=_cp(("parallel",)),
    )(w, g, m, v)


def add_half(g4, recv, c, *, name, tr=256):
    _, _, Rh, C = g4.shape
    tr = _row_tile(Rh, tr)

    def body(c_ref, a_ref, b_ref, o_ref):
        o_ref[...] = a_ref[...] + b_ref[...]

    grid_spec = pltpu.PrefetchScalarGridSpec(
        num_scalar_prefetch=1, grid=(N_CHIPS, Rh // tr),
        in_specs=[pl.BlockSpec((None, None, tr, C), lambda k, i, c_ref: (k, c_ref[0], i, 0)),
                  pl.BlockSpec((None, tr, C), lambda k, i, c_ref: (k, i, 0))],
        out_specs=pl.BlockSpec((None, tr, C), lambda k, i, c_ref: (k, i, 0)))
    return pl.pallas_call(body, name=name, grid_spec=grid_spec, out_shape=jax.ShapeDtypeStruct(recv.shape, f32),
                          compiler_params=_cp(("parallel", "parallel")))(c, g4, recv)


def sum_chips(p4, *, name, tr=256):
    _, Rh, C = p4.shape
    tr = _row_tile(Rh, tr)

    def body(a_ref, b_ref, c_ref, d_ref, o_ref):
        o_ref[...] = ((a_ref[...] + b_ref[...]) + c_ref[...]) + d_ref[...]

    spec = lambda k: pl.BlockSpec((None, tr, C), lambda i: (k, i, 0))
    return pl.pallas_call(
        body, name=name, grid=(Rh // tr,), in_specs=[spec(0), spec(1), spec(2), spec(3)],
        out_specs=pl.BlockSpec((tr, C), lambda i: (i, 0)), out_shape=jax.ShapeDtypeStruct((Rh, C), f32),
        compiler_params=_cp(("parallel",)),
    )(p4, p4, p4, p4)


def _place():
    return lax.axis_index("x"), lax.axis_index("y"), lax.axis_index("c")


def _other_chips(x, y):
    return [(1 - x, y), (x, 1 - y), (1 - x, 1 - y)]


def _chip_exchange(ins, outs, sems, scatter, start):
    if not ins:
        return
    send, recv, loc = sems
    x, y, c = _place()
    me = 2 * x + y
    for a in range(len(ins)):
        own = pltpu.make_async_copy(ins[a].at[me] if scatter else ins[a], outs[a].at[me], loc.at[a])
        own.start() if start else own.wait()
        for p, (px, py) in enumerate(_other_chips(x, y)):
            k = 2 * px + py
            cp = pltpu.make_async_remote_copy(
                src_ref=ins[a].at[k] if scatter else ins[a], dst_ref=outs[a].at[me if start else k],
                send_sem=send.at[3 * a + p], recv_sem=recv.at[3 * a + p], device_id=(px, py, c), device_id_type=MESH)
            cp.start() if start else cp.wait()


def _chip_exchange_args(arrs, scatter):
    n = len(arrs)
    shapes = [jax.ShapeDtypeStruct(a.shape if scatter else (N_CHIPS,) + a.shape, a.dtype) for a in arrs]
    sems = [pltpu.SemaphoreType.DMA((3 * n,)), pltpu.SemaphoreType.DMA((3 * n,)), pltpu.SemaphoreType.DMA((n,))]
    return shapes, sems if n else []


def _chip_exchange_call(arrs, scatter, name):
    n = len(arrs)

    def body(*refs):
        ins, outs, sems = refs[:n], refs[n:2 * n], refs[2 * n:]
        _chip_exchange(ins, outs, sems, scatter, True)
        _chip_exchange(ins, outs, sems, scatter, False)

    shapes, sems = _chip_exchange_args(arrs, scatter)
    return pl.pallas_call(
        body, name=name, in_specs=[ANY] * n, out_specs=[ANY] * n, out_shape=shapes, scratch_shapes=sems,
        compiler_params=pltpu.CompilerParams(has_side_effects=True),
    )(*arrs)


def allgather_chips(arrs, *, name):
    return _chip_exchange_call(arrs, False, name)


def sibling_swap(arrs, *, half, name):
    n = len(arrs)

    def body(*refs):
        ins, outs = refs[:n], refs[n:2 * n]
        send, recv = refs[2 * n:]
        x, y, c = _place()
        cps = []
        for a in range(n):
            src = ins[a].at[:, 1 - c] if half else ins[a]
            cp = pltpu.make_async_remote_copy(src_ref=src, dst_ref=outs[a], send_sem=send.at[a], recv_sem=recv.at[a],
                                              device_id=(x, y, 1 - c), device_id_type=MESH)
            cp.start()
            cps.append(cp)
        for cp in cps:
            cp.wait()

    def oshape(a):
        return jax.ShapeDtypeStruct((a.shape[0],) + a.shape[2:] if half else a.shape, a.dtype)

    return pl.pallas_call(
        body, name=name, in_specs=[ANY] * n, out_specs=[ANY] * n, out_shape=[oshape(a) for a in arrs],
        scratch_shapes=[pltpu.SemaphoreType.DMA((n,)), pltpu.SemaphoreType.DMA((n,))],
        compiler_params=pltpu.CompilerParams(has_side_effects=True),
    )(*arrs)


def chip_scatter(arrs, *, name):
    return _chip_exchange_call(arrs, True, name)


def _pad_cols(w):
    K = w.shape[0]
    w = w.reshape(K, -1, SB_HEAD_DIM)
    return jnp.pad(w, ((0, 0), (0, 0), (0, LANES - SB_HEAD_DIM))).reshape(K, -1)


def _unpad_cols(w):
    K = w.shape[0]
    return w.reshape(K, -1, LANES)[:, :, :SB_HEAD_DIM].reshape(K, -1)


def _pad_rows(w):
    N = w.shape[1]
    w = w.reshape(-1, SB_HEAD_DIM, N)
    return jnp.pad(w, ((0, 0), (0, LANES - SB_HEAD_DIM), (0, 0))).reshape(-1, N)


def _unpad_rows(w):
    N = w.shape[1]
    return w.reshape(-1, LANES, N)[:, :SB_HEAD_DIM, :].reshape(-1, N)


_PACK_ROWS = N_CHIPS * 2 * SUBLANES


def _pack(arrs):
    flat = jnp.concatenate([a.reshape(-1) for a in arrs])
    rows = -(-flat.shape[0] // LANES)
    rows = -(-rows // _PACK_ROWS) * _PACK_ROWS
    return jnp.pad(flat, (0, rows * LANES - flat.shape[0])).reshape(rows, LANES)


def _unpack(buf, shapes):
    flat = buf.reshape(-1)
    out, pos = [], 0
    for shp in shapes:
        size = 1
        for d in shp:
            size *= d
        out.append(flat[pos:pos + size].reshape(shp))
        pos += size
    return out


BIG = ("w_in", "ssm_w_glu", "w_out", "xa_w_q", "xa_w_kv", "xa_w_o", "w_up", "w_down")
SMALL = ("g_mix", "ssm_a_re", "ssm_a_im", "ssm_log_dt", "ssm_b_re", "ssm_b_im", "ssm_c_re", "ssm_c_im", "ssm_d",
         "sb_g_q", "sb_g_k", "g_out_ssm", "g_out_sb", "g_xa", "g_mem", "xa_g_q", "xa_g_k", "g_mlp")
WEIGHTS = ("g_mix", "w_in", "ssm_a_re", "ssm_a_im", "ssm_log_dt", "ssm_b_re", "ssm_b_im", "ssm_c_re", "ssm_c_im",
           "ssm_d", "ssm_w_glu", "sb_g_q", "sb_g_k", "g_out_ssm", "g_out_sb", "w_out", "g_xa", "g_mem", "xa_w_q",
           "xa_w_kv", "xa_g_q", "xa_g_k", "xa_w_o", "g_mlp", "w_up", "w_down")


def kernel(x, mem, g_mix, w_in, ssm_a_re, ssm_a_im, ssm_log_dt, ssm_b_re, ssm_b_im, ssm_c_re, ssm_c_im, ssm_d, ssm_w_glu, sb_g_q, sb_g_k, g_out_ssm, g_out_sb, w_out, g_xa, g_mem, xa_w_q, xa_w_kv, xa_g_q, xa_g_k, xa_w_o, g_mlp, w_up, w_down, loss_target, m_g_mix, m_w_in, m_ssm_a_re, m_ssm_a_im, m_ssm_log_dt, m_ssm_b_re, m_ssm_b_im, m_ssm_c_re, m_ssm_c_im, m_ssm_d, m_ssm_w_glu, m_sb_g_q, m_sb_g_k, m_g_out_ssm, m_g_out_sb, m_w_out, m_g_xa, m_g_mem, m_xa_w_q, m_xa_w_kv, m_xa_g_q, m_xa_g_k, m_xa_w_o, m_g_mlp, m_w_up, m_w_down, v_g_mix, v_w_in, v_ssm_a_re, v_ssm_a_im, v_ssm_log_dt, v_ssm_b_re, v_ssm_b_im, v_ssm_c_re, v_ssm_c_im, v_ssm_d, v_ssm_w_glu, v_sb_g_q, v_sb_g_k, v_g_out_ssm, v_g_out_sb, v_w_out, v_g_xa, v_g_mem, v_xa_w_q, v_xa_w_kv, v_xa_g_q, v_xa_g_k, v_xa_w_o, v_g_mlp, v_w_up, v_w_down):
    env = dict(locals())
    W = {n: env[n] for n in WEIGHTS}
    M1 = {n: env["m_" + n] for n in WEIGHTS}
    V2 = {n: env["v_" + n] for n in WEIGHTS}
    xs, mems, tgt = x[0], mem[0], loss_target[0]
    S, D = xs.shape
    G, P, C = SSM_GROUPS, SSM_STATE, SSM_GROUP
    GP = G * P
    SBW = SB_HEADS * SB_HEAD_DIM
    c_idx = lax.axis_index("c")

    (g_in,) = allgather_chips([w_in[0].astype(bf16)], name="gather_w_in")
    Wu = g_in[0]
    Wqkv = jnp.concatenate([_pad_cols(g_in[1]), _pad_cols(g_in[2]), _pad_cols(g_in[3])], axis=1)
    gq_pad, gk_pad = _pad_cols(sb_g_q), _pad_cols(sb_g_k)
    gosb_pad = _pad_cols(g_out_sb)
    a_re, a_im = ssm_a_re.reshape(1, GP), ssm_a_im.reshape(1, GP)
    bT_re = ssm_b_re[0].transpose(2, 0, 1).reshape(C, GP)
    bT_im = ssm_b_im[0].transpose(2, 0, 1).reshape(C, GP)
    cT_re = ssm_c_re[0].transpose(1, 0, 2).reshape(C, GP)
    cT_im = ssm_c_im[0].transpose(1, 0, 2).reshape(C, GP)
    s5_in = (a_re, a_im, ssm_log_dt, bT_re, bT_im, cT_re, cT_im)

    big = dict(tn=1024, tk=1024)
    wide = dict(tm=1024, tn=1024, tk=2048)
    h0 = rms_norm(xs, g_mix, D, name="norm_x")
    u = mm(h0, Wu, mode="nn", name="proj_u", tk=1024)
    shard = {n: W[n][0].astype(bf16) for n in BIG[1:]}
    qkv, (g_glu, g_out, g_xq, g_xkv, g_xo) = mm(
        h0, Wqkv, mode="nn", name="proj_qkv", tm=1024,
        exchange=([shard[n] for n in ("ssm_w_glu", "w_out", "xa_w_q", "xa_w_kv", "xa_w_o")], False), **big)
    qn, kn, vb = qkv_prep(qkv, gq_pad, gk_pad)
    o = sb_fwd(qn, kn, vb)
    Wglu = g_glu.reshape(-1, g_glu.shape[-1])
    Wout = g_out.reshape(-1, g_out.shape[-1])
    Wo_ssm, Wo_sb = Wout[:SBW], _pad_rows(Wout[SBW:])
    Wxq = g_xq.reshape(-1, g_xq.shape[-1])
    Wxkv = g_xkv.reshape(-1, g_xkv.shape[-1])
    Wxo = g_xo.transpose(1, 0, 2).reshape(g_xo.shape[1], -1)
    abr, abi, Bm, Cm = s5_prep(*s5_in)
    n_pos = S // SCAN_CHUNKS
    u_il = row_shuffle(u, n_pos, SCAN_CHUNKS, name="u_interleave", out_dtype=bf16)
    st, (g_up,) = scan_fwd(u_il, Bm, abr, abi, ([shard["w_up"]], False))
    ypre0_il, (g_down,) = mm(st, Cm, mode="nt", name="s5_y", a_shards=2, b_shards=2, tm=1024, tk=2048,
                             exchange=([shard["w_down"]], False))
    Wup = g_up.transpose(1, 0, 2).reshape(g_up.shape[1], -1)
    Wdown = g_down.reshape(-1, g_down.shape[-1])
    ypre0 = row_shuffle(ypre0_il, SCAN_CHUNKS, n_pos, name="y_token_order")
    ypre, z, zn = glu_fwd(ypre0, u, ssm_d, Wglu, g_out_ssm)
    on = rms_norm(o, gosb_pad, SBW, name="norm_o")
    x1a = mm(zn, Wo_ssm, mode="nn", name="out_ssm", epi="add", aux=xs, tn=1024)
    x1 = mm(on, Wo_sb, mode="nn", name="out_sb", epi="add", aux=x1a, **big)
    h1 = rms_norm(x1, g_xa, D, name="norm_x1")
    qx = mm(h1, Wxq, mode="nn", name="xa_q", tk=1024)
    memn = rms_norm(mems, g_mem, D, name="norm_mem")
    kv = mm(memn, Wxkv, mode="nn", name="xa_kv", **big)
    ox = xa_fwd(qx, kv, xa_g_q, xa_g_k)
    x2 = mm(ox, Wxo, mode="nn", name="xa_o", epi="add", aux=x1, tn=1024)
    h2 = rms_norm(x2, g_mlp, D, name="norm_x2")
    act = mm(h2, Wup, mode="nn", name="mlp_up", out_dtype=bf16, tm=1024, tn=2048, tk=1024)
    x3 = mm(act, Wdown, mode="nn", name="mlp_down", pro="relu2", epi="add", aux=x2, **wide)
    dx3, loss_part, dx3b = loss_head(x3, tgt)
    loss = lax.psum(loss_part[0, 0], ("x", "y", "c"))

    dact = mm(dx3b, Wdown, mode="nt", name="d_act", epi="mul2relu", aux=act, out_dtype=bf16, tm=1024, tn=2048,
              tk=1024)
    dWdown = mm(act, dx3b, mode="tn", name="dw_down", pro="relu2", **wide)
    dWup = mm(h2, dact, mode="tn", name="dw_up", out_shards=N_CHIPS, **wide)
    dh2 = mm(dact, Wup, mode="nt", name="d_h2", **wide)
    dx2, dg_mlp, dx2b = rms_bwd(x2, g_mlp, dh2, dx3, D, name="rms_bwd_mlp", twin=True)
    dox = mm(dx2b, Wxo, mode="nt", name="d_ox", out_dtype=bf16, tk=1024)
    dWxo = mm(ox, dx2b, mode="tn", name="dw_xo", out_shards=N_CHIPS, tk=1024)
    dqx, dkv, dg_xq, dg_xk = xa_bwd(qx, kv, xa_g_q, xa_g_k, dox)
    dWxq = mm(h1, dqx, mode="tn", name="dw_xq", tm=1024, tk=1024)
    dh1 = mm(dqx, Wxq, mode="nt", name="d_h1", tn=1024)
    dx1, dg_xa, dx1b = rms_bwd(x1, g_xa, dh1, dx2, D, name="rms_bwd_xa", twin=True)
    dWxkv = mm(memn, dkv, mode="tn", name="dw_xkv", tm=1024, tn=1024)
    dmemn = mm(dkv, Wxkv, mode="nt", name="d_memn", **big)
    _, dg_mem = rms_bwd(mems, g_mem, dmemn, None, D, name="rms_bwd_mem")
    dyn_ssm = mm(dx1b, Wo_ssm, mode="nt", name="d_yn_ssm", tk=1024)
    dyn_sb = mm(dx1b, Wo_sb, mode="nt", name="d_yn_sb", **big)
    dWo_ssm = mm(zn, dx1b, mode="tn", name="dw_out_ssm", **big)
    dWo_sb = mm(on, dx1b, mode="tn", name="dw_out_sb", tm=1024, **big)
    dz, dg_os = rms_bwd(z, g_out_ssm, dyn_ssm, None, SBW, name="rms_bwd_ssm")
    do, dg_osb = rms_bwd(o, gosb_pad, dyn_sb, None, SBW, name="rms_bwd_sb")
    c_arr = c_idx.astype(jnp.int32).reshape(1)

    def sibling_sums(grads, names, tag):
        g4 = [g.reshape(N_CHIPS, 2, g.shape[1] // 2, g.shape[2]) for g in grads]
        from_sib = sibling_swap(g4, half=True, name="grad_to_sibling_" + tag)
        return [add_half(a, b, c_arr, name="add_sibling_" + n) for a, b, n in zip(g4, from_sib, names)]

    early = ("xa_w_q", "xa_w_kv", "xa_w_o", "w_up", "w_down")
    early_g = [dWxq.reshape(N_CHIPS, -1, dWxq.shape[1]), dWxkv.reshape(N_CHIPS, -1, dWxkv.shape[1]), dWxo, dWup,
               dWdown.reshape(N_CHIPS, -1, D)]
    pair = sibling_sums(early_g, early, "early")
    dqn, dkn, dv = sb_bwd(qn, kn, vb, do)
    dqkv, dg_q, dg_k, parts_xa = qkv_bwd(qkv, gq_pad, gk_pad, dqn, dkn, dv, (pair[:3], True))
    dypre, du_skip, dWglu, dd = glu_bwd(ypre, u, ssm_d, Wglu, dz)
    dypre_il = row_shuffle(dypre, n_pos, SCAN_CHUNKS, name="dy_interleave", out_dtype=bf16)
    dCm = mm(dypre_il, st, mode="tn", name="d_cmat", b_shards=2, out_shards=2, **wide)
    gst, dabr, dabi, parts_down = scan_bwd(dypre_il, Cm, st, abr, abi, (pair[4:], True))
    dBm = mm(u_il, gst, mode="tn", name="d_bmat", b_shards=2, out_shards=2, **wide)
    du_il, parts_up = mm(gst, Bm, mode="nt", name="d_u", a_shards=2, b_shards=2, tm=1024, tk=2048,
                         exchange=(pair[3:4], True))
    mine = {n: sum_chips(p, name="sum_chips_" + n) for n, p in zip(early, [*parts_xa, *parts_up, *parts_down])}
    du = row_shuffle(du_il, SCAN_CHUNKS, n_pos, name="du_token_order", add=du_skip, out_dtype=bf16)
    s5_g = s5_prep_bwd(*s5_in, dabr, dabi, dBm, dCm)
    dWu = mm(h0, du, mode="tn", name="dw_u", tm=1024, tk=1024)
    dWqkv = mm(h0, dqkv, mode="tn", name="dw_qkv", **wide)
    dh0a = mm(du, Wu, mode="nt", name="d_h0_u", tn=1024)
    dh0 = mm(dqkv, Wqkv, mode="nt", name="d_h0_qkv", epi="add", aux=dh0a, tm=1024, **big)
    dx, dg_mix = rms_bwd(xs, g_mix, dh0, dx1, D, name="rms_bwd_mix")

    HW = SB_HEADS * LANES
    late = ("w_in", "ssm_w_glu", "w_out", "small")
    late_g = [jnp.stack([dWu, _unpad_cols(dWqkv[:, :HW]), _unpad_cols(dWqkv[:, HW:2 * HW]),
                         _unpad_cols(dWqkv[:, 2 * HW:])]),
              dWglu.reshape(N_CHIPS, -1, dWglu.shape[1]),
              jnp.concatenate([dWo_ssm, _unpad_rows(dWo_sb)]).reshape(N_CHIPS, -1, D)]
    da_re, da_im, dldt, dbT_re, dbT_im, dcT_re, dcT_im = s5_g
    small_g = {
        "g_mix": dg_mix, "ssm_a_re": da_re, "ssm_a_im": da_im, "ssm_log_dt": dldt,
        "ssm_b_re": dbT_re.reshape(C, G, P).transpose(1, 2, 0), "ssm_b_im": dbT_im.reshape(C, G, P).transpose(1, 2, 0),
        "ssm_c_re": dcT_re.reshape(C, G, P).transpose(1, 0, 2), "ssm_c_im": dcT_im.reshape(C, G, P).transpose(1, 0, 2),
        "ssm_d": dd, "sb_g_q": dg_q[:, :SB_HEAD_DIM], "sb_g_k": dg_k[:, :SB_HEAD_DIM], "g_out_ssm": dg_os,
        "g_out_sb": _unpad_cols(dg_osb), "g_xa": dg_xa, "g_mem": dg_mem, "xa_g_q": dg_xq, "xa_g_k": dg_xk,
        "g_mlp": dg_mlp,
    }
    late_g.append(_pack([small_g[n] for n in SMALL]).reshape(N_CHIPS, -1, LANES))

    parts_late = chip_scatter(sibling_sums(late_g, late, "late"), name="grad_to_chips_late")
    mine.update({n: sum_chips(p, name="sum_chips_" + n) for n, p in zip(late, parts_late)})
    mine = [mine[n] for n in list(BIG) + ["small"]]
    other = sibling_swap(mine, half=False, name="grad_half_to_sibling")
    shard = [jnp.where(c_idx == 0, jnp.concatenate([a, b]), jnp.concatenate([b, a])) for a, b in zip(mine, other)]
    small_all = allgather_chips([shard[-1]], name="gather_small")[0]
    small_red = small_all.reshape(-1, LANES)

    out = {}
    for n, gs in zip(BIG, shard[:-1]):
        shp = W[n].shape
        w2, m2, v2 = (t.reshape(gs.shape) for t in (W[n], M1[n], V2[n]))
        d, nm, nv = adamw(w2, gs, m2, v2, name="adamw_" + n)
        out[n] = tuple(t.reshape(shp) for t in (gs, d, nm, nv))
    shapes = [W[n].shape for n in SMALL]
    d, nm, nv = adamw(_pack([W[n] for n in SMALL]), small_red, _pack([M1[n] for n in SMALL]),
                      _pack([V2[n] for n in SMALL]), name="adamw_small")
    for n, gs, dd_, mm_, vv_ in zip(SMALL, _unpack(small_red, shapes), _unpack(d, shapes), _unpack(nm, shapes),
                                    _unpack(nv, shapes)):
        out[n] = (gs, dd_, mm_, vv_)
    res = [loss, dx[None]]
    for kind in range(4):
        res += [out[n][kind] for n in WEIGHTS]
    return tuple(res)
```

```python
import jax
import jax.numpy as jnp
from jax import lax
from jax.experimental import pallas as pl
from jax.experimental.pallas import tpu as pltpu

f32 = jnp.float32
bf16 = jnp.bfloat16

NORM_EPS = 1e-6
SSM_GROUPS = 32
SSM_GROUP = 16
SSM_STATE = 64
SB_HEADS = 8
SB_HEAD_DIM = 64
XA_HEADS = 4
XA_HEAD_DIM = 128
LANES = 128
SUBLANES = 8
N_CHIPS = 4
ADAM_LR = 0.001
ADAM_B1 = 0.9
ADAM_B2 = 0.999
ADAM_EPS = 1e-08
ADAM_WD = 0.01
ADAM_STEP = 10
VMEM_LIMIT = 56 * 1024 * 1024
MESH = pl.DeviceIdType.MESH
ANY = pl.BlockSpec(memory_space=pl.ANY)


def _cp(sem=None):
    return pltpu.CompilerParams(dimension_semantics=sem, vmem_limit_bytes=VMEM_LIMIT)


def _tile(n, pref):
    if n <= pref:
        return n
    t = (pref // LANES) * LANES
    while t > LANES and n % t:
        t -= LANES
    assert n % t == 0, (n, pref)
    return t


def _row_tile(n, pref):
    if n <= pref:
        return n
    t = (pref // SUBLANES) * SUBLANES
    while n % t:
        t -= SUBLANES
    return t


def _dot(a, b, dims):
    return lax.dot_general(a.astype(bf16), b.astype(bf16), (dims, ((), ())), preferred_element_type=f32)


_NN = ((1,), (0,))
_NT = ((1,), (1,))
_TN = ((0,), (0,))


@jax.custom_vjp
def bdot_nn(a, b):
    return _dot(a, b, _NN)


def _bdot_nn_fwd(a, b):
    return _dot(a, b, _NN), (a, b)


def _bdot_nn_bwd(res, g):
    a, b = res
    return _dot(g, b, _NT), _dot(a, g, _TN)


bdot_nn.defvjp(_bdot_nn_fwd, _bdot_nn_bwd)


@jax.custom_vjp
def bdot_nt(a, b):
    return _dot(a, b, _NT)


def _bdot_nt_fwd(a, b):
    return _dot(a, b, _NT), (a, b)


def _bdot_nt_bwd(res, g):
    a, b = res
    return _dot(g, b, _NN), _dot(g, a, _TN)


bdot_nt.defvjp(_bdot_nt_fwd, _bdot_nt_bwd)


def _rms(x, g, denom):
    r = lax.rsqrt(jnp.sum(x * x, axis=-1, keepdims=True) * (1.0 / denom) + NORM_EPS)
    return x * r * g


def _opspec(block, row_of, col_of, shards, ncol_tiles):
    if shards == 1:
        return pl.BlockSpec(block, lambda i, j, k: (row_of(i, j, k), col_of(i, j, k)))
    per = ncol_tiles // shards
    return pl.BlockSpec((None,) + block,
                        lambda i, j, k: (col_of(i, j, k) // per, row_of(i, j, k), col_of(i, j, k) % per))


def _call_with_exchange(body, args, exchange, *, name, grid, in_specs, out_specs, out_shape, scratch_shapes=()):
    xs, scatter = exchange
    n, n_in, n_out, n_scr = len(xs), len(in_specs), len(out_specs), len(scratch_shapes)
    x_shapes, sems = _chip_exchange_args(xs, scatter)

    def wrapped(*refs):
        ins, x_ins = refs[:n_in], refs[n_in:n_in + n]
        outs, x_outs = refs[n_in + n:n_in + n + n_out], refs[n_in + n + n_out:n_in + 2 * n + n_out]
        scratch, x_sems = refs[n_in + 2 * n + n_out:n_in + 2 * n + n_out + n_scr], refs[n_in + 2 * n + n_out + n_scr:]
        first, last = True, True
        for d, steps in enumerate(grid):
            first = first & (pl.program_id(d) == 0)
            last = last & (pl.program_id(d) == steps - 1)

        @pl.when(first)
        def _():
            _chip_exchange(x_ins, x_outs, x_sems, scatter, True)

        body(*ins, *outs, *scratch)

        @pl.when(last)
        def _():
            _chip_exchange(x_ins, x_outs, x_sems, scatter, False)

    res = pl.pallas_call(
        wrapped, name=name, grid=grid, in_specs=list(in_specs) + [ANY] * n, out_specs=list(out_specs) + [ANY] * n,
        out_shape=list(out_shape) + x_shapes, scratch_shapes=list(scratch_shapes) + sems,
        compiler_params=_cp(("arbitrary",) * len(grid)),
    )(*args, *xs)
    return res[:n_out], res[n_out:]


def mm(a, b, *, mode, name, tm=512, tn=512, tk=512, pro="none", epi="none", aux=None,
       out_dtype=f32, a_shards=1, b_shards=1, out_shards=1, exchange=None):
    ar, ac = a.shape[-2], a.shape[-1] * a_shards
    br, bc = b.shape[-2], b.shape[-1] * b_shards
    if mode == "nn":
        M, K, N = ar, ac, bc
        assert br == K
    elif mode == "nt":
        M, K, N = ar, ac, br
        assert bc == K
    else:
        M, K, N = ac, ar, bc
        assert br == K
    tm, tn, tk = _tile(M, tm), _tile(N, tn), _tile(K, tk)
    if a_shards > 1:
        if mode == "tn":
            tm = _tile(ac // a_shards, tm)
        else:
            tk = _tile(ac // a_shards, tk)
    if b_shards > 1:
        if mode == "nt":
            tk = _tile(bc // b_shards, tk)
        else:
            tn = _tile(bc // b_shards, tn)
    if out_shards > 1:
        tn = _tile(N // out_shards, tn)
    nm, nn_, nk = M // tm, N // tn, K // tk
    I = lambda i, j, k: i
    J = lambda i, j, k: j
    Kk = lambda i, j, k: k
    if mode == "nn":
        a_spec = _opspec((tm, tk), I, Kk, a_shards, nk)
        b_spec = _opspec((tk, tn), Kk, J, b_shards, nn_)
        dims = _NN
    elif mode == "nt":
        a_spec = _opspec((tm, tk), I, Kk, a_shards, nk)
        b_spec = _opspec((tn, tk), J, Kk, b_shards, nk)
        dims = _NT
    else:
        a_spec = _opspec((tk, tm), Kk, I, a_shards, nm)
        b_spec = _opspec((tk, tn), Kk, J, b_shards, nn_)
        dims = _TN
    in_specs = [a_spec, b_spec]
    args = [a, b]
    if epi != "none":
        in_specs.append(pl.BlockSpec((tm, tn), lambda i, j, k: (i, j)))
        args.append(aux)
    if out_shards == 1:
        out_spec = pl.BlockSpec((tm, tn), lambda i, j, k: (i, j))
        out_shape = jax.ShapeDtypeStruct((M, N), out_dtype)
    else:
        per = nn_ // out_shards
        out_spec = pl.BlockSpec((None, tm, tn), lambda i, j, k: (j // per, i, j % per))
        out_shape = jax.ShapeDtypeStruct((out_shards, M, N // out_shards), out_dtype)

    def body(*refs):
        a_ref, b_ref = refs[0], refs[1]
        pos = 2
        if epi != "none":
            aux_ref = refs[pos]
            pos += 1
        o_ref, acc_ref = refs[pos], refs[pos + 1]
        k = pl.program_id(2)

        @pl.when(k == 0)
        def _():
            acc_ref[...] = jnp.zeros_like(acc_ref)

        av = a_ref[...]
        if pro == "relu2":
            av = jnp.square(jnp.maximum(av.astype(f32), 0.0))
        acc_ref[...] += _dot(av, b_ref[...], dims)

        @pl.when(k == nk - 1)
        def _():
            res = acc_ref[...]
            if epi == "add":
                res = res + aux_ref[...].astype(f32)
            elif epi == "mul2relu":
                res = res * (2.0 * jnp.maximum(aux_ref[...].astype(f32), 0.0))
            o_ref[...] = res.astype(out_dtype)

    acc = [pltpu.VMEM((tm, tn), f32)]
    if exchange is not None:
        (out,), moved = _call_with_exchange(body, args, exchange, name=name, grid=(nm, nn_, nk), in_specs=in_specs,
                                            out_specs=[out_spec], out_shape=[out_shape], scratch_shapes=acc)
        return out, moved
    return pl.pallas_call(
        body, name=name, grid=(nm, nn_, nk), in_specs=in_specs, out_specs=out_spec, out_shape=out_shape,
        scratch_shapes=acc, compiler_params=_cp(("parallel", "parallel", "arbitrary")),
    )(*args)


def rms_norm(x, g, denom, *, name, ts=512):
    S, D = x.shape
    ts = _tile(S, ts)

    def body(x_ref, g_ref, h_ref):
        h_ref[...] = _rms(x_ref[...], g_ref[...], denom).astype(bf16)

    row = pl.BlockSpec((ts, D), lambda i: (i, 0))
    return pl.pallas_call(
        body, name=name, grid=(S // ts,), in_specs=[row, pl.BlockSpec((1, D), lambda i: (0, 0))], out_specs=row,
        out_shape=jax.ShapeDtypeStruct((S, D), bf16), compiler_params=_cp(("parallel",)),
    )(x, g)


def rms_bwd(x, g, dy, res, denom, *, name, ts=256, twin=False):
    S, D = x.shape
    ts = _tile(S, ts)
    has_res = res is not None

    def body(*refs):
        x_ref, g_ref, dy_ref = refs[:3]
        outs = refs[4:] if has_res else refs[3:]
        _, vjp = jax.vjp(lambda xv, gv: _rms(xv, gv, denom), x_ref[...], g_ref[...])
        dx, dg = vjp(dy_ref[...])
        if has_res:
            dx = dx + refs[3][...]
        outs[0][...] = dx
        if twin:
            outs[2][...] = dx.astype(bf16)
        dg_ref = outs[1]

        @pl.when(pl.program_id(0) == 0)
        def _():
            dg_ref[...] = jnp.zeros_like(dg_ref)

        dg_ref[...] += dg

    row = pl.BlockSpec((ts, D), lambda i: (i, 0))
    vec = pl.BlockSpec((1, D), lambda i: (0, 0))
    in_specs = [row, vec, row] + ([row] if has_res else [])
    args = [x, g, dy] + ([res] if has_res else [])
    return pl.pallas_call(
        body, name=name, grid=(S // ts,), in_specs=in_specs, out_specs=[row, vec] + ([row] if twin else []),
        out_shape=[jax.ShapeDtypeStruct((S, D), f32), jax.ShapeDtypeStruct((1, D), f32)]
        + ([jax.ShapeDtypeStruct((S, D), bf16)] if twin else []),
        compiler_params=_cp(("arbitrary",)),
    )(*args)


LOG2E = 1.4426950408889634
LN2 = 0.6931471805599453


def _qk_fn(q, k, gq, gk):
    qs, ks = [], []
    for h in range(SB_HEADS):
        sl = slice(h * LANES, (h + 1) * LANES)
        qs.append(_rms(q[:, sl], gq, SB_HEAD_DIM) * (SB_HEAD_DIM ** -0.5 * LOG2E))
        ks.append(_rms(k[:, sl], gk, SB_HEAD_DIM))
    return jnp.concatenate(qs, axis=1), jnp.concatenate(ks, axis=1)


def qkv_prep(qkv, gq, gk, *, ts=256):
    S = qkv.shape[0]
    W = SB_HEADS * LANES
    ts = _tile(S, ts)

    def body(q_ref, k_ref, v_ref, gq_ref, gk_ref, qn_ref, kn_ref, vb_ref):
        qn, kn = _qk_fn(q_ref[...], k_ref[...], gq_ref[...], gk_ref[...])
        qn_ref[...] = qn.astype(bf16)
        kn_ref[...] = kn.astype(bf16)
        vb_ref[...] = v_ref[...].astype(bf16)

    out = jax.ShapeDtypeStruct((S, W), bf16)
    gspec = pl.BlockSpec((1, LANES), lambda i: (0, 0))
    ospec = pl.BlockSpec((ts, W), lambda i: (i, 0))
    col = lambda c: pl.BlockSpec((ts, W), lambda i: (i, c))
    return pl.pallas_call(
        body, name="qkv_prep", grid=(S // ts,), in_specs=[col(0), col(1), col(2), gspec, gspec],
        out_specs=[ospec, ospec, ospec], out_shape=[out, out, out], compiler_params=_cp(("parallel",)),
    )(qkv, qkv, qkv, gq, gk)


def qkv_bwd(qkv, gq, gk, dqn, dkn, dv, exchange, *, ts=256):
    S = qkv.shape[0]
    W = SB_HEADS * LANES
    ts = _tile(S, ts)

    def body(q_ref, k_ref, gq_ref, gk_ref, dqn_ref, dkn_ref, dv_ref, o_ref, dgq_ref, dgk_ref):
        _, vjp = jax.vjp(_qk_fn, q_ref[...], k_ref[...], gq_ref[...], gk_ref[...])
        dq, dk, dgq, dgk = vjp((dqn_ref[...] * LN2, dkn_ref[...] * LN2))
        o_ref[:, 0:W] = dq.astype(bf16)
        o_ref[:, W:2 * W] = dk.astype(bf16)
        o_ref[:, 2 * W:3 * W] = dv_ref[...].astype(bf16)

        @pl.when(pl.program_id(0) == 0)
        def _():
            dgq_ref[...] = jnp.zeros_like(dgq_ref)
            dgk_ref[...] = jnp.zeros_like(dgk_ref)

        dgq_ref[...] += dgq
        dgk_ref[...] += dgk

    gspec = pl.BlockSpec((1, LANES), lambda i: (0, 0))
    row = pl.BlockSpec((ts, W), lambda i: (i, 0))
    col = lambda c: pl.BlockSpec((ts, W), lambda i: (i, c))
    (dqkv, dgq, dgk), moved = _call_with_exchange(
        body, (qkv, qkv, gq, gk, dqn, dkn, dv), exchange, name="qkv_bwd", grid=(S // ts,),
        in_specs=[col(0), col(1), gspec, gspec, row, row, row],
        out_specs=[pl.BlockSpec((ts, 3 * W), lambda i: (i, 0)), gspec, gspec],
        out_shape=[jax.ShapeDtypeStruct((S, 3 * W), bf16), jax.ShapeDtypeStruct((1, LANES), f32),
                   jax.ShapeDtypeStruct((1, LANES), f32)])
    return dqkv, dgq, dgk, moved


def _sb_weights(q, ks, R, masked, row, col, UU):
    ls = [_dot(q, k, _NT) for k in ks]
    lbs, lm0s, cats = [], [], []
    for l, diag in zip(ls, masked):
        neg_abs = pltpu.bitcast(pltpu.bitcast(l, jnp.uint32) | jnp.uint32(0x80000000), f32)
        lp = jnp.log2(1.0 + jnp.exp2(neg_abs))
        lb = jnp.minimum(l, 0.0) - lp
        lm = lb - l
        if diag:
            lm = jnp.where(col < row, lm, 0.0)
        hi = lm.astype(bf16)
        lo = (lm - hi.astype(f32)).astype(bf16)
        lbs.append(lb)
        lm0s.append(lm[:, 0:1])
        cats.append(jnp.concatenate([hi, lo], axis=1))
    sums = [_dot(c, UU, _NN) for c in cats]
    ws = []
    for lb, lm0, A, diag in zip(lbs, lm0s, sums, masked):
        w = jnp.exp2(lb + (A + R))
        if diag:
            w = jnp.where(col < row, w, 0.0)
        R = R + (A[:, 0:1] + lm0)
        ws.append(w)
    return lbs, ws, R


def _tri2(tk):
    r = lax.broadcasted_iota(jnp.int32, (2 * tk, tk), 0)
    r = jnp.where(r >= tk, r - tk, r)
    c = lax.broadcasted_iota(jnp.int32, (2 * tk, tk), 1)
    return (r > c).astype(bf16)


SB_GROUP = 8


SB_ALL_ZERO_BELOW = -160.0


def _sweep(i, blocks_of, carry, descending, right_sum=None, ran=None):
    G = SB_GROUP
    n = jnp.maximum(i - 1, 0)
    rem, full = n % G, n // G
    asc = lambda js: js if descending else js[::-1]

    def first_group(c):
        one = lambda c: blocks_of([i], c, [True])
        two = lambda c: blocks_of(asc([i, i - 1]), c, asc([True, False]))
        return lax.cond(i >= 1, two, one, c)

    def body(p, c):
        return blocks_of(asc([i - 2 - p * G - u for u in range(G)]), c, [False] * G)

    def left_over(r):
        return lambda c: blocks_of(asc([r - 1 - u for u in range(r)]), c, [False] * r) if r else c

    if descending:
        carry = first_group(carry)
        alive = lambda c: jnp.max(right_sum(c)) > SB_ALL_ZERO_BELOW
        bodies, carry = lax.while_loop(lambda s: (s[0] < full) & alive(s[1]),
                                       lambda s: (s[0] + 1, body(s[0], s[1])), (jnp.int32(0), carry))
        tail = (bodies == full) & alive(carry)
        carry = lax.switch(jnp.where(tail, rem, 0), [left_over(r) for r in range(G)], carry)
        return carry, (bodies, tail)
    bodies, tail = ran
    carry = lax.switch(jnp.where(tail, rem, 0), [left_over(r) for r in range(G)], carry)
    carry = lax.fori_loop(0, bodies, lambda t, c: body(bodies - 1 - t, c), carry)
    return first_group(carry)


def sb_fwd(qn, kn, vb, exchange, *, tq=256):
    S, W = qn.shape
    H = W // LANES
    tq = _tile(S, tq)
    tk = tq
    nq = S // tq

    def body(q_ref, k_ref, v_ref, o_ref, uu_s):
        i = pl.program_id(1)

        @pl.when((pl.program_id(0) == 0) & (i == 0))
        def _():
            uu_s[...] = _tri2(tk)

        q = q_ref[...]
        row = lax.broadcasted_iota(jnp.int32, (tq, tk), 0)
        col = lax.broadcasted_iota(jnp.int32, (tq, tk), 1)
        UU = uu_s[...]

        def blocks(js, c, masked):
            rows = [pl.ds(pl.multiple_of(j * tk, tk), tk) for j in js]
            _, ws, R = _sb_weights(q, [k_ref[r, :] for r in rows], c[0], masked, row, col, UU)
            acc = c[1]
            for w, r in zip(ws, rows):
                acc = acc + _dot(w, v_ref[r, :], _NN)
            return R, acc

        c, _ = _sweep(i, blocks, (jnp.zeros((tq, 1), f32), jnp.zeros((tq, LANES), f32)), True, lambda c: c[0])
        o_ref[...] = c[1]

    qspec = pl.BlockSpec((tq, LANES), lambda h, i: (i, h))
    kspec = pl.BlockSpec((S, LANES), lambda h, i: (0, h))
    (o,), moved = _call_with_exchange(
        body, (qn, kn, vb), exchange, name="sb_fwd", grid=(H, nq), in_specs=[qspec, kspec, kspec], out_specs=[qspec],
        out_shape=[jax.ShapeDtypeStruct((S, W), f32)], scratch_shapes=[pltpu.VMEM((2 * tk, tk), bf16)])
    return o, moved


def sb_bwd(qn, kn, vb, do, exchange, *, tq=256):
    S, W = qn.shape
    H = W // LANES
    tq = _tile(S, tq)
    tk = tq
    nq = S // tq

    def body(q_ref, k_ref, v_ref, do_ref, dq_ref, dk_ref, dv_ref, dz_s, beta_s, uu_s, ue_s):
        i = pl.program_id(1)
        row = lax.broadcasted_iota(jnp.int32, (tq, tk), 0)
        col = lax.broadcasted_iota(jnp.int32, (tq, tk), 1)

        @pl.when((pl.program_id(0) == 0) & (i == 0))
        def _():
            uu_s[...] = _tri2(tk)
            ue_s[...] = (row < col).astype(bf16)

        @pl.when(i == 0)
        def _():
            dk_ref[...] = jnp.zeros_like(dk_ref)
            dv_ref[...] = jnp.zeros_like(dv_ref)

        q = q_ref[...]
        dob = do_ref[...].astype(bf16)
        UU = uu_s[...]
        Ue = ue_s[...]

        def sweep1(js, R, masked):
            rows = [pl.ds(pl.multiple_of(j * tk, tk), tk) for j in js]
            dws = [_dot(dob, v_ref[r, :], _NT) for r in rows]
            lbs, ws, R = _sb_weights(q, [k_ref[r, :] for r in rows], R, masked, row, col, UU)
            for j, lb, w, dw in zip(js, lbs, ws, dws):
                dz_s[j] = (dw * w).astype(bf16)
                beta_s[j] = jnp.exp2(lb).astype(bf16)
            for r, w in zip(rows, ws):
                dv_ref[r, :] += _dot(w, dob, _TN)
            return R

        _, ran = _sweep(i, sweep1, jnp.zeros((tq, 1), f32), True, lambda R: R)

        def sweep2(js, c, masked):
            rows = [pl.ds(pl.multiple_of(j * tk, tk), tk) for j in js]
            dzbs = [dz_s[j] for j in js]
            sums = [_dot(dzb, Ue, _NN) for dzb in dzbs]
            Lz, dq = c
            dlbs = []
            for j, dzb, Cz, diag in zip(js, dzbs, sums, masked):
                dz = dzb.astype(f32)
                dl = dz - beta_s[j].astype(f32) * (dz + (Cz + Lz))
                if diag:
                    dl = jnp.where(col < row, dl, 0.0)
                Lz = Lz + (Cz[:, tk - 1:tk] + dz[:, tk - 1:tk])
                dlbs.append(dl.astype(bf16))
            for r, dlb in zip(rows, dlbs):
                dq = dq + _dot(dlb, k_ref[r, :], _NN)
            for r, dlb in zip(rows, dlbs):
                dk_ref[r, :] += _dot(dlb, q, _TN)
            return Lz, dq

        c = _sweep(i, sweep2, (jnp.zeros((tq, 1), f32), jnp.zeros((tq, LANES), f32)), False, ran=ran)
        dq_ref[...] = c[1]

    qspec = pl.BlockSpec((tq, LANES), lambda h, i: (i, h))
    kspec = pl.BlockSpec((S, LANES), lambda h, i: (0, h))
    full = jax.ShapeDtypeStruct((S, W), f32)
    (dq, dk, dv), moved = _call_with_exchange(
        body, (qn, kn, vb, do), exchange, name="sb_bwd", grid=(H, nq), in_specs=[qspec, kspec, kspec, qspec],
        out_specs=[qspec, kspec, kspec], out_shape=[full, full, full],
        scratch_shapes=[pltpu.VMEM((nq, tq, tk), bf16), pltpu.VMEM((nq, tq, tk), bf16),
                        pltpu.VMEM((2 * tk, tk), bf16), pltpu.VMEM((tk, tk), bf16)])
    return dq, dk, dv, moved


def _xa_fn(qx, kv, gq, gk):
    XW = XA_HEADS * XA_HEAD_DIM
    outs = []
    for h in range(XA_HEADS):
        sl = slice(h * XA_HEAD_DIM, (h + 1) * XA_HEAD_DIM)
        qn = _rms(qx[:, sl], gq, XA_HEAD_DIM)
        kn = _rms(kv[:, sl], gk, XA_HEAD_DIM)
        v = kv[:, XW + h * XA_HEAD_DIM:XW + (h + 1) * XA_HEAD_DIM]
        s = bdot_nt(qn, kn) * (XA_HEAD_DIM ** -0.5)
        e = jnp.exp(s - lax.stop_gradient(jnp.max(s, axis=-1, keepdims=True)))
        p = e / jnp.sum(e, axis=-1, keepdims=True)
        outs.append(bdot_nn(p, v))
    return jnp.concatenate(outs, axis=1)


def xa_fwd(qx, kv, gq, gk, *, ts=256):
    S, XW = qx.shape
    M = kv.shape[0]
    ts = _tile(S, ts)

    def body(q_ref, kv_ref, gq_ref, gk_ref, o_ref):
        o_ref[...] = _xa_fn(q_ref[...], kv_ref[...], gq_ref[...], gk_ref[...]).astype(bf16)

    row = pl.BlockSpec((ts, XW), lambda i: (i, 0))
    gspec = pl.BlockSpec((1, XA_HEAD_DIM), lambda i: (0, 0))
    return pl.pallas_call(
        body, name="xa_fwd", grid=(S // ts,),
        in_specs=[row, pl.BlockSpec((M, 2 * XW), lambda i: (0, 0)), gspec, gspec], out_specs=row,
        out_shape=jax.ShapeDtypeStruct((S, XW), bf16), compiler_params=_cp(("parallel",)),
    )(qx, kv, gq, gk)


def xa_bwd(qx, kv, gq, gk, do, *, ts=256):
    S, XW = qx.shape
    M = kv.shape[0]
    ts = _tile(S, ts)

    def body(q_ref, kv_ref, gq_ref, gk_ref, do_ref, dq_ref, dkv_ref, dgq_ref, dgk_ref):
        _, vjp = jax.vjp(_xa_fn, q_ref[...], kv_ref[...], gq_ref[...], gk_ref[...])
        dq, dkv, dgq, dgk = vjp(do_ref[...].astype(f32))
        dq_ref[...] = dq.astype(bf16)

        @pl.when(pl.program_id(0) == 0)
        def _():
            dkv_ref[...] = jnp.zeros_like(dkv_ref)
            dgq_ref[...] = jnp.zeros_like(dgq_ref)
            dgk_ref[...] = jnp.zeros_like(dgk_ref)

        dkv_ref[...] += dkv
        dgq_ref[...] += dgq
        dgk_ref[...] += dgk

    row = pl.BlockSpec((ts, XW), lambda i: (i, 0))
    gspec = pl.BlockSpec((1, XA_HEAD_DIM), lambda i: (0, 0))
    kvspec = pl.BlockSpec((M, 2 * XW), lambda i: (0, 0))
    gshape = jax.ShapeDtypeStruct((1, XA_HEAD_DIM), f32)
    return pl.pallas_call(
        body, name="xa_bwd", grid=(S // ts,), in_specs=[row, kvspec, gspec, gspec, row],
        out_specs=[row, kvspec, gspec, gspec],
        out_shape=[jax.ShapeDtypeStruct((S, XW), bf16), jax.ShapeDtypeStruct((M, 2 * XW), f32), gshape, gshape],
        compiler_params=_cp(("arbitrary",)),
    )(qx, kv, gq, gk, do)


def _s5_prep_fn(a_re, a_im, ldt, bT_re, bT_im, cT_re, cT_im):
    G, P, C = SSM_GROUPS, SSM_STATE, SSM_GROUP
    GP, GC = G * P, G * C
    lg_p, lg_c = P.bit_length() - 1, C.bit_length() - 1
    gi = lax.broadcasted_iota(jnp.int32, (G, GP), 0)
    ci = lax.broadcasted_iota(jnp.int32, (G, GP), 1) >> lg_p
    expand_dt = (gi == ci).astype(f32)
    dte = jnp.dot(jnp.exp(ldt), expand_dt, precision=lax.Precision.HIGHEST, preferred_element_type=f32)
    zr, zi = a_re * dte, a_im * dte
    mag = jnp.exp(zr)
    abr, abi = mag * jnp.cos(zi), mag * jnp.sin(zi)
    nr, ni = abr - 1.0, abi
    den = a_re * a_re + a_im * a_im
    cr = (nr * a_re + ni * a_im) / den
    cim = (ni * a_re - nr * a_im) / den
    bbr = cr * bT_re - cim * bT_im
    bbi = cr * bT_im + cim * bT_re
    rowg = lax.broadcasted_iota(jnp.int32, (GC, GP), 0) >> lg_c
    colg = lax.broadcasted_iota(jnp.int32, (GC, GP), 1) >> lg_p
    diag = rowg == colg

    def expand(t):
        return jnp.where(diag, jnp.broadcast_to(t[None], (G, C, GP)).reshape(GC, GP), 0.0)

    return abr, abi, expand(bbr), expand(bbi), expand(cT_re), expand(-cT_im)


def s5_prep(a_re, a_im, ldt, bT_re, bT_im, cT_re, cT_im):
    GP, GC = SSM_GROUPS * SSM_STATE, SSM_GROUPS * SSM_GROUP

    def body(a_re_ref, a_im_ref, ldt_ref, bTr_ref, bTi_ref, cTr_ref, cTi_ref, abr_ref, abi_ref, B_ref, C_ref):
        abr, abi, Br, Bi, Cr, Ci = _s5_prep_fn(a_re_ref[...], a_im_ref[...], ldt_ref[...], bTr_ref[...],
                                               bTi_ref[...], cTr_ref[...], cTi_ref[...])
        abr_ref[...] = abr
        abi_ref[...] = abi
        B_ref[0] = Br.astype(bf16)
        B_ref[1] = Bi.astype(bf16)
        C_ref[0] = Cr.astype(bf16)
        C_ref[1] = Ci.astype(bf16)

    vec = jax.ShapeDtypeStruct((1, GP), f32)
    mat = jax.ShapeDtypeStruct((2, GC, GP), bf16)
    return pl.pallas_call(body, name="s5_prep", out_shape=[vec, vec, mat, mat], compiler_params=_cp())(
        a_re, a_im, ldt, bT_re, bT_im, cT_re, cT_im)


def s5_prep_bwd(a_re, a_im, ldt, bT_re, bT_im, cT_re, cT_im, dabr, dabi, dB, dC):
    def body(a_re_ref, a_im_ref, ldt_ref, bTr_ref, bTi_ref, cTr_ref, cTi_ref, dabr_ref, dabi_ref, dB_ref, dC_ref,
             *outs):
        _, vjp = jax.vjp(_s5_prep_fn, a_re_ref[...], a_im_ref[...], ldt_ref[...], bTr_ref[...], bTi_ref[...],
                         cTr_ref[...], cTi_ref[...])
        grads = vjp((dabr_ref[...], dabi_ref[...], dB_ref[0], dB_ref[1], dC_ref[0], dC_ref[1]))
        for o_ref, gv in zip(outs, grads):
            o_ref[...] = gv

    ins = (a_re, a_im, ldt, bT_re, bT_im, cT_re, cT_im)
    return pl.pallas_call(body, name="s5_prep_bwd", out_shape=[jax.ShapeDtypeStruct(v.shape, f32) for v in ins],
                          compiler_params=_cp())(*ins, dabr, dabi, dB, dC)


def _cmul(ar, ai, br, bi):
    return ar * br - ai * bi, ar * bi + ai * br


SCAN_CHUNKS = 32


def _chunk_carry(Lr, Li, Pr, Pi, scratch, reverse):
    lr_ref, li_ref, cr_ref, ci_ref = scratch
    lr_ref[...] = Lr
    li_ref[...] = Li
    cur_r = jnp.zeros((1, LANES), f32)
    cur_i = jnp.zeros((1, LANES), f32)
    order = range(SCAN_CHUNKS - 1, -1, -1) if reverse else range(SCAN_CHUNKS)
    for c in order:
        cr_ref[pl.ds(c, 1), :] = cur_r
        ci_ref[pl.ds(c, 1), :] = cur_i
        mr, mi = _cmul(Pr, Pi, cur_r, cur_i)
        cur_r, cur_i = lr_ref[pl.ds(c, 1), :] + mr, li_ref[pl.ds(c, 1), :] + mi
    return cr_ref[...], ci_ref[...]


def _chunk_rows(j):
    return pl.ds(pl.multiple_of(j * SCAN_CHUNKS, SCAN_CHUNKS), SCAN_CHUNKS)


def row_shuffle(x, a, b, *, name, add=None, out_dtype=f32):
    S, W = x.shape
    assert a * b == S and x.dtype == f32

    def body(*refs):
        x_ref, o_ref = refs[0], refs[-1]

        def step(i, _):
            dst = pl.ds(pl.multiple_of(i * b, b), b)
            v = x_ref[pl.ds(i, b, stride=a), :]
            if add is not None:
                v = v + refs[1][dst, :]
            o_ref[dst, :] = v.astype(out_dtype)
            return 0

        lax.fori_loop(0, a, step, 0)

    col = pl.BlockSpec((S, LANES), lambda t: (0, t))
    args = [x] + ([add] if add is not None else [])
    return pl.pallas_call(
        body, name=name, grid=(W // LANES,), in_specs=[col] * len(args), out_specs=col,
        out_shape=jax.ShapeDtypeStruct((S, W), out_dtype), compiler_params=_cp(("parallel",)),
    )(*args)


def _scan_scratch(n):
    small = pltpu.VMEM((SCAN_CHUNKS, LANES), f32)
    return [pltpu.VMEM((n, LANES), f32), pltpu.VMEM((n, LANES), f32), small, small, small, small]


def _drive(x_ref, m_ref, work_ref):
    S = x_ref.shape[0]
    rows = min(S, 1024)
    m = jnp.concatenate([m_ref[0], m_ref[1]], axis=1)

    def chunk(c, _):
        r = pl.ds(pl.multiple_of(c * rows, rows), rows)
        y = _dot(x_ref[r, :], m, _NN)
        work_ref[0, r, :] = y[:, :LANES]
        work_ref[1, r, :] = y[:, LANES:]
        return 0

    lax.fori_loop(0, S // rows, chunk, 0)


def scan_fwd(u_il, Bm, abr, abi, exchange):
    S, C = u_il.shape
    N = Bm.shape[2]
    n = S // SCAN_CHUNKS
    shp = (SCAN_CHUNKS, LANES)

    def body(u_ref, B_ref, ar_ref, ai_ref, st_ref, work_ref, pwr_ref, pwi_ref, *scratch):
        a1r, a1i = ar_ref[...], ai_ref[...]
        ar = jnp.broadcast_to(a1r, shp)
        ai = jnp.broadcast_to(a1i, shp)
        _drive(u_ref, B_ref, work_ref)
        sr_ref, si_ref = work_ref.at[0], work_ref.at[1]

        def step(j, c):
            sr, si, pr, pi = c
            rows = _chunk_rows(j)
            mr, mi = _cmul(ar, ai, sr, si)
            sr, si = mr + sr_ref[rows, :], mi + si_ref[rows, :]
            sr_ref[rows, :] = sr
            si_ref[rows, :] = si
            pwr_ref[pl.ds(j, 1), :] = pr
            pwi_ref[pl.ds(j, 1), :] = pi
            npr, npi = _cmul(a1r, a1i, pr, pi)
            return sr, si, npr, npi

        z = jnp.zeros(shp, f32)
        sr, si, _, _ = lax.fori_loop(0, n, step, (z, z, a1r, a1i), unroll=2)
        cr, ci = _chunk_carry(sr, si, pwr_ref[pl.ds(n - 1, 1), :], pwi_ref[pl.ds(n - 1, 1), :], scratch, False)

        def step2(j, _):
            rows = _chunk_rows(j)
            pr = jnp.broadcast_to(pwr_ref[pl.ds(j, 1), :], shp)
            pi = jnp.broadcast_to(pwi_ref[pl.ds(j, 1), :], shp)
            mr, mi = _cmul(pr, pi, cr, ci)
            st_ref[0, rows, :] = (sr_ref[rows, :] + mr).astype(bf16)
            st_ref[1, rows, :] = (si_ref[rows, :] + mi).astype(bf16)
            return 0

        lax.fori_loop(0, n, step2, 0, unroll=4)

    blk = pl.BlockSpec((2, S, LANES), lambda t: (0, 0, t))
    vec = pl.BlockSpec((1, LANES), lambda t: (0, t))
    (st,), moved = _call_with_exchange(
        body, (u_il, Bm, abr, abi), exchange, name="scan_fwd", grid=(N // LANES,),
        in_specs=[pl.BlockSpec((S, C), lambda t: (0, 0)), pl.BlockSpec((2, C, LANES), lambda t: (0, 0, t)), vec, vec],
        out_specs=[blk], out_shape=[jax.ShapeDtypeStruct((2, S, N), bf16)],
        scratch_shapes=[pltpu.VMEM((2, S, LANES), f32)] + _scan_scratch(n))
    return st, moved


def scan_bwd(dy_il, Cm, st, abr, abi, exchange):
    _, S, N = st.shape
    C = dy_il.shape[1]
    n = S // SCAN_CHUNKS
    shp = (SCAN_CHUNKS, LANES)

    def body(dy_ref, C_ref, st_ref, ar_ref, ai_ref, g_ref, dar_ref, dai_ref, work_ref, qwr_ref, qwi_ref, *scratch):
        a1r, a1i = ar_ref[...], -ai_ref[...]
        ar = jnp.broadcast_to(a1r, shp)
        nai = jnp.broadcast_to(a1i, shp)
        _drive(dy_ref, C_ref, work_ref)
        gr_ref, gi_ref = work_ref.at[0], work_ref.at[1]
        sr_ref, si_ref = st_ref.at[0], st_ref.at[1]

        def step(jj, c):
            gr, gi, qr, qi = c
            j = n - 1 - jj
            rows = _chunk_rows(j)
            mr, mi = _cmul(ar, nai, gr, gi)
            gr, gi = mr + gr_ref[rows, :], mi + gi_ref[rows, :]
            gr_ref[rows, :] = gr
            gi_ref[rows, :] = gi
            qwr_ref[pl.ds(j, 1), :] = qr
            qwi_ref[pl.ds(j, 1), :] = qi
            nqr, nqi = _cmul(a1r, a1i, qr, qi)
            return gr, gi, nqr, nqi

        z = jnp.zeros(shp, f32)
        gr, gi, _, _ = lax.fori_loop(0, n, step, (z, z, a1r, a1i), unroll=2)
        cr, ci = _chunk_carry(gr, gi, qwr_ref[pl.ds(0, 1), :], qwi_ref[pl.ds(0, 1), :], scratch, True)
        sub = lax.broadcasted_iota(jnp.int32, shp, 0)

        def fix(j, spr, spi, acc):
            rows = _chunk_rows(j)
            qr = jnp.broadcast_to(qwr_ref[pl.ds(j, 1), :], shp)
            qi = jnp.broadcast_to(qwi_ref[pl.ds(j, 1), :], shp)
            mr, mi = _cmul(qr, qi, cr, ci)
            gr = gr_ref[rows, :] + mr
            gi = gi_ref[rows, :] + mi
            g_ref[0, rows, :] = gr.astype(bf16)
            g_ref[1, rows, :] = gi.astype(bf16)
            return acc[0] + gr * spr + gi * spi, acc[1] + gi * spr - gr * spi

        last = _chunk_rows(n - 1)
        spr = jnp.where(sub == 0, 0.0, pltpu.roll(sr_ref[last, :].astype(f32), 1, 0))
        spi = jnp.where(sub == 0, 0.0, pltpu.roll(si_ref[last, :].astype(f32), 1, 0))
        acc = fix(0, spr, spi, (z, z))

        def step2(j, acc):
            prev = _chunk_rows(j - 1)
            return fix(j, sr_ref[prev, :].astype(f32), si_ref[prev, :].astype(f32), acc)

        acc = lax.fori_loop(1, n, step2, acc)
        dar_ref[...] = jnp.sum(acc[0], axis=0, keepdims=True)
        dai_ref[...] = jnp.sum(acc[1], axis=0, keepdims=True)

    blk = pl.BlockSpec((2, S, LANES), lambda t: (0, 0, t))
    vec = pl.BlockSpec((1, LANES), lambda t: (0, t))
    vshape = jax.ShapeDtypeStruct((1, N), f32)
    (g, dar, dai), moved = _call_with_exchange(
        body, (dy_il, Cm, st, abr, abi), exchange, name="scan_bwd", grid=(N // LANES,),
        in_specs=[pl.BlockSpec((S, C), lambda t: (0, 0)), pl.BlockSpec((2, C, LANES), lambda t: (0, 0, t)), blk,
                  vec, vec],
        out_specs=[blk, vec, vec], out_shape=[jax.ShapeDtypeStruct((2, S, N), bf16), vshape, vshape],
        scratch_shapes=[pltpu.VMEM((2, S, LANES), f32)] + _scan_scratch(n))
    return g, dar, dai, moved


def _glu_fn(ypre, wglu):
    y = jax.nn.gelu(ypre)
    return y * jax.nn.sigmoid(bdot_nn(y, wglu))


def glu_fwd(ypre0, u, d, wglu, g_out, *, ts=512):
    S, W = u.shape
    ts = _tile(S, ts)

    def body(y0_ref, u_ref, d_ref, w_ref, g_ref, ypre_ref, z_ref, zn_ref):
        ypre = y0_ref[...] + d_ref[...] * u_ref[...]
        z = _glu_fn(ypre, w_ref[...])
        ypre_ref[...] = ypre
        z_ref[...] = z
        zn_ref[...] = _rms(z, g_ref[...], W).astype(bf16)

    row = pl.BlockSpec((ts, W), lambda i: (i, 0))
    vec = pl.BlockSpec((1, W), lambda i: (0, 0))
    full = jax.ShapeDtypeStruct((S, W), f32)
    return pl.pallas_call(
        body, name="glu_fwd", grid=(S // ts,),
        in_specs=[row, row, vec, pl.BlockSpec((W, W), lambda i: (0, 0)), vec], out_specs=[row, row, row],
        out_shape=[full, full, jax.ShapeDtypeStruct((S, W), bf16)], compiler_params=_cp(("parallel",)),
    )(ypre0, u, d, wglu, g_out)


def glu_bwd(ypre, u, d, wglu, dz, *, ts=512):
    S, W = u.shape
    ts = _tile(S, ts)

    def body(y_ref, u_ref, d_ref, w_ref, dz_ref, dy_ref, du_ref, dw_ref, dd_ref):
        _, vjp = jax.vjp(_glu_fn, y_ref[...], w_ref[...])
        dy, dw = vjp(dz_ref[...])
        dy_ref[...] = dy
        du_ref[...] = d_ref[...] * dy

        @pl.when(pl.program_id(0) == 0)
        def _():
            dw_ref[...] = jnp.zeros_like(dw_ref)
            dd_ref[...] = jnp.zeros_like(dd_ref)

        dw_ref[...] += dw
        dd_ref[...] += jnp.sum(dy * u_ref[...], axis=0, keepdims=True)

    row = pl.BlockSpec((ts, W), lambda i: (i, 0))
    vec = pl.BlockSpec((1, W), lambda i: (0, 0))
    sq = pl.BlockSpec((W, W), lambda i: (0, 0))
    full = jax.ShapeDtypeStruct((S, W), f32)
    return pl.pallas_call(
        body, name="glu_bwd", grid=(S // ts,), in_specs=[row, row, vec, sq, row], out_specs=[row, row, sq, vec],
        out_shape=[full, full, jax.ShapeDtypeStruct((W, W), f32), jax.ShapeDtypeStruct((1, W), f32)],
        compiler_params=_cp(("arbitrary",)),
    )(ypre, u, d, wglu, dz)


def loss_head(y, target, *, ts=512):
    S, D = y.shape
    ts = _tile(S, ts)

    def body(y_ref, t_ref, dy_ref, l_ref, dyb_ref):
        err = y_ref[...] - t_ref[...]
        dy_ref[...] = err * (1.0 / D)
        dyb_ref[...] = (err * (1.0 / D)).astype(bf16)

        @pl.when(pl.program_id(0) == 0)
        def _():
            l_ref[...] = jnp.zeros_like(l_ref)

        rows = jnp.sum(err * err, axis=1, keepdims=True) * (1.0 / D)
        l_ref[...] += 0.5 * jnp.sum(rows, axis=0, keepdims=True)

    row = pl.BlockSpec((ts, D), lambda i: (i, 0))
    return pl.pallas_call(
        body, name="loss_head", grid=(S // ts,), in_specs=[row, row],
        out_specs=[row, pl.BlockSpec((1, 1), lambda i: (0, 0)), row],
        out_shape=[jax.ShapeDtypeStruct((S, D), f32), jax.ShapeDtypeStruct((1, 1), f32),
                   jax.ShapeDtypeStruct((S, D), bf16)],
        compiler_params=_cp(("arbitrary",)),
    )(y, target)


def adamw(w, g, m, v, *, name, tr=256):
    R, C = w.shape
    tr = _row_tile(R, tr)

    def body(w_ref, g_ref, m_ref, v_ref, d_ref, nm_ref, nv_ref):
        gv = g_ref[...]
        nm = ADAM_B1 * m_ref[...] + (1.0 - ADAM_B1) * gv
        nv = ADAM_B2 * v_ref[...] + (1.0 - ADAM_B2) * jnp.square(gv)
        m_hat = nm / (1.0 - ADAM_B1 ** ADAM_STEP)
        v_hat = nv / (1.0 - ADAM_B2 ** ADAM_STEP)
        d_ref[...] = -ADAM_LR * (m_hat / (jnp.sqrt(v_hat) + ADAM_EPS) + ADAM_WD * w_ref[...])
        nm_ref[...] = nm
        nv_ref[...] = nv

    row = pl.BlockSpec((tr, C), lambda i: (i, 0))
    full = jax.ShapeDtypeStruct((R, C), f32)
    return pl.pallas_call(
        body, name=name, grid=(R // tr,), in_specs=[row] * 4, out_specs=[row] * 3, out_shape=[full] * 3,
        compiler_params=_cp(("parallel",)),
    )(w, g, m, v)


def add_half(g4, recv, c, *, name, tr=256):
    _, _, Rh, C = g4.shape
    tr = _row_tile(Rh, tr)

    def body(c_ref, a_ref, b_ref, o_ref):
        o_ref[...] = a_ref[...] + b_ref[...]

    grid_spec = pltpu.PrefetchScalarGridSpec(
        num_scalar_prefetch=1, grid=(N_CHIPS, Rh // tr),
        in_specs=[pl.BlockSpec((None, None, tr, C), lambda k, i, c_ref: (k, c_ref[0], i, 0)),
                  pl.BlockSpec((None, tr, C), lambda k, i, c_ref: (k, i, 0))],
        out_specs=pl.BlockSpec((None, tr, C), lambda k, i, c_ref: (k, i, 0)))
    return pl.pallas_call(body, name=name, grid_spec=grid_spec, out_shape=jax.ShapeDtypeStruct(recv.shape, f32),
                          compiler_params=_cp(("parallel", "parallel")))(c, g4, recv)


def sum_chips(p4, *, name, tr=256):
    _, Rh, C = p4.shape
    tr = _row_tile(Rh, tr)

    def body(a_ref, b_ref, c_ref, d_ref, o_ref):
        o_ref[...] = ((a_ref[...] + b_ref[...]) + c_ref[...]) + d_ref[...]

    spec = lambda k: pl.BlockSpec((None, tr, C), lambda i: (k, i, 0))
    return pl.pallas_call(
        body, name=name, grid=(Rh // tr,), in_specs=[spec(0), spec(1), spec(2), spec(3)],
        out_specs=pl.BlockSpec((tr, C), lambda i: (i, 0)), out_shape=jax.ShapeDtypeStruct((Rh, C), f32),
        compiler_params=_cp(("parallel",)),
    )(p4, p4, p4, p4)


def _place():
    return lax.axis_index("x"), lax.axis_index("y"), lax.axis_index("c")


def _other_chips(x, y):
    return [(1 - x, y), (x, 1 - y), (1 - x, 1 - y)]


def _chip_exchange(ins, outs, sems, scatter, start):
    if not ins:
        return
    send, recv, loc = sems
    x, y, c = _place()
    me = 2 * x + y
    for a in range(len(ins)):
        own = pltpu.make_async_copy(ins[a].at[me] if scatter else ins[a], outs[a].at[me], loc.at[a])
        own.start() if start else own.wait()
        for p, (px, py) in enumerate(_other_chips(x, y)):
            k = 2 * px + py
            cp = pltpu.make_async_remote_copy(
                src_ref=ins[a].at[k] if scatter else ins[a], dst_ref=outs[a].at[me if start else k],
                send_sem=send.at[3 * a + p], recv_sem=recv.at[3 * a + p], device_id=(px, py, c), device_id_type=MESH)
            cp.start() if start else cp.wait()


def _chip_exchange_args(arrs, scatter):
    n = len(arrs)
    shapes = [jax.ShapeDtypeStruct(a.shape if scatter else (N_CHIPS,) + a.shape, a.dtype) for a in arrs]
    sems = [pltpu.SemaphoreType.DMA((3 * n,)), pltpu.SemaphoreType.DMA((3 * n,)), pltpu.SemaphoreType.DMA((n,))]
    return shapes, sems if n else []


def _chip_exchange_call(arrs, scatter, name):
    n = len(arrs)

    def body(*refs):
        ins, outs, sems = refs[:n], refs[n:2 * n], refs[2 * n:]
        _chip_exchange(ins, outs, sems, scatter, True)
        _chip_exchange(ins, outs, sems, scatter, False)

    shapes, sems = _chip_exchange_args(arrs, scatter)
    return pl.pallas_call(
        body, name=name, in_specs=[ANY] * n, out_specs=[ANY] * n, out_shape=shapes, scratch_shapes=sems,
        compiler_params=pltpu.CompilerParams(has_side_effects=True),
    )(*arrs)


def allgather_chips(arrs, *, name):
    return _chip_exchange_call(arrs, False, name)


def sibling_swap(arrs, *, half, name):
    n = len(arrs)

    def body(*refs):
        ins, outs = refs[:n], refs[n:2 * n]
        send, recv = refs[2 * n:]
        x, y, c = _place()
        cps = []
        for a in range(n):
            src = ins[a].at[:, 1 - c] if half else ins[a]
            cp = pltpu.make_async_remote_copy(src_ref=src, dst_ref=outs[a], send_sem=send.at[a], recv_sem=recv.at[a],
                                              device_id=(x, y, 1 - c), device_id_type=MESH)
            cp.start()
            cps.append(cp)
        for cp in cps:
            cp.wait()

    def oshape(a):
        return jax.ShapeDtypeStruct((a.shape[0],) + a.shape[2:] if half else a.shape, a.dtype)

    return pl.pallas_call(
        body, name=name, in_specs=[ANY] * n, out_specs=[ANY] * n, out_shape=[oshape(a) for a in arrs],
        scratch_shapes=[pltpu.SemaphoreType.DMA((n,)), pltpu.SemaphoreType.DMA((n,))],
        compiler_params=pltpu.CompilerParams(has_side_effects=True),
    )(*arrs)


def chip_scatter(arrs, *, name):
    return _chip_exchange_call(arrs, True, name)


def _pad_cols(w):
    K = w.shape[0]
    w = w.reshape(K, -1, SB_HEAD_DIM)
    return jnp.pad(w, ((0, 0), (0, 0), (0, LANES - SB_HEAD_DIM))).reshape(K, -1)


def _unpad_cols(w):
    K = w.shape[0]
    return w.reshape(K, -1, LANES)[:, :, :SB_HEAD_DIM].reshape(K, -1)


def _pad_rows(w):
    N = w.shape[1]
    w = w.reshape(-1, SB_HEAD_DIM, N)
    return jnp.pad(w, ((0, 0), (0, LANES - SB_HEAD_DIM), (0, 0))).reshape(-1, N)


def _unpad_rows(w):
    N = w.shape[1]
    return w.reshape(-1, LANES, N)[:, :SB_HEAD_DIM, :].reshape(-1, N)


_PACK_ROWS = N_CHIPS * 2 * SUBLANES


def _pack(arrs):
    flat = jnp.concatenate([a.reshape(-1) for a in arrs])
    rows = -(-flat.shape[0] // LANES)
    rows = -(-rows // _PACK_ROWS) * _PACK_ROWS
    return jnp.pad(flat, (0, rows * LANES - flat.shape[0])).reshape(rows, LANES)


def _unpack(buf, shapes):
    flat = buf.reshape(-1)
    out, pos = [], 0
    for shp in shapes:
        size = 1
        for d in shp:
            size *= d
        out.append(flat[pos:pos + size].reshape(shp))
        pos += size
    return out


BIG = ("w_in", "ssm_w_glu", "w_out", "xa_w_q", "xa_w_kv", "xa_w_o", "w_up", "w_down")
SMALL = ("g_mix", "ssm_a_re", "ssm_a_im", "ssm_log_dt", "ssm_b_re", "ssm_b_im", "ssm_c_re", "ssm_c_im", "ssm_d",
         "sb_g_q", "sb_g_k", "g_out_ssm", "g_out_sb", "g_xa", "g_mem", "xa_g_q", "xa_g_k", "g_mlp")
WEIGHTS = ("g_mix", "w_in", "ssm_a_re", "ssm_a_im", "ssm_log_dt", "ssm_b_re", "ssm_b_im", "ssm_c_re", "ssm_c_im",
           "ssm_d", "ssm_w_glu", "sb_g_q", "sb_g_k", "g_out_ssm", "g_out_sb", "w_out", "g_xa", "g_mem", "xa_w_q",
           "xa_w_kv", "xa_g_q", "xa_g_k", "xa_w_o", "g_mlp", "w_up", "w_down")


def kernel(x, mem, g_mix, w_in, ssm_a_re, ssm_a_im, ssm_log_dt, ssm_b_re, ssm_b_im, ssm_c_re, ssm_c_im, ssm_d, ssm_w_glu, sb_g_q, sb_g_k, g_out_ssm, g_out_sb, w_out, g_xa, g_mem, xa_w_q, xa_w_kv, xa_g_q, xa_g_k, xa_w_o, g_mlp, w_up, w_down, loss_target, m_g_mix, m_w_in, m_ssm_a_re, m_ssm_a_im, m_ssm_log_dt, m_ssm_b_re, m_ssm_b_im, m_ssm_c_re, m_ssm_c_im, m_ssm_d, m_ssm_w_glu, m_sb_g_q, m_sb_g_k, m_g_out_ssm, m_g_out_sb, m_w_out, m_g_xa, m_g_mem, m_xa_w_q, m_xa_w_kv, m_xa_g_q, m_xa_g_k, m_xa_w_o, m_g_mlp, m_w_up, m_w_down, v_g_mix, v_w_in, v_ssm_a_re, v_ssm_a_im, v_ssm_log_dt, v_ssm_b_re, v_ssm_b_im, v_ssm_c_re, v_ssm_c_im, v_ssm_d, v_ssm_w_glu, v_sb_g_q, v_sb_g_k, v_g_out_ssm, v_g_out_sb, v_w_out, v_g_xa, v_g_mem, v_xa_w_q, v_xa_w_kv, v_xa_g_q, v_xa_g_k, v_xa_w_o, v_g_mlp, v_w_up, v_w_down):
    env = dict(locals())
    W = {n: env[n] for n in WEIGHTS}
    M1 = {n: env["m_" + n] for n in WEIGHTS}
    V2 = {n: env["v_" + n] for n in WEIGHTS}
    xs, mems, tgt = x[0], mem[0], loss_target[0]
    S, D = xs.shape
    G, P, C = SSM_GROUPS, SSM_STATE, SSM_GROUP
    GP = G * P
    SBW = SB_HEADS * SB_HEAD_DIM
    c_idx = lax.axis_index("c")

    (g_in,) = allgather_chips([w_in[0].astype(bf16)], name="gather_w_in")
    Wu = g_in[0]
    Wqkv = jnp.concatenate([_pad_cols(g_in[1]), _pad_cols(g_in[2]), _pad_cols(g_in[3])], axis=1)
    gq_pad, gk_pad = _pad_cols(sb_g_q), _pad_cols(sb_g_k)
    gosb_pad = _pad_cols(g_out_sb)
    a_re, a_im = ssm_a_re.reshape(1, GP), ssm_a_im.reshape(1, GP)
    bT_re = ssm_b_re[0].transpose(2, 0, 1).reshape(C, GP)
    bT_im = ssm_b_im[0].transpose(2, 0, 1).reshape(C, GP)
    cT_re = ssm_c_re[0].transpose(1, 0, 2).reshape(C, GP)
    cT_im = ssm_c_im[0].transpose(1, 0, 2).reshape(C, GP)
    s5_in = (a_re, a_im, ssm_log_dt, bT_re, bT_im, cT_re, cT_im)

    big = dict(tn=1024, tk=1024)
    wide = dict(tm=1024, tn=1024, tk=2048)
    h0 = rms_norm(xs, g_mix, D, name="norm_x")
    u = mm(h0, Wu, mode="nn", name="proj_u", tk=1024)
    shard = {n: W[n][0].astype(bf16) for n in BIG[1:]}
    qkv, (g_glu, g_out, g_xq, g_xkv, g_xo) = mm(
        h0, Wqkv, mode="nn", name="proj_qkv", tm=1024,
        exchange=([shard[n] for n in ("ssm_w_glu", "w_out", "xa_w_q", "xa_w_kv", "xa_w_o")], False), **big)
    qn, kn, vb = qkv_prep(qkv, gq_pad, gk_pad)
    o, (g_down,) = sb_fwd(qn, kn, vb, ([shard["w_down"]], False))
    Wglu = g_glu.reshape(-1, g_glu.shape[-1])
    Wout = g_out.reshape(-1, g_out.shape[-1])
    Wo_ssm, Wo_sb = Wout[:SBW], _pad_rows(Wout[SBW:])
    Wxq = g_xq.reshape(-1, g_xq.shape[-1])
    Wxkv = g_xkv.reshape(-1, g_xkv.shape[-1])
    Wxo = g_xo.transpose(1, 0, 2).reshape(g_xo.shape[1], -1)
    abr, abi, Bm, Cm = s5_prep(*s5_in)
    n_pos = S // SCAN_CHUNKS
    u_il = row_shuffle(u, n_pos, SCAN_CHUNKS, name="u_interleave", out_dtype=bf16)
    st, (g_up,) = scan_fwd(u_il, Bm, abr, abi, ([shard["w_up"]], False))
    ypre0_il = mm(st, Cm, mode="nt", name="s5_y", a_shards=2, b_shards=2, tm=1024, tk=2048)
    Wup = g_up.transpose(1, 0, 2).reshape(g_up.shape[1], -1)
    Wdown = g_down.reshape(-1, g_down.shape[-1])
    ypre0 = row_shuffle(ypre0_il, SCAN_CHUNKS, n_pos, name="y_token_order")
    ypre, z, zn = glu_fwd(ypre0, u, ssm_d, Wglu, g_out_ssm)
    on = rms_norm(o, gosb_pad, SBW, name="norm_o")
    x1a = mm(zn, Wo_ssm, mode="nn", name="out_ssm", epi="add", aux=xs, tn=1024)
    x1 = mm(on, Wo_sb, mode="nn", name="out_sb", epi="add", aux=x1a, **big)
    h1 = rms_norm(x1, g_xa, D, name="norm_x1")
    qx = mm(h1, Wxq, mode="nn", name="xa_q", tk=1024)
    memn = rms_norm(mems, g_mem, D, name="norm_mem")
    kv = mm(memn, Wxkv, mode="nn", name="xa_kv", **big)
    ox = xa_fwd(qx, kv, xa_g_q, xa_g_k)
    x2 = mm(ox, Wxo, mode="nn", name="xa_o", epi="add", aux=x1, tn=1024)
    h2 = rms_norm(x2, g_mlp, D, name="norm_x2")
    act = mm(h2, Wup, mode="nn", name="mlp_up", out_dtype=bf16, tm=1024, tn=2048, tk=1024)
    x3 = mm(act, Wdown, mode="nn", name="mlp_down", pro="relu2", epi="add", aux=x2, **wide)
    dx3, loss_part, dx3b = loss_head(x3, tgt)
    loss = lax.psum(loss_part[0, 0], ("x", "y", "c"))

    dact = mm(dx3b, Wdown, mode="nt", name="d_act", epi="mul2relu", aux=act, out_dtype=bf16, tm=1024, tn=2048,
              tk=1024)
    dWdown = mm(act, dx3b, mode="tn", name="dw_down", pro="relu2", **wide)
    dWup = mm(h2, dact, mode="tn", name="dw_up", out_shards=N_CHIPS, **wide)
    dh2 = mm(dact, Wup, mode="nt", name="d_h2", **wide)
    dx2, dg_mlp, dx2b = rms_bwd(x2, g_mlp, dh2, dx3, D, name="rms_bwd_mlp", twin=True)
    dox = mm(dx2b, Wxo, mode="nt", name="d_ox", out_dtype=bf16, tk=1024)
    dWxo = mm(ox, dx2b, mode="tn", name="dw_xo", out_shards=N_CHIPS, tk=1024)
    dqx, dkv, dg_xq, dg_xk = xa_bwd(qx, kv, xa_g_q, xa_g_k, dox)
    dWxq = mm(h1, dqx, mode="tn", name="dw_xq", tm=1024, tk=1024)
    dh1 = mm(dqx, Wxq, mode="nt", name="d_h1", tn=1024)
    dx1, dg_xa, dx1b = rms_bwd(x1, g_xa, dh1, dx2, D, name="rms_bwd_xa", twin=True)
    dWxkv = mm(memn, dkv, mode="tn", name="dw_xkv", tm=1024, tn=1024)
    dmemn = mm(dkv, Wxkv, mode="nt", name="d_memn", **big)
    _, dg_mem = rms_bwd(mems, g_mem, dmemn, None, D, name="rms_bwd_mem")
    dyn_ssm = mm(dx1b, Wo_ssm, mode="nt", name="d_yn_ssm", tk=1024)
    dyn_sb = mm(dx1b, Wo_sb, mode="nt", name="d_yn_sb", **big)
    dWo_ssm = mm(zn, dx1b, mode="tn", name="dw_out_ssm", **big)
    dWo_sb = mm(on, dx1b, mode="tn", name="dw_out_sb", tm=1024, **big)
    dz, dg_os = rms_bwd(z, g_out_ssm, dyn_ssm, None, SBW, name="rms_bwd_ssm")
    do, dg_osb = rms_bwd(o, gosb_pad, dyn_sb, None, SBW, name="rms_bwd_sb")
    c_arr = c_idx.astype(jnp.int32).reshape(1)

    def sibling_sums(grads, names, tag):
        g4 = [g.reshape(N_CHIPS, 2, g.shape[1] // 2, g.shape[2]) for g in grads]
        from_sib = sibling_swap(g4, half=True, name="grad_to_sibling_" + tag)
        return [add_half(a, b, c_arr, name="add_sibling_" + n) for a, b, n in zip(g4, from_sib, names)]

    early = ("xa_w_q", "xa_w_kv", "xa_w_o", "w_up", "w_down")
    early_g = [dWxq.reshape(N_CHIPS, -1, dWxq.shape[1]), dWxkv.reshape(N_CHIPS, -1, dWxkv.shape[1]), dWxo, dWup,
               dWdown.reshape(N_CHIPS, -1, D)]
    pair = sibling_sums(early_g, early, "early")
    dqn, dkn, dv, parts_mlp = sb_bwd(qn, kn, vb, do, (pair[3:], True))
    dqkv, dg_q, dg_k, parts_xa = qkv_bwd(qkv, gq_pad, gk_pad, dqn, dkn, dv, (pair[:3], True))
    dypre, du_skip, dWglu, dd = glu_bwd(ypre, u, ssm_d, Wglu, dz)
    mid = ("ssm_w_glu", "w_out")
    mid_g = [dWglu.reshape(N_CHIPS, -1, dWglu.shape[1]),
             jnp.concatenate([dWo_ssm, _unpad_rows(dWo_sb)]).reshape(N_CHIPS, -1, D)]
    dypre_il = row_shuffle(dypre, n_pos, SCAN_CHUNKS, name="dy_interleave", out_dtype=bf16)
    dCm = mm(dypre_il, st, mode="tn", name="d_cmat", b_shards=2, out_shards=2, **wide)
    gst, dabr, dabi, parts_mid = scan_bwd(dypre_il, Cm, st, abr, abi, (sibling_sums(mid_g, mid, "mid"), True))
    dBm = mm(u_il, gst, mode="tn", name="d_bmat", b_shards=2, out_shards=2, **wide)
    du_il = mm(gst, Bm, mode="nt", name="d_u", a_shards=2, b_shards=2, tm=1024, tk=2048)
    mine = {n: sum_chips(p, name="sum_chips_" + n)
            for n, p in zip(early + mid, [*parts_xa, *parts_mlp, *parts_mid])}
    du = row_shuffle(du_il, SCAN_CHUNKS, n_pos, name="du_token_order", add=du_skip, out_dtype=bf16)
    s5_g = s5_prep_bwd(*s5_in, dabr, dabi, dBm, dCm)
    dWu = mm(h0, du, mode="tn", name="dw_u", tm=1024, tk=1024)
    dWqkv = mm(h0, dqkv, mode="tn", name="dw_qkv", **wide)
    dh0a = mm(du, Wu, mode="nt", name="d_h0_u", tn=1024)
    dh0 = mm(dqkv, Wqkv, mode="nt", name="d_h0_qkv", epi="add", aux=dh0a, tm=1024, **big)
    dx, dg_mix = rms_bwd(xs, g_mix, dh0, dx1, D, name="rms_bwd_mix")

    HW = SB_HEADS * LANES
    late = ("w_in", "small")
    late_g = [jnp.stack([dWu, _unpad_cols(dWqkv[:, :HW]), _unpad_cols(dWqkv[:, HW:2 * HW]),
                         _unpad_cols(dWqkv[:, 2 * HW:])])]
    da_re, da_im, dldt, dbT_re, dbT_im, dcT_re, dcT_im = s5_g
    small_g = {
        "g_mix": dg_mix, "ssm_a_re": da_re, "ssm_a_im": da_im, "ssm_log_dt": dldt,
        "ssm_b_re": dbT_re.reshape(C, G, P).transpose(1, 2, 0), "ssm_b_im": dbT_im.reshape(C, G, P).transpose(1, 2, 0),
        "ssm_c_re": dcT_re.reshape(C, G, P).transpose(1, 0, 2), "ssm_c_im": dcT_im.reshape(C, G, P).transpose(1, 0, 2),
        "ssm_d": dd, "sb_g_q": dg_q[:, :SB_HEAD_DIM], "sb_g_k": dg_k[:, :SB_HEAD_DIM], "g_out_ssm": dg_os,
        "g_out_sb": _unpad_cols(dg_osb), "g_xa": dg_xa, "g_mem": dg_mem, "xa_g_q": dg_xq, "xa_g_k": dg_xk,
        "g_mlp": dg_mlp,
    }
    late_g.append(_pack([small_g[n] for n in SMALL]).reshape(N_CHIPS, -1, LANES))

    parts_late = chip_scatter(sibling_sums(late_g, late, "late"), name="grad_to_chips_late")
    mine.update({n: sum_chips(p, name="sum_chips_" + n) for n, p in zip(late, parts_late)})
    mine = [mine[n] for n in list(BIG) + ["small"]]
    other = sibling_swap(mine, half=False, name="grad_half_to_sibling")
    shard = [jnp.where(c_idx == 0, jnp.concatenate([a, b]), jnp.concatenate([b, a])) for a, b in zip(mine, other)]
    small_all = allgather_chips([shard[-1]], name="gather_small")[0]
    small_red = small_all.reshape(-1, LANES)

    out = {}
    for n, gs in zip(BIG, shard[:-1]):
        shp = W[n].shape
        w2, m2, v2 = (t.reshape(gs.shape) for t in (W[n], M1[n], V2[n]))
        d, nm, nv = adamw(w2, gs, m2, v2, name="adamw_" + n)
        out[n] = tuple(t.reshape(shp) for t in (gs, d, nm, nv))
    shapes = [W[n].shape for n in SMALL]
    d, nm, nv = adamw(_pack([W[n] for n in SMALL]), small_red, _pack([M1[n] for n in SMALL]),
                      _pack([V2[n] for n in SMALL]), name="adamw_small")
    for n, gs, dd_, mm_, vv_ in zip(SMALL, _unpack(small_red, shapes), _unpack(d, shapes), _unpack(nm, shapes),
                                    _unpack(nv, shapes)):
        out[n] = (gs, dd_, mm_, vv_)
    res = [loss, dx[None]]
    for kind in range(4):
        res += [out[n][kind] for n in WEIGHTS]
    return tuple(res)
```

```python
import jax
import jax.numpy as jnp
from jax import lax
from jax.experimental import pallas as pl
from jax.experimental.pallas import tpu as pltpu

f32 = jnp.float32
bf16 = jnp.bfloat16

NORM_EPS = 1e-6
SSM_GROUPS = 32
SSM_GROUP = 16
SSM_STATE = 64
SB_HEADS = 8
SB_HEAD_DIM = 64
XA_HEADS = 4
XA_HEAD_DIM = 128
LANES = 128
SUBLANES = 8
N_CHIPS = 4
ADAM_LR = 0.001
ADAM_B1 = 0.9
ADAM_B2 = 0.999
ADAM_EPS = 1e-08
ADAM_WD = 0.01
ADAM_STEP = 10
VMEM_LIMIT = 56 * 1024 * 1024
MESH = pl.DeviceIdType.MESH
ANY = pl.BlockSpec(memory_space=pl.ANY)


def _cp(sem=None):
    return pltpu.CompilerParams(dimension_semantics=sem, vmem_limit_bytes=VMEM_LIMIT)


def _tile(n, pref):
    if n <= pref:
        return n
    t = (pref // LANES) * LANES
    while t > LANES and n % t:
        t -= LANES
    assert n % t == 0, (n, pref)
    return t


def _row_tile(n, pref):
    if n <= pref:
        return n
    t = (pref // SUBLANES) * SUBLANES
    while n % t:
        t -= SUBLANES
    return t


def _dot(a, b, dims):
    return lax.dot_general(a.astype(bf16), b.astype(bf16), (dims, ((), ())), preferred_element_type=f32)


_NN = ((1,), (0,))
_NT = ((1,), (1,))
_TN = ((0,), (0,))


@jax.custom_vjp
def bdot_nn(a, b):
    return _dot(a, b, _NN)


def _bdot_nn_fwd(a, b):
    return _dot(a, b, _NN), (a, b)


def _bdot_nn_bwd(res, g):
    a, b = res
    return _dot(g, b, _NT), _dot(a, g, _TN)


bdot_nn.defvjp(_bdot_nn_fwd, _bdot_nn_bwd)


@jax.custom_vjp
def bdot_nt(a, b):
    return _dot(a, b, _NT)


def _bdot_nt_fwd(a, b):
    return _dot(a, b, _NT), (a, b)


def _bdot_nt_bwd(res, g):
    a, b = res
    return _dot(g, b, _NN), _dot(g, a, _TN)


bdot_nt.defvjp(_bdot_nt_fwd, _bdot_nt_bwd)


def _rms(x, g, denom):
    r = lax.rsqrt(jnp.sum(x * x, axis=-1, keepdims=True) * (1.0 / denom) + NORM_EPS)
    return x * r * g


def _opspec(block, row_of, col_of, shards, ncol_tiles):
    if shards == 1:
        return pl.BlockSpec(block, lambda i, j, k: (row_of(i, j, k), col_of(i, j, k)))
    per = ncol_tiles // shards
    return pl.BlockSpec((None,) + block,
                        lambda i, j, k: (col_of(i, j, k) // per, row_of(i, j, k), col_of(i, j, k) % per))


def _call_with_exchange(body, args, exchange, *, name, grid, in_specs, out_specs, out_shape, scratch_shapes=()):
    xs, scatter = exchange
    n, n_in, n_out, n_scr = len(xs), len(in_specs), len(out_specs), len(scratch_shapes)
    x_shapes, sems = _chip_exchange_args(xs, scatter)

    def wrapped(*refs):
        ins, x_ins = refs[:n_in], refs[n_in:n_in + n]
        outs, x_outs = refs[n_in + n:n_in + n + n_out], refs[n_in + n + n_out:n_in + 2 * n + n_out]
        scratch, x_sems = refs[n_in + 2 * n + n_out:n_in + 2 * n + n_out + n_scr], refs[n_in + 2 * n + n_out + n_scr:]
        first, last = True, True
        for d, steps in enumerate(grid):
            first = first & (pl.program_id(d) == 0)
            last = last & (pl.program_id(d) == steps - 1)

        @pl.when(first)
        def _():
            _chip_exchange(x_ins, x_outs, x_sems, scatter, True)

        body(*ins, *outs, *scratch)

        @pl.when(last)
        def _():
            _chip_exchange(x_ins, x_outs, x_sems, scatter, False)

    res = pl.pallas_call(
        wrapped, name=name, grid=grid, in_specs=list(in_specs) + [ANY] * n, out_specs=list(out_specs) + [ANY] * n,
        out_shape=list(out_shape) + x_shapes, scratch_shapes=list(scratch_shapes) + sems,
        compiler_params=_cp(("arbitrary",) * len(grid)),
    )(*args, *xs)
    return res[:n_out], res[n_out:]


def mm(a, b, *, mode, name, tm=512, tn=512, tk=512, pro="none", epi="none", aux=None,
       out_dtype=f32, a_shards=1, b_shards=1, out_shards=1, exchange=None):
    ar, ac = a.shape[-2], a.shape[-1] * a_shards
    br, bc = b.shape[-2], b.shape[-1] * b_shards
    if mode == "nn":
        M, K, N = ar, ac, bc
        assert br == K
    elif mode == "nt":
        M, K, N = ar, ac, br
        assert bc == K
    else:
        M, K, N = ac, ar, bc
        assert br == K
    tm, tn, tk = _tile(M, tm), _tile(N, tn), _tile(K, tk)
    if a_shards > 1:
        if mode == "tn":
            tm = _tile(ac // a_shards, tm)
        else:
            tk = _tile(ac // a_shards, tk)
    if b_shards > 1:
        if mode == "nt":
            tk = _tile(bc // b_shards, tk)
        else:
            tn = _tile(bc // b_shards, tn)
    if out_shards > 1:
        tn = _tile(N // out_shards, tn)
    nm, nn_, nk = M // tm, N // tn, K // tk
    I = lambda i, j, k: i
    J = lambda i, j, k: j
    Kk = lambda i, j, k: k
    if mode == "nn":
        a_spec = _opspec((tm, tk), I, Kk, a_shards, nk)
        b_spec = _opspec((tk, tn), Kk, J, b_shards, nn_)
        dims = _NN
    elif mode == "nt":
        a_spec = _opspec((tm, tk), I, Kk, a_shards, nk)
        b_spec = _opspec((tn, tk), J, Kk, b_shards, nk)
        dims = _NT
    else:
        a_spec = _opspec((tk, tm), Kk, I, a_shards, nm)
        b_spec = _opspec((tk, tn), Kk, J, b_shards, nn_)
        dims = _TN
    in_specs = [a_spec, b_spec]
    args = [a, b]
    if epi != "none":
        in_specs.append(pl.BlockSpec((tm, tn), lambda i, j, k: (i, j)))
        args.append(aux)
    if out_shards == 1:
        out_spec = pl.BlockSpec((tm, tn), lambda i, j, k: (i, j))
        out_shape = jax.ShapeDtypeStruct((M, N), out_dtype)
    else:
        per = nn_ // out_shards
        out_spec = pl.BlockSpec((None, tm, tn), lambda i, j, k: (j // per, i, j % per))
        out_shape = jax.ShapeDtypeStruct((out_shards, M, N // out_shards), out_dtype)

    def body(*refs):
        a_ref, b_ref = refs[0], refs[1]
        pos = 2
        if epi != "none":
            aux_ref = refs[pos]
            pos += 1
        o_ref, acc_ref = refs[pos], refs[pos + 1]
        k = pl.program_id(2)

        @pl.when(k == 0)
        def _():
            acc_ref[...] = jnp.zeros_like(acc_ref)

        av = a_ref[...]
        if pro == "relu2":
            av = jnp.square(jnp.maximum(av.astype(f32), 0.0))
        acc_ref[...] += _dot(av, b_ref[...], dims)

        @pl.when(k == nk - 1)
        def _():
            res = acc_ref[...]
            if epi == "add":
                res = res + aux_ref[...].astype(f32)
            elif epi == "mul2relu":
                res = res * (2.0 * jnp.maximum(aux_ref[...].astype(f32), 0.0))
            o_ref[...] = res.astype(out_dtype)

    acc = [pltpu.VMEM((tm, tn), f32)]
    if exchange is not None:
        (out,), moved = _call_with_exchange(body, args, exchange, name=name, grid=(nm, nn_, nk), in_specs=in_specs,
                                            out_specs=[out_spec], out_shape=[out_shape], scratch_shapes=acc)
        return out, moved
    return pl.pallas_call(
        body, name=name, grid=(nm, nn_, nk), in_specs=in_specs, out_specs=out_spec, out_shape=out_shape,
        scratch_shapes=acc, compiler_params=_cp(("parallel", "parallel", "arbitrary")),
    )(*args)


def rms_norm(x, g, denom, *, name, ts=512, exchange=None):
    S, D = x.shape
    ts = _tile(S, ts)

    def body(x_ref, g_ref, h_ref):
        h_ref[...] = _rms(x_ref[...], g_ref[...], denom).astype(bf16)

    row = pl.BlockSpec((ts, D), lambda i: (i, 0))
    kw = dict(name=name, grid=(S // ts,), in_specs=[row, pl.BlockSpec((1, D), lambda i: (0, 0))])
    if exchange is not None:
        (h,), moved = _call_with_exchange(body, (x, g), exchange, out_specs=[row],
                                          out_shape=[jax.ShapeDtypeStruct((S, D), bf16)], **kw)
        return h, moved
    return pl.pallas_call(body, out_specs=row, out_shape=jax.ShapeDtypeStruct((S, D), bf16),
                          compiler_params=_cp(("parallel",)), **kw)(x, g)


def rms_bwd(x, g, dy, res, denom, *, name, ts=256, twin=False):
    S, D = x.shape
    ts = _tile(S, ts)
    has_res = res is not None

    def body(*refs):
        x_ref, g_ref, dy_ref = refs[:3]
        outs = refs[4:] if has_res else refs[3:]
        _, vjp = jax.vjp(lambda xv, gv: _rms(xv, gv, denom), x_ref[...], g_ref[...])
        dx, dg = vjp(dy_ref[...])
        if has_res:
            dx = dx + refs[3][...]
        outs[0][...] = dx
        if twin:
            outs[2][...] = dx.astype(bf16)
        dg_ref = outs[1]

        @pl.when(pl.program_id(0) == 0)
        def _():
            dg_ref[...] = jnp.zeros_like(dg_ref)

        dg_ref[...] += dg

    row = pl.BlockSpec((ts, D), lambda i: (i, 0))
    vec = pl.BlockSpec((1, D), lambda i: (0, 0))
    in_specs = [row, vec, row] + ([row] if has_res else [])
    args = [x, g, dy] + ([res] if has_res else [])
    return pl.pallas_call(
        body, name=name, grid=(S // ts,), in_specs=in_specs, out_specs=[row, vec] + ([row] if twin else []),
        out_shape=[jax.ShapeDtypeStruct((S, D), f32), jax.ShapeDtypeStruct((1, D), f32)]
        + ([jax.ShapeDtypeStruct((S, D), bf16)] if twin else []),
        compiler_params=_cp(("arbitrary",)),
    )(*args)


LOG2E = 1.4426950408889634
LN2 = 0.6931471805599453


def _qk_fn(q, k, gq, gk):
    qs, ks = [], []
    for h in range(SB_HEADS):
        sl = slice(h * LANES, (h + 1) * LANES)
        qs.append(_rms(q[:, sl], gq, SB_HEAD_DIM) * (SB_HEAD_DIM ** -0.5 * LOG2E))
        ks.append(_rms(k[:, sl], gk, SB_HEAD_DIM))
    return jnp.concatenate(qs, axis=1), jnp.concatenate(ks, axis=1)


def qkv_prep(qkv, gq, gk, *, ts=256):
    S = qkv.shape[0]
    W = SB_HEADS * LANES
    ts = _tile(S, ts)

    def body(q_ref, k_ref, v_ref, gq_ref, gk_ref, qn_ref, kn_ref, vb_ref):
        qn, kn = _qk_fn(q_ref[...], k_ref[...], gq_ref[...], gk_ref[...])
        qn_ref[...] = qn.astype(bf16)
        kn_ref[...] = kn.astype(bf16)
        vb_ref[...] = v_ref[...].astype(bf16)

    out = jax.ShapeDtypeStruct((S, W), bf16)
    gspec = pl.BlockSpec((1, LANES), lambda i: (0, 0))
    ospec = pl.BlockSpec((ts, W), lambda i: (i, 0))
    col = lambda c: pl.BlockSpec((ts, W), lambda i: (i, c))
    return pl.pallas_call(
        body, name="qkv_prep", grid=(S // ts,), in_specs=[col(0), col(1), col(2), gspec, gspec],
        out_specs=[ospec, ospec, ospec], out_shape=[out, out, out], compiler_params=_cp(("parallel",)),
    )(qkv, qkv, qkv, gq, gk)


def qkv_bwd(qkv, gq, gk, dqn, dkn, dv, exchange, *, ts=256):
    S = qkv.shape[0]
    W = SB_HEADS * LANES
    ts = _tile(S, ts)

    def body(q_ref, k_ref, gq_ref, gk_ref, dqn_ref, dkn_ref, dv_ref, o_ref, dgq_ref, dgk_ref):
        _, vjp = jax.vjp(_qk_fn, q_ref[...], k_ref[...], gq_ref[...], gk_ref[...])
        dq, dk, dgq, dgk = vjp((dqn_ref[...] * LN2, dkn_ref[...] * LN2))
        o_ref[:, 0:W] = dq.astype(bf16)
        o_ref[:, W:2 * W] = dk.astype(bf16)
        o_ref[:, 2 * W:3 * W] = dv_ref[...].astype(bf16)

        @pl.when(pl.program_id(0) == 0)
        def _():
            dgq_ref[...] = jnp.zeros_like(dgq_ref)
            dgk_ref[...] = jnp.zeros_like(dgk_ref)

        dgq_ref[...] += dgq
        dgk_ref[...] += dgk

    gspec = pl.BlockSpec((1, LANES), lambda i: (0, 0))
    row = pl.BlockSpec((ts, W), lambda i: (i, 0))
    col = lambda c: pl.BlockSpec((ts, W), lambda i: (i, c))
    (dqkv, dgq, dgk), moved = _call_with_exchange(
        body, (qkv, qkv, gq, gk, dqn, dkn, dv), exchange, name="qkv_bwd", grid=(S // ts,),
        in_specs=[col(0), col(1), gspec, gspec, row, row, row],
        out_specs=[pl.BlockSpec((ts, 3 * W), lambda i: (i, 0)), gspec, gspec],
        out_shape=[jax.ShapeDtypeStruct((S, 3 * W), bf16), jax.ShapeDtypeStruct((1, LANES), f32),
                   jax.ShapeDtypeStruct((1, LANES), f32)])
    return dqkv, dgq, dgk, moved


def _sb_weights(q, ks, R, masked, row, col, UU):
    ls = [_dot(q, k, _NT) for k in ks]
    lbs, lm0s, cats = [], [], []
    for l, diag in zip(ls, masked):
        neg_abs = pltpu.bitcast(pltpu.bitcast(l, jnp.uint32) | jnp.uint32(0x80000000), f32)
        lp = jnp.log2(1.0 + jnp.exp2(neg_abs))
        lb = jnp.minimum(l, 0.0) - lp
        lm = lb - l
        if diag:
            lm = jnp.where(col < row, lm, 0.0)
        hi = lm.astype(bf16)
        lo = (lm - hi.astype(f32)).astype(bf16)
        lbs.append(lb)
        lm0s.append(lm[:, 0:1])
        cats.append(jnp.concatenate([hi, lo], axis=1))
    sums = [_dot(c, UU, _NN) for c in cats]
    ws = []
    for lb, lm0, A, diag in zip(lbs, lm0s, sums, masked):
        w = jnp.exp2(lb + (A + R))
        if diag:
            w = jnp.where(col < row, w, 0.0)
        R = R + (A[:, 0:1] + lm0)
        ws.append(w)
    return lbs, ws, R


def _tri2(tk):
    r = lax.broadcasted_iota(jnp.int32, (2 * tk, tk), 0)
    r = jnp.where(r >= tk, r - tk, r)
    c = lax.broadcasted_iota(jnp.int32, (2 * tk, tk), 1)
    return (r > c).astype(bf16)


SB_GROUP = 8


SB_ALL_ZERO_BELOW = -160.0


def _sweep(i, blocks_of, carry, descending, right_sum=None, ran=None):
    G = SB_GROUP
    n = jnp.maximum(i - 1, 0)
    rem, full = n % G, n // G
    asc = lambda js: js if descending else js[::-1]

    def first_group(c):
        one = lambda c: blocks_of([i], c, [True])
        two = lambda c: blocks_of(asc([i, i - 1]), c, asc([True, False]))
        return lax.cond(i >= 1, two, one, c)

    def body(p, c):
        return blocks_of(asc([i - 2 - p * G - u for u in range(G)]), c, [False] * G)

    def left_over(r):
        return lambda c: blocks_of(asc([r - 1 - u for u in range(r)]), c, [False] * r) if r else c

    if descending:
        carry = first_group(carry)
        alive = lambda c: jnp.max(right_sum(c)) > SB_ALL_ZERO_BELOW
        bodies, carry = lax.while_loop(lambda s: (s[0] < full) & alive(s[1]),
                                       lambda s: (s[0] + 1, body(s[0], s[1])), (jnp.int32(0), carry))
        tail = (bodies == full) & alive(carry)
        carry = lax.switch(jnp.where(tail, rem, 0), [left_over(r) for r in range(G)], carry)
        return carry, (bodies, tail)
    bodies, tail = ran
    carry = lax.switch(jnp.where(tail, rem, 0), [left_over(r) for r in range(G)], carry)
    carry = lax.fori_loop(0, bodies, lambda t, c: body(bodies - 1 - t, c), carry)
    return first_group(carry)


def sb_fwd(qn, kn, vb, exchange, *, tq=256):
    S, W = qn.shape
    H = W // LANES
    tq = _tile(S, tq)
    tk = tq
    nq = S // tq

    def body(q_ref, k_ref, v_ref, o_ref, uu_s):
        i = pl.program_id(1)

        @pl.when((pl.program_id(0) == 0) & (i == 0))
        def _():
            uu_s[...] = _tri2(tk)

        q = q_ref[...]
        row = lax.broadcasted_iota(jnp.int32, (tq, tk), 0)
        col = lax.broadcasted_iota(jnp.int32, (tq, tk), 1)
        UU = uu_s[...]

        def blocks(js, c, masked):
            rows = [pl.ds(pl.multiple_of(j * tk, tk), tk) for j in js]
            _, ws, R = _sb_weights(q, [k_ref[r, :] for r in rows], c[0], masked, row, col, UU)
            acc = c[1]
            for w, r in zip(ws, rows):
                acc = acc + _dot(w, v_ref[r, :], _NN)
            return R, acc

        c, _ = _sweep(i, blocks, (jnp.zeros((tq, 1), f32), jnp.zeros((tq, LANES), f32)), True, lambda c: c[0])
        o_ref[...] = c[1]

    qspec = pl.BlockSpec((tq, LANES), lambda h, i: (i, h))
    kspec = pl.BlockSpec((S, LANES), lambda h, i: (0, h))
    (o,), moved = _call_with_exchange(
        body, (qn, kn, vb), exchange, name="sb_fwd", grid=(H, nq), in_specs=[qspec, kspec, kspec], out_specs=[qspec],
        out_shape=[jax.ShapeDtypeStruct((S, W), f32)], scratch_shapes=[pltpu.VMEM((2 * tk, tk), bf16)])
    return o, moved


def sb_bwd(qn, kn, vb, do, exchange, *, tq=256):
    S, W = qn.shape
    H = W // LANES
    tq = _tile(S, tq)
    tk = tq
    nq = S // tq

    def body(q_ref, k_ref, v_ref, do_ref, dq_ref, dk_ref, dv_ref, dz_s, beta_s, uu_s, ue_s):
        i = pl.program_id(1)
        row = lax.broadcasted_iota(jnp.int32, (tq, tk), 0)
        col = lax.broadcasted_iota(jnp.int32, (tq, tk), 1)

        @pl.when((pl.program_id(0) == 0) & (i == 0))
        def _():
            uu_s[...] = _tri2(tk)
            ue_s[...] = (row < col).astype(bf16)

        @pl.when(i == 0)
        def _():
            dk_ref[...] = jnp.zeros_like(dk_ref)
            dv_ref[...] = jnp.zeros_like(dv_ref)

        q = q_ref[...]
        dob = do_ref[...].astype(bf16)
        UU = uu_s[...]
        Ue = ue_s[...]

        def sweep1(js, R, masked):
            rows = [pl.ds(pl.multiple_of(j * tk, tk), tk) for j in js]
            dws = [_dot(dob, v_ref[r, :], _NT) for r in rows]
            lbs, ws, R = _sb_weights(q, [k_ref[r, :] for r in rows], R, masked, row, col, UU)
            for j, lb, w, dw in zip(js, lbs, ws, dws):
                dz_s[j] = (dw * w).astype(bf16)
                beta_s[j] = jnp.exp2(lb).astype(bf16)
            for r, w in zip(rows, ws):
                dv_ref[r, :] += _dot(w, dob, _TN)
            return R

        _, ran = _sweep(i, sweep1, jnp.zeros((tq, 1), f32), True, lambda R: R)

        def sweep2(js, c, masked):
            rows = [pl.ds(pl.multiple_of(j * tk, tk), tk) for j in js]
            dzbs = [dz_s[j] for j in js]
            sums = [_dot(dzb, Ue, _NN) for dzb in dzbs]
            Lz, dq = c
            dlbs = []
            for j, dzb, Cz, diag in zip(js, dzbs, sums, masked):
                dz = dzb.astype(f32)
                dl = dz - beta_s[j].astype(f32) * (dz + (Cz + Lz))
                if diag:
                    dl = jnp.where(col < row, dl, 0.0)
                Lz = Lz + (Cz[:, tk - 1:tk] + dz[:, tk - 1:tk])
                dlbs.append(dl.astype(bf16))
            for r, dlb in zip(rows, dlbs):
                dq = dq + _dot(dlb, k_ref[r, :], _NN)
            for r, dlb in zip(rows, dlbs):
                dk_ref[r, :] += _dot(dlb, q, _TN)
            return Lz, dq

        c = _sweep(i, sweep2, (jnp.zeros((tq, 1), f32), jnp.zeros((tq, LANES), f32)), False, ran=ran)
        dq_ref[...] = c[1]

    qspec = pl.BlockSpec((tq, LANES), lambda h, i: (i, h))
    kspec = pl.BlockSpec((S, LANES), lambda h, i: (0, h))
    full = jax.ShapeDtypeStruct((S, W), f32)
    (dq, dk, dv), moved = _call_with_exchange(
        body, (qn, kn, vb, do), exchange, name="sb_bwd", grid=(H, nq), in_specs=[qspec, kspec, kspec, qspec],
        out_specs=[qspec, kspec, kspec], out_shape=[full, full, full],
        scratch_shapes=[pltpu.VMEM((nq, tq, tk), bf16), pltpu.VMEM((nq, tq, tk), bf16),
                        pltpu.VMEM((2 * tk, tk), bf16), pltpu.VMEM((tk, tk), bf16)])
    return dq, dk, dv, moved


def _xa_fn(qx, kv, gq, gk):
    XW = XA_HEADS * XA_HEAD_DIM
    outs = []
    for h in range(XA_HEADS):
        sl = slice(h * XA_HEAD_DIM, (h + 1) * XA_HEAD_DIM)
        qn = _rms(qx[:, sl], gq, XA_HEAD_DIM)
        kn = _rms(kv[:, sl], gk, XA_HEAD_DIM)
        v = kv[:, XW + h * XA_HEAD_DIM:XW + (h + 1) * XA_HEAD_DIM]
        s = bdot_nt(qn, kn) * (XA_HEAD_DIM ** -0.5)
        e = jnp.exp(s - lax.stop_gradient(jnp.max(s, axis=-1, keepdims=True)))
        p = e / jnp.sum(e, axis=-1, keepdims=True)
        outs.append(bdot_nn(p, v))
    return jnp.concatenate(outs, axis=1)


def xa_fwd(qx, kv, gq, gk, *, ts=256):
    S, XW = qx.shape
    M = kv.shape[0]
    ts = _tile(S, ts)

    def body(q_ref, kv_ref, gq_ref, gk_ref, o_ref):
        o_ref[...] = _xa_fn(q_ref[...], kv_ref[...], gq_ref[...], gk_ref[...]).astype(bf16)

    row = pl.BlockSpec((ts, XW), lambda i: (i, 0))
    gspec = pl.BlockSpec((1, XA_HEAD_DIM), lambda i: (0, 0))
    return pl.pallas_call(
        body, name="xa_fwd", grid=(S // ts,),
        in_specs=[row, pl.BlockSpec((M, 2 * XW), lambda i: (0, 0)), gspec, gspec], out_specs=row,
        out_shape=jax.ShapeDtypeStruct((S, XW), bf16), compiler_params=_cp(("parallel",)),
    )(qx, kv, gq, gk)


def xa_bwd(qx, kv, gq, gk, do, *, ts=256):
    S, XW = qx.shape
    M = kv.shape[0]
    ts = _tile(S, ts)

    def body(q_ref, kv_ref, gq_ref, gk_ref, do_ref, dq_ref, dkv_ref, dgq_ref, dgk_ref):
        _, vjp = jax.vjp(_xa_fn, q_ref[...], kv_ref[...], gq_ref[...], gk_ref[...])
        dq, dkv, dgq, dgk = vjp(do_ref[...].astype(f32))
        dq_ref[...] = dq.astype(bf16)

        @pl.when(pl.program_id(0) == 0)
        def _():
            dkv_ref[...] = jnp.zeros_like(dkv_ref)
            dgq_ref[...] = jnp.zeros_like(dgq_ref)
            dgk_ref[...] = jnp.zeros_like(dgk_ref)

        dkv_ref[...] += dkv
        dgq_ref[...] += dgq
        dgk_ref[...] += dgk

    row = pl.BlockSpec((ts, XW), lambda i: (i, 0))
    gspec = pl.BlockSpec((1, XA_HEAD_DIM), lambda i: (0, 0))
    kvspec = pl.BlockSpec((M, 2 * XW), lambda i: (0, 0))
    gshape = jax.ShapeDtypeStruct((1, XA_HEAD_DIM), f32)
    return pl.pallas_call(
        body, name="xa_bwd", grid=(S // ts,), in_specs=[row, kvspec, gspec, gspec, row],
        out_specs=[row, kvspec, gspec, gspec],
        out_shape=[jax.ShapeDtypeStruct((S, XW), bf16), jax.ShapeDtypeStruct((M, 2 * XW), f32), gshape, gshape],
        compiler_params=_cp(("arbitrary",)),
    )(qx, kv, gq, gk, do)


def _s5_prep_fn(a_re, a_im, ldt, bT_re, bT_im, cT_re, cT_im):
    G, P, C = SSM_GROUPS, SSM_STATE, SSM_GROUP
    GP, GC = G * P, G * C
    lg_p, lg_c = P.bit_length() - 1, C.bit_length() - 1
    gi = lax.broadcasted_iota(jnp.int32, (G, GP), 0)
    ci = lax.broadcasted_iota(jnp.int32, (G, GP), 1) >> lg_p
    expand_dt = (gi == ci).astype(f32)
    dte = jnp.dot(jnp.exp(ldt), expand_dt, precision=lax.Precision.HIGHEST, preferred_element_type=f32)
    zr, zi = a_re * dte, a_im * dte
    mag = jnp.exp(zr)
    abr, abi = mag * jnp.cos(zi), mag * jnp.sin(zi)
    nr, ni = abr - 1.0, abi
    den = a_re * a_re + a_im * a_im
    cr = (nr * a_re + ni * a_im) / den
    cim = (ni * a_re - nr * a_im) / den
    bbr = cr * bT_re - cim * bT_im
    bbi = cr * bT_im + cim * bT_re
    rowg = lax.broadcasted_iota(jnp.int32, (GC, GP), 0) >> lg_c
    colg = lax.broadcasted_iota(jnp.int32, (GC, GP), 1) >> lg_p
    diag = rowg == colg

    def expand(t):
        return jnp.where(diag, jnp.broadcast_to(t[None], (G, C, GP)).reshape(GC, GP), 0.0)

    return abr, abi, expand(bbr), expand(bbi), expand(cT_re), expand(-cT_im)


def s5_prep(a_re, a_im, ldt, bT_re, bT_im, cT_re, cT_im):
    GP, GC = SSM_GROUPS * SSM_STATE, SSM_GROUPS * SSM_GROUP

    def body(a_re_ref, a_im_ref, ldt_ref, bTr_ref, bTi_ref, cTr_ref, cTi_ref, abr_ref, abi_ref, B_ref, C_ref):
        abr, abi, Br, Bi, Cr, Ci = _s5_prep_fn(a_re_ref[...], a_im_ref[...], ldt_ref[...], bTr_ref[...],
                                               bTi_ref[...], cTr_ref[...], cTi_ref[...])
        abr_ref[...] = abr
        abi_ref[...] = abi
        B_ref[0] = Br.astype(bf16)
        B_ref[1] = Bi.astype(bf16)
        C_ref[0] = Cr.astype(bf16)
        C_ref[1] = Ci.astype(bf16)

    vec = jax.ShapeDtypeStruct((1, GP), f32)
    mat = jax.ShapeDtypeStruct((2, GC, GP), bf16)
    return pl.pallas_call(body, name="s5_prep", out_shape=[vec, vec, mat, mat], compiler_params=_cp())(
        a_re, a_im, ldt, bT_re, bT_im, cT_re, cT_im)


def s5_prep_bwd(a_re, a_im, ldt, bT_re, bT_im, cT_re, cT_im, dabr, dabi, dB, dC):
    def body(a_re_ref, a_im_ref, ldt_ref, bTr_ref, bTi_ref, cTr_ref, cTi_ref, dabr_ref, dabi_ref, dB_ref, dC_ref,
             *outs):
        _, vjp = jax.vjp(_s5_prep_fn, a_re_ref[...], a_im_ref[...], ldt_ref[...], bTr_ref[...], bTi_ref[...],
                         cTr_ref[...], cTi_ref[...])
        grads = vjp((dabr_ref[...], dabi_ref[...], dB_ref[0], dB_ref[1], dC_ref[0], dC_ref[1]))
        for o_ref, gv in zip(outs, grads):
            o_ref[...] = gv

    ins = (a_re, a_im, ldt, bT_re, bT_im, cT_re, cT_im)
    return pl.pallas_call(body, name="s5_prep_bwd", out_shape=[jax.ShapeDtypeStruct(v.shape, f32) for v in ins],
                          compiler_params=_cp())(*ins, dabr, dabi, dB, dC)


def _cmul(ar, ai, br, bi):
    return ar * br - ai * bi, ar * bi + ai * br


SCAN_CHUNKS = 32


def _chunk_carry(Lr, Li, Pr, Pi, scratch, reverse):
    lr_ref, li_ref, cr_ref, ci_ref = scratch
    lr_ref[...] = Lr
    li_ref[...] = Li
    cur_r = jnp.zeros((1, LANES), f32)
    cur_i = jnp.zeros((1, LANES), f32)
    order = range(SCAN_CHUNKS - 1, -1, -1) if reverse else range(SCAN_CHUNKS)
    for c in order:
        cr_ref[pl.ds(c, 1), :] = cur_r
        ci_ref[pl.ds(c, 1), :] = cur_i
        mr, mi = _cmul(Pr, Pi, cur_r, cur_i)
        cur_r, cur_i = lr_ref[pl.ds(c, 1), :] + mr, li_ref[pl.ds(c, 1), :] + mi
    return cr_ref[...], ci_ref[...]


def _chunk_rows(j):
    return pl.ds(pl.multiple_of(j * SCAN_CHUNKS, SCAN_CHUNKS), SCAN_CHUNKS)


def row_shuffle(x, a, b, *, name, add=None, out_dtype=f32):
    S, W = x.shape
    assert a * b == S and x.dtype == f32

    def body(*refs):
        x_ref, o_ref = refs[0], refs[-1]

        def step(i, _):
            dst = pl.ds(pl.multiple_of(i * b, b), b)
            v = x_ref[pl.ds(i, b, stride=a), :]
            if add is not None:
                v = v + refs[1][dst, :]
            o_ref[dst, :] = v.astype(out_dtype)
            return 0

        lax.fori_loop(0, a, step, 0)

    col = pl.BlockSpec((S, LANES), lambda t: (0, t))
    args = [x] + ([add] if add is not None else [])
    return pl.pallas_call(
        body, name=name, grid=(W // LANES,), in_specs=[col] * len(args), out_specs=col,
        out_shape=jax.ShapeDtypeStruct((S, W), out_dtype), compiler_params=_cp(("parallel",)),
    )(*args)


def _scan_scratch(n):
    small = pltpu.VMEM((SCAN_CHUNKS, LANES), f32)
    return [pltpu.VMEM((n, LANES), f32), pltpu.VMEM((n, LANES), f32), small, small, small, small]


def _drive(x_ref, m_ref, work_ref):
    S = x_ref.shape[0]
    rows = min(S, 1024)
    m = jnp.concatenate([m_ref[0], m_ref[1]], axis=1)

    def chunk(c, _):
        r = pl.ds(pl.multiple_of(c * rows, rows), rows)
        y = _dot(x_ref[r, :], m, _NN)
        work_ref[0, r, :] = y[:, :LANES]
        work_ref[1, r, :] = y[:, LANES:]
        return 0

    lax.fori_loop(0, S // rows, chunk, 0)


def scan_fwd(u_il, Bm, abr, abi, exchange):
    S, C = u_il.shape
    N = Bm.shape[2]
    n = S // SCAN_CHUNKS
    shp = (SCAN_CHUNKS, LANES)

    def body(u_ref, B_ref, ar_ref, ai_ref, st_ref, work_ref, pwr_ref, pwi_ref, *scratch):
        a1r, a1i = ar_ref[...], ai_ref[...]
        ar = jnp.broadcast_to(a1r, shp)
        ai = jnp.broadcast_to(a1i, shp)
        _drive(u_ref, B_ref, work_ref)
        sr_ref, si_ref = work_ref.at[0], work_ref.at[1]

        def step(j, c):
            sr, si, pr, pi = c
            rows = _chunk_rows(j)
            mr, mi = _cmul(ar, ai, sr, si)
            sr, si = mr + sr_ref[rows, :], mi + si_ref[rows, :]
            sr_ref[rows, :] = sr
            si_ref[rows, :] = si
            pwr_ref[pl.ds(j, 1), :] = pr
            pwi_ref[pl.ds(j, 1), :] = pi
            npr, npi = _cmul(a1r, a1i, pr, pi)
            return sr, si, npr, npi

        z = jnp.zeros(shp, f32)
        sr, si, _, _ = lax.fori_loop(0, n, step, (z, z, a1r, a1i), unroll=2)
        cr, ci = _chunk_carry(sr, si, pwr_ref[pl.ds(n - 1, 1), :], pwi_ref[pl.ds(n - 1, 1), :], scratch, False)

        def step2(j, _):
            rows = _chunk_rows(j)
            pr = jnp.broadcast_to(pwr_ref[pl.ds(j, 1), :], shp)
            pi = jnp.broadcast_to(pwi_ref[pl.ds(j, 1), :], shp)
            mr, mi = _cmul(pr, pi, cr, ci)
            st_ref[0, rows, :] = (sr_ref[rows, :] + mr).astype(bf16)
            st_ref[1, rows, :] = (si_ref[rows, :] + mi).astype(bf16)
            return 0

        lax.fori_loop(0, n, step2, 0, unroll=4)

    blk = pl.BlockSpec((2, S, LANES), lambda t: (0, 0, t))
    vec = pl.BlockSpec((1, LANES), lambda t: (0, t))
    (st,), moved = _call_with_exchange(
        body, (u_il, Bm, abr, abi), exchange, name="scan_fwd", grid=(N // LANES,),
        in_specs=[pl.BlockSpec((S, C), lambda t: (0, 0)), pl.BlockSpec((2, C, LANES), lambda t: (0, 0, t)), vec, vec],
        out_specs=[blk], out_shape=[jax.ShapeDtypeStruct((2, S, N), bf16)],
        scratch_shapes=[pltpu.VMEM((2, S, LANES), f32)] + _scan_scratch(n))
    return st, moved


def scan_bwd(dy_il, Cm, st, abr, abi, exchange):
    _, S, N = st.shape
    C = dy_il.shape[1]
    n = S // SCAN_CHUNKS
    shp = (SCAN_CHUNKS, LANES)

    def body(dy_ref, C_ref, st_ref, ar_ref, ai_ref, g_ref, dar_ref, dai_ref, work_ref, qwr_ref, qwi_ref, *scratch):
        a1r, a1i = ar_ref[...], -ai_ref[...]
        ar = jnp.broadcast_to(a1r, shp)
        nai = jnp.broadcast_to(a1i, shp)
        _drive(dy_ref, C_ref, work_ref)
        gr_ref, gi_ref = work_ref.at[0], work_ref.at[1]
        sr_ref, si_ref = st_ref.at[0], st_ref.at[1]

        def step(jj, c):
            gr, gi, qr, qi = c
            j = n - 1 - jj
            rows = _chunk_rows(j)
            mr, mi = _cmul(ar, nai, gr, gi)
            gr, gi = mr + gr_ref[rows, :], mi + gi_ref[rows, :]
            gr_ref[rows, :] = gr
            gi_ref[rows, :] = gi
            qwr_ref[pl.ds(j, 1), :] = qr
            qwi_ref[pl.ds(j, 1), :] = qi
            nqr, nqi = _cmul(a1r, a1i, qr, qi)
            return gr, gi, nqr, nqi

        z = jnp.zeros(shp, f32)
        gr, gi, _, _ = lax.fori_loop(0, n, step, (z, z, a1r, a1i), unroll=2)
        cr, ci = _chunk_carry(gr, gi, qwr_ref[pl.ds(0, 1), :], qwi_ref[pl.ds(0, 1), :], scratch, True)
        sub = lax.broadcasted_iota(jnp.int32, shp, 0)

        def fix(j, spr, spi, acc):
            rows = _chunk_rows(j)
            qr = jnp.broadcast_to(qwr_ref[pl.ds(j, 1), :], shp)
            qi = jnp.broadcast_to(qwi_ref[pl.ds(j, 1), :], shp)
            mr, mi = _cmul(qr, qi, cr, ci)
            gr = gr_ref[rows, :] + mr
            gi = gi_ref[rows, :] + mi
            g_ref[0, rows, :] = gr.astype(bf16)
            g_ref[1, rows, :] = gi.astype(bf16)
            return acc[0] + gr * spr + gi * spi, acc[1] + gi * spr - gr * spi

        last = _chunk_rows(n - 1)
        spr = jnp.where(sub == 0, 0.0, pltpu.roll(sr_ref[last, :].astype(f32), 1, 0))
        spi = jnp.where(sub == 0, 0.0, pltpu.roll(si_ref[last, :].astype(f32), 1, 0))
        acc = fix(0, spr, spi, (z, z))

        def step2(j, acc):
            prev = _chunk_rows(j - 1)
            return fix(j, sr_ref[prev, :].astype(f32), si_ref[prev, :].astype(f32), acc)

        acc = lax.fori_loop(1, n, step2, acc)
        dar_ref[...] = jnp.sum(acc[0], axis=0, keepdims=True)
        dai_ref[...] = jnp.sum(acc[1], axis=0, keepdims=True)

    blk = pl.BlockSpec((2, S, LANES), lambda t: (0, 0, t))
    vec = pl.BlockSpec((1, LANES), lambda t: (0, t))
    vshape = jax.ShapeDtypeStruct((1, N), f32)
    (g, dar, dai), moved = _call_with_exchange(
        body, (dy_il, Cm, st, abr, abi), exchange, name="scan_bwd", grid=(N // LANES,),
        in_specs=[pl.BlockSpec((S, C), lambda t: (0, 0)), pl.BlockSpec((2, C, LANES), lambda t: (0, 0, t)), blk,
                  vec, vec],
        out_specs=[blk, vec, vec], out_shape=[jax.ShapeDtypeStruct((2, S, N), bf16), vshape, vshape],
        scratch_shapes=[pltpu.VMEM((2, S, LANES), f32)] + _scan_scratch(n))
    return g, dar, dai, moved


def _glu_fn(ypre, wglu):
    y = jax.nn.gelu(ypre)
    return y * jax.nn.sigmoid(bdot_nn(y, wglu))


def glu_fwd(ypre0, u, d, wglu, g_out, *, ts=512):
    S, W = u.shape
    ts = _tile(S, ts)

    def body(y0_ref, u_ref, d_ref, w_ref, g_ref, ypre_ref, z_ref, zn_ref):
        ypre = y0_ref[...] + d_ref[...] * u_ref[...]
        z = _glu_fn(ypre, w_ref[...])
        ypre_ref[...] = ypre
        z_ref[...] = z
        zn_ref[...] = _rms(z, g_ref[...], W).astype(bf16)

    row = pl.BlockSpec((ts, W), lambda i: (i, 0))
    vec = pl.BlockSpec((1, W), lambda i: (0, 0))
    full = jax.ShapeDtypeStruct((S, W), f32)
    return pl.pallas_call(
        body, name="glu_fwd", grid=(S // ts,),
        in_specs=[row, row, vec, pl.BlockSpec((W, W), lambda i: (0, 0)), vec], out_specs=[row, row, row],
        out_shape=[full, full, jax.ShapeDtypeStruct((S, W), bf16)], compiler_params=_cp(("parallel",)),
    )(ypre0, u, d, wglu, g_out)


def glu_bwd(ypre, u, d, wglu, dz, *, ts=512):
    S, W = u.shape
    ts = _tile(S, ts)

    def body(y_ref, u_ref, d_ref, w_ref, dz_ref, dy_ref, du_ref, dw_ref, dd_ref):
        _, vjp = jax.vjp(_glu_fn, y_ref[...], w_ref[...])
        dy, dw = vjp(dz_ref[...])
        dy_ref[...] = dy
        du_ref[...] = d_ref[...] * dy

        @pl.when(pl.program_id(0) == 0)
        def _():
            dw_ref[...] = jnp.zeros_like(dw_ref)
            dd_ref[...] = jnp.zeros_like(dd_ref)

        dw_ref[...] += dw
        dd_ref[...] += jnp.sum(dy * u_ref[...], axis=0, keepdims=True)

    row = pl.BlockSpec((ts, W), lambda i: (i, 0))
    vec = pl.BlockSpec((1, W), lambda i: (0, 0))
    sq = pl.BlockSpec((W, W), lambda i: (0, 0))
    full = jax.ShapeDtypeStruct((S, W), f32)
    return pl.pallas_call(
        body, name="glu_bwd", grid=(S // ts,), in_specs=[row, row, vec, sq, row], out_specs=[row, row, sq, vec],
        out_shape=[full, full, jax.ShapeDtypeStruct((W, W), f32), jax.ShapeDtypeStruct((1, W), f32)],
        compiler_params=_cp(("arbitrary",)),
    )(ypre, u, d, wglu, dz)


def loss_head(y, target, *, ts=512):
    S, D = y.shape
    ts = _tile(S, ts)

    def body(y_ref, t_ref, dy_ref, l_ref, dyb_ref):
        err = y_ref[...] - t_ref[...]
        dy_ref[...] = err * (1.0 / D)
        dyb_ref[...] = (err * (1.0 / D)).astype(bf16)

        @pl.when(pl.program_id(0) == 0)
        def _():
            l_ref[...] = jnp.zeros_like(l_ref)

        rows = jnp.sum(err * err, axis=1, keepdims=True) * (1.0 / D)
        l_ref[...] += 0.5 * jnp.sum(rows, axis=0, keepdims=True)

    row = pl.BlockSpec((ts, D), lambda i: (i, 0))
    return pl.pallas_call(
        body, name="loss_head", grid=(S // ts,), in_specs=[row, row],
        out_specs=[row, pl.BlockSpec((1, 1), lambda i: (0, 0)), row],
        out_shape=[jax.ShapeDtypeStruct((S, D), f32), jax.ShapeDtypeStruct((1, 1), f32),
                   jax.ShapeDtypeStruct((S, D), bf16)],
        compiler_params=_cp(("arbitrary",)),
    )(y, target)


def adamw(w, g, m, v, *, name, tr=256):
    R, C = w.shape
    tr = _row_tile(R, tr)

    def body(w_ref, g_ref, m_ref, v_ref, d_ref, nm_ref, nv_ref):
        gv = g_ref[...]
        nm = ADAM_B1 * m_ref[...] + (1.0 - ADAM_B1) * gv
        nv = ADAM_B2 * v_ref[...] + (1.0 - ADAM_B2) * jnp.square(gv)
        m_hat = nm / (1.0 - ADAM_B1 ** ADAM_STEP)
        v_hat = nv / (1.0 - ADAM_B2 ** ADAM_STEP)
        d_ref[...] = -ADAM_LR * (m_hat / (jnp.sqrt(v_hat) + ADAM_EPS) + ADAM_WD * w_ref[...])
        nm_ref[...] = nm
        nv_ref[...] = nv

    row = pl.BlockSpec((tr, C), lambda i: (i, 0))
    full = jax.ShapeDtypeStruct((R, C), f32)
    return pl.pallas_call(
        body, name=name, grid=(R // tr,), in_specs=[row] * 4, out_specs=[row] * 3, out_shape=[full] * 3,
        compiler_params=_cp(("parallel",)),
    )(w, g, m, v)


def add_half(g4, recv, c, *, name, tr=256):
    _, _, Rh, C = g4.shape
    tr = _row_tile(Rh, tr)

    def body(c_ref, a_ref, b_ref, o_ref):
        o_ref[...] = a_ref[...] + b_ref[...]

    grid_spec = pltpu.PrefetchScalarGridSpec(
        num_scalar_prefetch=1, grid=(N_CHIPS, Rh // tr),
        in_specs=[pl.BlockSpec((None, None, tr, C), lambda k, i, c_ref: (k, c_ref[0], i, 0)),
                  pl.BlockSpec((None, tr, C), lambda k, i, c_ref: (k, i, 0))],
        out_specs=pl.BlockSpec((None, tr, C), lambda k, i, c_ref: (k, i, 0)))
    return pl.pallas_call(body, name=name, grid_spec=grid_spec, out_shape=jax.ShapeDtypeStruct(recv.shape, f32),
                          compiler_params=_cp(("parallel", "parallel")))(c, g4, recv)


def sum_chips(p4, *, name, tr=256):
    _, Rh, C = p4.shape
    tr = _row_tile(Rh, tr)

    def body(a_ref, b_ref, c_ref, d_ref, o_ref):
        o_ref[...] = ((a_ref[...] + b_ref[...]) + c_ref[...]) + d_ref[...]

    spec = lambda k: pl.BlockSpec((None, tr, C), lambda i: (k, i, 0))
    return pl.pallas_call(
        body, name=name, grid=(Rh // tr,), in_specs=[spec(0), spec(1), spec(2), spec(3)],
        out_specs=pl.BlockSpec((tr, C), lambda i: (i, 0)), out_shape=jax.ShapeDtypeStruct((Rh, C), f32),
        compiler_params=_cp(("parallel",)),
    )(p4, p4, p4, p4)


def _place():
    return lax.axis_index("x"), lax.axis_index("y"), lax.axis_index("c")


def _other_chips(x, y):
    return [(1 - x, y), (x, 1 - y), (1 - x, 1 - y)]


def _chip_exchange(ins, outs, sems, scatter, start):
    if not ins:
        return
    send, recv, loc = sems
    x, y, c = _place()
    me = 2 * x + y
    for a in range(len(ins)):
        own = pltpu.make_async_copy(ins[a].at[me] if scatter else ins[a], outs[a].at[me], loc.at[a])
        own.start() if start else own.wait()
        for p, (px, py) in enumerate(_other_chips(x, y)):
            k = 2 * px + py
            cp = pltpu.make_async_remote_copy(
                src_ref=ins[a].at[k] if scatter else ins[a], dst_ref=outs[a].at[me if start else k],
                send_sem=send.at[3 * a + p], recv_sem=recv.at[3 * a + p], device_id=(px, py, c), device_id_type=MESH)
            cp.start() if start else cp.wait()


def _chip_exchange_args(arrs, scatter):
    n = len(arrs)
    shapes = [jax.ShapeDtypeStruct(a.shape if scatter else (N_CHIPS,) + a.shape, a.dtype) for a in arrs]
    sems = [pltpu.SemaphoreType.DMA((3 * n,)), pltpu.SemaphoreType.DMA((3 * n,)), pltpu.SemaphoreType.DMA((n,))]
    return shapes, sems if n else []


def _chip_exchange_call(arrs, scatter, name):
    n = len(arrs)

    def body(*refs):
        ins, outs, sems = refs[:n], refs[n:2 * n], refs[2 * n:]
        _chip_exchange(ins, outs, sems, scatter, True)
        _chip_exchange(ins, outs, sems, scatter, False)

    shapes, sems = _chip_exchange_args(arrs, scatter)
    return pl.pallas_call(
        body, name=name, in_specs=[ANY] * n, out_specs=[ANY] * n, out_shape=shapes, scratch_shapes=sems,
        compiler_params=pltpu.CompilerParams(has_side_effects=True),
    )(*arrs)


def allgather_chips(arrs, *, name):
    return _chip_exchange_call(arrs, False, name)


def sibling_swap(arrs, *, half, name):
    n = len(arrs)

    def body(*refs):
        ins, outs = refs[:n], refs[n:2 * n]
        send, recv = refs[2 * n:]
        x, y, c = _place()
        cps = []
        for a in range(n):
            src = ins[a].at[:, 1 - c] if half else ins[a]
            cp = pltpu.make_async_remote_copy(src_ref=src, dst_ref=outs[a], send_sem=send.at[a], recv_sem=recv.at[a],
                                              device_id=(x, y, 1 - c), device_id_type=MESH)
            cp.start()
            cps.append(cp)
        for cp in cps:
            cp.wait()

    def oshape(a):
        return jax.ShapeDtypeStruct((a.shape[0],) + a.shape[2:] if half else a.shape, a.dtype)

    return pl.pallas_call(
        body, name=name, in_specs=[ANY] * n, out_specs=[ANY] * n, out_shape=[oshape(a) for a in arrs],
        scratch_shapes=[pltpu.SemaphoreType.DMA((n,)), pltpu.SemaphoreType.DMA((n,))],
        compiler_params=pltpu.CompilerParams(has_side_effects=True),
    )(*arrs)


def chip_scatter(arrs, *, name):
    return _chip_exchange_call(arrs, True, name)


def _pad_cols(w):
    K = w.shape[0]
    w = w.reshape(K, -1, SB_HEAD_DIM)
    return jnp.pad(w, ((0, 0), (0, 0), (0, LANES - SB_HEAD_DIM))).reshape(K, -1)


def _unpad_cols(w):
    K = w.shape[0]
    return w.reshape(K, -1, LANES)[:, :, :SB_HEAD_DIM].reshape(K, -1)


def _pad_rows(w):
    N = w.shape[1]
    w = w.reshape(-1, SB_HEAD_DIM, N)
    return jnp.pad(w, ((0, 0), (0, LANES - SB_HEAD_DIM), (0, 0))).reshape(-1, N)


def _unpad_rows(w):
    N = w.shape[1]
    return w.reshape(-1, LANES, N)[:, :SB_HEAD_DIM, :].reshape(-1, N)


_PACK_ROWS = N_CHIPS * 2 * SUBLANES


def _pack(arrs):
    flat = jnp.concatenate([a.reshape(-1) for a in arrs])
    rows = -(-flat.shape[0] // LANES)
    rows = -(-rows // _PACK_ROWS) * _PACK_ROWS
    return jnp.pad(flat, (0, rows * LANES - flat.shape[0])).reshape(rows, LANES)


def _unpack(buf, shapes):
    flat = buf.reshape(-1)
    out, pos = [], 0
    for shp in shapes:
        size = 1
        for d in shp:
            size *= d
        out.append(flat[pos:pos + size].reshape(shp))
        pos += size
    return out


BIG = ("w_in", "ssm_w_glu", "w_out", "xa_w_q", "xa_w_kv", "xa_w_o", "w_up", "w_down")
SMALL = ("g_mix", "ssm_a_re", "ssm_a_im", "ssm_log_dt", "ssm_b_re", "ssm_b_im", "ssm_c_re", "ssm_c_im", "ssm_d",
         "sb_g_q", "sb_g_k", "g_out_ssm", "g_out_sb", "g_xa", "g_mem", "xa_g_q", "xa_g_k", "g_mlp")
WEIGHTS = ("g_mix", "w_in", "ssm_a_re", "ssm_a_im", "ssm_log_dt", "ssm_b_re", "ssm_b_im", "ssm_c_re", "ssm_c_im",
           "ssm_d", "ssm_w_glu", "sb_g_q", "sb_g_k", "g_out_ssm", "g_out_sb", "w_out", "g_xa", "g_mem", "xa_w_q",
           "xa_w_kv", "xa_g_q", "xa_g_k", "xa_w_o", "g_mlp", "w_up", "w_down")


def kernel(x, mem, g_mix, w_in, ssm_a_re, ssm_a_im, ssm_log_dt, ssm_b_re, ssm_b_im, ssm_c_re, ssm_c_im, ssm_d, ssm_w_glu, sb_g_q, sb_g_k, g_out_ssm, g_out_sb, w_out, g_xa, g_mem, xa_w_q, xa_w_kv, xa_g_q, xa_g_k, xa_w_o, g_mlp, w_up, w_down, loss_target, m_g_mix, m_w_in, m_ssm_a_re, m_ssm_a_im, m_ssm_log_dt, m_ssm_b_re, m_ssm_b_im, m_ssm_c_re, m_ssm_c_im, m_ssm_d, m_ssm_w_glu, m_sb_g_q, m_sb_g_k, m_g_out_ssm, m_g_out_sb, m_w_out, m_g_xa, m_g_mem, m_xa_w_q, m_xa_w_kv, m_xa_g_q, m_xa_g_k, m_xa_w_o, m_g_mlp, m_w_up, m_w_down, v_g_mix, v_w_in, v_ssm_a_re, v_ssm_a_im, v_ssm_log_dt, v_ssm_b_re, v_ssm_b_im, v_ssm_c_re, v_ssm_c_im, v_ssm_d, v_ssm_w_glu, v_sb_g_q, v_sb_g_k, v_g_out_ssm, v_g_out_sb, v_w_out, v_g_xa, v_g_mem, v_xa_w_q, v_xa_w_kv, v_xa_g_q, v_xa_g_k, v_xa_w_o, v_g_mlp, v_w_up, v_w_down):
    env = dict(locals())
    W = {n: env[n] for n in WEIGHTS}
    M1 = {n: env["m_" + n] for n in WEIGHTS}
    V2 = {n: env["v_" + n] for n in WEIGHTS}
    xs, mems, tgt = x[0], mem[0], loss_target[0]
    S, D = xs.shape
    G, P, C = SSM_GROUPS, SSM_STATE, SSM_GROUP
    GP = G * P
    SBW = SB_HEADS * SB_HEAD_DIM
    c_idx = lax.axis_index("c")

    big = dict(tn=1024, tk=1024)
    wide = dict(tm=1024, tn=1024, tk=2048)
    h0, (g_in,) = rms_norm(xs, g_mix, D, name="norm_x", exchange=([w_in[0].astype(bf16)], False))
    Wu = g_in[0]
    Wqkv = jnp.concatenate([_pad_cols(g_in[1]), _pad_cols(g_in[2]), _pad_cols(g_in[3])], axis=1)
    gq_pad, gk_pad = _pad_cols(sb_g_q), _pad_cols(sb_g_k)
    gosb_pad = _pad_cols(g_out_sb)
    a_re, a_im = ssm_a_re.reshape(1, GP), ssm_a_im.reshape(1, GP)
    bT_re = ssm_b_re[0].transpose(2, 0, 1).reshape(C, GP)
    bT_im = ssm_b_im[0].transpose(2, 0, 1).reshape(C, GP)
    cT_re = ssm_c_re[0].transpose(1, 0, 2).reshape(C, GP)
    cT_im = ssm_c_im[0].transpose(1, 0, 2).reshape(C, GP)
    s5_in = (a_re, a_im, ssm_log_dt, bT_re, bT_im, cT_re, cT_im)

    u = mm(h0, Wu, mode="nn", name="proj_u", tk=1024)
    shard = {n: W[n][0].astype(bf16) for n in BIG[1:]}
    qkv, (g_glu, g_out, g_xq, g_xkv, g_xo) = mm(
        h0, Wqkv, mode="nn", name="proj_qkv", tm=1024,
        exchange=([shard[n] for n in ("ssm_w_glu", "w_out", "xa_w_q", "xa_w_kv", "xa_w_o")], False), **big)
    qn, kn, vb = qkv_prep(qkv, gq_pad, gk_pad)
    o, (g_down,) = sb_fwd(qn, kn, vb, ([shard["w_down"]], False))
    Wglu = g_glu.reshape(-1, g_glu.shape[-1])
    Wout = g_out.reshape(-1, g_out.shape[-1])
    Wo_ssm, Wo_sb = Wout[:SBW], _pad_rows(Wout[SBW:])
    Wxq = g_xq.reshape(-1, g_xq.shape[-1])
    Wxkv = g_xkv.reshape(-1, g_xkv.shape[-1])
    Wxo = g_xo.transpose(1, 0, 2).reshape(g_xo.shape[1], -1)
    abr, abi, Bm, Cm = s5_prep(*s5_in)
    n_pos = S // SCAN_CHUNKS
    u_il = row_shuffle(u, n_pos, SCAN_CHUNKS, name="u_interleave", out_dtype=bf16)
    st, (g_up,) = scan_fwd(u_il, Bm, abr, abi, ([shard["w_up"]], False))
    ypre0_il = mm(st, Cm, mode="nt", name="s5_y", a_shards=2, b_shards=2, tm=1024, tk=2048)
    Wup = g_up.transpose(1, 0, 2).reshape(g_up.shape[1], -1)
    Wdown = g_down.reshape(-1, g_down.shape[-1])
    ypre0 = row_shuffle(ypre0_il, SCAN_CHUNKS, n_pos, name="y_token_order")
    ypre, z, zn = glu_fwd(ypre0, u, ssm_d, Wglu, g_out_ssm)
    on = rms_norm(o, gosb_pad, SBW, name="norm_o")
    x1a = mm(zn, Wo_ssm, mode="nn", name="out_ssm", epi="add", aux=xs, tn=1024)
    x1 = mm(on, Wo_sb, mode="nn", name="out_sb", epi="add", aux=x1a, **big)
    h1 = rms_norm(x1, g_xa, D, name="norm_x1")
    qx = mm(h1, Wxq, mode="nn", name="xa_q", tk=1024)
    memn = rms_norm(mems, g_mem, D, name="norm_mem")
    kv = mm(memn, Wxkv, mode="nn", name="xa_kv", **big)
    ox = xa_fwd(qx, kv, xa_g_q, xa_g_k)
    x2 = mm(ox, Wxo, mode="nn", name="xa_o", epi="add", aux=x1, tn=1024)
    h2 = rms_norm(x2, g_mlp, D, name="norm_x2")
    act = mm(h2, Wup, mode="nn", name="mlp_up", out_dtype=bf16, tm=1024, tn=2048, tk=1024)
    x3 = mm(act, Wdown, mode="nn", name="mlp_down", pro="relu2", epi="add", aux=x2, **wide)
    dx3, loss_part, dx3b = loss_head(x3, tgt)
    loss = lax.psum(loss_part[0, 0], ("x", "y", "c"))

    dact = mm(dx3b, Wdown, mode="nt", name="d_act", epi="mul2relu", aux=act, out_dtype=bf16, tm=1024, tn=2048,
              tk=1024)
    dWdown = mm(act, dx3b, mode="tn", name="dw_down", pro="relu2", **wide)
    dWup = mm(h2, dact, mode="tn", name="dw_up", out_shards=N_CHIPS, **wide)
    dh2 = mm(dact, Wup, mode="nt", name="d_h2", **wide)
    dx2, dg_mlp, dx2b = rms_bwd(x2, g_mlp, dh2, dx3, D, name="rms_bwd_mlp", twin=True)
    dox = mm(dx2b, Wxo, mode="nt", name="d_ox", out_dtype=bf16, tk=1024)
    dWxo = mm(ox, dx2b, mode="tn", name="dw_xo", out_shards=N_CHIPS, tk=1024)
    dqx, dkv, dg_xq, dg_xk = xa_bwd(qx, kv, xa_g_q, xa_g_k, dox)
    dWxq = mm(h1, dqx, mode="tn", name="dw_xq", tm=1024, tk=1024)
    dh1 = mm(dqx, Wxq, mode="nt", name="d_h1", tn=1024)
    dx1, dg_xa, dx1b = rms_bwd(x1, g_xa, dh1, dx2, D, name="rms_bwd_xa", twin=True)
    dWxkv = mm(memn, dkv, mode="tn", name="dw_xkv", tm=1024, tn=1024)
    dmemn = mm(dkv, Wxkv, mode="nt", name="d_memn", **big)
    _, dg_mem = rms_bwd(mems, g_mem, dmemn, None, D, name="rms_bwd_mem")
    dyn_ssm = mm(dx1b, Wo_ssm, mode="nt", name="d_yn_ssm", tk=1024)
    dyn_sb = mm(dx1b, Wo_sb, mode="nt", name="d_yn_sb", **big)
    dWo_ssm = mm(zn, dx1b, mode="tn", name="dw_out_ssm", **big)
    dWo_sb = mm(on, dx1b, mode="tn", name="dw_out_sb", tm=1024, **big)
    dz, dg_os = rms_bwd(z, g_out_ssm, dyn_ssm, None, SBW, name="rms_bwd_ssm")
    do, dg_osb = rms_bwd(o, gosb_pad, dyn_sb, None, SBW, name="rms_bwd_sb")
    c_arr = c_idx.astype(jnp.int32).reshape(1)

    def sibling_sums(grads, names, tag):
        g4 = [g.reshape(N_CHIPS, 2, g.shape[1] // 2, g.shape[2]) for g in grads]
        from_sib = sibling_swap(g4, half=True, name="grad_to_sibling_" + tag)
        return [add_half(a, b, c_arr, name="add_sibling_" + n) for a, b, n in zip(g4, from_sib, names)]

    early = ("xa_w_q", "xa_w_kv", "xa_w_o", "w_up", "w_down")
    early_g = [dWxq.reshape(N_CHIPS, -1, dWxq.shape[1]), dWxkv.reshape(N_CHIPS, -1, dWxkv.shape[1]), dWxo, dWup,
               dWdown.reshape(N_CHIPS, -1, D)]
    pair = sibling_sums(early_g, early, "early")
    dqn, dkn, dv, parts_mlp = sb_bwd(qn, kn, vb, do, (pair[3:], True))
    dqkv, dg_q, dg_k, parts_xa = qkv_bwd(qkv, gq_pad, gk_pad, dqn, dkn, dv, (pair[:3], True))
    dypre, du_skip, dWglu, dd = glu_bwd(ypre, u, ssm_d, Wglu, dz)
    mid = ("ssm_w_glu", "w_out")
    mid_g = [dWglu.reshape(N_CHIPS, -1, dWglu.shape[1]),
             jnp.concatenate([dWo_ssm, _unpad_rows(dWo_sb)]).reshape(N_CHIPS, -1, D)]
    dypre_il = row_shuffle(dypre, n_pos, SCAN_CHUNKS, name="dy_interleave", out_dtype=bf16)
    dCm = mm(dypre_il, st, mode="tn", name="d_cmat", b_shards=2, out_shards=2, **wide)
    gst, dabr, dabi, parts_mid = scan_bwd(dypre_il, Cm, st, abr, abi, (sibling_sums(mid_g, mid, "mid"), True))
    dBm = mm(u_il, gst, mode="tn", name="d_bmat", b_shards=2, out_shards=2, **wide)
    du_il = mm(gst, Bm, mode="nt", name="d_u", a_shards=2, b_shards=2, tm=1024, tk=2048)
    mine = {n: sum_chips(p, name="sum_chips_" + n)
            for n, p in zip(early + mid, [*parts_xa, *parts_mlp, *parts_mid])}
    du = row_shuffle(du_il, SCAN_CHUNKS, n_pos, name="du_token_order", add=du_skip, out_dtype=bf16)
    s5_g = s5_prep_bwd(*s5_in, dabr, dabi, dBm, dCm)
    dWu = mm(h0, du, mode="tn", name="dw_u", tm=1024, tk=1024)
    dWqkv = mm(h0, dqkv, mode="tn", name="dw_qkv", **wide)
    HW = SB_HEADS * LANES
    w_in_g = jnp.stack([dWu, _unpad_cols(dWqkv[:, :HW]), _unpad_cols(dWqkv[:, HW:2 * HW]),
                        _unpad_cols(dWqkv[:, 2 * HW:])])
    dh0a = mm(du, Wu, mode="nt", name="d_h0_u", tn=1024)
    dh0, parts_in = mm(dqkv, Wqkv, mode="nt", name="d_h0_qkv", epi="add", aux=dh0a, tm=1024,
                       exchange=(sibling_sums([w_in_g], ("w_in",), "w_in"), True), **big)
    mine["w_in"] = sum_chips(parts_in[0], name="sum_chips_w_in")
    dx, dg_mix = rms_bwd(xs, g_mix, dh0, dx1, D, name="rms_bwd_mix")

    late = ("small",)
    late_g = []
    da_re, da_im, dldt, dbT_re, dbT_im, dcT_re, dcT_im = s5_g
    small_g = {
        "g_mix": dg_mix, "ssm_a_re": da_re, "ssm_a_im": da_im, "ssm_log_dt": dldt,
        "ssm_b_re": dbT_re.reshape(C, G, P).transpose(1, 2, 0), "ssm_b_im": dbT_im.reshape(C, G, P).transpose(1, 2, 0),
        "ssm_c_re": dcT_re.reshape(C, G, P).transpose(1, 0, 2), "ssm_c_im": dcT_im.reshape(C, G, P).transpose(1, 0, 2),
        "ssm_d": dd, "sb_g_q": dg_q[:, :SB_HEAD_DIM], "sb_g_k": dg_k[:, :SB_HEAD_DIM], "g_out_ssm": dg_os,
        "g_out_sb": _unpad_cols(dg_osb), "g_xa": dg_xa, "g_mem": dg_mem, "xa_g_q": dg_xq, "xa_g_k": dg_xk,
        "g_mlp": dg_mlp,
    }
    late_g.append(_pack([small_g[n] for n in SMALL]).reshape(N_CHIPS, -1, LANES))

    parts_late = chip_scatter(sibling_sums(late_g, late, "late"), name="grad_to_chips_late")
    mine.update({n: sum_chips(p, name="sum_chips_" + n) for n, p in zip(late, parts_late)})
    mine = [mine[n] for n in list(BIG) + ["small"]]
    other = sibling_swap(mine, half=False, name="grad_half_to_sibling")
    shard = [jnp.where(c_idx == 0, jnp.concatenate([a, b]), jnp.concatenate([b, a])) for a, b in zip(mine, other)]
    small_all = allgather_chips([shard[-1]], name="gather_small")[0]
    small_red = small_all.reshape(-1, LANES)

    out = {}
    for n, gs in zip(BIG, shard[:-1]):
        shp = W[n].shape
        w2, m2, v2 = (t.reshape(gs.shape) for t in (W[n], M1[n], V2[n]))
        d, nm, nv = adamw(w2, gs, m2, v2, name="adamw_" + n)
        out[n] = tuple(t.reshape(shp) for t in (gs, d, nm, nv))
    shapes = [W[n].shape for n in SMALL]
    d, nm, nv = adamw(_pack([W[n] for n in SMALL]), small_red, _pack([M1[n] for n in SMALL]),
                      _pack([V2[n] for n in SMALL]), name="adamw_small")
    for n, gs, dd_, mm_, vv_ in zip(SMALL, _unpack(small_red, shapes), _unpack(d, shapes), _unpack(nm, shapes),
                                    _unpack(nv, shapes)):
        out[n] = (gs, dd_, mm_, vv_)
    res = [loss, dx[None]]
    for kind in range(4):
        res += [out[n][kind] for n in WEIGHTS]
    return tuple(res)
```

```python
import jax
import jax.numpy as jnp
from jax import lax
from jax.experimental import pallas as pl
from jax.experimental.pallas import tpu as pltpu

f32 = jnp.float32
bf16 = jnp.bfloat16

NORM_EPS = 1e-6
SSM_GROUPS = 32
SSM_GROUP = 16
SSM_STATE = 64
SB_HEADS = 8
SB_HEAD_DIM = 64
XA_HEADS = 4
XA_HEAD_DIM = 128
LANES = 128
SUBLANES = 8
N_CHIPS = 4
ADAM_LR = 0.001
ADAM_B1 = 0.9
ADAM_B2 = 0.999
ADAM_EPS = 1e-08
ADAM_WD = 0.01
ADAM_STEP = 10
VMEM_LIMIT = 56 * 1024 * 1024
MESH = pl.DeviceIdType.MESH
ANY = pl.BlockSpec(memory_space=pl.ANY)


def _cp(sem=None):
    return pltpu.CompilerParams(dimension_semantics=sem, vmem_limit_bytes=VMEM_LIMIT)


def _tile(n, pref):
    if n <= pref:
        return n
    t = (pref // LANES) * LANES
    while t > LANES and n % t:
        t -= LANES
    assert n % t == 0, (n, pref)
    return t


def _row_tile(n, pref):
    if n <= pref:
        return n
    t = (pref // SUBLANES) * SUBLANES
    while n % t:
        t -= SUBLANES
    return t


def _dot(a, b, dims):
    return lax.dot_general(a.astype(bf16), b.astype(bf16), (dims, ((), ())), preferred_element_type=f32)


_NN = ((1,), (0,))
_NT = ((1,), (1,))
_TN = ((0,), (0,))


@jax.custom_vjp
def bdot_nn(a, b):
    return _dot(a, b, _NN)


def _bdot_nn_fwd(a, b):
    return _dot(a, b, _NN), (a, b)


def _bdot_nn_bwd(res, g):
    a, b = res
    return _dot(g, b, _NT), _dot(a, g, _TN)


bdot_nn.defvjp(_bdot_nn_fwd, _bdot_nn_bwd)


@jax.custom_vjp
def bdot_nt(a, b):
    return _dot(a, b, _NT)


def _bdot_nt_fwd(a, b):
    return _dot(a, b, _NT), (a, b)


def _bdot_nt_bwd(res, g):
    a, b = res
    return _dot(g, b, _NN), _dot(g, a, _TN)


bdot_nt.defvjp(_bdot_nt_fwd, _bdot_nt_bwd)


def _rms(x, g, denom):
    r = lax.rsqrt(jnp.sum(x * x, axis=-1, keepdims=True) * (1.0 / denom) + NORM_EPS)
    return x * r * g


def _opspec(block, row_of, col_of, shards, ncol_tiles):
    if shards == 1:
        return pl.BlockSpec(block, lambda i, j, k: (row_of(i, j, k), col_of(i, j, k)))
    per = ncol_tiles // shards
    return pl.BlockSpec((None,) + block,
                        lambda i, j, k: (col_of(i, j, k) // per, row_of(i, j, k), col_of(i, j, k) % per))


def _call_with_exchange(body, args, exchange, *, name, grid, in_specs, out_specs, out_shape, scratch_shapes=()):
    xs, scatter = exchange
    n, n_in, n_out, n_scr = len(xs), len(in_specs), len(out_specs), len(scratch_shapes)
    x_shapes, sems = _chip_exchange_args(xs, scatter)

    def wrapped(*refs):
        ins, x_ins = refs[:n_in], refs[n_in:n_in + n]
        outs, x_outs = refs[n_in + n:n_in + n + n_out], refs[n_in + n + n_out:n_in + 2 * n + n_out]
        scratch, x_sems = refs[n_in + 2 * n + n_out:n_in + 2 * n + n_out + n_scr], refs[n_in + 2 * n + n_out + n_scr:]
        first, last = True, True
        for d, steps in enumerate(grid):
            first = first & (pl.program_id(d) == 0)
            last = last & (pl.program_id(d) == steps - 1)

        @pl.when(first)
        def _():
            _chip_exchange(x_ins, x_outs, x_sems, scatter, True)

        body(*ins, *outs, *scratch)

        @pl.when(last)
        def _():
            _chip_exchange(x_ins, x_outs, x_sems, scatter, False)

    res = pl.pallas_call(
        wrapped, name=name, grid=grid, in_specs=list(in_specs) + [ANY] * n, out_specs=list(out_specs) + [ANY] * n,
        out_shape=list(out_shape) + x_shapes, scratch_shapes=list(scratch_shapes) + sems,
        compiler_params=_cp(("arbitrary",) * len(grid)),
    )(*args, *xs)
    return res[:n_out], res[n_out:]


def mm(a, b, *, mode, name, tm=512, tn=512, tk=512, pro="none", epi="none", aux=None,
       out_dtype=f32, a_shards=1, b_shards=1, out_shards=1, exchange=None, norm_bwd=None):
    ar, ac = a.shape[-2], a.shape[-1] * a_shards
    br, bc = b.shape[-2], b.shape[-1] * b_shards
    if mode == "nn":
        M, K, N = ar, ac, bc
        assert br == K
    elif mode == "nt":
        M, K, N = ar, ac, br
        assert bc == K
    else:
        M, K, N = ac, ar, bc
        assert br == K
    tm, tn, tk = _tile(M, tm), _tile(N, tn), _tile(K, tk)
    if a_shards > 1:
        if mode == "tn":
            tm = _tile(ac // a_shards, tm)
        else:
            tk = _tile(ac // a_shards, tk)
    if b_shards > 1:
        if mode == "nt":
            tk = _tile(bc // b_shards, tk)
        else:
            tn = _tile(bc // b_shards, tn)
    if out_shards > 1:
        tn = _tile(N // out_shards, tn)
    nm, nn_, nk = M // tm, N // tn, K // tk
    I = lambda i, j, k: i
    J = lambda i, j, k: j
    Kk = lambda i, j, k: k
    if mode == "nn":
        a_spec = _opspec((tm, tk), I, Kk, a_shards, nk)
        b_spec = _opspec((tk, tn), Kk, J, b_shards, nn_)
        dims = _NN
    elif mode == "nt":
        a_spec = _opspec((tm, tk), I, Kk, a_shards, nk)
        b_spec = _opspec((tn, tk), J, Kk, b_shards, nk)
        dims = _NT
    else:
        a_spec = _opspec((tk, tm), Kk, I, a_shards, nm)
        b_spec = _opspec((tk, tn), Kk, J, b_shards, nn_)
        dims = _TN
    in_specs = [a_spec, b_spec]
    args = [a, b]
    tile = pl.BlockSpec((tm, tn), lambda i, j, k: (i, j))
    if epi != "none":
        in_specs.append(tile)
        args.append(aux)
    if out_shards == 1:
        out_spec = tile
        out_shape = jax.ShapeDtypeStruct((M, N), out_dtype)
    else:
        per = nn_ // out_shards
        out_spec = pl.BlockSpec((None, tm, tn), lambda i, j, k: (j // per, i, j % per))
        out_shape = jax.ShapeDtypeStruct((out_shards, M, N // out_shards), out_dtype)
    out_specs, out_shapes = [out_spec], [out_shape]
    if norm_bwd is not None:
        nx, ng, nres, denom = norm_bwd
        assert tn == N and out_shards == 1 and out_dtype == f32
        vec = pl.BlockSpec((1, tn), lambda i, j, k: (0, 0))
        in_specs += [tile, vec, tile]
        args += [nx, ng, nres]
        out_specs += [vec, tile]
        out_shapes += [jax.ShapeDtypeStruct((1, N), f32), jax.ShapeDtypeStruct((M, N), bf16)]
    n_in = len(in_specs)

    def body(*refs):
        a_ref, b_ref = refs[0], refs[1]
        aux_ref = refs[2] if epi != "none" else None
        o_ref, acc_ref = refs[n_in], refs[-1]
        i, k = pl.program_id(0), pl.program_id(2)

        @pl.when(k == 0)
        def _():
            acc_ref[...] = jnp.zeros_like(acc_ref)

        av = a_ref[...]
        if pro == "relu2":
            av = jnp.square(jnp.maximum(av.astype(f32), 0.0))
        acc_ref[...] += _dot(av, b_ref[...], dims)

        @pl.when(k == nk - 1)
        def _():
            res = acc_ref[...]
            if epi == "add":
                res = res + aux_ref[...].astype(f32)
            elif epi == "mul2relu":
                res = res * (2.0 * jnp.maximum(aux_ref[...].astype(f32), 0.0))
            if norm_bwd is None:
                o_ref[...] = res.astype(out_dtype)
            else:
                x_ref, g_ref, res_ref = refs[n_in - 3:n_in]
                dg_ref, twin_ref = refs[n_in + 1], refs[n_in + 2]
                _, vjp = jax.vjp(lambda xv, gv: _rms(xv, gv, denom), x_ref[...], g_ref[...])
                dx, dg = vjp(res)
                dx = dx + res_ref[...]
                o_ref[...] = dx
                twin_ref[...] = dx.astype(bf16)

                @pl.when(i == 0)
                def _():
                    dg_ref[...] = jnp.zeros_like(dg_ref)

                dg_ref[...] += dg

    acc = [pltpu.VMEM((tm, tn), f32)]
    single = norm_bwd is None
    if exchange is not None:
        outs, moved = _call_with_exchange(body, args, exchange, name=name, grid=(nm, nn_, nk), in_specs=in_specs,
                                          out_specs=out_specs, out_shape=out_shapes, scratch_shapes=acc)
        return (outs[0] if single else tuple(outs)), moved
    outs = pl.pallas_call(
        body, name=name, grid=(nm, nn_, nk), in_specs=in_specs, out_specs=out_specs, out_shape=out_shapes,
        scratch_shapes=acc, compiler_params=_cp(("parallel" if single else "arbitrary", "parallel", "arbitrary")),
    )(*args)
    return outs[0] if single else tuple(outs)


def rms_norm(x, g, denom, *, name, ts=512, exchange=None):
    S, D = x.shape
    ts = _tile(S, ts)

    def body(x_ref, g_ref, h_ref):
        h_ref[...] = _rms(x_ref[...], g_ref[...], denom).astype(bf16)

    row = pl.BlockSpec((ts, D), lambda i: (i, 0))
    kw = dict(name=name, grid=(S // ts,), in_specs=[row, pl.BlockSpec((1, D), lambda i: (0, 0))])
    if exchange is not None:
        (h,), moved = _call_with_exchange(body, (x, g), exchange, out_specs=[row],
                                          out_shape=[jax.ShapeDtypeStruct((S, D), bf16)], **kw)
        return h, moved
    return pl.pallas_call(body, out_specs=row, out_shape=jax.ShapeDtypeStruct((S, D), bf16),
                          compiler_params=_cp(("parallel",)), **kw)(x, g)


def rms_bwd(x, g, dy, res, denom, *, name, ts=256, twin=False):
    S, D = x.shape
    ts = _tile(S, ts)
    has_res = res is not None

    def body(*refs):
        x_ref, g_ref, dy_ref = refs[:3]
        outs = refs[4:] if has_res else refs[3:]
        _, vjp = jax.vjp(lambda xv, gv: _rms(xv, gv, denom), x_ref[...], g_ref[...])
        dx, dg = vjp(dy_ref[...])
        if has_res:
            dx = dx + refs[3][...]
        outs[0][...] = dx
        if twin:
            outs[2][...] = dx.astype(bf16)
        dg_ref = outs[1]

        @pl.when(pl.program_id(0) == 0)
        def _():
            dg_ref[...] = jnp.zeros_like(dg_ref)

        dg_ref[...] += dg

    row = pl.BlockSpec((ts, D), lambda i: (i, 0))
    vec = pl.BlockSpec((1, D), lambda i: (0, 0))
    in_specs = [row, vec, row] + ([row] if has_res else [])
    args = [x, g, dy] + ([res] if has_res else [])
    return pl.pallas_call(
        body, name=name, grid=(S // ts,), in_specs=in_specs, out_specs=[row, vec] + ([row] if twin else []),
        out_shape=[jax.ShapeDtypeStruct((S, D), f32), jax.ShapeDtypeStruct((1, D), f32)]
        + ([jax.ShapeDtypeStruct((S, D), bf16)] if twin else []),
        compiler_params=_cp(("arbitrary",)),
    )(*args)


LOG2E = 1.4426950408889634
LN2 = 0.6931471805599453


def _qk_fn(q, k, gq, gk):
    qs, ks = [], []
    for h in range(SB_HEADS):
        sl = slice(h * LANES, (h + 1) * LANES)
        qs.append(_rms(q[:, sl], gq, SB_HEAD_DIM) * (SB_HEAD_DIM ** -0.5 * LOG2E))
        ks.append(_rms(k[:, sl], gk, SB_HEAD_DIM))
    return jnp.concatenate(qs, axis=1), jnp.concatenate(ks, axis=1)


def qkv_prep(qkv, gq, gk, *, ts=256):
    S = qkv.shape[0]
    W = SB_HEADS * LANES
    ts = _tile(S, ts)

    def body(q_ref, k_ref, v_ref, gq_ref, gk_ref, qn_ref, kn_ref, vb_ref):
        qn, kn = _qk_fn(q_ref[...], k_ref[...], gq_ref[...], gk_ref[...])
        qn_ref[...] = qn.astype(bf16)
        kn_ref[...] = kn.astype(bf16)
        vb_ref[...] = v_ref[...].astype(bf16)

    out = jax.ShapeDtypeStruct((S, W), bf16)
    gspec = pl.BlockSpec((1, LANES), lambda i: (0, 0))
    ospec = pl.BlockSpec((ts, W), lambda i: (i, 0))
    col = lambda c: pl.BlockSpec((ts, W), lambda i: (i, c))
    return pl.pallas_call(
        body, name="qkv_prep", grid=(S // ts,), in_specs=[col(0), col(1), col(2), gspec, gspec],
        out_specs=[ospec, ospec, ospec], out_shape=[out, out, out], compiler_params=_cp(("parallel",)),
    )(qkv, qkv, qkv, gq, gk)


def qkv_bwd(qkv, gq, gk, dqn, dkn, dv, exchange, *, ts=256):
    S = qkv.shape[0]
    W = SB_HEADS * LANES
    ts = _tile(S, ts)

    def body(q_ref, k_ref, gq_ref, gk_ref, dqn_ref, dkn_ref, dv_ref, o_ref, dgq_ref, dgk_ref):
        _, vjp = jax.vjp(_qk_fn, q_ref[...], k_ref[...], gq_ref[...], gk_ref[...])
        dq, dk, dgq, dgk = vjp((dqn_ref[...] * LN2, dkn_ref[...] * LN2))
        o_ref[:, 0:W] = dq.astype(bf16)
        o_ref[:, W:2 * W] = dk.astype(bf16)
        o_ref[:, 2 * W:3 * W] = dv_ref[...].astype(bf16)

        @pl.when(pl.program_id(0) == 0)
        def _():
            dgq_ref[...] = jnp.zeros_like(dgq_ref)
            dgk_ref[...] = jnp.zeros_like(dgk_ref)

        dgq_ref[...] += dgq
        dgk_ref[...] += dgk

    gspec = pl.BlockSpec((1, LANES), lambda i: (0, 0))
    row = pl.BlockSpec((ts, W), lambda i: (i, 0))
    col = lambda c: pl.BlockSpec((ts, W), lambda i: (i, c))
    (dqkv, dgq, dgk), moved = _call_with_exchange(
        body, (qkv, qkv, gq, gk, dqn, dkn, dv), exchange, name="qkv_bwd", grid=(S // ts,),
        in_specs=[col(0), col(1), gspec, gspec, row, row, row],
        out_specs=[pl.BlockSpec((ts, 3 * W), lambda i: (i, 0)), gspec, gspec],
        out_shape=[jax.ShapeDtypeStruct((S, 3 * W), bf16), jax.ShapeDtypeStruct((1, LANES), f32),
                   jax.ShapeDtypeStruct((1, LANES), f32)])
    return dqkv, dgq, dgk, moved


def _sb_weights(q, ks, R, masked, row, col, UU):
    ls = [_dot(q, k, _NT) for k in ks]
    lbs, lm0s, cats = [], [], []
    for l, diag in zip(ls, masked):
        neg_abs = pltpu.bitcast(pltpu.bitcast(l, jnp.uint32) | jnp.uint32(0x80000000), f32)
        lp = jnp.log2(1.0 + jnp.exp2(neg_abs))
        lb = jnp.minimum(l, 0.0) - lp
        lm = lb - l
        if diag:
            lm = jnp.where(col < row, lm, 0.0)
        hi = lm.astype(bf16)
        lo = (lm - hi.astype(f32)).astype(bf16)
        lbs.append(lb)
        lm0s.append(lm[:, 0:1])
        cats.append(jnp.concatenate([hi, lo], axis=1))
    sums = [_dot(c, UU, _NN) for c in cats]
    ws = []
    for lb, lm0, A, diag in zip(lbs, lm0s, sums, masked):
        w = jnp.exp2(lb + (A + R))
        if diag:
            w = jnp.where(col < row, w, 0.0)
        R = R + (A[:, 0:1] + lm0)
        ws.append(w)
    return lbs, ws, R


def _tri2(tk):
    r = lax.broadcasted_iota(jnp.int32, (2 * tk, tk), 0)
    r = jnp.where(r >= tk, r - tk, r)
    c = lax.broadcasted_iota(jnp.int32, (2 * tk, tk), 1)
    return (r > c).astype(bf16)


SB_GROUP = 8


SB_ALL_ZERO_BELOW = -160.0


def _sweep(i, blocks_of, carry, descending, right_sum=None, ran=None):
    G = SB_GROUP
    n = jnp.maximum(i - 1, 0)
    rem, full = n % G, n // G
    asc = lambda js: js if descending else js[::-1]

    def first_group(c):
        one = lambda c: blocks_of([i], c, [True])
        two = lambda c: blocks_of(asc([i, i - 1]), c, asc([True, False]))
        return lax.cond(i >= 1, two, one, c)

    def body(p, c):
        return blocks_of(asc([i - 2 - p * G - u for u in range(G)]), c, [False] * G)

    def left_over(r):
        return lambda c: blocks_of(asc([r - 1 - u for u in range(r)]), c, [False] * r) if r else c

    if descending:
        carry = first_group(carry)
        alive = lambda c: jnp.max(right_sum(c)) > SB_ALL_ZERO_BELOW
        bodies, carry = lax.while_loop(lambda s: (s[0] < full) & alive(s[1]),
                                       lambda s: (s[0] + 1, body(s[0], s[1])), (jnp.int32(0), carry))
        tail = (bodies == full) & alive(carry)
        carry = lax.switch(jnp.where(tail, rem, 0), [left_over(r) for r in range(G)], carry)
        return carry, (bodies, tail)
    bodies, tail = ran
    carry = lax.switch(jnp.where(tail, rem, 0), [left_over(r) for r in range(G)], carry)
    carry = lax.fori_loop(0, bodies, lambda t, c: body(bodies - 1 - t, c), carry)
    return first_group(carry)


def sb_fwd(qn, kn, vb, exchange, *, tq=256):
    S, W = qn.shape
    H = W // LANES
    tq = _tile(S, tq)
    tk = tq
    nq = S // tq

    def body(q_ref, k_ref, v_ref, o_ref, uu_s):
        i = pl.program_id(1)

        @pl.when((pl.program_id(0) == 0) & (i == 0))
        def _():
            uu_s[...] = _tri2(tk)

        q = q_ref[...]
        row = lax.broadcasted_iota(jnp.int32, (tq, tk), 0)
        col = lax.broadcasted_iota(jnp.int32, (tq, tk), 1)
        UU = uu_s[...]

        def blocks(js, c, masked):
            rows = [pl.ds(pl.multiple_of(j * tk, tk), tk) for j in js]
            _, ws, R = _sb_weights(q, [k_ref[r, :] for r in rows], c[0], masked, row, col, UU)
            acc = c[1]
            for w, r in zip(ws, rows):
                acc = acc + _dot(w, v_ref[r, :], _NN)
            return R, acc

        c, _ = _sweep(i, blocks, (jnp.zeros((tq, 1), f32), jnp.zeros((tq, LANES), f32)), True, lambda c: c[0])
        o_ref[...] = c[1]

    qspec = pl.BlockSpec((tq, LANES), lambda h, i: (i, h))
    kspec = pl.BlockSpec((S, LANES), lambda h, i: (0, h))
    (o,), moved = _call_with_exchange(
        body, (qn, kn, vb), exchange, name="sb_fwd", grid=(H, nq), in_specs=[qspec, kspec, kspec], out_specs=[qspec],
        out_shape=[jax.ShapeDtypeStruct((S, W), f32)], scratch_shapes=[pltpu.VMEM((2 * tk, tk), bf16)])
    return o, moved


def sb_bwd(qn, kn, vb, do, exchange, *, tq=256):
    S, W = qn.shape
    H = W // LANES
    tq = _tile(S, tq)
    tk = tq
    nq = S // tq

    def body(q_ref, k_ref, v_ref, do_ref, dq_ref, dk_ref, dv_ref, dz_s, beta_s, uu_s, ue_s):
        i = pl.program_id(1)
        row = lax.broadcasted_iota(jnp.int32, (tq, tk), 0)
        col = lax.broadcasted_iota(jnp.int32, (tq, tk), 1)

        @pl.when((pl.program_id(0) == 0) & (i == 0))
        def _():
            uu_s[...] = _tri2(tk)
            ue_s[...] = (row < col).astype(bf16)

        @pl.when(i == 0)
        def _():
            dk_ref[...] = jnp.zeros_like(dk_ref)
            dv_ref[...] = jnp.zeros_like(dv_ref)

        q = q_ref[...]
        dob = do_ref[...].astype(bf16)
        UU = uu_s[...]
        Ue = ue_s[...]

        def sweep1(js, R, masked):
            rows = [pl.ds(pl.multiple_of(j * tk, tk), tk) for j in js]
            dws = [_dot(dob, v_ref[r, :], _NT) for r in rows]
            lbs, ws, R = _sb_weights(q, [k_ref[r, :] for r in rows], R, masked, row, col, UU)
            for j, lb, w, dw in zip(js, lbs, ws, dws):
                dz_s[j] = (dw * w).astype(bf16)
                beta_s[j] = jnp.exp2(lb).astype(bf16)
            for r, w in zip(rows, ws):
                dv_ref[r, :] += _dot(w, dob, _TN)
            return R

        _, ran = _sweep(i, sweep1, jnp.zeros((tq, 1), f32), True, lambda R: R)

        def sweep2(js, c, masked):
            rows = [pl.ds(pl.multiple_of(j * tk, tk), tk) for j in js]
            dzbs = [dz_s[j] for j in js]
            sums = [_dot(dzb, Ue, _NN) for dzb in dzbs]
            Lz, dq = c
            dlbs = []
            for j, dzb, Cz, diag in zip(js, dzbs, sums, masked):
                dz = dzb.astype(f32)
                dl = dz - beta_s[j].astype(f32) * (dz + (Cz + Lz))
                if diag:
                    dl = jnp.where(col < row, dl, 0.0)
                Lz = Lz + (Cz[:, tk - 1:tk] + dz[:, tk - 1:tk])
                dlbs.append(dl.astype(bf16))
            for r, dlb in zip(rows, dlbs):
                dq = dq + _dot(dlb, k_ref[r, :], _NN)
            for r, dlb in zip(rows, dlbs):
                dk_ref[r, :] += _dot(dlb, q, _TN)
            return Lz, dq

        c = _sweep(i, sweep2, (jnp.zeros((tq, 1), f32), jnp.zeros((tq, LANES), f32)), False, ran=ran)
        dq_ref[...] = c[1]

    qspec = pl.BlockSpec((tq, LANES), lambda h, i: (i, h))
    kspec = pl.BlockSpec((S, LANES), lambda h, i: (0, h))
    full = jax.ShapeDtypeStruct((S, W), f32)
    (dq, dk, dv), moved = _call_with_exchange(
        body, (qn, kn, vb, do), exchange, name="sb_bwd", grid=(H, nq), in_specs=[qspec, kspec, kspec, qspec],
        out_specs=[qspec, kspec, kspec], out_shape=[full, full, full],
        scratch_shapes=[pltpu.VMEM((nq, tq, tk), bf16), pltpu.VMEM((nq, tq, tk), bf16),
                        pltpu.VMEM((2 * tk, tk), bf16), pltpu.VMEM((tk, tk), bf16)])
    return dq, dk, dv, moved


def _xa_fn(qx, kv, gq, gk):
    XW = XA_HEADS * XA_HEAD_DIM
    outs = []
    for h in range(XA_HEADS):
        sl = slice(h * XA_HEAD_DIM, (h + 1) * XA_HEAD_DIM)
        qn = _rms(qx[:, sl], gq, XA_HEAD_DIM)
        kn = _rms(kv[:, sl], gk, XA_HEAD_DIM)
        v = kv[:, XW + h * XA_HEAD_DIM:XW + (h + 1) * XA_HEAD_DIM]
        s = bdot_nt(qn, kn) * (XA_HEAD_DIM ** -0.5)
        e = jnp.exp(s - lax.stop_gradient(jnp.max(s, axis=-1, keepdims=True)))
        p = e / jnp.sum(e, axis=-1, keepdims=True)
        outs.append(bdot_nn(p, v))
    return jnp.concatenate(outs, axis=1)


def xa_fwd(qx, kv, gq, gk, *, ts=256):
    S, XW = qx.shape
    M = kv.shape[0]
    ts = _tile(S, ts)

    def body(q_ref, kv_ref, gq_ref, gk_ref, o_ref):
        o_ref[...] = _xa_fn(q_ref[...], kv_ref[...], gq_ref[...], gk_ref[...]).astype(bf16)

    row = pl.BlockSpec((ts, XW), lambda i: (i, 0))
    gspec = pl.BlockSpec((1, XA_HEAD_DIM), lambda i: (0, 0))
    return pl.pallas_call(
        body, name="xa_fwd", grid=(S // ts,),
        in_specs=[row, pl.BlockSpec((M, 2 * XW), lambda i: (0, 0)), gspec, gspec], out_specs=row,
        out_shape=jax.ShapeDtypeStruct((S, XW), bf16), compiler_params=_cp(("parallel",)),
    )(qx, kv, gq, gk)


def xa_bwd(qx, kv, gq, gk, do, *, ts=256):
    S, XW = qx.shape
    M = kv.shape[0]
    ts = _tile(S, ts)

    def body(q_ref, kv_ref, gq_ref, gk_ref, do_ref, dq_ref, dkv_ref, dgq_ref, dgk_ref):
        _, vjp = jax.vjp(_xa_fn, q_ref[...], kv_ref[...], gq_ref[...], gk_ref[...])
        dq, dkv, dgq, dgk = vjp(do_ref[...].astype(f32))
        dq_ref[...] = dq.astype(bf16)

        @pl.when(pl.program_id(0) == 0)
        def _():
            dkv_ref[...] = jnp.zeros_like(dkv_ref)
            dgq_ref[...] = jnp.zeros_like(dgq_ref)
            dgk_ref[...] = jnp.zeros_like(dgk_ref)

        dkv_ref[...] += dkv
        dgq_ref[...] += dgq
        dgk_ref[...] += dgk

    row = pl.BlockSpec((ts, XW), lambda i: (i, 0))
    gspec = pl.BlockSpec((1, XA_HEAD_DIM), lambda i: (0, 0))
    kvspec = pl.BlockSpec((M, 2 * XW), lambda i: (0, 0))
    gshape = jax.ShapeDtypeStruct((1, XA_HEAD_DIM), f32)
    return pl.pallas_call(
        body, name="xa_bwd", grid=(S // ts,), in_specs=[row, kvspec, gspec, gspec, row],
        out_specs=[row, kvspec, gspec, gspec],
        out_shape=[jax.ShapeDtypeStruct((S, XW), bf16), jax.ShapeDtypeStruct((M, 2 * XW), f32), gshape, gshape],
        compiler_params=_cp(("arbitrary",)),
    )(qx, kv, gq, gk, do)


def _s5_prep_fn(a_re, a_im, ldt, bT_re, bT_im, cT_re, cT_im):
    G, P, C = SSM_GROUPS, SSM_STATE, SSM_GROUP
    GP, GC = G * P, G * C
    lg_p, lg_c = P.bit_length() - 1, C.bit_length() - 1
    gi = lax.broadcasted_iota(jnp.int32, (G, GP), 0)
    ci = lax.broadcasted_iota(jnp.int32, (G, GP), 1) >> lg_p
    expand_dt = (gi == ci).astype(f32)
    dte = jnp.dot(jnp.exp(ldt), expand_dt, precision=lax.Precision.HIGHEST, preferred_element_type=f32)
    zr, zi = a_re * dte, a_im * dte
    mag = jnp.exp(zr)
    abr, abi = mag * jnp.cos(zi), mag * jnp.sin(zi)
    nr, ni = abr - 1.0, abi
    den = a_re * a_re + a_im * a_im
    cr = (nr * a_re + ni * a_im) / den
    cim = (ni * a_re - nr * a_im) / den
    bbr = cr * bT_re - cim * bT_im
    bbi = cr * bT_im + cim * bT_re
    rowg = lax.broadcasted_iota(jnp.int32, (GC, GP), 0) >> lg_c
    colg = lax.broadcasted_iota(jnp.int32, (GC, GP), 1) >> lg_p
    diag = rowg == colg

    def expand(t):
        return jnp.where(diag, jnp.broadcast_to(t[None], (G, C, GP)).reshape(GC, GP), 0.0)

    return abr, abi, expand(bbr), expand(bbi), expand(cT_re), expand(-cT_im)


def s5_prep(a_re, a_im, ldt, bT_re, bT_im, cT_re, cT_im):
    GP, GC = SSM_GROUPS * SSM_STATE, SSM_GROUPS * SSM_GROUP

    def body(a_re_ref, a_im_ref, ldt_ref, bTr_ref, bTi_ref, cTr_ref, cTi_ref, abr_ref, abi_ref, B_ref, C_ref):
        abr, abi, Br, Bi, Cr, Ci = _s5_prep_fn(a_re_ref[...], a_im_ref[...], ldt_ref[...], bTr_ref[...],
                                               bTi_ref[...], cTr_ref[...], cTi_ref[...])
        abr_ref[...] = abr
        abi_ref[...] = abi
        B_ref[0] = Br.astype(bf16)
        B_ref[1] = Bi.astype(bf16)
        C_ref[0] = Cr.astype(bf16)
        C_ref[1] = Ci.astype(bf16)

    vec = jax.ShapeDtypeStruct((1, GP), f32)
    mat = jax.ShapeDtypeStruct((2, GC, GP), bf16)
    return pl.pallas_call(body, name="s5_prep", out_shape=[vec, vec, mat, mat], compiler_params=_cp())(
        a_re, a_im, ldt, bT_re, bT_im, cT_re, cT_im)


def s5_prep_bwd(a_re, a_im, ldt, bT_re, bT_im, cT_re, cT_im, dabr, dabi, dB, dC):
    def body(a_re_ref, a_im_ref, ldt_ref, bTr_ref, bTi_ref, cTr_ref, cTi_ref, dabr_ref, dabi_ref, dB_ref, dC_ref,
             *outs):
        _, vjp = jax.vjp(_s5_prep_fn, a_re_ref[...], a_im_ref[...], ldt_ref[...], bTr_ref[...], bTi_ref[...],
                         cTr_ref[...], cTi_ref[...])
        grads = vjp((dabr_ref[...], dabi_ref[...], dB_ref[0], dB_ref[1], dC_ref[0], dC_ref[1]))
        for o_ref, gv in zip(outs, grads):
            o_ref[...] = gv

    ins = (a_re, a_im, ldt, bT_re, bT_im, cT_re, cT_im)
    return pl.pallas_call(body, name="s5_prep_bwd", out_shape=[jax.ShapeDtypeStruct(v.shape, f32) for v in ins],
                          compiler_params=_cp())(*ins, dabr, dabi, dB, dC)


def _cmul(ar, ai, br, bi):
    return ar * br - ai * bi, ar * bi + ai * br


SCAN_CHUNKS = 32


def _chunk_carry(Lr, Li, Pr, Pi, scratch, reverse):
    lr_ref, li_ref, cr_ref, ci_ref = scratch
    lr_ref[...] = Lr
    li_ref[...] = Li
    cur_r = jnp.zeros((1, LANES), f32)
    cur_i = jnp.zeros((1, LANES), f32)
    order = range(SCAN_CHUNKS - 1, -1, -1) if reverse else range(SCAN_CHUNKS)
    for c in order:
        cr_ref[pl.ds(c, 1), :] = cur_r
        ci_ref[pl.ds(c, 1), :] = cur_i
        mr, mi = _cmul(Pr, Pi, cur_r, cur_i)
        cur_r, cur_i = lr_ref[pl.ds(c, 1), :] + mr, li_ref[pl.ds(c, 1), :] + mi
    return cr_ref[...], ci_ref[...]


def _chunk_rows(j):
    return pl.ds(pl.multiple_of(j * SCAN_CHUNKS, SCAN_CHUNKS), SCAN_CHUNKS)


def row_shuffle(x, a, b, *, name, add=None, out_dtype=f32):
    S, W = x.shape
    assert a * b == S and x.dtype == f32

    def body(*refs):
        x_ref, o_ref = refs[0], refs[-1]

        def step(i, _):
            dst = pl.ds(pl.multiple_of(i * b, b), b)
            v = x_ref[pl.ds(i, b, stride=a), :]
            if add is not None:
                v = v + refs[1][dst, :]
            o_ref[dst, :] = v.astype(out_dtype)
            return 0

        lax.fori_loop(0, a, step, 0)

    col = pl.BlockSpec((S, LANES), lambda t: (0, t))
    args = [x] + ([add] if add is not None else [])
    return pl.pallas_call(
        body, name=name, grid=(W // LANES,), in_specs=[col] * len(args), out_specs=col,
        out_shape=jax.ShapeDtypeStruct((S, W), out_dtype), compiler_params=_cp(("parallel",)),
    )(*args)


def _scan_scratch(n):
    small = pltpu.VMEM((SCAN_CHUNKS, LANES), f32)
    return [pltpu.VMEM((n, LANES), f32), pltpu.VMEM((n, LANES), f32), small, small, small, small]


def _drive(x_ref, m_ref, work_ref):
    S = x_ref.shape[0]
    rows = min(S, 1024)
    m = jnp.concatenate([m_ref[0], m_ref[1]], axis=1)

    def chunk(c, _):
        r = pl.ds(pl.multiple_of(c * rows, rows), rows)
        y = _dot(x_ref[r, :], m, _NN)
        work_ref[0, r, :] = y[:, :LANES]
        work_ref[1, r, :] = y[:, LANES:]
        return 0

    lax.fori_loop(0, S // rows, chunk, 0)


def scan_fwd(u_il, Bm, abr, abi, exchange):
    S, C = u_il.shape
    N = Bm.shape[2]
    n = S // SCAN_CHUNKS
    shp = (SCAN_CHUNKS, LANES)

    def body(u_ref, B_ref, ar_ref, ai_ref, st_ref, work_ref, pwr_ref, pwi_ref, *scratch):
        a1r, a1i = ar_ref[...], ai_ref[...]
        ar = jnp.broadcast_to(a1r, shp)
        ai = jnp.broadcast_to(a1i, shp)
        _drive(u_ref, B_ref, work_ref)
        sr_ref, si_ref = work_ref.at[0], work_ref.at[1]

        def step(j, c):
            sr, si, pr, pi = c
            rows = _chunk_rows(j)
            mr, mi = _cmul(ar, ai, sr, si)
            sr, si = mr + sr_ref[rows, :], mi + si_ref[rows, :]
            sr_ref[rows, :] = sr
            si_ref[rows, :] = si
            pwr_ref[pl.ds(j, 1), :] = pr
            pwi_ref[pl.ds(j, 1), :] = pi
            npr, npi = _cmul(a1r, a1i, pr, pi)
            return sr, si, npr, npi

        z = jnp.zeros(shp, f32)
        sr, si, _, _ = lax.fori_loop(0, n, step, (z, z, a1r, a1i), unroll=2)
        cr, ci = _chunk_carry(sr, si, pwr_ref[pl.ds(n - 1, 1), :], pwi_ref[pl.ds(n - 1, 1), :], scratch, False)

        def step2(j, _):
            rows = _chunk_rows(j)
            pr = jnp.broadcast_to(pwr_ref[pl.ds(j, 1), :], shp)
            pi = jnp.broadcast_to(pwi_ref[pl.ds(j, 1), :], shp)
            mr, mi = _cmul(pr, pi, cr, ci)
            st_ref[0, rows, :] = (sr_ref[rows, :] + mr).astype(bf16)
            st_ref[1, rows, :] = (si_ref[rows, :] + mi).astype(bf16)
            return 0

        lax.fori_loop(0, n, step2, 0, unroll=4)

    blk = pl.BlockSpec((2, S, LANES), lambda t: (0, 0, t))
    vec = pl.BlockSpec((1, LANES), lambda t: (0, t))
    (st,), moved = _call_with_exchange(
        body, (u_il, Bm, abr, abi), exchange, name="scan_fwd", grid=(N // LANES,),
        in_specs=[pl.BlockSpec((S, C), lambda t: (0, 0)), pl.BlockSpec((2, C, LANES), lambda t: (0, 0, t)), vec, vec],
        out_specs=[blk], out_shape=[jax.ShapeDtypeStruct((2, S, N), bf16)],
        scratch_shapes=[pltpu.VMEM((2, S, LANES), f32)] + _scan_scratch(n))
    return st, moved


def scan_bwd(dy_il, Cm, st, abr, abi, exchange):
    _, S, N = st.shape
    C = dy_il.shape[1]
    n = S // SCAN_CHUNKS
    shp = (SCAN_CHUNKS, LANES)

    def body(dy_ref, C_ref, st_ref, ar_ref, ai_ref, g_ref, dar_ref, dai_ref, work_ref, qwr_ref, qwi_ref, *scratch):
        a1r, a1i = ar_ref[...], -ai_ref[...]
        ar = jnp.broadcast_to(a1r, shp)
        nai = jnp.broadcast_to(a1i, shp)
        _drive(dy_ref, C_ref, work_ref)
        gr_ref, gi_ref = work_ref.at[0], work_ref.at[1]
        sr_ref, si_ref = st_ref.at[0], st_ref.at[1]

        def step(jj, c):
            gr, gi, qr, qi = c
            j = n - 1 - jj
            rows = _chunk_rows(j)
            mr, mi = _cmul(ar, nai, gr, gi)
            gr, gi = mr + gr_ref[rows, :], mi + gi_ref[rows, :]
            gr_ref[rows, :] = gr
            gi_ref[rows, :] = gi
            qwr_ref[pl.ds(j, 1), :] = qr
            qwi_ref[pl.ds(j, 1), :] = qi
            nqr, nqi = _cmul(a1r, a1i, qr, qi)
            return gr, gi, nqr, nqi

        z = jnp.zeros(shp, f32)
        gr, gi, _, _ = lax.fori_loop(0, n, step, (z, z, a1r, a1i), unroll=2)
        cr, ci = _chunk_carry(gr, gi, qwr_ref[pl.ds(0, 1), :], qwi_ref[pl.ds(0, 1), :], scratch, True)
        sub = lax.broadcasted_iota(jnp.int32, shp, 0)

        def fix(j, spr, spi, acc):
            rows = _chunk_rows(j)
            qr = jnp.broadcast_to(qwr_ref[pl.ds(j, 1), :], shp)
            qi = jnp.broadcast_to(qwi_ref[pl.ds(j, 1), :], shp)
            mr, mi = _cmul(qr, qi, cr, ci)
            gr = gr_ref[rows, :] + mr
            gi = gi_ref[rows, :] + mi
            g_ref[0, rows, :] = gr.astype(bf16)
            g_ref[1, rows, :] = gi.astype(bf16)
            return acc[0] + gr * spr + gi * spi, acc[1] + gi * spr - gr * spi

        last = _chunk_rows(n - 1)
        spr = jnp.where(sub == 0, 0.0, pltpu.roll(sr_ref[last, :].astype(f32), 1, 0))
        spi = jnp.where(sub == 0, 0.0, pltpu.roll(si_ref[last, :].astype(f32), 1, 0))
        acc = fix(0, spr, spi, (z, z))

        def step2(j, acc):
            prev = _chunk_rows(j - 1)
            return fix(j, sr_ref[prev, :].astype(f32), si_ref[prev, :].astype(f32), acc)

        acc = lax.fori_loop(1, n, step2, acc)
        dar_ref[...] = jnp.sum(acc[0], axis=0, keepdims=True)
        dai_ref[...] = jnp.sum(acc[1], axis=0, keepdims=True)

    blk = pl.BlockSpec((2, S, LANES), lambda t: (0, 0, t))
    vec = pl.BlockSpec((1, LANES), lambda t: (0, t))
    vshape = jax.ShapeDtypeStruct((1, N), f32)
    (g, dar, dai), moved = _call_with_exchange(
        body, (dy_il, Cm, st, abr, abi), exchange, name="scan_bwd", grid=(N // LANES,),
        in_specs=[pl.BlockSpec((S, C), lambda t: (0, 0)), pl.BlockSpec((2, C, LANES), lambda t: (0, 0, t)), blk,
                  vec, vec],
        out_specs=[blk, vec, vec], out_shape=[jax.ShapeDtypeStruct((2, S, N), bf16), vshape, vshape],
        scratch_shapes=[pltpu.VMEM((2, S, LANES), f32)] + _scan_scratch(n))
    return g, dar, dai, moved


def _glu_fn(ypre, wglu):
    y = jax.nn.gelu(ypre)
    return y * jax.nn.sigmoid(bdot_nn(y, wglu))


def glu_fwd(ypre0, u, d, wglu, g_out, *, ts=512):
    S, W = u.shape
    ts = _tile(S, ts)

    def body(y0_ref, u_ref, d_ref, w_ref, g_ref, ypre_ref, z_ref, zn_ref):
        ypre = y0_ref[...] + d_ref[...] * u_ref[...]
        z = _glu_fn(ypre, w_ref[...])
        ypre_ref[...] = ypre
        z_ref[...] = z
        zn_ref[...] = _rms(z, g_ref[...], W).astype(bf16)

    row = pl.BlockSpec((ts, W), lambda i: (i, 0))
    vec = pl.BlockSpec((1, W), lambda i: (0, 0))
    full = jax.ShapeDtypeStruct((S, W), f32)
    return pl.pallas_call(
        body, name="glu_fwd", grid=(S // ts,),
        in_specs=[row, row, vec, pl.BlockSpec((W, W), lambda i: (0, 0)), vec], out_specs=[row, row, row],
        out_shape=[full, full, jax.ShapeDtypeStruct((S, W), bf16)], compiler_params=_cp(("parallel",)),
    )(ypre0, u, d, wglu, g_out)


def glu_bwd(ypre, u, d, wglu, dz, *, ts=512):
    S, W = u.shape
    ts = _tile(S, ts)

    def body(y_ref, u_ref, d_ref, w_ref, dz_ref, dy_ref, du_ref, dw_ref, dd_ref):
        _, vjp = jax.vjp(_glu_fn, y_ref[...], w_ref[...])
        dy, dw = vjp(dz_ref[...])
        dy_ref[...] = dy
        du_ref[...] = d_ref[...] * dy

        @pl.when(pl.program_id(0) == 0)
        def _():
            dw_ref[...] = jnp.zeros_like(dw_ref)
            dd_ref[...] = jnp.zeros_like(dd_ref)

        dw_ref[...] += dw
        dd_ref[...] += jnp.sum(dy * u_ref[...], axis=0, keepdims=True)

    row = pl.BlockSpec((ts, W), lambda i: (i, 0))
    vec = pl.BlockSpec((1, W), lambda i: (0, 0))
    sq = pl.BlockSpec((W, W), lambda i: (0, 0))
    full = jax.ShapeDtypeStruct((S, W), f32)
    return pl.pallas_call(
        body, name="glu_bwd", grid=(S // ts,), in_specs=[row, row, vec, sq, row], out_specs=[row, row, sq, vec],
        out_shape=[full, full, jax.ShapeDtypeStruct((W, W), f32), jax.ShapeDtypeStruct((1, W), f32)],
        compiler_params=_cp(("arbitrary",)),
    )(ypre, u, d, wglu, dz)


def loss_head(y, target, *, ts=512):
    S, D = y.shape
    ts = _tile(S, ts)

    def body(y_ref, t_ref, dy_ref, l_ref, dyb_ref):
        err = y_ref[...] - t_ref[...]
        dy_ref[...] = err * (1.0 / D)
        dyb_ref[...] = (err * (1.0 / D)).astype(bf16)

        @pl.when(pl.program_id(0) == 0)
        def _():
            l_ref[...] = jnp.zeros_like(l_ref)

        rows = jnp.sum(err * err, axis=1, keepdims=True) * (1.0 / D)
        l_ref[...] += 0.5 * jnp.sum(rows, axis=0, keepdims=True)

    row = pl.BlockSpec((ts, D), lambda i: (i, 0))
    return pl.pallas_call(
        body, name="loss_head", grid=(S // ts,), in_specs=[row, row],
        out_specs=[row, pl.BlockSpec((1, 1), lambda i: (0, 0)), row],
        out_shape=[jax.ShapeDtypeStruct((S, D), f32), jax.ShapeDtypeStruct((1, 1), f32),
                   jax.ShapeDtypeStruct((S, D), bf16)],
        compiler_params=_cp(("arbitrary",)),
    )(y, target)


def adamw(w, g, m, v, *, name, tr=256):
    R, C = w.shape
    tr = _row_tile(R, tr)

    def body(w_ref, g_ref, m_ref, v_ref, d_ref, nm_ref, nv_ref):
        gv = g_ref[...]
        nm = ADAM_B1 * m_ref[...] + (1.0 - ADAM_B1) * gv
        nv = ADAM_B2 * v_ref[...] + (1.0 - ADAM_B2) * jnp.square(gv)
        m_hat = nm / (1.0 - ADAM_B1 ** ADAM_STEP)
        v_hat = nv / (1.0 - ADAM_B2 ** ADAM_STEP)
        d_ref[...] = -ADAM_LR * (m_hat / (jnp.sqrt(v_hat) + ADAM_EPS) + ADAM_WD * w_ref[...])
        nm_ref[...] = nm
        nv_ref[...] = nv

    row = pl.BlockSpec((tr, C), lambda i: (i, 0))
    full = jax.ShapeDtypeStruct((R, C), f32)
    return pl.pallas_call(
        body, name=name, grid=(R // tr,), in_specs=[row] * 4, out_specs=[row] * 3, out_shape=[full] * 3,
        compiler_params=_cp(("parallel",)),
    )(w, g, m, v)


def add_half(g4, recv, c, *, name, tr=256):
    _, _, Rh, C = g4.shape
    tr = _row_tile(Rh, tr)

    def body(c_ref, a_ref, b_ref, o_ref):
        o_ref[...] = a_ref[...] + b_ref[...]

    grid_spec = pltpu.PrefetchScalarGridSpec(
        num_scalar_prefetch=1, grid=(N_CHIPS, Rh // tr),
        in_specs=[pl.BlockSpec((None, None, tr, C), lambda k, i, c_ref: (k, c_ref[0], i, 0)),
                  pl.BlockSpec((None, tr, C), lambda k, i, c_ref: (k, i, 0))],
        out_specs=pl.BlockSpec((None, tr, C), lambda k, i, c_ref: (k, i, 0)))
    return pl.pallas_call(body, name=name, grid_spec=grid_spec, out_shape=jax.ShapeDtypeStruct(recv.shape, f32),
                          compiler_params=_cp(("parallel", "parallel")))(c, g4, recv)


def sum_chips(p4, *, name, tr=256):
    _, Rh, C = p4.shape
    tr = _row_tile(Rh, tr)

    def body(a_ref, b_ref, c_ref, d_ref, o_ref):
        o_ref[...] = ((a_ref[...] + b_ref[...]) + c_ref[...]) + d_ref[...]

    spec = lambda k: pl.BlockSpec((None, tr, C), lambda i: (k, i, 0))
    return pl.pallas_call(
        body, name=name, grid=(Rh // tr,), in_specs=[spec(0), spec(1), spec(2), spec(3)],
        out_specs=pl.BlockSpec((tr, C), lambda i: (i, 0)), out_shape=jax.ShapeDtypeStruct((Rh, C), f32),
        compiler_params=_cp(("parallel",)),
    )(p4, p4, p4, p4)


def _place():
    return lax.axis_index("x"), lax.axis_index("y"), lax.axis_index("c")


def _other_chips(x, y):
    return [(1 - x, y), (x, 1 - y), (1 - x, 1 - y)]


def _chip_exchange(ins, outs, sems, scatter, start):
    if not ins:
        return
    send, recv, loc = sems
    x, y, c = _place()
    me = 2 * x + y
    for a in range(len(ins)):
        own = pltpu.make_async_copy(ins[a].at[me] if scatter else ins[a], outs[a].at[me], loc.at[a])
        own.start() if start else own.wait()
        for p, (px, py) in enumerate(_other_chips(x, y)):
            k = 2 * px + py
            cp = pltpu.make_async_remote_copy(
                src_ref=ins[a].at[k] if scatter else ins[a], dst_ref=outs[a].at[me if start else k],
                send_sem=send.at[3 * a + p], recv_sem=recv.at[3 * a + p], device_id=(px, py, c), device_id_type=MESH)
            cp.start() if start else cp.wait()


def _chip_exchange_args(arrs, scatter):
    n = len(arrs)
    shapes = [jax.ShapeDtypeStruct(a.shape if scatter else (N_CHIPS,) + a.shape, a.dtype) for a in arrs]
    sems = [pltpu.SemaphoreType.DMA((3 * n,)), pltpu.SemaphoreType.DMA((3 * n,)), pltpu.SemaphoreType.DMA((n,))]
    return shapes, sems if n else []


def _chip_exchange_call(arrs, scatter, name):
    n = len(arrs)

    def body(*refs):
        ins, outs, sems = refs[:n], refs[n:2 * n], refs[2 * n:]
        _chip_exchange(ins, outs, sems, scatter, True)
        _chip_exchange(ins, outs, sems, scatter, False)

    shapes, sems = _chip_exchange_args(arrs, scatter)
    return pl.pallas_call(
        body, name=name, in_specs=[ANY] * n, out_specs=[ANY] * n, out_shape=shapes, scratch_shapes=sems,
        compiler_params=pltpu.CompilerParams(has_side_effects=True),
    )(*arrs)


def allgather_chips(arrs, *, name):
    return _chip_exchange_call(arrs, False, name)


def sibling_swap(arrs, *, half, name):
    n = len(arrs)

    def body(*refs):
        ins, outs = refs[:n], refs[n:2 * n]
        send, recv = refs[2 * n:]
        x, y, c = _place()
        cps = []
        for a in range(n):
            src = ins[a].at[:, 1 - c] if half else ins[a]
            cp = pltpu.make_async_remote_copy(src_ref=src, dst_ref=outs[a], send_sem=send.at[a], recv_sem=recv.at[a],
                                              device_id=(x, y, 1 - c), device_id_type=MESH)
            cp.start()
            cps.append(cp)
        for cp in cps:
            cp.wait()

    def oshape(a):
        return jax.ShapeDtypeStruct((a.shape[0],) + a.shape[2:] if half else a.shape, a.dtype)

    return pl.pallas_call(
        body, name=name, in_specs=[ANY] * n, out_specs=[ANY] * n, out_shape=[oshape(a) for a in arrs],
        scratch_shapes=[pltpu.SemaphoreType.DMA((n,)), pltpu.SemaphoreType.DMA((n,))],
        compiler_params=pltpu.CompilerParams(has_side_effects=True),
    )(*arrs)


def chip_scatter(arrs, *, name):
    return _chip_exchange_call(arrs, True, name)


def _pad_cols(w):
    K = w.shape[0]
    w = w.reshape(K, -1, SB_HEAD_DIM)
    return jnp.pad(w, ((0, 0), (0, 0), (0, LANES - SB_HEAD_DIM))).reshape(K, -1)


def _unpad_cols(w):
    K = w.shape[0]
    return w.reshape(K, -1, LANES)[:, :, :SB_HEAD_DIM].reshape(K, -1)


def _pad_rows(w):
    N = w.shape[1]
    w = w.reshape(-1, SB_HEAD_DIM, N)
    return jnp.pad(w, ((0, 0), (0, LANES - SB_HEAD_DIM), (0, 0))).reshape(-1, N)


def _unpad_rows(w):
    N = w.shape[1]
    return w.reshape(-1, LANES, N)[:, :SB_HEAD_DIM, :].reshape(-1, N)


_PACK_ROWS = N_CHIPS * 2 * SUBLANES


def _pack(arrs):
    flat = jnp.concatenate([a.reshape(-1) for a in arrs])
    rows = -(-flat.shape[0] // LANES)
    rows = -(-rows // _PACK_ROWS) * _PACK_ROWS
    return jnp.pad(flat, (0, rows * LANES - flat.shape[0])).reshape(rows, LANES)


def _unpack(buf, shapes):
    flat = buf.reshape(-1)
    out, pos = [], 0
    for shp in shapes:
        size = 1
        for d in shp:
            size *= d
        out.append(flat[pos:pos + size].reshape(shp))
        pos += size
    return out


BIG = ("w_in", "ssm_w_glu", "w_out", "xa_w_q", "xa_w_kv", "xa_w_o", "w_up", "w_down")
SMALL = ("g_mix", "ssm_a_re", "ssm_a_im", "ssm_log_dt", "ssm_b_re", "ssm_b_im", "ssm_c_re", "ssm_c_im", "ssm_d",
         "sb_g_q", "sb_g_k", "g_out_ssm", "g_out_sb", "g_xa", "g_mem", "xa_g_q", "xa_g_k", "g_mlp")
WEIGHTS = ("g_mix", "w_in", "ssm_a_re", "ssm_a_im", "ssm_log_dt", "ssm_b_re", "ssm_b_im", "ssm_c_re", "ssm_c_im",
           "ssm_d", "ssm_w_glu", "sb_g_q", "sb_g_k", "g_out_ssm", "g_out_sb", "w_out", "g_xa", "g_mem", "xa_w_q",
           "xa_w_kv", "xa_g_q", "xa_g_k", "xa_w_o", "g_mlp", "w_up", "w_down")


def kernel(x, mem, g_mix, w_in, ssm_a_re, ssm_a_im, ssm_log_dt, ssm_b_re, ssm_b_im, ssm_c_re, ssm_c_im, ssm_d, ssm_w_glu, sb_g_q, sb_g_k, g_out_ssm, g_out_sb, w_out, g_xa, g_mem, xa_w_q, xa_w_kv, xa_g_q, xa_g_k, xa_w_o, g_mlp, w_up, w_down, loss_target, m_g_mix, m_w_in, m_ssm_a_re, m_ssm_a_im, m_ssm_log_dt, m_ssm_b_re, m_ssm_b_im, m_ssm_c_re, m_ssm_c_im, m_ssm_d, m_ssm_w_glu, m_sb_g_q, m_sb_g_k, m_g_out_ssm, m_g_out_sb, m_w_out, m_g_xa, m_g_mem, m_xa_w_q, m_xa_w_kv, m_xa_g_q, m_xa_g_k, m_xa_w_o, m_g_mlp, m_w_up, m_w_down, v_g_mix, v_w_in, v_ssm_a_re, v_ssm_a_im, v_ssm_log_dt, v_ssm_b_re, v_ssm_b_im, v_ssm_c_re, v_ssm_c_im, v_ssm_d, v_ssm_w_glu, v_sb_g_q, v_sb_g_k, v_g_out_ssm, v_g_out_sb, v_w_out, v_g_xa, v_g_mem, v_xa_w_q, v_xa_w_kv, v_xa_g_q, v_xa_g_k, v_xa_w_o, v_g_mlp, v_w_up, v_w_down):
    env = dict(locals())
    W = {n: env[n] for n in WEIGHTS}
    M1 = {n: env["m_" + n] for n in WEIGHTS}
    V2 = {n: env["v_" + n] for n in WEIGHTS}
    xs, mems, tgt = x[0], mem[0], loss_target[0]
    S, D = xs.shape
    G, P, C = SSM_GROUPS, SSM_STATE, SSM_GROUP
    GP = G * P
    SBW = SB_HEADS * SB_HEAD_DIM
    c_idx = lax.axis_index("c")

    big = dict(tn=1024, tk=1024)
    wide = dict(tm=1024, tn=1024, tk=2048)
    h0, (g_in,) = rms_norm(xs, g_mix, D, name="norm_x", exchange=([w_in[0].astype(bf16)], False))
    Wu = g_in[0]
    Wqkv = jnp.concatenate([_pad_cols(g_in[1]), _pad_cols(g_in[2]), _pad_cols(g_in[3])], axis=1)
    gq_pad, gk_pad = _pad_cols(sb_g_q), _pad_cols(sb_g_k)
    gosb_pad = _pad_cols(g_out_sb)
    a_re, a_im = ssm_a_re.reshape(1, GP), ssm_a_im.reshape(1, GP)
    bT_re = ssm_b_re[0].transpose(2, 0, 1).reshape(C, GP)
    bT_im = ssm_b_im[0].transpose(2, 0, 1).reshape(C, GP)
    cT_re = ssm_c_re[0].transpose(1, 0, 2).reshape(C, GP)
    cT_im = ssm_c_im[0].transpose(1, 0, 2).reshape(C, GP)
    s5_in = (a_re, a_im, ssm_log_dt, bT_re, bT_im, cT_re, cT_im)

    u = mm(h0, Wu, mode="nn", name="proj_u", tk=1024)
    shard = {n: W[n][0].astype(bf16) for n in BIG[1:]}
    qkv, (g_glu, g_out, g_xq, g_xkv, g_xo) = mm(
        h0, Wqkv, mode="nn", name="proj_qkv", tm=1024,
        exchange=([shard[n] for n in ("ssm_w_glu", "w_out", "xa_w_q", "xa_w_kv", "xa_w_o")], False), **big)
    qn, kn, vb = qkv_prep(qkv, gq_pad, gk_pad)
    o, (g_down,) = sb_fwd(qn, kn, vb, ([shard["w_down"]], False))
    Wglu = g_glu.reshape(-1, g_glu.shape[-1])
    Wout = g_out.reshape(-1, g_out.shape[-1])
    Wo_ssm, Wo_sb = Wout[:SBW], _pad_rows(Wout[SBW:])
    Wxq = g_xq.reshape(-1, g_xq.shape[-1])
    Wxkv = g_xkv.reshape(-1, g_xkv.shape[-1])
    Wxo = g_xo.transpose(1, 0, 2).reshape(g_xo.shape[1], -1)
    abr, abi, Bm, Cm = s5_prep(*s5_in)
    n_pos = S // SCAN_CHUNKS
    u_il = row_shuffle(u, n_pos, SCAN_CHUNKS, name="u_interleave", out_dtype=bf16)
    st, (g_up,) = scan_fwd(u_il, Bm, abr, abi, ([shard["w_up"]], False))
    ypre0_il = mm(st, Cm, mode="nt", name="s5_y", a_shards=2, b_shards=2, tm=1024, tk=2048)
    Wup = g_up.transpose(1, 0, 2).reshape(g_up.shape[1], -1)
    Wdown = g_down.reshape(-1, g_down.shape[-1])
    ypre0 = row_shuffle(ypre0_il, SCAN_CHUNKS, n_pos, name="y_token_order")
    ypre, z, zn = glu_fwd(ypre0, u, ssm_d, Wglu, g_out_ssm)
    on = rms_norm(o, gosb_pad, SBW, name="norm_o")
    x1a = mm(zn, Wo_ssm, mode="nn", name="out_ssm", epi="add", aux=xs, tn=1024)
    x1 = mm(on, Wo_sb, mode="nn", name="out_sb", epi="add", aux=x1a, **big)
    h1 = rms_norm(x1, g_xa, D, name="norm_x1")
    qx = mm(h1, Wxq, mode="nn", name="xa_q", tk=1024)
    memn = rms_norm(mems, g_mem, D, name="norm_mem")
    kv = mm(memn, Wxkv, mode="nn", name="xa_kv", **big)
    ox = xa_fwd(qx, kv, xa_g_q, xa_g_k)
    x2 = mm(ox, Wxo, mode="nn", name="xa_o", epi="add", aux=x1, tn=1024)
    h2 = rms_norm(x2, g_mlp, D, name="norm_x2")
    act = mm(h2, Wup, mode="nn", name="mlp_up", out_dtype=bf16, tm=1024, tn=2048, tk=1024)
    x3 = mm(act, Wdown, mode="nn", name="mlp_down", pro="relu2", epi="add", aux=x2, **wide)
    dx3, loss_part, dx3b = loss_head(x3, tgt)
    loss = lax.psum(loss_part[0, 0], ("x", "y", "c"))

    dact = mm(dx3b, Wdown, mode="nt", name="d_act", epi="mul2relu", aux=act, out_dtype=bf16, tm=1024, tn=2048,
              tk=1024)
    dWdown = mm(act, dx3b, mode="tn", name="dw_down", pro="relu2", **wide)
    dWup = mm(h2, dact, mode="tn", name="dw_up", out_shards=N_CHIPS, **wide)
    dx2, dg_mlp, dx2b = mm(dact, Wup, mode="nt", name="d_h2", norm_bwd=(x2, g_mlp, dx3, D), tn=1024, tk=2048)
    dox = mm(dx2b, Wxo, mode="nt", name="d_ox", out_dtype=bf16, tk=1024)
    dWxo = mm(ox, dx2b, mode="tn", name="dw_xo", out_shards=N_CHIPS, tk=1024)
    dqx, dkv, dg_xq, dg_xk = xa_bwd(qx, kv, xa_g_q, xa_g_k, dox)
    dWxq = mm(h1, dqx, mode="tn", name="dw_xq", tm=1024, tk=1024)
    dx1, dg_xa, dx1b = mm(dqx, Wxq, mode="nt", name="d_h1", norm_bwd=(x1, g_xa, dx2, D), tn=1024)
    dWxkv = mm(memn, dkv, mode="tn", name="dw_xkv", tm=1024, tn=1024)
    dmemn = mm(dkv, Wxkv, mode="nt", name="d_memn", **big)
    _, dg_mem = rms_bwd(mems, g_mem, dmemn, None, D, name="rms_bwd_mem")
    dyn_ssm = mm(dx1b, Wo_ssm, mode="nt", name="d_yn_ssm", tk=1024)
    dyn_sb = mm(dx1b, Wo_sb, mode="nt", name="d_yn_sb", **big)
    dWo_ssm = mm(zn, dx1b, mode="tn", name="dw_out_ssm", **big)
    dWo_sb = mm(on, dx1b, mode="tn", name="dw_out_sb", tm=1024, **big)
    dz, dg_os = rms_bwd(z, g_out_ssm, dyn_ssm, None, SBW, name="rms_bwd_ssm")
    do, dg_osb = rms_bwd(o, gosb_pad, dyn_sb, None, SBW, name="rms_bwd_sb")
    c_arr = c_idx.astype(jnp.int32).reshape(1)

    def sibling_sums(grads, names, tag):
        g4 = [g.reshape(N_CHIPS, 2, g.shape[1] // 2, g.shape[2]) for g in grads]
        from_sib = sibling_swap(g4, half=True, name="grad_to_sibling_" + tag)
        return [add_half(a, b, c_arr, name="add_sibling_" + n) for a, b, n in zip(g4, from_sib, names)]

    early = ("xa_w_q", "xa_w_kv", "xa_w_o", "w_up", "w_down")
    early_g = [dWxq.reshape(N_CHIPS, -1, dWxq.shape[1]), dWxkv.reshape(N_CHIPS, -1, dWxkv.shape[1]), dWxo, dWup,
               dWdown.reshape(N_CHIPS, -1, D)]
    pair = sibling_sums(early_g, early, "early")
    dqn, dkn, dv, parts_mlp = sb_bwd(qn, kn, vb, do, (pair[3:], True))
    dqkv, dg_q, dg_k, parts_xa = qkv_bwd(qkv, gq_pad, gk_pad, dqn, dkn, dv, (pair[:3], True))
    dypre, du_skip, dWglu, dd = glu_bwd(ypre, u, ssm_d, Wglu, dz)
    mid = ("ssm_w_glu", "w_out")
    mid_g = [dWglu.reshape(N_CHIPS, -1, dWglu.shape[1]),
             jnp.concatenate([dWo_ssm, _unpad_rows(dWo_sb)]).reshape(N_CHIPS, -1, D)]
    dypre_il = row_shuffle(dypre, n_pos, SCAN_CHUNKS, name="dy_interleave", out_dtype=bf16)
    dCm = mm(dypre_il, st, mode="tn", name="d_cmat", b_shards=2, out_shards=2, **wide)
    gst, dabr, dabi, parts_mid = scan_bwd(dypre_il, Cm, st, abr, abi, (sibling_sums(mid_g, mid, "mid"), True))
    dBm = mm(u_il, gst, mode="tn", name="d_bmat", b_shards=2, out_shards=2, **wide)
    du_il = mm(gst, Bm, mode="nt", name="d_u", a_shards=2, b_shards=2, tm=1024, tk=2048)
    mine = {n: sum_chips(p, name="sum_chips_" + n)
            for n, p in zip(early + mid, [*parts_xa, *parts_mlp, *parts_mid])}
    du = row_shuffle(du_il, SCAN_CHUNKS, n_pos, name="du_token_order", add=du_skip, out_dtype=bf16)
    s5_g = s5_prep_bwd(*s5_in, dabr, dabi, dBm, dCm)
    dWu = mm(h0, du, mode="tn", name="dw_u", tm=1024, tk=1024)
    dWqkv = mm(h0, dqkv, mode="tn", name="dw_qkv", **wide)
    HW = SB_HEADS * LANES
    w_in_g = jnp.stack([dWu, _unpad_cols(dWqkv[:, :HW]), _unpad_cols(dWqkv[:, HW:2 * HW]),
                        _unpad_cols(dWqkv[:, 2 * HW:])])
    dh0a = mm(du, Wu, mode="nt", name="d_h0_u", tn=1024)
    (dx, dg_mix, _), parts_in = mm(dqkv, Wqkv, mode="nt", name="d_h0_qkv", epi="add", aux=dh0a,
                                   norm_bwd=(xs, g_mix, dx1, D),
                                   exchange=(sibling_sums([w_in_g], ("w_in",), "w_in"), True), **big)
    mine["w_in"] = sum_chips(parts_in[0], name="sum_chips_w_in")

    late = ("small",)
    late_g = []
    da_re, da_im, dldt, dbT_re, dbT_im, dcT_re, dcT_im = s5_g
    small_g = {
        "g_mix": dg_mix, "ssm_a_re": da_re, "ssm_a_im": da_im, "ssm_log_dt": dldt,
        "ssm_b_re": dbT_re.reshape(C, G, P).transpose(1, 2, 0), "ssm_b_im": dbT_im.reshape(C, G, P).transpose(1, 2, 0),
        "ssm_c_re": dcT_re.reshape(C, G, P).transpose(1, 0, 2), "ssm_c_im": dcT_im.reshape(C, G, P).transpose(1, 0, 2),
        "ssm_d": dd, "sb_g_q": dg_q[:, :SB_HEAD_DIM], "sb_g_k": dg_k[:, :SB_HEAD_DIM], "g_out_ssm": dg_os,
        "g_out_sb": _unpad_cols(dg_osb), "g_xa": dg_xa, "g_mem": dg_mem, "xa_g_q": dg_xq, "xa_g_k": dg_xk,
        "g_mlp": dg_mlp,
    }
    late_g.append(_pack([small_g[n] for n in SMALL]).reshape(N_CHIPS, -1, LANES))

    parts_late = chip_scatter(sibling_sums(late_g, late, "late"), name="grad_to_chips_late")
    mine.update({n: sum_chips(p, name="sum_chips_" + n) for n, p in zip(late, parts_late)})
    mine = [mine[n] for n in list(BIG) + ["small"]]
    other = sibling_swap(mine, half=False, name="grad_half_to_sibling")
    shard = [jnp.where(c_idx == 0, jnp.concatenate([a, b]), jnp.concatenate([b, a])) for a, b in zip(mine, other)]
    small_all = allgather_chips([shard[-1]], name="gather_small")[0]
    small_red = small_all.reshape(-1, LANES)

    out = {}
    for n, gs in zip(BIG, shard[:-1]):
        shp = W[n].shape
        w2, m2, v2 = (t.reshape(gs.shape) for t in (W[n], M1[n], V2[n]))
        d, nm, nv = adamw(w2, gs, m2, v2, name="adamw_" + n)
        out[n] = tuple(t.reshape(shp) for t in (gs, d, nm, nv))
    shapes = [W[n].shape for n in SMALL]
    d, nm, nv = adamw(_pack([W[n] for n in SMALL]), small_red, _pack([M1[n] for n in SMALL]),
                      _pack([V2[n] for n in SMALL]), name="adamw_small")
    for n, gs, dd_, mm_, vv_ in zip(SMALL, _unpack(small_red, shapes), _unpack(d, shapes), _unpack(nm, shapes),
                                    _unpack(nv, shapes)):
        out[n] = (gs, dd_, mm_, vv_)
    res = [loss, dx[None]]
    for kind in range(4):
        res += [out[n][kind] for n in WEIGHTS]
    return tuple(res)
```

```python
import jax
import jax.numpy as jnp
from jax import lax
from jax.experimental import pallas as pl
from jax.experimental.pallas import tpu as pltpu

f32 = jnp.float32
bf16 = jnp.bfloat16

NORM_EPS = 1e-6
SSM_GROUPS = 32
SSM_GROUP = 16
SSM_STATE = 64
SB_HEADS = 8
SB_HEAD_DIM = 64
XA_HEADS = 4
XA_HEAD_DIM = 128
LANES = 128
SUBLANES = 8
N_CHIPS = 4
ADAM_LR = 0.001
ADAM_B1 = 0.9
ADAM_B2 = 0.999
ADAM_EPS = 1e-08
ADAM_WD = 0.01
ADAM_STEP = 10
VMEM_LIMIT = 56 * 1024 * 1024
MESH = pl.DeviceIdType.MESH
ANY = pl.BlockSpec(memory_space=pl.ANY)


def _cp(sem=None):
    return pltpu.CompilerParams(dimension_semantics=sem, vmem_limit_bytes=VMEM_LIMIT)


def _tile(n, pref):
    if n <= pref:
        return n
    t = (pref // LANES) * LANES
    while t > LANES and n % t:
        t -= LANES
    assert n % t == 0, (n, pref)
    return t


def _row_tile(n, pref):
    if n <= pref:
        return n
    t = (pref // SUBLANES) * SUBLANES
    while n % t:
        t -= SUBLANES
    return t


def _dot(a, b, dims):
    return lax.dot_general(a.astype(bf16), b.astype(bf16), (dims, ((), ())), preferred_element_type=f32)


_NN = ((1,), (0,))
_NT = ((1,), (1,))
_TN = ((0,), (0,))


@jax.custom_vjp
def bdot_nn(a, b):
    return _dot(a, b, _NN)


def _bdot_nn_fwd(a, b):
    return _dot(a, b, _NN), (a, b)


def _bdot_nn_bwd(res, g):
    a, b = res
    return _dot(g, b, _NT), _dot(a, g, _TN)


bdot_nn.defvjp(_bdot_nn_fwd, _bdot_nn_bwd)


@jax.custom_vjp
def bdot_nt(a, b):
    return _dot(a, b, _NT)


def _bdot_nt_fwd(a, b):
    return _dot(a, b, _NT), (a, b)


def _bdot_nt_bwd(res, g):
    a, b = res
    return _dot(g, b, _NN), _dot(g, a, _TN)


bdot_nt.defvjp(_bdot_nt_fwd, _bdot_nt_bwd)


def _rms(x, g, denom):
    r = lax.rsqrt(jnp.sum(x * x, axis=-1, keepdims=True) * (1.0 / denom) + NORM_EPS)
    return x * r * g


def _opspec(block, row_of, col_of, shards, ncol_tiles):
    if shards == 1:
        return pl.BlockSpec(block, lambda i, j, k: (row_of(i, j, k), col_of(i, j, k)))
    per = ncol_tiles // shards
    return pl.BlockSpec((None,) + block,
                        lambda i, j, k: (col_of(i, j, k) // per, row_of(i, j, k), col_of(i, j, k) % per))


def _call_with_exchange(body, args, exchange, *, name, grid, in_specs, out_specs, out_shape, scratch_shapes=()):
    xs, scatter = exchange
    n, n_in, n_out, n_scr = len(xs), len(in_specs), len(out_specs), len(scratch_shapes)
    x_shapes, sems = _chip_exchange_args(xs, scatter)

    def wrapped(*refs):
        ins, x_ins = refs[:n_in], refs[n_in:n_in + n]
        outs, x_outs = refs[n_in + n:n_in + n + n_out], refs[n_in + n + n_out:n_in + 2 * n + n_out]
        scratch, x_sems = refs[n_in + 2 * n + n_out:n_in + 2 * n + n_out + n_scr], refs[n_in + 2 * n + n_out + n_scr:]
        first, last = True, True
        for d, steps in enumerate(grid):
            first = first & (pl.program_id(d) == 0)
            last = last & (pl.program_id(d) == steps - 1)

        @pl.when(first)
        def _():
            _chip_exchange(x_ins, x_outs, x_sems, scatter, True)

        body(*ins, *outs, *scratch)

        @pl.when(last)
        def _():
            _chip_exchange(x_ins, x_outs, x_sems, scatter, False)

    res = pl.pallas_call(
        wrapped, name=name, grid=grid, in_specs=list(in_specs) + [ANY] * n, out_specs=list(out_specs) + [ANY] * n,
        out_shape=list(out_shape) + x_shapes, scratch_shapes=list(scratch_shapes) + sems,
        compiler_params=_cp(("arbitrary",) * len(grid)),
    )(*args, *xs)
    return res[:n_out], res[n_out:]


def mm(a, b, *, mode, name, tm=512, tn=512, tk=512, pro="none", epi="none", aux=None,
       out_dtype=f32, a_shards=1, b_shards=1, out_shards=1, exchange=None, norm_bwd=None, loss_target=None):
    ar, ac = a.shape[-2], a.shape[-1] * a_shards
    br, bc = b.shape[-2], b.shape[-1] * b_shards
    if mode == "nn":
        M, K, N = ar, ac, bc
        assert br == K
    elif mode == "nt":
        M, K, N = ar, ac, br
        assert bc == K
    else:
        M, K, N = ac, ar, bc
        assert br == K
    tm, tn, tk = _tile(M, tm), _tile(N, tn), _tile(K, tk)
    if a_shards > 1:
        if mode == "tn":
            tm = _tile(ac // a_shards, tm)
        else:
            tk = _tile(ac // a_shards, tk)
    if b_shards > 1:
        if mode == "nt":
            tk = _tile(bc // b_shards, tk)
        else:
            tn = _tile(bc // b_shards, tn)
    if out_shards > 1:
        tn = _tile(N // out_shards, tn)
    nm, nn_, nk = M // tm, N // tn, K // tk
    I = lambda i, j, k: i
    J = lambda i, j, k: j
    Kk = lambda i, j, k: k
    if mode == "nn":
        a_spec = _opspec((tm, tk), I, Kk, a_shards, nk)
        b_spec = _opspec((tk, tn), Kk, J, b_shards, nn_)
        dims = _NN
    elif mode == "nt":
        a_spec = _opspec((tm, tk), I, Kk, a_shards, nk)
        b_spec = _opspec((tn, tk), J, Kk, b_shards, nk)
        dims = _NT
    else:
        a_spec = _opspec((tk, tm), Kk, I, a_shards, nm)
        b_spec = _opspec((tk, tn), Kk, J, b_shards, nn_)
        dims = _TN
    in_specs = [a_spec, b_spec]
    args = [a, b]
    tile = pl.BlockSpec((tm, tn), lambda i, j, k: (i, j))
    if epi != "none":
        in_specs.append(tile)
        args.append(aux)
    if out_shards == 1:
        out_spec = tile
        out_shape = jax.ShapeDtypeStruct((M, N), out_dtype)
    else:
        per = nn_ // out_shards
        out_spec = pl.BlockSpec((None, tm, tn), lambda i, j, k: (j // per, i, j % per))
        out_shape = jax.ShapeDtypeStruct((out_shards, M, N // out_shards), out_dtype)
    out_specs, out_shapes = [out_spec], [out_shape]
    if norm_bwd is not None:
        nx, ng, nres, denom = norm_bwd
        assert tn == N and out_shards == 1 and out_dtype == f32
        vec = pl.BlockSpec((1, tn), lambda i, j, k: (0, 0))
        in_specs += [tile, vec, tile]
        args += [nx, ng, nres]
        out_specs += [vec, tile]
        out_shapes += [jax.ShapeDtypeStruct((1, N), f32), jax.ShapeDtypeStruct((M, N), bf16)]
    if loss_target is not None:
        assert tn == N and out_shards == 1 and out_dtype == f32 and norm_bwd is None
        in_specs.append(tile)
        args.append(loss_target)
        out_specs += [pl.BlockSpec((1, 1), lambda i, j, k: (0, 0)), tile]
        out_shapes += [jax.ShapeDtypeStruct((1, 1), f32), jax.ShapeDtypeStruct((M, N), bf16)]
    n_in = len(in_specs)

    def body(*refs):
        a_ref, b_ref = refs[0], refs[1]
        aux_ref = refs[2] if epi != "none" else None
        o_ref, acc_ref = refs[n_in], refs[-1]
        i, k = pl.program_id(0), pl.program_id(2)

        @pl.when(k == 0)
        def _():
            acc_ref[...] = jnp.zeros_like(acc_ref)

        av = a_ref[...]
        if pro == "relu2":
            av = jnp.square(jnp.maximum(av.astype(f32), 0.0))
        acc_ref[...] += _dot(av, b_ref[...], dims)

        @pl.when(k == nk - 1)
        def _():
            res = acc_ref[...]
            if epi == "add":
                res = res + aux_ref[...].astype(f32)
            elif epi == "mul2relu":
                res = res * (2.0 * jnp.maximum(aux_ref[...].astype(f32), 0.0))
            if loss_target is not None:
                l_ref, twin_ref = refs[n_in + 1], refs[n_in + 2]
                err = res - refs[n_in - 1][...]
                dy = err * (1.0 / N)
                o_ref[...] = dy
                twin_ref[...] = dy.astype(bf16)

                @pl.when(i == 0)
                def _():
                    l_ref[...] = jnp.zeros_like(l_ref)

                rows = jnp.sum(err * err, axis=1, keepdims=True) * (1.0 / N)
                l_ref[...] += 0.5 * jnp.sum(rows, axis=0, keepdims=True)
            elif norm_bwd is None:
                o_ref[...] = res.astype(out_dtype)
            else:
                x_ref, g_ref, res_ref = refs[n_in - 3:n_in]
                dg_ref, twin_ref = refs[n_in + 1], refs[n_in + 2]
                _, vjp = jax.vjp(lambda xv, gv: _rms(xv, gv, denom), x_ref[...], g_ref[...])
                dx, dg = vjp(res)
                dx = dx + res_ref[...]
                o_ref[...] = dx
                twin_ref[...] = dx.astype(bf16)

                @pl.when(i == 0)
                def _():
                    dg_ref[...] = jnp.zeros_like(dg_ref)

                dg_ref[...] += dg

    acc = [pltpu.VMEM((tm, tn), f32)]
    single = norm_bwd is None and loss_target is None
    if exchange is not None:
        outs, moved = _call_with_exchange(body, args, exchange, name=name, grid=(nm, nn_, nk), in_specs=in_specs,
                                          out_specs=out_specs, out_shape=out_shapes, scratch_shapes=acc)
        return (outs[0] if single else tuple(outs)), moved
    outs = pl.pallas_call(
        body, name=name, grid=(nm, nn_, nk), in_specs=in_specs, out_specs=out_specs, out_shape=out_shapes,
        scratch_shapes=acc, compiler_params=_cp(("parallel" if single else "arbitrary", "parallel", "arbitrary")),
    )(*args)
    return outs[0] if single else tuple(outs)


def rms_norm(x, g, denom, *, name, ts=512, exchange=None):
    S, D = x.shape
    ts = _tile(S, ts)

    def body(x_ref, g_ref, h_ref):
        h_ref[...] = _rms(x_ref[...], g_ref[...], denom).astype(bf16)

    row = pl.BlockSpec((ts, D), lambda i: (i, 0))
    kw = dict(name=name, grid=(S // ts,), in_specs=[row, pl.BlockSpec((1, D), lambda i: (0, 0))])
    if exchange is not None:
        (h,), moved = _call_with_exchange(body, (x, g), exchange, out_specs=[row],
                                          out_shape=[jax.ShapeDtypeStruct((S, D), bf16)], **kw)
        return h, moved
    return pl.pallas_call(body, out_specs=row, out_shape=jax.ShapeDtypeStruct((S, D), bf16),
                          compiler_params=_cp(("parallel",)), **kw)(x, g)


def rms_bwd(x, g, dy, res, denom, *, name, ts=256, twin=False):
    S, D = x.shape
    ts = _tile(S, ts)
    has_res = res is not None

    def body(*refs):
        x_ref, g_ref, dy_ref = refs[:3]
        outs = refs[4:] if has_res else refs[3:]
        _, vjp = jax.vjp(lambda xv, gv: _rms(xv, gv, denom), x_ref[...], g_ref[...])
        dx, dg = vjp(dy_ref[...])
        if has_res:
            dx = dx + refs[3][...]
        outs[0][...] = dx
        if twin:
            outs[2][...] = dx.astype(bf16)
        dg_ref = outs[1]

        @pl.when(pl.program_id(0) == 0)
        def _():
            dg_ref[...] = jnp.zeros_like(dg_ref)

        dg_ref[...] += dg

    row = pl.BlockSpec((ts, D), lambda i: (i, 0))
    vec = pl.BlockSpec((1, D), lambda i: (0, 0))
    in_specs = [row, vec, row] + ([row] if has_res else [])
    args = [x, g, dy] + ([res] if has_res else [])
    return pl.pallas_call(
        body, name=name, grid=(S // ts,), in_specs=in_specs, out_specs=[row, vec] + ([row] if twin else []),
        out_shape=[jax.ShapeDtypeStruct((S, D), f32), jax.ShapeDtypeStruct((1, D), f32)]
        + ([jax.ShapeDtypeStruct((S, D), bf16)] if twin else []),
        compiler_params=_cp(("arbitrary",)),
    )(*args)


LOG2E = 1.4426950408889634
LN2 = 0.6931471805599453


def _qk_fn(q, k, gq, gk):
    qs, ks = [], []
    for h in range(SB_HEADS):
        sl = slice(h * LANES, (h + 1) * LANES)
        qs.append(_rms(q[:, sl], gq, SB_HEAD_DIM) * (SB_HEAD_DIM ** -0.5 * LOG2E))
        ks.append(_rms(k[:, sl], gk, SB_HEAD_DIM))
    return jnp.concatenate(qs, axis=1), jnp.concatenate(ks, axis=1)


def qkv_prep(qkv, gq, gk, *, ts=256):
    S = qkv.shape[0]
    W = SB_HEADS * LANES
    ts = _tile(S, ts)

    def body(q_ref, k_ref, v_ref, gq_ref, gk_ref, qn_ref, kn_ref, vb_ref):
        qn, kn = _qk_fn(q_ref[...], k_ref[...], gq_ref[...], gk_ref[...])
        qn_ref[...] = qn.astype(bf16)
        kn_ref[...] = kn.astype(bf16)
        vb_ref[...] = v_ref[...].astype(bf16)

    out = jax.ShapeDtypeStruct((S, W), bf16)
    gspec = pl.BlockSpec((1, LANES), lambda i: (0, 0))
    ospec = pl.BlockSpec((ts, W), lambda i: (i, 0))
    col = lambda c: pl.BlockSpec((ts, W), lambda i: (i, c))
    return pl.pallas_call(
        body, name="qkv_prep", grid=(S // ts,), in_specs=[col(0), col(1), col(2), gspec, gspec],
        out_specs=[ospec, ospec, ospec], out_shape=[out, out, out], compiler_params=_cp(("parallel",)),
    )(qkv, qkv, qkv, gq, gk)


def qkv_bwd(qkv, gq, gk, dqn, dkn, dv, exchange, *, ts=256):
    S = qkv.shape[0]
    W = SB_HEADS * LANES
    ts = _tile(S, ts)

    def body(q_ref, k_ref, gq_ref, gk_ref, dqn_ref, dkn_ref, dv_ref, o_ref, dgq_ref, dgk_ref):
        _, vjp = jax.vjp(_qk_fn, q_ref[...], k_ref[...], gq_ref[...], gk_ref[...])
        dq, dk, dgq, dgk = vjp((dqn_ref[...] * LN2, dkn_ref[...] * LN2))
        o_ref[:, 0:W] = dq.astype(bf16)
        o_ref[:, W:2 * W] = dk.astype(bf16)
        o_ref[:, 2 * W:3 * W] = dv_ref[...].astype(bf16)

        @pl.when(pl.program_id(0) == 0)
        def _():
            dgq_ref[...] = jnp.zeros_like(dgq_ref)
            dgk_ref[...] = jnp.zeros_like(dgk_ref)

        dgq_ref[...] += dgq
        dgk_ref[...] += dgk

    gspec = pl.BlockSpec((1, LANES), lambda i: (0, 0))
    row = pl.BlockSpec((ts, W), lambda i: (i, 0))
    col = lambda c: pl.BlockSpec((ts, W), lambda i: (i, c))
    (dqkv, dgq, dgk), moved = _call_with_exchange(
        body, (qkv, qkv, gq, gk, dqn, dkn, dv), exchange, name="qkv_bwd", grid=(S // ts,),
        in_specs=[col(0), col(1), gspec, gspec, row, row, row],
        out_specs=[pl.BlockSpec((ts, 3 * W), lambda i: (i, 0)), gspec, gspec],
        out_shape=[jax.ShapeDtypeStruct((S, 3 * W), bf16), jax.ShapeDtypeStruct((1, LANES), f32),
                   jax.ShapeDtypeStruct((1, LANES), f32)])
    return dqkv, dgq, dgk, moved


def _sb_weights(q, ks, R, masked, row, col, UU):
    ls = [_dot(q, k, _NT) for k in ks]
    lbs, lm0s, cats = [], [], []
    for l, diag in zip(ls, masked):
        neg_abs = pltpu.bitcast(pltpu.bitcast(l, jnp.uint32) | jnp.uint32(0x80000000), f32)
        lp = jnp.log2(1.0 + jnp.exp2(neg_abs))
        lb = jnp.minimum(l, 0.0) - lp
        lm = lb - l
        if diag:
            lm = jnp.where(col < row, lm, 0.0)
        hi = lm.astype(bf16)
        lo = (lm - hi.astype(f32)).astype(bf16)
        lbs.append(lb)
        lm0s.append(lm[:, 0:1])
        cats.append(jnp.concatenate([hi, lo], axis=1))
    sums = [_dot(c, UU, _NN) for c in cats]
    ws = []
    for lb, lm0, A, diag in zip(lbs, lm0s, sums, masked):
        w = jnp.exp2(lb + (A + R))
        if diag:
            w = jnp.where(col < row, w, 0.0)
        R = R + (A[:, 0:1] + lm0)
        ws.append(w)
    return lbs, ws, R


def _tri2(tk):
    r = lax.broadcasted_iota(jnp.int32, (2 * tk, tk), 0)
    r = jnp.where(r >= tk, r - tk, r)
    c = lax.broadcasted_iota(jnp.int32, (2 * tk, tk), 1)
    return (r > c).astype(bf16)


SB_GROUP = 8


SB_ALL_ZERO_BELOW = -160.0


def _sweep(i, blocks_of, carry, descending, right_sum=None, ran=None):
    G = SB_GROUP
    n = jnp.maximum(i - 1, 0)
    rem, full = n % G, n // G
    asc = lambda js: js if descending else js[::-1]

    def first_group(c):
        one = lambda c: blocks_of([i], c, [True])
        two = lambda c: blocks_of(asc([i, i - 1]), c, asc([True, False]))
        return lax.cond(i >= 1, two, one, c)

    def body(p, c):
        return blocks_of(asc([i - 2 - p * G - u for u in range(G)]), c, [False] * G)

    def left_over(r):
        return lambda c: blocks_of(asc([r - 1 - u for u in range(r)]), c, [False] * r) if r else c

    if descending:
        carry = first_group(carry)
        alive = lambda c: jnp.max(right_sum(c)) > SB_ALL_ZERO_BELOW
        bodies, carry = lax.while_loop(lambda s: (s[0] < full) & alive(s[1]),
                                       lambda s: (s[0] + 1, body(s[0], s[1])), (jnp.int32(0), carry))
        tail = (bodies == full) & alive(carry)
        carry = lax.switch(jnp.where(tail, rem, 0), [left_over(r) for r in range(G)], carry)
        return carry, (bodies, tail)
    bodies, tail = ran
    carry = lax.switch(jnp.where(tail, rem, 0), [left_over(r) for r in range(G)], carry)
    carry = lax.fori_loop(0, bodies, lambda t, c: body(bodies - 1 - t, c), carry)
    return first_group(carry)


def sb_fwd(qn, kn, vb, exchange, *, tq=256):
    S, W = qn.shape
    H = W // LANES
    tq = _tile(S, tq)
    tk = tq
    nq = S // tq

    def body(q_ref, k_ref, v_ref, o_ref, uu_s):
        i = pl.program_id(1)

        @pl.when((pl.program_id(0) == 0) & (i == 0))
        def _():
            uu_s[...] = _tri2(tk)

        q = q_ref[...]
        row = lax.broadcasted_iota(jnp.int32, (tq, tk), 0)
        col = lax.broadcasted_iota(jnp.int32, (tq, tk), 1)
        UU = uu_s[...]

        def blocks(js, c, masked):
            rows = [pl.ds(pl.multiple_of(j * tk, tk), tk) for j in js]
            _, ws, R = _sb_weights(q, [k_ref[r, :] for r in rows], c[0], masked, row, col, UU)
            acc = c[1]
            for w, r in zip(ws, rows):
                acc = acc + _dot(w, v_ref[r, :], _NN)
            return R, acc

        c, _ = _sweep(i, blocks, (jnp.zeros((tq, 1), f32), jnp.zeros((tq, LANES), f32)), True, lambda c: c[0])
        o_ref[...] = c[1]

    qspec = pl.BlockSpec((tq, LANES), lambda h, i: (i, h))
    kspec = pl.BlockSpec((S, LANES), lambda h, i: (0, h))
    (o,), moved = _call_with_exchange(
        body, (qn, kn, vb), exchange, name="sb_fwd", grid=(H, nq), in_specs=[qspec, kspec, kspec], out_specs=[qspec],
        out_shape=[jax.ShapeDtypeStruct((S, W), f32)], scratch_shapes=[pltpu.VMEM((2 * tk, tk), bf16)])
    return o, moved


def sb_bwd(qn, kn, vb, do, exchange, *, tq=256):
    S, W = qn.shape
    H = W // LANES
    tq = _tile(S, tq)
    tk = tq
    nq = S // tq

    def body(q_ref, k_ref, v_ref, do_ref, dq_ref, dk_ref, dv_ref, dz_s, beta_s, uu_s, ue_s):
        i = pl.program_id(1)
        row = lax.broadcasted_iota(jnp.int32, (tq, tk), 0)
        col = lax.broadcasted_iota(jnp.int32, (tq, tk), 1)

        @pl.when((pl.program_id(0) == 0) & (i == 0))
        def _():
            uu_s[...] = _tri2(tk)
            ue_s[...] = (row < col).astype(bf16)

        @pl.when(i == 0)
        def _():
            dk_ref[...] = jnp.zeros_like(dk_ref)
            dv_ref[...] = jnp.zeros_like(dv_ref)

        q = q_ref[...]
        dob = do_ref[...].astype(bf16)
        UU = uu_s[...]
        Ue = ue_s[...]

        def sweep1(js, R, masked):
            rows = [pl.ds(pl.multiple_of(j * tk, tk), tk) for j in js]
            dws = [_dot(dob, v_ref[r, :], _NT) for r in rows]
            lbs, ws, R = _sb_weights(q, [k_ref[r, :] for r in rows], R, masked, row, col, UU)
            for j, lb, w, dw in zip(js, lbs, ws, dws):
                dz_s[j] = (dw * w).astype(bf16)
                beta_s[j] = jnp.exp2(lb).astype(bf16)
            for r, w in zip(rows, ws):
                dv_ref[r, :] += _dot(w, dob, _TN)
            return R

        _, ran = _sweep(i, sweep1, jnp.zeros((tq, 1), f32), True, lambda R: R)

        def sweep2(js, c, masked):
            rows = [pl.ds(pl.multiple_of(j * tk, tk), tk) for j in js]
            dzbs = [dz_s[j] for j in js]
            sums = [_dot(dzb, Ue, _NN) for dzb in dzbs]
            Lz, dq = c
            dlbs = []
            for j, dzb, Cz, diag in zip(js, dzbs, sums, masked):
                dz = dzb.astype(f32)
                dl = dz - beta_s[j].astype(f32) * (dz + (Cz + Lz))
                if diag:
                    dl = jnp.where(col < row, dl, 0.0)
                Lz = Lz + (Cz[:, tk - 1:tk] + dz[:, tk - 1:tk])
                dlbs.append(dl.astype(bf16))
            for r, dlb in zip(rows, dlbs):
                dq = dq + _dot(dlb, k_ref[r, :], _NN)
            for r, dlb in zip(rows, dlbs):
                dk_ref[r, :] += _dot(dlb, q, _TN)
            return Lz, dq

        c = _sweep(i, sweep2, (jnp.zeros((tq, 1), f32), jnp.zeros((tq, LANES), f32)), False, ran=ran)
        dq_ref[...] = c[1]

    qspec = pl.BlockSpec((tq, LANES), lambda h, i: (i, h))
    kspec = pl.BlockSpec((S, LANES), lambda h, i: (0, h))
    full = jax.ShapeDtypeStruct((S, W), f32)
    (dq, dk, dv), moved = _call_with_exchange(
        body, (qn, kn, vb, do), exchange, name="sb_bwd", grid=(H, nq), in_specs=[qspec, kspec, kspec, qspec],
        out_specs=[qspec, kspec, kspec], out_shape=[full, full, full],
        scratch_shapes=[pltpu.VMEM((nq, tq, tk), bf16), pltpu.VMEM((nq, tq, tk), bf16),
                        pltpu.VMEM((2 * tk, tk), bf16), pltpu.VMEM((tk, tk), bf16)])
    return dq, dk, dv, moved


def _xa_fn(qx, kv, gq, gk):
    XW = XA_HEADS * XA_HEAD_DIM
    outs = []
    for h in range(XA_HEADS):
        sl = slice(h * XA_HEAD_DIM, (h + 1) * XA_HEAD_DIM)
        qn = _rms(qx[:, sl], gq, XA_HEAD_DIM)
        kn = _rms(kv[:, sl], gk, XA_HEAD_DIM)
        v = kv[:, XW + h * XA_HEAD_DIM:XW + (h + 1) * XA_HEAD_DIM]
        s = bdot_nt(qn, kn) * (XA_HEAD_DIM ** -0.5)
        e = jnp.exp(s - lax.stop_gradient(jnp.max(s, axis=-1, keepdims=True)))
        p = e / jnp.sum(e, axis=-1, keepdims=True)
        outs.append(bdot_nn(p, v))
    return jnp.concatenate(outs, axis=1)


def xa_fwd(qx, kv, gq, gk, *, ts=256):
    S, XW = qx.shape
    M = kv.shape[0]
    ts = _tile(S, ts)

    def body(q_ref, kv_ref, gq_ref, gk_ref, o_ref):
        o_ref[...] = _xa_fn(q_ref[...], kv_ref[...], gq_ref[...], gk_ref[...]).astype(bf16)

    row = pl.BlockSpec((ts, XW), lambda i: (i, 0))
    gspec = pl.BlockSpec((1, XA_HEAD_DIM), lambda i: (0, 0))
    return pl.pallas_call(
        body, name="xa_fwd", grid=(S // ts,),
        in_specs=[row, pl.BlockSpec((M, 2 * XW), lambda i: (0, 0)), gspec, gspec], out_specs=row,
        out_shape=jax.ShapeDtypeStruct((S, XW), bf16), compiler_params=_cp(("parallel",)),
    )(qx, kv, gq, gk)


def xa_bwd(qx, kv, gq, gk, do, *, ts=256):
    S, XW = qx.shape
    M = kv.shape[0]
    ts = _tile(S, ts)

    def body(q_ref, kv_ref, gq_ref, gk_ref, do_ref, dq_ref, dkv_ref, dgq_ref, dgk_ref):
        _, vjp = jax.vjp(_xa_fn, q_ref[...], kv_ref[...], gq_ref[...], gk_ref[...])
        dq, dkv, dgq, dgk = vjp(do_ref[...].astype(f32))
        dq_ref[...] = dq.astype(bf16)

        @pl.when(pl.program_id(0) == 0)
        def _():
            dkv_ref[...] = jnp.zeros_like(dkv_ref)
            dgq_ref[...] = jnp.zeros_like(dgq_ref)
            dgk_ref[...] = jnp.zeros_like(dgk_ref)

        dkv_ref[...] += dkv
        dgq_ref[...] += dgq
        dgk_ref[...] += dgk

    row = pl.BlockSpec((ts, XW), lambda i: (i, 0))
    gspec = pl.BlockSpec((1, XA_HEAD_DIM), lambda i: (0, 0))
    kvspec = pl.BlockSpec((M, 2 * XW), lambda i: (0, 0))
    gshape = jax.ShapeDtypeStruct((1, XA_HEAD_DIM), f32)
    return pl.pallas_call(
        body, name="xa_bwd", grid=(S // ts,), in_specs=[row, kvspec, gspec, gspec, row],
        out_specs=[row, kvspec, gspec, gspec],
        out_shape=[jax.ShapeDtypeStruct((S, XW), bf16), jax.ShapeDtypeStruct((M, 2 * XW), f32), gshape, gshape],
        compiler_params=_cp(("arbitrary",)),
    )(qx, kv, gq, gk, do)


def _s5_prep_fn(a_re, a_im, ldt, bT_re, bT_im, cT_re, cT_im):
    G, P, C = SSM_GROUPS, SSM_STATE, SSM_GROUP
    GP, GC = G * P, G * C
    lg_p, lg_c = P.bit_length() - 1, C.bit_length() - 1
    gi = lax.broadcasted_iota(jnp.int32, (G, GP), 0)
    ci = lax.broadcasted_iota(jnp.int32, (G, GP), 1) >> lg_p
    expand_dt = (gi == ci).astype(f32)
    dte = jnp.dot(jnp.exp(ldt), expand_dt, precision=lax.Precision.HIGHEST, preferred_element_type=f32)
    zr, zi = a_re * dte, a_im * dte
    mag = jnp.exp(zr)
    abr, abi = mag * jnp.cos(zi), mag * jnp.sin(zi)
    nr, ni = abr - 1.0, abi
    den = a_re * a_re + a_im * a_im
    cr = (nr * a_re + ni * a_im) / den
    cim = (ni * a_re - nr * a_im) / den
    bbr = cr * bT_re - cim * bT_im
    bbi = cr * bT_im + cim * bT_re
    rowg = lax.broadcasted_iota(jnp.int32, (GC, GP), 0) >> lg_c
    colg = lax.broadcasted_iota(jnp.int32, (GC, GP), 1) >> lg_p
    diag = rowg == colg

    def expand(t):
        return jnp.where(diag, jnp.broadcast_to(t[None], (G, C, GP)).reshape(GC, GP), 0.0)

    return abr, abi, expand(bbr), expand(bbi), expand(cT_re), expand(-cT_im)


def s5_prep(a_re, a_im, ldt, bT_re, bT_im, cT_re, cT_im):
    GP, GC = SSM_GROUPS * SSM_STATE, SSM_GROUPS * SSM_GROUP

    def body(a_re_ref, a_im_ref, ldt_ref, bTr_ref, bTi_ref, cTr_ref, cTi_ref, abr_ref, abi_ref, B_ref, C_ref):
        abr, abi, Br, Bi, Cr, Ci = _s5_prep_fn(a_re_ref[...], a_im_ref[...], ldt_ref[...], bTr_ref[...],
                                               bTi_ref[...], cTr_ref[...], cTi_ref[...])
        abr_ref[...] = abr
        abi_ref[...] = abi
        B_ref[0] = Br.astype(bf16)
        B_ref[1] = Bi.astype(bf16)
        C_ref[0] = Cr.astype(bf16)
        C_ref[1] = Ci.astype(bf16)

    vec = jax.ShapeDtypeStruct((1, GP), f32)
    mat = jax.ShapeDtypeStruct((2, GC, GP), bf16)
    return pl.pallas_call(body, name="s5_prep", out_shape=[vec, vec, mat, mat], compiler_params=_cp())(
        a_re, a_im, ldt, bT_re, bT_im, cT_re, cT_im)


def s5_prep_bwd(a_re, a_im, ldt, bT_re, bT_im, cT_re, cT_im, dabr, dabi, dB, dC):
    def body(a_re_ref, a_im_ref, ldt_ref, bTr_ref, bTi_ref, cTr_ref, cTi_ref, dabr_ref, dabi_ref, dB_ref, dC_ref,
             *outs):
        _, vjp = jax.vjp(_s5_prep_fn, a_re_ref[...], a_im_ref[...], ldt_ref[...], bTr_ref[...], bTi_ref[...],
                         cTr_ref[...], cTi_ref[...])
        grads = vjp((dabr_ref[...], dabi_ref[...], dB_ref[0], dB_ref[1], dC_ref[0], dC_ref[1]))
        for o_ref, gv in zip(outs, grads):
            o_ref[...] = gv

    ins = (a_re, a_im, ldt, bT_re, bT_im, cT_re, cT_im)
    return pl.pallas_call(body, name="s5_prep_bwd", out_shape=[jax.ShapeDtypeStruct(v.shape, f32) for v in ins],
                          compiler_params=_cp())(*ins, dabr, dabi, dB, dC)


def _cmul(ar, ai, br, bi):
    return ar * br - ai * bi, ar * bi + ai * br


SCAN_CHUNKS = 32


def _chunk_carry(Lr, Li, Pr, Pi, scratch, reverse):
    lr_ref, li_ref, cr_ref, ci_ref = scratch
    lr_ref[...] = Lr
    li_ref[...] = Li
    cur_r = jnp.zeros((1, LANES), f32)
    cur_i = jnp.zeros((1, LANES), f32)
    order = range(SCAN_CHUNKS - 1, -1, -1) if reverse else range(SCAN_CHUNKS)
    for c in order:
        cr_ref[pl.ds(c, 1), :] = cur_r
        ci_ref[pl.ds(c, 1), :] = cur_i
        mr, mi = _cmul(Pr, Pi, cur_r, cur_i)
        cur_r, cur_i = lr_ref[pl.ds(c, 1), :] + mr, li_ref[pl.ds(c, 1), :] + mi
    return cr_ref[...], ci_ref[...]


def _chunk_rows(j):
    return pl.ds(pl.multiple_of(j * SCAN_CHUNKS, SCAN_CHUNKS), SCAN_CHUNKS)


def row_shuffle(x, a, b, *, name, add=None, out_dtype=f32):
    S, W = x.shape
    assert a * b == S and x.dtype == f32

    def body(*refs):
        x_ref, o_ref = refs[0], refs[-1]

        def step(i, _):
            dst = pl.ds(pl.multiple_of(i * b, b), b)
            v = x_ref[pl.ds(i, b, stride=a), :]
            if add is not None:
                v = v + refs[1][dst, :]
            o_ref[dst, :] = v.astype(out_dtype)
            return 0

        lax.fori_loop(0, a, step, 0)

    col = pl.BlockSpec((S, LANES), lambda t: (0, t))
    args = [x] + ([add] if add is not None else [])
    return pl.pallas_call(
        body, name=name, grid=(W // LANES,), in_specs=[col] * len(args), out_specs=col,
        out_shape=jax.ShapeDtypeStruct((S, W), out_dtype), compiler_params=_cp(("parallel",)),
    )(*args)


def _scan_scratch(n):
    small = pltpu.VMEM((SCAN_CHUNKS, LANES), f32)
    return [pltpu.VMEM((n, LANES), f32), pltpu.VMEM((n, LANES), f32), small, small, small, small]


def _drive(x_ref, m_ref, work_ref):
    S = x_ref.shape[0]
    rows = min(S, 1024)
    m = jnp.concatenate([m_ref[0], m_ref[1]], axis=1)

    def chunk(c, _):
        r = pl.ds(pl.multiple_of(c * rows, rows), rows)
        y = _dot(x_ref[r, :], m, _NN)
        work_ref[0, r, :] = y[:, :LANES]
        work_ref[1, r, :] = y[:, LANES:]
        return 0

    lax.fori_loop(0, S // rows, chunk, 0)


def scan_fwd(u_il, Bm, abr, abi, exchange):
    S, C = u_il.shape
    N = Bm.shape[2]
    n = S // SCAN_CHUNKS
    shp = (SCAN_CHUNKS, LANES)

    def body(u_ref, B_ref, ar_ref, ai_ref, st_ref, work_ref, pwr_ref, pwi_ref, *scratch):
        a1r, a1i = ar_ref[...], ai_ref[...]
        ar = jnp.broadcast_to(a1r, shp)
        ai = jnp.broadcast_to(a1i, shp)
        _drive(u_ref, B_ref, work_ref)
        sr_ref, si_ref = work_ref.at[0], work_ref.at[1]

        def step(j, c):
            sr, si, pr, pi = c
            rows = _chunk_rows(j)
            mr, mi = _cmul(ar, ai, sr, si)
            sr, si = mr + sr_ref[rows, :], mi + si_ref[rows, :]
            sr_ref[rows, :] = sr
            si_ref[rows, :] = si
            pwr_ref[pl.ds(j, 1), :] = pr
            pwi_ref[pl.ds(j, 1), :] = pi
            npr, npi = _cmul(a1r, a1i, pr, pi)
            return sr, si, npr, npi

        z = jnp.zeros(shp, f32)
        sr, si, _, _ = lax.fori_loop(0, n, step, (z, z, a1r, a1i), unroll=2)
        cr, ci = _chunk_carry(sr, si, pwr_ref[pl.ds(n - 1, 1), :], pwi_ref[pl.ds(n - 1, 1), :], scratch, False)

        def step2(j, _):
            rows = _chunk_rows(j)
            pr = jnp.broadcast_to(pwr_ref[pl.ds(j, 1), :], shp)
            pi = jnp.broadcast_to(pwi_ref[pl.ds(j, 1), :], shp)
            mr, mi = _cmul(pr, pi, cr, ci)
            st_ref[0, rows, :] = (sr_ref[rows, :] + mr).astype(bf16)
            st_ref[1, rows, :] = (si_ref[rows, :] + mi).astype(bf16)
            return 0

        lax.fori_loop(0, n, step2, 0, unroll=4)

    blk = pl.BlockSpec((2, S, LANES), lambda t: (0, 0, t))
    vec = pl.BlockSpec((1, LANES), lambda t: (0, t))
    (st,), moved = _call_with_exchange(
        body, (u_il, Bm, abr, abi), exchange, name="scan_fwd", grid=(N // LANES,),
        in_specs=[pl.BlockSpec((S, C), lambda t: (0, 0)), pl.BlockSpec((2, C, LANES), lambda t: (0, 0, t)), vec, vec],
        out_specs=[blk], out_shape=[jax.ShapeDtypeStruct((2, S, N), bf16)],
        scratch_shapes=[pltpu.VMEM((2, S, LANES), f32)] + _scan_scratch(n))
    return st, moved


def scan_bwd(dy_il, Cm, st, abr, abi, exchange):
    _, S, N = st.shape
    C = dy_il.shape[1]
    n = S // SCAN_CHUNKS
    shp = (SCAN_CHUNKS, LANES)

    def body(dy_ref, C_ref, st_ref, ar_ref, ai_ref, g_ref, dar_ref, dai_ref, work_ref, qwr_ref, qwi_ref, *scratch):
        a1r, a1i = ar_ref[...], -ai_ref[...]
        ar = jnp.broadcast_to(a1r, shp)
        nai = jnp.broadcast_to(a1i, shp)
        _drive(dy_ref, C_ref, work_ref)
        gr_ref, gi_ref = work_ref.at[0], work_ref.at[1]
        sr_ref, si_ref = st_ref.at[0], st_ref.at[1]

        def step(jj, c):
            gr, gi, qr, qi = c
            j = n - 1 - jj
            rows = _chunk_rows(j)
            mr, mi = _cmul(ar, nai, gr, gi)
            gr, gi = mr + gr_ref[rows, :], mi + gi_ref[rows, :]
            gr_ref[rows, :] = gr
            gi_ref[rows, :] = gi
            qwr_ref[pl.ds(j, 1), :] = qr
            qwi_ref[pl.ds(j, 1), :] = qi
            nqr, nqi = _cmul(a1r, a1i, qr, qi)
            return gr, gi, nqr, nqi

        z = jnp.zeros(shp, f32)
        gr, gi, _, _ = lax.fori_loop(0, n, step, (z, z, a1r, a1i), unroll=2)
        cr, ci = _chunk_carry(gr, gi, qwr_ref[pl.ds(0, 1), :], qwi_ref[pl.ds(0, 1), :], scratch, True)
        sub = lax.broadcasted_iota(jnp.int32, shp, 0)

        def fix(j, spr, spi, acc):
            rows = _chunk_rows(j)
            qr = jnp.broadcast_to(qwr_ref[pl.ds(j, 1), :], shp)
            qi = jnp.broadcast_to(qwi_ref[pl.ds(j, 1), :], shp)
            mr, mi = _cmul(qr, qi, cr, ci)
            gr = gr_ref[rows, :] + mr
            gi = gi_ref[rows, :] + mi
            g_ref[0, rows, :] = gr.astype(bf16)
            g_ref[1, rows, :] = gi.astype(bf16)
            return acc[0] + gr * spr + gi * spi, acc[1] + gi * spr - gr * spi

        last = _chunk_rows(n - 1)
        spr = jnp.where(sub == 0, 0.0, pltpu.roll(sr_ref[last, :].astype(f32), 1, 0))
        spi = jnp.where(sub == 0, 0.0, pltpu.roll(si_ref[last, :].astype(f32), 1, 0))
        acc = fix(0, spr, spi, (z, z))

        def step2(j, acc):
            prev = _chunk_rows(j - 1)
            return fix(j, sr_ref[prev, :].astype(f32), si_ref[prev, :].astype(f32), acc)

        acc = lax.fori_loop(1, n, step2, acc)
        dar_ref[...] = jnp.sum(acc[0], axis=0, keepdims=True)
        dai_ref[...] = jnp.sum(acc[1], axis=0, keepdims=True)

    blk = pl.BlockSpec((2, S, LANES), lambda t: (0, 0, t))
    vec = pl.BlockSpec((1, LANES), lambda t: (0, t))
    vshape = jax.ShapeDtypeStruct((1, N), f32)
    (g, dar, dai), moved = _call_with_exchange(
        body, (dy_il, Cm, st, abr, abi), exchange, name="scan_bwd", grid=(N // LANES,),
        in_specs=[pl.BlockSpec((S, C), lambda t: (0, 0)), pl.BlockSpec((2, C, LANES), lambda t: (0, 0, t)), blk,
                  vec, vec],
        out_specs=[blk, vec, vec], out_shape=[jax.ShapeDtypeStruct((2, S, N), bf16), vshape, vshape],
        scratch_shapes=[pltpu.VMEM((2, S, LANES), f32)] + _scan_scratch(n))
    return g, dar, dai, moved


def _glu_fn(ypre, wglu):
    y = jax.nn.gelu(ypre)
    return y * jax.nn.sigmoid(bdot_nn(y, wglu))


def glu_fwd(ypre0, u, d, wglu, g_out, *, ts=512):
    S, W = u.shape
    ts = _tile(S, ts)

    def body(y0_ref, u_ref, d_ref, w_ref, g_ref, ypre_ref, z_ref, zn_ref):
        ypre = y0_ref[...] + d_ref[...] * u_ref[...]
        z = _glu_fn(ypre, w_ref[...])
        ypre_ref[...] = ypre
        z_ref[...] = z
        zn_ref[...] = _rms(z, g_ref[...], W).astype(bf16)

    row = pl.BlockSpec((ts, W), lambda i: (i, 0))
    vec = pl.BlockSpec((1, W), lambda i: (0, 0))
    full = jax.ShapeDtypeStruct((S, W), f32)
    return pl.pallas_call(
        body, name="glu_fwd", grid=(S // ts,),
        in_specs=[row, row, vec, pl.BlockSpec((W, W), lambda i: (0, 0)), vec], out_specs=[row, row, row],
        out_shape=[full, full, jax.ShapeDtypeStruct((S, W), bf16)], compiler_params=_cp(("parallel",)),
    )(ypre0, u, d, wglu, g_out)


def glu_bwd(ypre, u, d, wglu, dz, *, ts=512):
    S, W = u.shape
    ts = _tile(S, ts)

    def body(y_ref, u_ref, d_ref, w_ref, dz_ref, dy_ref, du_ref, dw_ref, dd_ref):
        _, vjp = jax.vjp(_glu_fn, y_ref[...], w_ref[...])
        dy, dw = vjp(dz_ref[...])
        dy_ref[...] = dy
        du_ref[...] = d_ref[...] * dy

        @pl.when(pl.program_id(0) == 0)
        def _():
            dw_ref[...] = jnp.zeros_like(dw_ref)
            dd_ref[...] = jnp.zeros_like(dd_ref)

        dw_ref[...] += dw
        dd_ref[...] += jnp.sum(dy * u_ref[...], axis=0, keepdims=True)

    row = pl.BlockSpec((ts, W), lambda i: (i, 0))
    vec = pl.BlockSpec((1, W), lambda i: (0, 0))
    sq = pl.BlockSpec((W, W), lambda i: (0, 0))
    full = jax.ShapeDtypeStruct((S, W), f32)
    return pl.pallas_call(
        body, name="glu_bwd", grid=(S // ts,), in_specs=[row, row, vec, sq, row], out_specs=[row, row, sq, vec],
        out_shape=[full, full, jax.ShapeDtypeStruct((W, W), f32), jax.ShapeDtypeStruct((1, W), f32)],
        compiler_params=_cp(("arbitrary",)),
    )(ypre, u, d, wglu, dz)


def adamw(w, g, m, v, *, name, tr=256):
    R, C = w.shape
    tr = _row_tile(R, tr)

    def body(w_ref, g_ref, m_ref, v_ref, d_ref, nm_ref, nv_ref):
        gv = g_ref[...]
        nm = ADAM_B1 * m_ref[...] + (1.0 - ADAM_B1) * gv
        nv = ADAM_B2 * v_ref[...] + (1.0 - ADAM_B2) * jnp.square(gv)
        m_hat = nm / (1.0 - ADAM_B1 ** ADAM_STEP)
        v_hat = nv / (1.0 - ADAM_B2 ** ADAM_STEP)
        d_ref[...] = -ADAM_LR * (m_hat / (jnp.sqrt(v_hat) + ADAM_EPS) + ADAM_WD * w_ref[...])
        nm_ref[...] = nm
        nv_ref[...] = nv

    row = pl.BlockSpec((tr, C), lambda i: (i, 0))
    full = jax.ShapeDtypeStruct((R, C), f32)
    return pl.pallas_call(
        body, name=name, grid=(R // tr,), in_specs=[row] * 4, out_specs=[row] * 3, out_shape=[full] * 3,
        compiler_params=_cp(("parallel",)),
    )(w, g, m, v)


def add_half(g4, recv, c, *, name, tr=256):
    _, _, Rh, C = g4.shape
    tr = _row_tile(Rh, tr)

    def body(c_ref, a_ref, b_ref, o_ref):
        o_ref[...] = a_ref[...] + b_ref[...]

    grid_spec = pltpu.PrefetchScalarGridSpec(
        num_scalar_prefetch=1, grid=(N_CHIPS, Rh // tr),
        in_specs=[pl.BlockSpec((None, None, tr, C), lambda k, i, c_ref: (k, c_ref[0], i, 0)),
                  pl.BlockSpec((None, tr, C), lambda k, i, c_ref: (k, i, 0))],
        out_specs=pl.BlockSpec((None, tr, C), lambda k, i, c_ref: (k, i, 0)))
    return pl.pallas_call(body, name=name, grid_spec=grid_spec, out_shape=jax.ShapeDtypeStruct(recv.shape, f32),
                          compiler_params=_cp(("parallel", "parallel")))(c, g4, recv)


def sum_chips(p4, *, name, tr=256):
    _, Rh, C = p4.shape
    tr = _row_tile(Rh, tr)

    def body(a_ref, b_ref, c_ref, d_ref, o_ref):
        o_ref[...] = ((a_ref[...] + b_ref[...]) + c_ref[...]) + d_ref[...]

    spec = lambda k: pl.BlockSpec((None, tr, C), lambda i: (k, i, 0))
    return pl.pallas_call(
        body, name=name, grid=(Rh // tr,), in_specs=[spec(0), spec(1), spec(2), spec(3)],
        out_specs=pl.BlockSpec((tr, C), lambda i: (i, 0)), out_shape=jax.ShapeDtypeStruct((Rh, C), f32),
        compiler_params=_cp(("parallel",)),
    )(p4, p4, p4, p4)


def _place():
    return lax.axis_index("x"), lax.axis_index("y"), lax.axis_index("c")


def _other_chips(x, y):
    return [(1 - x, y), (x, 1 - y), (1 - x, 1 - y)]


def _chip_exchange(ins, outs, sems, scatter, start):
    if not ins:
        return
    send, recv, loc = sems
    x, y, c = _place()
    me = 2 * x + y
    for a in range(len(ins)):
        own = pltpu.make_async_copy(ins[a].at[me] if scatter else ins[a], outs[a].at[me], loc.at[a])
        own.start() if start else own.wait()
        for p, (px, py) in enumerate(_other_chips(x, y)):
            k = 2 * px + py
            cp = pltpu.make_async_remote_copy(
                src_ref=ins[a].at[k] if scatter else ins[a], dst_ref=outs[a].at[me if start else k],
                send_sem=send.at[3 * a + p], recv_sem=recv.at[3 * a + p], device_id=(px, py, c), device_id_type=MESH)
            cp.start() if start else cp.wait()


def _chip_exchange_args(arrs, scatter):
    n = len(arrs)
    shapes = [jax.ShapeDtypeStruct(a.shape if scatter else (N_CHIPS,) + a.shape, a.dtype) for a in arrs]
    sems = [pltpu.SemaphoreType.DMA((3 * n,)), pltpu.SemaphoreType.DMA((3 * n,)), pltpu.SemaphoreType.DMA((n,))]
    return shapes, sems if n else []


def _chip_exchange_call(arrs, scatter, name):
    n = len(arrs)

    def body(*refs):
        ins, outs, sems = refs[:n], refs[n:2 * n], refs[2 * n:]
        _chip_exchange(ins, outs, sems, scatter, True)
        _chip_exchange(ins, outs, sems, scatter, False)

    shapes, sems = _chip_exchange_args(arrs, scatter)
    return pl.pallas_call(
        body, name=name, in_specs=[ANY] * n, out_specs=[ANY] * n, out_shape=shapes, scratch_shapes=sems,
        compiler_params=pltpu.CompilerParams(has_side_effects=True),
    )(*arrs)


def allgather_chips(arrs, *, name):
    return _chip_exchange_call(arrs, False, name)


def sibling_swap(arrs, *, half, name):
    n = len(arrs)

    def body(*refs):
        ins, outs = refs[:n], refs[n:2 * n]
        send, recv = refs[2 * n:]
        x, y, c = _place()
        cps = []
        for a in range(n):
            src = ins[a].at[:, 1 - c] if half else ins[a]
            cp = pltpu.make_async_remote_copy(src_ref=src, dst_ref=outs[a], send_sem=send.at[a], recv_sem=recv.at[a],
                                              device_id=(x, y, 1 - c), device_id_type=MESH)
            cp.start()
            cps.append(cp)
        for cp in cps:
            cp.wait()

    def oshape(a):
        return jax.ShapeDtypeStruct((a.shape[0],) + a.shape[2:] if half else a.shape, a.dtype)

    return pl.pallas_call(
        body, name=name, in_specs=[ANY] * n, out_specs=[ANY] * n, out_shape=[oshape(a) for a in arrs],
        scratch_shapes=[pltpu.SemaphoreType.DMA((n,)), pltpu.SemaphoreType.DMA((n,))],
        compiler_params=pltpu.CompilerParams(has_side_effects=True),
    )(*arrs)


def chip_scatter(arrs, *, name):
    return _chip_exchange_call(arrs, True, name)


def _pad_cols(w):
    K = w.shape[0]
    w = w.reshape(K, -1, SB_HEAD_DIM)
    return jnp.pad(w, ((0, 0), (0, 0), (0, LANES - SB_HEAD_DIM))).reshape(K, -1)


def _unpad_cols(w):
    K = w.shape[0]
    return w.reshape(K, -1, LANES)[:, :, :SB_HEAD_DIM].reshape(K, -1)


def _pad_rows(w):
    N = w.shape[1]
    w = w.reshape(-1, SB_HEAD_DIM, N)
    return jnp.pad(w, ((0, 0), (0, LANES - SB_HEAD_DIM), (0, 0))).reshape(-1, N)


def _unpad_rows(w):
    N = w.shape[1]
    return w.reshape(-1, LANES, N)[:, :SB_HEAD_DIM, :].reshape(-1, N)


_PACK_ROWS = N_CHIPS * 2 * SUBLANES


def _pack(arrs):
    flat = jnp.concatenate([a.reshape(-1) for a in arrs])
    rows = -(-flat.shape[0] // LANES)
    rows = -(-rows // _PACK_ROWS) * _PACK_ROWS
    return jnp.pad(flat, (0, rows * LANES - flat.shape[0])).reshape(rows, LANES)


def _unpack(buf, shapes):
    flat = buf.reshape(-1)
    out, pos = [], 0
    for shp in shapes:
        size = 1
        for d in shp:
            size *= d
        out.append(flat[pos:pos + size].reshape(shp))
        pos += size
    return out


BIG = ("w_in", "ssm_w_glu", "w_out", "xa_w_q", "xa_w_kv", "xa_w_o", "w_up", "w_down")
SMALL = ("g_mix", "ssm_a_re", "ssm_a_im", "ssm_log_dt", "ssm_b_re", "ssm_b_im", "ssm_c_re", "ssm_c_im", "ssm_d",
         "sb_g_q", "sb_g_k", "g_out_ssm", "g_out_sb", "g_xa", "g_mem", "xa_g_q", "xa_g_k", "g_mlp")
WEIGHTS = ("g_mix", "w_in", "ssm_a_re", "ssm_a_im", "ssm_log_dt", "ssm_b_re", "ssm_b_im", "ssm_c_re", "ssm_c_im",
           "ssm_d", "ssm_w_glu", "sb_g_q", "sb_g_k", "g_out_ssm", "g_out_sb", "w_out", "g_xa", "g_mem", "xa_w_q",
           "xa_w_kv", "xa_g_q", "xa_g_k", "xa_w_o", "g_mlp", "w_up", "w_down")


def kernel(x, mem, g_mix, w_in, ssm_a_re, ssm_a_im, ssm_log_dt, ssm_b_re, ssm_b_im, ssm_c_re, ssm_c_im, ssm_d, ssm_w_glu, sb_g_q, sb_g_k, g_out_ssm, g_out_sb, w_out, g_xa, g_mem, xa_w_q, xa_w_kv, xa_g_q, xa_g_k, xa_w_o, g_mlp, w_up, w_down, loss_target, m_g_mix, m_w_in, m_ssm_a_re, m_ssm_a_im, m_ssm_log_dt, m_ssm_b_re, m_ssm_b_im, m_ssm_c_re, m_ssm_c_im, m_ssm_d, m_ssm_w_glu, m_sb_g_q, m_sb_g_k, m_g_out_ssm, m_g_out_sb, m_w_out, m_g_xa, m_g_mem, m_xa_w_q, m_xa_w_kv, m_xa_g_q, m_xa_g_k, m_xa_w_o, m_g_mlp, m_w_up, m_w_down, v_g_mix, v_w_in, v_ssm_a_re, v_ssm_a_im, v_ssm_log_dt, v_ssm_b_re, v_ssm_b_im, v_ssm_c_re, v_ssm_c_im, v_ssm_d, v_ssm_w_glu, v_sb_g_q, v_sb_g_k, v_g_out_ssm, v_g_out_sb, v_w_out, v_g_xa, v_g_mem, v_xa_w_q, v_xa_w_kv, v_xa_g_q, v_xa_g_k, v_xa_w_o, v_g_mlp, v_w_up, v_w_down):
    env = dict(locals())
    W = {n: env[n] for n in WEIGHTS}
    M1 = {n: env["m_" + n] for n in WEIGHTS}
    V2 = {n: env["v_" + n] for n in WEIGHTS}
    xs, mems, tgt = x[0], mem[0], loss_target[0]
    S, D = xs.shape
    G, P, C = SSM_GROUPS, SSM_STATE, SSM_GROUP
    GP = G * P
    SBW = SB_HEADS * SB_HEAD_DIM
    c_idx = lax.axis_index("c")

    big = dict(tn=1024, tk=1024)
    wide = dict(tm=1024, tn=1024, tk=2048)
    h0, (g_in,) = rms_norm(xs, g_mix, D, name="norm_x", exchange=([w_in[0].astype(bf16)], False))
    Wu = g_in[0]
    Wqkv = jnp.concatenate([_pad_cols(g_in[1]), _pad_cols(g_in[2]), _pad_cols(g_in[3])], axis=1)
    gq_pad, gk_pad = _pad_cols(sb_g_q), _pad_cols(sb_g_k)
    gosb_pad = _pad_cols(g_out_sb)
    a_re, a_im = ssm_a_re.reshape(1, GP), ssm_a_im.reshape(1, GP)
    bT_re = ssm_b_re[0].transpose(2, 0, 1).reshape(C, GP)
    bT_im = ssm_b_im[0].transpose(2, 0, 1).reshape(C, GP)
    cT_re = ssm_c_re[0].transpose(1, 0, 2).reshape(C, GP)
    cT_im = ssm_c_im[0].transpose(1, 0, 2).reshape(C, GP)
    s5_in = (a_re, a_im, ssm_log_dt, bT_re, bT_im, cT_re, cT_im)

    u = mm(h0, Wu, mode="nn", name="proj_u", tk=1024)
    shard = {n: W[n][0].astype(bf16) for n in BIG[1:]}
    qkv, (g_glu, g_out, g_xq, g_xkv, g_xo) = mm(
        h0, Wqkv, mode="nn", name="proj_qkv", tm=1024,
        exchange=([shard[n] for n in ("ssm_w_glu", "w_out", "xa_w_q", "xa_w_kv", "xa_w_o")], False), **big)
    qn, kn, vb = qkv_prep(qkv, gq_pad, gk_pad)
    o, (g_down,) = sb_fwd(qn, kn, vb, ([shard["w_down"]], False))
    Wglu = g_glu.reshape(-1, g_glu.shape[-1])
    Wout = g_out.reshape(-1, g_out.shape[-1])
    Wo_ssm, Wo_sb = Wout[:SBW], _pad_rows(Wout[SBW:])
    Wxq = g_xq.reshape(-1, g_xq.shape[-1])
    Wxkv = g_xkv.reshape(-1, g_xkv.shape[-1])
    Wxo = g_xo.transpose(1, 0, 2).reshape(g_xo.shape[1], -1)
    abr, abi, Bm, Cm = s5_prep(*s5_in)
    n_pos = S // SCAN_CHUNKS
    u_il = row_shuffle(u, n_pos, SCAN_CHUNKS, name="u_interleave", out_dtype=bf16)
    st, (g_up,) = scan_fwd(u_il, Bm, abr, abi, ([shard["w_up"]], False))
    ypre0_il = mm(st, Cm, mode="nt", name="s5_y", a_shards=2, b_shards=2, tm=1024, tk=2048)
    Wup = g_up.transpose(1, 0, 2).reshape(g_up.shape[1], -1)
    Wdown = g_down.reshape(-1, g_down.shape[-1])
    ypre0 = row_shuffle(ypre0_il, SCAN_CHUNKS, n_pos, name="y_token_order")
    ypre, z, zn = glu_fwd(ypre0, u, ssm_d, Wglu, g_out_ssm)
    on = rms_norm(o, gosb_pad, SBW, name="norm_o")
    x1a = mm(zn, Wo_ssm, mode="nn", name="out_ssm", epi="add", aux=xs, tn=1024)
    x1 = mm(on, Wo_sb, mode="nn", name="out_sb", epi="add", aux=x1a, **big)
    h1 = rms_norm(x1, g_xa, D, name="norm_x1")
    qx = mm(h1, Wxq, mode="nn", name="xa_q", tk=1024)
    memn = rms_norm(mems, g_mem, D, name="norm_mem")
    kv = mm(memn, Wxkv, mode="nn", name="xa_kv", **big)
    ox = xa_fwd(qx, kv, xa_g_q, xa_g_k)
    x2 = mm(ox, Wxo, mode="nn", name="xa_o", epi="add", aux=x1, tn=1024)
    h2 = rms_norm(x2, g_mlp, D, name="norm_x2")
    act = mm(h2, Wup, mode="nn", name="mlp_up", out_dtype=bf16, tm=1024, tn=2048, tk=1024)
    dx3, loss_part, dx3b = mm(act, Wdown, mode="nn", name="mlp_down", pro="relu2", epi="add", aux=x2,
                              loss_target=tgt, tn=1024, tk=2048)
    loss = lax.psum(loss_part[0, 0], ("x", "y", "c"))

    dact = mm(dx3b, Wdown, mode="nt", name="d_act", epi="mul2relu", aux=act, out_dtype=bf16, tm=1024, tn=2048,
              tk=1024)
    dWdown = mm(act, dx3b, mode="tn", name="dw_down", pro="relu2", **wide)
    dWup = mm(h2, dact, mode="tn", name="dw_up", out_shards=N_CHIPS, **wide)
    dx2, dg_mlp, dx2b = mm(dact, Wup, mode="nt", name="d_h2", norm_bwd=(x2, g_mlp, dx3, D), tn=1024, tk=2048)
    dox = mm(dx2b, Wxo, mode="nt", name="d_ox", out_dtype=bf16, tk=1024)
    dWxo = mm(ox, dx2b, mode="tn", name="dw_xo", out_shards=N_CHIPS, tk=1024)
    dqx, dkv, dg_xq, dg_xk = xa_bwd(qx, kv, xa_g_q, xa_g_k, dox)
    dWxq = mm(h1, dqx, mode="tn", name="dw_xq", tm=1024, tk=1024)
    dx1, dg_xa, dx1b = mm(dqx, Wxq, mode="nt", name="d_h1", norm_bwd=(x1, g_xa, dx2, D), tn=1024)
    dWxkv = mm(memn, dkv, mode="tn", name="dw_xkv", tm=1024, tn=1024)
    dmemn = mm(dkv, Wxkv, mode="nt", name="d_memn", **big)
    _, dg_mem = rms_bwd(mems, g_mem, dmemn, None, D, name="rms_bwd_mem")
    dyn_ssm = mm(dx1b, Wo_ssm, mode="nt", name="d_yn_ssm", tk=1024)
    dyn_sb = mm(dx1b, Wo_sb, mode="nt", name="d_yn_sb", **big)
    dWo_ssm = mm(zn, dx1b, mode="tn", name="dw_out_ssm", **big)
    dWo_sb = mm(on, dx1b, mode="tn", name="dw_out_sb", tm=1024, **big)
    dz, dg_os = rms_bwd(z, g_out_ssm, dyn_ssm, None, SBW, name="rms_bwd_ssm")
    do, dg_osb = rms_bwd(o, gosb_pad, dyn_sb, None, SBW, name="rms_bwd_sb")
    c_arr = c_idx.astype(jnp.int32).reshape(1)

    def sibling_sums(grads, names, tag):
        g4 = [g.reshape(N_CHIPS, 2, g.shape[1] // 2, g.shape[2]) for g in grads]
        from_sib = sibling_swap(g4, half=True, name="grad_to_sibling_" + tag)
        return [add_half(a, b, c_arr, name="add_sibling_" + n) for a, b, n in zip(g4, from_sib, names)]

    early = ("xa_w_q", "xa_w_kv", "xa_w_o", "w_up", "w_down")
    early_g = [dWxq.reshape(N_CHIPS, -1, dWxq.shape[1]), dWxkv.reshape(N_CHIPS, -1, dWxkv.shape[1]), dWxo, dWup,
               dWdown.reshape(N_CHIPS, -1, D)]
    pair = sibling_sums(early_g, early, "early")
    dqn, dkn, dv, parts_mlp = sb_bwd(qn, kn, vb, do, (pair[3:], True))
    dqkv, dg_q, dg_k, parts_xa = qkv_bwd(qkv, gq_pad, gk_pad, dqn, dkn, dv, (pair[:3], True))
    dypre, du_skip, dWglu, dd = glu_bwd(ypre, u, ssm_d, Wglu, dz)
    mid = ("ssm_w_glu", "w_out")
    mid_g = [dWglu.reshape(N_CHIPS, -1, dWglu.shape[1]),
             jnp.concatenate([dWo_ssm, _unpad_rows(dWo_sb)]).reshape(N_CHIPS, -1, D)]
    dypre_il = row_shuffle(dypre, n_pos, SCAN_CHUNKS, name="dy_interleave", out_dtype=bf16)
    dCm = mm(dypre_il, st, mode="tn", name="d_cmat", b_shards=2, out_shards=2, **wide)
    gst, dabr, dabi, parts_mid = scan_bwd(dypre_il, Cm, st, abr, abi, (sibling_sums(mid_g, mid, "mid"), True))
    dBm = mm(u_il, gst, mode="tn", name="d_bmat", b_shards=2, out_shards=2, **wide)
    du_il = mm(gst, Bm, mode="nt", name="d_u", a_shards=2, b_shards=2, tm=1024, tk=2048)
    mine = {n: sum_chips(p, name="sum_chips_" + n)
            for n, p in zip(early + mid, [*parts_xa, *parts_mlp, *parts_mid])}
    du = row_shuffle(du_il, SCAN_CHUNKS, n_pos, name="du_token_order", add=du_skip, out_dtype=bf16)
    s5_g = s5_prep_bwd(*s5_in, dabr, dabi, dBm, dCm)
    dWu = mm(h0, du, mode="tn", name="dw_u", tm=1024, tk=1024)
    dWqkv = mm(h0, dqkv, mode="tn", name="dw_qkv", **wide)
    HW = SB_HEADS * LANES
    w_in_g = jnp.stack([dWu, _unpad_cols(dWqkv[:, :HW]), _unpad_cols(dWqkv[:, HW:2 * HW]),
                        _unpad_cols(dWqkv[:, 2 * HW:])])
    dh0a = mm(du, Wu, mode="nt", name="d_h0_u", tn=1024)
    (dx, dg_mix, _), parts_in = mm(dqkv, Wqkv, mode="nt", name="d_h0_qkv", epi="add", aux=dh0a,
                                   norm_bwd=(xs, g_mix, dx1, D),
                                   exchange=(sibling_sums([w_in_g], ("w_in",), "w_in"), True), **big)
    mine["w_in"] = sum_chips(parts_in[0], name="sum_chips_w_in")

    late = ("small",)
    late_g = []
    da_re, da_im, dldt, dbT_re, dbT_im, dcT_re, dcT_im = s5_g
    small_g = {
        "g_mix": dg_mix, "ssm_a_re": da_re, "ssm_a_im": da_im, "ssm_log_dt": dldt,
        "ssm_b_re": dbT_re.reshape(C, G, P).transpose(1, 2, 0), "ssm_b_im": dbT_im.reshape(C, G, P).transpose(1, 2, 0),
        "ssm_c_re": dcT_re.reshape(C, G, P).transpose(1, 0, 2), "ssm_c_im": dcT_im.reshape(C, G, P).transpose(1, 0, 2),
        "ssm_d": dd, "sb_g_q": dg_q[:, :SB_HEAD_DIM], "sb_g_k": dg_k[:, :SB_HEAD_DIM], "g_out_ssm": dg_os,
        "g_out_sb": _unpad_cols(dg_osb), "g_xa": dg_xa, "g_mem": dg_mem, "xa_g_q": dg_xq, "xa_g_k": dg_xk,
        "g_mlp": dg_mlp,
    }
    late_g.append(_pack([small_g[n] for n in SMALL]).reshape(N_CHIPS, -1, LANES))

    parts_late = chip_scatter(sibling_sums(late_g, late, "late"), name="grad_to_chips_late")
    mine.update({n: sum_chips(p, name="sum_chips_" + n) for n, p in zip(late, parts_late)})
    mine = [mine[n] for n in list(BIG) + ["small"]]
    other = sibling_swap(mine, half=False, name="grad_half_to_sibling")
    shard = [jnp.where(c_idx == 0, jnp.concatenate([a, b]), jnp.concatenate([b, a])) for a, b in zip(mine, other)]
    small_all = allgather_chips([shard[-1]], name="gather_small")[0]
    small_red = small_all.reshape(-1, LANES)

    out = {}
    for n, gs in zip(BIG, shard[:-1]):
        shp = W[n].shape
        w2, m2, v2 = (t.reshape(gs.shape) for t in (W[n], M1[n], V2[n]))
        d, nm, nv = adamw(w2, gs, m2, v2, name="adamw_" + n)
        out[n] = tuple(t.reshape(shp) for t in (gs, d, nm, nv))
    shapes = [W[n].shape for n in SMALL]
    d, nm, nv = adamw(_pack([W[n] for n in SMALL]), small_red, _pack([M1[n] for n in SMALL]),
                      _pack([V2[n] for n in SMALL]), name="adamw_small")
    for n, gs, dd_, mm_, vv_ in zip(SMALL, _unpack(small_red, shapes), _unpack(d, shapes), _unpack(nm, shapes),
                                    _unpack(nv, shapes)):
        out[n] = (gs, dd_, mm_, vv_)
    res = [loss, dx[None]]
    for kind in range(4):
        res += [out[n][kind] for n in WEIGHTS]
    return tuple(res)
```

```python
import jax
import jax.numpy as jnp
from jax import lax
from jax.experimental import pallas as pl
from jax.experimental.pallas import tpu as pltpu

f32 = jnp.float32
bf16 = jnp.bfloat16

NORM_EPS = 1e-6
SSM_GROUPS = 32
SSM_GROUP = 16
SSM_STATE = 64
SB_HEADS = 8
SB_HEAD_DIM = 64
XA_HEADS = 4
XA_HEAD_DIM = 128
LANES = 128
SUBLANES = 8
N_CHIPS = 4
ADAM_LR = 0.001
ADAM_B1 = 0.9
ADAM_B2 = 0.999
ADAM_EPS = 1e-08
ADAM_WD = 0.01
ADAM_STEP = 10
VMEM_LIMIT = 56 * 1024 * 1024
MESH = pl.DeviceIdType.MESH
ANY = pl.BlockSpec(memory_space=pl.ANY)


def _cp(sem=None):
    return pltpu.CompilerParams(dimension_semantics=sem, vmem_limit_bytes=VMEM_LIMIT)


def _tile(n, pref):
    if n <= pref:
        return n
    t = (pref // LANES) * LANES
    while t > LANES and n % t:
        t -= LANES
    assert n % t == 0, (n, pref)
    return t


def _row_tile(n, pref):
    if n <= pref:
        return n
    t = (pref // SUBLANES) * SUBLANES
    while n % t:
        t -= SUBLANES
    return t


def _dot(a, b, dims):
    return lax.dot_general(a.astype(bf16), b.astype(bf16), (dims, ((), ())), preferred_element_type=f32)


_NN = ((1,), (0,))
_NT = ((1,), (1,))
_TN = ((0,), (0,))


@jax.custom_vjp
def bdot_nn(a, b):
    return _dot(a, b, _NN)


def _bdot_nn_fwd(a, b):
    return _dot(a, b, _NN), (a, b)


def _bdot_nn_bwd(res, g):
    a, b = res
    return _dot(g, b, _NT), _dot(a, g, _TN)


bdot_nn.defvjp(_bdot_nn_fwd, _bdot_nn_bwd)


@jax.custom_vjp
def bdot_nt(a, b):
    return _dot(a, b, _NT)


def _bdot_nt_fwd(a, b):
    return _dot(a, b, _NT), (a, b)


def _bdot_nt_bwd(res, g):
    a, b = res
    return _dot(g, b, _NN), _dot(g, a, _TN)


bdot_nt.defvjp(_bdot_nt_fwd, _bdot_nt_bwd)


def _rms(x, g, denom):
    r = lax.rsqrt(jnp.sum(x * x, axis=-1, keepdims=True) * (1.0 / denom) + NORM_EPS)
    return x * r * g


def _opspec(block, row_of, col_of, shards, ncol_tiles):
    if shards == 1:
        return pl.BlockSpec(block, lambda i, j, k: (row_of(i, j, k), col_of(i, j, k)))
    per = ncol_tiles // shards
    return pl.BlockSpec((None,) + block,
                        lambda i, j, k: (col_of(i, j, k) // per, row_of(i, j, k), col_of(i, j, k) % per))


def _call_with_exchange(body, args, exchange, *, name, grid, in_specs, out_specs, out_shape, scratch_shapes=()):
    xs, scatter = exchange
    n, n_in, n_out, n_scr = len(xs), len(in_specs), len(out_specs), len(scratch_shapes)
    x_shapes, sems = _chip_exchange_args(xs, scatter)

    def wrapped(*refs):
        ins, x_ins = refs[:n_in], refs[n_in:n_in + n]
        outs, x_outs = refs[n_in + n:n_in + n + n_out], refs[n_in + n + n_out:n_in + 2 * n + n_out]
        scratch, x_sems = refs[n_in + 2 * n + n_out:n_in + 2 * n + n_out + n_scr], refs[n_in + 2 * n + n_out + n_scr:]
        first, last = True, True
        for d, steps in enumerate(grid):
            first = first & (pl.program_id(d) == 0)
            last = last & (pl.program_id(d) == steps - 1)

        @pl.when(first)
        def _():
            _chip_exchange(x_ins, x_outs, x_sems, scatter, True)

        body(*ins, *outs, *scratch)

        @pl.when(last)
        def _():
            _chip_exchange(x_ins, x_outs, x_sems, scatter, False)

    res = pl.pallas_call(
        wrapped, name=name, grid=grid, in_specs=list(in_specs) + [ANY] * n, out_specs=list(out_specs) + [ANY] * n,
        out_shape=list(out_shape) + x_shapes, scratch_shapes=list(scratch_shapes) + sems,
        compiler_params=_cp(("arbitrary",) * len(grid)),
    )(*args, *xs)
    return res[:n_out], res[n_out:]


def mm(a, b, *, mode, name, tm=512, tn=512, tk=512, pro="none", epi="none", aux=None,
       out_dtype=f32, a_shards=1, b_shards=1, out_shards=1, exchange=None, norm_bwd=None, loss_target=None):
    ar, ac = a.shape[-2], a.shape[-1] * a_shards
    br, bc = b.shape[-2], b.shape[-1] * b_shards
    if mode == "nn":
        M, K, N = ar, ac, bc
        assert br == K
    elif mode == "nt":
        M, K, N = ar, ac, br
        assert bc == K
    else:
        M, K, N = ac, ar, bc
        assert br == K
    tm, tn, tk = _tile(M, tm), _tile(N, tn), _tile(K, tk)
    if a_shards > 1:
        if mode == "tn":
            tm = _tile(ac // a_shards, tm)
        else:
            tk = _tile(ac // a_shards, tk)
    if b_shards > 1:
        if mode == "nt":
            tk = _tile(bc // b_shards, tk)
        else:
            tn = _tile(bc // b_shards, tn)
    if out_shards > 1:
        tn = _tile(N // out_shards, tn)
    nm, nn_, nk = M // tm, N // tn, K // tk
    I = lambda i, j, k: i
    J = lambda i, j, k: j
    Kk = lambda i, j, k: k
    if mode == "nn":
        a_spec = _opspec((tm, tk), I, Kk, a_shards, nk)
        b_spec = _opspec((tk, tn), Kk, J, b_shards, nn_)
        dims = _NN
    elif mode == "nt":
        a_spec = _opspec((tm, tk), I, Kk, a_shards, nk)
        b_spec = _opspec((tn, tk), J, Kk, b_shards, nk)
        dims = _NT
    else:
        a_spec = _opspec((tk, tm), Kk, I, a_shards, nm)
        b_spec = _opspec((tk, tn), Kk, J, b_shards, nn_)
        dims = _TN
    in_specs = [a_spec, b_spec]
    args = [a, b]
    tile = pl.BlockSpec((tm, tn), lambda i, j, k: (i, j))
    if epi != "none":
        in_specs.append(tile)
        args.append(aux)
    if out_shards == 1:
        out_spec = tile
        out_shape = jax.ShapeDtypeStruct((M, N), out_dtype)
    else:
        per = nn_ // out_shards
        out_spec = pl.BlockSpec((None, tm, tn), lambda i, j, k: (j // per, i, j % per))
        out_shape = jax.ShapeDtypeStruct((out_shards, M, N // out_shards), out_dtype)
    out_specs, out_shapes = [out_spec], [out_shape]
    if norm_bwd is not None:
        nx, ng, nres, denom = norm_bwd
        assert tn == N and out_shards == 1 and out_dtype == f32
        vec = pl.BlockSpec((1, tn), lambda i, j, k: (0, 0))
        in_specs += [tile, vec, tile]
        args += [nx, ng, nres]
        out_specs += [vec, tile]
        out_shapes += [jax.ShapeDtypeStruct((1, N), f32), jax.ShapeDtypeStruct((M, N), bf16)]
    if loss_target is not None:
        assert tn == N and out_shards == 1 and out_dtype == f32 and norm_bwd is None
        in_specs.append(tile)
        args.append(loss_target)
        out_specs += [pl.BlockSpec((1, 1), lambda i, j, k: (0, 0)), tile]
        out_shapes += [jax.ShapeDtypeStruct((1, 1), f32), jax.ShapeDtypeStruct((M, N), bf16)]
    n_in = len(in_specs)

    def body(*refs):
        a_ref, b_ref = refs[0], refs[1]
        aux_ref = refs[2] if epi != "none" else None
        o_ref, acc_ref = refs[n_in], refs[-1]
        i, k = pl.program_id(0), pl.program_id(2)

        @pl.when(k == 0)
        def _():
            acc_ref[...] = jnp.zeros_like(acc_ref)

        av = a_ref[...]
        if pro == "relu2":
            av = jnp.square(jnp.maximum(av.astype(f32), 0.0))
        acc_ref[...] += _dot(av, b_ref[...], dims)

        @pl.when(k == nk - 1)
        def _():
            res = acc_ref[...]
            if epi == "add":
                res = res + aux_ref[...].astype(f32)
            elif epi == "mul2relu":
                res = res * (2.0 * jnp.maximum(aux_ref[...].astype(f32), 0.0))
            if loss_target is not None:
                l_ref, twin_ref = refs[n_in + 1], refs[n_in + 2]
                err = res - refs[n_in - 1][...]
                dy = err * (1.0 / N)
                o_ref[...] = dy
                twin_ref[...] = dy.astype(bf16)

                @pl.when(i == 0)
                def _():
                    l_ref[...] = jnp.zeros_like(l_ref)

                rows = jnp.sum(err * err, axis=1, keepdims=True) * (1.0 / N)
                l_ref[...] += 0.5 * jnp.sum(rows, axis=0, keepdims=True)
            elif norm_bwd is None:
                o_ref[...] = res.astype(out_dtype)
            else:
                x_ref, g_ref, res_ref = refs[n_in - 3:n_in]
                dg_ref, twin_ref = refs[n_in + 1], refs[n_in + 2]
                _, vjp = jax.vjp(lambda xv, gv: _rms(xv, gv, denom), x_ref[...], g_ref[...])
                dx, dg = vjp(res)
                dx = dx + res_ref[...]
                o_ref[...] = dx
                twin_ref[...] = dx.astype(bf16)

                @pl.when(i == 0)
                def _():
                    dg_ref[...] = jnp.zeros_like(dg_ref)

                dg_ref[...] += dg

    acc = [pltpu.VMEM((tm, tn), f32)]
    single = norm_bwd is None and loss_target is None
    if exchange is not None:
        outs, moved = _call_with_exchange(body, args, exchange, name=name, grid=(nm, nn_, nk), in_specs=in_specs,
                                          out_specs=out_specs, out_shape=out_shapes, scratch_shapes=acc)
        return (outs[0] if single else tuple(outs)), moved
    outs = pl.pallas_call(
        body, name=name, grid=(nm, nn_, nk), in_specs=in_specs, out_specs=out_specs, out_shape=out_shapes,
        scratch_shapes=acc, compiler_params=_cp(("parallel" if single else "arbitrary", "parallel", "arbitrary")),
    )(*args)
    return outs[0] if single else tuple(outs)


def rms_norm(x, g, denom, *, name, ts=512, exchange=None):
    S, D = x.shape
    ts = _tile(S, ts)

    def body(x_ref, g_ref, h_ref):
        h_ref[...] = _rms(x_ref[...], g_ref[...], denom).astype(bf16)

    row = pl.BlockSpec((ts, D), lambda i: (i, 0))
    kw = dict(name=name, grid=(S // ts,), in_specs=[row, pl.BlockSpec((1, D), lambda i: (0, 0))])
    if exchange is not None:
        (h,), moved = _call_with_exchange(body, (x, g), exchange, out_specs=[row],
                                          out_shape=[jax.ShapeDtypeStruct((S, D), bf16)], **kw)
        return h, moved
    return pl.pallas_call(body, out_specs=row, out_shape=jax.ShapeDtypeStruct((S, D), bf16),
                          compiler_params=_cp(("parallel",)), **kw)(x, g)


def rms_bwd(x, g, dy, res, denom, *, name, ts=256, twin=False):
    S, D = x.shape
    ts = _tile(S, ts)
    has_res = res is not None

    def body(*refs):
        x_ref, g_ref, dy_ref = refs[:3]
        outs = refs[4:] if has_res else refs[3:]
        _, vjp = jax.vjp(lambda xv, gv: _rms(xv, gv, denom), x_ref[...], g_ref[...])
        dx, dg = vjp(dy_ref[...])
        if has_res:
            dx = dx + refs[3][...]
        outs[0][...] = dx
        if twin:
            outs[2][...] = dx.astype(bf16)
        dg_ref = outs[1]

        @pl.when(pl.program_id(0) == 0)
        def _():
            dg_ref[...] = jnp.zeros_like(dg_ref)

        dg_ref[...] += dg

    row = pl.BlockSpec((ts, D), lambda i: (i, 0))
    vec = pl.BlockSpec((1, D), lambda i: (0, 0))
    in_specs = [row, vec, row] + ([row] if has_res else [])
    args = [x, g, dy] + ([res] if has_res else [])
    return pl.pallas_call(
        body, name=name, grid=(S // ts,), in_specs=in_specs, out_specs=[row, vec] + ([row] if twin else []),
        out_shape=[jax.ShapeDtypeStruct((S, D), f32), jax.ShapeDtypeStruct((1, D), f32)]
        + ([jax.ShapeDtypeStruct((S, D), bf16)] if twin else []),
        compiler_params=_cp(("arbitrary",)),
    )(*args)


LOG2E = 1.4426950408889634
LN2 = 0.6931471805599453


def _qk_fn(q, k, gq, gk):
    qs, ks = [], []
    for h in range(SB_HEADS):
        sl = slice(h * LANES, (h + 1) * LANES)
        qs.append(_rms(q[:, sl], gq, SB_HEAD_DIM) * (SB_HEAD_DIM ** -0.5 * LOG2E))
        ks.append(_rms(k[:, sl], gk, SB_HEAD_DIM))
    return jnp.concatenate(qs, axis=1), jnp.concatenate(ks, axis=1)


def qkv_prep(qkv, gq, gk, *, ts=256):
    S = qkv.shape[0]
    W = SB_HEADS * LANES
    ts = _tile(S, ts)

    def body(q_ref, k_ref, v_ref, gq_ref, gk_ref, qn_ref, kn_ref, vb_ref):
        qn, kn = _qk_fn(q_ref[...], k_ref[...], gq_ref[...], gk_ref[...])
        qn_ref[...] = qn.astype(bf16)
        kn_ref[...] = kn.astype(bf16)
        vb_ref[...] = v_ref[...].astype(bf16)

    out = jax.ShapeDtypeStruct((S, W), bf16)
    gspec = pl.BlockSpec((1, LANES), lambda i: (0, 0))
    ospec = pl.BlockSpec((ts, W), lambda i: (i, 0))
    col = lambda c: pl.BlockSpec((ts, W), lambda i: (i, c))
    return pl.pallas_call(
        body, name="qkv_prep", grid=(S // ts,), in_specs=[col(0), col(1), col(2), gspec, gspec],
        out_specs=[ospec, ospec, ospec], out_shape=[out, out, out], compiler_params=_cp(("parallel",)),
    )(qkv, qkv, qkv, gq, gk)


def qkv_bwd(qkv, gq, gk, dqn, dkn, dv, exchange, *, ts=256):
    S = qkv.shape[0]
    W = SB_HEADS * LANES
    ts = _tile(S, ts)

    def body(q_ref, k_ref, gq_ref, gk_ref, dqn_ref, dkn_ref, dv_ref, o_ref, dgq_ref, dgk_ref):
        _, vjp = jax.vjp(_qk_fn, q_ref[...], k_ref[...], gq_ref[...], gk_ref[...])
        dq, dk, dgq, dgk = vjp((dqn_ref[...] * LN2, dkn_ref[...] * LN2))
        o_ref[:, 0:W] = dq.astype(bf16)
        o_ref[:, W:2 * W] = dk.astype(bf16)
        o_ref[:, 2 * W:3 * W] = dv_ref[...].astype(bf16)

        @pl.when(pl.program_id(0) == 0)
        def _():
            dgq_ref[...] = jnp.zeros_like(dgq_ref)
            dgk_ref[...] = jnp.zeros_like(dgk_ref)

        dgq_ref[...] += dgq
        dgk_ref[...] += dgk

    gspec = pl.BlockSpec((1, LANES), lambda i: (0, 0))
    row = pl.BlockSpec((ts, W), lambda i: (i, 0))
    col = lambda c: pl.BlockSpec((ts, W), lambda i: (i, c))
    (dqkv, dgq, dgk), moved = _call_with_exchange(
        body, (qkv, qkv, gq, gk, dqn, dkn, dv), exchange, name="qkv_bwd", grid=(S // ts,),
        in_specs=[col(0), col(1), gspec, gspec, row, row, row],
        out_specs=[pl.BlockSpec((ts, 3 * W), lambda i: (i, 0)), gspec, gspec],
        out_shape=[jax.ShapeDtypeStruct((S, 3 * W), bf16), jax.ShapeDtypeStruct((1, LANES), f32),
                   jax.ShapeDtypeStruct((1, LANES), f32)])
    return dqkv, dgq, dgk, moved


def _sb_weights(q, ks, R, masked, row, col, UU):
    ls = [_dot(q, k, _NT) for k in ks]
    lbs, lm0s, cats = [], [], []
    for l, diag in zip(ls, masked):
        neg_abs = pltpu.bitcast(pltpu.bitcast(l, jnp.uint32) | jnp.uint32(0x80000000), f32)
        lp = jnp.log2(1.0 + jnp.exp2(neg_abs))
        lb = jnp.minimum(l, 0.0) - lp
        lm = lb - l
        if diag:
            lm = jnp.where(col < row, lm, 0.0)
        hi = lm.astype(bf16)
        lo = (lm - hi.astype(f32)).astype(bf16)
        lbs.append(lb)
        lm0s.append(lm[:, 0:1])
        cats.append(jnp.concatenate([hi, lo], axis=1))
    sums = [_dot(c, UU, _NN) for c in cats]
    ws = []
    for lb, lm0, A, diag in zip(lbs, lm0s, sums, masked):
        w = jnp.exp2(lb + (A + R))
        if diag:
            w = jnp.where(col < row, w, 0.0)
        R = R + (A[:, 0:1] + lm0)
        ws.append(w)
    return lbs, ws, R


def _tri2(tk):
    r = lax.broadcasted_iota(jnp.int32, (2 * tk, tk), 0)
    r = jnp.where(r >= tk, r - tk, r)
    c = lax.broadcasted_iota(jnp.int32, (2 * tk, tk), 1)
    return (r > c).astype(bf16)


SB_GROUP = 8


SB_ALL_ZERO_BELOW = -160.0


def _sweep(i, blocks_of, carry, descending, right_sum=None, ran=None):
    G = SB_GROUP
    n = jnp.maximum(i - 1, 0)
    rem, full = n % G, n // G
    asc = lambda js: js if descending else js[::-1]

    def first_group(c):
        one = lambda c: blocks_of([i], c, [True])
        two = lambda c: blocks_of(asc([i, i - 1]), c, asc([True, False]))
        return lax.cond(i >= 1, two, one, c)

    def body(p, c):
        return blocks_of(asc([i - 2 - p * G - u for u in range(G)]), c, [False] * G)

    def left_over(r):
        return lambda c: blocks_of(asc([r - 1 - u for u in range(r)]), c, [False] * r) if r else c

    if descending:
        carry = first_group(carry)
        alive = lambda c: jnp.max(right_sum(c)) > SB_ALL_ZERO_BELOW
        bodies, carry = lax.while_loop(lambda s: (s[0] < full) & alive(s[1]),
                                       lambda s: (s[0] + 1, body(s[0], s[1])), (jnp.int32(0), carry))
        tail = (bodies == full) & alive(carry)
        carry = lax.switch(jnp.where(tail, rem, 0), [left_over(r) for r in range(G)], carry)
        return carry, (bodies, tail)
    bodies, tail = ran
    carry = lax.switch(jnp.where(tail, rem, 0), [left_over(r) for r in range(G)], carry)
    carry = lax.fori_loop(0, bodies, lambda t, c: body(bodies - 1 - t, c), carry)
    return first_group(carry)


def sb_fwd(qn, kn, vb, exchange, *, tq=256):
    S, W = qn.shape
    H = W // LANES
    tq = _tile(S, tq)
    tk = tq
    nq = S // tq

    def body(q_ref, k_ref, v_ref, o_ref, uu_s):
        i = pl.program_id(1)

        @pl.when((pl.program_id(0) == 0) & (i == 0))
        def _():
            uu_s[...] = _tri2(tk)

        q = q_ref[...]
        row = lax.broadcasted_iota(jnp.int32, (tq, tk), 0)
        col = lax.broadcasted_iota(jnp.int32, (tq, tk), 1)
        UU = uu_s[...]

        def blocks(js, c, masked):
            rows = [pl.ds(pl.multiple_of(j * tk, tk), tk) for j in js]
            _, ws, R = _sb_weights(q, [k_ref[r, :] for r in rows], c[0], masked, row, col, UU)
            acc = c[1]
            for w, r in zip(ws, rows):
                acc = acc + _dot(w, v_ref[r, :], _NN)
            return R, acc

        c, _ = _sweep(i, blocks, (jnp.zeros((tq, 1), f32), jnp.zeros((tq, LANES), f32)), True, lambda c: c[0])
        o_ref[...] = c[1]

    qspec = pl.BlockSpec((tq, LANES), lambda h, i: (i, h))
    kspec = pl.BlockSpec((S, LANES), lambda h, i: (0, h))
    (o,), moved = _call_with_exchange(
        body, (qn, kn, vb), exchange, name="sb_fwd", grid=(H, nq), in_specs=[qspec, kspec, kspec], out_specs=[qspec],
        out_shape=[jax.ShapeDtypeStruct((S, W), f32)], scratch_shapes=[pltpu.VMEM((2 * tk, tk), bf16)])
    return o, moved


def sb_bwd(qn, kn, vb, do, exchange, *, tq=256):
    S, W = qn.shape
    H = W // LANES
    tq = _tile(S, tq)
    tk = tq
    nq = S // tq

    def body(q_ref, k_ref, v_ref, do_ref, dq_ref, dk_ref, dv_ref, dz_s, beta_s, uu_s, ue_s):
        i = pl.program_id(1)
        row = lax.broadcasted_iota(jnp.int32, (tq, tk), 0)
        col = lax.broadcasted_iota(jnp.int32, (tq, tk), 1)

        @pl.when((pl.program_id(0) == 0) & (i == 0))
        def _():
            uu_s[...] = _tri2(tk)
            ue_s[...] = (row < col).astype(bf16)

        @pl.when(i == 0)
        def _():
            dk_ref[...] = jnp.zeros_like(dk_ref)
            dv_ref[...] = jnp.zeros_like(dv_ref)

        q = q_ref[...]
        dob = do_ref[...].astype(bf16)
        UU = uu_s[...]
        Ue = ue_s[...]

        def sweep1(js, R, masked):
            rows = [pl.ds(pl.multiple_of(j * tk, tk), tk) for j in js]
            dws = [_dot(dob, v_ref[r, :], _NT) for r in rows]
            lbs, ws, R = _sb_weights(q, [k_ref[r, :] for r in rows], R, masked, row, col, UU)
            for j, lb, w, dw in zip(js, lbs, ws, dws):
                dz_s[j] = (dw * w).astype(bf16)
                beta_s[j] = jnp.exp2(lb).astype(bf16)
            for r, w in zip(rows, ws):
                dv_ref[r, :] += _dot(w, dob, _TN)
            return R

        _, ran = _sweep(i, sweep1, jnp.zeros((tq, 1), f32), True, lambda R: R)

        def sweep2(js, c, masked):
            rows = [pl.ds(pl.multiple_of(j * tk, tk), tk) for j in js]
            dzbs = [dz_s[j] for j in js]
            sums = [_dot(dzb, Ue, _NN) for dzb in dzbs]
            Lz, dq = c
            dlbs = []
            for j, dzb, Cz, diag in zip(js, dzbs, sums, masked):
                dz = dzb.astype(f32)
                dl = dz - beta_s[j].astype(f32) * (dz + (Cz + Lz))
                if diag:
                    dl = jnp.where(col < row, dl, 0.0)
                Lz = Lz + (Cz[:, tk - 1:tk] + dz[:, tk - 1:tk])
                dlbs.append(dl.astype(bf16))
            for r, dlb in zip(rows, dlbs):
                dq = dq + _dot(dlb, k_ref[r, :], _NN)
            for r, dlb in zip(rows, dlbs):
                dk_ref[r, :] += _dot(dlb, q, _TN)
            return Lz, dq

        c = _sweep(i, sweep2, (jnp.zeros((tq, 1), f32), jnp.zeros((tq, LANES), f32)), False, ran=ran)
        dq_ref[...] = c[1]

    qspec = pl.BlockSpec((tq, LANES), lambda h, i: (i, h))
    kspec = pl.BlockSpec((S, LANES), lambda h, i: (0, h))
    full = jax.ShapeDtypeStruct((S, W), f32)
    (dq, dk, dv), moved = _call_with_exchange(
        body, (qn, kn, vb, do), exchange, name="sb_bwd", grid=(H, nq), in_specs=[qspec, kspec, kspec, qspec],
        out_specs=[qspec, kspec, kspec], out_shape=[full, full, full],
        scratch_shapes=[pltpu.VMEM((nq, tq, tk), bf16), pltpu.VMEM((nq, tq, tk), bf16),
                        pltpu.VMEM((2 * tk, tk), bf16), pltpu.VMEM((tk, tk), bf16)])
    return dq, dk, dv, moved


def _xa_fn(qx, kv, gq, gk):
    XW = XA_HEADS * XA_HEAD_DIM
    outs = []
    for h in range(XA_HEADS):
        sl = slice(h * XA_HEAD_DIM, (h + 1) * XA_HEAD_DIM)
        qn = _rms(qx[:, sl], gq, XA_HEAD_DIM)
        kn = _rms(kv[:, sl], gk, XA_HEAD_DIM)
        v = kv[:, XW + h * XA_HEAD_DIM:XW + (h + 1) * XA_HEAD_DIM]
        s = bdot_nt(qn, kn) * (XA_HEAD_DIM ** -0.5)
        e = jnp.exp(s - lax.stop_gradient(jnp.max(s, axis=-1, keepdims=True)))
        p = e / jnp.sum(e, axis=-1, keepdims=True)
        outs.append(bdot_nn(p, v))
    return jnp.concatenate(outs, axis=1)


def xa_fwd(qx, kv, gq, gk, *, ts=256):
    S, XW = qx.shape
    M = kv.shape[0]
    ts = _tile(S, ts)

    def body(q_ref, kv_ref, gq_ref, gk_ref, o_ref):
        o_ref[...] = _xa_fn(q_ref[...], kv_ref[...], gq_ref[...], gk_ref[...]).astype(bf16)

    row = pl.BlockSpec((ts, XW), lambda i: (i, 0))
    gspec = pl.BlockSpec((1, XA_HEAD_DIM), lambda i: (0, 0))
    return pl.pallas_call(
        body, name="xa_fwd", grid=(S // ts,),
        in_specs=[row, pl.BlockSpec((M, 2 * XW), lambda i: (0, 0)), gspec, gspec], out_specs=row,
        out_shape=jax.ShapeDtypeStruct((S, XW), bf16), compiler_params=_cp(("parallel",)),
    )(qx, kv, gq, gk)


def xa_bwd(qx, kv, gq, gk, do, *, ts=256):
    S, XW = qx.shape
    M = kv.shape[0]
    ts = _tile(S, ts)

    def body(q_ref, kv_ref, gq_ref, gk_ref, do_ref, dq_ref, dkv_ref, dgq_ref, dgk_ref):
        _, vjp = jax.vjp(_xa_fn, q_ref[...], kv_ref[...], gq_ref[...], gk_ref[...])
        dq, dkv, dgq, dgk = vjp(do_ref[...].astype(f32))
        dq_ref[...] = dq.astype(bf16)

        @pl.when(pl.program_id(0) == 0)
        def _():
            dkv_ref[...] = jnp.zeros_like(dkv_ref)
            dgq_ref[...] = jnp.zeros_like(dgq_ref)
            dgk_ref[...] = jnp.zeros_like(dgk_ref)

        dkv_ref[...] += dkv
        dgq_ref[...] += dgq
        dgk_ref[...] += dgk

    row = pl.BlockSpec((ts, XW), lambda i: (i, 0))
    gspec = pl.BlockSpec((1, XA_HEAD_DIM), lambda i: (0, 0))
    kvspec = pl.BlockSpec((M, 2 * XW), lambda i: (0, 0))
    gshape = jax.ShapeDtypeStruct((1, XA_HEAD_DIM), f32)
    return pl.pallas_call(
        body, name="xa_bwd", grid=(S // ts,), in_specs=[row, kvspec, gspec, gspec, row],
        out_specs=[row, kvspec, gspec, gspec],
        out_shape=[jax.ShapeDtypeStruct((S, XW), bf16), jax.ShapeDtypeStruct((M, 2 * XW), f32), gshape, gshape],
        compiler_params=_cp(("arbitrary",)),
    )(qx, kv, gq, gk, do)


def _s5_prep_fn(a_re, a_im, ldt, bT_re, bT_im, cT_re, cT_im):
    G, P, C = SSM_GROUPS, SSM_STATE, SSM_GROUP
    GP, GC = G * P, G * C
    lg_p, lg_c = P.bit_length() - 1, C.bit_length() - 1
    gi = lax.broadcasted_iota(jnp.int32, (G, GP), 0)
    ci = lax.broadcasted_iota(jnp.int32, (G, GP), 1) >> lg_p
    expand_dt = (gi == ci).astype(f32)
    dte = jnp.dot(jnp.exp(ldt), expand_dt, precision=lax.Precision.HIGHEST, preferred_element_type=f32)
    zr, zi = a_re * dte, a_im * dte
    mag = jnp.exp(zr)
    abr, abi = mag * jnp.cos(zi), mag * jnp.sin(zi)
    nr, ni = abr - 1.0, abi
    den = a_re * a_re + a_im * a_im
    cr = (nr * a_re + ni * a_im) / den
    cim = (ni * a_re - nr * a_im) / den
    bbr = cr * bT_re - cim * bT_im
    bbi = cr * bT_im + cim * bT_re
    rowg = lax.broadcasted_iota(jnp.int32, (GC, GP), 0) >> lg_c
    colg = lax.broadcasted_iota(jnp.int32, (GC, GP), 1) >> lg_p
    diag = rowg == colg

    def expand(t):
        return jnp.where(diag, jnp.broadcast_to(t[None], (G, C, GP)).reshape(GC, GP), 0.0)

    return abr, abi, expand(bbr), expand(bbi), expand(cT_re), expand(-cT_im)


def s5_prep(a_re, a_im, ldt, bT_re, bT_im, cT_re, cT_im):
    GP, GC = SSM_GROUPS * SSM_STATE, SSM_GROUPS * SSM_GROUP

    def body(a_re_ref, a_im_ref, ldt_ref, bTr_ref, bTi_ref, cTr_ref, cTi_ref, abr_ref, abi_ref, B_ref, C_ref):
        abr, abi, Br, Bi, Cr, Ci = _s5_prep_fn(a_re_ref[...], a_im_ref[...], ldt_ref[...], bTr_ref[...],
                                               bTi_ref[...], cTr_ref[...], cTi_ref[...])
        abr_ref[...] = abr
        abi_ref[...] = abi
        B_ref[0] = Br.astype(bf16)
        B_ref[1] = Bi.astype(bf16)
        C_ref[0] = Cr.astype(bf16)
        C_ref[1] = Ci.astype(bf16)

    vec = jax.ShapeDtypeStruct((1, GP), f32)
    mat = jax.ShapeDtypeStruct((2, GC, GP), bf16)
    return pl.pallas_call(body, name="s5_prep", out_shape=[vec, vec, mat, mat], compiler_params=_cp())(
        a_re, a_im, ldt, bT_re, bT_im, cT_re, cT_im)


def s5_prep_bwd(a_re, a_im, ldt, bT_re, bT_im, cT_re, cT_im, dabr, dabi, dB, dC):
    def body(a_re_ref, a_im_ref, ldt_ref, bTr_ref, bTi_ref, cTr_ref, cTi_ref, dabr_ref, dabi_ref, dB_ref, dC_ref,
             *outs):
        _, vjp = jax.vjp(_s5_prep_fn, a_re_ref[...], a_im_ref[...], ldt_ref[...], bTr_ref[...], bTi_ref[...],
                         cTr_ref[...], cTi_ref[...])
        grads = vjp((dabr_ref[...], dabi_ref[...], dB_ref[0], dB_ref[1], dC_ref[0], dC_ref[1]))
        for o_ref, gv in zip(outs, grads):
            o_ref[...] = gv

    ins = (a_re, a_im, ldt, bT_re, bT_im, cT_re, cT_im)
    return pl.pallas_call(body, name="s5_prep_bwd", out_shape=[jax.ShapeDtypeStruct(v.shape, f32) for v in ins],
                          compiler_params=_cp())(*ins, dabr, dabi, dB, dC)


def _cmul(ar, ai, br, bi):
    return ar * br - ai * bi, ar * bi + ai * br


SCAN_CHUNKS = 32


def _chunk_carry(Lr, Li, Pr, Pi, scratch, reverse):
    lr_ref, li_ref, cr_ref, ci_ref = scratch
    lr_ref[...] = Lr
    li_ref[...] = Li
    cur_r = jnp.zeros((1, LANES), f32)
    cur_i = jnp.zeros((1, LANES), f32)
    order = range(SCAN_CHUNKS - 1, -1, -1) if reverse else range(SCAN_CHUNKS)
    for c in order:
        cr_ref[pl.ds(c, 1), :] = cur_r
        ci_ref[pl.ds(c, 1), :] = cur_i
        mr, mi = _cmul(Pr, Pi, cur_r, cur_i)
        cur_r, cur_i = lr_ref[pl.ds(c, 1), :] + mr, li_ref[pl.ds(c, 1), :] + mi
    return cr_ref[...], ci_ref[...]


def _chunk_rows(j):
    return pl.ds(pl.multiple_of(j * SCAN_CHUNKS, SCAN_CHUNKS), SCAN_CHUNKS)


def row_shuffle(x, a, b, *, name, add=None, out_dtype=f32):
    S, W = x.shape
    assert a * b == S and x.dtype == f32

    def body(*refs):
        x_ref, o_ref = refs[0], refs[-1]

        def step(i, _):
            dst = pl.ds(pl.multiple_of(i * b, b), b)
            v = x_ref[pl.ds(i, b, stride=a), :]
            if add is not None:
                v = v + refs[1][dst, :]
            o_ref[dst, :] = v.astype(out_dtype)
            return 0

        lax.fori_loop(0, a, step, 0)

    col = pl.BlockSpec((S, LANES), lambda t: (0, t))
    args = [x] + ([add] if add is not None else [])
    return pl.pallas_call(
        body, name=name, grid=(W // LANES,), in_specs=[col] * len(args), out_specs=col,
        out_shape=jax.ShapeDtypeStruct((S, W), out_dtype), compiler_params=_cp(("parallel",)),
    )(*args)


def _scan_scratch(n):
    small = pltpu.VMEM((SCAN_CHUNKS, LANES), f32)
    return [pltpu.VMEM((n, LANES), f32), pltpu.VMEM((n, LANES), f32), small, small, small, small]


def _drive(x_ref, m_ref, work_ref):
    S = x_ref.shape[0]
    rows = min(S, 1024)
    m = jnp.concatenate([m_ref[0], m_ref[1]], axis=1)

    def chunk(c, _):
        r = pl.ds(pl.multiple_of(c * rows, rows), rows)
        y = _dot(x_ref[r, :], m, _NN)
        work_ref[0, r, :] = y[:, :LANES]
        work_ref[1, r, :] = y[:, LANES:]
        return 0

    lax.fori_loop(0, S // rows, chunk, 0)


def scan_fwd(u_il, Bm, abr, abi, exchange):
    S, C = u_il.shape
    N = Bm.shape[2]
    n = S // SCAN_CHUNKS
    shp = (SCAN_CHUNKS, LANES)

    def body(u_ref, B_ref, ar_ref, ai_ref, st_ref, work_ref, pwr_ref, pwi_ref, *scratch):
        a1r, a1i = ar_ref[...], ai_ref[...]
        ar = jnp.broadcast_to(a1r, shp)
        ai = jnp.broadcast_to(a1i, shp)
        _drive(u_ref, B_ref, work_ref)
        sr_ref, si_ref = work_ref.at[0], work_ref.at[1]

        def step(j, c):
            sr, si, pr, pi = c
            rows = _chunk_rows(j)
            mr, mi = _cmul(ar, ai, sr, si)
            sr, si = mr + sr_ref[rows, :], mi + si_ref[rows, :]
            sr_ref[rows, :] = sr
            si_ref[rows, :] = si
            pwr_ref[pl.ds(j, 1), :] = pr
            pwi_ref[pl.ds(j, 1), :] = pi
            npr, npi = _cmul(a1r, a1i, pr, pi)
            return sr, si, npr, npi

        z = jnp.zeros(shp, f32)
        sr, si, _, _ = lax.fori_loop(0, n, step, (z, z, a1r, a1i), unroll=2)
        cr, ci = _chunk_carry(sr, si, pwr_ref[pl.ds(n - 1, 1), :], pwi_ref[pl.ds(n - 1, 1), :], scratch, False)

        def step2(j, _):
            rows = _chunk_rows(j)
            pr = jnp.broadcast_to(pwr_ref[pl.ds(j, 1), :], shp)
            pi = jnp.broadcast_to(pwi_ref[pl.ds(j, 1), :], shp)
            mr, mi = _cmul(pr, pi, cr, ci)
            st_ref[0, rows, :] = (sr_ref[rows, :] + mr).astype(bf16)
            st_ref[1, rows, :] = (si_ref[rows, :] + mi).astype(bf16)
            return 0

        lax.fori_loop(0, n, step2, 0, unroll=4)

    blk = pl.BlockSpec((2, S, LANES), lambda t: (0, 0, t))
    vec = pl.BlockSpec((1, LANES), lambda t: (0, t))
    (st,), moved = _call_with_exchange(
        body, (u_il, Bm, abr, abi), exchange, name="scan_fwd", grid=(N // LANES,),
        in_specs=[pl.BlockSpec((S, C), lambda t: (0, 0)), pl.BlockSpec((2, C, LANES), lambda t: (0, 0, t)), vec, vec],
        out_specs=[blk], out_shape=[jax.ShapeDtypeStruct((2, S, N), bf16)],
        scratch_shapes=[pltpu.VMEM((2, S, LANES), f32)] + _scan_scratch(n))
    return st, moved


def scan_bwd(dy_il, Cm, st, abr, abi, exchange):
    _, S, N = st.shape
    C = dy_il.shape[1]
    n = S // SCAN_CHUNKS
    shp = (SCAN_CHUNKS, LANES)

    def body(dy_ref, C_ref, st_ref, ar_ref, ai_ref, g_ref, dar_ref, dai_ref, work_ref, qwr_ref, qwi_ref, *scratch):
        a1r, a1i = ar_ref[...], -ai_ref[...]
        ar = jnp.broadcast_to(a1r, shp)
        nai = jnp.broadcast_to(a1i, shp)
        _drive(dy_ref, C_ref, work_ref)
        gr_ref, gi_ref = work_ref.at[0], work_ref.at[1]
        sr_ref, si_ref = st_ref.at[0], st_ref.at[1]

        def step(jj, c):
            gr, gi, qr, qi = c
            j = n - 1 - jj
            rows = _chunk_rows(j)
            mr, mi = _cmul(ar, nai, gr, gi)
            gr, gi = mr + gr_ref[rows, :], mi + gi_ref[rows, :]
            gr_ref[rows, :] = gr
            gi_ref[rows, :] = gi
            qwr_ref[pl.ds(j, 1), :] = qr
            qwi_ref[pl.ds(j, 1), :] = qi
            nqr, nqi = _cmul(a1r, a1i, qr, qi)
            return gr, gi, nqr, nqi

        z = jnp.zeros(shp, f32)
        gr, gi, _, _ = lax.fori_loop(0, n, step, (z, z, a1r, a1i), unroll=2)
        cr, ci = _chunk_carry(gr, gi, qwr_ref[pl.ds(0, 1), :], qwi_ref[pl.ds(0, 1), :], scratch, True)
        sub = lax.broadcasted_iota(jnp.int32, shp, 0)

        def fix(j, spr, spi, acc):
            rows = _chunk_rows(j)
            qr = jnp.broadcast_to(qwr_ref[pl.ds(j, 1), :], shp)
            qi = jnp.broadcast_to(qwi_ref[pl.ds(j, 1), :], shp)
            mr, mi = _cmul(qr, qi, cr, ci)
            gr = gr_ref[rows, :] + mr
            gi = gi_ref[rows, :] + mi
            g_ref[0, rows, :] = gr.astype(bf16)
            g_ref[1, rows, :] = gi.astype(bf16)
            return acc[0] + gr * spr + gi * spi, acc[1] + gi * spr - gr * spi

        last = _chunk_rows(n - 1)
        spr = jnp.where(sub == 0, 0.0, pltpu.roll(sr_ref[last, :].astype(f32), 1, 0))
        spi = jnp.where(sub == 0, 0.0, pltpu.roll(si_ref[last, :].astype(f32), 1, 0))
        acc = fix(0, spr, spi, (z, z))

        def step2(j, acc):
            prev = _chunk_rows(j - 1)
            return fix(j, sr_ref[prev, :].astype(f32), si_ref[prev, :].astype(f32), acc)

        acc = lax.fori_loop(1, n, step2, acc)
        dar_ref[...] = jnp.sum(acc[0], axis=0, keepdims=True)
        dai_ref[...] = jnp.sum(acc[1], axis=0, keepdims=True)

    blk = pl.BlockSpec((2, S, LANES), lambda t: (0, 0, t))
    vec = pl.BlockSpec((1, LANES), lambda t: (0, t))
    vshape = jax.ShapeDtypeStruct((1, N), f32)
    (g, dar, dai), moved = _call_with_exchange(
        body, (dy_il, Cm, st, abr, abi), exchange, name="scan_bwd", grid=(N // LANES,),
        in_specs=[pl.BlockSpec((S, C), lambda t: (0, 0)), pl.BlockSpec((2, C, LANES), lambda t: (0, 0, t)), blk,
                  vec, vec],
        out_specs=[blk, vec, vec], out_shape=[jax.ShapeDtypeStruct((2, S, N), bf16), vshape, vshape],
        scratch_shapes=[pltpu.VMEM((2, S, LANES), f32)] + _scan_scratch(n))
    return g, dar, dai, moved


def _glu_fn(ypre, wglu):
    y = jax.nn.gelu(ypre)
    return y * jax.nn.sigmoid(bdot_nn(y, wglu))


def glu_fwd(ypre0, u, d, wglu, g_out, *, ts=512):
    S, W = u.shape
    ts = _tile(S, ts)

    def body(y0_ref, u_ref, d_ref, w_ref, g_ref, ypre_ref, z_ref, zn_ref):
        ypre = y0_ref[...] + d_ref[...] * u_ref[...]
        z = _glu_fn(ypre, w_ref[...])
        ypre_ref[...] = ypre
        z_ref[...] = z
        zn_ref[...] = _rms(z, g_ref[...], W).astype(bf16)

    row = pl.BlockSpec((ts, W), lambda i: (i, 0))
    vec = pl.BlockSpec((1, W), lambda i: (0, 0))
    full = jax.ShapeDtypeStruct((S, W), f32)
    return pl.pallas_call(
        body, name="glu_fwd", grid=(S // ts,),
        in_specs=[row, row, vec, pl.BlockSpec((W, W), lambda i: (0, 0)), vec], out_specs=[row, row, row],
        out_shape=[full, full, jax.ShapeDtypeStruct((S, W), bf16)], compiler_params=_cp(("parallel",)),
    )(ypre0, u, d, wglu, g_out)


def glu_bwd(ypre, u, d, wglu, dz, *, ts=512):
    S, W = u.shape
    ts = _tile(S, ts)

    def body(y_ref, u_ref, d_ref, w_ref, dz_ref, dy_ref, du_ref, dw_ref, dd_ref):
        _, vjp = jax.vjp(_glu_fn, y_ref[...], w_ref[...])
        dy, dw = vjp(dz_ref[...])
        dy_ref[...] = dy
        du_ref[...] = d_ref[...] * dy

        @pl.when(pl.program_id(0) == 0)
        def _():
            dw_ref[...] = jnp.zeros_like(dw_ref)
            dd_ref[...] = jnp.zeros_like(dd_ref)

        dw_ref[...] += dw
        dd_ref[...] += jnp.sum(dy * u_ref[...], axis=0, keepdims=True)

    row = pl.BlockSpec((ts, W), lambda i: (i, 0))
    vec = pl.BlockSpec((1, W), lambda i: (0, 0))
    sq = pl.BlockSpec((W, W), lambda i: (0, 0))
    full = jax.ShapeDtypeStruct((S, W), f32)
    return pl.pallas_call(
        body, name="glu_bwd", grid=(S // ts,), in_specs=[row, row, vec, sq, row], out_specs=[row, row, sq, vec],
        out_shape=[full, full, jax.ShapeDtypeStruct((W, W), f32), jax.ShapeDtypeStruct((1, W), f32)],
        compiler_params=_cp(("arbitrary",)),
    )(ypre, u, d, wglu, dz)


def adamw(w, g, m, v, *, name, tr=256):
    R, C = w.shape
    tr = _row_tile(R, tr)

    def body(w_ref, g_ref, m_ref, v_ref, d_ref, nm_ref, nv_ref):
        gv = g_ref[...]
        nm = ADAM_B1 * m_ref[...] + (1.0 - ADAM_B1) * gv
        nv = ADAM_B2 * v_ref[...] + (1.0 - ADAM_B2) * jnp.square(gv)
        m_hat = nm / (1.0 - ADAM_B1 ** ADAM_STEP)
        v_hat = nv / (1.0 - ADAM_B2 ** ADAM_STEP)
        d_ref[...] = -ADAM_LR * (m_hat / (jnp.sqrt(v_hat) + ADAM_EPS) + ADAM_WD * w_ref[...])
        nm_ref[...] = nm
        nv_ref[...] = nv

    row = pl.BlockSpec((tr, C), lambda i: (i, 0))
    full = jax.ShapeDtypeStruct((R, C), f32)
    return pl.pallas_call(
        body, name=name, grid=(R // tr,), in_specs=[row] * 4, out_specs=[row] * 3, out_shape=[full] * 3,
        compiler_params=_cp(("parallel",)),
    )(w, g, m, v)


def add_half(g4, recv, c, *, name, tr=256):
    _, _, Rh, C = g4.shape
    tr = _row_tile(Rh, tr)

    def body(c_ref, a_ref, b_ref, o_ref):
        o_ref[...] = a_ref[...] + b_ref[...]

    grid_spec = pltpu.PrefetchScalarGridSpec(
        num_scalar_prefetch=1, grid=(N_CHIPS, Rh // tr),
        in_specs=[pl.BlockSpec((None, None, tr, C), lambda k, i, c_ref: (k, c_ref[0], i, 0)),
                  pl.BlockSpec((None, tr, C), lambda k, i, c_ref: (k, i, 0))],
        out_specs=pl.BlockSpec((None, tr, C), lambda k, i, c_ref: (k, i, 0)))
    return pl.pallas_call(body, name=name, grid_spec=grid_spec, out_shape=jax.ShapeDtypeStruct(recv.shape, f32),
                          compiler_params=_cp(("parallel", "parallel")))(c, g4, recv)


def sum_chips(p4, *, name, tr=256):
    _, Rh, C = p4.shape
    tr = _row_tile(Rh, tr)

    def body(a_ref, b_ref, c_ref, d_ref, o_ref):
        o_ref[...] = ((a_ref[...] + b_ref[...]) + c_ref[...]) + d_ref[...]

    spec = lambda k: pl.BlockSpec((None, tr, C), lambda i: (k, i, 0))
    return pl.pallas_call(
        body, name=name, grid=(Rh // tr,), in_specs=[spec(0), spec(1), spec(2), spec(3)],
        out_specs=pl.BlockSpec((tr, C), lambda i: (i, 0)), out_shape=jax.ShapeDtypeStruct((Rh, C), f32),
        compiler_params=_cp(("parallel",)),
    )(p4, p4, p4, p4)


def _place():
    return lax.axis_index("x"), lax.axis_index("y"), lax.axis_index("c")


def _other_chips(x, y):
    return [(1 - x, y), (x, 1 - y), (1 - x, 1 - y)]


def _chip_exchange(ins, outs, sems, scatter, start):
    if not ins:
        return
    send, recv, loc = sems
    x, y, c = _place()
    me = 2 * x + y
    for a in range(len(ins)):
        own = pltpu.make_async_copy(ins[a].at[me] if scatter else ins[a], outs[a].at[me], loc.at[a])
        own.start() if start else own.wait()
        for p, (px, py) in enumerate(_other_chips(x, y)):
            k = 2 * px + py
            cp = pltpu.make_async_remote_copy(
                src_ref=ins[a].at[k] if scatter else ins[a], dst_ref=outs[a].at[me if start else k],
                send_sem=send.at[3 * a + p], recv_sem=recv.at[3 * a + p], device_id=(px, py, c), device_id_type=MESH)
            cp.start() if start else cp.wait()


def _chip_exchange_args(arrs, scatter):
    n = len(arrs)
    shapes = [jax.ShapeDtypeStruct(a.shape if scatter else (N_CHIPS,) + a.shape, a.dtype) for a in arrs]
    sems = [pltpu.SemaphoreType.DMA((3 * n,)), pltpu.SemaphoreType.DMA((3 * n,)), pltpu.SemaphoreType.DMA((n,))]
    return shapes, sems if n else []


def _chip_exchange_call(arrs, scatter, name):
    n = len(arrs)

    def body(*refs):
        ins, outs, sems = refs[:n], refs[n:2 * n], refs[2 * n:]
        _chip_exchange(ins, outs, sems, scatter, True)
        _chip_exchange(ins, outs, sems, scatter, False)

    shapes, sems = _chip_exchange_args(arrs, scatter)
    return pl.pallas_call(
        body, name=name, in_specs=[ANY] * n, out_specs=[ANY] * n, out_shape=shapes, scratch_shapes=sems,
        compiler_params=pltpu.CompilerParams(has_side_effects=True),
    )(*arrs)


def allgather_chips(arrs, *, name):
    return _chip_exchange_call(arrs, False, name)


def sibling_swap(arrs, *, half, name):
    n = len(arrs)

    def body(*refs):
        ins, outs = refs[:n], refs[n:2 * n]
        send, recv = refs[2 * n:]
        x, y, c = _place()
        cps = []
        for a in range(n):
            src = ins[a].at[:, 1 - c] if half else ins[a]
            cp = pltpu.make_async_remote_copy(src_ref=src, dst_ref=outs[a], send_sem=send.at[a], recv_sem=recv.at[a],
                                              device_id=(x, y, 1 - c), device_id_type=MESH)
            cp.start()
            cps.append(cp)
        for cp in cps:
            cp.wait()

    def oshape(a):
        return jax.ShapeDtypeStruct((a.shape[0],) + a.shape[2:] if half else a.shape, a.dtype)

    return pl.pallas_call(
        body, name=name, in_specs=[ANY] * n, out_specs=[ANY] * n, out_shape=[oshape(a) for a in arrs],
        scratch_shapes=[pltpu.SemaphoreType.DMA((n,)), pltpu.SemaphoreType.DMA((n,))],
        compiler_params=pltpu.CompilerParams(has_side_effects=True),
    )(*arrs)


def chip_scatter(arrs, *, name):
    return _chip_exchange_call(arrs, True, name)


def _pad_cols(w):
    K = w.shape[0]
    w = w.reshape(K, -1, SB_HEAD_DIM)
    return jnp.pad(w, ((0, 0), (0, 0), (0, LANES - SB_HEAD_DIM))).reshape(K, -1)


def _unpad_cols(w):
    K = w.shape[0]
    return w.reshape(K, -1, LANES)[:, :, :SB_HEAD_DIM].reshape(K, -1)


def _pad_rows(w):
    N = w.shape[1]
    w = w.reshape(-1, SB_HEAD_DIM, N)
    return jnp.pad(w, ((0, 0), (0, LANES - SB_HEAD_DIM), (0, 0))).reshape(-1, N)


def _unpad_rows(w):
    N = w.shape[1]
    return w.reshape(-1, LANES, N)[:, :SB_HEAD_DIM, :].reshape(-1, N)


_PACK_ROWS = N_CHIPS * 2 * SUBLANES


def _pack(arrs):
    flat = jnp.concatenate([a.reshape(-1) for a in arrs])
    rows = -(-flat.shape[0] // LANES)
    rows = -(-rows // _PACK_ROWS) * _PACK_ROWS
    return jnp.pad(flat, (0, rows * LANES - flat.shape[0])).reshape(rows, LANES)


def _unpack(buf, shapes):
    flat = buf.reshape(-1)
    out, pos = [], 0
    for shp in shapes:
        size = 1
        for d in shp:
            size *= d
        out.append(flat[pos:pos + size].reshape(shp))
        pos += size
    return out


BIG = ("w_in", "ssm_w_glu", "w_out", "xa_w_q", "xa_w_kv", "xa_w_o", "w_up", "w_down")
SMALL = ("g_mix", "ssm_a_re", "ssm_a_im", "ssm_log_dt", "ssm_b_re", "ssm_b_im", "ssm_c_re", "ssm_c_im", "ssm_d",
         "sb_g_q", "sb_g_k", "g_out_ssm", "g_out_sb", "g_xa", "g_mem", "xa_g_q", "xa_g_k", "g_mlp")
WEIGHTS = ("g_mix", "w_in", "ssm_a_re", "ssm_a_im", "ssm_log_dt", "ssm_b_re", "ssm_b_im", "ssm_c_re", "ssm_c_im",
           "ssm_d", "ssm_w_glu", "sb_g_q", "sb_g_k", "g_out_ssm", "g_out_sb", "w_out", "g_xa", "g_mem", "xa_w_q",
           "xa_w_kv", "xa_g_q", "xa_g_k", "xa_w_o", "g_mlp", "w_up", "w_down")


def kernel(x, mem, g_mix, w_in, ssm_a_re, ssm_a_im, ssm_log_dt, ssm_b_re, ssm_b_im, ssm_c_re, ssm_c_im, ssm_d, ssm_w_glu, sb_g_q, sb_g_k, g_out_ssm, g_out_sb, w_out, g_xa, g_mem, xa_w_q, xa_w_kv, xa_g_q, xa_g_k, xa_w_o, g_mlp, w_up, w_down, loss_target, m_g_mix, m_w_in, m_ssm_a_re, m_ssm_a_im, m_ssm_log_dt, m_ssm_b_re, m_ssm_b_im, m_ssm_c_re, m_ssm_c_im, m_ssm_d, m_ssm_w_glu, m_sb_g_q, m_sb_g_k, m_g_out_ssm, m_g_out_sb, m_w_out, m_g_xa, m_g_mem, m_xa_w_q, m_xa_w_kv, m_xa_g_q, m_xa_g_k, m_xa_w_o, m_g_mlp, m_w_up, m_w_down, v_g_mix, v_w_in, v_ssm_a_re, v_ssm_a_im, v_ssm_log_dt, v_ssm_b_re, v_ssm_b_im, v_ssm_c_re, v_ssm_c_im, v_ssm_d, v_ssm_w_glu, v_sb_g_q, v_sb_g_k, v_g_out_ssm, v_g_out_sb, v_w_out, v_g_xa, v_g_mem, v_xa_w_q, v_xa_w_kv, v_xa_g_q, v_xa_g_k, v_xa_w_o, v_g_mlp, v_w_up, v_w_down):
    env = dict(locals())
    W = {n: env[n] for n in WEIGHTS}
    M1 = {n: env["m_" + n] for n in WEIGHTS}
    V2 = {n: env["v_" + n] for n in WEIGHTS}
    xs, mems, tgt = x[0], mem[0], loss_target[0]
    S, D = xs.shape
    G, P, C = SSM_GROUPS, SSM_STATE, SSM_GROUP
    GP = G * P
    SBW = SB_HEADS * SB_HEAD_DIM
    c_idx = lax.axis_index("c")

    big = dict(tn=1024, tk=1024)
    wide = dict(tm=1024, tn=1024, tk=2048)
    h0, (g_in,) = rms_norm(xs, g_mix, D, name="norm_x", exchange=([w_in[0].astype(bf16)], False))
    Wu = g_in[0]
    Wqkv = jnp.concatenate([_pad_cols(g_in[1]), _pad_cols(g_in[2]), _pad_cols(g_in[3])], axis=1)
    gq_pad, gk_pad = _pad_cols(sb_g_q), _pad_cols(sb_g_k)
    gosb_pad = _pad_cols(g_out_sb)
    a_re, a_im = ssm_a_re.reshape(1, GP), ssm_a_im.reshape(1, GP)
    bT_re = ssm_b_re[0].transpose(2, 0, 1).reshape(C, GP)
    bT_im = ssm_b_im[0].transpose(2, 0, 1).reshape(C, GP)
    cT_re = ssm_c_re[0].transpose(1, 0, 2).reshape(C, GP)
    cT_im = ssm_c_im[0].transpose(1, 0, 2).reshape(C, GP)
    s5_in = (a_re, a_im, ssm_log_dt, bT_re, bT_im, cT_re, cT_im)

    u = mm(h0, Wu, mode="nn", name="proj_u", tk=1024)
    shard = {n: W[n][0].astype(bf16) for n in BIG[1:]}
    qkv, (g_glu, g_out, g_xq, g_xkv, g_xo) = mm(
        h0, Wqkv, mode="nn", name="proj_qkv", tm=1024,
        exchange=([shard[n] for n in ("ssm_w_glu", "w_out", "xa_w_q", "xa_w_kv", "xa_w_o")], False), **big)
    qn, kn, vb = qkv_prep(qkv, gq_pad, gk_pad)
    o, (g_down,) = sb_fwd(qn, kn, vb, ([shard["w_down"]], False))
    Wglu = g_glu.reshape(-1, g_glu.shape[-1])
    Wout = g_out.reshape(-1, g_out.shape[-1])
    Wo_ssm, Wo_sb = Wout[:SBW], _pad_rows(Wout[SBW:])
    Wxq = g_xq.reshape(-1, g_xq.shape[-1])
    Wxkv = g_xkv.reshape(-1, g_xkv.shape[-1])
    Wxo = g_xo.transpose(1, 0, 2).reshape(g_xo.shape[1], -1)
    abr, abi, Bm, Cm = s5_prep(*s5_in)
    n_pos = S // SCAN_CHUNKS
    u_il = row_shuffle(u, n_pos, SCAN_CHUNKS, name="u_interleave", out_dtype=bf16)
    st, (g_up,) = scan_fwd(u_il, Bm, abr, abi, ([shard["w_up"]], False))
    ypre0_il = mm(st, Cm, mode="nt", name="s5_y", a_shards=2, b_shards=2, tm=1024, tk=2048)
    Wup = g_up.transpose(1, 0, 2).reshape(g_up.shape[1], -1)
    Wdown = g_down.reshape(-1, g_down.shape[-1])
    ypre0 = row_shuffle(ypre0_il, SCAN_CHUNKS, n_pos, name="y_token_order")
    ypre, z, zn = glu_fwd(ypre0, u, ssm_d, Wglu, g_out_ssm)
    on = rms_norm(o, gosb_pad, SBW, name="norm_o")
    x1a = mm(zn, Wo_ssm, mode="nn", name="out_ssm", epi="add", aux=xs, tn=1024)
    x1 = mm(on, Wo_sb, mode="nn", name="out_sb", epi="add", aux=x1a, **big)
    h1 = rms_norm(x1, g_xa, D, name="norm_x1")
    qx = mm(h1, Wxq, mode="nn", name="xa_q", tk=1024)
    memn = rms_norm(mems, g_mem, D, name="norm_mem")
    kv = mm(memn, Wxkv, mode="nn", name="xa_kv", **big)
    ox = xa_fwd(qx, kv, xa_g_q, xa_g_k)
    x2 = mm(ox, Wxo, mode="nn", name="xa_o", epi="add", aux=x1, tn=1024)
    h2 = rms_norm(x2, g_mlp, D, name="norm_x2")
    act = mm(h2, Wup, mode="nn", name="mlp_up", out_dtype=bf16, tm=1024, tn=2048, tk=1024)
    dx3, loss_part, dx3b = mm(act, Wdown, mode="nn", name="mlp_down", pro="relu2", epi="add", aux=x2,
                              loss_target=tgt, tm=1024, **big)
    loss = lax.psum(loss_part[0, 0], ("x", "y", "c"))

    dact = mm(dx3b, Wdown, mode="nt", name="d_act", epi="mul2relu", aux=act, out_dtype=bf16, tm=1024, tn=2048,
              tk=1024)
    dWdown = mm(act, dx3b, mode="tn", name="dw_down", pro="relu2", **wide)
    dWup = mm(h2, dact, mode="tn", name="dw_up", out_shards=N_CHIPS, **wide)
    dx2, dg_mlp, dx2b = mm(dact, Wup, mode="nt", name="d_h2", norm_bwd=(x2, g_mlp, dx3, D), tm=1024, **big)
    dox = mm(dx2b, Wxo, mode="nt", name="d_ox", out_dtype=bf16, tk=1024)
    dWxo = mm(ox, dx2b, mode="tn", name="dw_xo", out_shards=N_CHIPS, tk=1024)
    dqx, dkv, dg_xq, dg_xk = xa_bwd(qx, kv, xa_g_q, xa_g_k, dox)
    dWxq = mm(h1, dqx, mode="tn", name="dw_xq", tm=1024, tk=1024)
    dx1, dg_xa, dx1b = mm(dqx, Wxq, mode="nt", name="d_h1", norm_bwd=(x1, g_xa, dx2, D), tn=1024)
    dWxkv = mm(memn, dkv, mode="tn", name="dw_xkv", tm=1024, tn=1024)
    dmemn = mm(dkv, Wxkv, mode="nt", name="d_memn", **big)
    _, dg_mem = rms_bwd(mems, g_mem, dmemn, None, D, name="rms_bwd_mem")
    dyn_ssm = mm(dx1b, Wo_ssm, mode="nt", name="d_yn_ssm", tk=1024)
    dyn_sb = mm(dx1b, Wo_sb, mode="nt", name="d_yn_sb", **big)
    dWo_ssm = mm(zn, dx1b, mode="tn", name="dw_out_ssm", **big)
    dWo_sb = mm(on, dx1b, mode="tn", name="dw_out_sb", tm=1024, **big)
    dz, dg_os = rms_bwd(z, g_out_ssm, dyn_ssm, None, SBW, name="rms_bwd_ssm")
    do, dg_osb = rms_bwd(o, gosb_pad, dyn_sb, None, SBW, name="rms_bwd_sb")
    c_arr = c_idx.astype(jnp.int32).reshape(1)

    def sibling_sums(grads, names, tag):
        g4 = [g.reshape(N_CHIPS, 2, g.shape[1] // 2, g.shape[2]) for g in grads]
        from_sib = sibling_swap(g4, half=True, name="grad_to_sibling_" + tag)
        return [add_half(a, b, c_arr, name="add_sibling_" + n) for a, b, n in zip(g4, from_sib, names)]

    early = ("xa_w_q", "xa_w_kv", "xa_w_o", "w_up", "w_down")
    early_g = [dWxq.reshape(N_CHIPS, -1, dWxq.shape[1]), dWxkv.reshape(N_CHIPS, -1, dWxkv.shape[1]), dWxo, dWup,
               dWdown.reshape(N_CHIPS, -1, D)]
    pair = sibling_sums(early_g, early, "early")
    dqn, dkn, dv, parts_mlp = sb_bwd(qn, kn, vb, do, (pair[3:], True))
    dqkv, dg_q, dg_k, parts_xa = qkv_bwd(qkv, gq_pad, gk_pad, dqn, dkn, dv, (pair[:3], True))
    dypre, du_skip, dWglu, dd = glu_bwd(ypre, u, ssm_d, Wglu, dz)
    mid = ("ssm_w_glu", "w_out")
    mid_g = [dWglu.reshape(N_CHIPS, -1, dWglu.shape[1]),
             jnp.concatenate([dWo_ssm, _unpad_rows(dWo_sb)]).reshape(N_CHIPS, -1, D)]
    dypre_il = row_shuffle(dypre, n_pos, SCAN_CHUNKS, name="dy_interleave", out_dtype=bf16)
    dCm = mm(dypre_il, st, mode="tn", name="d_cmat", b_shards=2, out_shards=2, **wide)
    gst, dabr, dabi, parts_mid = scan_bwd(dypre_il, Cm, st, abr, abi, (sibling_sums(mid_g, mid, "mid"), True))
    dBm = mm(u_il, gst, mode="tn", name="d_bmat", b_shards=2, out_shards=2, **wide)
    du_il = mm(gst, Bm, mode="nt", name="d_u", a_shards=2, b_shards=2, tm=1024, tk=2048)
    mine = {n: sum_chips(p, name="sum_chips_" + n)
            for n, p in zip(early + mid, [*parts_xa, *parts_mlp, *parts_mid])}
    du = row_shuffle(du_il, SCAN_CHUNKS, n_pos, name="du_token_order", add=du_skip, out_dtype=bf16)
    s5_g = s5_prep_bwd(*s5_in, dabr, dabi, dBm, dCm)
    dWu = mm(h0, du, mode="tn", name="dw_u", tm=1024, tk=1024)
    dWqkv = mm(h0, dqkv, mode="tn", name="dw_qkv", **wide)
    HW = SB_HEADS * LANES
    w_in_g = jnp.stack([dWu, _unpad_cols(dWqkv[:, :HW]), _unpad_cols(dWqkv[:, HW:2 * HW]),
                        _unpad_cols(dWqkv[:, 2 * HW:])])
    dh0a = mm(du, Wu, mode="nt", name="d_h0_u", tn=1024)
    (dx, dg_mix, _), parts_in = mm(dqkv, Wqkv, mode="nt", name="d_h0_qkv", epi="add", aux=dh0a,
                                   norm_bwd=(xs, g_mix, dx1, D), tm=1024,
                                   exchange=(sibling_sums([w_in_g], ("w_in",), "w_in"), True), **big)
    mine["w_in"] = sum_chips(parts_in[0], name="sum_chips_w_in")

    late = ("small",)
    late_g = []
    da_re, da_im, dldt, dbT_re, dbT_im, dcT_re, dcT_im = s5_g
    small_g = {
        "g_mix": dg_mix, "ssm_a_re": da_re, "ssm_a_im": da_im, "ssm_log_dt": dldt,
        "ssm_b_re": dbT_re.reshape(C, G, P).transpose(1, 2, 0), "ssm_b_im": dbT_im.reshape(C, G, P).transpose(1, 2, 0),
        "ssm_c_re": dcT_re.reshape(C, G, P).transpose(1, 0, 2), "ssm_c_im": dcT_im.reshape(C, G, P).transpose(1, 0, 2),
        "ssm_d": dd, "sb_g_q": dg_q[:, :SB_HEAD_DIM], "sb_g_k": dg_k[:, :SB_HEAD_DIM], "g_out_ssm": dg_os,
        "g_out_sb": _unpad_cols(dg_osb), "g_xa": dg_xa, "g_mem": dg_mem, "xa_g_q": dg_xq, "xa_g_k": dg_xk,
        "g_mlp": dg_mlp,
    }
    late_g.append(_pack([small_g[n] for n in SMALL]).reshape(N_CHIPS, -1, LANES))

    parts_late = chip_scatter(sibling_sums(late_g, late, "late"), name="grad_to_chips_late")
    mine.update({n: sum_chips(p, name="sum_chips_" + n) for n, p in zip(late, parts_late)})
    mine = [mine[n] for n in list(BIG) + ["small"]]
    other = sibling_swap(mine, half=False, name="grad_half_to_sibling")
    shard = [jnp.where(c_idx == 0, jnp.concatenate([a, b]), jnp.concatenate([b, a])) for a, b in zip(mine, other)]
    small_all = allgather_chips([shard[-1]], name="gather_small")[0]
    small_red = small_all.reshape(-1, LANES)

    out = {}
    for n, gs in zip(BIG, shard[:-1]):
        shp = W[n].shape
        w2, m2, v2 = (t.reshape(gs.shape) for t in (W[n], M1[n], V2[n]))
        d, nm, nv = adamw(w2, gs, m2, v2, name="adamw_" + n)
        out[n] = tuple(t.reshape(shp) for t in (gs, d, nm, nv))
    shapes = [W[n].shape for n in SMALL]
    d, nm, nv = adamw(_pack([W[n] for n in SMALL]), small_red, _pack([M1[n] for n in SMALL]),
                      _pack([V2[n] for n in SMALL]), name="adamw_small")
    for n, gs, dd_, mm_, vv_ in zip(SMALL, _unpack(small_red, shapes), _unpack(d, shapes), _unpack(nm, shapes),
                                    _unpack(nv, shapes)):
        out[n] = (gs, dd_, mm_, vv_)
    res = [loss, dx[None]]
    for kind in range(4):
        res += [out[n][kind] for n in WEIGHTS]
    return tuple(res)
```

```python
import jax
import jax.numpy as jnp
from jax import lax
from jax.experimental import pallas as pl
from jax.experimental.pallas import tpu as pltpu

f32 = jnp.float32
bf16 = jnp.bfloat16

NORM_EPS = 1e-6
SSM_GROUPS = 32
SSM_GROUP = 16
SSM_STATE = 64
SB_HEADS = 8
SB_HEAD_DIM = 64
XA_HEADS = 4
XA_HEAD_DIM = 128
LANES = 128
SUBLANES = 8
N_CHIPS = 4
ADAM_LR = 0.001
ADAM_B1 = 0.9
ADAM_B2 = 0.999
ADAM_EPS = 1e-08
ADAM_WD = 0.01
ADAM_STEP = 10
VMEM_LIMIT = 56 * 1024 * 1024
MESH = pl.DeviceIdType.MESH
ANY = pl.BlockSpec(memory_space=pl.ANY)


def _cp(sem=None):
    return pltpu.CompilerParams(dimension_semantics=sem, vmem_limit_bytes=VMEM_LIMIT)


def _tile(n, pref):
    if n <= pref:
        return n
    t = (pref // LANES) * LANES
    while t > LANES and n % t:
        t -= LANES
    assert n % t == 0, (n, pref)
    return t


def _row_tile(n, pref):
    if n <= pref:
        return n
    t = (pref // SUBLANES) * SUBLANES
    while n % t:
        t -= SUBLANES
    return t


def _dot(a, b, dims):
    return lax.dot_general(a.astype(bf16), b.astype(bf16), (dims, ((), ())), preferred_element_type=f32)


_NN = ((1,), (0,))
_NT = ((1,), (1,))
_TN = ((0,), (0,))


@jax.custom_vjp
def bdot_nn(a, b):
    return _dot(a, b, _NN)


def _bdot_nn_fwd(a, b):
    return _dot(a, b, _NN), (a, b)


def _bdot_nn_bwd(res, g):
    a, b = res
    return _dot(g, b, _NT), _dot(a, g, _TN)


bdot_nn.defvjp(_bdot_nn_fwd, _bdot_nn_bwd)


@jax.custom_vjp
def bdot_nt(a, b):
    return _dot(a, b, _NT)


def _bdot_nt_fwd(a, b):
    return _dot(a, b, _NT), (a, b)


def _bdot_nt_bwd(res, g):
    a, b = res
    return _dot(g, b, _NN), _dot(g, a, _TN)


bdot_nt.defvjp(_bdot_nt_fwd, _bdot_nt_bwd)


def _rms(x, g, denom):
    r = lax.rsqrt(jnp.sum(x * x, axis=-1, keepdims=True) * (1.0 / denom) + NORM_EPS)
    return x * r * g


def _opspec(block, row_of, col_of, shards, ncol_tiles):
    if shards == 1:
        return pl.BlockSpec(block, lambda i, j, k: (row_of(i, j, k), col_of(i, j, k)))
    per = ncol_tiles // shards
    return pl.BlockSpec((None,) + block,
                        lambda i, j, k: (col_of(i, j, k) // per, row_of(i, j, k), col_of(i, j, k) % per))


def _call_with_exchange(body, args, exchange, *, name, grid, in_specs, out_specs, out_shape, scratch_shapes=()):
    xs, scatter = exchange
    n, n_in, n_out, n_scr = len(xs), len(in_specs), len(out_specs), len(scratch_shapes)
    x_shapes, sems = _chip_exchange_args(xs, scatter)

    def wrapped(*refs):
        ins, x_ins = refs[:n_in], refs[n_in:n_in + n]
        outs, x_outs = refs[n_in + n:n_in + n + n_out], refs[n_in + n + n_out:n_in + 2 * n + n_out]
        scratch, x_sems = refs[n_in + 2 * n + n_out:n_in + 2 * n + n_out + n_scr], refs[n_in + 2 * n + n_out + n_scr:]
        first, last = True, True
        for d, steps in enumerate(grid):
            first = first & (pl.program_id(d) == 0)
            last = last & (pl.program_id(d) == steps - 1)

        @pl.when(first)
        def _():
            _chip_exchange(x_ins, x_outs, x_sems, scatter, True)

        body(*ins, *outs, *scratch)

        @pl.when(last)
        def _():
            _chip_exchange(x_ins, x_outs, x_sems, scatter, False)

    res = pl.pallas_call(
        wrapped, name=name, grid=grid, in_specs=list(in_specs) + [ANY] * n, out_specs=list(out_specs) + [ANY] * n,
        out_shape=list(out_shape) + x_shapes, scratch_shapes=list(scratch_shapes) + sems,
        compiler_params=_cp(("arbitrary",) * len(grid)),
    )(*args, *xs)
    return res[:n_out], res[n_out:]


def mm(a, b, *, mode, name, tm=512, tn=512, tk=512, pro="none", epi="none", aux=None,
       out_dtype=f32, a_shards=1, b_shards=1, out_shards=1, exchange=None, norm_bwd=None, loss_target=None):
    ar, ac = a.shape[-2], a.shape[-1] * a_shards
    br, bc = b.shape[-2], b.shape[-1] * b_shards
    if mode == "nn":
        M, K, N = ar, ac, bc
        assert br == K
    elif mode == "nt":
        M, K, N = ar, ac, br
        assert bc == K
    else:
        M, K, N = ac, ar, bc
        assert br == K
    tm, tn, tk = _tile(M, tm), _tile(N, tn), _tile(K, tk)
    if a_shards > 1:
        if mode == "tn":
            tm = _tile(ac // a_shards, tm)
        else:
            tk = _tile(ac // a_shards, tk)
    if b_shards > 1:
        if mode == "nt":
            tk = _tile(bc // b_shards, tk)
        else:
            tn = _tile(bc // b_shards, tn)
    if out_shards > 1:
        tn = _tile(N // out_shards, tn)
    nm, nn_, nk = M // tm, N // tn, K // tk
    I = lambda i, j, k: i
    J = lambda i, j, k: j
    Kk = lambda i, j, k: k
    if mode == "nn":
        a_spec = _opspec((tm, tk), I, Kk, a_shards, nk)
        b_spec = _opspec((tk, tn), Kk, J, b_shards, nn_)
        dims = _NN
    elif mode == "nt":
        a_spec = _opspec((tm, tk), I, Kk, a_shards, nk)
        b_spec = _opspec((tn, tk), J, Kk, b_shards, nk)
        dims = _NT
    else:
        a_spec = _opspec((tk, tm), Kk, I, a_shards, nm)
        b_spec = _opspec((tk, tn), Kk, J, b_shards, nn_)
        dims = _TN
    in_specs = [a_spec, b_spec]
    args = [a, b]
    tile = pl.BlockSpec((tm, tn), lambda i, j, k: (i, j))
    if epi != "none":
        in_specs.append(tile)
        args.append(aux)
    if out_shards == 1:
        out_spec = tile
        out_shape = jax.ShapeDtypeStruct((M, N), out_dtype)
    else:
        per = nn_ // out_shards
        out_spec = pl.BlockSpec((None, tm, tn), lambda i, j, k: (j // per, i, j % per))
        out_shape = jax.ShapeDtypeStruct((out_shards, M, N // out_shards), out_dtype)
    out_specs, out_shapes = [out_spec], [out_shape]
    if norm_bwd is not None:
        nx, ng, nres, denom = norm_bwd
        assert tn == N and out_shards == 1 and out_dtype == f32
        vec = pl.BlockSpec((1, tn), lambda i, j, k: (0, 0))
        in_specs += [tile, vec, tile]
        args += [nx, ng, nres]
        out_specs += [vec, tile]
        out_shapes += [jax.ShapeDtypeStruct((1, N), f32), jax.ShapeDtypeStruct((M, N), bf16)]
    if loss_target is not None:
        assert tn == N and out_shards == 1 and out_dtype == f32 and norm_bwd is None
        in_specs.append(tile)
        args.append(loss_target)
        out_specs += [pl.BlockSpec((1, 1), lambda i, j, k: (0, 0)), tile]
        out_shapes += [jax.ShapeDtypeStruct((1, 1), f32), jax.ShapeDtypeStruct((M, N), bf16)]
    n_in = len(in_specs)

    def body(*refs):
        a_ref, b_ref = refs[0], refs[1]
        aux_ref = refs[2] if epi != "none" else None
        o_ref, acc_ref = refs[n_in], refs[-1]
        i, k = pl.program_id(0), pl.program_id(2)

        @pl.when(k == 0)
        def _():
            acc_ref[...] = jnp.zeros_like(acc_ref)

        av = a_ref[...]
        if pro == "relu2":
            av = jnp.square(jnp.maximum(av.astype(f32), 0.0))
        acc_ref[...] += _dot(av, b_ref[...], dims)

        @pl.when(k == nk - 1)
        def _():
            res = acc_ref[...]
            if epi == "add":
                res = res + aux_ref[...].astype(f32)
            elif epi == "mul2relu":
                res = res * (2.0 * jnp.maximum(aux_ref[...].astype(f32), 0.0))
            if loss_target is not None:
                l_ref, twin_ref = refs[n_in + 1], refs[n_in + 2]
                err = res - refs[n_in - 1][...]
                dy = err * (1.0 / N)
                o_ref[...] = dy
                twin_ref[...] = dy.astype(bf16)

                @pl.when(i == 0)
                def _():
                    l_ref[...] = jnp.zeros_like(l_ref)

                rows = jnp.sum(err * err, axis=1, keepdims=True) * (1.0 / N)
                l_ref[...] += 0.5 * jnp.sum(rows, axis=0, keepdims=True)
            elif norm_bwd is None:
                o_ref[...] = res.astype(out_dtype)
            else:
                x_ref, g_ref, res_ref = refs[n_in - 3:n_in]
                dg_ref, twin_ref = refs[n_in + 1], refs[n_in + 2]
                _, vjp = jax.vjp(lambda xv, gv: _rms(xv, gv, denom), x_ref[...], g_ref[...])
                dx, dg = vjp(res)
                dx = dx + res_ref[...]
                o_ref[...] = dx
                twin_ref[...] = dx.astype(bf16)

                @pl.when(i == 0)
                def _():
                    dg_ref[...] = jnp.zeros_like(dg_ref)

                dg_ref[...] += dg

    acc = [pltpu.VMEM((tm, tn), f32)]
    single = norm_bwd is None and loss_target is None
    if exchange is not None:
        outs, moved = _call_with_exchange(body, args, exchange, name=name, grid=(nm, nn_, nk), in_specs=in_specs,
                                          out_specs=out_specs, out_shape=out_shapes, scratch_shapes=acc)
        return (outs[0] if single else tuple(outs)), moved
    outs = pl.pallas_call(
        body, name=name, grid=(nm, nn_, nk), in_specs=in_specs, out_specs=out_specs, out_shape=out_shapes,
        scratch_shapes=acc, compiler_params=_cp(("parallel" if single else "arbitrary", "parallel", "arbitrary")),
    )(*args)
    return outs[0] if single else tuple(outs)


def rms_norm(x, g, denom, *, name, ts=512, exchange=None):
    S, D = x.shape
    ts = _tile(S, ts)

    def body(x_ref, g_ref, h_ref):
        h_ref[...] = _rms(x_ref[...], g_ref[...], denom).astype(bf16)

    row = pl.BlockSpec((ts, D), lambda i: (i, 0))
    kw = dict(name=name, grid=(S // ts,), in_specs=[row, pl.BlockSpec((1, D), lambda i: (0, 0))])
    if exchange is not None:
        (h,), moved = _call_with_exchange(body, (x, g), exchange, out_specs=[row],
                                          out_shape=[jax.ShapeDtypeStruct((S, D), bf16)], **kw)
        return h, moved
    return pl.pallas_call(body, out_specs=row, out_shape=jax.ShapeDtypeStruct((S, D), bf16),
                          compiler_params=_cp(("parallel",)), **kw)(x, g)


def rms_bwd(x, g, dy, res, denom, *, name, ts=512, twin=False):
    S, D = x.shape
    ts = _tile(S, ts)
    has_res = res is not None

    def body(*refs):
        x_ref, g_ref, dy_ref = refs[:3]
        outs = refs[4:] if has_res else refs[3:]
        _, vjp = jax.vjp(lambda xv, gv: _rms(xv, gv, denom), x_ref[...], g_ref[...])
        dx, dg = vjp(dy_ref[...])
        if has_res:
            dx = dx + refs[3][...]
        outs[0][...] = dx
        if twin:
            outs[2][...] = dx.astype(bf16)
        dg_ref = outs[1]

        @pl.when(pl.program_id(0) == 0)
        def _():
            dg_ref[...] = jnp.zeros_like(dg_ref)

        dg_ref[...] += dg

    row = pl.BlockSpec((ts, D), lambda i: (i, 0))
    vec = pl.BlockSpec((1, D), lambda i: (0, 0))
    in_specs = [row, vec, row] + ([row] if has_res else [])
    args = [x, g, dy] + ([res] if has_res else [])
    return pl.pallas_call(
        body, name=name, grid=(S // ts,), in_specs=in_specs, out_specs=[row, vec] + ([row] if twin else []),
        out_shape=[jax.ShapeDtypeStruct((S, D), f32), jax.ShapeDtypeStruct((1, D), f32)]
        + ([jax.ShapeDtypeStruct((S, D), bf16)] if twin else []),
        compiler_params=_cp(("arbitrary",)),
    )(*args)


LOG2E = 1.4426950408889634
LN2 = 0.6931471805599453


def _qk_fn(q, k, gq, gk):
    qs, ks = [], []
    for h in range(SB_HEADS):
        sl = slice(h * LANES, (h + 1) * LANES)
        qs.append(_rms(q[:, sl], gq, SB_HEAD_DIM) * (SB_HEAD_DIM ** -0.5 * LOG2E))
        ks.append(_rms(k[:, sl], gk, SB_HEAD_DIM))
    return jnp.concatenate(qs, axis=1), jnp.concatenate(ks, axis=1)


def qkv_prep(qkv, gq, gk, *, ts=512):
    S = qkv.shape[0]
    W = SB_HEADS * LANES
    ts = _tile(S, ts)

    def body(q_ref, k_ref, v_ref, gq_ref, gk_ref, qn_ref, kn_ref, vb_ref):
        qn, kn = _qk_fn(q_ref[...], k_ref[...], gq_ref[...], gk_ref[...])
        qn_ref[...] = qn.astype(bf16)
        kn_ref[...] = kn.astype(bf16)
        vb_ref[...] = v_ref[...].astype(bf16)

    out = jax.ShapeDtypeStruct((S, W), bf16)
    gspec = pl.BlockSpec((1, LANES), lambda i: (0, 0))
    ospec = pl.BlockSpec((ts, W), lambda i: (i, 0))
    col = lambda c: pl.BlockSpec((ts, W), lambda i: (i, c))
    return pl.pallas_call(
        body, name="qkv_prep", grid=(S // ts,), in_specs=[col(0), col(1), col(2), gspec, gspec],
        out_specs=[ospec, ospec, ospec], out_shape=[out, out, out], compiler_params=_cp(("parallel",)),
    )(qkv, qkv, qkv, gq, gk)


def qkv_bwd(qkv, gq, gk, dqn, dkn, dv, exchange, *, ts=512):
    S = qkv.shape[0]
    W = SB_HEADS * LANES
    ts = _tile(S, ts)

    def body(q_ref, k_ref, gq_ref, gk_ref, dqn_ref, dkn_ref, dv_ref, o_ref, dgq_ref, dgk_ref):
        _, vjp = jax.vjp(_qk_fn, q_ref[...], k_ref[...], gq_ref[...], gk_ref[...])
        dq, dk, dgq, dgk = vjp((dqn_ref[...] * LN2, dkn_ref[...] * LN2))
        o_ref[:, 0:W] = dq.astype(bf16)
        o_ref[:, W:2 * W] = dk.astype(bf16)
        o_ref[:, 2 * W:3 * W] = dv_ref[...].astype(bf16)

        @pl.when(pl.program_id(0) == 0)
        def _():
            dgq_ref[...] = jnp.zeros_like(dgq_ref)
            dgk_ref[...] = jnp.zeros_like(dgk_ref)

        dgq_ref[...] += dgq
        dgk_ref[...] += dgk

    gspec = pl.BlockSpec((1, LANES), lambda i: (0, 0))
    row = pl.BlockSpec((ts, W), lambda i: (i, 0))
    col = lambda c: pl.BlockSpec((ts, W), lambda i: (i, c))
    (dqkv, dgq, dgk), moved = _call_with_exchange(
        body, (qkv, qkv, gq, gk, dqn, dkn, dv), exchange, name="qkv_bwd", grid=(S // ts,),
        in_specs=[col(0), col(1), gspec, gspec, row, row, row],
        out_specs=[pl.BlockSpec((ts, 3 * W), lambda i: (i, 0)), gspec, gspec],
        out_shape=[jax.ShapeDtypeStruct((S, 3 * W), bf16), jax.ShapeDtypeStruct((1, LANES), f32),
                   jax.ShapeDtypeStruct((1, LANES), f32)])
    return dqkv, dgq, dgk, moved


def _sb_weights(q, ks, R, masked, row, col, UU):
    ls = [_dot(q, k, _NT) for k in ks]
    lbs, lm0s, cats = [], [], []
    for l, diag in zip(ls, masked):
        neg_abs = pltpu.bitcast(pltpu.bitcast(l, jnp.uint32) | jnp.uint32(0x80000000), f32)
        lp = jnp.log2(1.0 + jnp.exp2(neg_abs))
        lb = jnp.minimum(l, 0.0) - lp
        lm = lb - l
        if diag:
            lm = jnp.where(col < row, lm, 0.0)
        hi = lm.astype(bf16)
        lo = (lm - hi.astype(f32)).astype(bf16)
        lbs.append(lb)
        lm0s.append(lm[:, 0:1])
        cats.append(jnp.concatenate([hi, lo], axis=1))
    sums = [_dot(c, UU, _NN) for c in cats]
    ws = []
    for lb, lm0, A, diag in zip(lbs, lm0s, sums, masked):
        w = jnp.exp2(lb + (A + R))
        if diag:
            w = jnp.where(col < row, w, 0.0)
        R = R + (A[:, 0:1] + lm0)
        ws.append(w)
    return lbs, ws, R


def _tri2(tk):
    r = lax.broadcasted_iota(jnp.int32, (2 * tk, tk), 0)
    r = jnp.where(r >= tk, r - tk, r)
    c = lax.broadcasted_iota(jnp.int32, (2 * tk, tk), 1)
    return (r > c).astype(bf16)


SB_GROUP = 8


SB_ALL_ZERO_BELOW = -160.0


def _sweep(i, blocks_of, carry, descending, right_sum=None, ran=None):
    G = SB_GROUP
    n = jnp.maximum(i - 1, 0)
    rem, full = n % G, n // G
    asc = lambda js: js if descending else js[::-1]

    def first_group(c):
        one = lambda c: blocks_of([i], c, [True])
        two = lambda c: blocks_of(asc([i, i - 1]), c, asc([True, False]))
        return lax.cond(i >= 1, two, one, c)

    def body(p, c):
        return blocks_of(asc([i - 2 - p * G - u for u in range(G)]), c, [False] * G)

    def left_over(r):
        return lambda c: blocks_of(asc([r - 1 - u for u in range(r)]), c, [False] * r) if r else c

    if descending:
        carry = first_group(carry)
        alive = lambda c: jnp.max(right_sum(c)) > SB_ALL_ZERO_BELOW
        bodies, carry = lax.while_loop(lambda s: (s[0] < full) & alive(s[1]),
                                       lambda s: (s[0] + 1, body(s[0], s[1])), (jnp.int32(0), carry))
        tail = (bodies == full) & alive(carry)
        carry = lax.switch(jnp.where(tail, rem, 0), [left_over(r) for r in range(G)], carry)
        return carry, (bodies, tail)
    bodies, tail = ran
    carry = lax.switch(jnp.where(tail, rem, 0), [left_over(r) for r in range(G)], carry)
    carry = lax.fori_loop(0, bodies, lambda t, c: body(bodies - 1 - t, c), carry)
    return first_group(carry)


def sb_fwd(qn, kn, vb, exchange, *, tq=256):
    S, W = qn.shape
    H = W // LANES
    tq = _tile(S, tq)
    tk = tq
    nq = S // tq

    def body(q_ref, k_ref, v_ref, o_ref, uu_s):
        i = pl.program_id(1)

        @pl.when((pl.program_id(0) == 0) & (i == 0))
        def _():
            uu_s[...] = _tri2(tk)

        q = q_ref[...]
        row = lax.broadcasted_iota(jnp.int32, (tq, tk), 0)
        col = lax.broadcasted_iota(jnp.int32, (tq, tk), 1)
        UU = uu_s[...]

        def blocks(js, c, masked):
            rows = [pl.ds(pl.multiple_of(j * tk, tk), tk) for j in js]
            _, ws, R = _sb_weights(q, [k_ref[r, :] for r in rows], c[0], masked, row, col, UU)
            acc = c[1]
            for w, r in zip(ws, rows):
                acc = acc + _dot(w, v_ref[r, :], _NN)
            return R, acc

        c, _ = _sweep(i, blocks, (jnp.zeros((tq, 1), f32), jnp.zeros((tq, LANES), f32)), True, lambda c: c[0])
        o_ref[...] = c[1]

    qspec = pl.BlockSpec((tq, LANES), lambda h, i: (i, h))
    kspec = pl.BlockSpec((S, LANES), lambda h, i: (0, h))
    (o,), moved = _call_with_exchange(
        body, (qn, kn, vb), exchange, name="sb_fwd", grid=(H, nq), in_specs=[qspec, kspec, kspec], out_specs=[qspec],
        out_shape=[jax.ShapeDtypeStruct((S, W), f32)], scratch_shapes=[pltpu.VMEM((2 * tk, tk), bf16)])
    return o, moved


def sb_bwd(qn, kn, vb, do, exchange, *, tq=256):
    S, W = qn.shape
    H = W // LANES
    tq = _tile(S, tq)
    tk = tq
    nq = S // tq

    def body(q_ref, k_ref, v_ref, do_ref, dq_ref, dk_ref, dv_ref, dz_s, beta_s, uu_s, ue_s):
        i = pl.program_id(1)
        row = lax.broadcasted_iota(jnp.int32, (tq, tk), 0)
        col = lax.broadcasted_iota(jnp.int32, (tq, tk), 1)

        @pl.when((pl.program_id(0) == 0) & (i == 0))
        def _():
            uu_s[...] = _tri2(tk)
            ue_s[...] = (row < col).astype(bf16)

        @pl.when(i == 0)
        def _():
            dk_ref[...] = jnp.zeros_like(dk_ref)
            dv_ref[...] = jnp.zeros_like(dv_ref)

        q = q_ref[...]
        dob = do_ref[...].astype(bf16)
        UU = uu_s[...]
        Ue = ue_s[...]

        def sweep1(js, R, masked):
            rows = [pl.ds(pl.multiple_of(j * tk, tk), tk) for j in js]
            dws = [_dot(dob, v_ref[r, :], _NT) for r in rows]
            lbs, ws, R = _sb_weights(q, [k_ref[r, :] for r in rows], R, masked, row, col, UU)
            for j, lb, w, dw in zip(js, lbs, ws, dws):
                dz_s[j] = (dw * w).astype(bf16)
                beta_s[j] = jnp.exp2(lb).astype(bf16)
            for r, w in zip(rows, ws):
                dv_ref[r, :] += _dot(w, dob, _TN)
            return R

        _, ran = _sweep(i, sweep1, jnp.zeros((tq, 1), f32), True, lambda R: R)

        def sweep2(js, c, masked):
            rows = [pl.ds(pl.multiple_of(j * tk, tk), tk) for j in js]
            dzbs = [dz_s[j] for j in js]
            sums = [_dot(dzb, Ue, _NN) for dzb in dzbs]
            Lz, dq = c
            dlbs = []
            for j, dzb, Cz, diag in zip(js, dzbs, sums, masked):
                dz = dzb.astype(f32)
                dl = dz - beta_s[j].astype(f32) * (dz + (Cz + Lz))
                if diag:
                    dl = jnp.where(col < row, dl, 0.0)
                Lz = Lz + (Cz[:, tk - 1:tk] + dz[:, tk - 1:tk])
                dlbs.append(dl.astype(bf16))
            for r, dlb in zip(rows, dlbs):
                dq = dq + _dot(dlb, k_ref[r, :], _NN)
            for r, dlb in zip(rows, dlbs):
                dk_ref[r, :] += _dot(dlb, q, _TN)
            return Lz, dq

        c = _sweep(i, sweep2, (jnp.zeros((tq, 1), f32), jnp.zeros((tq, LANES), f32)), False, ran=ran)
        dq_ref[...] = c[1]

    qspec = pl.BlockSpec((tq, LANES), lambda h, i: (i, h))
    kspec = pl.BlockSpec((S, LANES), lambda h, i: (0, h))
    full = jax.ShapeDtypeStruct((S, W), f32)
    (dq, dk, dv), moved = _call_with_exchange(
        body, (qn, kn, vb, do), exchange, name="sb_bwd", grid=(H, nq), in_specs=[qspec, kspec, kspec, qspec],
        out_specs=[qspec, kspec, kspec], out_shape=[full, full, full],
        scratch_shapes=[pltpu.VMEM((nq, tq, tk), bf16), pltpu.VMEM((nq, tq, tk), bf16),
                        pltpu.VMEM((2 * tk, tk), bf16), pltpu.VMEM((tk, tk), bf16)])
    return dq, dk, dv, moved


def _xa_fn(qx, kv, gq, gk):
    XW = XA_HEADS * XA_HEAD_DIM
    outs = []
    for h in range(XA_HEADS):
        sl = slice(h * XA_HEAD_DIM, (h + 1) * XA_HEAD_DIM)
        qn = _rms(qx[:, sl], gq, XA_HEAD_DIM)
        kn = _rms(kv[:, sl], gk, XA_HEAD_DIM)
        v = kv[:, XW + h * XA_HEAD_DIM:XW + (h + 1) * XA_HEAD_DIM]
        s = bdot_nt(qn, kn) * (XA_HEAD_DIM ** -0.5)
        e = jnp.exp(s - lax.stop_gradient(jnp.max(s, axis=-1, keepdims=True)))
        p = e / jnp.sum(e, axis=-1, keepdims=True)
        outs.append(bdot_nn(p, v))
    return jnp.concatenate(outs, axis=1)


def xa_fwd(qx, kv, gq, gk, *, ts=512):
    S, XW = qx.shape
    M = kv.shape[0]
    ts = _tile(S, ts)

    def body(q_ref, kv_ref, gq_ref, gk_ref, o_ref):
        o_ref[...] = _xa_fn(q_ref[...], kv_ref[...], gq_ref[...], gk_ref[...]).astype(bf16)

    row = pl.BlockSpec((ts, XW), lambda i: (i, 0))
    gspec = pl.BlockSpec((1, XA_HEAD_DIM), lambda i: (0, 0))
    return pl.pallas_call(
        body, name="xa_fwd", grid=(S // ts,),
        in_specs=[row, pl.BlockSpec((M, 2 * XW), lambda i: (0, 0)), gspec, gspec], out_specs=row,
        out_shape=jax.ShapeDtypeStruct((S, XW), bf16), compiler_params=_cp(("parallel",)),
    )(qx, kv, gq, gk)


def xa_bwd(qx, kv, gq, gk, do, *, ts=512):
    S, XW = qx.shape
    M = kv.shape[0]
    ts = _tile(S, ts)

    def body(q_ref, kv_ref, gq_ref, gk_ref, do_ref, dq_ref, dkv_ref, dgq_ref, dgk_ref):
        _, vjp = jax.vjp(_xa_fn, q_ref[...], kv_ref[...], gq_ref[...], gk_ref[...])
        dq, dkv, dgq, dgk = vjp(do_ref[...].astype(f32))
        dq_ref[...] = dq.astype(bf16)

        @pl.when(pl.program_id(0) == 0)
        def _():
            dkv_ref[...] = jnp.zeros_like(dkv_ref)
            dgq_ref[...] = jnp.zeros_like(dgq_ref)
            dgk_ref[...] = jnp.zeros_like(dgk_ref)

        dkv_ref[...] += dkv
        dgq_ref[...] += dgq
        dgk_ref[...] += dgk

    row = pl.BlockSpec((ts, XW), lambda i: (i, 0))
    gspec = pl.BlockSpec((1, XA_HEAD_DIM), lambda i: (0, 0))
    kvspec = pl.BlockSpec((M, 2 * XW), lambda i: (0, 0))
    gshape = jax.ShapeDtypeStruct((1, XA_HEAD_DIM), f32)
    return pl.pallas_call(
        body, name="xa_bwd", grid=(S // ts,), in_specs=[row, kvspec, gspec, gspec, row],
        out_specs=[row, kvspec, gspec, gspec],
        out_shape=[jax.ShapeDtypeStruct((S, XW), bf16), jax.ShapeDtypeStruct((M, 2 * XW), f32), gshape, gshape],
        compiler_params=_cp(("arbitrary",)),
    )(qx, kv, gq, gk, do)


def _s5_prep_fn(a_re, a_im, ldt, bT_re, bT_im, cT_re, cT_im):
    G, P, C = SSM_GROUPS, SSM_STATE, SSM_GROUP
    GP, GC = G * P, G * C
    lg_p, lg_c = P.bit_length() - 1, C.bit_length() - 1
    gi = lax.broadcasted_iota(jnp.int32, (G, GP), 0)
    ci = lax.broadcasted_iota(jnp.int32, (G, GP), 1) >> lg_p
    expand_dt = (gi == ci).astype(f32)
    dte = jnp.dot(jnp.exp(ldt), expand_dt, precision=lax.Precision.HIGHEST, preferred_element_type=f32)
    zr, zi = a_re * dte, a_im * dte
    mag = jnp.exp(zr)
    abr, abi = mag * jnp.cos(zi), mag * jnp.sin(zi)
    nr, ni = abr - 1.0, abi
    den = a_re * a_re + a_im * a_im
    cr = (nr * a_re + ni * a_im) / den
    cim = (ni * a_re - nr * a_im) / den
    bbr = cr * bT_re - cim * bT_im
    bbi = cr * bT_im + cim * bT_re
    rowg = lax.broadcasted_iota(jnp.int32, (GC, GP), 0) >> lg_c
    colg = lax.broadcasted_iota(jnp.int32, (GC, GP), 1) >> lg_p
    diag = rowg == colg

    def expand(t):
        return jnp.where(diag, jnp.broadcast_to(t[None], (G, C, GP)).reshape(GC, GP), 0.0)

    return abr, abi, expand(bbr), expand(bbi), expand(cT_re), expand(-cT_im)


def s5_prep(a_re, a_im, ldt, bT_re, bT_im, cT_re, cT_im):
    GP, GC = SSM_GROUPS * SSM_STATE, SSM_GROUPS * SSM_GROUP

    def body(a_re_ref, a_im_ref, ldt_ref, bTr_ref, bTi_ref, cTr_ref, cTi_ref, abr_ref, abi_ref, B_ref, C_ref):
        abr, abi, Br, Bi, Cr, Ci = _s5_prep_fn(a_re_ref[...], a_im_ref[...], ldt_ref[...], bTr_ref[...],
                                               bTi_ref[...], cTr_ref[...], cTi_ref[...])
        abr_ref[...] = abr
        abi_ref[...] = abi
        B_ref[0] = Br.astype(bf16)
        B_ref[1] = Bi.astype(bf16)
        C_ref[0] = Cr.astype(bf16)
        C_ref[1] = Ci.astype(bf16)

    vec = jax.ShapeDtypeStruct((1, GP), f32)
    mat = jax.ShapeDtypeStruct((2, GC, GP), bf16)
    return pl.pallas_call(body, name="s5_prep", out_shape=[vec, vec, mat, mat], compiler_params=_cp())(
        a_re, a_im, ldt, bT_re, bT_im, cT_re, cT_im)


def s5_prep_bwd(a_re, a_im, ldt, bT_re, bT_im, cT_re, cT_im, dabr, dabi, dB, dC):
    def body(a_re_ref, a_im_ref, ldt_ref, bTr_ref, bTi_ref, cTr_ref, cTi_ref, dabr_ref, dabi_ref, dB_ref, dC_ref,
             *outs):
        _, vjp = jax.vjp(_s5_prep_fn, a_re_ref[...], a_im_ref[...], ldt_ref[...], bTr_ref[...], bTi_ref[...],
                         cTr_ref[...], cTi_ref[...])
        grads = vjp((dabr_ref[...], dabi_ref[...], dB_ref[0], dB_ref[1], dC_ref[0], dC_ref[1]))
        for o_ref, gv in zip(outs, grads):
            o_ref[...] = gv

    ins = (a_re, a_im, ldt, bT_re, bT_im, cT_re, cT_im)
    return pl.pallas_call(body, name="s5_prep_bwd", out_shape=[jax.ShapeDtypeStruct(v.shape, f32) for v in ins],
                          compiler_params=_cp())(*ins, dabr, dabi, dB, dC)


def _cmul(ar, ai, br, bi):
    return ar * br - ai * bi, ar * bi + ai * br


SCAN_CHUNKS = 32


def _chunk_carry(Lr, Li, Pr, Pi, scratch, reverse):
    lr_ref, li_ref, cr_ref, ci_ref = scratch
    lr_ref[...] = Lr
    li_ref[...] = Li
    cur_r = jnp.zeros((1, LANES), f32)
    cur_i = jnp.zeros((1, LANES), f32)
    order = range(SCAN_CHUNKS - 1, -1, -1) if reverse else range(SCAN_CHUNKS)
    for c in order:
        cr_ref[pl.ds(c, 1), :] = cur_r
        ci_ref[pl.ds(c, 1), :] = cur_i
        mr, mi = _cmul(Pr, Pi, cur_r, cur_i)
        cur_r, cur_i = lr_ref[pl.ds(c, 1), :] + mr, li_ref[pl.ds(c, 1), :] + mi
    return cr_ref[...], ci_ref[...]


def _chunk_rows(j):
    return pl.ds(pl.multiple_of(j * SCAN_CHUNKS, SCAN_CHUNKS), SCAN_CHUNKS)


def row_shuffle(x, a, b, *, name, add=None, out_dtype=f32):
    S, W = x.shape
    assert a * b == S and x.dtype == f32

    def body(*refs):
        x_ref, o_ref = refs[0], refs[-1]

        def step(i, _):
            dst = pl.ds(pl.multiple_of(i * b, b), b)
            v = x_ref[pl.ds(i, b, stride=a), :]
            if add is not None:
                v = v + refs[1][dst, :]
            o_ref[dst, :] = v.astype(out_dtype)
            return 0

        lax.fori_loop(0, a, step, 0)

    col = pl.BlockSpec((S, LANES), lambda t: (0, t))
    args = [x] + ([add] if add is not None else [])
    return pl.pallas_call(
        body, name=name, grid=(W // LANES,), in_specs=[col] * len(args), out_specs=col,
        out_shape=jax.ShapeDtypeStruct((S, W), out_dtype), compiler_params=_cp(("parallel",)),
    )(*args)


def _scan_scratch(n):
    small = pltpu.VMEM((SCAN_CHUNKS, LANES), f32)
    return [pltpu.VMEM((n, LANES), f32), pltpu.VMEM((n, LANES), f32), small, small, small, small]


def _drive(x_ref, m_ref, work_ref):
    S = x_ref.shape[0]
    rows = min(S, 1024)
    m = jnp.concatenate([m_ref[0], m_ref[1]], axis=1)

    def chunk(c, _):
        r = pl.ds(pl.multiple_of(c * rows, rows), rows)
        y = _dot(x_ref[r, :], m, _NN)
        work_ref[0, r, :] = y[:, :LANES]
        work_ref[1, r, :] = y[:, LANES:]
        return 0

    lax.fori_loop(0, S // rows, chunk, 0)


def scan_fwd(u_il, Bm, abr, abi, exchange):
    S, C = u_il.shape
    N = Bm.shape[2]
    n = S // SCAN_CHUNKS
    shp = (SCAN_CHUNKS, LANES)

    def body(u_ref, B_ref, ar_ref, ai_ref, st_ref, work_ref, pwr_ref, pwi_ref, *scratch):
        a1r, a1i = ar_ref[...], ai_ref[...]
        ar = jnp.broadcast_to(a1r, shp)
        ai = jnp.broadcast_to(a1i, shp)
        _drive(u_ref, B_ref, work_ref)
        sr_ref, si_ref = work_ref.at[0], work_ref.at[1]

        def step(j, c):
            sr, si, pr, pi = c
            rows = _chunk_rows(j)
            mr, mi = _cmul(ar, ai, sr, si)
            sr, si = mr + sr_ref[rows, :], mi + si_ref[rows, :]
            sr_ref[rows, :] = sr
            si_ref[rows, :] = si
            pwr_ref[pl.ds(j, 1), :] = pr
            pwi_ref[pl.ds(j, 1), :] = pi
            npr, npi = _cmul(a1r, a1i, pr, pi)
            return sr, si, npr, npi

        z = jnp.zeros(shp, f32)
        sr, si, _, _ = lax.fori_loop(0, n, step, (z, z, a1r, a1i), unroll=2)
        cr, ci = _chunk_carry(sr, si, pwr_ref[pl.ds(n - 1, 1), :], pwi_ref[pl.ds(n - 1, 1), :], scratch, False)

        def step2(j, _):
            rows = _chunk_rows(j)
            pr = jnp.broadcast_to(pwr_ref[pl.ds(j, 1), :], shp)
            pi = jnp.broadcast_to(pwi_ref[pl.ds(j, 1), :], shp)
            mr, mi = _cmul(pr, pi, cr, ci)
            st_ref[0, rows, :] = (sr_ref[rows, :] + mr).astype(bf16)
            st_ref[1, rows, :] = (si_ref[rows, :] + mi).astype(bf16)
            return 0

        lax.fori_loop(0, n, step2, 0, unroll=4)

    blk = pl.BlockSpec((2, S, LANES), lambda t: (0, 0, t))
    vec = pl.BlockSpec((1, LANES), lambda t: (0, t))
    (st,), moved = _call_with_exchange(
        body, (u_il, Bm, abr, abi), exchange, name="scan_fwd", grid=(N // LANES,),
        in_specs=[pl.BlockSpec((S, C), lambda t: (0, 0)), pl.BlockSpec((2, C, LANES), lambda t: (0, 0, t)), vec, vec],
        out_specs=[blk], out_shape=[jax.ShapeDtypeStruct((2, S, N), bf16)],
        scratch_shapes=[pltpu.VMEM((2, S, LANES), f32)] + _scan_scratch(n))
    return st, moved


def scan_bwd(dy_il, Cm, st, abr, abi, exchange):
    _, S, N = st.shape
    C = dy_il.shape[1]
    n = S // SCAN_CHUNKS
    shp = (SCAN_CHUNKS, LANES)

    def body(dy_ref, C_ref, st_ref, ar_ref, ai_ref, g_ref, dar_ref, dai_ref, work_ref, qwr_ref, qwi_ref, *scratch):
        a1r, a1i = ar_ref[...], -ai_ref[...]
        ar = jnp.broadcast_to(a1r, shp)
        nai = jnp.broadcast_to(a1i, shp)
        _drive(dy_ref, C_ref, work_ref)
        gr_ref, gi_ref = work_ref.at[0], work_ref.at[1]
        sr_ref, si_ref = st_ref.at[0], st_ref.at[1]

        def step(jj, c):
            gr, gi, qr, qi = c
            j = n - 1 - jj
            rows = _chunk_rows(j)
            mr, mi = _cmul(ar, nai, gr, gi)
            gr, gi = mr + gr_ref[rows, :], mi + gi_ref[rows, :]
            gr_ref[rows, :] = gr
            gi_ref[rows, :] = gi
            qwr_ref[pl.ds(j, 1), :] = qr
            qwi_ref[pl.ds(j, 1), :] = qi
            nqr, nqi = _cmul(a1r, a1i, qr, qi)
            return gr, gi, nqr, nqi

        z = jnp.zeros(shp, f32)
        gr, gi, _, _ = lax.fori_loop(0, n, step, (z, z, a1r, a1i), unroll=2)
        cr, ci = _chunk_carry(gr, gi, qwr_ref[pl.ds(0, 1), :], qwi_ref[pl.ds(0, 1), :], scratch, True)
        sub = lax.broadcasted_iota(jnp.int32, shp, 0)

        def fix(j, spr, spi, acc):
            rows = _chunk_rows(j)
            qr = jnp.broadcast_to(qwr_ref[pl.ds(j, 1), :], shp)
            qi = jnp.broadcast_to(qwi_ref[pl.ds(j, 1), :], shp)
            mr, mi = _cmul(qr, qi, cr, ci)
            gr = gr_ref[rows, :] + mr
            gi = gi_ref[rows, :] + mi
            g_ref[0, rows, :] = gr.astype(bf16)
            g_ref[1, rows, :] = gi.astype(bf16)
            return acc[0] + gr * spr + gi * spi, acc[1] + gi * spr - gr * spi

        last = _chunk_rows(n - 1)
        spr = jnp.where(sub == 0, 0.0, pltpu.roll(sr_ref[last, :].astype(f32), 1, 0))
        spi = jnp.where(sub == 0, 0.0, pltpu.roll(si_ref[last, :].astype(f32), 1, 0))
        acc = fix(0, spr, spi, (z, z))

        def step2(j, acc):
            prev = _chunk_rows(j - 1)
            return fix(j, sr_ref[prev, :].astype(f32), si_ref[prev, :].astype(f32), acc)

        acc = lax.fori_loop(1, n, step2, acc)
        dar_ref[...] = jnp.sum(acc[0], axis=0, keepdims=True)
        dai_ref[...] = jnp.sum(acc[1], axis=0, keepdims=True)

    blk = pl.BlockSpec((2, S, LANES), lambda t: (0, 0, t))
    vec = pl.BlockSpec((1, LANES), lambda t: (0, t))
    vshape = jax.ShapeDtypeStruct((1, N), f32)
    (g, dar, dai), moved = _call_with_exchange(
        body, (dy_il, Cm, st, abr, abi), exchange, name="scan_bwd", grid=(N // LANES,),
        in_specs=[pl.BlockSpec((S, C), lambda t: (0, 0)), pl.BlockSpec((2, C, LANES), lambda t: (0, 0, t)), blk,
                  vec, vec],
        out_specs=[blk, vec, vec], out_shape=[jax.ShapeDtypeStruct((2, S, N), bf16), vshape, vshape],
        scratch_shapes=[pltpu.VMEM((2, S, LANES), f32)] + _scan_scratch(n))
    return g, dar, dai, moved


def _glu_fn(ypre, wglu):
    y = jax.nn.gelu(ypre)
    return y * jax.nn.sigmoid(bdot_nn(y, wglu))


def glu_fwd(ypre0, u, d, wglu, g_out, *, ts=512):
    S, W = u.shape
    ts = _tile(S, ts)

    def body(y0_ref, u_ref, d_ref, w_ref, g_ref, ypre_ref, z_ref, zn_ref):
        ypre = y0_ref[...] + d_ref[...] * u_ref[...]
        z = _glu_fn(ypre, w_ref[...])
        ypre_ref[...] = ypre
        z_ref[...] = z
        zn_ref[...] = _rms(z, g_ref[...], W).astype(bf16)

    row = pl.BlockSpec((ts, W), lambda i: (i, 0))
    vec = pl.BlockSpec((1, W), lambda i: (0, 0))
    full = jax.ShapeDtypeStruct((S, W), f32)
    return pl.pallas_call(
        body, name="glu_fwd", grid=(S // ts,),
        in_specs=[row, row, vec, pl.BlockSpec((W, W), lambda i: (0, 0)), vec], out_specs=[row, row, row],
        out_shape=[full, full, jax.ShapeDtypeStruct((S, W), bf16)], compiler_params=_cp(("parallel",)),
    )(ypre0, u, d, wglu, g_out)


def glu_bwd(ypre, u, d, wglu, dz, *, ts=512):
    S, W = u.shape
    ts = _tile(S, ts)

    def body(y_ref, u_ref, d_ref, w_ref, dz_ref, dy_ref, du_ref, dw_ref, dd_ref):
        _, vjp = jax.vjp(_glu_fn, y_ref[...], w_ref[...])
        dy, dw = vjp(dz_ref[...])
        dy_ref[...] = dy
        du_ref[...] = d_ref[...] * dy

        @pl.when(pl.program_id(0) == 0)
        def _():
            dw_ref[...] = jnp.zeros_like(dw_ref)
            dd_ref[...] = jnp.zeros_like(dd_ref)

        dw_ref[...] += dw
        dd_ref[...] += jnp.sum(dy * u_ref[...], axis=0, keepdims=True)

    row = pl.BlockSpec((ts, W), lambda i: (i, 0))
    vec = pl.BlockSpec((1, W), lambda i: (0, 0))
    sq = pl.BlockSpec((W, W), lambda i: (0, 0))
    full = jax.ShapeDtypeStruct((S, W), f32)
    return pl.pallas_call(
        body, name="glu_bwd", grid=(S // ts,), in_specs=[row, row, vec, sq, row], out_specs=[row, row, sq, vec],
        out_shape=[full, full, jax.ShapeDtypeStruct((W, W), f32), jax.ShapeDtypeStruct((1, W), f32)],
        compiler_params=_cp(("arbitrary",)),
    )(ypre, u, d, wglu, dz)


def adamw(w, g, m, v, *, name, tr=256):
    R, C = w.shape
    tr = _row_tile(R, tr)

    def body(w_ref, g_ref, m_ref, v_ref, d_ref, nm_ref, nv_ref):
        gv = g_ref[...]
        nm = ADAM_B1 * m_ref[...] + (1.0 - ADAM_B1) * gv
        nv = ADAM_B2 * v_ref[...] + (1.0 - ADAM_B2) * jnp.square(gv)
        m_hat = nm / (1.0 - ADAM_B1 ** ADAM_STEP)
        v_hat = nv / (1.0 - ADAM_B2 ** ADAM_STEP)
        d_ref[...] = -ADAM_LR * (m_hat / (jnp.sqrt(v_hat) + ADAM_EPS) + ADAM_WD * w_ref[...])
        nm_ref[...] = nm
        nv_ref[...] = nv

    row = pl.BlockSpec((tr, C), lambda i: (i, 0))
    full = jax.ShapeDtypeStruct((R, C), f32)
    return pl.pallas_call(
        body, name=name, grid=(R // tr,), in_specs=[row] * 4, out_specs=[row] * 3, out_shape=[full] * 3,
        compiler_params=_cp(("parallel",)),
    )(w, g, m, v)


def add_half(g4, recv, c, *, name, tr=256):
    _, _, Rh, C = g4.shape
    tr = _row_tile(Rh, tr)

    def body(c_ref, a_ref, b_ref, o_ref):
        o_ref[...] = a_ref[...] + b_ref[...]

    grid_spec = pltpu.PrefetchScalarGridSpec(
        num_scalar_prefetch=1, grid=(N_CHIPS, Rh // tr),
        in_specs=[pl.BlockSpec((None, None, tr, C), lambda k, i, c_ref: (k, c_ref[0], i, 0)),
                  pl.BlockSpec((None, tr, C), lambda k, i, c_ref: (k, i, 0))],
        out_specs=pl.BlockSpec((None, tr, C), lambda k, i, c_ref: (k, i, 0)))
    return pl.pallas_call(body, name=name, grid_spec=grid_spec, out_shape=jax.ShapeDtypeStruct(recv.shape, f32),
                          compiler_params=_cp(("parallel", "parallel")))(c, g4, recv)


def sum_chips(p4, *, name, tr=256):
    _, Rh, C = p4.shape
    tr = _row_tile(Rh, tr)

    def body(a_ref, b_ref, c_ref, d_ref, o_ref):
        o_ref[...] = ((a_ref[...] + b_ref[...]) + c_ref[...]) + d_ref[...]

    spec = lambda k: pl.BlockSpec((None, tr, C), lambda i: (k, i, 0))
    return pl.pallas_call(
        body, name=name, grid=(Rh // tr,), in_specs=[spec(0), spec(1), spec(2), spec(3)],
        out_specs=pl.BlockSpec((tr, C), lambda i: (i, 0)), out_shape=jax.ShapeDtypeStruct((Rh, C), f32),
        compiler_params=_cp(("parallel",)),
    )(p4, p4, p4, p4)


def _place():
    return lax.axis_index("x"), lax.axis_index("y"), lax.axis_index("c")


def _other_chips(x, y):
    return [(1 - x, y), (x, 1 - y), (1 - x, 1 - y)]


def _chip_exchange(ins, outs, sems, scatter, start):
    if not ins:
        return
    send, recv, loc = sems
    x, y, c = _place()
    me = 2 * x + y
    for a in range(len(ins)):
        own = pltpu.make_async_copy(ins[a].at[me] if scatter else ins[a], outs[a].at[me], loc.at[a])
        own.start() if start else own.wait()
        for p, (px, py) in enumerate(_other_chips(x, y)):
            k = 2 * px + py
            cp = pltpu.make_async_remote_copy(
                src_ref=ins[a].at[k] if scatter else ins[a], dst_ref=outs[a].at[me if start else k],
                send_sem=send.at[3 * a + p], recv_sem=recv.at[3 * a + p], device_id=(px, py, c), device_id_type=MESH)
            cp.start() if start else cp.wait()


def _chip_exchange_args(arrs, scatter):
    n = len(arrs)
    shapes = [jax.ShapeDtypeStruct(a.shape if scatter else (N_CHIPS,) + a.shape, a.dtype) for a in arrs]
    sems = [pltpu.SemaphoreType.DMA((3 * n,)), pltpu.SemaphoreType.DMA((3 * n,)), pltpu.SemaphoreType.DMA((n,))]
    return shapes, sems if n else []


def _chip_exchange_call(arrs, scatter, name):
    n = len(arrs)

    def body(*refs):
        ins, outs, sems = refs[:n], refs[n:2 * n], refs[2 * n:]
        _chip_exchange(ins, outs, sems, scatter, True)
        _chip_exchange(ins, outs, sems, scatter, False)

    shapes, sems = _chip_exchange_args(arrs, scatter)
    return pl.pallas_call(
        body, name=name, in_specs=[ANY] * n, out_specs=[ANY] * n, out_shape=shapes, scratch_shapes=sems,
        compiler_params=pltpu.CompilerParams(has_side_effects=True),
    )(*arrs)


def allgather_chips(arrs, *, name):
    return _chip_exchange_call(arrs, False, name)


def sibling_swap(arrs, *, half, name):
    n = len(arrs)

    def body(*refs):
        ins, outs = refs[:n], refs[n:2 * n]
        send, recv = refs[2 * n:]
        x, y, c = _place()
        cps = []
        for a in range(n):
            src = ins[a].at[:, 1 - c] if half else ins[a]
            cp = pltpu.make_async_remote_copy(src_ref=src, dst_ref=outs[a], send_sem=send.at[a], recv_sem=recv.at[a],
                                              device_id=(x, y, 1 - c), device_id_type=MESH)
            cp.start()
            cps.append(cp)
        for cp in cps:
            cp.wait()

    def oshape(a):
        return jax.ShapeDtypeStruct((a.shape[0],) + a.shape[2:] if half else a.shape, a.dtype)

    return pl.pallas_call(
        body, name=name, in_specs=[ANY] * n, out_specs=[ANY] * n, out_shape=[oshape(a) for a in arrs],
        scratch_shapes=[pltpu.SemaphoreType.DMA((n,)), pltpu.SemaphoreType.DMA((n,))],
        compiler_params=pltpu.CompilerParams(has_side_effects=True),
    )(*arrs)


def chip_scatter(arrs, *, name):
    return _chip_exchange_call(arrs, True, name)


def _pad_cols(w):
    K = w.shape[0]
    w = w.reshape(K, -1, SB_HEAD_DIM)
    return jnp.pad(w, ((0, 0), (0, 0), (0, LANES - SB_HEAD_DIM))).reshape(K, -1)


def _unpad_cols(w):
    K = w.shape[0]
    return w.reshape(K, -1, LANES)[:, :, :SB_HEAD_DIM].reshape(K, -1)


def _pad_rows(w):
    N = w.shape[1]
    w = w.reshape(-1, SB_HEAD_DIM, N)
    return jnp.pad(w, ((0, 0), (0, LANES - SB_HEAD_DIM), (0, 0))).reshape(-1, N)


def _unpad_rows(w):
    N = w.shape[1]
    return w.reshape(-1, LANES, N)[:, :SB_HEAD_DIM, :].reshape(-1, N)


_PACK_ROWS = N_CHIPS * 2 * SUBLANES


def _pack(arrs):
    flat = jnp.concatenate([a.reshape(-1) for a in arrs])
    rows = -(-flat.shape[0] // LANES)
    rows = -(-rows // _PACK_ROWS) * _PACK_ROWS
    return jnp.pad(flat, (0, rows * LANES - flat.shape[0])).reshape(rows, LANES)


def _unpack(buf, shapes):
    flat = buf.reshape(-1)
    out, pos = [], 0
    for shp in shapes:
        size = 1
        for d in shp:
            size *= d
        out.append(flat[pos:pos + size].reshape(shp))
        pos += size
    return out


BIG = ("w_in", "ssm_w_glu", "w_out", "xa_w_q", "xa_w_kv", "xa_w_o", "w_up", "w_down")
SMALL = ("g_mix", "ssm_a_re", "ssm_a_im", "ssm_log_dt", "ssm_b_re", "ssm_b_im", "ssm_c_re", "ssm_c_im", "ssm_d",
         "sb_g_q", "sb_g_k", "g_out_ssm", "g_out_sb", "g_xa", "g_mem", "xa_g_q", "xa_g_k", "g_mlp")
WEIGHTS = ("g_mix", "w_in", "ssm_a_re", "ssm_a_im", "ssm_log_dt", "ssm_b_re", "ssm_b_im", "ssm_c_re", "ssm_c_im",
           "ssm_d", "ssm_w_glu", "sb_g_q", "sb_g_k", "g_out_ssm", "g_out_sb", "w_out", "g_xa", "g_mem", "xa_w_q",
           "xa_w_kv", "xa_g_q", "xa_g_k", "xa_w_o", "g_mlp", "w_up", "w_down")


def kernel(x, mem, g_mix, w_in, ssm_a_re, ssm_a_im, ssm_log_dt, ssm_b_re, ssm_b_im, ssm_c_re, ssm_c_im, ssm_d, ssm_w_glu, sb_g_q, sb_g_k, g_out_ssm, g_out_sb, w_out, g_xa, g_mem, xa_w_q, xa_w_kv, xa_g_q, xa_g_k, xa_w_o, g_mlp, w_up, w_down, loss_target, m_g_mix, m_w_in, m_ssm_a_re, m_ssm_a_im, m_ssm_log_dt, m_ssm_b_re, m_ssm_b_im, m_ssm_c_re, m_ssm_c_im, m_ssm_d, m_ssm_w_glu, m_sb_g_q, m_sb_g_k, m_g_out_ssm, m_g_out_sb, m_w_out, m_g_xa, m_g_mem, m_xa_w_q, m_xa_w_kv, m_xa_g_q, m_xa_g_k, m_xa_w_o, m_g_mlp, m_w_up, m_w_down, v_g_mix, v_w_in, v_ssm_a_re, v_ssm_a_im, v_ssm_log_dt, v_ssm_b_re, v_ssm_b_im, v_ssm_c_re, v_ssm_c_im, v_ssm_d, v_ssm_w_glu, v_sb_g_q, v_sb_g_k, v_g_out_ssm, v_g_out_sb, v_w_out, v_g_xa, v_g_mem, v_xa_w_q, v_xa_w_kv, v_xa_g_q, v_xa_g_k, v_xa_w_o, v_g_mlp, v_w_up, v_w_down):
    env = dict(locals())
    W = {n: env[n] for n in WEIGHTS}
    M1 = {n: env["m_" + n] for n in WEIGHTS}
    V2 = {n: env["v_" + n] for n in WEIGHTS}
    xs, mems, tgt = x[0], mem[0], loss_target[0]
    S, D = xs.shape
    G, P, C = SSM_GROUPS, SSM_STATE, SSM_GROUP
    GP = G * P
    SBW = SB_HEADS * SB_HEAD_DIM
    c_idx = lax.axis_index("c")

    big = dict(tn=1024, tk=1024)
    wide = dict(tm=1024, tn=1024, tk=2048)
    h0, (g_in,) = rms_norm(xs, g_mix, D, name="norm_x", exchange=([w_in[0].astype(bf16)], False))
    Wu = g_in[0]
    Wqkv = jnp.concatenate([_pad_cols(g_in[1]), _pad_cols(g_in[2]), _pad_cols(g_in[3])], axis=1)
    gq_pad, gk_pad = _pad_cols(sb_g_q), _pad_cols(sb_g_k)
    gosb_pad = _pad_cols(g_out_sb)
    a_re, a_im = ssm_a_re.reshape(1, GP), ssm_a_im.reshape(1, GP)
    bT_re = ssm_b_re[0].transpose(2, 0, 1).reshape(C, GP)
    bT_im = ssm_b_im[0].transpose(2, 0, 1).reshape(C, GP)
    cT_re = ssm_c_re[0].transpose(1, 0, 2).reshape(C, GP)
    cT_im = ssm_c_im[0].transpose(1, 0, 2).reshape(C, GP)
    s5_in = (a_re, a_im, ssm_log_dt, bT_re, bT_im, cT_re, cT_im)

    u = mm(h0, Wu, mode="nn", name="proj_u", tk=1024)
    shard = {n: W[n][0].astype(bf16) for n in BIG[1:]}
    qkv, (g_glu, g_out, g_xq, g_xkv, g_xo) = mm(
        h0, Wqkv, mode="nn", name="proj_qkv", tm=1024,
        exchange=([shard[n] for n in ("ssm_w_glu", "w_out", "xa_w_q", "xa_w_kv", "xa_w_o")], False), **big)
    qn, kn, vb = qkv_prep(qkv, gq_pad, gk_pad)
    o, (g_down,) = sb_fwd(qn, kn, vb, ([shard["w_down"]], False))
    Wglu = g_glu.reshape(-1, g_glu.shape[-1])
    Wout = g_out.reshape(-1, g_out.shape[-1])
    Wo_ssm, Wo_sb = Wout[:SBW], _pad_rows(Wout[SBW:])
    Wxq = g_xq.reshape(-1, g_xq.shape[-1])
    Wxkv = g_xkv.reshape(-1, g_xkv.shape[-1])
    Wxo = g_xo.transpose(1, 0, 2).reshape(g_xo.shape[1], -1)
    abr, abi, Bm, Cm = s5_prep(*s5_in)
    n_pos = S // SCAN_CHUNKS
    u_il = row_shuffle(u, n_pos, SCAN_CHUNKS, name="u_interleave", out_dtype=bf16)
    st, (g_up,) = scan_fwd(u_il, Bm, abr, abi, ([shard["w_up"]], False))
    ypre0_il = mm(st, Cm, mode="nt", name="s5_y", a_shards=2, b_shards=2, tm=1024, tk=2048)
    Wup = g_up.transpose(1, 0, 2).reshape(g_up.shape[1], -1)
    Wdown = g_down.reshape(-1, g_down.shape[-1])
    ypre0 = row_shuffle(ypre0_il, SCAN_CHUNKS, n_pos, name="y_token_order")
    ypre, z, zn = glu_fwd(ypre0, u, ssm_d, Wglu, g_out_ssm)
    on = rms_norm(o, gosb_pad, SBW, name="norm_o")
    x1a = mm(zn, Wo_ssm, mode="nn", name="out_ssm", epi="add", aux=xs, tn=1024)
    x1 = mm(on, Wo_sb, mode="nn", name="out_sb", epi="add", aux=x1a, **big)
    h1 = rms_norm(x1, g_xa, D, name="norm_x1")
    qx = mm(h1, Wxq, mode="nn", name="xa_q", tk=1024)
    memn = rms_norm(mems, g_mem, D, name="norm_mem")
    kv = mm(memn, Wxkv, mode="nn", name="xa_kv", **big)
    ox = xa_fwd(qx, kv, xa_g_q, xa_g_k)
    x2 = mm(ox, Wxo, mode="nn", name="xa_o", epi="add", aux=x1, tn=1024)
    h2 = rms_norm(x2, g_mlp, D, name="norm_x2")
    act = mm(h2, Wup, mode="nn", name="mlp_up", out_dtype=bf16, tm=1024, tn=2048, tk=1024)
    dx3, loss_part, dx3b = mm(act, Wdown, mode="nn", name="mlp_down", pro="relu2", epi="add", aux=x2,
                              loss_target=tgt, tm=1024, **big)
    loss = lax.psum(loss_part[0, 0], ("x", "y", "c"))

    dact = mm(dx3b, Wdown, mode="nt", name="d_act", epi="mul2relu", aux=act, out_dtype=bf16, tm=1024, tn=2048,
              tk=1024)
    dWdown = mm(act, dx3b, mode="tn", name="dw_down", pro="relu2", **wide)
    dWup = mm(h2, dact, mode="tn", name="dw_up", out_shards=N_CHIPS, **wide)
    dx2, dg_mlp, dx2b = mm(dact, Wup, mode="nt", name="d_h2", norm_bwd=(x2, g_mlp, dx3, D), tm=1024, **big)
    dox = mm(dx2b, Wxo, mode="nt", name="d_ox", out_dtype=bf16, tk=1024)
    dWxo = mm(ox, dx2b, mode="tn", name="dw_xo", out_shards=N_CHIPS, tk=1024)
    dqx, dkv, dg_xq, dg_xk = xa_bwd(qx, kv, xa_g_q, xa_g_k, dox)
    dWxq = mm(h1, dqx, mode="tn", name="dw_xq", tm=1024, tk=1024)
    dx1, dg_xa, dx1b = mm(dqx, Wxq, mode="nt", name="d_h1", norm_bwd=(x1, g_xa, dx2, D), tn=1024)
    dWxkv = mm(memn, dkv, mode="tn", name="dw_xkv", tm=1024, tn=1024)
    dmemn = mm(dkv, Wxkv, mode="nt", name="d_memn", **big)
    _, dg_mem = rms_bwd(mems, g_mem, dmemn, None, D, name="rms_bwd_mem")
    dyn_ssm = mm(dx1b, Wo_ssm, mode="nt", name="d_yn_ssm", tk=1024)
    dyn_sb = mm(dx1b, Wo_sb, mode="nt", name="d_yn_sb", **big)
    dWo_ssm = mm(zn, dx1b, mode="tn", name="dw_out_ssm", **big)
    dWo_sb = mm(on, dx1b, mode="tn", name="dw_out_sb", tm=1024, **big)
    dz, dg_os = rms_bwd(z, g_out_ssm, dyn_ssm, None, SBW, name="rms_bwd_ssm")
    do, dg_osb = rms_bwd(o, gosb_pad, dyn_sb, None, SBW, name="rms_bwd_sb")
    c_arr = c_idx.astype(jnp.int32).reshape(1)

    def sibling_sums(grads, names, tag):
        g4 = [g.reshape(N_CHIPS, 2, g.shape[1] // 2, g.shape[2]) for g in grads]
        from_sib = sibling_swap(g4, half=True, name="grad_to_sibling_" + tag)
        return [add_half(a, b, c_arr, name="add_sibling_" + n) for a, b, n in zip(g4, from_sib, names)]

    early = ("xa_w_q", "xa_w_kv", "xa_w_o", "w_up", "w_down")
    early_g = [dWxq.reshape(N_CHIPS, -1, dWxq.shape[1]), dWxkv.reshape(N_CHIPS, -1, dWxkv.shape[1]), dWxo, dWup,
               dWdown.reshape(N_CHIPS, -1, D)]
    pair = sibling_sums(early_g, early, "early")
    dqn, dkn, dv, parts_mlp = sb_bwd(qn, kn, vb, do, (pair[3:], True))
    dqkv, dg_q, dg_k, parts_xa = qkv_bwd(qkv, gq_pad, gk_pad, dqn, dkn, dv, (pair[:3], True))
    dypre, du_skip, dWglu, dd = glu_bwd(ypre, u, ssm_d, Wglu, dz)
    mid = ("ssm_w_glu", "w_out")
    mid_g = [dWglu.reshape(N_CHIPS, -1, dWglu.shape[1]),
             jnp.concatenate([dWo_ssm, _unpad_rows(dWo_sb)]).reshape(N_CHIPS, -1, D)]
    dypre_il = row_shuffle(dypre, n_pos, SCAN_CHUNKS, name="dy_interleave", out_dtype=bf16)
    dCm = mm(dypre_il, st, mode="tn", name="d_cmat", b_shards=2, out_shards=2, **wide)
    gst, dabr, dabi, parts_mid = scan_bwd(dypre_il, Cm, st, abr, abi, (sibling_sums(mid_g, mid, "mid"), True))
    dBm = mm(u_il, gst, mode="tn", name="d_bmat", b_shards=2, out_shards=2, **wide)
    du_il = mm(gst, Bm, mode="nt", name="d_u", a_shards=2, b_shards=2, tm=1024, tk=2048)
    mine = {n: sum_chips(p, name="sum_chips_" + n)
            for n, p in zip(early + mid, [*parts_xa, *parts_mlp, *parts_mid])}
    du = row_shuffle(du_il, SCAN_CHUNKS, n_pos, name="du_token_order", add=du_skip, out_dtype=bf16)
    s5_g = s5_prep_bwd(*s5_in, dabr, dabi, dBm, dCm)
    dWu = mm(h0, du, mode="tn", name="dw_u", tm=1024, tk=1024)
    dWqkv = mm(h0, dqkv, mode="tn", name="dw_qkv", **wide)
    HW = SB_HEADS * LANES
    w_in_g = jnp.stack([dWu, _unpad_cols(dWqkv[:, :HW]), _unpad_cols(dWqkv[:, HW:2 * HW]),
                        _unpad_cols(dWqkv[:, 2 * HW:])])
    dh0a = mm(du, Wu, mode="nt", name="d_h0_u", tn=1024)
    (dx, dg_mix, _), parts_in = mm(dqkv, Wqkv, mode="nt", name="d_h0_qkv", epi="add", aux=dh0a,
                                   norm_bwd=(xs, g_mix, dx1, D), tm=1024,
                                   exchange=(sibling_sums([w_in_g], ("w_in",), "w_in"), True), **big)
    mine["w_in"] = sum_chips(parts_in[0], name="sum_chips_w_in")

    late = ("small",)
    late_g = []
    da_re, da_im, dldt, dbT_re, dbT_im, dcT_re, dcT_im = s5_g
    small_g = {
        "g_mix": dg_mix, "ssm_a_re": da_re, "ssm_a_im": da_im, "ssm_log_dt": dldt,
        "ssm_b_re": dbT_re.reshape(C, G, P).transpose(1, 2, 0), "ssm_b_im": dbT_im.reshape(C, G, P).transpose(1, 2, 0),
        "ssm_c_re": dcT_re.reshape(C, G, P).transpose(1, 0, 2), "ssm_c_im": dcT_im.reshape(C, G, P).transpose(1, 0, 2),
        "ssm_d": dd, "sb_g_q": dg_q[:, :SB_HEAD_DIM], "sb_g_k": dg_k[:, :SB_HEAD_DIM], "g_out_ssm": dg_os,
        "g_out_sb": _unpad_cols(dg_osb), "g_xa": dg_xa, "g_mem": dg_mem, "xa_g_q": dg_xq, "xa_g_k": dg_xk,
        "g_mlp": dg_mlp,
    }
    late_g.append(_pack([small_g[n] for n in SMALL]).reshape(N_CHIPS, -1, LANES))

    parts_late = chip_scatter(sibling_sums(late_g, late, "late"), name="grad_to_chips_late")
    mine.update({n: sum_chips(p, name="sum_chips_" + n) for n, p in zip(late, parts_late)})
    mine = [mine[n] for n in list(BIG) + ["small"]]
    other = sibling_swap(mine, half=False, name="grad_half_to_sibling")
    shard = [jnp.where(c_idx == 0, jnp.concatenate([a, b]), jnp.concatenate([b, a])) for a, b in zip(mine, other)]
    small_all = allgather_chips([shard[-1]], name="gather_small")[0]
    small_red = small_all.reshape(-1, LANES)

    out = {}
    for n, gs in zip(BIG, shard[:-1]):
        shp = W[n].shape
        w2, m2, v2 = (t.reshape(gs.shape) for t in (W[n], M1[n], V2[n]))
        d, nm, nv = adamw(w2, gs, m2, v2, name="adamw_" + n)
        out[n] = tuple(t.reshape(shp) for t in (gs, d, nm, nv))
    shapes = [W[n].shape for n in SMALL]
    d, nm, nv = adamw(_pack([W[n] for n in SMALL]), small_red, _pack([M1[n] for n in SMALL]),
                      _pack([V2[n] for n in SMALL]), name="adamw_small")
    for n, gs, dd_, mm_, vv_ in zip(SMALL, _unpack(small_red, shapes), _unpack(d, shapes), _unpack(nm, shapes),
                                    _unpack(nv, shapes)):
        out[n] = (gs, dd_, mm_, vv_)
    res = [loss, dx[None]]
    for kind in range(4):
        res += [out[n][kind] for n in WEIGHTS]
    return tuple(res)
```

```python
import jax
import jax.numpy as jnp
from jax import lax
from jax.experimental import pallas as pl
from jax.experimental.pallas import tpu as pltpu

f32 = jnp.float32
bf16 = jnp.bfloat16

NORM_EPS = 1e-6
SSM_GROUPS = 32
SSM_GROUP = 16
SSM_STATE = 64
SB_HEADS = 8
SB_HEAD_DIM = 64
XA_HEADS = 4
XA_HEAD_DIM = 128
LANES = 128
SUBLANES = 8
N_CHIPS = 4
ADAM_LR = 0.001
ADAM_B1 = 0.9
ADAM_B2 = 0.999
ADAM_EPS = 1e-08
ADAM_WD = 0.01
ADAM_STEP = 10
VMEM_LIMIT = 56 * 1024 * 1024
MESH = pl.DeviceIdType.MESH
ANY = pl.BlockSpec(memory_space=pl.ANY)


def _cp(sem=None):
    return pltpu.CompilerParams(dimension_semantics=sem, vmem_limit_bytes=VMEM_LIMIT)


def _tile(n, pref):
    if n <= pref:
        return n
    t = (pref // LANES) * LANES
    while t > LANES and n % t:
        t -= LANES
    assert n % t == 0, (n, pref)
    return t


def _row_tile(n, pref):
    if n <= pref:
        return n
    t = (pref // SUBLANES) * SUBLANES
    while n % t:
        t -= SUBLANES
    return t


def _dot(a, b, dims):
    return lax.dot_general(a.astype(bf16), b.astype(bf16), (dims, ((), ())), preferred_element_type=f32)


_NN = ((1,), (0,))
_NT = ((1,), (1,))
_TN = ((0,), (0,))


@jax.custom_vjp
def bdot_nn(a, b):
    return _dot(a, b, _NN)


def _bdot_nn_fwd(a, b):
    return _dot(a, b, _NN), (a, b)


def _bdot_nn_bwd(res, g):
    a, b = res
    return _dot(g, b, _NT), _dot(a, g, _TN)


bdot_nn.defvjp(_bdot_nn_fwd, _bdot_nn_bwd)


@jax.custom_vjp
def bdot_nt(a, b):
    return _dot(a, b, _NT)


def _bdot_nt_fwd(a, b):
    return _dot(a, b, _NT), (a, b)


def _bdot_nt_bwd(res, g):
    a, b = res
    return _dot(g, b, _NN), _dot(g, a, _TN)


bdot_nt.defvjp(_bdot_nt_fwd, _bdot_nt_bwd)


def _rms(x, g, denom):
    r = lax.rsqrt(jnp.sum(x * x, axis=-1, keepdims=True) * (1.0 / denom) + NORM_EPS)
    return x * r * g


def _opspec(block, row_of, col_of, shards, ncol_tiles):
    if shards == 1:
        return pl.BlockSpec(block, lambda i, j, k: (row_of(i, j, k), col_of(i, j, k)))
    per = ncol_tiles // shards
    return pl.BlockSpec((None,) + block,
                        lambda i, j, k: (col_of(i, j, k) // per, row_of(i, j, k), col_of(i, j, k) % per))


def _call_with_exchange(body, args, exchange, *, name, grid, in_specs, out_specs, out_shape, scratch_shapes=()):
    xs, scatter = exchange
    n, n_in, n_out, n_scr = len(xs), len(in_specs), len(out_specs), len(scratch_shapes)
    x_shapes, sems = _chip_exchange_args(xs, scatter)

    def wrapped(*refs):
        ins, x_ins = refs[:n_in], refs[n_in:n_in + n]
        outs, x_outs = refs[n_in + n:n_in + n + n_out], refs[n_in + n + n_out:n_in + 2 * n + n_out]
        scratch, x_sems = refs[n_in + 2 * n + n_out:n_in + 2 * n + n_out + n_scr], refs[n_in + 2 * n + n_out + n_scr:]
        first, last = True, True
        for d, steps in enumerate(grid):
            first = first & (pl.program_id(d) == 0)
            last = last & (pl.program_id(d) == steps - 1)

        @pl.when(first)
        def _():
            _chip_exchange(x_ins, x_outs, x_sems, scatter, True)

        body(*ins, *outs, *scratch)

        @pl.when(last)
        def _():
            _chip_exchange(x_ins, x_outs, x_sems, scatter, False)

    res = pl.pallas_call(
        wrapped, name=name, grid=grid, in_specs=list(in_specs) + [ANY] * n, out_specs=list(out_specs) + [ANY] * n,
        out_shape=list(out_shape) + x_shapes, scratch_shapes=list(scratch_shapes) + sems,
        compiler_params=_cp(("arbitrary",) * len(grid)),
    )(*args, *xs)
    return res[:n_out], res[n_out:]


def mm(a, b, *, mode, name, tm=1024, tn=512, tk=512, pro="none", epi="none", aux=None,
       out_dtype=f32, a_shards=1, b_shards=1, out_shards=1, exchange=None, norm_bwd=None, loss_target=None):
    ar, ac = a.shape[-2], a.shape[-1] * a_shards
    br, bc = b.shape[-2], b.shape[-1] * b_shards
    if mode == "nn":
        M, K, N = ar, ac, bc
        assert br == K
    elif mode == "nt":
        M, K, N = ar, ac, br
        assert bc == K
    else:
        M, K, N = ac, ar, bc
        assert br == K
    tm, tn, tk = _tile(M, tm), _tile(N, tn), _tile(K, tk)
    if a_shards > 1:
        if mode == "tn":
            tm = _tile(ac // a_shards, tm)
        else:
            tk = _tile(ac // a_shards, tk)
    if b_shards > 1:
        if mode == "nt":
            tk = _tile(bc // b_shards, tk)
        else:
            tn = _tile(bc // b_shards, tn)
    if out_shards > 1:
        tn = _tile(N // out_shards, tn)
    nm, nn_, nk = M // tm, N // tn, K // tk
    I = lambda i, j, k: i
    J = lambda i, j, k: j
    Kk = lambda i, j, k: k
    if mode == "nn":
        a_spec = _opspec((tm, tk), I, Kk, a_shards, nk)
        b_spec = _opspec((tk, tn), Kk, J, b_shards, nn_)
        dims = _NN
    elif mode == "nt":
        a_spec = _opspec((tm, tk), I, Kk, a_shards, nk)
        b_spec = _opspec((tn, tk), J, Kk, b_shards, nk)
        dims = _NT
    else:
        a_spec = _opspec((tk, tm), Kk, I, a_shards, nm)
        b_spec = _opspec((tk, tn), Kk, J, b_shards, nn_)
        dims = _TN
    in_specs = [a_spec, b_spec]
    args = [a, b]
    tile = pl.BlockSpec((tm, tn), lambda i, j, k: (i, j))
    if epi != "none":
        in_specs.append(tile)
        args.append(aux)
    if out_shards == 1:
        out_spec = tile
        out_shape = jax.ShapeDtypeStruct((M, N), out_dtype)
    else:
        per = nn_ // out_shards
        out_spec = pl.BlockSpec((None, tm, tn), lambda i, j, k: (j // per, i, j % per))
        out_shape = jax.ShapeDtypeStruct((out_shards, M, N // out_shards), out_dtype)
    out_specs, out_shapes = [out_spec], [out_shape]
    if norm_bwd is not None:
        nx, ng, nres, denom = norm_bwd
        assert tn == N and out_shards == 1 and out_dtype == f32
        vec = pl.BlockSpec((1, tn), lambda i, j, k: (0, 0))
        in_specs += [tile, vec, tile]
        args += [nx, ng, nres]
        out_specs += [vec, tile]
        out_shapes += [jax.ShapeDtypeStruct((1, N), f32), jax.ShapeDtypeStruct((M, N), bf16)]
    if loss_target is not None:
        assert tn == N and out_shards == 1 and out_dtype == f32 and norm_bwd is None
        in_specs.append(tile)
        args.append(loss_target)
        out_specs += [pl.BlockSpec((1, 1), lambda i, j, k: (0, 0)), tile]
        out_shapes += [jax.ShapeDtypeStruct((1, 1), f32), jax.ShapeDtypeStruct((M, N), bf16)]
    n_in = len(in_specs)

    def body(*refs):
        a_ref, b_ref = refs[0], refs[1]
        aux_ref = refs[2] if epi != "none" else None
        o_ref, acc_ref = refs[n_in], refs[-1]
        i, k = pl.program_id(0), pl.program_id(2)

        @pl.when(k == 0)
        def _():
            acc_ref[...] = jnp.zeros_like(acc_ref)

        av = a_ref[...]
        if pro == "relu2":
            av = jnp.square(jnp.maximum(av.astype(f32), 0.0))
        acc_ref[...] += _dot(av, b_ref[...], dims)

        @pl.when(k == nk - 1)
        def _():
            res = acc_ref[...]
            if epi == "add":
                res = res + aux_ref[...].astype(f32)
            elif epi == "mul2relu":
                res = res * (2.0 * jnp.maximum(aux_ref[...].astype(f32), 0.0))
            if loss_target is not None:
                l_ref, twin_ref = refs[n_in + 1], refs[n_in + 2]
                err = res - refs[n_in - 1][...]
                dy = err * (1.0 / N)
                o_ref[...] = dy
                twin_ref[...] = dy.astype(bf16)

                @pl.when(i == 0)
                def _():
                    l_ref[...] = jnp.zeros_like(l_ref)

                rows = jnp.sum(err * err, axis=1, keepdims=True) * (1.0 / N)
                l_ref[...] += 0.5 * jnp.sum(rows, axis=0, keepdims=True)
            elif norm_bwd is None:
                o_ref[...] = res.astype(out_dtype)
            else:
                x_ref, g_ref, res_ref = refs[n_in - 3:n_in]
                dg_ref, twin_ref = refs[n_in + 1], refs[n_in + 2]
                _, vjp = jax.vjp(lambda xv, gv: _rms(xv, gv, denom), x_ref[...], g_ref[...])
                dx, dg = vjp(res)
                dx = dx + res_ref[...]
                o_ref[...] = dx
                twin_ref[...] = dx.astype(bf16)

                @pl.when(i == 0)
                def _():
                    dg_ref[...] = jnp.zeros_like(dg_ref)

                dg_ref[...] += dg

    acc = [pltpu.VMEM((tm, tn), f32)]
    single = norm_bwd is None and loss_target is None
    if exchange is not None:
        outs, moved = _call_with_exchange(body, args, exchange, name=name, grid=(nm, nn_, nk), in_specs=in_specs,
                                          out_specs=out_specs, out_shape=out_shapes, scratch_shapes=acc)
        return (outs[0] if single else tuple(outs)), moved
    outs = pl.pallas_call(
        body, name=name, grid=(nm, nn_, nk), in_specs=in_specs, out_specs=out_specs, out_shape=out_shapes,
        scratch_shapes=acc, compiler_params=_cp(("parallel" if single else "arbitrary", "parallel", "arbitrary")),
    )(*args)
    return outs[0] if single else tuple(outs)


def rms_norm(x, g, denom, *, name, ts=512, exchange=None):
    S, D = x.shape
    ts = _tile(S, ts)

    def body(x_ref, g_ref, h_ref):
        h_ref[...] = _rms(x_ref[...], g_ref[...], denom).astype(bf16)

    row = pl.BlockSpec((ts, D), lambda i: (i, 0))
    kw = dict(name=name, grid=(S // ts,), in_specs=[row, pl.BlockSpec((1, D), lambda i: (0, 0))])
    if exchange is not None:
        (h,), moved = _call_with_exchange(body, (x, g), exchange, out_specs=[row],
                                          out_shape=[jax.ShapeDtypeStruct((S, D), bf16)], **kw)
        return h, moved
    return pl.pallas_call(body, out_specs=row, out_shape=jax.ShapeDtypeStruct((S, D), bf16),
                          compiler_params=_cp(("parallel",)), **kw)(x, g)


def rms_bwd(x, g, dy, res, denom, *, name, ts=512, twin=False):
    S, D = x.shape
    ts = _tile(S, ts)
    has_res = res is not None

    def body(*refs):
        x_ref, g_ref, dy_ref = refs[:3]
        outs = refs[4:] if has_res else refs[3:]
        _, vjp = jax.vjp(lambda xv, gv: _rms(xv, gv, denom), x_ref[...], g_ref[...])
        dx, dg = vjp(dy_ref[...])
        if has_res:
            dx = dx + refs[3][...]
        outs[0][...] = dx
        if twin:
            outs[2][...] = dx.astype(bf16)
        dg_ref = outs[1]

        @pl.when(pl.program_id(0) == 0)
        def _():
            dg_ref[...] = jnp.zeros_like(dg_ref)

        dg_ref[...] += dg

    row = pl.BlockSpec((ts, D), lambda i: (i, 0))
    vec = pl.BlockSpec((1, D), lambda i: (0, 0))
    in_specs = [row, vec, row] + ([row] if has_res else [])
    args = [x, g, dy] + ([res] if has_res else [])
    return pl.pallas_call(
        body, name=name, grid=(S // ts,), in_specs=in_specs, out_specs=[row, vec] + ([row] if twin else []),
        out_shape=[jax.ShapeDtypeStruct((S, D), f32), jax.ShapeDtypeStruct((1, D), f32)]
        + ([jax.ShapeDtypeStruct((S, D), bf16)] if twin else []),
        compiler_params=_cp(("arbitrary",)),
    )(*args)


LOG2E = 1.4426950408889634
LN2 = 0.6931471805599453


def _qk_fn(q, k, gq, gk):
    qs, ks = [], []
    for h in range(SB_HEADS):
        sl = slice(h * LANES, (h + 1) * LANES)
        qs.append(_rms(q[:, sl], gq, SB_HEAD_DIM) * (SB_HEAD_DIM ** -0.5 * LOG2E))
        ks.append(_rms(k[:, sl], gk, SB_HEAD_DIM))
    return jnp.concatenate(qs, axis=1), jnp.concatenate(ks, axis=1)


def qkv_prep(qkv, gq, gk, *, ts=512):
    S = qkv.shape[0]
    W = SB_HEADS * LANES
    ts = _tile(S, ts)

    def body(q_ref, k_ref, v_ref, gq_ref, gk_ref, qn_ref, kn_ref, vb_ref):
        qn, kn = _qk_fn(q_ref[...], k_ref[...], gq_ref[...], gk_ref[...])
        qn_ref[...] = qn.astype(bf16)
        kn_ref[...] = kn.astype(bf16)
        vb_ref[...] = v_ref[...].astype(bf16)

    out = jax.ShapeDtypeStruct((S, W), bf16)
    gspec = pl.BlockSpec((1, LANES), lambda i: (0, 0))
    ospec = pl.BlockSpec((ts, W), lambda i: (i, 0))
    col = lambda c: pl.BlockSpec((ts, W), lambda i: (i, c))
    return pl.pallas_call(
        body, name="qkv_prep", grid=(S // ts,), in_specs=[col(0), col(1), col(2), gspec, gspec],
        out_specs=[ospec, ospec, ospec], out_shape=[out, out, out], compiler_params=_cp(("parallel",)),
    )(qkv, qkv, qkv, gq, gk)


def qkv_bwd(qkv, gq, gk, dqn, dkn, dv, exchange, *, ts=512):
    S = qkv.shape[0]
    W = SB_HEADS * LANES
    ts = _tile(S, ts)

    def body(q_ref, k_ref, gq_ref, gk_ref, dqn_ref, dkn_ref, dv_ref, o_ref, dgq_ref, dgk_ref):
        _, vjp = jax.vjp(_qk_fn, q_ref[...], k_ref[...], gq_ref[...], gk_ref[...])
        dq, dk, dgq, dgk = vjp((dqn_ref[...] * LN2, dkn_ref[...] * LN2))
        o_ref[:, 0:W] = dq.astype(bf16)
        o_ref[:, W:2 * W] = dk.astype(bf16)
        o_ref[:, 2 * W:3 * W] = dv_ref[...].astype(bf16)

        @pl.when(pl.program_id(0) == 0)
        def _():
            dgq_ref[...] = jnp.zeros_like(dgq_ref)
            dgk_ref[...] = jnp.zeros_like(dgk_ref)

        dgq_ref[...] += dgq
        dgk_ref[...] += dgk

    gspec = pl.BlockSpec((1, LANES), lambda i: (0, 0))
    row = pl.BlockSpec((ts, W), lambda i: (i, 0))
    col = lambda c: pl.BlockSpec((ts, W), lambda i: (i, c))
    (dqkv, dgq, dgk), moved = _call_with_exchange(
        body, (qkv, qkv, gq, gk, dqn, dkn, dv), exchange, name="qkv_bwd", grid=(S // ts,),
        in_specs=[col(0), col(1), gspec, gspec, row, row, row],
        out_specs=[pl.BlockSpec((ts, 3 * W), lambda i: (i, 0)), gspec, gspec],
        out_shape=[jax.ShapeDtypeStruct((S, 3 * W), bf16), jax.ShapeDtypeStruct((1, LANES), f32),
                   jax.ShapeDtypeStruct((1, LANES), f32)])
    return dqkv, dgq, dgk, moved


def _sb_weights(q, ks, R, masked, row, col, UU):
    ls = [_dot(q, k, _NT) for k in ks]
    lbs, lm0s, cats = [], [], []
    for l, diag in zip(ls, masked):
        neg_abs = pltpu.bitcast(pltpu.bitcast(l, jnp.uint32) | jnp.uint32(0x80000000), f32)
        lp = jnp.log2(1.0 + jnp.exp2(neg_abs))
        lb = jnp.minimum(l, 0.0) - lp
        lm = lb - l
        if diag:
            lm = jnp.where(col < row, lm, 0.0)
        hi = lm.astype(bf16)
        lo = (lm - hi.astype(f32)).astype(bf16)
        lbs.append(lb)
        lm0s.append(lm[:, 0:1])
        cats.append(jnp.concatenate([hi, lo], axis=1))
    sums = [_dot(c, UU, _NN) for c in cats]
    ws = []
    for lb, lm0, A, diag in zip(lbs, lm0s, sums, masked):
        w = jnp.exp2(lb + (A + R))
        if diag:
            w = jnp.where(col < row, w, 0.0)
        R = R + (A[:, 0:1] + lm0)
        ws.append(w)
    return lbs, ws, R


def _tri2(tk):
    r = lax.broadcasted_iota(jnp.int32, (2 * tk, tk), 0)
    r = jnp.where(r >= tk, r - tk, r)
    c = lax.broadcasted_iota(jnp.int32, (2 * tk, tk), 1)
    return (r > c).astype(bf16)


SB_GROUP = 8


SB_ALL_ZERO_BELOW = -160.0


def _sweep(i, blocks_of, carry, descending, right_sum=None, ran=None):
    G = SB_GROUP
    n = jnp.maximum(i - 1, 0)
    rem, full = n % G, n // G
    asc = lambda js: js if descending else js[::-1]

    def first_group(c):
        one = lambda c: blocks_of([i], c, [True])
        two = lambda c: blocks_of(asc([i, i - 1]), c, asc([True, False]))
        return lax.cond(i >= 1, two, one, c)

    def body(p, c):
        return blocks_of(asc([i - 2 - p * G - u for u in range(G)]), c, [False] * G)

    def left_over(r):
        return lambda c: blocks_of(asc([r - 1 - u for u in range(r)]), c, [False] * r) if r else c

    if descending:
        carry = first_group(carry)
        alive = lambda c: jnp.max(right_sum(c)) > SB_ALL_ZERO_BELOW
        bodies, carry = lax.while_loop(lambda s: (s[0] < full) & alive(s[1]),
                                       lambda s: (s[0] + 1, body(s[0], s[1])), (jnp.int32(0), carry))
        tail = (bodies == full) & alive(carry)
        carry = lax.switch(jnp.where(tail, rem, 0), [left_over(r) for r in range(G)], carry)
        return carry, (bodies, tail)
    bodies, tail = ran
    carry = lax.switch(jnp.where(tail, rem, 0), [left_over(r) for r in range(G)], carry)
    carry = lax.fori_loop(0, bodies, lambda t, c: body(bodies - 1 - t, c), carry)
    return first_group(carry)


def sb_fwd(qn, kn, vb, exchange, *, tq=256):
    S, W = qn.shape
    H = W // LANES
    tq = _tile(S, tq)
    tk = tq
    nq = S // tq

    def body(q_ref, k_ref, v_ref, o_ref, uu_s):
        i = pl.program_id(1)

        @pl.when((pl.program_id(0) == 0) & (i == 0))
        def _():
            uu_s[...] = _tri2(tk)

        q = q_ref[...]
        row = lax.broadcasted_iota(jnp.int32, (tq, tk), 0)
        col = lax.broadcasted_iota(jnp.int32, (tq, tk), 1)
        UU = uu_s[...]

        def blocks(js, c, masked):
            rows = [pl.ds(pl.multiple_of(j * tk, tk), tk) for j in js]
            _, ws, R = _sb_weights(q, [k_ref[r, :] for r in rows], c[0], masked, row, col, UU)
            acc = c[1]
            for w, r in zip(ws, rows):
                acc = acc + _dot(w, v_ref[r, :], _NN)
            return R, acc

        c, _ = _sweep(i, blocks, (jnp.zeros((tq, 1), f32), jnp.zeros((tq, LANES), f32)), True, lambda c: c[0])
        o_ref[...] = c[1]

    qspec = pl.BlockSpec((tq, LANES), lambda h, i: (i, h))
    kspec = pl.BlockSpec((S, LANES), lambda h, i: (0, h))
    (o,), moved = _call_with_exchange(
        body, (qn, kn, vb), exchange, name="sb_fwd", grid=(H, nq), in_specs=[qspec, kspec, kspec], out_specs=[qspec],
        out_shape=[jax.ShapeDtypeStruct((S, W), f32)], scratch_shapes=[pltpu.VMEM((2 * tk, tk), bf16)])
    return o, moved


def sb_bwd(qn, kn, vb, do, exchange, *, tq=256):
    S, W = qn.shape
    H = W // LANES
    tq = _tile(S, tq)
    tk = tq
    nq = S // tq

    def body(q_ref, k_ref, v_ref, do_ref, dq_ref, dk_ref, dv_ref, dz_s, beta_s, uu_s, ue_s):
        i = pl.program_id(1)
        row = lax.broadcasted_iota(jnp.int32, (tq, tk), 0)
        col = lax.broadcasted_iota(jnp.int32, (tq, tk), 1)

        @pl.when((pl.program_id(0) == 0) & (i == 0))
        def _():
            uu_s[...] = _tri2(tk)
            ue_s[...] = (row < col).astype(bf16)

        @pl.when(i == 0)
        def _():
            dk_ref[...] = jnp.zeros_like(dk_ref)
            dv_ref[...] = jnp.zeros_like(dv_ref)

        q = q_ref[...]
        dob = do_ref[...].astype(bf16)
        UU = uu_s[...]
        Ue = ue_s[...]

        def sweep1(js, R, masked):
            rows = [pl.ds(pl.multiple_of(j * tk, tk), tk) for j in js]
            dws = [_dot(dob, v_ref[r, :], _NT) for r in rows]
            lbs, ws, R = _sb_weights(q, [k_ref[r, :] for r in rows], R, masked, row, col, UU)
            for j, lb, w, dw in zip(js, lbs, ws, dws):
                dz_s[j] = (dw * w).astype(bf16)
                beta_s[j] = jnp.exp2(lb).astype(bf16)
            for r, w in zip(rows, ws):
                dv_ref[r, :] += _dot(w, dob, _TN)
            return R

        _, ran = _sweep(i, sweep1, jnp.zeros((tq, 1), f32), True, lambda R: R)

        def sweep2(js, c, masked):
            rows = [pl.ds(pl.multiple_of(j * tk, tk), tk) for j in js]
            dzbs = [dz_s[j] for j in js]
            sums = [_dot(dzb, Ue, _NN) for dzb in dzbs]
            Lz, dq = c
            dlbs = []
            for j, dzb, Cz, diag in zip(js, dzbs, sums, masked):
                dz = dzb.astype(f32)
                dl = dz - beta_s[j].astype(f32) * (dz + (Cz + Lz))
                if diag:
                    dl = jnp.where(col < row, dl, 0.0)
                Lz = Lz + (Cz[:, tk - 1:tk] + dz[:, tk - 1:tk])
                dlbs.append(dl.astype(bf16))
            for r, dlb in zip(rows, dlbs):
                dq = dq + _dot(dlb, k_ref[r, :], _NN)
            for r, dlb in zip(rows, dlbs):
                dk_ref[r, :] += _dot(dlb, q, _TN)
            return Lz, dq

        c = _sweep(i, sweep2, (jnp.zeros((tq, 1), f32), jnp.zeros((tq, LANES), f32)), False, ran=ran)
        dq_ref[...] = c[1]

    qspec = pl.BlockSpec((tq, LANES), lambda h, i: (i, h))
    kspec = pl.BlockSpec((S, LANES), lambda h, i: (0, h))
    full = jax.ShapeDtypeStruct((S, W), f32)
    (dq, dk, dv), moved = _call_with_exchange(
        body, (qn, kn, vb, do), exchange, name="sb_bwd", grid=(H, nq), in_specs=[qspec, kspec, kspec, qspec],
        out_specs=[qspec, kspec, kspec], out_shape=[full, full, full],
        scratch_shapes=[pltpu.VMEM((nq, tq, tk), bf16), pltpu.VMEM((nq, tq, tk), bf16),
                        pltpu.VMEM((2 * tk, tk), bf16), pltpu.VMEM((tk, tk), bf16)])
    return dq, dk, dv, moved


def _xa_fn(qx, kv, gq, gk):
    XW = XA_HEADS * XA_HEAD_DIM
    outs = []
    for h in range(XA_HEADS):
        sl = slice(h * XA_HEAD_DIM, (h + 1) * XA_HEAD_DIM)
        qn = _rms(qx[:, sl], gq, XA_HEAD_DIM)
        kn = _rms(kv[:, sl], gk, XA_HEAD_DIM)
        v = kv[:, XW + h * XA_HEAD_DIM:XW + (h + 1) * XA_HEAD_DIM]
        s = bdot_nt(qn, kn) * (XA_HEAD_DIM ** -0.5)
        e = jnp.exp(s - lax.stop_gradient(jnp.max(s, axis=-1, keepdims=True)))
        p = e / jnp.sum(e, axis=-1, keepdims=True)
        outs.append(bdot_nn(p, v))
    return jnp.concatenate(outs, axis=1)


def xa_fwd(qx, kv, gq, gk, *, ts=512):
    S, XW = qx.shape
    M = kv.shape[0]
    ts = _tile(S, ts)

    def body(q_ref, kv_ref, gq_ref, gk_ref, o_ref):
        o_ref[...] = _xa_fn(q_ref[...], kv_ref[...], gq_ref[...], gk_ref[...]).astype(bf16)

    row = pl.BlockSpec((ts, XW), lambda i: (i, 0))
    gspec = pl.BlockSpec((1, XA_HEAD_DIM), lambda i: (0, 0))
    return pl.pallas_call(
        body, name="xa_fwd", grid=(S // ts,),
        in_specs=[row, pl.BlockSpec((M, 2 * XW), lambda i: (0, 0)), gspec, gspec], out_specs=row,
        out_shape=jax.ShapeDtypeStruct((S, XW), bf16), compiler_params=_cp(("parallel",)),
    )(qx, kv, gq, gk)


def xa_bwd(qx, kv, gq, gk, do, *, ts=512):
    S, XW = qx.shape
    M = kv.shape[0]
    ts = _tile(S, ts)

    def body(q_ref, kv_ref, gq_ref, gk_ref, do_ref, dq_ref, dkv_ref, dgq_ref, dgk_ref):
        _, vjp = jax.vjp(_xa_fn, q_ref[...], kv_ref[...], gq_ref[...], gk_ref[...])
        dq, dkv, dgq, dgk = vjp(do_ref[...].astype(f32))
        dq_ref[...] = dq.astype(bf16)

        @pl.when(pl.program_id(0) == 0)
        def _():
            dkv_ref[...] = jnp.zeros_like(dkv_ref)
            dgq_ref[...] = jnp.zeros_like(dgq_ref)
            dgk_ref[...] = jnp.zeros_like(dgk_ref)

        dkv_ref[...] += dkv
        dgq_ref[...] += dgq
        dgk_ref[...] += dgk

    row = pl.BlockSpec((ts, XW), lambda i: (i, 0))
    gspec = pl.BlockSpec((1, XA_HEAD_DIM), lambda i: (0, 0))
    kvspec = pl.BlockSpec((M, 2 * XW), lambda i: (0, 0))
    gshape = jax.ShapeDtypeStruct((1, XA_HEAD_DIM), f32)
    return pl.pallas_call(
        body, name="xa_bwd", grid=(S // ts,), in_specs=[row, kvspec, gspec, gspec, row],
        out_specs=[row, kvspec, gspec, gspec],
        out_shape=[jax.ShapeDtypeStruct((S, XW), bf16), jax.ShapeDtypeStruct((M, 2 * XW), f32), gshape, gshape],
        compiler_params=_cp(("arbitrary",)),
    )(qx, kv, gq, gk, do)


def _s5_prep_fn(a_re, a_im, ldt, bT_re, bT_im, cT_re, cT_im):
    G, P, C = SSM_GROUPS, SSM_STATE, SSM_GROUP
    GP, GC = G * P, G * C
    lg_p, lg_c = P.bit_length() - 1, C.bit_length() - 1
    gi = lax.broadcasted_iota(jnp.int32, (G, GP), 0)
    ci = lax.broadcasted_iota(jnp.int32, (G, GP), 1) >> lg_p
    expand_dt = (gi == ci).astype(f32)
    dte = jnp.dot(jnp.exp(ldt), expand_dt, precision=lax.Precision.HIGHEST, preferred_element_type=f32)
    zr, zi = a_re * dte, a_im * dte
    mag = jnp.exp(zr)
    abr, abi = mag * jnp.cos(zi), mag * jnp.sin(zi)
    nr, ni = abr - 1.0, abi
    den = a_re * a_re + a_im * a_im
    cr = (nr * a_re + ni * a_im) / den
    cim = (ni * a_re - nr * a_im) / den
    bbr = cr * bT_re - cim * bT_im
    bbi = cr * bT_im + cim * bT_re
    rowg = lax.broadcasted_iota(jnp.int32, (GC, GP), 0) >> lg_c
    colg = lax.broadcasted_iota(jnp.int32, (GC, GP), 1) >> lg_p
    diag = rowg == colg

    def expand(t):
        return jnp.where(diag, jnp.broadcast_to(t[None], (G, C, GP)).reshape(GC, GP), 0.0)

    return abr, abi, expand(bbr), expand(bbi), expand(cT_re), expand(-cT_im)


def s5_prep(a_re, a_im, ldt, bT_re, bT_im, cT_re, cT_im):
    GP, GC = SSM_GROUPS * SSM_STATE, SSM_GROUPS * SSM_GROUP

    def body(a_re_ref, a_im_ref, ldt_ref, bTr_ref, bTi_ref, cTr_ref, cTi_ref, abr_ref, abi_ref, B_ref, C_ref):
        abr, abi, Br, Bi, Cr, Ci = _s5_prep_fn(a_re_ref[...], a_im_ref[...], ldt_ref[...], bTr_ref[...],
                                               bTi_ref[...], cTr_ref[...], cTi_ref[...])
        abr_ref[...] = abr
        abi_ref[...] = abi
        B_ref[0] = Br.astype(bf16)
        B_ref[1] = Bi.astype(bf16)
        C_ref[0] = Cr.astype(bf16)
        C_ref[1] = Ci.astype(bf16)

    vec = jax.ShapeDtypeStruct((1, GP), f32)
    mat = jax.ShapeDtypeStruct((2, GC, GP), bf16)
    return pl.pallas_call(body, name="s5_prep", out_shape=[vec, vec, mat, mat], compiler_params=_cp())(
        a_re, a_im, ldt, bT_re, bT_im, cT_re, cT_im)


def s5_prep_bwd(a_re, a_im, ldt, bT_re, bT_im, cT_re, cT_im, dabr, dabi, dB, dC):
    def body(a_re_ref, a_im_ref, ldt_ref, bTr_ref, bTi_ref, cTr_ref, cTi_ref, dabr_ref, dabi_ref, dB_ref, dC_ref,
             *outs):
        _, vjp = jax.vjp(_s5_prep_fn, a_re_ref[...], a_im_ref[...], ldt_ref[...], bTr_ref[...], bTi_ref[...],
                         cTr_ref[...], cTi_ref[...])
        grads = vjp((dabr_ref[...], dabi_ref[...], dB_ref[0], dB_ref[1], dC_ref[0], dC_ref[1]))
        for o_ref, gv in zip(outs, grads):
            o_ref[...] = gv

    ins = (a_re, a_im, ldt, bT_re, bT_im, cT_re, cT_im)
    return pl.pallas_call(body, name="s5_prep_bwd", out_shape=[jax.ShapeDtypeStruct(v.shape, f32) for v in ins],
                          compiler_params=_cp())(*ins, dabr, dabi, dB, dC)


def _cmul(ar, ai, br, bi):
    return ar * br - ai * bi, ar * bi + ai * br


SCAN_CHUNKS = 32


def _chunk_carry(Lr, Li, Pr, Pi, scratch, reverse):
    lr_ref, li_ref, cr_ref, ci_ref = scratch
    lr_ref[...] = Lr
    li_ref[...] = Li
    cur_r = jnp.zeros((1, LANES), f32)
    cur_i = jnp.zeros((1, LANES), f32)
    order = range(SCAN_CHUNKS - 1, -1, -1) if reverse else range(SCAN_CHUNKS)
    for c in order:
        cr_ref[pl.ds(c, 1), :] = cur_r
        ci_ref[pl.ds(c, 1), :] = cur_i
        mr, mi = _cmul(Pr, Pi, cur_r, cur_i)
        cur_r, cur_i = lr_ref[pl.ds(c, 1), :] + mr, li_ref[pl.ds(c, 1), :] + mi
    return cr_ref[...], ci_ref[...]


def _chunk_rows(j):
    return pl.ds(pl.multiple_of(j * SCAN_CHUNKS, SCAN_CHUNKS), SCAN_CHUNKS)


def row_shuffle(x, a, b, *, name, add=None, out_dtype=f32):
    S, W = x.shape
    assert a * b == S and x.dtype == f32

    def body(*refs):
        x_ref, o_ref = refs[0], refs[-1]

        def step(i, _):
            dst = pl.ds(pl.multiple_of(i * b, b), b)
            v = x_ref[pl.ds(i, b, stride=a), :]
            if add is not None:
                v = v + refs[1][dst, :]
            o_ref[dst, :] = v.astype(out_dtype)
            return 0

        lax.fori_loop(0, a, step, 0)

    col = pl.BlockSpec((S, LANES), lambda t: (0, t))
    args = [x] + ([add] if add is not None else [])
    return pl.pallas_call(
        body, name=name, grid=(W // LANES,), in_specs=[col] * len(args), out_specs=col,
        out_shape=jax.ShapeDtypeStruct((S, W), out_dtype), compiler_params=_cp(("parallel",)),
    )(*args)


def _scan_scratch(n):
    small = pltpu.VMEM((SCAN_CHUNKS, LANES), f32)
    return [pltpu.VMEM((n, LANES), f32), pltpu.VMEM((n, LANES), f32), small, small, small, small]


def _drive(x_ref, m_ref, work_ref):
    S = x_ref.shape[0]
    rows = min(S, 1024)
    m = jnp.concatenate([m_ref[0], m_ref[1]], axis=1)

    def chunk(c, _):
        r = pl.ds(pl.multiple_of(c * rows, rows), rows)
        y = _dot(x_ref[r, :], m, _NN)
        work_ref[0, r, :] = y[:, :LANES]
        work_ref[1, r, :] = y[:, LANES:]
        return 0

    lax.fori_loop(0, S // rows, chunk, 0)


def scan_fwd(u_il, Bm, abr, abi, exchange):
    S, C = u_il.shape
    N = Bm.shape[2]
    n = S // SCAN_CHUNKS
    shp = (SCAN_CHUNKS, LANES)

    def body(u_ref, B_ref, ar_ref, ai_ref, st_ref, work_ref, pwr_ref, pwi_ref, *scratch):
        a1r, a1i = ar_ref[...], ai_ref[...]
        ar = jnp.broadcast_to(a1r, shp)
        ai = jnp.broadcast_to(a1i, shp)
        _drive(u_ref, B_ref, work_ref)
        sr_ref, si_ref = work_ref.at[0], work_ref.at[1]

        def step(j, c):
            sr, si, pr, pi = c
            rows = _chunk_rows(j)
            mr, mi = _cmul(ar, ai, sr, si)
            sr, si = mr + sr_ref[rows, :], mi + si_ref[rows, :]
            sr_ref[rows, :] = sr
            si_ref[rows, :] = si
            pwr_ref[pl.ds(j, 1), :] = pr
            pwi_ref[pl.ds(j, 1), :] = pi
            npr, npi = _cmul(a1r, a1i, pr, pi)
            return sr, si, npr, npi

        z = jnp.zeros(shp, f32)
        sr, si, _, _ = lax.fori_loop(0, n, step, (z, z, a1r, a1i), unroll=2)
        cr, ci = _chunk_carry(sr, si, pwr_ref[pl.ds(n - 1, 1), :], pwi_ref[pl.ds(n - 1, 1), :], scratch, False)

        def step2(j, _):
            rows = _chunk_rows(j)
            pr = jnp.broadcast_to(pwr_ref[pl.ds(j, 1), :], shp)
            pi = jnp.broadcast_to(pwi_ref[pl.ds(j, 1), :], shp)
            mr, mi = _cmul(pr, pi, cr, ci)
            st_ref[0, rows, :] = (sr_ref[rows, :] + mr).astype(bf16)
            st_ref[1, rows, :] = (si_ref[rows, :] + mi).astype(bf16)
            return 0

        lax.fori_loop(0, n, step2, 0, unroll=4)

    blk = pl.BlockSpec((2, S, LANES), lambda t: (0, 0, t))
    vec = pl.BlockSpec((1, LANES), lambda t: (0, t))
    (st,), moved = _call_with_exchange(
        body, (u_il, Bm, abr, abi), exchange, name="scan_fwd", grid=(N // LANES,),
        in_specs=[pl.BlockSpec((S, C), lambda t: (0, 0)), pl.BlockSpec((2, C, LANES), lambda t: (0, 0, t)), vec, vec],
        out_specs=[blk], out_shape=[jax.ShapeDtypeStruct((2, S, N), bf16)],
        scratch_shapes=[pltpu.VMEM((2, S, LANES), f32)] + _scan_scratch(n))
    return st, moved


def scan_bwd(dy_il, Cm, st, abr, abi, exchange):
    _, S, N = st.shape
    C = dy_il.shape[1]
    n = S // SCAN_CHUNKS
    shp = (SCAN_CHUNKS, LANES)

    def body(dy_ref, C_ref, st_ref, ar_ref, ai_ref, g_ref, dar_ref, dai_ref, work_ref, qwr_ref, qwi_ref, *scratch):
        a1r, a1i = ar_ref[...], -ai_ref[...]
        ar = jnp.broadcast_to(a1r, shp)
        nai = jnp.broadcast_to(a1i, shp)
        _drive(dy_ref, C_ref, work_ref)
        gr_ref, gi_ref = work_ref.at[0], work_ref.at[1]
        sr_ref, si_ref = st_ref.at[0], st_ref.at[1]

        def step(jj, c):
            gr, gi, qr, qi = c
            j = n - 1 - jj
            rows = _chunk_rows(j)
            mr, mi = _cmul(ar, nai, gr, gi)
            gr, gi = mr + gr_ref[rows, :], mi + gi_ref[rows, :]
            gr_ref[rows, :] = gr
            gi_ref[rows, :] = gi
            qwr_ref[pl.ds(j, 1), :] = qr
            qwi_ref[pl.ds(j, 1), :] = qi
            nqr, nqi = _cmul(a1r, a1i, qr, qi)
            return gr, gi, nqr, nqi

        z = jnp.zeros(shp, f32)
        gr, gi, _, _ = lax.fori_loop(0, n, step, (z, z, a1r, a1i), unroll=2)
        cr, ci = _chunk_carry(gr, gi, qwr_ref[pl.ds(0, 1), :], qwi_ref[pl.ds(0, 1), :], scratch, True)
        sub = lax.broadcasted_iota(jnp.int32, shp, 0)

        def fix(j, spr, spi, acc):
            rows = _chunk_rows(j)
            qr = jnp.broadcast_to(qwr_ref[pl.ds(j, 1), :], shp)
            qi = jnp.broadcast_to(qwi_ref[pl.ds(j, 1), :], shp)
            mr, mi = _cmul(qr, qi, cr, ci)
            gr = gr_ref[rows, :] + mr
            gi = gi_ref[rows, :] + mi
            g_ref[0, rows, :] = gr.astype(bf16)
            g_ref[1, rows, :] = gi.astype(bf16)
            return acc[0] + gr * spr + gi * spi, acc[1] + gi * spr - gr * spi

        last = _chunk_rows(n - 1)
        spr = jnp.where(sub == 0, 0.0, pltpu.roll(sr_ref[last, :].astype(f32), 1, 0))
        spi = jnp.where(sub == 0, 0.0, pltpu.roll(si_ref[last, :].astype(f32), 1, 0))
        acc = fix(0, spr, spi, (z, z))

        def step2(j, acc):
            prev = _chunk_rows(j - 1)
            return fix(j, sr_ref[prev, :].astype(f32), si_ref[prev, :].astype(f32), acc)

        acc = lax.fori_loop(1, n, step2, acc)
        dar_ref[...] = jnp.sum(acc[0], axis=0, keepdims=True)
        dai_ref[...] = jnp.sum(acc[1], axis=0, keepdims=True)

    blk = pl.BlockSpec((2, S, LANES), lambda t: (0, 0, t))
    vec = pl.BlockSpec((1, LANES), lambda t: (0, t))
    vshape = jax.ShapeDtypeStruct((1, N), f32)
    (g, dar, dai), moved = _call_with_exchange(
        body, (dy_il, Cm, st, abr, abi), exchange, name="scan_bwd", grid=(N // LANES,),
        in_specs=[pl.BlockSpec((S, C), lambda t: (0, 0)), pl.BlockSpec((2, C, LANES), lambda t: (0, 0, t)), blk,
                  vec, vec],
        out_specs=[blk, vec, vec], out_shape=[jax.ShapeDtypeStruct((2, S, N), bf16), vshape, vshape],
        scratch_shapes=[pltpu.VMEM((2, S, LANES), f32)] + _scan_scratch(n))
    return g, dar, dai, moved


def _glu_fn(ypre, wglu):
    y = jax.nn.gelu(ypre)
    return y * jax.nn.sigmoid(bdot_nn(y, wglu))


def glu_fwd(ypre0, u, d, wglu, g_out, *, ts=512):
    S, W = u.shape
    ts = _tile(S, ts)

    def body(y0_ref, u_ref, d_ref, w_ref, g_ref, ypre_ref, z_ref, zn_ref):
        ypre = y0_ref[...] + d_ref[...] * u_ref[...]
        z = _glu_fn(ypre, w_ref[...])
        ypre_ref[...] = ypre
        z_ref[...] = z
        zn_ref[...] = _rms(z, g_ref[...], W).astype(bf16)

    row = pl.BlockSpec((ts, W), lambda i: (i, 0))
    vec = pl.BlockSpec((1, W), lambda i: (0, 0))
    full = jax.ShapeDtypeStruct((S, W), f32)
    return pl.pallas_call(
        body, name="glu_fwd", grid=(S // ts,),
        in_specs=[row, row, vec, pl.BlockSpec((W, W), lambda i: (0, 0)), vec], out_specs=[row, row, row],
        out_shape=[full, full, jax.ShapeDtypeStruct((S, W), bf16)], compiler_params=_cp(("parallel",)),
    )(ypre0, u, d, wglu, g_out)


def glu_bwd(ypre, u, d, wglu, dz, *, ts=512):
    S, W = u.shape
    ts = _tile(S, ts)

    def body(y_ref, u_ref, d_ref, w_ref, dz_ref, dy_ref, du_ref, dw_ref, dd_ref):
        _, vjp = jax.vjp(_glu_fn, y_ref[...], w_ref[...])
        dy, dw = vjp(dz_ref[...])
        dy_ref[...] = dy
        du_ref[...] = d_ref[...] * dy

        @pl.when(pl.program_id(0) == 0)
        def _():
            dw_ref[...] = jnp.zeros_like(dw_ref)
            dd_ref[...] = jnp.zeros_like(dd_ref)

        dw_ref[...] += dw
        dd_ref[...] += jnp.sum(dy * u_ref[...], axis=0, keepdims=True)

    row = pl.BlockSpec((ts, W), lambda i: (i, 0))
    vec = pl.BlockSpec((1, W), lambda i: (0, 0))
    sq = pl.BlockSpec((W, W), lambda i: (0, 0))
    full = jax.ShapeDtypeStruct((S, W), f32)
    return pl.pallas_call(
        body, name="glu_bwd", grid=(S // ts,), in_specs=[row, row, vec, sq, row], out_specs=[row, row, sq, vec],
        out_shape=[full, full, jax.ShapeDtypeStruct((W, W), f32), jax.ShapeDtypeStruct((1, W), f32)],
        compiler_params=_cp(("arbitrary",)),
    )(ypre, u, d, wglu, dz)


def adamw(w, g, m, v, *, name, tr=256):
    R, C = w.shape
    tr = _row_tile(R, tr)

    def body(w_ref, g_ref, m_ref, v_ref, d_ref, nm_ref, nv_ref):
        gv = g_ref[...]
        nm = ADAM_B1 * m_ref[...] + (1.0 - ADAM_B1) * gv
        nv = ADAM_B2 * v_ref[...] + (1.0 - ADAM_B2) * jnp.square(gv)
        m_hat = nm / (1.0 - ADAM_B1 ** ADAM_STEP)
        v_hat = nv / (1.0 - ADAM_B2 ** ADAM_STEP)
        d_ref[...] = -ADAM_LR * (m_hat / (jnp.sqrt(v_hat) + ADAM_EPS) + ADAM_WD * w_ref[...])
        nm_ref[...] = nm
        nv_ref[...] = nv

    row = pl.BlockSpec((tr, C), lambda i: (i, 0))
    full = jax.ShapeDtypeStruct((R, C), f32)
    return pl.pallas_call(
        body, name=name, grid=(R // tr,), in_specs=[row] * 4, out_specs=[row] * 3, out_shape=[full] * 3,
        compiler_params=_cp(("parallel",)),
    )(w, g, m, v)


def add_half(g4, recv, c, *, name, tr=256):
    _, _, Rh, C = g4.shape
    tr = _row_tile(Rh, tr)

    def body(c_ref, a_ref, b_ref, o_ref):
        o_ref[...] = a_ref[...] + b_ref[...]

    grid_spec = pltpu.PrefetchScalarGridSpec(
        num_scalar_prefetch=1, grid=(N_CHIPS, Rh // tr),
        in_specs=[pl.BlockSpec((None, None, tr, C), lambda k, i, c_ref: (k, c_ref[0], i, 0)),
                  pl.BlockSpec((None, tr, C), lambda k, i, c_ref: (k, i, 0))],
        out_specs=pl.BlockSpec((None, tr, C), lambda k, i, c_ref: (k, i, 0)))
    return pl.pallas_call(body, name=name, grid_spec=grid_spec, out_shape=jax.ShapeDtypeStruct(recv.shape, f32),
                          compiler_params=_cp(("parallel", "parallel")))(c, g4, recv)


def sum_chips(p4, *, name, tr=256):
    _, Rh, C = p4.shape
    tr = _row_tile(Rh, tr)

    def body(a_ref, b_ref, c_ref, d_ref, o_ref):
        o_ref[...] = ((a_ref[...] + b_ref[...]) + c_ref[...]) + d_ref[...]

    spec = lambda k: pl.BlockSpec((None, tr, C), lambda i: (k, i, 0))
    return pl.pallas_call(
        body, name=name, grid=(Rh // tr,), in_specs=[spec(0), spec(1), spec(2), spec(3)],
        out_specs=pl.BlockSpec((tr, C), lambda i: (i, 0)), out_shape=jax.ShapeDtypeStruct((Rh, C), f32),
        compiler_params=_cp(("parallel",)),
    )(p4, p4, p4, p4)


def _place():
    return lax.axis_index("x"), lax.axis_index("y"), lax.axis_index("c")


def _other_chips(x, y):
    return [(1 - x, y), (x, 1 - y), (1 - x, 1 - y)]


def _chip_exchange(ins, outs, sems, scatter, start):
    if not ins:
        return
    send, recv, loc = sems
    x, y, c = _place()
    me = 2 * x + y
    for a in range(len(ins)):
        own = pltpu.make_async_copy(ins[a].at[me] if scatter else ins[a], outs[a].at[me], loc.at[a])
        own.start() if start else own.wait()
        for p, (px, py) in enumerate(_other_chips(x, y)):
            k = 2 * px + py
            cp = pltpu.make_async_remote_copy(
                src_ref=ins[a].at[k] if scatter else ins[a], dst_ref=outs[a].at[me if start else k],
                send_sem=send.at[3 * a + p], recv_sem=recv.at[3 * a + p], device_id=(px, py, c), device_id_type=MESH)
            cp.start() if start else cp.wait()


def _chip_exchange_args(arrs, scatter):
    n = len(arrs)
    shapes = [jax.ShapeDtypeStruct(a.shape if scatter else (N_CHIPS,) + a.shape, a.dtype) for a in arrs]
    sems = [pltpu.SemaphoreType.DMA((3 * n,)), pltpu.SemaphoreType.DMA((3 * n,)), pltpu.SemaphoreType.DMA((n,))]
    return shapes, sems if n else []


def _chip_exchange_call(arrs, scatter, name):
    n = len(arrs)

    def body(*refs):
        ins, outs, sems = refs[:n], refs[n:2 * n], refs[2 * n:]
        _chip_exchange(ins, outs, sems, scatter, True)
        _chip_exchange(ins, outs, sems, scatter, False)

    shapes, sems = _chip_exchange_args(arrs, scatter)
    return pl.pallas_call(
        body, name=name, in_specs=[ANY] * n, out_specs=[ANY] * n, out_shape=shapes, scratch_shapes=sems,
        compiler_params=pltpu.CompilerParams(has_side_effects=True),
    )(*arrs)


def allgather_chips(arrs, *, name):
    return _chip_exchange_call(arrs, False, name)


def sibling_swap(arrs, *, half, name):
    n = len(arrs)

    def body(*refs):
        ins, outs = refs[:n], refs[n:2 * n]
        send, recv = refs[2 * n:]
        x, y, c = _place()
        cps = []
        for a in range(n):
            src = ins[a].at[:, 1 - c] if half else ins[a]
            cp = pltpu.make_async_remote_copy(src_ref=src, dst_ref=outs[a], send_sem=send.at[a], recv_sem=recv.at[a],
                                              device_id=(x, y, 1 - c), device_id_type=MESH)
            cp.start()
            cps.append(cp)
        for cp in cps:
            cp.wait()

    def oshape(a):
        return jax.ShapeDtypeStruct((a.shape[0],) + a.shape[2:] if half else a.shape, a.dtype)

    return pl.pallas_call(
        body, name=name, in_specs=[ANY] * n, out_specs=[ANY] * n, out_shape=[oshape(a) for a in arrs],
        scratch_shapes=[pltpu.SemaphoreType.DMA((n,)), pltpu.SemaphoreType.DMA((n,))],
        compiler_params=pltpu.CompilerParams(has_side_effects=True),
    )(*arrs)


def chip_scatter(arrs, *, name):
    return _chip_exchange_call(arrs, True, name)


def _pad_cols(w):
    K = w.shape[0]
    w = w.reshape(K, -1, SB_HEAD_DIM)
    return jnp.pad(w, ((0, 0), (0, 0), (0, LANES - SB_HEAD_DIM))).reshape(K, -1)


def _unpad_cols(w):
    K = w.shape[0]
    return w.reshape(K, -1, LANES)[:, :, :SB_HEAD_DIM].reshape(K, -1)


def _pad_rows(w):
    N = w.shape[1]
    w = w.reshape(-1, SB_HEAD_DIM, N)
    return jnp.pad(w, ((0, 0), (0, LANES - SB_HEAD_DIM), (0, 0))).reshape(-1, N)


def _unpad_rows(w):
    N = w.shape[1]
    return w.reshape(-1, LANES, N)[:, :SB_HEAD_DIM, :].reshape(-1, N)


_PACK_ROWS = N_CHIPS * 2 * SUBLANES


def _pack(arrs):
    flat = jnp.concatenate([a.reshape(-1) for a in arrs])
    rows = -(-flat.shape[0] // LANES)
    rows = -(-rows // _PACK_ROWS) * _PACK_ROWS
    return jnp.pad(flat, (0, rows * LANES - flat.shape[0])).reshape(rows, LANES)


def _unpack(buf, shapes):
    flat = buf.reshape(-1)
    out, pos = [], 0
    for shp in shapes:
        size = 1
        for d in shp:
            size *= d
        out.append(flat[pos:pos + size].reshape(shp))
        pos += size
    return out


BIG = ("w_in", "ssm_w_glu", "w_out", "xa_w_q", "xa_w_kv", "xa_w_o", "w_up", "w_down")
SMALL = ("g_mix", "ssm_a_re", "ssm_a_im", "ssm_log_dt", "ssm_b_re", "ssm_b_im", "ssm_c_re", "ssm_c_im", "ssm_d",
         "sb_g_q", "sb_g_k", "g_out_ssm", "g_out_sb", "g_xa", "g_mem", "xa_g_q", "xa_g_k", "g_mlp")
WEIGHTS = ("g_mix", "w_in", "ssm_a_re", "ssm_a_im", "ssm_log_dt", "ssm_b_re", "ssm_b_im", "ssm_c_re", "ssm_c_im",
           "ssm_d", "ssm_w_glu", "sb_g_q", "sb_g_k", "g_out_ssm", "g_out_sb", "w_out", "g_xa", "g_mem", "xa_w_q",
           "xa_w_kv", "xa_g_q", "xa_g_k", "xa_w_o", "g_mlp", "w_up", "w_down")


def kernel(x, mem, g_mix, w_in, ssm_a_re, ssm_a_im, ssm_log_dt, ssm_b_re, ssm_b_im, ssm_c_re, ssm_c_im, ssm_d, ssm_w_glu, sb_g_q, sb_g_k, g_out_ssm, g_out_sb, w_out, g_xa, g_mem, xa_w_q, xa_w_kv, xa_g_q, xa_g_k, xa_w_o, g_mlp, w_up, w_down, loss_target, m_g_mix, m_w_in, m_ssm_a_re, m_ssm_a_im, m_ssm_log_dt, m_ssm_b_re, m_ssm_b_im, m_ssm_c_re, m_ssm_c_im, m_ssm_d, m_ssm_w_glu, m_sb_g_q, m_sb_g_k, m_g_out_ssm, m_g_out_sb, m_w_out, m_g_xa, m_g_mem, m_xa_w_q, m_xa_w_kv, m_xa_g_q, m_xa_g_k, m_xa_w_o, m_g_mlp, m_w_up, m_w_down, v_g_mix, v_w_in, v_ssm_a_re, v_ssm_a_im, v_ssm_log_dt, v_ssm_b_re, v_ssm_b_im, v_ssm_c_re, v_ssm_c_im, v_ssm_d, v_ssm_w_glu, v_sb_g_q, v_sb_g_k, v_g_out_ssm, v_g_out_sb, v_w_out, v_g_xa, v_g_mem, v_xa_w_q, v_xa_w_kv, v_xa_g_q, v_xa_g_k, v_xa_w_o, v_g_mlp, v_w_up, v_w_down):
    env = dict(locals())
    W = {n: env[n] for n in WEIGHTS}
    M1 = {n: env["m_" + n] for n in WEIGHTS}
    V2 = {n: env["v_" + n] for n in WEIGHTS}
    xs, mems, tgt = x[0], mem[0], loss_target[0]
    S, D = xs.shape
    G, P, C = SSM_GROUPS, SSM_STATE, SSM_GROUP
    GP = G * P
    SBW = SB_HEADS * SB_HEAD_DIM
    c_idx = lax.axis_index("c")

    big = dict(tn=1024, tk=1024)
    wide = dict(tm=1024, tn=1024, tk=2048)
    h0, (g_in,) = rms_norm(xs, g_mix, D, name="norm_x", exchange=([w_in[0].astype(bf16)], False))
    Wu = g_in[0]
    Wqkv = jnp.concatenate([_pad_cols(g_in[1]), _pad_cols(g_in[2]), _pad_cols(g_in[3])], axis=1)
    gq_pad, gk_pad = _pad_cols(sb_g_q), _pad_cols(sb_g_k)
    gosb_pad = _pad_cols(g_out_sb)
    a_re, a_im = ssm_a_re.reshape(1, GP), ssm_a_im.reshape(1, GP)
    bT_re = ssm_b_re[0].transpose(2, 0, 1).reshape(C, GP)
    bT_im = ssm_b_im[0].transpose(2, 0, 1).reshape(C, GP)
    cT_re = ssm_c_re[0].transpose(1, 0, 2).reshape(C, GP)
    cT_im = ssm_c_im[0].transpose(1, 0, 2).reshape(C, GP)
    s5_in = (a_re, a_im, ssm_log_dt, bT_re, bT_im, cT_re, cT_im)

    u = mm(h0, Wu, mode="nn", name="proj_u", tk=1024)
    shard = {n: W[n][0].astype(bf16) for n in BIG[1:]}
    qkv, (g_glu, g_out, g_xq, g_xkv, g_xo) = mm(
        h0, Wqkv, mode="nn", name="proj_qkv", tm=1024,
        exchange=([shard[n] for n in ("ssm_w_glu", "w_out", "xa_w_q", "xa_w_kv", "xa_w_o")], False), **big)
    qn, kn, vb = qkv_prep(qkv, gq_pad, gk_pad)
    o, (g_down,) = sb_fwd(qn, kn, vb, ([shard["w_down"]], False))
    Wglu = g_glu.reshape(-1, g_glu.shape[-1])
    Wout = g_out.reshape(-1, g_out.shape[-1])
    Wo_ssm, Wo_sb = Wout[:SBW], _pad_rows(Wout[SBW:])
    Wxq = g_xq.reshape(-1, g_xq.shape[-1])
    Wxkv = g_xkv.reshape(-1, g_xkv.shape[-1])
    Wxo = g_xo.transpose(1, 0, 2).reshape(g_xo.shape[1], -1)
    abr, abi, Bm, Cm = s5_prep(*s5_in)
    n_pos = S // SCAN_CHUNKS
    u_il = row_shuffle(u, n_pos, SCAN_CHUNKS, name="u_interleave", out_dtype=bf16)
    st, (g_up,) = scan_fwd(u_il, Bm, abr, abi, ([shard["w_up"]], False))
    ypre0_il = mm(st, Cm, mode="nt", name="s5_y", a_shards=2, b_shards=2, tm=1024, tk=2048)
    Wup = g_up.transpose(1, 0, 2).reshape(g_up.shape[1], -1)
    Wdown = g_down.reshape(-1, g_down.shape[-1])
    ypre0 = row_shuffle(ypre0_il, SCAN_CHUNKS, n_pos, name="y_token_order")
    ypre, z, zn = glu_fwd(ypre0, u, ssm_d, Wglu, g_out_ssm)
    on = rms_norm(o, gosb_pad, SBW, name="norm_o")
    x1a = mm(zn, Wo_ssm, mode="nn", name="out_ssm", epi="add", aux=xs, tn=1024)
    x1 = mm(on, Wo_sb, mode="nn", name="out_sb", epi="add", aux=x1a, **big)
    h1 = rms_norm(x1, g_xa, D, name="norm_x1")
    qx = mm(h1, Wxq, mode="nn", name="xa_q", tk=1024)
    memn = rms_norm(mems, g_mem, D, name="norm_mem")
    kv = mm(memn, Wxkv, mode="nn", name="xa_kv", **big)
    ox = xa_fwd(qx, kv, xa_g_q, xa_g_k)
    x2 = mm(ox, Wxo, mode="nn", name="xa_o", epi="add", aux=x1, tn=1024)
    h2 = rms_norm(x2, g_mlp, D, name="norm_x2")
    act = mm(h2, Wup, mode="nn", name="mlp_up", out_dtype=bf16, tm=1024, tn=2048, tk=1024)
    dx3, loss_part, dx3b = mm(act, Wdown, mode="nn", name="mlp_down", pro="relu2", epi="add", aux=x2,
                              loss_target=tgt, tm=1024, **big)
    loss = lax.psum(loss_part[0, 0], ("x", "y", "c"))

    dact = mm(dx3b, Wdown, mode="nt", name="d_act", epi="mul2relu", aux=act, out_dtype=bf16, tm=1024, tn=2048,
              tk=1024)
    dWdown = mm(act, dx3b, mode="tn", name="dw_down", pro="relu2", **wide)
    dWup = mm(h2, dact, mode="tn", name="dw_up", out_shards=N_CHIPS, **wide)
    dx2, dg_mlp, dx2b = mm(dact, Wup, mode="nt", name="d_h2", norm_bwd=(x2, g_mlp, dx3, D), tm=1024, **big)
    dox = mm(dx2b, Wxo, mode="nt", name="d_ox", out_dtype=bf16, tk=1024)
    dWxo = mm(ox, dx2b, mode="tn", name="dw_xo", out_shards=N_CHIPS, tk=1024)
    dqx, dkv, dg_xq, dg_xk = xa_bwd(qx, kv, xa_g_q, xa_g_k, dox)
    dWxq = mm(h1, dqx, mode="tn", name="dw_xq", tm=1024, tk=1024)
    dx1, dg_xa, dx1b = mm(dqx, Wxq, mode="nt", name="d_h1", norm_bwd=(x1, g_xa, dx2, D), tn=1024)
    dWxkv = mm(memn, dkv, mode="tn", name="dw_xkv", tm=1024, tn=1024)
    dmemn = mm(dkv, Wxkv, mode="nt", name="d_memn", **big)
    _, dg_mem = rms_bwd(mems, g_mem, dmemn, None, D, name="rms_bwd_mem")
    dyn_ssm = mm(dx1b, Wo_ssm, mode="nt", name="d_yn_ssm", tk=1024)
    dyn_sb = mm(dx1b, Wo_sb, mode="nt", name="d_yn_sb", **big)
    dWo_ssm = mm(zn, dx1b, mode="tn", name="dw_out_ssm", **big)
    dWo_sb = mm(on, dx1b, mode="tn", name="dw_out_sb", tm=1024, **big)
    dz, dg_os = rms_bwd(z, g_out_ssm, dyn_ssm, None, SBW, name="rms_bwd_ssm")
    do, dg_osb = rms_bwd(o, gosb_pad, dyn_sb, None, SBW, name="rms_bwd_sb")
    c_arr = c_idx.astype(jnp.int32).reshape(1)

    def sibling_sums(grads, names, tag):
        g4 = [g.reshape(N_CHIPS, 2, g.shape[1] // 2, g.shape[2]) for g in grads]
        from_sib = sibling_swap(g4, half=True, name="grad_to_sibling_" + tag)
        return [add_half(a, b, c_arr, name="add_sibling_" + n) for a, b, n in zip(g4, from_sib, names)]

    early = ("xa_w_q", "xa_w_kv", "xa_w_o", "w_up", "w_down")
    early_g = [dWxq.reshape(N_CHIPS, -1, dWxq.shape[1]), dWxkv.reshape(N_CHIPS, -1, dWxkv.shape[1]), dWxo, dWup,
               dWdown.reshape(N_CHIPS, -1, D)]
    pair = sibling_sums(early_g, early, "early")
    dqn, dkn, dv, parts_mlp = sb_bwd(qn, kn, vb, do, (pair[3:], True))
    dqkv, dg_q, dg_k, parts_xa = qkv_bwd(qkv, gq_pad, gk_pad, dqn, dkn, dv, (pair[:3], True))
    dypre, du_skip, dWglu, dd = glu_bwd(ypre, u, ssm_d, Wglu, dz)
    mid = ("ssm_w_glu", "w_out")
    mid_g = [dWglu.reshape(N_CHIPS, -1, dWglu.shape[1]),
             jnp.concatenate([dWo_ssm, _unpad_rows(dWo_sb)]).reshape(N_CHIPS, -1, D)]
    dypre_il = row_shuffle(dypre, n_pos, SCAN_CHUNKS, name="dy_interleave", out_dtype=bf16)
    dCm = mm(dypre_il, st, mode="tn", name="d_cmat", b_shards=2, out_shards=2, **wide)
    gst, dabr, dabi, parts_mid = scan_bwd(dypre_il, Cm, st, abr, abi, (sibling_sums(mid_g, mid, "mid"), True))
    dBm = mm(u_il, gst, mode="tn", name="d_bmat", b_shards=2, out_shards=2, **wide)
    du_il = mm(gst, Bm, mode="nt", name="d_u", a_shards=2, b_shards=2, tm=1024, tk=2048)
    mine = {n: sum_chips(p, name="sum_chips_" + n)
            for n, p in zip(early + mid, [*parts_xa, *parts_mlp, *parts_mid])}
    du = row_shuffle(du_il, SCAN_CHUNKS, n_pos, name="du_token_order", add=du_skip, out_dtype=bf16)
    s5_g = s5_prep_bwd(*s5_in, dabr, dabi, dBm, dCm)
    dWu = mm(h0, du, mode="tn", name="dw_u", tm=1024, tk=1024)
    dWqkv = mm(h0, dqkv, mode="tn", name="dw_qkv", **wide)
    HW = SB_HEADS * LANES
    w_in_g = jnp.stack([dWu, _unpad_cols(dWqkv[:, :HW]), _unpad_cols(dWqkv[:, HW:2 * HW]),
                        _unpad_cols(dWqkv[:, 2 * HW:])])
    dh0a = mm(du, Wu, mode="nt", name="d_h0_u", tn=1024)
    (dx, dg_mix, _), parts_in = mm(dqkv, Wqkv, mode="nt", name="d_h0_qkv", epi="add", aux=dh0a,
                                   norm_bwd=(xs, g_mix, dx1, D), tm=1024,
                                   exchange=(sibling_sums([w_in_g], ("w_in",), "w_in"), True), **big)
    mine["w_in"] = sum_chips(parts_in[0], name="sum_chips_w_in")

    late = ("small",)
    late_g = []
    da_re, da_im, dldt, dbT_re, dbT_im, dcT_re, dcT_im = s5_g
    small_g = {
        "g_mix": dg_mix, "ssm_a_re": da_re, "ssm_a_im": da_im, "ssm_log_dt": dldt,
        "ssm_b_re": dbT_re.reshape(C, G, P).transpose(1, 2, 0), "ssm_b_im": dbT_im.reshape(C, G, P).transpose(1, 2, 0),
        "ssm_c_re": dcT_re.reshape(C, G, P).transpose(1, 0, 2), "ssm_c_im": dcT_im.reshape(C, G, P).transpose(1, 0, 2),
        "ssm_d": dd, "sb_g_q": dg_q[:, :SB_HEAD_DIM], "sb_g_k": dg_k[:, :SB_HEAD_DIM], "g_out_ssm": dg_os,
        "g_out_sb": _unpad_cols(dg_osb), "g_xa": dg_xa, "g_mem": dg_mem, "xa_g_q": dg_xq, "xa_g_k": dg_xk,
        "g_mlp": dg_mlp,
    }
    late_g.append(_pack([small_g[n] for n in SMALL]).reshape(N_CHIPS, -1, LANES))

    parts_late = chip_scatter(sibling_sums(late_g, late, "late"), name="grad_to_chips_late")
    mine.update({n: sum_chips(p, name="sum_chips_" + n) for n, p in zip(late, parts_late)})
    mine = [mine[n] for n in list(BIG) + ["small"]]
    other = sibling_swap(mine, half=False, name="grad_half_to_sibling")
    shard = [jnp.where(c_idx == 0, jnp.concatenate([a, b]), jnp.concatenate([b, a])) for a, b in zip(mine, other)]
    small_all = allgather_chips([shard[-1]], name="gather_small")[0]
    small_red = small_all.reshape(-1, LANES)

    out = {}
    for n, gs in zip(BIG, shard[:-1]):
        shp = W[n].shape
        w2, m2, v2 = (t.reshape(gs.shape) for t in (W[n], M1[n], V2[n]))
        d, nm, nv = adamw(w2, gs, m2, v2, name="adamw_" + n)
        out[n] = tuple(t.reshape(shp) for t in (gs, d, nm, nv))
    shapes = [W[n].shape for n in SMALL]
    d, nm, nv = adamw(_pack([W[n] for n in SMALL]), small_red, _pack([M1[n] for n in SMALL]),
                      _pack([V2[n] for n in SMALL]), name="adamw_small")
    for n, gs, dd_, mm_, vv_ in zip(SMALL, _unpack(small_red, shapes), _unpack(d, shapes), _unpack(nm, shapes),
                                    _unpack(nv, shapes)):
        out[n] = (gs, dd_, mm_, vv_)
    res = [loss, dx[None]]
    for kind in range(4):
        res += [out[n][kind] for n in WEIGHTS]
    return tuple(res)
```

```python
import jax
import jax.numpy as jnp
from jax import lax
from jax.experimental import pallas as pl
from jax.experimental.pallas import tpu as pltpu

f32 = jnp.float32
bf16 = jnp.bfloat16

NORM_EPS = 1e-6
SSM_GROUPS = 32
SSM_GROUP = 16
SSM_STATE = 64
SB_HEADS = 8
SB_HEAD_DIM = 64
XA_HEADS = 4
XA_HEAD_DIM = 128
LANES = 128
SUBLANES = 8
N_CHIPS = 4
ADAM_LR = 0.001
ADAM_B1 = 0.9
ADAM_B2 = 0.999
ADAM_EPS = 1e-08
ADAM_WD = 0.01
ADAM_STEP = 10
VMEM_LIMIT = 56 * 1024 * 1024
MESH = pl.DeviceIdType.MESH
ANY = pl.BlockSpec(memory_space=pl.ANY)


def _cp(sem=None):
    return pltpu.CompilerParams(dimension_semantics=sem, vmem_limit_bytes=VMEM_LIMIT)


def _tile(n, pref):
    if n <= pref:
        return n
    t = (pref // LANES) * LANES
    while t > LANES and n % t:
        t -= LANES
    assert n % t == 0, (n, pref)
    return t


def _row_tile(n, pref):
    if n <= pref:
        return n
    t = (pref // SUBLANES) * SUBLANES
    while n % t:
        t -= SUBLANES
    return t


def _dot(a, b, dims):
    return lax.dot_general(a.astype(bf16), b.astype(bf16), (dims, ((), ())), preferred_element_type=f32)


_NN = ((1,), (0,))
_NT = ((1,), (1,))
_TN = ((0,), (0,))


@jax.custom_vjp
def bdot_nn(a, b):
    return _dot(a, b, _NN)


def _bdot_nn_fwd(a, b):
    return _dot(a, b, _NN), (a, b)


def _bdot_nn_bwd(res, g):
    a, b = res
    return _dot(g, b, _NT), _dot(a, g, _TN)


bdot_nn.defvjp(_bdot_nn_fwd, _bdot_nn_bwd)


@jax.custom_vjp
def bdot_nt(a, b):
    return _dot(a, b, _NT)


def _bdot_nt_fwd(a, b):
    return _dot(a, b, _NT), (a, b)


def _bdot_nt_bwd(res, g):
    a, b = res
    return _dot(g, b, _NN), _dot(g, a, _TN)


bdot_nt.defvjp(_bdot_nt_fwd, _bdot_nt_bwd)


def _rms(x, g, denom):
    r = lax.rsqrt(jnp.sum(x * x, axis=-1, keepdims=True) * (1.0 / denom) + NORM_EPS)
    return x * r * g


def _opspec(block, row_of, col_of, shards, ncol_tiles):
    if shards == 1:
        return pl.BlockSpec(block, lambda i, j, k: (row_of(i, j, k), col_of(i, j, k)))
    per = ncol_tiles // shards
    return pl.BlockSpec((None,) + block,
                        lambda i, j, k: (col_of(i, j, k) // per, row_of(i, j, k), col_of(i, j, k) % per))


def _call_with_exchange(body, args, exchange, *, name, grid, in_specs, out_specs, out_shape, scratch_shapes=()):
    xs, scatter = exchange
    n, n_in, n_out, n_scr = len(xs), len(in_specs), len(out_specs), len(scratch_shapes)
    x_shapes, sems = _chip_exchange_args(xs, scatter)

    def wrapped(*refs):
        ins, x_ins = refs[:n_in], refs[n_in:n_in + n]
        outs, x_outs = refs[n_in + n:n_in + n + n_out], refs[n_in + n + n_out:n_in + 2 * n + n_out]
        scratch, x_sems = refs[n_in + 2 * n + n_out:n_in + 2 * n + n_out + n_scr], refs[n_in + 2 * n + n_out + n_scr:]
        first, last = True, True
        for d, steps in enumerate(grid):
            first = first & (pl.program_id(d) == 0)
            last = last & (pl.program_id(d) == steps - 1)

        @pl.when(first)
        def _():
            _chip_exchange(x_ins, x_outs, x_sems, scatter, True)

        body(*ins, *outs, *scratch)

        @pl.when(last)
        def _():
            _chip_exchange(x_ins, x_outs, x_sems, scatter, False)

    res = pl.pallas_call(
        wrapped, name=name, grid=grid, in_specs=list(in_specs) + [ANY] * n, out_specs=list(out_specs) + [ANY] * n,
        out_shape=list(out_shape) + x_shapes, scratch_shapes=list(scratch_shapes) + sems,
        compiler_params=_cp(("arbitrary",) * len(grid)),
    )(*args, *xs)
    return res[:n_out], res[n_out:]


def mm(a, b, *, mode, name, tm=1024, tn=512, tk=512, pro="none", epi="none", aux=None,
       out_dtype=f32, a_shards=1, b_shards=1, out_shards=1, exchange=None, norm_bwd=None, loss_target=None):
    ar, ac = a.shape[-2], a.shape[-1] * a_shards
    br, bc = b.shape[-2], b.shape[-1] * b_shards
    if mode == "nn":
        M, K, N = ar, ac, bc
        assert br == K
    elif mode == "nt":
        M, K, N = ar, ac, br
        assert bc == K
    else:
        M, K, N = ac, ar, bc
        assert br == K
    tm, tn, tk = _tile(M, tm), _tile(N, tn), _tile(K, tk)
    if a_shards > 1:
        if mode == "tn":
            tm = _tile(ac // a_shards, tm)
        else:
            tk = _tile(ac // a_shards, tk)
    if b_shards > 1:
        if mode == "nt":
            tk = _tile(bc // b_shards, tk)
        else:
            tn = _tile(bc // b_shards, tn)
    if out_shards > 1:
        tn = _tile(N // out_shards, tn)
    nm, nn_, nk = M // tm, N // tn, K // tk
    I = lambda i, j, k: i
    J = lambda i, j, k: j
    Kk = lambda i, j, k: k
    if mode == "nn":
        a_spec = _opspec((tm, tk), I, Kk, a_shards, nk)
        b_spec = _opspec((tk, tn), Kk, J, b_shards, nn_)
        dims = _NN
    elif mode == "nt":
        a_spec = _opspec((tm, tk), I, Kk, a_shards, nk)
        b_spec = _opspec((tn, tk), J, Kk, b_shards, nk)
        dims = _NT
    else:
        a_spec = _opspec((tk, tm), Kk, I, a_shards, nm)
        b_spec = _opspec((tk, tn), Kk, J, b_shards, nn_)
        dims = _TN
    in_specs = [a_spec, b_spec]
    args = [a, b]
    tile = pl.BlockSpec((tm, tn), lambda i, j, k: (i, j))
    if epi != "none":
        in_specs.append(tile)
        args.append(aux)
    if out_shards == 1:
        out_spec = tile
        out_shape = jax.ShapeDtypeStruct((M, N), out_dtype)
    else:
        per = nn_ // out_shards
        out_spec = pl.BlockSpec((None, tm, tn), lambda i, j, k: (j // per, i, j % per))
        out_shape = jax.ShapeDtypeStruct((out_shards, M, N // out_shards), out_dtype)
    out_specs, out_shapes = [out_spec], [out_shape]
    if norm_bwd is not None:
        nx, ng, nres, denom = norm_bwd
        assert tn == N and out_shards == 1 and out_dtype == f32
        vec = pl.BlockSpec((1, tn), lambda i, j, k: (0, 0))
        in_specs += [tile, vec, tile]
        args += [nx, ng, nres]
        out_specs += [vec, tile]
        out_shapes += [jax.ShapeDtypeStruct((1, N), f32), jax.ShapeDtypeStruct((M, N), bf16)]
    if loss_target is not None:
        assert tn == N and out_shards == 1 and out_dtype == f32 and norm_bwd is None
        in_specs.append(tile)
        args.append(loss_target)
        out_specs += [pl.BlockSpec((1, 1), lambda i, j, k: (0, 0)), tile]
        out_shapes += [jax.ShapeDtypeStruct((1, 1), f32), jax.ShapeDtypeStruct((M, N), bf16)]
    n_in = len(in_specs)

    def body(*refs):
        a_ref, b_ref = refs[0], refs[1]
        aux_ref = refs[2] if epi != "none" else None
        o_ref, acc_ref = refs[n_in], refs[-1]
        i, k = pl.program_id(0), pl.program_id(2)

        @pl.when(k == 0)
        def _():
            acc_ref[...] = jnp.zeros_like(acc_ref)

        av = a_ref[...]
        if pro == "relu2":
            av = jnp.square(jnp.maximum(av.astype(f32), 0.0))
        acc_ref[...] += _dot(av, b_ref[...], dims)

        @pl.when(k == nk - 1)
        def _():
            res = acc_ref[...]
            if epi == "add":
                res = res + aux_ref[...].astype(f32)
            elif epi == "mul2relu":
                res = res * (2.0 * jnp.maximum(aux_ref[...].astype(f32), 0.0))
            if loss_target is not None:
                l_ref, twin_ref = refs[n_in + 1], refs[n_in + 2]
                err = res - refs[n_in - 1][...]
                dy = err * (1.0 / N)
                o_ref[...] = dy
                twin_ref[...] = dy.astype(bf16)

                @pl.when(i == 0)
                def _():
                    l_ref[...] = jnp.zeros_like(l_ref)

                rows = jnp.sum(err * err, axis=1, keepdims=True) * (1.0 / N)
                l_ref[...] += 0.5 * jnp.sum(rows, axis=0, keepdims=True)
            elif norm_bwd is None:
                o_ref[...] = res.astype(out_dtype)
            else:
                x_ref, g_ref, res_ref = refs[n_in - 3:n_in]
                dg_ref, twin_ref = refs[n_in + 1], refs[n_in + 2]
                _, vjp = jax.vjp(lambda xv, gv: _rms(xv, gv, denom), x_ref[...], g_ref[...])
                dx, dg = vjp(res)
                dx = dx + res_ref[...]
                o_ref[...] = dx
                twin_ref[...] = dx.astype(bf16)

                @pl.when(i == 0)
                def _():
                    dg_ref[...] = jnp.zeros_like(dg_ref)

                dg_ref[...] += dg

    acc = [pltpu.VMEM((tm, tn), f32)]
    single = norm_bwd is None and loss_target is None
    if exchange is not None:
        outs, moved = _call_with_exchange(body, args, exchange, name=name, grid=(nm, nn_, nk), in_specs=in_specs,
                                          out_specs=out_specs, out_shape=out_shapes, scratch_shapes=acc)
        return (outs[0] if single else tuple(outs)), moved
    outs = pl.pallas_call(
        body, name=name, grid=(nm, nn_, nk), in_specs=in_specs, out_specs=out_specs, out_shape=out_shapes,
        scratch_shapes=acc, compiler_params=_cp(("parallel" if single else "arbitrary", "parallel", "arbitrary")),
    )(*args)
    return outs[0] if single else tuple(outs)


def rms_norm(x, g, denom, *, name, ts=1024, exchange=None):
    S, D = x.shape
    ts = _tile(S, ts)

    def body(x_ref, g_ref, h_ref):
        h_ref[...] = _rms(x_ref[...], g_ref[...], denom).astype(bf16)

    row = pl.BlockSpec((ts, D), lambda i: (i, 0))
    kw = dict(name=name, grid=(S // ts,), in_specs=[row, pl.BlockSpec((1, D), lambda i: (0, 0))])
    if exchange is not None:
        (h,), moved = _call_with_exchange(body, (x, g), exchange, out_specs=[row],
                                          out_shape=[jax.ShapeDtypeStruct((S, D), bf16)], **kw)
        return h, moved
    return pl.pallas_call(body, out_specs=row, out_shape=jax.ShapeDtypeStruct((S, D), bf16),
                          compiler_params=_cp(("parallel",)), **kw)(x, g)


def rms_bwd(x, g, dy, res, denom, *, name, ts=512, twin=False):
    S, D = x.shape
    ts = _tile(S, ts)
    has_res = res is not None

    def body(*refs):
        x_ref, g_ref, dy_ref = refs[:3]
        outs = refs[4:] if has_res else refs[3:]
        _, vjp = jax.vjp(lambda xv, gv: _rms(xv, gv, denom), x_ref[...], g_ref[...])
        dx, dg = vjp(dy_ref[...])
        if has_res:
            dx = dx + refs[3][...]
        outs[0][...] = dx
        if twin:
            outs[2][...] = dx.astype(bf16)
        dg_ref = outs[1]

        @pl.when(pl.program_id(0) == 0)
        def _():
            dg_ref[...] = jnp.zeros_like(dg_ref)

        dg_ref[...] += dg

    row = pl.BlockSpec((ts, D), lambda i: (i, 0))
    vec = pl.BlockSpec((1, D), lambda i: (0, 0))
    in_specs = [row, vec, row] + ([row] if has_res else [])
    args = [x, g, dy] + ([res] if has_res else [])
    return pl.pallas_call(
        body, name=name, grid=(S // ts,), in_specs=in_specs, out_specs=[row, vec] + ([row] if twin else []),
        out_shape=[jax.ShapeDtypeStruct((S, D), f32), jax.ShapeDtypeStruct((1, D), f32)]
        + ([jax.ShapeDtypeStruct((S, D), bf16)] if twin else []),
        compiler_params=_cp(("arbitrary",)),
    )(*args)


LOG2E = 1.4426950408889634
LN2 = 0.6931471805599453


def _qk_fn(q, k, gq, gk):
    qs, ks = [], []
    for h in range(SB_HEADS):
        sl = slice(h * LANES, (h + 1) * LANES)
        qs.append(_rms(q[:, sl], gq, SB_HEAD_DIM) * (SB_HEAD_DIM ** -0.5 * LOG2E))
        ks.append(_rms(k[:, sl], gk, SB_HEAD_DIM))
    return jnp.concatenate(qs, axis=1), jnp.concatenate(ks, axis=1)


def qkv_prep(qkv, gq, gk, *, ts=512):
    S = qkv.shape[0]
    W = SB_HEADS * LANES
    ts = _tile(S, ts)

    def body(q_ref, k_ref, v_ref, gq_ref, gk_ref, qn_ref, kn_ref, vb_ref):
        qn, kn = _qk_fn(q_ref[...], k_ref[...], gq_ref[...], gk_ref[...])
        qn_ref[...] = qn.astype(bf16)
        kn_ref[...] = kn.astype(bf16)
        vb_ref[...] = v_ref[...].astype(bf16)

    out = jax.ShapeDtypeStruct((S, W), bf16)
    gspec = pl.BlockSpec((1, LANES), lambda i: (0, 0))
    ospec = pl.BlockSpec((ts, W), lambda i: (i, 0))
    col = lambda c: pl.BlockSpec((ts, W), lambda i: (i, c))
    return pl.pallas_call(
        body, name="qkv_prep", grid=(S // ts,), in_specs=[col(0), col(1), col(2), gspec, gspec],
        out_specs=[ospec, ospec, ospec], out_shape=[out, out, out], compiler_params=_cp(("parallel",)),
    )(qkv, qkv, qkv, gq, gk)


def qkv_bwd(qkv, gq, gk, dqn, dkn, dv, exchange, *, ts=512):
    S = qkv.shape[0]
    W = SB_HEADS * LANES
    ts = _tile(S, ts)

    def body(q_ref, k_ref, gq_ref, gk_ref, dqn_ref, dkn_ref, dv_ref, o_ref, dgq_ref, dgk_ref):
        _, vjp = jax.vjp(_qk_fn, q_ref[...], k_ref[...], gq_ref[...], gk_ref[...])
        dq, dk, dgq, dgk = vjp((dqn_ref[...] * LN2, dkn_ref[...] * LN2))
        o_ref[:, 0:W] = dq.astype(bf16)
        o_ref[:, W:2 * W] = dk.astype(bf16)
        o_ref[:, 2 * W:3 * W] = dv_ref[...].astype(bf16)

        @pl.when(pl.program_id(0) == 0)
        def _():
            dgq_ref[...] = jnp.zeros_like(dgq_ref)
            dgk_ref[...] = jnp.zeros_like(dgk_ref)

        dgq_ref[...] += dgq
        dgk_ref[...] += dgk

    gspec = pl.BlockSpec((1, LANES), lambda i: (0, 0))
    row = pl.BlockSpec((ts, W), lambda i: (i, 0))
    col = lambda c: pl.BlockSpec((ts, W), lambda i: (i, c))
    (dqkv, dgq, dgk), moved = _call_with_exchange(
        body, (qkv, qkv, gq, gk, dqn, dkn, dv), exchange, name="qkv_bwd", grid=(S // ts,),
        in_specs=[col(0), col(1), gspec, gspec, row, row, row],
        out_specs=[pl.BlockSpec((ts, 3 * W), lambda i: (i, 0)), gspec, gspec],
        out_shape=[jax.ShapeDtypeStruct((S, 3 * W), bf16), jax.ShapeDtypeStruct((1, LANES), f32),
                   jax.ShapeDtypeStruct((1, LANES), f32)])
    return dqkv, dgq, dgk, moved


def _sb_weights(q, ks, R, masked, row, col, UU):
    ls = [_dot(q, k, _NT) for k in ks]
    lbs, lm0s, cats = [], [], []
    for l, diag in zip(ls, masked):
        neg_abs = pltpu.bitcast(pltpu.bitcast(l, jnp.uint32) | jnp.uint32(0x80000000), f32)
        lp = jnp.log2(1.0 + jnp.exp2(neg_abs))
        lb = jnp.minimum(l, 0.0) - lp
        lm = lb - l
        if diag:
            lm = jnp.where(col < row, lm, 0.0)
        hi = lm.astype(bf16)
        lo = (lm - hi.astype(f32)).astype(bf16)
        lbs.append(lb)
        lm0s.append(lm[:, 0:1])
        cats.append(jnp.concatenate([hi, lo], axis=1))
    sums = [_dot(c, UU, _NN) for c in cats]
    ws = []
    for lb, lm0, A, diag in zip(lbs, lm0s, sums, masked):
        w = jnp.exp2(lb + (A + R))
        if diag:
            w = jnp.where(col < row, w, 0.0)
        R = R + (A[:, 0:1] + lm0)
        ws.append(w)
    return lbs, ws, R


def _tri2(tk):
    r = lax.broadcasted_iota(jnp.int32, (2 * tk, tk), 0)
    r = jnp.where(r >= tk, r - tk, r)
    c = lax.broadcasted_iota(jnp.int32, (2 * tk, tk), 1)
    return (r > c).astype(bf16)


SB_GROUP = 8


SB_ALL_ZERO_BELOW = -160.0


def _sweep(i, blocks_of, carry, descending, right_sum=None, ran=None):
    G = SB_GROUP
    n = jnp.maximum(i - 1, 0)
    rem, full = n % G, n // G
    asc = lambda js: js if descending else js[::-1]

    def first_group(c):
        one = lambda c: blocks_of([i], c, [True])
        two = lambda c: blocks_of(asc([i, i - 1]), c, asc([True, False]))
        return lax.cond(i >= 1, two, one, c)

    def body(p, c):
        return blocks_of(asc([i - 2 - p * G - u for u in range(G)]), c, [False] * G)

    def left_over(r):
        return lambda c: blocks_of(asc([r - 1 - u for u in range(r)]), c, [False] * r) if r else c

    if descending:
        carry = first_group(carry)
        alive = lambda c: jnp.max(right_sum(c)) > SB_ALL_ZERO_BELOW
        bodies, carry = lax.while_loop(lambda s: (s[0] < full) & alive(s[1]),
                                       lambda s: (s[0] + 1, body(s[0], s[1])), (jnp.int32(0), carry))
        tail = (bodies == full) & alive(carry)
        carry = lax.switch(jnp.where(tail, rem, 0), [left_over(r) for r in range(G)], carry)
        return carry, (bodies, tail)
    bodies, tail = ran
    carry = lax.switch(jnp.where(tail, rem, 0), [left_over(r) for r in range(G)], carry)
    carry = lax.fori_loop(0, bodies, lambda t, c: body(bodies - 1 - t, c), carry)
    return first_group(carry)


def sb_fwd(qn, kn, vb, exchange, *, tq=256):
    S, W = qn.shape
    H = W // LANES
    tq = _tile(S, tq)
    tk = tq
    nq = S // tq

    def body(q_ref, k_ref, v_ref, o_ref, uu_s):
        i = pl.program_id(1)

        @pl.when((pl.program_id(0) == 0) & (i == 0))
        def _():
            uu_s[...] = _tri2(tk)

        q = q_ref[...]
        row = lax.broadcasted_iota(jnp.int32, (tq, tk), 0)
        col = lax.broadcasted_iota(jnp.int32, (tq, tk), 1)
        UU = uu_s[...]

        def blocks(js, c, masked):
            rows = [pl.ds(pl.multiple_of(j * tk, tk), tk) for j in js]
            _, ws, R = _sb_weights(q, [k_ref[r, :] for r in rows], c[0], masked, row, col, UU)
            acc = c[1]
            for w, r in zip(ws, rows):
                acc = acc + _dot(w, v_ref[r, :], _NN)
            return R, acc

        c, _ = _sweep(i, blocks, (jnp.zeros((tq, 1), f32), jnp.zeros((tq, LANES), f32)), True, lambda c: c[0])
        o_ref[...] = c[1]

    qspec = pl.BlockSpec((tq, LANES), lambda h, i: (i, h))
    kspec = pl.BlockSpec((S, LANES), lambda h, i: (0, h))
    (o,), moved = _call_with_exchange(
        body, (qn, kn, vb), exchange, name="sb_fwd", grid=(H, nq), in_specs=[qspec, kspec, kspec], out_specs=[qspec],
        out_shape=[jax.ShapeDtypeStruct((S, W), f32)], scratch_shapes=[pltpu.VMEM((2 * tk, tk), bf16)])
    return o, moved


def sb_bwd(qn, kn, vb, do, exchange, *, tq=256):
    S, W = qn.shape
    H = W // LANES
    tq = _tile(S, tq)
    tk = tq
    nq = S // tq

    def body(q_ref, k_ref, v_ref, do_ref, dq_ref, dk_ref, dv_ref, dz_s, beta_s, uu_s, ue_s):
        i = pl.program_id(1)
        row = lax.broadcasted_iota(jnp.int32, (tq, tk), 0)
        col = lax.broadcasted_iota(jnp.int32, (tq, tk), 1)

        @pl.when((pl.program_id(0) == 0) & (i == 0))
        def _():
            uu_s[...] = _tri2(tk)
            ue_s[...] = (row < col).astype(bf16)

        @pl.when(i == 0)
        def _():
            dk_ref[...] = jnp.zeros_like(dk_ref)
            dv_ref[...] = jnp.zeros_like(dv_ref)

        q = q_ref[...]
        dob = do_ref[...].astype(bf16)
        UU = uu_s[...]
        Ue = ue_s[...]

        def sweep1(js, R, masked):
            rows = [pl.ds(pl.multiple_of(j * tk, tk), tk) for j in js]
            dws = [_dot(dob, v_ref[r, :], _NT) for r in rows]
            lbs, ws, R = _sb_weights(q, [k_ref[r, :] for r in rows], R, masked, row, col, UU)
            for j, lb, w, dw in zip(js, lbs, ws, dws):
                dz_s[j] = (dw * w).astype(bf16)
                beta_s[j] = jnp.exp2(lb).astype(bf16)
            for r, w in zip(rows, ws):
                dv_ref[r, :] += _dot(w, dob, _TN)
            return R

        _, ran = _sweep(i, sweep1, jnp.zeros((tq, 1), f32), True, lambda R: R)

        def sweep2(js, c, masked):
            rows = [pl.ds(pl.multiple_of(j * tk, tk), tk) for j in js]
            dzbs = [dz_s[j] for j in js]
            sums = [_dot(dzb, Ue, _NN) for dzb in dzbs]
            Lz, dq = c
            dlbs = []
            for j, dzb, Cz, diag in zip(js, dzbs, sums, masked):
                dz = dzb.astype(f32)
                dl = dz - beta_s[j].astype(f32) * (dz + (Cz + Lz))
                if diag:
                    dl = jnp.where(col < row, dl, 0.0)
                Lz = Lz + (Cz[:, tk - 1:tk] + dz[:, tk - 1:tk])
                dlbs.append(dl.astype(bf16))
            for r, dlb in zip(rows, dlbs):
                dq = dq + _dot(dlb, k_ref[r, :], _NN)
            for r, dlb in zip(rows, dlbs):
                dk_ref[r, :] += _dot(dlb, q, _TN)
            return Lz, dq

        c = _sweep(i, sweep2, (jnp.zeros((tq, 1), f32), jnp.zeros((tq, LANES), f32)), False, ran=ran)
        dq_ref[...] = c[1]

    qspec = pl.BlockSpec((tq, LANES), lambda h, i: (i, h))
    kspec = pl.BlockSpec((S, LANES), lambda h, i: (0, h))
    full = jax.ShapeDtypeStruct((S, W), f32)
    (dq, dk, dv), moved = _call_with_exchange(
        body, (qn, kn, vb, do), exchange, name="sb_bwd", grid=(H, nq), in_specs=[qspec, kspec, kspec, qspec],
        out_specs=[qspec, kspec, kspec], out_shape=[full, full, full],
        scratch_shapes=[pltpu.VMEM((nq, tq, tk), bf16), pltpu.VMEM((nq, tq, tk), bf16),
                        pltpu.VMEM((2 * tk, tk), bf16), pltpu.VMEM((tk, tk), bf16)])
    return dq, dk, dv, moved


def _xa_fn(qx, kv, gq, gk):
    XW = XA_HEADS * XA_HEAD_DIM
    outs = []
    for h in range(XA_HEADS):
        sl = slice(h * XA_HEAD_DIM, (h + 1) * XA_HEAD_DIM)
        qn = _rms(qx[:, sl], gq, XA_HEAD_DIM)
        kn = _rms(kv[:, sl], gk, XA_HEAD_DIM)
        v = kv[:, XW + h * XA_HEAD_DIM:XW + (h + 1) * XA_HEAD_DIM]
        s = bdot_nt(qn, kn) * (XA_HEAD_DIM ** -0.5)
        e = jnp.exp(s - lax.stop_gradient(jnp.max(s, axis=-1, keepdims=True)))
        p = e / jnp.sum(e, axis=-1, keepdims=True)
        outs.append(bdot_nn(p, v))
    return jnp.concatenate(outs, axis=1)


def xa_fwd(qx, kv, gq, gk, *, ts=512):
    S, XW = qx.shape
    M = kv.shape[0]
    ts = _tile(S, ts)

    def body(q_ref, kv_ref, gq_ref, gk_ref, o_ref):
        o_ref[...] = _xa_fn(q_ref[...], kv_ref[...], gq_ref[...], gk_ref[...]).astype(bf16)

    row = pl.BlockSpec((ts, XW), lambda i: (i, 0))
    gspec = pl.BlockSpec((1, XA_HEAD_DIM), lambda i: (0, 0))
    return pl.pallas_call(
        body, name="xa_fwd", grid=(S // ts,),
        in_specs=[row, pl.BlockSpec((M, 2 * XW), lambda i: (0, 0)), gspec, gspec], out_specs=row,
        out_shape=jax.ShapeDtypeStruct((S, XW), bf16), compiler_params=_cp(("parallel",)),
    )(qx, kv, gq, gk)


def xa_bwd(qx, kv, gq, gk, do, *, ts=512):
    S, XW = qx.shape
    M = kv.shape[0]
    ts = _tile(S, ts)

    def body(q_ref, kv_ref, gq_ref, gk_ref, do_ref, dq_ref, dkv_ref, dgq_ref, dgk_ref):
        _, vjp = jax.vjp(_xa_fn, q_ref[...], kv_ref[...], gq_ref[...], gk_ref[...])
        dq, dkv, dgq, dgk = vjp(do_ref[...].astype(f32))
        dq_ref[...] = dq.astype(bf16)

        @pl.when(pl.program_id(0) == 0)
        def _():
            dkv_ref[...] = jnp.zeros_like(dkv_ref)
            dgq_ref[...] = jnp.zeros_like(dgq_ref)
            dgk_ref[...] = jnp.zeros_like(dgk_ref)

        dkv_ref[...] += dkv
        dgq_ref[...] += dgq
        dgk_ref[...] += dgk

    row = pl.BlockSpec((ts, XW), lambda i: (i, 0))
    gspec = pl.BlockSpec((1, XA_HEAD_DIM), lambda i: (0, 0))
    kvspec = pl.BlockSpec((M, 2 * XW), lambda i: (0, 0))
    gshape = jax.ShapeDtypeStruct((1, XA_HEAD_DIM), f32)
    return pl.pallas_call(
        body, name="xa_bwd", grid=(S // ts,), in_specs=[row, kvspec, gspec, gspec, row],
        out_specs=[row, kvspec, gspec, gspec],
        out_shape=[jax.ShapeDtypeStruct((S, XW), bf16), jax.ShapeDtypeStruct((M, 2 * XW), f32), gshape, gshape],
        compiler_params=_cp(("arbitrary",)),
    )(qx, kv, gq, gk, do)


def _s5_prep_fn(a_re, a_im, ldt, bT_re, bT_im, cT_re, cT_im):
    G, P, C = SSM_GROUPS, SSM_STATE, SSM_GROUP
    GP, GC = G * P, G * C
    lg_p, lg_c = P.bit_length() - 1, C.bit_length() - 1
    gi = lax.broadcasted_iota(jnp.int32, (G, GP), 0)
    ci = lax.broadcasted_iota(jnp.int32, (G, GP), 1) >> lg_p
    expand_dt = (gi == ci).astype(f32)
    dte = jnp.dot(jnp.exp(ldt), expand_dt, precision=lax.Precision.HIGHEST, preferred_element_type=f32)
    zr, zi = a_re * dte, a_im * dte
    mag = jnp.exp(zr)
    abr, abi = mag * jnp.cos(zi), mag * jnp.sin(zi)
    nr, ni = abr - 1.0, abi
    den = a_re * a_re + a_im * a_im
    cr = (nr * a_re + ni * a_im) / den
    cim = (ni * a_re - nr * a_im) / den
    bbr = cr * bT_re - cim * bT_im
    bbi = cr * bT_im + cim * bT_re
    rowg = lax.broadcasted_iota(jnp.int32, (GC, GP), 0) >> lg_c
    colg = lax.broadcasted_iota(jnp.int32, (GC, GP), 1) >> lg_p
    diag = rowg == colg

    def expand(t):
        return jnp.where(diag, jnp.broadcast_to(t[None], (G, C, GP)).reshape(GC, GP), 0.0)

    return abr, abi, expand(bbr), expand(bbi), expand(cT_re), expand(-cT_im)


def s5_prep(a_re, a_im, ldt, bT_re, bT_im, cT_re, cT_im):
    GP, GC = SSM_GROUPS * SSM_STATE, SSM_GROUPS * SSM_GROUP

    def body(a_re_ref, a_im_ref, ldt_ref, bTr_ref, bTi_ref, cTr_ref, cTi_ref, abr_ref, abi_ref, B_ref, C_ref):
        abr, abi, Br, Bi, Cr, Ci = _s5_prep_fn(a_re_ref[...], a_im_ref[...], ldt_ref[...], bTr_ref[...],
                                               bTi_ref[...], cTr_ref[...], cTi_ref[...])
        abr_ref[...] = abr
        abi_ref[...] = abi
        B_ref[0] = Br.astype(bf16)
        B_ref[1] = Bi.astype(bf16)
        C_ref[0] = Cr.astype(bf16)
        C_ref[1] = Ci.astype(bf16)

    vec = jax.ShapeDtypeStruct((1, GP), f32)
    mat = jax.ShapeDtypeStruct((2, GC, GP), bf16)
    return pl.pallas_call(body, name="s5_prep", out_shape=[vec, vec, mat, mat], compiler_params=_cp())(
        a_re, a_im, ldt, bT_re, bT_im, cT_re, cT_im)


def s5_prep_bwd(a_re, a_im, ldt, bT_re, bT_im, cT_re, cT_im, dabr, dabi, dB, dC):
    def body(a_re_ref, a_im_ref, ldt_ref, bTr_ref, bTi_ref, cTr_ref, cTi_ref, dabr_ref, dabi_ref, dB_ref, dC_ref,
             *outs):
        _, vjp = jax.vjp(_s5_prep_fn, a_re_ref[...], a_im_ref[...], ldt_ref[...], bTr_ref[...], bTi_ref[...],
                         cTr_ref[...], cTi_ref[...])
        grads = vjp((dabr_ref[...], dabi_ref[...], dB_ref[0], dB_ref[1], dC_ref[0], dC_ref[1]))
        for o_ref, gv in zip(outs, grads):
            o_ref[...] = gv

    ins = (a_re, a_im, ldt, bT_re, bT_im, cT_re, cT_im)
    return pl.pallas_call(body, name="s5_prep_bwd", out_shape=[jax.ShapeDtypeStruct(v.shape, f32) for v in ins],
                          compiler_params=_cp())(*ins, dabr, dabi, dB, dC)


def _cmul(ar, ai, br, bi):
    return ar * br - ai * bi, ar * bi + ai * br


SCAN_CHUNKS = 32


def _chunk_carry(Lr, Li, Pr, Pi, scratch, reverse):
    lr_ref, li_ref, cr_ref, ci_ref = scratch
    lr_ref[...] = Lr
    li_ref[...] = Li
    cur_r = jnp.zeros((1, LANES), f32)
    cur_i = jnp.zeros((1, LANES), f32)
    order = range(SCAN_CHUNKS - 1, -1, -1) if reverse else range(SCAN_CHUNKS)
    for c in order:
        cr_ref[pl.ds(c, 1), :] = cur_r
        ci_ref[pl.ds(c, 1), :] = cur_i
        mr, mi = _cmul(Pr, Pi, cur_r, cur_i)
        cur_r, cur_i = lr_ref[pl.ds(c, 1), :] + mr, li_ref[pl.ds(c, 1), :] + mi
    return cr_ref[...], ci_ref[...]


def _chunk_rows(j):
    return pl.ds(pl.multiple_of(j * SCAN_CHUNKS, SCAN_CHUNKS), SCAN_CHUNKS)


def row_shuffle(x, a, b, *, name, add=None, out_dtype=f32):
    S, W = x.shape
    assert a * b == S and x.dtype == f32

    def body(*refs):
        x_ref, o_ref = refs[0], refs[-1]

        def step(i, _):
            dst = pl.ds(pl.multiple_of(i * b, b), b)
            v = x_ref[pl.ds(i, b, stride=a), :]
            if add is not None:
                v = v + refs[1][dst, :]
            o_ref[dst, :] = v.astype(out_dtype)
            return 0

        lax.fori_loop(0, a, step, 0)

    col = pl.BlockSpec((S, LANES), lambda t: (0, t))
    args = [x] + ([add] if add is not None else [])
    return pl.pallas_call(
        body, name=name, grid=(W // LANES,), in_specs=[col] * len(args), out_specs=col,
        out_shape=jax.ShapeDtypeStruct((S, W), out_dtype), compiler_params=_cp(("parallel",)),
    )(*args)


def _scan_scratch(n):
    small = pltpu.VMEM((SCAN_CHUNKS, LANES), f32)
    return [pltpu.VMEM((n, LANES), f32), pltpu.VMEM((n, LANES), f32), small, small, small, small]


def _drive(x_ref, m_ref, work_ref):
    S = x_ref.shape[0]
    rows = min(S, 1024)
    m = jnp.concatenate([m_ref[0], m_ref[1]], axis=1)

    def chunk(c, _):
        r = pl.ds(pl.multiple_of(c * rows, rows), rows)
        y = _dot(x_ref[r, :], m, _NN)
        work_ref[0, r, :] = y[:, :LANES]
        work_ref[1, r, :] = y[:, LANES:]
        return 0

    lax.fori_loop(0, S // rows, chunk, 0)


def scan_fwd(u_il, Bm, abr, abi, exchange):
    S, C = u_il.shape
    N = Bm.shape[2]
    n = S // SCAN_CHUNKS
    shp = (SCAN_CHUNKS, LANES)

    def body(u_ref, B_ref, ar_ref, ai_ref, st_ref, work_ref, pwr_ref, pwi_ref, *scratch):
        a1r, a1i = ar_ref[...], ai_ref[...]
        ar = jnp.broadcast_to(a1r, shp)
        ai = jnp.broadcast_to(a1i, shp)
        _drive(u_ref, B_ref, work_ref)
        sr_ref, si_ref = work_ref.at[0], work_ref.at[1]

        def step(j, c):
            sr, si, pr, pi = c
            rows = _chunk_rows(j)
            mr, mi = _cmul(ar, ai, sr, si)
            sr, si = mr + sr_ref[rows, :], mi + si_ref[rows, :]
            sr_ref[rows, :] = sr
            si_ref[rows, :] = si
            pwr_ref[pl.ds(j, 1), :] = pr
            pwi_ref[pl.ds(j, 1), :] = pi
            npr, npi = _cmul(a1r, a1i, pr, pi)
            return sr, si, npr, npi

        z = jnp.zeros(shp, f32)
        sr, si, _, _ = lax.fori_loop(0, n, step, (z, z, a1r, a1i), unroll=2)
        cr, ci = _chunk_carry(sr, si, pwr_ref[pl.ds(n - 1, 1), :], pwi_ref[pl.ds(n - 1, 1), :], scratch, False)

        def step2(j, _):
            rows = _chunk_rows(j)
            pr = jnp.broadcast_to(pwr_ref[pl.ds(j, 1), :], shp)
            pi = jnp.broadcast_to(pwi_ref[pl.ds(j, 1), :], shp)
            mr, mi = _cmul(pr, pi, cr, ci)
            st_ref[0, rows, :] = (sr_ref[rows, :] + mr).astype(bf16)
            st_ref[1, rows, :] = (si_ref[rows, :] + mi).astype(bf16)
            return 0

        lax.fori_loop(0, n, step2, 0, unroll=4)

    blk = pl.BlockSpec((2, S, LANES), lambda t: (0, 0, t))
    vec = pl.BlockSpec((1, LANES), lambda t: (0, t))
    (st,), moved = _call_with_exchange(
        body, (u_il, Bm, abr, abi), exchange, name="scan_fwd", grid=(N // LANES,),
        in_specs=[pl.BlockSpec((S, C), lambda t: (0, 0)), pl.BlockSpec((2, C, LANES), lambda t: (0, 0, t)), vec, vec],
        out_specs=[blk], out_shape=[jax.ShapeDtypeStruct((2, S, N), bf16)],
        scratch_shapes=[pltpu.VMEM((2, S, LANES), f32)] + _scan_scratch(n))
    return st, moved


def scan_bwd(dy_il, Cm, st, abr, abi, exchange):
    _, S, N = st.shape
    C = dy_il.shape[1]
    n = S // SCAN_CHUNKS
    shp = (SCAN_CHUNKS, LANES)

    def body(dy_ref, C_ref, st_ref, ar_ref, ai_ref, g_ref, dar_ref, dai_ref, work_ref, qwr_ref, qwi_ref, *scratch):
        a1r, a1i = ar_ref[...], -ai_ref[...]
        ar = jnp.broadcast_to(a1r, shp)
        nai = jnp.broadcast_to(a1i, shp)
        _drive(dy_ref, C_ref, work_ref)
        gr_ref, gi_ref = work_ref.at[0], work_ref.at[1]
        sr_ref, si_ref = st_ref.at[0], st_ref.at[1]

        def step(jj, c):
            gr, gi, qr, qi = c
            j = n - 1 - jj
            rows = _chunk_rows(j)
            mr, mi = _cmul(ar, nai, gr, gi)
            gr, gi = mr + gr_ref[rows, :], mi + gi_ref[rows, :]
            gr_ref[rows, :] = gr
            gi_ref[rows, :] = gi
            qwr_ref[pl.ds(j, 1), :] = qr
            qwi_ref[pl.ds(j, 1), :] = qi
            nqr, nqi = _cmul(a1r, a1i, qr, qi)
            return gr, gi, nqr, nqi

        z = jnp.zeros(shp, f32)
        gr, gi, _, _ = lax.fori_loop(0, n, step, (z, z, a1r, a1i), unroll=2)
        cr, ci = _chunk_carry(gr, gi, qwr_ref[pl.ds(0, 1), :], qwi_ref[pl.ds(0, 1), :], scratch, True)
        sub = lax.broadcasted_iota(jnp.int32, shp, 0)

        def fix(j, spr, spi, acc):
            rows = _chunk_rows(j)
            qr = jnp.broadcast_to(qwr_ref[pl.ds(j, 1), :], shp)
            qi = jnp.broadcast_to(qwi_ref[pl.ds(j, 1), :], shp)
            mr, mi = _cmul(qr, qi, cr, ci)
            gr = gr_ref[rows, :] + mr
            gi = gi_ref[rows, :] + mi
            g_ref[0, rows, :] = gr.astype(bf16)
            g_ref[1, rows, :] = gi.astype(bf16)
            return acc[0] + gr * spr + gi * spi, acc[1] + gi * spr - gr * spi

        last = _chunk_rows(n - 1)
        spr = jnp.where(sub == 0, 0.0, pltpu.roll(sr_ref[last, :].astype(f32), 1, 0))
        spi = jnp.where(sub == 0, 0.0, pltpu.roll(si_ref[last, :].astype(f32), 1, 0))
        acc = fix(0, spr, spi, (z, z))

        def step2(j, acc):
            prev = _chunk_rows(j - 1)
            return fix(j, sr_ref[prev, :].astype(f32), si_ref[prev, :].astype(f32), acc)

        acc = lax.fori_loop(1, n, step2, acc)
        dar_ref[...] = jnp.sum(acc[0], axis=0, keepdims=True)
        dai_ref[...] = jnp.sum(acc[1], axis=0, keepdims=True)

    blk = pl.BlockSpec((2, S, LANES), lambda t: (0, 0, t))
    vec = pl.BlockSpec((1, LANES), lambda t: (0, t))
    vshape = jax.ShapeDtypeStruct((1, N), f32)
    (g, dar, dai), moved = _call_with_exchange(
        body, (dy_il, Cm, st, abr, abi), exchange, name="scan_bwd", grid=(N // LANES,),
        in_specs=[pl.BlockSpec((S, C), lambda t: (0, 0)), pl.BlockSpec((2, C, LANES), lambda t: (0, 0, t)), blk,
                  vec, vec],
        out_specs=[blk, vec, vec], out_shape=[jax.ShapeDtypeStruct((2, S, N), bf16), vshape, vshape],
        scratch_shapes=[pltpu.VMEM((2, S, LANES), f32)] + _scan_scratch(n))
    return g, dar, dai, moved


def _glu_fn(ypre, wglu):
    y = jax.nn.gelu(ypre)
    return y * jax.nn.sigmoid(bdot_nn(y, wglu))


def glu_fwd(ypre0, u, d, wglu, g_out, *, ts=512):
    S, W = u.shape
    ts = _tile(S, ts)

    def body(y0_ref, u_ref, d_ref, w_ref, g_ref, ypre_ref, z_ref, zn_ref):
        ypre = y0_ref[...] + d_ref[...] * u_ref[...]
        z = _glu_fn(ypre, w_ref[...])
        ypre_ref[...] = ypre
        z_ref[...] = z
        zn_ref[...] = _rms(z, g_ref[...], W).astype(bf16)

    row = pl.BlockSpec((ts, W), lambda i: (i, 0))
    vec = pl.BlockSpec((1, W), lambda i: (0, 0))
    full = jax.ShapeDtypeStruct((S, W), f32)
    return pl.pallas_call(
        body, name="glu_fwd", grid=(S // ts,),
        in_specs=[row, row, vec, pl.BlockSpec((W, W), lambda i: (0, 0)), vec], out_specs=[row, row, row],
        out_shape=[full, full, jax.ShapeDtypeStruct((S, W), bf16)], compiler_params=_cp(("parallel",)),
    )(ypre0, u, d, wglu, g_out)


def glu_bwd(ypre, u, d, wglu, dz, *, ts=512):
    S, W = u.shape
    ts = _tile(S, ts)

    def body(y_ref, u_ref, d_ref, w_ref, dz_ref, dy_ref, du_ref, dw_ref, dd_ref):
        _, vjp = jax.vjp(_glu_fn, y_ref[...], w_ref[...])
        dy, dw = vjp(dz_ref[...])
        dy_ref[...] = dy
        du_ref[...] = d_ref[...] * dy

        @pl.when(pl.program_id(0) == 0)
        def _():
            dw_ref[...] = jnp.zeros_like(dw_ref)
            dd_ref[...] = jnp.zeros_like(dd_ref)

        dw_ref[...] += dw
        dd_ref[...] += jnp.sum(dy * u_ref[...], axis=0, keepdims=True)

    row = pl.BlockSpec((ts, W), lambda i: (i, 0))
    vec = pl.BlockSpec((1, W), lambda i: (0, 0))
    sq = pl.BlockSpec((W, W), lambda i: (0, 0))
    full = jax.ShapeDtypeStruct((S, W), f32)
    return pl.pallas_call(
        body, name="glu_bwd", grid=(S // ts,), in_specs=[row, row, vec, sq, row], out_specs=[row, row, sq, vec],
        out_shape=[full, full, jax.ShapeDtypeStruct((W, W), f32), jax.ShapeDtypeStruct((1, W), f32)],
        compiler_params=_cp(("arbitrary",)),
    )(ypre, u, d, wglu, dz)


def adamw(w, g, m, v, *, name, tr=512):
    R, C = w.shape
    tr = _row_tile(R, tr)

    def body(w_ref, g_ref, m_ref, v_ref, d_ref, nm_ref, nv_ref):
        gv = g_ref[...]
        nm = ADAM_B1 * m_ref[...] + (1.0 - ADAM_B1) * gv
        nv = ADAM_B2 * v_ref[...] + (1.0 - ADAM_B2) * jnp.square(gv)
        m_hat = nm / (1.0 - ADAM_B1 ** ADAM_STEP)
        v_hat = nv / (1.0 - ADAM_B2 ** ADAM_STEP)
        d_ref[...] = -ADAM_LR * (m_hat / (jnp.sqrt(v_hat) + ADAM_EPS) + ADAM_WD * w_ref[...])
        nm_ref[...] = nm
        nv_ref[...] = nv

    row = pl.BlockSpec((tr, C), lambda i: (i, 0))
    full = jax.ShapeDtypeStruct((R, C), f32)
    return pl.pallas_call(
        body, name=name, grid=(R // tr,), in_specs=[row] * 4, out_specs=[row] * 3, out_shape=[full] * 3,
        compiler_params=_cp(("parallel",)),
    )(w, g, m, v)


def add_half(g4, recv, c, *, name, tr=256):
    _, _, Rh, C = g4.shape
    tr = _row_tile(Rh, tr)

    def body(c_ref, a_ref, b_ref, o_ref):
        o_ref[...] = a_ref[...] + b_ref[...]

    grid_spec = pltpu.PrefetchScalarGridSpec(
        num_scalar_prefetch=1, grid=(N_CHIPS, Rh // tr),
        in_specs=[pl.BlockSpec((None, None, tr, C), lambda k, i, c_ref: (k, c_ref[0], i, 0)),
                  pl.BlockSpec((None, tr, C), lambda k, i, c_ref: (k, i, 0))],
        out_specs=pl.BlockSpec((None, tr, C), lambda k, i, c_ref: (k, i, 0)))
    return pl.pallas_call(body, name=name, grid_spec=grid_spec, out_shape=jax.ShapeDtypeStruct(recv.shape, f32),
                          compiler_params=_cp(("parallel", "parallel")))(c, g4, recv)


def sum_chips(p4, *, name, tr=256):
    _, Rh, C = p4.shape
    tr = _row_tile(Rh, tr)

    def body(a_ref, b_ref, c_ref, d_ref, o_ref):
        o_ref[...] = ((a_ref[...] + b_ref[...]) + c_ref[...]) + d_ref[...]

    spec = lambda k: pl.BlockSpec((None, tr, C), lambda i: (k, i, 0))
    return pl.pallas_call(
        body, name=name, grid=(Rh // tr,), in_specs=[spec(0), spec(1), spec(2), spec(3)],
        out_specs=pl.BlockSpec((tr, C), lambda i: (i, 0)), out_shape=jax.ShapeDtypeStruct((Rh, C), f32),
        compiler_params=_cp(("parallel",)),
    )(p4, p4, p4, p4)


def _place():
    return lax.axis_index("x"), lax.axis_index("y"), lax.axis_index("c")


def _other_chips(x, y):
    return [(1 - x, y), (x, 1 - y), (1 - x, 1 - y)]


def _chip_exchange(ins, outs, sems, scatter, start):
    if not ins:
        return
    send, recv, loc = sems
    x, y, c = _place()
    me = 2 * x + y
    for a in range(len(ins)):
        own = pltpu.make_async_copy(ins[a].at[me] if scatter else ins[a], outs[a].at[me], loc.at[a])
        own.start() if start else own.wait()
        for p, (px, py) in enumerate(_other_chips(x, y)):
            k = 2 * px + py
            cp = pltpu.make_async_remote_copy(
                src_ref=ins[a].at[k] if scatter else ins[a], dst_ref=outs[a].at[me if start else k],
                send_sem=send.at[3 * a + p], recv_sem=recv.at[3 * a + p], device_id=(px, py, c), device_id_type=MESH)
            cp.start() if start else cp.wait()


def _chip_exchange_args(arrs, scatter):
    n = len(arrs)
    shapes = [jax.ShapeDtypeStruct(a.shape if scatter else (N_CHIPS,) + a.shape, a.dtype) for a in arrs]
    sems = [pltpu.SemaphoreType.DMA((3 * n,)), pltpu.SemaphoreType.DMA((3 * n,)), pltpu.SemaphoreType.DMA((n,))]
    return shapes, sems if n else []


def _chip_exchange_call(arrs, scatter, name):
    n = len(arrs)

    def body(*refs):
        ins, outs, sems = refs[:n], refs[n:2 * n], refs[2 * n:]
        _chip_exchange(ins, outs, sems, scatter, True)
        _chip_exchange(ins, outs, sems, scatter, False)

    shapes, sems = _chip_exchange_args(arrs, scatter)
    return pl.pallas_call(
        body, name=name, in_specs=[ANY] * n, out_specs=[ANY] * n, out_shape=shapes, scratch_shapes=sems,
        compiler_params=pltpu.CompilerParams(has_side_effects=True),
    )(*arrs)


def allgather_chips(arrs, *, name):
    return _chip_exchange_call(arrs, False, name)


def sibling_swap(arrs, *, half, name):
    n = len(arrs)

    def body(*refs):
        ins, outs = refs[:n], refs[n:2 * n]
        send, recv = refs[2 * n:]
        x, y, c = _place()
        cps = []
        for a in range(n):
            src = ins[a].at[:, 1 - c] if half else ins[a]
            cp = pltpu.make_async_remote_copy(src_ref=src, dst_ref=outs[a], send_sem=send.at[a], recv_sem=recv.at[a],
                                              device_id=(x, y, 1 - c), device_id_type=MESH)
            cp.start()
            cps.append(cp)
        for cp in cps:
            cp.wait()

    def oshape(a):
        return jax.ShapeDtypeStruct((a.shape[0],) + a.shape[2:] if half else a.shape, a.dtype)

    return pl.pallas_call(
        body, name=name, in_specs=[ANY] * n, out_specs=[ANY] * n, out_shape=[oshape(a) for a in arrs],
        scratch_shapes=[pltpu.SemaphoreType.DMA((n,)), pltpu.SemaphoreType.DMA((n,))],
        compiler_params=pltpu.CompilerParams(has_side_effects=True),
    )(*arrs)


def chip_scatter(arrs, *, name):
    return _chip_exchange_call(arrs, True, name)


def _pad_cols(w):
    K = w.shape[0]
    w = w.reshape(K, -1, SB_HEAD_DIM)
    return jnp.pad(w, ((0, 0), (0, 0), (0, LANES - SB_HEAD_DIM))).reshape(K, -1)


def _unpad_cols(w):
    K = w.shape[0]
    return w.reshape(K, -1, LANES)[:, :, :SB_HEAD_DIM].reshape(K, -1)


def _pad_rows(w):
    N = w.shape[1]
    w = w.reshape(-1, SB_HEAD_DIM, N)
    return jnp.pad(w, ((0, 0), (0, LANES - SB_HEAD_DIM), (0, 0))).reshape(-1, N)


def _unpad_rows(w):
    N = w.shape[1]
    return w.reshape(-1, LANES, N)[:, :SB_HEAD_DIM, :].reshape(-1, N)


_PACK_ROWS = N_CHIPS * 2 * SUBLANES


def _pack(arrs):
    flat = jnp.concatenate([a.reshape(-1) for a in arrs])
    rows = -(-flat.shape[0] // LANES)
    rows = -(-rows // _PACK_ROWS) * _PACK_ROWS
    return jnp.pad(flat, (0, rows * LANES - flat.shape[0])).reshape(rows, LANES)


def _unpack(buf, shapes):
    flat = buf.reshape(-1)
    out, pos = [], 0
    for shp in shapes:
        size = 1
        for d in shp:
            size *= d
        out.append(flat[pos:pos + size].reshape(shp))
        pos += size
    return out


BIG = ("w_in", "ssm_w_glu", "w_out", "xa_w_q", "xa_w_kv", "xa_w_o", "w_up", "w_down")
SMALL = ("g_mix", "ssm_a_re", "ssm_a_im", "ssm_log_dt", "ssm_b_re", "ssm_b_im", "ssm_c_re", "ssm_c_im", "ssm_d",
         "sb_g_q", "sb_g_k", "g_out_ssm", "g_out_sb", "g_xa", "g_mem", "xa_g_q", "xa_g_k", "g_mlp")
WEIGHTS = ("g_mix", "w_in", "ssm_a_re", "ssm_a_im", "ssm_log_dt", "ssm_b_re", "ssm_b_im", "ssm_c_re", "ssm_c_im",
           "ssm_d", "ssm_w_glu", "sb_g_q", "sb_g_k", "g_out_ssm", "g_out_sb", "w_out", "g_xa", "g_mem", "xa_w_q",
           "xa_w_kv", "xa_g_q", "xa_g_k", "xa_w_o", "g_mlp", "w_up", "w_down")


def kernel(x, mem, g_mix, w_in, ssm_a_re, ssm_a_im, ssm_log_dt, ssm_b_re, ssm_b_im, ssm_c_re, ssm_c_im, ssm_d, ssm_w_glu, sb_g_q, sb_g_k, g_out_ssm, g_out_sb, w_out, g_xa, g_mem, xa_w_q, xa_w_kv, xa_g_q, xa_g_k, xa_w_o, g_mlp, w_up, w_down, loss_target, m_g_mix, m_w_in, m_ssm_a_re, m_ssm_a_im, m_ssm_log_dt, m_ssm_b_re, m_ssm_b_im, m_ssm_c_re, m_ssm_c_im, m_ssm_d, m_ssm_w_glu, m_sb_g_q, m_sb_g_k, m_g_out_ssm, m_g_out_sb, m_w_out, m_g_xa, m_g_mem, m_xa_w_q, m_xa_w_kv, m_xa_g_q, m_xa_g_k, m_xa_w_o, m_g_mlp, m_w_up, m_w_down, v_g_mix, v_w_in, v_ssm_a_re, v_ssm_a_im, v_ssm_log_dt, v_ssm_b_re, v_ssm_b_im, v_ssm_c_re, v_ssm_c_im, v_ssm_d, v_ssm_w_glu, v_sb_g_q, v_sb_g_k, v_g_out_ssm, v_g_out_sb, v_w_out, v_g_xa, v_g_mem, v_xa_w_q, v_xa_w_kv, v_xa_g_q, v_xa_g_k, v_xa_w_o, v_g_mlp, v_w_up, v_w_down):
    env = dict(locals())
    W = {n: env[n] for n in WEIGHTS}
    M1 = {n: env["m_" + n] for n in WEIGHTS}
    V2 = {n: env["v_" + n] for n in WEIGHTS}
    xs, mems, tgt = x[0], mem[0], loss_target[0]
    S, D = xs.shape
    G, P, C = SSM_GROUPS, SSM_STATE, SSM_GROUP
    GP = G * P
    SBW = SB_HEADS * SB_HEAD_DIM
    c_idx = lax.axis_index("c")

    big = dict(tn=1024, tk=1024)
    wide = dict(tm=1024, tn=1024, tk=2048)
    h0, (g_in,) = rms_norm(xs, g_mix, D, name="norm_x", exchange=([w_in[0].astype(bf16)], False))
    Wu = g_in[0]
    Wqkv = jnp.concatenate([_pad_cols(g_in[1]), _pad_cols(g_in[2]), _pad_cols(g_in[3])], axis=1)
    gq_pad, gk_pad = _pad_cols(sb_g_q), _pad_cols(sb_g_k)
    gosb_pad = _pad_cols(g_out_sb)
    a_re, a_im = ssm_a_re.reshape(1, GP), ssm_a_im.reshape(1, GP)
    bT_re = ssm_b_re[0].transpose(2, 0, 1).reshape(C, GP)
    bT_im = ssm_b_im[0].transpose(2, 0, 1).reshape(C, GP)
    cT_re = ssm_c_re[0].transpose(1, 0, 2).reshape(C, GP)
    cT_im = ssm_c_im[0].transpose(1, 0, 2).reshape(C, GP)
    s5_in = (a_re, a_im, ssm_log_dt, bT_re, bT_im, cT_re, cT_im)

    u = mm(h0, Wu, mode="nn", name="proj_u", tk=1024)
    shard = {n: W[n][0].astype(bf16) for n in BIG[1:]}
    qkv, (g_glu, g_out, g_xq, g_xkv, g_xo) = mm(
        h0, Wqkv, mode="nn", name="proj_qkv", tm=1024,
        exchange=([shard[n] for n in ("ssm_w_glu", "w_out", "xa_w_q", "xa_w_kv", "xa_w_o")], False), **big)
    qn, kn, vb = qkv_prep(qkv, gq_pad, gk_pad)
    o, (g_down,) = sb_fwd(qn, kn, vb, ([shard["w_down"]], False))
    Wglu = g_glu.reshape(-1, g_glu.shape[-1])
    Wout = g_out.reshape(-1, g_out.shape[-1])
    Wo_ssm, Wo_sb = Wout[:SBW], _pad_rows(Wout[SBW:])
    Wxq = g_xq.reshape(-1, g_xq.shape[-1])
    Wxkv = g_xkv.reshape(-1, g_xkv.shape[-1])
    Wxo = g_xo.transpose(1, 0, 2).reshape(g_xo.shape[1], -1)
    abr, abi, Bm, Cm = s5_prep(*s5_in)
    n_pos = S // SCAN_CHUNKS
    u_il = row_shuffle(u, n_pos, SCAN_CHUNKS, name="u_interleave", out_dtype=bf16)
    st, (g_up,) = scan_fwd(u_il, Bm, abr, abi, ([shard["w_up"]], False))
    ypre0_il = mm(st, Cm, mode="nt", name="s5_y", a_shards=2, b_shards=2, tm=1024, tk=2048)
    Wup = g_up.transpose(1, 0, 2).reshape(g_up.shape[1], -1)
    Wdown = g_down.reshape(-1, g_down.shape[-1])
    ypre0 = row_shuffle(ypre0_il, SCAN_CHUNKS, n_pos, name="y_token_order")
    ypre, z, zn = glu_fwd(ypre0, u, ssm_d, Wglu, g_out_ssm)
    on = rms_norm(o, gosb_pad, SBW, name="norm_o")
    x1a = mm(zn, Wo_ssm, mode="nn", name="out_ssm", epi="add", aux=xs, tn=1024)
    x1 = mm(on, Wo_sb, mode="nn", name="out_sb", epi="add", aux=x1a, **big)
    h1 = rms_norm(x1, g_xa, D, name="norm_x1")
    qx = mm(h1, Wxq, mode="nn", name="xa_q", tk=1024)
    memn = rms_norm(mems, g_mem, D, name="norm_mem")
    kv = mm(memn, Wxkv, mode="nn", name="xa_kv", **big)
    ox = xa_fwd(qx, kv, xa_g_q, xa_g_k)
    x2 = mm(ox, Wxo, mode="nn", name="xa_o", epi="add", aux=x1, tn=1024)
    h2 = rms_norm(x2, g_mlp, D, name="norm_x2")
    act = mm(h2, Wup, mode="nn", name="mlp_up", out_dtype=bf16, tm=1024, tn=2048, tk=1024)
    dx3, loss_part, dx3b = mm(act, Wdown, mode="nn", name="mlp_down", pro="relu2", epi="add", aux=x2,
                              loss_target=tgt, tm=1024, **big)
    loss = lax.psum(loss_part[0, 0], ("x", "y", "c"))

    dact = mm(dx3b, Wdown, mode="nt", name="d_act", epi="mul2relu", aux=act, out_dtype=bf16, tm=1024, tn=2048,
              tk=1024)
    dWdown = mm(act, dx3b, mode="tn", name="dw_down", pro="relu2", **wide)
    dWup = mm(h2, dact, mode="tn", name="dw_up", out_shards=N_CHIPS, **wide)
    dx2, dg_mlp, dx2b = mm(dact, Wup, mode="nt", name="d_h2", norm_bwd=(x2, g_mlp, dx3, D), tm=1024, **big)
    dox = mm(dx2b, Wxo, mode="nt", name="d_ox", out_dtype=bf16, tk=1024)
    dWxo = mm(ox, dx2b, mode="tn", name="dw_xo", out_shards=N_CHIPS, tk=1024)
    dqx, dkv, dg_xq, dg_xk = xa_bwd(qx, kv, xa_g_q, xa_g_k, dox)
    dWxq = mm(h1, dqx, mode="tn", name="dw_xq", tm=1024, tk=1024)
    dx1, dg_xa, dx1b = mm(dqx, Wxq, mode="nt", name="d_h1", norm_bwd=(x1, g_xa, dx2, D), tn=1024)
    dWxkv = mm(memn, dkv, mode="tn", name="dw_xkv", tm=1024, tn=1024)
    dmemn = mm(dkv, Wxkv, mode="nt", name="d_memn", **big)
    _, dg_mem = rms_bwd(mems, g_mem, dmemn, None, D, name="rms_bwd_mem")
    dyn_ssm = mm(dx1b, Wo_ssm, mode="nt", name="d_yn_ssm", tk=1024)
    dyn_sb = mm(dx1b, Wo_sb, mode="nt", name="d_yn_sb", **big)
    dWo_ssm = mm(zn, dx1b, mode="tn", name="dw_out_ssm", **big)
    dWo_sb = mm(on, dx1b, mode="tn", name="dw_out_sb", tm=1024, **big)
    dz, dg_os = rms_bwd(z, g_out_ssm, dyn_ssm, None, SBW, name="rms_bwd_ssm")
    do, dg_osb = rms_bwd(o, gosb_pad, dyn_sb, None, SBW, name="rms_bwd_sb")
    c_arr = c_idx.astype(jnp.int32).reshape(1)

    def sibling_sums(grads, names, tag):
        g4 = [g.reshape(N_CHIPS, 2, g.shape[1] // 2, g.shape[2]) for g in grads]
        from_sib = sibling_swap(g4, half=True, name="grad_to_sibling_" + tag)
        return [add_half(a, b, c_arr, name="add_sibling_" + n) for a, b, n in zip(g4, from_sib, names)]

    early = ("xa_w_q", "xa_w_kv", "xa_w_o", "w_up", "w_down")
    early_g = [dWxq.reshape(N_CHIPS, -1, dWxq.shape[1]), dWxkv.reshape(N_CHIPS, -1, dWxkv.shape[1]), dWxo, dWup,
               dWdown.reshape(N_CHIPS, -1, D)]
    pair = sibling_sums(early_g, early, "early")
    dqn, dkn, dv, parts_mlp = sb_bwd(qn, kn, vb, do, (pair[3:], True))
    dqkv, dg_q, dg_k, parts_xa = qkv_bwd(qkv, gq_pad, gk_pad, dqn, dkn, dv, (pair[:3], True))
    dypre, du_skip, dWglu, dd = glu_bwd(ypre, u, ssm_d, Wglu, dz)
    mid = ("ssm_w_glu", "w_out")
    mid_g = [dWglu.reshape(N_CHIPS, -1, dWglu.shape[1]),
             jnp.concatenate([dWo_ssm, _unpad_rows(dWo_sb)]).reshape(N_CHIPS, -1, D)]
    dypre_il = row_shuffle(dypre, n_pos, SCAN_CHUNKS, name="dy_interleave", out_dtype=bf16)
    dCm = mm(dypre_il, st, mode="tn", name="d_cmat", b_shards=2, out_shards=2, **wide)
    gst, dabr, dabi, parts_mid = scan_bwd(dypre_il, Cm, st, abr, abi, (sibling_sums(mid_g, mid, "mid"), True))
    dBm = mm(u_il, gst, mode="tn", name="d_bmat", b_shards=2, out_shards=2, **wide)
    du_il = mm(gst, Bm, mode="nt", name="d_u", a_shards=2, b_shards=2, tm=1024, tk=2048)
    mine = {n: sum_chips(p, name="sum_chips_" + n)
            for n, p in zip(early + mid, [*parts_xa, *parts_mlp, *parts_mid])}
    du = row_shuffle(du_il, SCAN_CHUNKS, n_pos, name="du_token_order", add=du_skip, out_dtype=bf16)
    s5_g = s5_prep_bwd(*s5_in, dabr, dabi, dBm, dCm)
    dWu = mm(h0, du, mode="tn", name="dw_u", tm=1024, tk=1024)
    dWqkv = mm(h0, dqkv, mode="tn", name="dw_qkv", **wide)
    HW = SB_HEADS * LANES
    w_in_g = jnp.stack([dWu, _unpad_cols(dWqkv[:, :HW]), _unpad_cols(dWqkv[:, HW:2 * HW]),
                        _unpad_cols(dWqkv[:, 2 * HW:])])
    dh0a = mm(du, Wu, mode="nt", name="d_h0_u", tn=1024)
    (dx, dg_mix, _), parts_in = mm(dqkv, Wqkv, mode="nt", name="d_h0_qkv", epi="add", aux=dh0a,
                                   norm_bwd=(xs, g_mix, dx1, D), tm=1024,
                                   exchange=(sibling_sums([w_in_g], ("w_in",), "w_in"), True), **big)
    mine["w_in"] = sum_chips(parts_in[0], name="sum_chips_w_in")

    late = ("small",)
    late_g = []
    da_re, da_im, dldt, dbT_re, dbT_im, dcT_re, dcT_im = s5_g
    small_g = {
        "g_mix": dg_mix, "ssm_a_re": da_re, "ssm_a_im": da_im, "ssm_log_dt": dldt,
        "ssm_b_re": dbT_re.reshape(C, G, P).transpose(1, 2, 0), "ssm_b_im": dbT_im.reshape(C, G, P).transpose(1, 2, 0),
        "ssm_c_re": dcT_re.reshape(C, G, P).transpose(1, 0, 2), "ssm_c_im": dcT_im.reshape(C, G, P).transpose(1, 0, 2),
        "ssm_d": dd, "sb_g_q": dg_q[:, :SB_HEAD_DIM], "sb_g_k": dg_k[:, :SB_HEAD_DIM], "g_out_ssm": dg_os,
        "g_out_sb": _unpad_cols(dg_osb), "g_xa": dg_xa, "g_mem": dg_mem, "xa_g_q": dg_xq, "xa_g_k": dg_xk,
        "g_mlp": dg_mlp,
    }
    late_g.append(_pack([small_g[n] for n in SMALL]).reshape(N_CHIPS, -1, LANES))

    parts_late = chip_scatter(sibling_sums(late_g, late, "late"), name="grad_to_chips_late")
    mine.update({n: sum_chips(p, name="sum_chips_" + n) for n, p in zip(late, parts_late)})
    mine = [mine[n] for n in list(BIG) + ["small"]]
    other = sibling_swap(mine, half=False, name="grad_half_to_sibling")
    shard = [jnp.where(c_idx == 0, jnp.concatenate([a, b]), jnp.concatenate([b, a])) for a, b in zip(mine, other)]
    small_all = allgather_chips([shard[-1]], name="gather_small")[0]
    small_red = small_all.reshape(-1, LANES)

    out = {}
    for n, gs in zip(BIG, shard[:-1]):
        shp = W[n].shape
        w2, m2, v2 = (t.reshape(gs.shape) for t in (W[n], M1[n], V2[n]))
        d, nm, nv = adamw(w2, gs, m2, v2, name="adamw_" + n)
        out[n] = tuple(t.reshape(shp) for t in (gs, d, nm, nv))
    shapes = [W[n].shape for n in SMALL]
    d, nm, nv = adamw(_pack([W[n] for n in SMALL]), small_red, _pack([M1[n] for n in SMALL]),
                      _pack([V2[n] for n in SMALL]), name="adamw_small")
    for n, gs, dd_, mm_, vv_ in zip(SMALL, _unpack(small_red, shapes), _unpack(d, shapes), _unpack(nm, shapes),
                                    _unpack(nv, shapes)):
        out[n] = (gs, dd_, mm_, vv_)
    res = [loss, dx[None]]
    for kind in range(4):
        res += [out[n][kind] for n in WEIGHTS]
    return tuple(res)
```
